```python
import math
import jax, jax.numpy as jnp
from jax import lax
import numpy as np

D_MODEL = 1024
BATCH = 8
SEQ = 2048
DEPTH = 1
DEC_BATCH = 128
DEC_SEQ = 1
PAST_LEN = 16384
PAGE_SIZE = 128

MIX_WIDTH = D_MODEL
M_HEADS = 4
M_HEAD_DIM = MIX_WIDTH // 2 // M_HEADS
M_WIDTH = M_HEADS * M_HEAD_DIM
R_HEADS = 4
R_HEAD_DIM = MIX_WIDTH // 2 // R_HEADS
R_WIDTH = R_HEADS * R_HEAD_DIM
CONV_W = 4
CHUNK = 128
N_MEM = 256
X_HEADS = 4
X_HEAD_DIM = D_MODEL // X_HEADS
D_FF = -(-8 * D_MODEL // (3 * 256)) * 256
ROPE_THETA = 10000.0
EPS = 1e-6
IN_COLS = 4 * M_WIDTH + 2 * M_HEADS + 4 * R_WIDTH

kernel_name = "hymba_mlstm_retention_decoder_step"


def rmsnorm(x, g):
    xf = x.astype(jnp.float32)
    xf = xf * lax.rsqrt(jnp.mean(xf * xf, axis=-1, keepdims=True) + EPS)
    return (xf * g.astype(jnp.float32)).astype(x.dtype)


def head_rmsnorm(h, g):
    B, T, H, D = h.shape
    h = h * lax.rsqrt(jnp.mean(h * h, axis=-1, keepdims=True) + EPS)
    return h.reshape(B, T, H * D) * g.astype(jnp.float32)


def short_conv(u, buf, w, b):
    T = u.shape[1]
    full = jnp.concatenate([buf.astype(u.dtype), u], axis=1)
    out = b + sum(full[:, j:j + T] * w[j] for j in range(CONV_W))
    return jax.nn.silu(out), full[:, T:]


def rope(x, pos):
    half = x.shape[-1] // 2
    inv = ROPE_THETA ** (-jnp.arange(half, dtype=jnp.float32) / half)
    ang = pos[:, None] * inv[None, :]
    cos = jnp.cos(ang)[None, :, None, :]
    sin = jnp.sin(ang)[None, :, None, :]
    x1, x2 = x[..., :half], x[..., half:]
    return jnp.concatenate([x1 * cos - x2 * sin, x2 * cos + x1 * sin], axis=-1)


def chunk_len(T):
    return CHUNK if T % CHUNK == 0 else T


def to_chunks(a, L):
    B, T, H, D = a.shape
    return a.reshape(B, T // L, L, H, D).transpose(1, 0, 3, 2, 4)


def gate_chunks(a, L):
    B, T, H = a.shape
    return a.reshape(B, T // L, L, H).transpose(1, 0, 3, 2)


def from_chunks(o):
    NC, B, H, L, D = o.shape
    return o.transpose(1, 0, 3, 2, 4).reshape(B, NC * L, H, D)


def mlstm_scan(q, k, v, ig, lf, C0, n0, m0):
    T = q.shape[1]
    L = chunk_len(T)
    tril = jnp.tril(jnp.ones((L, L), dtype=bool))

    def step(carry, xs):
        C, n, m = carry
        qc, kc, vc, ic, fc = xs
        b = jnp.cumsum(fc, axis=-1)
        dmat = b[..., :, None] - b[..., None, :] + ic[..., None, :]
        dmat = jnp.where(tril, dmat, -jnp.inf)
        inter = b + m[..., None]
        m_t = jnp.maximum(inter, jnp.max(dmat, axis=-1))
        wts = jnp.exp(dmat - m_t[..., None]) * jnp.einsum('bhtd,bhsd->bhts', qc, kc)
        w_in = jnp.exp(inter - m_t)
        num = jnp.einsum('bhts,bhsd->bhtd', wts, vc) + w_in[..., None] * jnp.einsum('bhvk,bhtk->bhtv', C, qc)
        den = jnp.sum(wts, axis=-1) + w_in * jnp.einsum('bhk,bhtk->bht', n, qc)
        h = num / jnp.maximum(jnp.abs(den), jnp.exp(-m_t))[..., None]
        bL = b[..., -1]
        g = bL[..., None] - b + ic
        m_new = jnp.maximum(bL + m, jnp.max(g, axis=-1))
        ws = jnp.exp(g - m_new[..., None])
        carry_scale = jnp.exp(bL + m - m_new)
        C_new = carry_scale[..., None, None] * C + jnp.einsum('bhsv,bhsk->bhvk', vc * ws[..., None], kc)
        n_new = carry_scale[..., None] * n + jnp.einsum('bhs,bhsk->bhk', ws, kc)
        return (C_new, n_new, m_new), h

    (C, n, m), h = lax.scan(step, (C0, n0, m0),
                            (to_chunks(q, L), to_chunks(k, L), to_chunks(v, L),
                             gate_chunks(ig, L), gate_chunks(lf, L)))
    return from_chunks(h), C, n, m


def retention_scan(q, k, v, S0):
    T = q.shape[1]
    L = chunk_len(T)
    lg = jnp.log1p(-jnp.exp2(-5.0 - jnp.arange(R_HEADS, dtype=jnp.float32)))
    t = jnp.arange(L, dtype=jnp.float32)
    diff = t[:, None] - t[None, :]
    decay = jnp.where(diff >= 0, jnp.exp(lg[:, None, None] * jnp.maximum(diff, 0.0)), 0.0)
    q_dec = jnp.exp(lg[:, None] * (t + 1.0))
    k_dec = jnp.exp(lg[:, None] * (L - 1.0 - t))
    chunk_dec = jnp.exp(lg * L)

    def step(S, xs):
        qc, kc, vc = xs
        o = (jnp.einsum('bhts,bhsv->bhtv', jnp.einsum('bhtd,bhsd->bhts', qc, kc) * decay, vc)
             + q_dec[..., None] * jnp.einsum('bhtk,bhkv->bhtv', qc, S))
        S = chunk_dec[:, None, None] * S + jnp.einsum('bhsk,bhsv->bhkv', kc * k_dec[..., None], vc)
        return S, o

    S, o = lax.scan(step, S0, (to_chunks(q, L), to_chunks(k, L), to_chunks(v, L)))
    return from_chunks(o), S


def mem_kv(mem, g_mem, w_ck, w_cv):
    B = mem.shape[0]
    mn = rmsnorm(mem, g_mem)
    k = (mn @ w_ck).reshape(B, N_MEM, X_HEADS, X_HEAD_DIM)
    v = (mn @ w_cv).reshape(B, N_MEM, X_HEADS, X_HEAD_DIM)
    return k, v


def cross_attn(x, mk, mv, w_cq, w_co):
    B, T, _ = x.shape
    q = (x @ w_cq).reshape(B, T, X_HEADS, X_HEAD_DIM).astype(jnp.float32)
    s = jnp.einsum('bthd,bmhd->bhtm', q, mk.astype(jnp.float32)) * (X_HEAD_DIM ** -0.5)
    p = jax.nn.softmax(s, axis=-1)
    o = jnp.einsum('bhtm,bmhd->bthd', p, mv.astype(jnp.float32)).reshape(B, T, D_MODEL)
    return o.astype(x.dtype) @ w_co


def layer(x, pos, conv_buf, C0, n0, m0, S0, mk_mem, mv_mem,
          w_in, b_gate, w_conv, b_conv, g_mix, g_mhead, g_rhead, w_out,
          g_xattn, w_cq, w_co, g_ffn, w_gate, w_up, w_down):
    f32 = jnp.float32
    B, T, _ = x.shape
    M, R, G0 = M_WIDTH, R_WIDTH, 4 * M_WIDTH
    u = rmsnorm(x, g_mix) @ w_in
    qk, conv_new = short_conv(u[..., :2 * M], conv_buf, w_conv, b_conv)
    mq = qk[..., :M].reshape(B, T, M_HEADS, M_HEAD_DIM).astype(f32)
    mk = qk[..., M:].reshape(B, T, M_HEADS, M_HEAD_DIM).astype(f32) * (M_HEAD_DIM ** -0.5)
    mv = u[..., 2 * M:3 * M].reshape(B, T, M_HEADS, M_HEAD_DIM).astype(f32)
    mo = u[..., 3 * M:4 * M].astype(f32)
    gates = u[..., G0:G0 + 2 * M_HEADS].astype(f32) + b_gate.astype(f32)
    ig = gates[..., :M_HEADS]
    lf = jax.nn.log_sigmoid(gates[..., M_HEADS:])
    hm, C, n, m = mlstm_scan(mq, mk, mv, ig, lf, C0.astype(f32), n0.astype(f32), m0.astype(f32))
    hm = head_rmsnorm(hm, g_mhead) * jax.nn.sigmoid(mo)
    r = u[..., G0 + 2 * M_HEADS:]
    rq = rope(r[..., :R].reshape(B, T, R_HEADS, R_HEAD_DIM).astype(f32), pos)
    rk = rope(r[..., R:2 * R].reshape(B, T, R_HEADS, R_HEAD_DIM).astype(f32), pos) * (R_HEAD_DIM ** -0.5)
    rv = r[..., 2 * R:3 * R].reshape(B, T, R_HEADS, R_HEAD_DIM).astype(f32)
    rg = r[..., 3 * R:].astype(f32)
    hr, S = retention_scan(rq, rk, rv, S0.astype(f32))
    hr = head_rmsnorm(hr, g_rhead) * jax.nn.silu(rg)
    x = x + jnp.concatenate([hm, hr], axis=-1).astype(x.dtype) @ w_out
    x = x + cross_attn(rmsnorm(x, g_xattn), mk_mem, mv_mem, w_cq, w_co)
    hf = rmsnorm(x, g_ffn)
    x = x + (jax.nn.silu(hf @ w_gate) * (hf @ w_up)) @ w_down
    return x, conv_new, C, n, m, S


def setup_inputs(seed: int = 0) -> dict:
    key = jax.random.key(seed)
    ks = jax.random.split(key, 32)
    f32 = jnp.float32
    nrm = lambda k, shape, s: jax.random.normal(k, shape, f32) * s
    gain = lambda k, shape: 1.0 + 0.01 * jax.random.normal(k, shape, f32)
    b_i = 0.01 * jax.random.normal(ks[10], (DEPTH, M_HEADS), f32)
    b_f = jnp.linspace(3.0, 6.0, M_HEADS, dtype=f32)[None, :] + 0.01 * jax.random.normal(ks[11], (DEPTH, M_HEADS), f32)
    return {
        "x_prompt": nrm(ks[0], (BATCH, SEQ, D_MODEL), 1.0),
        "x_sample": nrm(ks[1], (DEC_BATCH, DEC_SEQ, D_MODEL), 1.0),
        "cache_mem_k": nrm(ks[2], (DEPTH, DEC_BATCH, N_MEM, X_HEADS, X_HEAD_DIM), 1.0),
        "cache_mem_v": nrm(ks[3], (DEPTH, DEC_BATCH, N_MEM, X_HEADS, X_HEAD_DIM), 1.0),
        "state_mlstm_conv": nrm(ks[4], (DEPTH, DEC_BATCH, CONV_W - 1, 2 * M_WIDTH), 1.0),
        "state_mlstm_C": nrm(ks[5], (DEPTH, DEC_BATCH, M_HEADS, M_HEAD_DIM, M_HEAD_DIM), 0.1),
        "state_mlstm_n": nrm(ks[6], (DEPTH, DEC_BATCH, M_HEADS, M_HEAD_DIM), 0.1),
        "state_mlstm_m": nrm(ks[7], (DEPTH, DEC_BATCH, M_HEADS), 1.0),
        "state_ret_S": nrm(ks[8], (DEPTH, DEC_BATCH, R_HEADS, R_HEAD_DIM, R_HEAD_DIM), 0.5),
        "mem_prompt": nrm(ks[9], (BATCH, N_MEM, D_MODEL), 1.0),
        "w_in": nrm(ks[12], (DEPTH, D_MODEL, IN_COLS), D_MODEL ** -0.5),
        "b_gate": jnp.concatenate([b_i, b_f], axis=-1),
        "w_conv": nrm(ks[13], (DEPTH, CONV_W, 2 * M_WIDTH), CONV_W ** -0.5),
        "b_conv": nrm(ks[14], (DEPTH, 2 * M_WIDTH), 0.01),
        "g_mix": gain(ks[15], (DEPTH, D_MODEL)),
        "g_mhead": gain(ks[16], (DEPTH, M_WIDTH)),
        "g_rhead": gain(ks[17], (DEPTH, R_WIDTH)),
        "w_out": nrm(ks[18], (DEPTH, MIX_WIDTH, D_MODEL), MIX_WIDTH ** -0.5),
        "g_xattn": gain(ks[19], (DEPTH, D_MODEL)),
        "g_mem": gain(ks[20], (DEPTH, D_MODEL)),
        "w_ck": nrm(ks[21], (DEPTH, D_MODEL, D_MODEL), D_MODEL ** -0.5),
        "w_cv": nrm(ks[22], (DEPTH, D_MODEL, D_MODEL), D_MODEL ** -0.5),
        "w_cq": nrm(ks[23], (DEPTH, D_MODEL, D_MODEL), D_MODEL ** -0.5),
        "w_co": nrm(ks[24], (DEPTH, D_MODEL, D_MODEL), D_MODEL ** -0.5),
        "g_ffn": gain(ks[25], (DEPTH, D_MODEL)),
        "w_gate": nrm(ks[26], (DEPTH, D_MODEL, D_FF), D_MODEL ** -0.5),
        "w_up": nrm(ks[27], (DEPTH, D_MODEL, D_FF), D_MODEL ** -0.5),
        "w_down": nrm(ks[28], (DEPTH, D_FF, D_MODEL), D_FF ** -0.5),
        "g_final": gain(ks[29], (D_MODEL,)),
    }


def reference(x_prompt, x_sample, cache_mem_k, cache_mem_v, state_mlstm_conv, state_mlstm_C,
              state_mlstm_n, state_mlstm_m, state_ret_S, mem_prompt,
              w_in, b_gate, w_conv, b_conv, g_mix, g_mhead, g_rhead, w_out, g_xattn, g_mem,
              w_ck, w_cv, w_cq, w_co, g_ffn, w_gate, w_up, w_down, g_final):
    f32 = jnp.float32
    Bp, Tp, _ = x_prompt.shape
    Ts = x_sample.shape[1]
    pos_p = jnp.arange(Tp, dtype=f32)
    pos_s = PAST_LEN + jnp.arange(Ts, dtype=f32)
    h_p, h_s = x_prompt, x_sample
    mkp_l, mvp_l, convp_l, Cp_l, np_l, mp_l, Sp_l = [], [], [], [], [], [], []
    convs_l, Cs_l, ns_l, ms_l, Ss_l = [], [], [], [], []
    for l in range(DEPTH):
        lw = (w_in[l], b_gate[l], w_conv[l], b_conv[l], g_mix[l], g_mhead[l], g_rhead[l], w_out[l],
              g_xattn[l], w_cq[l], w_co[l], g_ffn[l], w_gate[l], w_up[l], w_down[l])
        mk_p, mv_p = mem_kv(mem_prompt, g_mem[l], w_ck[l], w_cv[l])
        h_p, conv_p, C_p, n_p, m_p, S_p = layer(
            h_p, pos_p,
            jnp.zeros((Bp, CONV_W - 1, 2 * M_WIDTH), h_p.dtype),
            jnp.zeros((Bp, M_HEADS, M_HEAD_DIM, M_HEAD_DIM), f32),
            jnp.zeros((Bp, M_HEADS, M_HEAD_DIM), f32),
            jnp.zeros((Bp, M_HEADS), f32),
            jnp.zeros((Bp, R_HEADS, R_HEAD_DIM, R_HEAD_DIM), f32),
            mk_p, mv_p, *lw)
        h_s, conv_s, C_s, n_s, m_s, S_s = layer(
            h_s, pos_s, state_mlstm_conv[l], state_mlstm_C[l], state_mlstm_n[l], state_mlstm_m[l],
            state_ret_S[l], cache_mem_k[l], cache_mem_v[l], *lw)
        mkp_l.append(mk_p); mvp_l.append(mv_p); convp_l.append(conv_p)
        Cp_l.append(C_p); np_l.append(n_p); mp_l.append(m_p); Sp_l.append(S_p)
        convs_l.append(conv_s); Cs_l.append(C_s); ns_l.append(n_s); ms_l.append(m_s); Ss_l.append(S_s)
    y_prompt = rmsnorm(h_p, g_final)
    y_sample = rmsnorm(h_s, g_final)
    return (y_prompt, y_sample,
            jnp.stack(mkp_l), jnp.stack(mvp_l), jnp.stack(convp_l), jnp.stack(Cp_l),
            jnp.stack(np_l), jnp.stack(mp_l), jnp.stack(Sp_l),
            jnp.stack(convs_l), jnp.stack(Cs_l), jnp.stack(ns_l), jnp.stack(ms_l), jnp.stack(Ss_l))
```

```python
import functools

import jax
import jax.numpy as jnp
from jax import lax
from jax.experimental import pallas as pl
from jax.experimental.pallas import tpu as pltpu

F32 = jnp.float32
BF16 = jnp.bfloat16

D_MODEL = 1024
HEADS = 4
HEAD_DIM = 128
MIX = HEADS * HEAD_DIM
CONV_W = 4
CHUNK = 128
N_MEM = 256
X_HEADS = 4
X_HEAD_DIM = 256
D_FF = 2816
ROPE_THETA = 10000.0
EPS = 1e-6
PAST_LEN = 16384

COL_QK, COL_V, COL_O = 0, 1024, 1536
COL_RQ, COL_RK, COL_RV, COL_RG = 2048, 2560, 3072, 3584
COL_G = 4096
IN_PAD = 4224
IN_CHUNKS = ((0, 1024), (1024, 2048), (2048, 3072), (3072, 4096), (4096, IN_PAD))
UOFF = 8

V7X_VMEM_LIMIT = 56 * 1024 * 1024

NT_DIMS = (((1,), (1,)), ((), ()))
TN_DIMS = (((0,), (0,)), ((), ()))


def _dot(a, b):
    return jnp.dot(a.astype(BF16), b.astype(BF16), preferred_element_type=F32)


def _dot_nt(a, b):
    return lax.dot_general(a.astype(BF16), b.astype(BF16), NT_DIMS, preferred_element_type=F32)


def _dot_tn(a, b):
    return lax.dot_general(a.astype(BF16), b.astype(BF16), TN_DIMS, preferred_element_type=F32)


def _rms(x, g):
    return x * lax.rsqrt(jnp.mean(x * x, axis=-1, keepdims=True) + EPS) * g


def _head_norm(h):
    return h * lax.rsqrt(jnp.mean(h * h, axis=-1, keepdims=True) + EPS)


def _silu(x):
    return x * jax.nn.sigmoid(x)


def _log_sigmoid(x):
    return jnp.minimum(x, 0.0) - jnp.log1p(jnp.exp(-jnp.abs(x)))


def _rope(x, cos_full, sin_signed):
    return x * cos_full + pltpu.roll(x, HEAD_DIM // 2, 1) * sin_signed


def _const_spec(shape):
    zeros = (0,) * len(shape)
    return pl.BlockSpec(shape, lambda *_: zeros, pipeline_mode=pl.Buffered(1))


def _params(*sem):
    return pltpu.CompilerParams(dimension_semantics=sem, vmem_limit_bytes=V7X_VMEM_LIMIT)


def _memkv_kernel(mem_ref, g_ref, wk_ref, wv_ref, k_ref, v_ref):
    mn = _rms(mem_ref[...], g_ref[...]).astype(BF16)
    k_ref[...] = jnp.dot(mn, wk_ref[...], preferred_element_type=F32)
    v_ref[...] = jnp.dot(mn, wv_ref[...], preferred_element_type=F32)


def _memkv(mem, g_mem, wk, wv, tm=512):
    rows = mem.shape[0]
    row_spec = pl.BlockSpec((tm, D_MODEL), lambda i: (i, 0))
    return pl.pallas_call(
        _memkv_kernel,
        grid=(rows // tm,),
        in_specs=[row_spec, _const_spec((1, D_MODEL)),
                  _const_spec((D_MODEL, D_MODEL)), _const_spec((D_MODEL, D_MODEL))],
        out_specs=[row_spec, row_spec],
        out_shape=[jax.ShapeDtypeStruct((rows, D_MODEL), F32)] * 2,
        compiler_params=_params("parallel"),
    )(mem, g_mem, wk, wv)


def _mixer_prompt_kernel(x_ref, w_ref, gmix_ref, bg_ref, wconv_ref, bconv_ref, gmh_ref, grh_ref,
                         cos_ref, sin_ref, decay_ref, qdec_ref, kdec_ref, cdec_ref,
                         h_ref, conv_ref, C_ref, n_ref, m_ref, S_ref, u_ref, *, tt):
    L = CHUNK
    t = pl.program_id(1)

    @pl.when(t == 0)
    def _():
        C_ref[...] = jnp.zeros_like(C_ref)
        S_ref[...] = jnp.zeros_like(S_ref)
        n_ref[...] = jnp.zeros_like(n_ref)
        m_ref[...] = jnp.zeros_like(m_ref)
        u_ref[0:UOFF, :] = jnp.zeros((UOFF, IN_PAD), F32)

    @pl.when(t > 0)
    def _():
        u_ref[0:UOFF, COL_QK:COL_V] = u_ref[tt:tt + UOFF, COL_QK:COL_V]

    xn = _rms(x_ref[...], gmix_ref[...]).astype(BF16)
    for c0, c1 in IN_CHUNKS:
        u_ref[UOFF:UOFF + tt, c0:c1] = jnp.dot(xn, w_ref[:, c0:c1], preferred_element_type=F32)

    row_id = lax.broadcasted_iota(jnp.int32, (L, L), 0)
    col_id = lax.broadcasted_iota(jnp.int32, (L, L), 1)
    causal = row_id >= col_id
    tril_bf = jnp.where(causal, 1.0, 0.0).astype(BF16)
    k_scale = HEAD_DIM ** -0.5

    def chunk_body(c, carry_unused):
        r0 = pl.multiple_of(c * L, L)
        rows = pl.ds(UOFF + r0, L)
        gates = u_ref[rows, COL_G:IN_PAD] + bg_ref[...]
        ig = gates
        lf = _log_sigmoid(pltpu.roll(gates, HEAD_DIM - HEADS, 1))
        lf_hi = lf.astype(BF16)
        r1 = lf - lf_hi.astype(F32)
        lf_mid = r1.astype(BF16)
        lf_lo = (r1 - lf_mid.astype(F32)).astype(BF16)
        bc = (jnp.dot(tril_bf, lf_hi, preferred_element_type=F32)
              + jnp.dot(tril_bf, lf_mid, preferred_element_type=F32)
              + jnp.dot(tril_bf, lf_lo, preferred_element_type=F32))
        bc_t = bc.T
        ig_t = ig.T
        m_prev = m_ref[0:1, :]
        inter_all = bc + m_prev
        b_last = bc[L - 1:L, :]
        g_all = b_last - bc + ig
        m_new = jnp.maximum(b_last + m_prev, jnp.max(g_all, axis=0, keepdims=True))
        ws_all = jnp.exp(g_all - m_new)
        carry_all = jnp.exp(b_last + m_prev - m_new)

        for h in range(HEADS):
            lo = h * HEAD_DIM
            def conv_act(col):
                win = u_ref[pl.ds(r0, L + UOFF), col:col + HEAD_DIM]
                acc = bconv_ref[:, col:col + HEAD_DIM]
                for j in range(CONV_W):
                    s = UOFF - (CONV_W - 1) + j
                    acc = acc + win[s:s + L] * wconv_ref[j:j + 1, col:col + HEAD_DIM]
                return _silu(acc)

            q = conv_act(COL_QK + lo)
            k = conv_act(COL_QK + MIX + lo) * k_scale
            v = u_ref[rows, COL_V + lo:COL_V + lo + HEAD_DIM]
            mo = u_ref[rows, COL_O + lo:COL_O + lo + HEAD_DIM]
            qb, kb, vb = q.astype(BF16), k.astype(BF16), v.astype(BF16)
            qk = _dot_nt(qb, kb)
            dmat = bc[:, h:h + 1] - bc_t[h:h + 1, :] + ig_t[h:h + 1, :]
            dmat = jnp.where(causal, dmat, -jnp.inf)
            inter = inter_all[:, h:h + 1]
            m_t = jnp.maximum(inter, jnp.max(dmat, axis=1, keepdims=True))
            wts = jnp.exp(dmat - m_t) * qk
            w_in = jnp.exp(inter - m_t)
            c_h = C_ref[h]
            n_h = n_ref[h:h + 1, :]
            num = _dot(wts, vb) + w_in * _dot_nt(qb, c_h)
            den = (jnp.sum(wts, axis=1, keepdims=True)
                   + w_in * jnp.sum(q * n_h, axis=1, keepdims=True))
            hm = num / jnp.maximum(jnp.abs(den), jnp.exp(-m_t))
            hm = _head_norm(hm) * gmh_ref[:, lo:lo + HEAD_DIM] * jax.nn.sigmoid(mo)
            h_ref[pl.ds(r0, L), lo:lo + HEAD_DIM] = hm.astype(BF16)
            ws = ws_all[:, h:h + 1]
            carry = carry_all[:, h:h + 1]
            C_ref[h] = carry * c_h + _dot_tn(v * ws, kb)
            n_ref[h:h + 1, :] = carry * n_h + jnp.sum(ws * k, axis=0, keepdims=True)

            cos_f = cos_ref[pl.ds(r0, L), :]
            sin_s = sin_ref[pl.ds(r0, L), :]
            rq = _rope(u_ref[rows, COL_RQ + lo:COL_RQ + lo + HEAD_DIM], cos_f, sin_s)
            rk = _rope(u_ref[rows, COL_RK + lo:COL_RK + lo + HEAD_DIM], cos_f, sin_s) * k_scale
            rv = u_ref[rows, COL_RV + lo:COL_RV + lo + HEAD_DIM].astype(BF16)
            rg = u_ref[rows, COL_RG + lo:COL_RG + lo + HEAD_DIM]
            rqb = rq.astype(BF16)
            s_h = S_ref[h]
            att = _dot_nt(rqb, rk) * decay_ref[h]
            o = _dot(att, rv) + qdec_ref[:, h:h + 1] * _dot(rqb, s_h)
            S_ref[h] = cdec_ref[0:1, h:h + 1] * s_h + _dot_tn(rk * kdec_ref[:, h:h + 1], rv)
            hr = _head_norm(o) * grh_ref[:, lo:lo + HEAD_DIM] * _silu(rg)
            h_ref[pl.ds(r0, L), MIX + lo:MIX + lo + HEAD_DIM] = hr.astype(BF16)

        m_ref[0:1, :] = m_new
        return carry_unused

    lax.fori_loop(0, tt // L, chunk_body, 0)

    @pl.when(t == pl.num_programs(1) - 1)
    def _():
        conv_ref[...] = u_ref[UOFF + tt - (CONV_W - 1):UOFF + tt, COL_QK:COL_V]


def _mixer_prompt(x, w1, g_mix, bg, w_conv, b_conv, g_mh, g_rh, cos_f, sin_s,
                  decay, qdec, kdec, cdec, tt=512):
    B, T, _ = x.shape
    tile = lambda b, t: (b, t, 0)
    per_b3 = lambda b, t: (b, 0, 0)
    per_b4 = lambda b, t: (b, 0, 0, 0)
    in_specs = [
        pl.BlockSpec((None, tt, D_MODEL), tile),
        _const_spec((D_MODEL, IN_PAD)),
        _const_spec((1, D_MODEL)), _const_spec((1, HEAD_DIM)),
        _const_spec((CONV_W, 2 * MIX)), _const_spec((1, 2 * MIX)),
        _const_spec((1, MIX)), _const_spec((1, MIX)),
        pl.BlockSpec((tt, HEAD_DIM), lambda b, t: (t, 0)),
        pl.BlockSpec((tt, HEAD_DIM), lambda b, t: (t, 0)),
        _const_spec((HEADS, CHUNK, CHUNK)),
        _const_spec((CHUNK, HEAD_DIM)), _const_spec((CHUNK, HEAD_DIM)), _const_spec((8, HEAD_DIM)),
    ]
    out_specs = [
        pl.BlockSpec((None, tt, D_MODEL), tile),
        pl.BlockSpec((None, CONV_W - 1, 2 * MIX), per_b3),
        pl.BlockSpec((None, HEADS, HEAD_DIM, HEAD_DIM), per_b4),
        pl.BlockSpec((None, HEADS, HEAD_DIM), per_b3),
        pl.BlockSpec((None, 8, HEAD_DIM), per_b3),
        pl.BlockSpec((None, HEADS, HEAD_DIM, HEAD_DIM), per_b4),
    ]
    out_shape = [
        jax.ShapeDtypeStruct((B, T, D_MODEL), BF16),
        jax.ShapeDtypeStruct((B, CONV_W - 1, 2 * MIX), F32),
        jax.ShapeDtypeStruct((B, HEADS, HEAD_DIM, HEAD_DIM), F32),
        jax.ShapeDtypeStruct((B, HEADS, HEAD_DIM), F32),
        jax.ShapeDtypeStruct((B, 8, HEAD_DIM), F32),
        jax.ShapeDtypeStruct((B, HEADS, HEAD_DIM, HEAD_DIM), F32),
    ]
    return pl.pallas_call(
        functools.partial(_mixer_prompt_kernel, tt=tt),
        grid=(B, T // tt),
        in_specs=in_specs, out_specs=out_specs, out_shape=out_shape,
        scratch_shapes=[pltpu.VMEM((tt + UOFF, IN_PAD), F32)],
        compiler_params=_params("parallel", "arbitrary"),
    )(x, w1, g_mix, bg, w_conv, b_conv, g_mh, g_rh, cos_f, sin_s, decay, qdec, kdec, cdec)


def _softmax_rows(s):
    e = jnp.exp(s - jnp.max(s, axis=-1, keepdims=True))
    return e / jnp.sum(e, axis=-1, keepdims=True)


def _attn_prompt_kernel(x_ref, h_ref, k_ref, v_ref, wout_ref, wcq_ref, gx_ref, x1_ref, o_ref):
    x1 = x_ref[...] + jnp.dot(h_ref[...], wout_ref[...], preferred_element_type=F32)
    x1_ref[...] = x1
    q = jnp.dot(_rms(x1, gx_ref[...]).astype(BF16), wcq_ref[...],
                preferred_element_type=F32).astype(BF16)
    for h in range(X_HEADS):
        sl = slice(h * X_HEAD_DIM, (h + 1) * X_HEAD_DIM)
        s = _dot_nt(q[:, sl], k_ref[:, sl]) * (X_HEAD_DIM ** -0.5)
        o_ref[:, sl] = _dot(_softmax_rows(s), v_ref[:, sl]).astype(BF16)


def _attn_prompt(x, hcat, mk, mv, w_out, w_cq, g_x, tm=512):
    B, T, _ = x.shape
    tile = pl.BlockSpec((None, tm, D_MODEL), lambda b, t: (b, t, 0))
    kv = pl.BlockSpec((None, N_MEM, D_MODEL), lambda b, t: (b, 0, 0))
    return pl.pallas_call(
        _attn_prompt_kernel,
        grid=(B, T // tm),
        in_specs=[tile, tile, kv, kv, _const_spec((D_MODEL, D_MODEL)),
                  _const_spec((D_MODEL, D_MODEL)), _const_spec((1, D_MODEL))],
        out_specs=[tile, tile],
        out_shape=[jax.ShapeDtypeStruct((B, T, D_MODEL), F32),
                   jax.ShapeDtypeStruct((B, T, D_MODEL), BF16)],
        compiler_params=_params("parallel", "parallel"),
    )(x, hcat, mk, mv, w_out, w_cq, g_x)


FF_CHUNK = D_FF // 2


def _ffn_kernel(x1_ref, o_ref, wco_ref, wg_ref, wu_ref, wd_ref, gffn_ref, gfin_ref, y_ref):
    x2 = x1_ref[...] + _dot(o_ref[...], wco_ref[...])
    hf = _rms(x2, gffn_ref[...]).astype(BF16)
    acc = x2
    for c0 in range(0, D_FF, FF_CHUNK):
        gate = jnp.dot(hf, wg_ref[:, c0:c0 + FF_CHUNK], preferred_element_type=F32)
        up = jnp.dot(hf, wu_ref[:, c0:c0 + FF_CHUNK], preferred_element_type=F32)
        acc = acc + _dot(_silu(gate) * up, wd_ref[c0:c0 + FF_CHUNK, :])
    y_ref[...] = _rms(acc, gfin_ref[...])


def _ffn(x1, o, w_co, w_gate, w_up, w_down, g_ffn, g_final, tm):
    rows = x1.shape[0]
    row_spec = pl.BlockSpec((tm, D_MODEL), lambda i: (i, 0))
    return pl.pallas_call(
        _ffn_kernel,
        grid=(rows // tm,),
        in_specs=[row_spec, row_spec, _const_spec((D_MODEL, D_MODEL)),
                  _const_spec((D_MODEL, D_FF)), _const_spec((D_MODEL, D_FF)),
                  _const_spec((D_FF, D_MODEL)), _const_spec((1, D_MODEL)), _const_spec((1, D_MODEL))],
        out_specs=row_spec,
        out_shape=jax.ShapeDtypeStruct((rows, D_MODEL), F32),
        compiler_params=_params("parallel"),
    )(x1, o, w_co, w_gate, w_up, w_down, g_ffn, g_final)


def _inproj_kernel(x_ref, g_ref, w_ref, u_ref):
    u_ref[...] = jnp.dot(_rms(x_ref[...], g_ref[...]).astype(BF16), w_ref[...],
                         preferred_element_type=F32)


def _inproj_sample(x, g_mix, w1, nc=1408):
    rows = x.shape[0]
    return pl.pallas_call(
        _inproj_kernel,
        grid=(IN_PAD // nc,),
        in_specs=[_const_spec((rows, D_MODEL)), _const_spec((1, D_MODEL)),
                  pl.BlockSpec((D_MODEL, nc), lambda j: (0, j))],
        out_specs=pl.BlockSpec((rows, nc), lambda j: (0, j)),
        out_shape=jax.ShapeDtypeStruct((rows, IN_PAD), F32),
        compiler_params=_params("parallel"),
    )(x, g_mix, w1)


def _mixer_sample_kernel(u_ref, conv_ref, C_ref, n_ref, m_ref, S_ref, bg_ref, wconv_ref, bconv_ref,
                         gmh_ref, grh_ref, cos_ref, sin_ref, rtab_ref,
                         h_ref, convo_ref, Co_ref, no_ref, mo_ref, So_ref,
                         q_s, k_s, vws_s, qr_s, kr_s, qc_s, qs_s, carry_s, *, bb):
    k_scale = HEAD_DIM ** -0.5
    uqk = u_ref[:, COL_QK:COL_V]
    conv = (bconv_ref[...] + wconv_ref[0:1, :] * conv_ref[0] + wconv_ref[1:2, :] * conv_ref[1]
            + wconv_ref[2:3, :] * conv_ref[2] + wconv_ref[3:4, :] * uqk)
    convo_ref[0] = conv_ref[1]
    convo_ref[1] = conv_ref[2]
    convo_ref[2] = uqk
    qk_act = _silu(conv)
    q_s[...] = qk_act[:, 0:MIX]
    k_s[...] = qk_act[:, MIX:2 * MIX] * k_scale

    gates = u_ref[:, COL_G:IN_PAD] + bg_ref[...]
    ig = gates[:, 0:HEADS]
    lf = _log_sigmoid(gates[:, HEADS:2 * HEADS])
    inter = lf + m_ref[...]
    m_t = jnp.maximum(inter, ig)
    ws = jnp.exp(ig - m_t)
    w_in = jnp.exp(inter - m_t)
    mo_ref[...] = m_t
    carry_s[:, 0:HEADS] = w_in

    cos_f = cos_ref[...]
    sin_s = sin_ref[...]
    for h in range(HEADS):
        lo = h * HEAD_DIM
        hs = slice(lo, lo + HEAD_DIM)
        vws_s[:, hs] = u_ref[:, COL_V + lo:COL_V + lo + HEAD_DIM] * ws[:, h:h + 1]
        qr_s[:, hs] = _rope(u_ref[:, COL_RQ + lo:COL_RQ + lo + HEAD_DIM], cos_f, sin_s)
        kr_s[:, hs] = _rope(u_ref[:, COL_RK + lo:COL_RK + lo + HEAD_DIM], cos_f, sin_s) * k_scale

    for b in range(bb):
        row = slice(b, b + 1)
        for h in range(HEADS):
            lo = h * HEAD_DIM
            hs = slice(lo, lo + HEAD_DIM)
            c_h = C_ref[b, h]
            q8 = jnp.broadcast_to(q_s[row, hs], (8, HEAD_DIM))
            qc_s[row, hs] = _dot_nt(q8, c_h)[0:1]
            v_col = jnp.broadcast_to(vws_s[row, hs], (HEAD_DIM, HEAD_DIM)).T
            Co_ref[b, h] = carry_s[row, h:h + 1] * c_h + v_col * k_s[row, hs]
            s_h = S_ref[b, h]
            qr8 = jnp.broadcast_to(qr_s[row, hs], (8, HEAD_DIM))
            qs_s[row, hs] = _dot(qr8, s_h)[0:1]
            k_col = jnp.broadcast_to(kr_s[row, hs] * rtab_ref[1:2, h:h + 1], (HEAD_DIM, HEAD_DIM)).T
            So_ref[b, h] = (rtab_ref[2:3, h:h + 1] * s_h
                            + k_col * u_ref[row, COL_RV + lo:COL_RV + lo + HEAD_DIM])

    for h in range(HEADS):
        lo = h * HEAD_DIM
        hs = slice(lo, lo + HEAD_DIM)
        q_h, k_h, n_h = q_s[:, hs], k_s[:, hs], n_ref[:, hs]
        ws_h, w_in_h = ws[:, h:h + 1], w_in[:, h:h + 1]
        wts = ws_h * jnp.sum(q_h * k_h, axis=1, keepdims=True)
        num = wts * u_ref[:, COL_V + lo:COL_V + lo + HEAD_DIM] + w_in_h * qc_s[:, hs]
        den = wts + w_in_h * jnp.sum(n_h * q_h, axis=1, keepdims=True)
        hm = num / jnp.maximum(jnp.abs(den), jnp.exp(-m_t[:, h:h + 1]))
        mo = u_ref[:, COL_O + lo:COL_O + lo + HEAD_DIM]
        h_ref[:, hs] = _head_norm(hm) * gmh_ref[:, hs] * jax.nn.sigmoid(mo)
        no_ref[:, hs] = w_in_h * n_h + ws_h * k_h

        att = jnp.sum(qr_s[:, hs] * kr_s[:, hs], axis=1, keepdims=True) * rtab_ref[3:4, h:h + 1]
        o = (att * u_ref[:, COL_RV + lo:COL_RV + lo + HEAD_DIM]
             + rtab_ref[0:1, h:h + 1] * qs_s[:, hs])
        rg = u_ref[:, COL_RG + lo:COL_RG + lo + HEAD_DIM]
        h_ref[:, MIX + lo:MIX + lo + HEAD_DIM] = _head_norm(o) * grh_ref[:, hs] * _silu(rg)


def _mixer_sample(u, conv_t, C, n2, m, S, bg, w_conv, b_conv, g_mh, g_rh, cos_f, sin_s, rtab, bb=8):
    B = u.shape[0]
    rows = lambda cols: pl.BlockSpec((bb, cols), lambda i: (i, 0))
    mats = pl.BlockSpec((bb, HEADS, HEAD_DIM, HEAD_DIM), lambda i: (i, 0, 0, 0))
    convs = pl.BlockSpec((CONV_W - 1, bb, 2 * MIX), lambda i: (0, i, 0))
    in_specs = [rows(IN_PAD), convs, mats, rows(MIX), rows(HEADS), mats,
                _const_spec((1, HEAD_DIM)), _const_spec((CONV_W, 2 * MIX)), _const_spec((1, 2 * MIX)),
                _const_spec((1, MIX)), _const_spec((1, MIX)),
                _const_spec((1, HEAD_DIM)), _const_spec((1, HEAD_DIM)), _const_spec((8, HEAD_DIM))]
    out_specs = [rows(D_MODEL), convs, mats, rows(MIX), rows(HEADS), mats]
    out_shape = [jax.ShapeDtypeStruct((B, D_MODEL), F32),
                 jax.ShapeDtypeStruct((CONV_W - 1, B, 2 * MIX), F32),
                 jax.ShapeDtypeStruct((B, HEADS, HEAD_DIM, HEAD_DIM), F32),
                 jax.ShapeDtypeStruct((B, MIX), F32),
                 jax.ShapeDtypeStruct((B, HEADS), F32),
                 jax.ShapeDtypeStruct((B, HEADS, HEAD_DIM, HEAD_DIM), F32)]
    scratch = [pltpu.VMEM((bb, MIX), F32) for _ in range(7)] + [pltpu.VMEM((bb, HEAD_DIM), F32)]
    return pl.pallas_call(
        functools.partial(_mixer_sample_kernel, bb=bb),
        grid=(B // bb,),
        in_specs=in_specs, out_specs=out_specs, out_shape=out_shape,
        scratch_shapes=scratch,
        compiler_params=_params("parallel"),
    )(u, conv_t, C, n2, m, S, bg, w_conv, b_conv, g_mh, g_rh, cos_f, sin_s, rtab)


def _outq_sample_kernel(x_ref, h_ref, wout_ref, wcq_ref, gx_ref, x1_ref, q_ref):
    x1 = x_ref[...] + _dot(h_ref[...], wout_ref[...])
    x1_ref[...] = x1
    q_ref[...] = _dot(_rms(x1, gx_ref[...]), wcq_ref[...])


def _outq_sample(x, hcat, w_out, w_cq, g_x):
    rows = x.shape[0]
    full = _const_spec((rows, D_MODEL))
    return pl.pallas_call(
        _outq_sample_kernel,
        grid=(1,),
        in_specs=[full, full, _const_spec((D_MODEL, D_MODEL)), _const_spec((D_MODEL, D_MODEL)),
                  _const_spec((1, D_MODEL))],
        out_specs=[pl.BlockSpec((rows, D_MODEL), lambda i: (0, 0))] * 2,
        out_shape=[jax.ShapeDtypeStruct((rows, D_MODEL), F32)] * 2,
        compiler_params=_params("arbitrary"),
    )(x, hcat, w_out, w_cq, g_x)


def _attn_sample_kernel(q_ref, k_ref, v_ref, o_ref, *, ba):
    for j in range(ba):
        row = slice(j, j + 1)
        for h in range(X_HEADS):
            sl = slice(h * X_HEAD_DIM, (h + 1) * X_HEAD_DIM)
            q8 = jnp.broadcast_to(q_ref[row, sl], (8, X_HEAD_DIM))
            s = _dot_nt(q8, k_ref[j, :, sl])[0:1] * (X_HEAD_DIM ** -0.5)
            p8 = jnp.broadcast_to(_softmax_rows(s), (8, N_MEM))
            o_ref[row, sl] = _dot(p8, v_ref[j, :, sl])[0:1]


def _attn_sample(q, ck, cv, ba=4):
    B = q.shape[0]
    kv = pl.BlockSpec((ba, N_MEM, D_MODEL), lambda i: (i, 0, 0))
    rows = pl.BlockSpec((None, ba, D_MODEL), lambda i: (i, 0, 0))
    return pl.pallas_call(
        functools.partial(_attn_sample_kernel, ba=ba),
        grid=(B // ba,),
        in_specs=[rows, kv, kv],
        out_specs=rows,
        out_shape=jax.ShapeDtypeStruct((B // ba, ba, D_MODEL), F32),
        compiler_params=_params("parallel"),
    )(q.reshape(B // ba, ba, D_MODEL), ck, cv).reshape(B, D_MODEL)


def _rope_tables(pos):
    half = HEAD_DIM // 2
    inv = ROPE_THETA ** (-jnp.arange(half, dtype=F32) / half)
    ang = pos[:, None] * inv[None, :]
    cos, sin = jnp.cos(ang), jnp.sin(ang)
    return jnp.concatenate([cos, cos], axis=-1), jnp.concatenate([-sin, sin], axis=-1)


def _retention_tables(L):
    lg = jnp.log1p(-jnp.exp2(-5.0 - jnp.arange(HEADS, dtype=F32)))
    t = jnp.arange(L, dtype=F32)
    diff = t[:, None] - t[None, :]
    decay = jnp.where(diff >= 0, jnp.exp(lg[:, None, None] * jnp.maximum(diff, 0.0)), 0.0)
    q_dec = jnp.exp(lg[:, None] * (t + 1.0))
    k_dec = jnp.exp(lg[:, None] * (L - 1.0 - t))
    chunk_dec = jnp.exp(lg * L)
    return decay, q_dec, k_dec, chunk_dec


def _lane_pad(a):
    return jnp.pad(a, ((0, 0), (0, HEAD_DIM - a.shape[1])))


def kernel(x_prompt, x_sample, cache_mem_k, cache_mem_v, state_mlstm_conv, state_mlstm_C, state_mlstm_n, state_mlstm_m, state_ret_S, mem_prompt, w_in, b_gate, w_conv, b_conv, g_mix, g_mhead, g_rhead, w_out, g_xattn, g_mem, w_ck, w_cv, w_cq, w_co, g_ffn, w_gate, w_up, w_down, g_final):
    Bp, Tp, _ = x_prompt.shape
    Bs = x_sample.shape[0]
    l = 0
    n_m = 4 * MIX
    wi = w_in[l]
    w1 = jnp.concatenate(
        [wi[:, :n_m], wi[:, n_m + 2 * HEADS:], wi[:, n_m:n_m + 2 * HEADS],
         jnp.zeros((D_MODEL, IN_PAD - COL_G - 2 * HEADS), F32)], axis=1).astype(BF16)
    bg = jnp.pad(b_gate[l], (0, HEAD_DIM - 2 * HEADS)).reshape(1, HEAD_DIM)
    row = lambda a: a.reshape(1, -1)
    bf = lambda a: a.astype(BF16)
    g_mix_r, g_mh_r, g_rh_r = row(g_mix[l]), row(g_mhead[l]), row(g_rhead[l])
    g_x_r, g_mem_r, g_ffn_r, g_fin_r = row(g_xattn[l]), row(g_mem[l]), row(g_ffn[l]), row(g_final)
    b_conv_r = row(b_conv[l])
    w_out_b, w_cq_b, w_co_b = bf(w_out[l]), bf(w_cq[l]), bf(w_co[l])
    w_gate_b, w_up_b, w_down_b = bf(w_gate[l]), bf(w_up[l]), bf(w_down[l])

    mk, mv = _memkv(mem_prompt.reshape(Bp * N_MEM, D_MODEL), g_mem_r, bf(w_ck[l]), bf(w_cv[l]))
    cos_p, sin_p = _rope_tables(jnp.arange(Tp, dtype=F32))
    decay, q_dec, k_dec, chunk_dec = _retention_tables(CHUNK)
    cdec = jnp.pad(_lane_pad(chunk_dec[None, :]), ((0, 7), (0, 0)))
    hcat_p, conv_p, C_p, n_p, m_p, S_p = _mixer_prompt(
        x_prompt, w1, g_mix_r, bg, w_conv[l], b_conv_r, g_mh_r, g_rh_r, cos_p, sin_p,
        decay, _lane_pad(q_dec.T), _lane_pad(k_dec.T), cdec)
    x1_p, o_p = _attn_prompt(x_prompt, hcat_p, mk.reshape(Bp, N_MEM, D_MODEL),
                             mv.reshape(Bp, N_MEM, D_MODEL), w_out_b, w_cq_b, g_x_r)
    y_p = _ffn(x1_p.reshape(Bp * Tp, D_MODEL), o_p.reshape(Bp * Tp, D_MODEL),
               w_co_b, w_gate_b, w_up_b, w_down_b, g_ffn_r, g_fin_r, tm=512)

    xs = x_sample.reshape(Bs, D_MODEL)
    u_s = _inproj_sample(xs, g_mix_r, w1)
    cos_s, sin_s = _rope_tables(PAST_LEN + jnp.arange(1, dtype=F32))
    decay1, q_dec1, k_dec1, chunk_dec1 = _retention_tables(1)
    rtab = jnp.pad(_lane_pad(jnp.stack([q_dec1[:, 0], k_dec1[:, 0], chunk_dec1, decay1[:, 0, 0]])),
                   ((0, 4), (0, 0)))
    hcat_s, conv_s, C_s, n_s, m_s, S_s = _mixer_sample(
        u_s, jnp.transpose(state_mlstm_conv[l], (1, 0, 2)), state_mlstm_C[l],
        state_mlstm_n[l].reshape(Bs, MIX), state_mlstm_m[l], state_ret_S[l],
        bg, w_conv[l], b_conv_r, g_mh_r, g_rh_r, cos_s, sin_s, rtab)
    x1_s, q_s = _outq_sample(xs, hcat_s, w_out_b, w_cq_b, g_x_r)
    o_s = _attn_sample(q_s, cache_mem_k[l].reshape(Bs, N_MEM, D_MODEL),
                       cache_mem_v[l].reshape(Bs, N_MEM, D_MODEL))
    y_s = _ffn(x1_s, o_s, w_co_b, w_gate_b, w_up_b, w_down_b, g_ffn_r, g_fin_r, tm=Bs)

    kv_shape = (1, Bp, N_MEM, X_HEADS, X_HEAD_DIM)
    return (y_p.reshape(Bp, Tp, D_MODEL), y_s.reshape(Bs, 1, D_MODEL),
            mk.reshape(kv_shape), mv.reshape(kv_shape),
            conv_p[None], C_p[None], n_p[None], m_p[None, :, 0, :HEADS], S_p[None],
            jnp.transpose(conv_s, (1, 0, 2))[None], C_s[None],
            n_s.reshape(1, Bs, HEADS, HEAD_DIM), m_s[None], S_s[None])
```

```python
import functools

import jax
import jax.numpy as jnp
from jax import lax
from jax.experimental import pallas as pl
from jax.experimental.pallas import tpu as pltpu

F32 = jnp.float32
BF16 = jnp.bfloat16

D_MODEL = 1024
HEADS = 4
HEAD_DIM = 128
MIX = HEADS * HEAD_DIM
CONV_W = 4
CHUNK = 128
N_MEM = 256
X_HEADS = 4
X_HEAD_DIM = 256
D_FF = 2816
ROPE_THETA = 10000.0
EPS = 1e-6
PAST_LEN = 16384

COL_QK, COL_V, COL_O = 0, 1024, 1536
COL_RQ, COL_RK, COL_RV, COL_RG = 2048, 2560, 3072, 3584
COL_G = 4096
IN_PAD = 4224
IN_CHUNKS = ((0, 1024), (1024, 2048), (2048, 3072), (3072, 4096), (4096, IN_PAD))
UOFF = 8

V7X_VMEM_LIMIT = 56 * 1024 * 1024

NT_DIMS = (((1,), (1,)), ((), ()))
TN_DIMS = (((0,), (0,)), ((), ()))


def _dot(a, b):
    return jnp.dot(a.astype(BF16), b.astype(BF16), preferred_element_type=F32)


def _dot_nt(a, b):
    return lax.dot_general(a.astype(BF16), b.astype(BF16), NT_DIMS, preferred_element_type=F32)


def _dot_tn(a, b):
    return lax.dot_general(a.astype(BF16), b.astype(BF16), TN_DIMS, preferred_element_type=F32)


def _rms(x, g):
    return x * lax.rsqrt(jnp.mean(x * x, axis=-1, keepdims=True) + EPS) * g


def _head_norm(h):
    return h * lax.rsqrt(jnp.mean(h * h, axis=-1, keepdims=True) + EPS)


def _silu(x):
    return x * jax.nn.sigmoid(x)


def _log_sigmoid(x):
    return jnp.minimum(x, 0.0) - jnp.log1p(jnp.exp(-jnp.abs(x)))


def _rope(x, cos_full, sin_signed):
    return x * cos_full + pltpu.roll(x, HEAD_DIM // 2, 1) * sin_signed


def _const_spec(shape):
    zeros = (0,) * len(shape)
    return pl.BlockSpec(shape, lambda *_: zeros, pipeline_mode=pl.Buffered(1))


def _params(*sem):
    return pltpu.CompilerParams(dimension_semantics=sem, vmem_limit_bytes=V7X_VMEM_LIMIT)


def _memkv_kernel(mem_ref, g_ref, wk_ref, wv_ref, k_ref, v_ref):
    mn = _rms(mem_ref[...], g_ref[...]).astype(BF16)
    k_ref[...] = jnp.dot(mn, wk_ref[...], preferred_element_type=F32)
    v_ref[...] = jnp.dot(mn, wv_ref[...], preferred_element_type=F32)


def _memkv(mem, g_mem, wk, wv, tm=512):
    rows = mem.shape[0]
    row_spec = pl.BlockSpec((tm, D_MODEL), lambda i: (i, 0))
    return pl.pallas_call(
        _memkv_kernel,
        grid=(rows // tm,),
        in_specs=[row_spec, _const_spec((1, D_MODEL)),
                  _const_spec((D_MODEL, D_MODEL)), _const_spec((D_MODEL, D_MODEL))],
        out_specs=[row_spec, row_spec],
        out_shape=[jax.ShapeDtypeStruct((rows, D_MODEL), F32)] * 2,
        compiler_params=_params("parallel"),
    )(mem, g_mem, wk, wv)


def _mixer_prompt_kernel(x_ref, w_ref, gmix_ref, bg_ref, wconv_ref, bconv_ref, gmh_ref, grh_ref,
                         cos_ref, sin_ref, decay_ref, qdec_ref, kdec_ref, cdec_ref,
                         h_ref, conv_ref, C_ref, n_ref, m_ref, S_ref, u_ref, *, tt):
    L = CHUNK
    t = pl.program_id(1)

    @pl.when(t == 0)
    def _():
        C_ref[...] = jnp.zeros_like(C_ref)
        S_ref[...] = jnp.zeros_like(S_ref)
        n_ref[...] = jnp.zeros_like(n_ref)
        m_ref[...] = jnp.zeros_like(m_ref)
        u_ref[0:UOFF, :] = jnp.zeros((UOFF, IN_PAD), F32)

    @pl.when(t > 0)
    def _():
        u_ref[0:UOFF, COL_QK:COL_V] = u_ref[tt:tt + UOFF, COL_QK:COL_V]

    xn = _rms(x_ref[...], gmix_ref[...]).astype(BF16)
    for c0, c1 in IN_CHUNKS:
        u_ref[UOFF:UOFF + tt, c0:c1] = jnp.dot(xn, w_ref[:, c0:c1], preferred_element_type=F32)

    row_id = lax.broadcasted_iota(jnp.int32, (L, L), 0)
    col_id = lax.broadcasted_iota(jnp.int32, (L, L), 1)
    causal = row_id >= col_id
    tril_bf = jnp.where(causal, 1.0, 0.0).astype(BF16)
    k_scale = HEAD_DIM ** -0.5

    def chunk_body(c, carry_unused):
        r0 = pl.multiple_of(c * L, L)
        rows = pl.ds(UOFF + r0, L)
        gates = u_ref[rows, COL_G:IN_PAD] + bg_ref[...]
        ig = gates
        lf = _log_sigmoid(pltpu.roll(gates, HEAD_DIM - HEADS, 1))
        lf_hi = lf.astype(BF16)
        r1 = lf - lf_hi.astype(F32)
        lf_mid = r1.astype(BF16)
        lf_lo = (r1 - lf_mid.astype(F32)).astype(BF16)
        bc = (jnp.dot(tril_bf, lf_hi, preferred_element_type=F32)
              + jnp.dot(tril_bf, lf_mid, preferred_element_type=F32)
              + jnp.dot(tril_bf, lf_lo, preferred_element_type=F32))
        bc_t = bc.T
        ig_t = ig.T
        m_prev = m_ref[0:1, :]
        inter_all = bc + m_prev
        b_last = bc[L - 1:L, :]
        g_all = b_last - bc + ig
        m_new = jnp.maximum(b_last + m_prev, jnp.max(g_all, axis=0, keepdims=True))
        ws_all = jnp.exp(g_all - m_new)
        carry_all = jnp.exp(b_last + m_prev - m_new)

        for h in range(HEADS):
            lo = h * HEAD_DIM
            def conv_act(col):
                win = u_ref[pl.ds(r0, L + UOFF), col:col + HEAD_DIM]
                acc = bconv_ref[:, col:col + HEAD_DIM]
                for j in range(CONV_W):
                    s = UOFF - (CONV_W - 1) + j
                    acc = acc + win[s:s + L] * wconv_ref[j:j + 1, col:col + HEAD_DIM]
                return _silu(acc)

            q = conv_act(COL_QK + lo)
            k = conv_act(COL_QK + MIX + lo) * k_scale
            v = u_ref[rows, COL_V + lo:COL_V + lo + HEAD_DIM]
            mo = u_ref[rows, COL_O + lo:COL_O + lo + HEAD_DIM]
            qb, kb, vb = q.astype(BF16), k.astype(BF16), v.astype(BF16)
            qk = _dot_nt(qb, kb)
            dmat = bc[:, h:h + 1] - bc_t[h:h + 1, :] + ig_t[h:h + 1, :]
            dmat = jnp.where(causal, dmat, -jnp.inf)
            inter = inter_all[:, h:h + 1]
            m_t = jnp.maximum(inter, jnp.max(dmat, axis=1, keepdims=True))
            wts = jnp.exp(dmat - m_t) * qk
            w_in = jnp.exp(inter - m_t)
            c_h = C_ref[h]
            n_h = n_ref[h:h + 1, :]
            num = _dot(wts, vb) + w_in * _dot_nt(qb, c_h)
            den = (jnp.sum(wts, axis=1, keepdims=True)
                   + w_in * jnp.sum(q * n_h, axis=1, keepdims=True))
            hm = num / jnp.maximum(jnp.abs(den), jnp.exp(-m_t))
            hm = _head_norm(hm) * gmh_ref[:, lo:lo + HEAD_DIM] * jax.nn.sigmoid(mo)
            h_ref[pl.ds(r0, L), lo:lo + HEAD_DIM] = hm.astype(BF16)
            ws = ws_all[:, h:h + 1]
            carry = carry_all[:, h:h + 1]
            C_ref[h] = carry * c_h + _dot_tn(v * ws, kb)
            n_ref[h:h + 1, :] = carry * n_h + jnp.sum(ws * k, axis=0, keepdims=True)

            cos_f = cos_ref[pl.ds(r0, L), :]
            sin_s = sin_ref[pl.ds(r0, L), :]
            rq = _rope(u_ref[rows, COL_RQ + lo:COL_RQ + lo + HEAD_DIM], cos_f, sin_s)
            rk = _rope(u_ref[rows, COL_RK + lo:COL_RK + lo + HEAD_DIM], cos_f, sin_s) * k_scale
            rv = u_ref[rows, COL_RV + lo:COL_RV + lo + HEAD_DIM].astype(BF16)
            rg = u_ref[rows, COL_RG + lo:COL_RG + lo + HEAD_DIM]
            rqb = rq.astype(BF16)
            s_h = S_ref[h]
            att = _dot_nt(rqb, rk) * decay_ref[h]
            o = _dot(att, rv) + qdec_ref[:, h:h + 1] * _dot(rqb, s_h)
            S_ref[h] = cdec_ref[0:1, h:h + 1] * s_h + _dot_tn(rk * kdec_ref[:, h:h + 1], rv)
            hr = _head_norm(o) * grh_ref[:, lo:lo + HEAD_DIM] * _silu(rg)
            h_ref[pl.ds(r0, L), MIX + lo:MIX + lo + HEAD_DIM] = hr.astype(BF16)

        m_ref[0:1, :] = m_new
        return carry_unused

    lax.fori_loop(0, tt // L, chunk_body, 0)

    @pl.when(t == pl.num_programs(1) - 1)
    def _():
        conv_ref[...] = u_ref[UOFF + tt - (CONV_W - 1):UOFF + tt, COL_QK:COL_V]


def _mixer_prompt(x, w1, g_mix, bg, w_conv, b_conv, g_mh, g_rh, cos_f, sin_s,
                  decay, qdec, kdec, cdec, tt=512):
    B, T, _ = x.shape
    tile = lambda b, t: (b, t, 0)
    per_b3 = lambda b, t: (b, 0, 0)
    per_b4 = lambda b, t: (b, 0, 0, 0)
    in_specs = [
        pl.BlockSpec((None, tt, D_MODEL), tile),
        _const_spec((D_MODEL, IN_PAD)),
        _const_spec((1, D_MODEL)), _const_spec((1, HEAD_DIM)),
        _const_spec((CONV_W, 2 * MIX)), _const_spec((1, 2 * MIX)),
        _const_spec((1, MIX)), _const_spec((1, MIX)),
        pl.BlockSpec((tt, HEAD_DIM), lambda b, t: (t, 0)),
        pl.BlockSpec((tt, HEAD_DIM), lambda b, t: (t, 0)),
        _const_spec((HEADS, CHUNK, CHUNK)),
        _const_spec((CHUNK, HEAD_DIM)), _const_spec((CHUNK, HEAD_DIM)), _const_spec((8, HEAD_DIM)),
    ]
    out_specs = [
        pl.BlockSpec((None, tt, D_MODEL), tile),
        pl.BlockSpec((None, CONV_W - 1, 2 * MIX), per_b3),
        pl.BlockSpec((None, HEADS, HEAD_DIM, HEAD_DIM), per_b4),
        pl.BlockSpec((None, HEADS, HEAD_DIM), per_b3),
        pl.BlockSpec((None, 8, HEAD_DIM), per_b3),
        pl.BlockSpec((None, HEADS, HEAD_DIM, HEAD_DIM), per_b4),
    ]
    out_shape = [
        jax.ShapeDtypeStruct((B, T, D_MODEL), BF16),
        jax.ShapeDtypeStruct((B, CONV_W - 1, 2 * MIX), F32),
        jax.ShapeDtypeStruct((B, HEADS, HEAD_DIM, HEAD_DIM), F32),
        jax.ShapeDtypeStruct((B, HEADS, HEAD_DIM), F32),
        jax.ShapeDtypeStruct((B, 8, HEAD_DIM), F32),
        jax.ShapeDtypeStruct((B, HEADS, HEAD_DIM, HEAD_DIM), F32),
    ]
    return pl.pallas_call(
        functools.partial(_mixer_prompt_kernel, tt=tt),
        grid=(B, T // tt),
        in_specs=in_specs, out_specs=out_specs, out_shape=out_shape,
        scratch_shapes=[pltpu.VMEM((tt + UOFF, IN_PAD), F32)],
        compiler_params=_params("parallel", "arbitrary"),
    )(x, w1, g_mix, bg, w_conv, b_conv, g_mh, g_rh, cos_f, sin_s, decay, qdec, kdec, cdec)


def _softmax_rows(s):
    e = jnp.exp(s - jnp.max(s, axis=-1, keepdims=True))
    return e / jnp.sum(e, axis=-1, keepdims=True)


def _attn_prompt_kernel(x_ref, h_ref, k_ref, v_ref, wout_ref, wcq_ref, gx_ref, x1_ref, o_ref):
    x1 = x_ref[...] + jnp.dot(h_ref[...], wout_ref[...], preferred_element_type=F32)
    x1_ref[...] = x1
    q = jnp.dot(_rms(x1, gx_ref[...]).astype(BF16), wcq_ref[...],
                preferred_element_type=F32).astype(BF16)
    for h in range(X_HEADS):
        sl = slice(h * X_HEAD_DIM, (h + 1) * X_HEAD_DIM)
        s = _dot_nt(q[:, sl], k_ref[:, sl]) * (X_HEAD_DIM ** -0.5)
        o_ref[:, sl] = _dot(_softmax_rows(s), v_ref[:, sl]).astype(BF16)


def _attn_prompt(x, hcat, mk, mv, w_out, w_cq, g_x, tm=512):
    B, T, _ = x.shape
    tile = pl.BlockSpec((None, tm, D_MODEL), lambda b, t: (b, t, 0))
    kv = pl.BlockSpec((None, N_MEM, D_MODEL), lambda b, t: (b, 0, 0))
    return pl.pallas_call(
        _attn_prompt_kernel,
        grid=(B, T // tm),
        in_specs=[tile, tile, kv, kv, _const_spec((D_MODEL, D_MODEL)),
                  _const_spec((D_MODEL, D_MODEL)), _const_spec((1, D_MODEL))],
        out_specs=[tile, tile],
        out_shape=[jax.ShapeDtypeStruct((B, T, D_MODEL), F32),
                   jax.ShapeDtypeStruct((B, T, D_MODEL), BF16)],
        compiler_params=_params("parallel", "parallel"),
    )(x, hcat, mk, mv, w_out, w_cq, g_x)


FF_CHUNK = D_FF // 2


def _ffn_kernel(x1_ref, o_ref, wco_ref, wg_ref, wu_ref, wd_ref, gffn_ref, gfin_ref, y_ref):
    x2 = x1_ref[...] + _dot(o_ref[...], wco_ref[...])
    hf = _rms(x2, gffn_ref[...]).astype(BF16)
    acc = x2
    for c0 in range(0, D_FF, FF_CHUNK):
        gate = jnp.dot(hf, wg_ref[:, c0:c0 + FF_CHUNK], preferred_element_type=F32)
        up = jnp.dot(hf, wu_ref[:, c0:c0 + FF_CHUNK], preferred_element_type=F32)
        acc = acc + _dot(_silu(gate) * up, wd_ref[c0:c0 + FF_CHUNK, :])
    y_ref[...] = _rms(acc, gfin_ref[...])


def _ffn(x1, o, w_co, w_gate, w_up, w_down, g_ffn, g_final, tm):
    rows = x1.shape[0]
    row_spec = pl.BlockSpec((tm, D_MODEL), lambda i: (i, 0))
    return pl.pallas_call(
        _ffn_kernel,
        grid=(rows // tm,),
        in_specs=[row_spec, row_spec, _const_spec((D_MODEL, D_MODEL)),
                  _const_spec((D_MODEL, D_FF)), _const_spec((D_MODEL, D_FF)),
                  _const_spec((D_FF, D_MODEL)), _const_spec((1, D_MODEL)), _const_spec((1, D_MODEL))],
        out_specs=row_spec,
        out_shape=jax.ShapeDtypeStruct((rows, D_MODEL), F32),
        compiler_params=_params("parallel"),
    )(x1, o, w_co, w_gate, w_up, w_down, g_ffn, g_final)


def _inproj_kernel(x_ref, g_ref, w_ref, u_ref):
    u_ref[...] = jnp.dot(_rms(x_ref[...], g_ref[...]).astype(BF16), w_ref[...],
                         preferred_element_type=F32)


def _inproj_sample(x, g_mix, w1, nc=1408):
    rows = x.shape[0]
    return pl.pallas_call(
        _inproj_kernel,
        grid=(IN_PAD // nc,),
        in_specs=[_const_spec((rows, D_MODEL)), _const_spec((1, D_MODEL)),
                  pl.BlockSpec((D_MODEL, nc), lambda j: (0, j))],
        out_specs=pl.BlockSpec((rows, nc), lambda j: (0, j)),
        out_shape=jax.ShapeDtypeStruct((rows, IN_PAD), F32),
        compiler_params=_params("parallel"),
    )(x, g_mix, w1)


def _mixer_sample_kernel(u_ref, conv_ref, C_ref, n_ref, m_ref, S_ref, bg_ref, wconv_ref, bconv_ref,
                         gmh_ref, grh_ref, cos_ref, sin_ref, rtab_ref,
                         h_ref, convo_ref, Co_ref, no_ref, mo_ref, So_ref,
                         q_s, k_s, vws_s, qr_s, kr_s, qc_s, qs_s, carry_s, *, bb):
    k_scale = HEAD_DIM ** -0.5
    uqk = u_ref[:, COL_QK:COL_V]
    conv = (bconv_ref[...] + wconv_ref[0:1, :] * conv_ref[0] + wconv_ref[1:2, :] * conv_ref[1]
            + wconv_ref[2:3, :] * conv_ref[2] + wconv_ref[3:4, :] * uqk)
    convo_ref[0] = conv_ref[1]
    convo_ref[1] = conv_ref[2]
    convo_ref[2] = uqk
    qk_act = _silu(conv)
    q_s[...] = qk_act[:, 0:MIX]
    k_s[...] = qk_act[:, MIX:2 * MIX] * k_scale

    gates = u_ref[:, COL_G:IN_PAD] + bg_ref[...]
    ig = gates[:, 0:HEADS]
    lf = _log_sigmoid(gates[:, HEADS:2 * HEADS])
    inter = lf + m_ref[...]
    m_t = jnp.maximum(inter, ig)
    ws = jnp.exp(ig - m_t)
    w_in = jnp.exp(inter - m_t)
    mo_ref[...] = m_t
    carry_s[:, 0:HEADS] = w_in

    cos_f = cos_ref[...]
    sin_s = sin_ref[...]
    for h in range(HEADS):
        lo = h * HEAD_DIM
        hs = slice(lo, lo + HEAD_DIM)
        vws_s[:, hs] = u_ref[:, COL_V + lo:COL_V + lo + HEAD_DIM] * ws[:, h:h + 1]
        qr_s[:, hs] = _rope(u_ref[:, COL_RQ + lo:COL_RQ + lo + HEAD_DIM], cos_f, sin_s)
        kr_s[:, hs] = _rope(u_ref[:, COL_RK + lo:COL_RK + lo + HEAD_DIM], cos_f, sin_s) * k_scale

    for b in range(bb):
        row = slice(b, b + 1)
        for h in range(HEADS):
            lo = h * HEAD_DIM
            hs = slice(lo, lo + HEAD_DIM)
            c_h = C_ref[b, h]
            q8 = jnp.broadcast_to(q_s[row, hs], (8, HEAD_DIM))
            qc_s[row, hs] = _dot_nt(q8, c_h)[0:1]
            v_col = jnp.broadcast_to(vws_s[row, hs], (HEAD_DIM, HEAD_DIM)).T
            Co_ref[b, h] = carry_s[row, h:h + 1] * c_h + v_col * k_s[row, hs]
            s_h = S_ref[b, h]
            qr8 = jnp.broadcast_to(qr_s[row, hs], (8, HEAD_DIM))
            qs_s[row, hs] = _dot(qr8, s_h)[0:1]
            k_col = jnp.broadcast_to(kr_s[row, hs] * rtab_ref[1:2, h:h + 1], (HEAD_DIM, HEAD_DIM)).T
            So_ref[b, h] = (rtab_ref[2:3, h:h + 1] * s_h
                            + k_col * u_ref[row, COL_RV + lo:COL_RV + lo + HEAD_DIM])

    for h in range(HEADS):
        lo = h * HEAD_DIM
        hs = slice(lo, lo + HEAD_DIM)
        q_h, k_h, n_h = q_s[:, hs], k_s[:, hs], n_ref[:, hs]
        ws_h, w_in_h = ws[:, h:h + 1], w_in[:, h:h + 1]
        wts = ws_h * jnp.sum(q_h * k_h, axis=1, keepdims=True)
        num = wts * u_ref[:, COL_V + lo:COL_V + lo + HEAD_DIM] + w_in_h * qc_s[:, hs]
        den = wts + w_in_h * jnp.sum(n_h * q_h, axis=1, keepdims=True)
        hm = num / jnp.maximum(jnp.abs(den), jnp.exp(-m_t[:, h:h + 1]))
        mo = u_ref[:, COL_O + lo:COL_O + lo + HEAD_DIM]
        h_ref[:, hs] = _head_norm(hm) * gmh_ref[:, hs] * jax.nn.sigmoid(mo)
        no_ref[:, hs] = w_in_h * n_h + ws_h * k_h

        att = jnp.sum(qr_s[:, hs] * kr_s[:, hs], axis=1, keepdims=True) * rtab_ref[3:4, h:h + 1]
        o = (att * u_ref[:, COL_RV + lo:COL_RV + lo + HEAD_DIM]
             + rtab_ref[0:1, h:h + 1] * qs_s[:, hs])
        rg = u_ref[:, COL_RG + lo:COL_RG + lo + HEAD_DIM]
        h_ref[:, MIX + lo:MIX + lo + HEAD_DIM] = _head_norm(o) * grh_ref[:, hs] * _silu(rg)


def _mixer_sample(u, conv_t, C, n2, m, S, bg, w_conv, b_conv, g_mh, g_rh, cos_f, sin_s, rtab, bb=8):
    B = u.shape[0]
    rows = lambda cols: pl.BlockSpec((bb, cols), lambda i: (i, 0))
    mats = pl.BlockSpec((bb, HEADS, HEAD_DIM, HEAD_DIM), lambda i: (i, 0, 0, 0))
    convs = pl.BlockSpec((CONV_W - 1, bb, 2 * MIX), lambda i: (0, i, 0))
    in_specs = [rows(IN_PAD), convs, mats, rows(MIX), rows(HEADS), mats,
                _const_spec((1, HEAD_DIM)), _const_spec((CONV_W, 2 * MIX)), _const_spec((1, 2 * MIX)),
                _const_spec((1, MIX)), _const_spec((1, MIX)),
                _const_spec((1, HEAD_DIM)), _const_spec((1, HEAD_DIM)), _const_spec((8, HEAD_DIM))]
    out_specs = [rows(D_MODEL), convs, mats, rows(MIX), rows(HEADS), mats]
    out_shape = [jax.ShapeDtypeStruct((B, D_MODEL), F32),
                 jax.ShapeDtypeStruct((CONV_W - 1, B, 2 * MIX), F32),
                 jax.ShapeDtypeStruct((B, HEADS, HEAD_DIM, HEAD_DIM), F32),
                 jax.ShapeDtypeStruct((B, MIX), F32),
                 jax.ShapeDtypeStruct((B, HEADS), F32),
                 jax.ShapeDtypeStruct((B, HEADS, HEAD_DIM, HEAD_DIM), F32)]
    scratch = [pltpu.VMEM((bb, MIX), F32) for _ in range(7)] + [pltpu.VMEM((bb, HEAD_DIM), F32)]
    return pl.pallas_call(
        functools.partial(_mixer_sample_kernel, bb=bb),
        grid=(B // bb,),
        in_specs=in_specs, out_specs=out_specs, out_shape=out_shape,
        scratch_shapes=scratch,
        compiler_params=_params("parallel"),
    )(u, conv_t, C, n2, m, S, bg, w_conv, b_conv, g_mh, g_rh, cos_f, sin_s, rtab)


def _outq_sample_kernel(x_ref, h_ref, wout_ref, wcq_ref, gx_ref, x1_ref, q_ref):
    x1 = x_ref[...] + _dot(h_ref[...], wout_ref[...])
    x1_ref[...] = x1
    q_ref[...] = _dot(_rms(x1, gx_ref[...]), wcq_ref[...])


def _outq_sample(x, hcat, w_out, w_cq, g_x):
    rows = x.shape[0]
    full = _const_spec((rows, D_MODEL))
    return pl.pallas_call(
        _outq_sample_kernel,
        grid=(1,),
        in_specs=[full, full, _const_spec((D_MODEL, D_MODEL)), _const_spec((D_MODEL, D_MODEL)),
                  _const_spec((1, D_MODEL))],
        out_specs=[pl.BlockSpec((rows, D_MODEL), lambda i: (0, 0))] * 2,
        out_shape=[jax.ShapeDtypeStruct((rows, D_MODEL), F32)] * 2,
        compiler_params=_params("arbitrary"),
    )(x, hcat, w_out, w_cq, g_x)


KV_ROWS = N_MEM * 2 * X_HEADS
QROWS = 2 * X_HEADS


def _attn_sample_kernel(q_ref, k_ref, v_ref, o_ref, *, ba):
    r_id = lax.broadcasted_iota(jnp.int32, (QROWS, KV_ROWS), 0)
    n_id = lax.broadcasted_iota(jnp.int32, (QROWS, KV_ROWS), 1)
    own = (n_id & 7) == (r_id >> 1) + 4 * (r_id & 1)
    low_half = (lax.broadcasted_iota(jnp.int32, (1, KV_ROWS), 1) & 4) == 0
    for j in range(ba):
        z = _dot_nt(q_ref[j], k_ref[j])
        zc = jnp.sum(jnp.where(own, z, 0.0), axis=0, keepdims=True)
        other = jnp.where(low_half, pltpu.roll(zc, KV_ROWS - 4, 1), pltpu.roll(zc, 4, 1))
        s = (zc + other) * (X_HEAD_DIM ** -0.5)
        p = _softmax_rows(jnp.where(own, s, -jnp.inf))
        o_ref[j] = _dot(p, v_ref[j])


def _attn_sample(q, ck, cv, ba=4):
    B = q.shape[0]
    kv = pl.BlockSpec((ba, KV_ROWS, 128), lambda i: (i, 0, 0))
    rows = pl.BlockSpec((ba, QROWS, 128), lambda i: (i, 0, 0))
    return pl.pallas_call(
        functools.partial(_attn_sample_kernel, ba=ba),
        grid=(B // ba,),
        in_specs=[rows, kv, kv],
        out_specs=rows,
        out_shape=jax.ShapeDtypeStruct((B, QROWS, 128), F32),
        compiler_params=_params("parallel"),
    )(q.reshape(B, QROWS, 128), ck, cv).reshape(B, D_MODEL)


def _kv_rows(cache):
    B = cache.shape[0]
    c5 = cache.reshape(B, N_MEM, X_HEADS, 2, 128)
    return jnp.transpose(c5, (0, 1, 3, 2, 4)).reshape(B, KV_ROWS, 128)


def _rope_tables(pos):
    half = HEAD_DIM // 2
    inv = ROPE_THETA ** (-jnp.arange(half, dtype=F32) / half)
    ang = pos[:, None] * inv[None, :]
    cos, sin = jnp.cos(ang), jnp.sin(ang)
    return jnp.concatenate([cos, cos], axis=-1), jnp.concatenate([-sin, sin], axis=-1)


def _retention_tables(L):
    lg = jnp.log1p(-jnp.exp2(-5.0 - jnp.arange(HEADS, dtype=F32)))
    t = jnp.arange(L, dtype=F32)
    diff = t[:, None] - t[None, :]
    decay = jnp.where(diff >= 0, jnp.exp(lg[:, None, None] * jnp.maximum(diff, 0.0)), 0.0)
    q_dec = jnp.exp(lg[:, None] * (t + 1.0))
    k_dec = jnp.exp(lg[:, None] * (L - 1.0 - t))
    chunk_dec = jnp.exp(lg * L)
    return decay, q_dec, k_dec, chunk_dec


def _lane_pad(a):
    return jnp.pad(a, ((0, 0), (0, HEAD_DIM - a.shape[1])))


def kernel(x_prompt, x_sample, cache_mem_k, cache_mem_v, state_mlstm_conv, state_mlstm_C, state_mlstm_n, state_mlstm_m, state_ret_S, mem_prompt, w_in, b_gate, w_conv, b_conv, g_mix, g_mhead, g_rhead, w_out, g_xattn, g_mem, w_ck, w_cv, w_cq, w_co, g_ffn, w_gate, w_up, w_down, g_final):
    Bp, Tp, _ = x_prompt.shape
    Bs = x_sample.shape[0]
    l = 0
    n_m = 4 * MIX
    wi = w_in[l]
    w1 = jnp.concatenate(
        [wi[:, :n_m], wi[:, n_m + 2 * HEADS:], wi[:, n_m:n_m + 2 * HEADS],
         jnp.zeros((D_MODEL, IN_PAD - COL_G - 2 * HEADS), F32)], axis=1).astype(BF16)
    bg = jnp.pad(b_gate[l], (0, HEAD_DIM - 2 * HEADS)).reshape(1, HEAD_DIM)
    row = lambda a: a.reshape(1, -1)
    bf = lambda a: a.astype(BF16)
    g_mix_r, g_mh_r, g_rh_r = row(g_mix[l]), row(g_mhead[l]), row(g_rhead[l])
    g_x_r, g_mem_r, g_ffn_r, g_fin_r = row(g_xattn[l]), row(g_mem[l]), row(g_ffn[l]), row(g_final)
    b_conv_r = row(b_conv[l])
    w_out_b, w_cq_b, w_co_b = bf(w_out[l]), bf(w_cq[l]), bf(w_co[l])
    w_gate_b, w_up_b, w_down_b = bf(w_gate[l]), bf(w_up[l]), bf(w_down[l])

    mk, mv = _memkv(mem_prompt.reshape(Bp * N_MEM, D_MODEL), g_mem_r, bf(w_ck[l]), bf(w_cv[l]))
    cos_p, sin_p = _rope_tables(jnp.arange(Tp, dtype=F32))
    decay, q_dec, k_dec, chunk_dec = _retention_tables(CHUNK)
    cdec = jnp.pad(_lane_pad(chunk_dec[None, :]), ((0, 7), (0, 0)))
    hcat_p, conv_p, C_p, n_p, m_p, S_p = _mixer_prompt(
        x_prompt, w1, g_mix_r, bg, w_conv[l], b_conv_r, g_mh_r, g_rh_r, cos_p, sin_p,
        decay, _lane_pad(q_dec.T), _lane_pad(k_dec.T), cdec)
    x1_p, o_p = _attn_prompt(x_prompt, hcat_p, mk.reshape(Bp, N_MEM, D_MODEL),
                             mv.reshape(Bp, N_MEM, D_MODEL), w_out_b, w_cq_b, g_x_r)
    y_p = _ffn(x1_p.reshape(Bp * Tp, D_MODEL), o_p.reshape(Bp * Tp, D_MODEL),
               w_co_b, w_gate_b, w_up_b, w_down_b, g_ffn_r, g_fin_r, tm=512)

    xs = x_sample.reshape(Bs, D_MODEL)
    u_s = _inproj_sample(xs, g_mix_r, w1)
    cos_s, sin_s = _rope_tables(PAST_LEN + jnp.arange(1, dtype=F32))
    decay1, q_dec1, k_dec1, chunk_dec1 = _retention_tables(1)
    rtab = jnp.pad(_lane_pad(jnp.stack([q_dec1[:, 0], k_dec1[:, 0], chunk_dec1, decay1[:, 0, 0]])),
                   ((0, 4), (0, 0)))
    hcat_s, conv_s, C_s, n_s, m_s, S_s = _mixer_sample(
        u_s, jnp.transpose(state_mlstm_conv[l], (1, 0, 2)), state_mlstm_C[l],
        state_mlstm_n[l].reshape(Bs, MIX), state_mlstm_m[l], state_ret_S[l],
        bg, w_conv[l], b_conv_r, g_mh_r, g_rh_r, cos_s, sin_s, rtab)
    x1_s, q_s = _outq_sample(xs, hcat_s, w_out_b, w_cq_b, g_x_r)
    o_s = _attn_sample(q_s, _kv_rows(cache_mem_k[l]), _kv_rows(cache_mem_v[l]))
    y_s = _ffn(x1_s, o_s, w_co_b, w_gate_b, w_up_b, w_down_b, g_ffn_r, g_fin_r, tm=Bs)

    kv_shape = (1, Bp, N_MEM, X_HEADS, X_HEAD_DIM)
    return (y_p.reshape(Bp, Tp, D_MODEL), y_s.reshape(Bs, 1, D_MODEL),
            mk.reshape(kv_shape), mv.reshape(kv_shape),
            conv_p[None], C_p[None], n_p[None], m_p[None, :, 0, :HEADS], S_p[None],
            jnp.transpose(conv_s, (1, 0, 2))[None], C_s[None],
            n_s.reshape(1, Bs, HEADS, HEAD_DIM), m_s[None], S_s[None])
```

```python
import functools

import jax
import jax.numpy as jnp
from jax import lax
from jax.experimental import pallas as pl
from jax.experimental.pallas import tpu as pltpu

F32 = jnp.float32
BF16 = jnp.bfloat16

D_MODEL = 1024
HEADS = 4
HEAD_DIM = 128
MIX = HEADS * HEAD_DIM
CONV_W = 4
CHUNK = 128
N_MEM = 256
X_HEADS = 4
X_HEAD_DIM = 256
D_FF = 2816
ROPE_THETA = 10000.0
EPS = 1e-6
PAST_LEN = 16384

COL_QK, COL_V, COL_O = 0, 1024, 1536
COL_RQ, COL_RK, COL_RV, COL_RG = 2048, 2560, 3072, 3584
COL_G = 4096
IN_PAD = 4224

N_QK, N_RQ, N_RK, N_COLS = 0, 1024, 1536, 2048
T_V, T_O, T_RV, T_RG, T_ROWS = 0, 512, 1024, 1536, 2048
GT_ROWS = 16
UOFF = 8

V7X_VMEM_LIMIT = 56 * 1024 * 1024

NT_DIMS = (((1,), (1,)), ((), ()))


def _dot(a, b):
    return jnp.dot(a.astype(BF16), b.astype(BF16), preferred_element_type=F32)


def _dot_nt(a, b):
    return lax.dot_general(a.astype(BF16), b.astype(BF16), NT_DIMS, preferred_element_type=F32)


def _rms(x, g):
    return x * lax.rsqrt(jnp.mean(x * x, axis=-1, keepdims=True) + EPS) * g


def _head_norm(h):
    return h * lax.rsqrt(jnp.mean(h * h, axis=-1, keepdims=True) + EPS)


def _silu(x):
    return x * jax.nn.sigmoid(x)


def _log_sigmoid(x):
    return jnp.minimum(x, 0.0) - jnp.log1p(jnp.exp(-jnp.abs(x)))


def _rope(x, cos_full, sin_signed):
    return x * cos_full + pltpu.roll(x, HEAD_DIM // 2, 1) * sin_signed


def _const_spec(shape):
    zeros = (0,) * len(shape)
    return pl.BlockSpec(shape, lambda *_: zeros, pipeline_mode=pl.Buffered(1))


def _params(*sem):
    return pltpu.CompilerParams(dimension_semantics=sem, vmem_limit_bytes=V7X_VMEM_LIMIT)


def _memkv_kernel(mem_ref, g_ref, wk_ref, wv_ref, k_ref, v_ref):
    mn = _rms(mem_ref[...], g_ref[...]).astype(BF16)
    k_ref[...] = jnp.dot(mn, wk_ref[...], preferred_element_type=F32)
    v_ref[...] = jnp.dot(mn, wv_ref[...], preferred_element_type=F32)


def _memkv(mem, g_mem, wk, wv, tm=512):
    rows = mem.shape[0]
    row_spec = pl.BlockSpec((tm, D_MODEL), lambda i: (i, 0))
    return pl.pallas_call(
        _memkv_kernel,
        grid=(rows // tm,),
        in_specs=[row_spec, _const_spec((1, D_MODEL)),
                  _const_spec((D_MODEL, D_MODEL)), _const_spec((D_MODEL, D_MODEL))],
        out_specs=[row_spec, row_spec],
        out_shape=[jax.ShapeDtypeStruct((rows, D_MODEL), F32)] * 2,
        compiler_params=_params("parallel"),
    )(mem, g_mem, wk, wv)


def _mixer_prompt_kernel(x_ref, wn_ref, wt_ref, wgt_ref, gmix_ref, bgt_ref, wconv_ref, bconv_ref,
                         gmh_ref, grh_ref, cos_ref, sin_ref, decay_ref, qdec_ref, kdec_ref, cdec_ref,
                         h_ref, conv_ref, C_ref, n_ref, m_ref, S_ref,
                         un_ref, qk_ref, ut_ref, gt_ref, st_ref, *, tt):
    L = CHUNK
    t = pl.program_id(1)
    chunks = range(tt // L)

    @pl.when(t == 0)
    def _():
        C_ref[...] = jnp.zeros_like(C_ref)
        st_ref[...] = jnp.zeros_like(st_ref)
        n_ref[...] = jnp.zeros_like(n_ref)
        m_ref[...] = jnp.zeros_like(m_ref)
        un_ref[0:UOFF, :] = jnp.zeros((UOFF, N_COLS), F32)

    @pl.when(t > 0)
    def _():
        un_ref[0:UOFF, N_QK:N_RQ] = un_ref[tt:tt + UOFF, N_QK:N_RQ]

    k_scale = HEAD_DIM ** -0.5
    xn = _rms(x_ref[...], gmix_ref[...]).astype(BF16)

    def project(c0, c1):
        un_ref[UOFF:UOFF + tt, c0:c1] = jnp.dot(xn, wn_ref[:, c0:c1], preferred_element_type=F32)

    def project_t(r0, act):
        res = lax.dot_general(wt_ref[r0:r0 + MIX, :], xn, NT_DIMS, preferred_element_type=F32)
        for c in chunks:
            ut_ref[c, r0:r0 + MIX, :] = act(res[:, c * L:(c + 1) * L])

    project(N_QK, N_RQ)
    project(N_RQ, N_COLS)
    for col in range(N_QK, N_RQ, HEAD_DIM):
        scale = k_scale if col >= N_QK + MIX else 1.0
        for r in range(0, tt, L):
            win = un_ref[r:r + L + UOFF, col:col + HEAD_DIM]
            acc = bconv_ref[:, col:col + HEAD_DIM]
            for j in range(CONV_W):
                s0 = UOFF - (CONV_W - 1) + j
                acc = acc + win[s0:s0 + L] * wconv_ref[j:j + 1, col:col + HEAD_DIM]
            act = _silu(acc)
            qk_ref[r:r + L, col:col + HEAD_DIM] = act if scale == 1.0 else act * scale
    project_t(T_V, lambda a: a)
    for col in range(N_RQ, N_COLS, HEAD_DIM):
        for r in range(0, tt, L):
            rb = slice(UOFF + r, UOFF + r + L)
            roped = _rope(un_ref[rb, col:col + HEAD_DIM], cos_ref[r:r + L, :], sin_ref[r:r + L, :])
            un_ref[rb, col:col + HEAD_DIM] = roped * k_scale if col >= N_RK else roped
    project_t(T_O, jax.nn.sigmoid)
    project_t(T_RV, lambda a: a)
    project_t(T_RG, _silu)
    gates_t = lax.dot_general(wgt_ref[...], xn, NT_DIMS, preferred_element_type=F32)
    for c in chunks:
        gt_ref[c] = gates_t[:, c * L:(c + 1) * L]

    src_id = lax.broadcasted_iota(jnp.int32, (L, L), 0)
    tgt_id = lax.broadcasted_iota(jnp.int32, (L, L), 1)
    causal = src_id <= tgt_id
    triu_bf = jnp.where(causal, 1.0, 0.0).astype(BF16)
    heads = range(HEADS)
    hcol = lambda base, h: slice(base + h * HEAD_DIM, base + (h + 1) * HEAD_DIM)

    def chunk_body(c, carry_unused):
        r0 = pl.multiple_of(c * L, L)
        rows = pl.ds(UOFF + r0, L)
        trows = pl.ds(r0, L)
        g_t = gt_ref[c] + bgt_ref[...]
        ig = g_t[0:8]
        lf = _log_sigmoid(g_t[8:16])
        lf_hi = lf.astype(BF16)
        r1 = lf - lf_hi.astype(F32)
        lf_mid = r1.astype(BF16)
        lf_lo = (r1 - lf_mid.astype(F32)).astype(BF16)
        bc = (jnp.dot(lf_hi, triu_bf, preferred_element_type=F32)
              + jnp.dot(lf_mid, triu_bf, preferred_element_type=F32)
              + jnp.dot(lf_lo, triu_bf, preferred_element_type=F32))
        m_prev = m_ref[...]
        inter = bc + m_prev
        b_last = bc[:, L - 1:L]
        g_w = b_last - bc + ig
        m_new = jnp.maximum(b_last + m_prev, jnp.max(g_w, axis=1, keepdims=True))
        ws = jnp.exp(g_w - m_new)
        carry = jnp.exp(b_last + m_prev - m_new)
        a_n = jnp.concatenate([ig - bc, jnp.zeros((L - 8, L), F32)], axis=0).T

        q = [qk_ref[trows, hcol(0, h)] for h in heads]
        kb = [qk_ref[trows, hcol(MIX, h)].astype(BF16) for h in heads]
        qb = [a.astype(BF16) for a in q]
        rqb = [un_ref[rows, hcol(N_RQ, h)].astype(BF16) for h in heads]
        rk = [un_ref[rows, hcol(N_RK, h)] for h in heads]
        v_t = [ut_ref[c, hcol(T_V, h), :] for h in heads]
        rvb = [ut_ref[c, hcol(T_RV, h), :].astype(BF16) for h in heads]
        c_old = [C_ref[h] for h in heads]
        s_old = [st_ref[h] for h in heads]
        n_old = [n_ref[h:h + 1, :] for h in heads]
        qk = [_dot_nt(kb[h], qb[h]) for h in heads]
        att = [_dot_nt(rk[h], rqb[h]) for h in heads]
        c_q = [_dot_nt(c_old[h], qb[h]) for h in heads]
        s_q = [_dot_nt(s_old[h], rqb[h]) for h in heads]
        n_q = [_dot_nt(jnp.broadcast_to(n_old[h], (8, HEAD_DIM)), qb[h])[0:1] for h in heads]
        d_c = [_dot(v_t[h] * ws[h:h + 1, :], kb[h]) for h in heads]
        d_n = [_dot(jnp.broadcast_to(ws[h:h + 1, :], (8, L)), kb[h])[0:1] for h in heads]
        d_s = [_dot(rvb[h], rk[h] * kdec_ref[h]) for h in heads]
        m_t, w_in, wts = [], [], []
        for h in heads:
            dmat = a_n[:, h:h + 1] + bc[h:h + 1, :]
            dmat = jnp.where(causal, dmat, -jnp.inf)
            m_t.append(jnp.maximum(inter[h:h + 1, :], jnp.max(dmat, axis=0, keepdims=True)))
            wts.append(jnp.exp(dmat - m_t[h]) * qk[h])
            w_in.append(jnp.exp(inter[h:h + 1, :] - m_t[h]))
        att_w = [att[h] * decay_ref[h] for h in heads]
        v_p = [_dot(v_t[h], wts[h]) for h in heads]
        v_a = [_dot(rvb[h], att_w[h]) for h in heads]
        for h in heads:
            num = v_p[h] + w_in[h] * c_q[h]
            den = jnp.sum(wts[h], axis=0, keepdims=True) + w_in[h] * n_q[h]
            hm = num / jnp.maximum(jnp.abs(den), jnp.exp(-m_t[h]))
            hm = hm * lax.rsqrt(jnp.mean(hm * hm, axis=0, keepdims=True) + EPS)
            hm = hm * gmh_ref[hcol(0, h), :] * ut_ref[c, hcol(T_O, h), :]
            h_ref[trows, hcol(0, h)] = hm.T.astype(BF16)
            carry_h = carry[h:h + 1, :]
            C_ref[h] = carry_h * c_old[h] + d_c[h]
            n_ref[h:h + 1, :] = carry_h * n_old[h] + d_n[h]
            o = v_a[h] + qdec_ref[h:h + 1, :] * s_q[h]
            st_ref[h] = cdec_ref[h:h + 1, :] * s_old[h] + d_s[h]
            hr = o * lax.rsqrt(jnp.mean(o * o, axis=0, keepdims=True) + EPS)
            hr = hr * grh_ref[hcol(0, h), :] * ut_ref[c, hcol(T_RG, h), :]
            h_ref[trows, hcol(MIX, h)] = hr.T.astype(BF16)

        m_ref[...] = m_new
        return carry_unused

    lax.fori_loop(0, tt // L, chunk_body, 0)

    @pl.when(t == pl.num_programs(1) - 1)
    def _():
        conv_ref[...] = un_ref[UOFF + tt - (CONV_W - 1):UOFF + tt, N_QK:N_RQ]
        for h in heads:
            S_ref[h] = st_ref[h].T


def _mixer_prompt(x, wn, wt, wgt, g_mix, bgt, w_conv, b_conv, gmh_cols, grh_cols, cos_f, sin_s,
                  decay_t, qdec_rows, kdec_cols, cdec_rows, tt=512):
    B, T, _ = x.shape
    tile = lambda b, t: (b, t, 0)
    per_b3 = lambda b, t: (b, 0, 0)
    per_b4 = lambda b, t: (b, 0, 0, 0)
    in_specs = [
        pl.BlockSpec((None, tt, D_MODEL), tile),
        _const_spec((D_MODEL, N_COLS)), _const_spec((T_ROWS, D_MODEL)), _const_spec((GT_ROWS, D_MODEL)),
        _const_spec((1, D_MODEL)), _const_spec((GT_ROWS, CHUNK)),
        _const_spec((CONV_W, 2 * MIX)), _const_spec((1, 2 * MIX)),
        _const_spec((MIX, CHUNK)), _const_spec((MIX, CHUNK)),
        pl.BlockSpec((tt, HEAD_DIM), lambda b, t: (t, 0)),
        pl.BlockSpec((tt, HEAD_DIM), lambda b, t: (t, 0)),
        _const_spec((HEADS, CHUNK, CHUNK)),
        _const_spec((8, CHUNK)), _const_spec((HEADS, CHUNK, HEAD_DIM)), _const_spec((8, HEAD_DIM)),
    ]
    out_specs = [
        pl.BlockSpec((None, tt, D_MODEL), tile),
        pl.BlockSpec((None, CONV_W - 1, 2 * MIX), per_b3),
        pl.BlockSpec((None, HEADS, HEAD_DIM, HEAD_DIM), per_b4),
        pl.BlockSpec((None, HEADS, HEAD_DIM), per_b3),
        pl.BlockSpec((None, 8, CHUNK), per_b3),
        pl.BlockSpec((None, HEADS, HEAD_DIM, HEAD_DIM), per_b4),
    ]
    out_shape = [
        jax.ShapeDtypeStruct((B, T, D_MODEL), BF16),
        jax.ShapeDtypeStruct((B, CONV_W - 1, 2 * MIX), F32),
        jax.ShapeDtypeStruct((B, HEADS, HEAD_DIM, HEAD_DIM), F32),
        jax.ShapeDtypeStruct((B, HEADS, HEAD_DIM), F32),
        jax.ShapeDtypeStruct((B, 8, CHUNK), F32),
        jax.ShapeDtypeStruct((B, HEADS, HEAD_DIM, HEAD_DIM), F32),
    ]
    n_chunks = tt // CHUNK
    scratch = [
        pltpu.VMEM((tt + UOFF, N_COLS), F32),
        pltpu.VMEM((tt, 2 * MIX), F32),
        pltpu.VMEM((n_chunks, T_ROWS, CHUNK), F32),
        pltpu.VMEM((n_chunks, GT_ROWS, CHUNK), F32),
        pltpu.VMEM((HEADS, HEAD_DIM, HEAD_DIM), F32),
    ]
    return pl.pallas_call(
        functools.partial(_mixer_prompt_kernel, tt=tt),
        grid=(B, T // tt),
        in_specs=in_specs, out_specs=out_specs, out_shape=out_shape,
        scratch_shapes=scratch,
        compiler_params=_params("parallel", "arbitrary"),
    )(x, wn, wt, wgt, g_mix, bgt, w_conv, b_conv, gmh_cols, grh_cols, cos_f, sin_s,
      decay_t, qdec_rows, kdec_cols, cdec_rows)


def _softmax_rows(s):
    e = jnp.exp(s - jnp.max(s, axis=-1, keepdims=True))
    return e / jnp.sum(e, axis=-1, keepdims=True)


def _attn_prompt_kernel(x_ref, h_ref, k_ref, v_ref, wout_ref, wcq_ref, gx_ref, x1_ref, o_ref):
    x1 = x_ref[...] + jnp.dot(h_ref[...], wout_ref[...], preferred_element_type=F32)
    x1_ref[...] = x1
    q = jnp.dot(_rms(x1, gx_ref[...]).astype(BF16), wcq_ref[...],
                preferred_element_type=F32).astype(BF16)
    for h in range(X_HEADS):
        sl = slice(h * X_HEAD_DIM, (h + 1) * X_HEAD_DIM)
        s = _dot_nt(q[:, sl], k_ref[:, sl]) * (X_HEAD_DIM ** -0.5)
        o_ref[:, sl] = _dot(_softmax_rows(s), v_ref[:, sl]).astype(BF16)


def _attn_prompt(x, hcat, mk, mv, w_out, w_cq, g_x, tm=512):
    B, T, _ = x.shape
    tile = pl.BlockSpec((None, tm, D_MODEL), lambda b, t: (b, t, 0))
    kv = pl.BlockSpec((None, N_MEM, D_MODEL), lambda b, t: (b, 0, 0))
    return pl.pallas_call(
        _attn_prompt_kernel,
        grid=(B, T // tm),
        in_specs=[tile, tile, kv, kv, _const_spec((D_MODEL, D_MODEL)),
                  _const_spec((D_MODEL, D_MODEL)), _const_spec((1, D_MODEL))],
        out_specs=[tile, tile],
        out_shape=[jax.ShapeDtypeStruct((B, T, D_MODEL), F32),
                   jax.ShapeDtypeStruct((B, T, D_MODEL), BF16)],
        compiler_params=_params("parallel", "parallel"),
    )(x, hcat, mk, mv, w_out, w_cq, g_x)


FF_CHUNK = D_FF // 2


def _ffn_kernel(x1_ref, o_ref, wco_ref, wg_ref, wu_ref, wd_ref, gffn_ref, gfin_ref, y_ref):
    x2 = x1_ref[...] + _dot(o_ref[...], wco_ref[...])
    hf = _rms(x2, gffn_ref[...]).astype(BF16)
    acc = x2
    for c0 in range(0, D_FF, FF_CHUNK):
        gate = jnp.dot(hf, wg_ref[:, c0:c0 + FF_CHUNK], preferred_element_type=F32)
        up = jnp.dot(hf, wu_ref[:, c0:c0 + FF_CHUNK], preferred_element_type=F32)
        acc = acc + _dot(_silu(gate) * up, wd_ref[c0:c0 + FF_CHUNK, :])
    y_ref[...] = _rms(acc, gfin_ref[...])


def _ffn(x1, o, w_co, w_gate, w_up, w_down, g_ffn, g_final, tm):
    rows = x1.shape[0]
    row_spec = pl.BlockSpec((tm, D_MODEL), lambda i: (i, 0))
    return pl.pallas_call(
        _ffn_kernel,
        grid=(rows // tm,),
        in_specs=[row_spec, row_spec, _const_spec((D_MODEL, D_MODEL)),
                  _const_spec((D_MODEL, D_FF)), _const_spec((D_MODEL, D_FF)),
                  _const_spec((D_FF, D_MODEL)), _const_spec((1, D_MODEL)), _const_spec((1, D_MODEL))],
        out_specs=row_spec,
        out_shape=jax.ShapeDtypeStruct((rows, D_MODEL), F32),
        compiler_params=_params("parallel"),
    )(x1, o, w_co, w_gate, w_up, w_down, g_ffn, g_final)


def _inproj_kernel(x_ref, g_ref, w_ref, u_ref):
    u_ref[...] = jnp.dot(_rms(x_ref[...], g_ref[...]).astype(BF16), w_ref[...],
                         preferred_element_type=F32)


def _inproj_sample(x, g_mix, w1, nc=1408):
    rows = x.shape[0]
    return pl.pallas_call(
        _inproj_kernel,
        grid=(IN_PAD // nc,),
        in_specs=[_const_spec((rows, D_MODEL)), _const_spec((1, D_MODEL)),
                  pl.BlockSpec((D_MODEL, nc), lambda j: (0, j))],
        out_specs=pl.BlockSpec((rows, nc), lambda j: (0, j)),
        out_shape=jax.ShapeDtypeStruct((rows, IN_PAD), F32),
        compiler_params=_params("parallel"),
    )(x, g_mix, w1)


def _mixer_sample_kernel(u_ref, conv_ref, C_ref, n_ref, m_ref, S_ref, bg_ref, wconv_ref, bconv_ref,
                         gmh_ref, grh_ref, cos_ref, sin_ref, rtab_ref,
                         h_ref, convo_ref, Co_ref, no_ref, mo_ref, So_ref,
                         q_s, k_s, vws_s, qr_s, kr_s, qc_s, qs_s, carry_s, *, bb):
    k_scale = HEAD_DIM ** -0.5
    uqk = u_ref[:, COL_QK:COL_V]
    conv = (bconv_ref[...] + wconv_ref[0:1, :] * conv_ref[0] + wconv_ref[1:2, :] * conv_ref[1]
            + wconv_ref[2:3, :] * conv_ref[2] + wconv_ref[3:4, :] * uqk)
    convo_ref[0] = conv_ref[1]
    convo_ref[1] = conv_ref[2]
    convo_ref[2] = uqk
    qk_act = _silu(conv)
    q_s[...] = qk_act[:, 0:MIX]
    k_s[...] = qk_act[:, MIX:2 * MIX] * k_scale

    gates = u_ref[:, COL_G:IN_PAD] + bg_ref[...]
    ig = gates[:, 0:HEADS]
    lf = _log_sigmoid(gates[:, HEADS:2 * HEADS])
    inter = lf + m_ref[...]
    m_t = jnp.maximum(inter, ig)
    ws = jnp.exp(ig - m_t)
    w_in = jnp.exp(inter - m_t)
    mo_ref[...] = m_t
    carry_s[:, 0:HEADS] = w_in

    cos_f = cos_ref[...]
    sin_s = sin_ref[...]
    for h in range(HEADS):
        lo = h * HEAD_DIM
        hs = slice(lo, lo + HEAD_DIM)
        vws_s[:, hs] = u_ref[:, COL_V + lo:COL_V + lo + HEAD_DIM] * ws[:, h:h + 1]
        qr_s[:, hs] = _rope(u_ref[:, COL_RQ + lo:COL_RQ + lo + HEAD_DIM], cos_f, sin_s)
        kr_s[:, hs] = _rope(u_ref[:, COL_RK + lo:COL_RK + lo + HEAD_DIM], cos_f, sin_s) * k_scale

    for b in range(bb):
        row = slice(b, b + 1)
        for h in range(HEADS):
            lo = h * HEAD_DIM
            hs = slice(lo, lo + HEAD_DIM)
            c_h = C_ref[b, h]
            q8 = jnp.broadcast_to(q_s[row, hs], (8, HEAD_DIM))
            qc_s[row, hs] = _dot_nt(q8, c_h)[0:1]
            v_col = jnp.broadcast_to(vws_s[row, hs], (HEAD_DIM, HEAD_DIM)).T
            Co_ref[b, h] = carry_s[row, h:h + 1] * c_h + v_col * k_s[row, hs]
            s_h = S_ref[b, h]
            qr8 = jnp.broadcast_to(qr_s[row, hs], (8, HEAD_DIM))
            qs_s[row, hs] = _dot(qr8, s_h)[0:1]
            k_col = jnp.broadcast_to(kr_s[row, hs] * rtab_ref[1:2, h:h + 1], (HEAD_DIM, HEAD_DIM)).T
            So_ref[b, h] = (rtab_ref[2:3, h:h + 1] * s_h
                            + k_col * u_ref[row, COL_RV + lo:COL_RV + lo + HEAD_DIM])

    for h in range(HEADS):
        lo = h * HEAD_DIM
        hs = slice(lo, lo + HEAD_DIM)
        q_h, k_h, n_h = q_s[:, hs], k_s[:, hs], n_ref[:, hs]
        ws_h, w_in_h = ws[:, h:h + 1], w_in[:, h:h + 1]
        wts = ws_h * jnp.sum(q_h * k_h, axis=1, keepdims=True)
        num = wts * u_ref[:, COL_V + lo:COL_V + lo + HEAD_DIM] + w_in_h * qc_s[:, hs]
        den = wts + w_in_h * jnp.sum(n_h * q_h, axis=1, keepdims=True)
        hm = num / jnp.maximum(jnp.abs(den), jnp.exp(-m_t[:, h:h + 1]))
        mo = u_ref[:, COL_O + lo:COL_O + lo + HEAD_DIM]
        h_ref[:, hs] = _head_norm(hm) * gmh_ref[:, hs] * jax.nn.sigmoid(mo)
        no_ref[:, hs] = w_in_h * n_h + ws_h * k_h

        att = jnp.sum(qr_s[:, hs] * kr_s[:, hs], axis=1, keepdims=True) * rtab_ref[3:4, h:h + 1]
        o = (att * u_ref[:, COL_RV + lo:COL_RV + lo + HEAD_DIM]
             + rtab_ref[0:1, h:h + 1] * qs_s[:, hs])
        rg = u_ref[:, COL_RG + lo:COL_RG + lo + HEAD_DIM]
        h_ref[:, MIX + lo:MIX + lo + HEAD_DIM] = _head_norm(o) * grh_ref[:, hs] * _silu(rg)


def _mixer_sample(u, conv_t, C, n2, m, S, bg, w_conv, b_conv, g_mh, g_rh, cos_f, sin_s, rtab, bb=8):
    B = u.shape[0]
    rows = lambda cols: pl.BlockSpec((bb, cols), lambda i: (i, 0))
    mats = pl.BlockSpec((bb, HEADS, HEAD_DIM, HEAD_DIM), lambda i: (i, 0, 0, 0))
    convs = pl.BlockSpec((CONV_W - 1, bb, 2 * MIX), lambda i: (0, i, 0))
    in_specs = [rows(IN_PAD), convs, mats, rows(MIX), rows(HEADS), mats,
                _const_spec((1, HEAD_DIM)), _const_spec((CONV_W, 2 * MIX)), _const_spec((1, 2 * MIX)),
                _const_spec((1, MIX)), _const_spec((1, MIX)),
                _const_spec((1, HEAD_DIM)), _const_spec((1, HEAD_DIM)), _const_spec((8, HEAD_DIM))]
    out_specs = [rows(D_MODEL), convs, mats, rows(MIX), rows(HEADS), mats]
    out_shape = [jax.ShapeDtypeStruct((B, D_MODEL), F32),
                 jax.ShapeDtypeStruct((CONV_W - 1, B, 2 * MIX), F32),
                 jax.ShapeDtypeStruct((B, HEADS, HEAD_DIM, HEAD_DIM), F32),
                 jax.ShapeDtypeStruct((B, MIX), F32),
                 jax.ShapeDtypeStruct((B, HEADS), F32),
                 jax.ShapeDtypeStruct((B, HEADS, HEAD_DIM, HEAD_DIM), F32)]
    scratch = [pltpu.VMEM((bb, MIX), F32) for _ in range(7)] + [pltpu.VMEM((bb, HEAD_DIM), F32)]
    return pl.pallas_call(
        functools.partial(_mixer_sample_kernel, bb=bb),
        grid=(B // bb,),
        in_specs=in_specs, out_specs=out_specs, out_shape=out_shape,
        scratch_shapes=scratch,
        compiler_params=_params("parallel"),
    )(u, conv_t, C, n2, m, S, bg, w_conv, b_conv, g_mh, g_rh, cos_f, sin_s, rtab)


def _outq_sample_kernel(x_ref, h_ref, wout_ref, wcq_ref, gx_ref, x1_ref, q_ref):
    x1 = x_ref[...] + _dot(h_ref[...], wout_ref[...])
    x1_ref[...] = x1
    q_ref[...] = _dot(_rms(x1, gx_ref[...]), wcq_ref[...])


def _outq_sample(x, hcat, w_out, w_cq, g_x):
    rows = x.shape[0]
    full = _const_spec((rows, D_MODEL))
    return pl.pallas_call(
        _outq_sample_kernel,
        grid=(1,),
        in_specs=[full, full, _const_spec((D_MODEL, D_MODEL)), _const_spec((D_MODEL, D_MODEL)),
                  _const_spec((1, D_MODEL))],
        out_specs=[pl.BlockSpec((rows, D_MODEL), lambda i: (0, 0))] * 2,
        out_shape=[jax.ShapeDtypeStruct((rows, D_MODEL), F32)] * 2,
        compiler_params=_params("arbitrary"),
    )(x, hcat, w_out, w_cq, g_x)


KV_ROWS = N_MEM * 2 * X_HEADS
QROWS = 2 * X_HEADS


def _attn_sample_kernel(q_ref, k_ref, v_ref, o_ref, *, ba):
    r_id = lax.broadcasted_iota(jnp.int32, (QROWS, KV_ROWS), 0)
    n_id = lax.broadcasted_iota(jnp.int32, (QROWS, KV_ROWS), 1)
    own = (n_id & 7) == (r_id >> 1) + 4 * (r_id & 1)
    low_half = (lax.broadcasted_iota(jnp.int32, (1, KV_ROWS), 1) & 4) == 0
    for j in range(ba):
        z = _dot_nt(q_ref[j], k_ref[j])
        zc = jnp.sum(jnp.where(own, z, 0.0), axis=0, keepdims=True)
        other = jnp.where(low_half, pltpu.roll(zc, KV_ROWS - 4, 1), pltpu.roll(zc, 4, 1))
        s = (zc + other) * (X_HEAD_DIM ** -0.5)
        p = _softmax_rows(jnp.where(own, s, -jnp.inf))
        o_ref[j] = _dot(p, v_ref[j])


def _attn_sample(q, ck, cv, ba=4):
    B = q.shape[0]
    kv = pl.BlockSpec((ba, KV_ROWS, 128), lambda i: (i, 0, 0))
    rows = pl.BlockSpec((ba, QROWS, 128), lambda i: (i, 0, 0))
    return pl.pallas_call(
        functools.partial(_attn_sample_kernel, ba=ba),
        grid=(B // ba,),
        in_specs=[rows, kv, kv],
        out_specs=rows,
        out_shape=jax.ShapeDtypeStruct((B, QROWS, 128), F32),
        compiler_params=_params("parallel"),
    )(q.reshape(B, QROWS, 128), ck, cv).reshape(B, D_MODEL)


def _kv_rows(cache):
    B = cache.shape[0]
    c5 = cache.reshape(B, N_MEM, X_HEADS, 2, 128)
    return jnp.transpose(c5, (0, 1, 3, 2, 4)).reshape(B, KV_ROWS, 128)


def _rope_tables(pos):
    half = HEAD_DIM // 2
    inv = ROPE_THETA ** (-jnp.arange(half, dtype=F32) / half)
    ang = pos[:, None] * inv[None, :]
    cos, sin = jnp.cos(ang), jnp.sin(ang)
    return jnp.concatenate([cos, cos], axis=-1), jnp.concatenate([-sin, sin], axis=-1)


def _retention_tables(L):
    lg = jnp.log1p(-jnp.exp2(-5.0 - jnp.arange(HEADS, dtype=F32)))
    t = jnp.arange(L, dtype=F32)
    diff = t[:, None] - t[None, :]
    decay = jnp.where(diff >= 0, jnp.exp(lg[:, None, None] * jnp.maximum(diff, 0.0)), 0.0)
    q_dec = jnp.exp(lg[:, None] * (t + 1.0))
    k_dec = jnp.exp(lg[:, None] * (L - 1.0 - t))
    chunk_dec = jnp.exp(lg * L)
    return decay, q_dec, k_dec, chunk_dec


def _lane_pad(a):
    return jnp.pad(a, ((0, 0), (0, HEAD_DIM - a.shape[1])))


def _lanes(a, n):
    return jnp.broadcast_to(a[..., None], a.shape + (n,))


def kernel(x_prompt, x_sample, cache_mem_k, cache_mem_v, state_mlstm_conv, state_mlstm_C, state_mlstm_n, state_mlstm_m, state_ret_S, mem_prompt, w_in, b_gate, w_conv, b_conv, g_mix, g_mhead, g_rhead, w_out, g_xattn, g_mem, w_ck, w_cv, w_cq, w_co, g_ffn, w_gate, w_up, w_down, g_final):
    Bp, Tp, _ = x_prompt.shape
    Bs = x_sample.shape[0]
    l = 0
    n_m = 4 * MIX
    wi = w_in[l]
    w_gates = wi[:, n_m:n_m + 2 * HEADS]
    w_ret = wi[:, n_m + 2 * HEADS:]
    w1 = jnp.concatenate(
        [wi[:, :n_m], w_ret, w_gates, jnp.zeros((D_MODEL, IN_PAD - COL_G - 2 * HEADS), F32)],
        axis=1).astype(BF16)
    bg = jnp.pad(b_gate[l], (0, HEAD_DIM - 2 * HEADS)).reshape(1, HEAD_DIM)
    row = lambda a: a.reshape(1, -1)
    bf = lambda a: a.astype(BF16)
    g_mix_r, g_mh_r, g_rh_r = row(g_mix[l]), row(g_mhead[l]), row(g_rhead[l])
    g_x_r, g_mem_r, g_ffn_r, g_fin_r = row(g_xattn[l]), row(g_mem[l]), row(g_ffn[l]), row(g_final)
    b_conv_r = row(b_conv[l])
    w_out_b, w_cq_b, w_co_b = bf(w_out[l]), bf(w_cq[l]), bf(w_co[l])
    w_gate_b, w_up_b, w_down_b = bf(w_gate[l]), bf(w_up[l]), bf(w_down[l])

    mk, mv = _memkv(mem_prompt.reshape(Bp * N_MEM, D_MODEL), g_mem_r, bf(w_ck[l]), bf(w_cv[l]))
    wn = bf(jnp.concatenate([wi[:, :2 * MIX], w_ret[:, :2 * MIX]], axis=1))
    wt = bf(jnp.concatenate([wi[:, 2 * MIX:n_m], w_ret[:, 2 * MIX:]], axis=1).T)
    gate_rows = lambda a: jnp.concatenate(
        [a[:HEADS], jnp.zeros((8 - HEADS,) + a.shape[1:], F32),
         a[HEADS:], jnp.zeros((8 - HEADS,) + a.shape[1:], F32)], axis=0)
    wgt = bf(gate_rows(w_gates.T))
    bgt = gate_rows(_lanes(b_gate[l], CHUNK))
    cos_p, sin_p = _rope_tables(jnp.arange(Tp, dtype=F32))
    decay, q_dec, k_dec, chunk_dec = _retention_tables(CHUNK)
    pad8 = lambda a: jnp.pad(a, ((0, 8 - a.shape[0]), (0, 0)))
    hcat_p, conv_p, C_p, n_p, m_p, S_p = _mixer_prompt(
        x_prompt, wn, wt, wgt, g_mix_r, bgt, w_conv[l], b_conv_r,
        _lanes(g_mhead[l], CHUNK), _lanes(g_rhead[l], CHUNK), cos_p, sin_p,
        jnp.swapaxes(decay, 1, 2), pad8(q_dec), _lanes(k_dec, HEAD_DIM), pad8(_lanes(chunk_dec, HEAD_DIM)))
    x1_p, o_p = _attn_prompt(x_prompt, hcat_p, mk.reshape(Bp, N_MEM, D_MODEL),
                             mv.reshape(Bp, N_MEM, D_MODEL), w_out_b, w_cq_b, g_x_r)
    y_p = _ffn(x1_p.reshape(Bp * Tp, D_MODEL), o_p.reshape(Bp * Tp, D_MODEL),
               w_co_b, w_gate_b, w_up_b, w_down_b, g_ffn_r, g_fin_r, tm=512)

    xs = x_sample.reshape(Bs, D_MODEL)
    u_s = _inproj_sample(xs, g_mix_r, w1)
    cos_s, sin_s = _rope_tables(PAST_LEN + jnp.arange(1, dtype=F32))
    decay1, q_dec1, k_dec1, chunk_dec1 = _retention_tables(1)
    rtab = jnp.pad(_lane_pad(jnp.stack([q_dec1[:, 0], k_dec1[:, 0], chunk_dec1, decay1[:, 0, 0]])),
                   ((0, 4), (0, 0)))
    hcat_s, conv_s, C_s, n_s, m_s, S_s = _mixer_sample(
        u_s, jnp.transpose(state_mlstm_conv[l], (1, 0, 2)), state_mlstm_C[l],
        state_mlstm_n[l].reshape(Bs, MIX), state_mlstm_m[l], state_ret_S[l],
        bg, w_conv[l], b_conv_r, g_mh_r, g_rh_r, cos_s, sin_s, rtab)
    x1_s, q_s = _outq_sample(xs, hcat_s, w_out_b, w_cq_b, g_x_r)
    o_s = _attn_sample(q_s, _kv_rows(cache_mem_k[l]), _kv_rows(cache_mem_v[l]))
    y_s = _ffn(x1_s, o_s, w_co_b, w_gate_b, w_up_b, w_down_b, g_ffn_r, g_fin_r, tm=Bs)

    kv_shape = (1, Bp, N_MEM, X_HEADS, X_HEAD_DIM)
    return (y_p.reshape(Bp, Tp, D_MODEL), y_s.reshape(Bs, 1, D_MODEL),
            mk.reshape(kv_shape), mv.reshape(kv_shape),
            conv_p[None], C_p[None], n_p[None], m_p[None, :, :HEADS, 0], S_p[None],
            jnp.transpose(conv_s, (1, 0, 2))[None], C_s[None],
            n_s.reshape(1, Bs, HEADS, HEAD_DIM), m_s[None], S_s[None])
```

```python
import functools

import jax
import jax.numpy as jnp
import numpy as np
from jax import lax
from jax.experimental import pallas as pl
from jax.experimental.pallas import tpu as pltpu

F32 = jnp.float32
BF16 = jnp.bfloat16

D_MODEL = 1024
HEADS = 4
HEAD_DIM = 128
MIX = HEADS * HEAD_DIM
CONV_W = 4
CHUNK = 128
N_MEM = 256
X_HEADS = 4
X_HEAD_DIM = 256
D_FF = 2816
ROPE_THETA = 10000.0
EPS = 1e-6
PAST_LEN = 16384
KV_ROWS = N_MEM * 2 * X_HEADS
QROWS = 2 * X_HEADS

N_QK, N_RQ, N_RK, N_COLS = 0, 1024, 1536, 2048
T_V, T_O, T_RV, T_RG, T_ROWS = 0, 512, 1024, 1536, 2048
GT_ROWS = 16
UOFF = 8

V7X_VMEM_LIMIT = 56 * 1024 * 1024

NT_DIMS = (((1,), (1,)), ((), ()))


def _dot(a, b):
    return jnp.dot(a.astype(BF16), b.astype(BF16), preferred_element_type=F32)


def _dot_nt(a, b):
    return lax.dot_general(a.astype(BF16), b.astype(BF16), NT_DIMS, preferred_element_type=F32)


def _rms(x, g):
    return x * lax.rsqrt(jnp.mean(x * x, axis=-1, keepdims=True) + EPS) * g


def _head_norm(h):
    return h * lax.rsqrt(jnp.mean(h * h, axis=-1, keepdims=True) + EPS)


def _silu(x):
    return x * jax.nn.sigmoid(x)


def _log_sigmoid(x):
    return jnp.minimum(x, 0.0) - jnp.log1p(jnp.exp(-jnp.abs(x)))


def _rope(x, cos_full, sin_signed):
    return x * cos_full + pltpu.roll(x, HEAD_DIM // 2, 1) * sin_signed


def _const_spec(shape):
    zeros = (0,) * len(shape)
    return pl.BlockSpec(shape, lambda *_: zeros, pipeline_mode=pl.Buffered(1))


def _params(*sem):
    return pltpu.CompilerParams(dimension_semantics=sem, vmem_limit_bytes=V7X_VMEM_LIMIT)


def _memkv_kernel(mem_ref, g_ref, wk_ref, wv_ref, kb_ref, vb_ref, krows_ref, vrows_ref):
    mn = _rms(mem_ref[...], g_ref[...]).astype(BF16)
    for w_ref, b_ref, rows_ref in ((wk_ref, kb_ref, krows_ref), (wv_ref, vb_ref, vrows_ref)):
        proj = jnp.dot(mn, w_ref[...], preferred_element_type=F32)
        b_ref[...] = proj.astype(BF16)
        for h in range(X_HEADS):
            for c in range(2):
                lane0 = h * X_HEAD_DIM + c * 128
                rows_ref[pl.ds(c * X_HEADS + h, N_MEM, stride=QROWS), :] = proj[:, lane0:lane0 + 128]


def _memkv(mem, g_mem, wk, wv):
    B = mem.shape[0]
    tok = pl.BlockSpec((None, N_MEM, D_MODEL), lambda b: (b, 0, 0))
    rows = pl.BlockSpec((None, KV_ROWS, 128), lambda b: (b, 0, 0))
    return pl.pallas_call(
        _memkv_kernel,
        grid=(B,),
        in_specs=[tok, _const_spec((1, D_MODEL)),
                  _const_spec((D_MODEL, D_MODEL)), _const_spec((D_MODEL, D_MODEL))],
        out_specs=[tok, tok, rows, rows],
        out_shape=[jax.ShapeDtypeStruct((B, N_MEM, D_MODEL), BF16)] * 2
        + [jax.ShapeDtypeStruct((B, KV_ROWS, 128), F32)] * 2,
        compiler_params=_params("parallel"),
    )(mem, g_mem, wk, wv)


def _mixer_prompt_kernel(x_ref, wn_ref, wt_ref, wgt_ref, gmix_ref, bgt_ref, wconv_ref, bconv_ref,
                         gmh_ref, grh_ref, cos_ref, sin_ref, decay_ref, qdec_ref, kdec_ref, cdec_ref,
                         h_ref, conv_ref, C_ref, n_ref, m_ref, S_ref,
                         un_ref, ut_ref, gt_ref, st_ref, *, tt):
    L = CHUNK
    t = pl.program_id(1)
    chunks = range(tt // L)

    @pl.when(t == 0)
    def _():
        C_ref[...] = jnp.zeros_like(C_ref)
        st_ref[...] = jnp.zeros_like(st_ref)
        n_ref[...] = jnp.zeros_like(n_ref)
        m_ref[...] = jnp.zeros_like(m_ref)
        un_ref[0:UOFF, :] = jnp.zeros((UOFF, N_COLS), F32)

    @pl.when(t > 0)
    def _():
        un_ref[0:UOFF, N_QK:N_RQ] = un_ref[tt:tt + UOFF, N_QK:N_RQ]

    k_scale = HEAD_DIM ** -0.5
    xn = _rms(x_ref[...], gmix_ref[...]).astype(BF16)
    for c0 in range(0, N_COLS, 2 * MIX):
        un_ref[UOFF:UOFF + tt, c0:c0 + 2 * MIX] = jnp.dot(xn, wn_ref[:, c0:c0 + 2 * MIX],
                                                          preferred_element_type=F32)
    for r0 in range(0, T_ROWS, MIX):
        res = lax.dot_general(wt_ref[r0:r0 + MIX, :], xn, NT_DIMS, preferred_element_type=F32)
        for c in chunks:
            ut_ref[c, r0:r0 + MIX, :] = res[:, c * L:(c + 1) * L]
    gates_t = lax.dot_general(wgt_ref[...], xn, NT_DIMS, preferred_element_type=F32)
    for c in chunks:
        gt_ref[c] = gates_t[:, c * L:(c + 1) * L]

    src_id = lax.broadcasted_iota(jnp.int32, (L, L), 0)
    tgt_id = lax.broadcasted_iota(jnp.int32, (L, L), 1)
    causal = src_id <= tgt_id
    triu_bf = jnp.where(causal, 1.0, 0.0).astype(BF16)
    heads = range(HEADS)
    hcol = lambda base, h: slice(base + h * HEAD_DIM, base + (h + 1) * HEAD_DIM)

    def chunk_body(c, carry_unused):
        r0 = pl.multiple_of(c * L, L)
        rows = pl.ds(UOFF + r0, L)
        trows = pl.ds(r0, L)
        g_t = gt_ref[c] + bgt_ref[...]
        ig = g_t[0:8]
        lf = _log_sigmoid(g_t[8:16])
        lf_hi = lf.astype(BF16)
        r1 = lf - lf_hi.astype(F32)
        lf_mid = r1.astype(BF16)
        lf_lo = (r1 - lf_mid.astype(F32)).astype(BF16)
        bc = (jnp.dot(lf_hi, triu_bf, preferred_element_type=F32)
              + jnp.dot(lf_mid, triu_bf, preferred_element_type=F32)
              + jnp.dot(lf_lo, triu_bf, preferred_element_type=F32))
        m_prev = m_ref[...]
        inter = bc + m_prev
        b_last = bc[:, L - 1:L]
        g_w = b_last - bc + ig
        m_new = jnp.maximum(b_last + m_prev, jnp.max(g_w, axis=1, keepdims=True))
        ws = jnp.exp(g_w - m_new)
        carry = jnp.exp(b_last + m_prev - m_new)
        a_n = jnp.concatenate([ig - bc, jnp.zeros((L - 8, L), F32)], axis=0).T

        def conv_act(col):
            win = un_ref[pl.ds(r0, L + UOFF), col:col + HEAD_DIM]
            acc = bconv_ref[:, col:col + HEAD_DIM]
            for j in range(CONV_W):
                s0 = UOFF - (CONV_W - 1) + j
                acc = acc + win[s0:s0 + L] * wconv_ref[j:j + 1, col:col + HEAD_DIM]
            return _silu(acc)

        cos_f, sin_s = cos_ref[trows, :], sin_ref[trows, :]
        q = [conv_act(N_QK + h * HEAD_DIM) for h in heads]
        kb = [(conv_act(N_QK + MIX + h * HEAD_DIM) * k_scale).astype(BF16) for h in heads]
        qb = [a.astype(BF16) for a in q]
        rqb = [_rope(un_ref[rows, hcol(N_RQ, h)], cos_f, sin_s).astype(BF16) for h in heads]
        rk = [_rope(un_ref[rows, hcol(N_RK, h)], cos_f, sin_s) * k_scale for h in heads]
        v_t = [ut_ref[c, hcol(T_V, h), :] for h in heads]
        rvb = [ut_ref[c, hcol(T_RV, h), :].astype(BF16) for h in heads]
        c_old = [C_ref[h] for h in heads]
        s_old = [st_ref[h] for h in heads]
        n_old = [n_ref[h:h + 1, :] for h in heads]
        qk = [_dot_nt(kb[h], qb[h]) for h in heads]
        att = [_dot_nt(rk[h], rqb[h]) for h in heads]
        c_q = [_dot_nt(c_old[h], qb[h]) for h in heads]
        s_q = [_dot_nt(s_old[h], rqb[h]) for h in heads]
        n_q = [_dot_nt(jnp.broadcast_to(n_old[h], (8, HEAD_DIM)), qb[h])[0:1] for h in heads]
        d_c = [_dot(v_t[h] * ws[h:h + 1, :], kb[h]) for h in heads]
        d_n = [_dot(jnp.broadcast_to(ws[h:h + 1, :], (8, L)), kb[h])[0:1] for h in heads]
        d_s = [_dot(rvb[h], rk[h] * kdec_ref[h]) for h in heads]
        m_t, w_in, wts = [], [], []
        for h in heads:
            dmat = a_n[:, h:h + 1] + bc[h:h + 1, :]
            dmat = jnp.where(causal, dmat, -jnp.inf)
            m_t.append(jnp.maximum(inter[h:h + 1, :], jnp.max(dmat, axis=0, keepdims=True)))
            wts.append(jnp.exp(dmat - m_t[h]) * qk[h])
            w_in.append(jnp.exp(inter[h:h + 1, :] - m_t[h]))
        att_w = [att[h] * decay_ref[h] for h in heads]
        v_p = [_dot(v_t[h], wts[h]) for h in heads]
        v_a = [_dot(rvb[h], att_w[h]) for h in heads]
        for h in heads:
            num = v_p[h] + w_in[h] * c_q[h]
            den = jnp.sum(wts[h], axis=0, keepdims=True) + w_in[h] * n_q[h]
            hm = num / jnp.maximum(jnp.abs(den), jnp.exp(-m_t[h]))
            hm = hm * lax.rsqrt(jnp.mean(hm * hm, axis=0, keepdims=True) + EPS)
            hm = hm * gmh_ref[hcol(0, h), :] * jax.nn.sigmoid(ut_ref[c, hcol(T_O, h), :])
            h_ref[trows, hcol(0, h)] = hm.T.astype(BF16)
            carry_h = carry[h:h + 1, :]
            C_ref[h] = carry_h * c_old[h] + d_c[h]
            n_ref[h:h + 1, :] = carry_h * n_old[h] + d_n[h]
            o = v_a[h] + qdec_ref[h:h + 1, :] * s_q[h]
            st_ref[h] = cdec_ref[h:h + 1, :] * s_old[h] + d_s[h]
            hr = o * lax.rsqrt(jnp.mean(o * o, axis=0, keepdims=True) + EPS)
            hr = hr * grh_ref[hcol(0, h), :] * _silu(ut_ref[c, hcol(T_RG, h), :])
            h_ref[trows, hcol(MIX, h)] = hr.T.astype(BF16)

        m_ref[...] = m_new
        return carry_unused

    lax.fori_loop(0, tt // L, chunk_body, 0)

    @pl.when(t == pl.num_programs(1) - 1)
    def _():
        conv_ref[...] = un_ref[UOFF + tt - (CONV_W - 1):UOFF + tt, N_QK:N_RQ]
        for h in heads:
            S_ref[h] = st_ref[h].T


def _mixer_prompt(x, wn, wt, wgt, g_mix, bgt, w_conv, b_conv, gmh_cols, grh_cols, cos_f, sin_s,
                  decay_t, qdec_rows, kdec_cols, cdec_rows, tt=512):
    B, T, _ = x.shape
    tile = lambda b, t: (b, t, 0)
    per_b3 = lambda b, t: (b, 0, 0)
    per_b4 = lambda b, t: (b, 0, 0, 0)
    in_specs = [
        pl.BlockSpec((None, tt, D_MODEL), tile),
        _const_spec((D_MODEL, N_COLS)), _const_spec((T_ROWS, D_MODEL)), _const_spec((GT_ROWS, D_MODEL)),
        _const_spec((1, D_MODEL)), _const_spec((GT_ROWS, CHUNK)),
        _const_spec((CONV_W, 2 * MIX)), _const_spec((1, 2 * MIX)),
        _const_spec((MIX, CHUNK)), _const_spec((MIX, CHUNK)),
        pl.BlockSpec((tt, HEAD_DIM), lambda b, t: (t, 0)),
        pl.BlockSpec((tt, HEAD_DIM), lambda b, t: (t, 0)),
        _const_spec((HEADS, CHUNK, CHUNK)),
        _const_spec((8, CHUNK)), _const_spec((HEADS, CHUNK, HEAD_DIM)), _const_spec((8, HEAD_DIM)),
    ]
    out_specs = [
        pl.BlockSpec((None, tt, D_MODEL), tile),
        pl.BlockSpec((None, CONV_W - 1, 2 * MIX), per_b3),
        pl.BlockSpec((None, HEADS, HEAD_DIM, HEAD_DIM), per_b4),
        pl.BlockSpec((None, HEADS, HEAD_DIM), per_b3),
        pl.BlockSpec((None, 8, CHUNK), per_b3),
        pl.BlockSpec((None, HEADS, HEAD_DIM, HEAD_DIM), per_b4),
    ]
    out_shape = [
        jax.ShapeDtypeStruct((B, T, D_MODEL), BF16),
        jax.ShapeDtypeStruct((B, CONV_W - 1, 2 * MIX), F32),
        jax.ShapeDtypeStruct((B, HEADS, HEAD_DIM, HEAD_DIM), F32),
        jax.ShapeDtypeStruct((B, HEADS, HEAD_DIM), F32),
        jax.ShapeDtypeStruct((B, 8, CHUNK), F32),
        jax.ShapeDtypeStruct((B, HEADS, HEAD_DIM, HEAD_DIM), F32),
    ]
    n_chunks = tt // CHUNK
    scratch = [
        pltpu.VMEM((tt + UOFF, N_COLS), F32),
        pltpu.VMEM((n_chunks, T_ROWS, CHUNK), F32),
        pltpu.VMEM((n_chunks, GT_ROWS, CHUNK), F32),
        pltpu.VMEM((HEADS, HEAD_DIM, HEAD_DIM), F32),
    ]
    return pl.pallas_call(
        functools.partial(_mixer_prompt_kernel, tt=tt),
        grid=(B, T // tt),
        in_specs=in_specs, out_specs=out_specs, out_shape=out_shape,
        scratch_shapes=scratch,
        compiler_params=_params("parallel", "arbitrary"),
    )(x, wn, wt, wgt, g_mix, bgt, w_conv, b_conv, gmh_cols, grh_cols, cos_f, sin_s,
      decay_t, qdec_rows, kdec_cols, cdec_rows)


def _softmax_rows(s):
    e = jnp.exp(s - jnp.max(s, axis=-1, keepdims=True))
    return e / jnp.sum(e, axis=-1, keepdims=True)


def _attn_prompt_kernel(x_ref, h_ref, k_ref, v_ref, wout_ref, wcq_ref, gx_ref, x1_ref, o_ref):
    x1 = x_ref[...] + jnp.dot(h_ref[...], wout_ref[...], preferred_element_type=F32)
    x1_ref[...] = x1
    q = jnp.dot(_rms(x1, gx_ref[...]).astype(BF16), wcq_ref[...],
                preferred_element_type=F32).astype(BF16)
    for h in range(X_HEADS):
        sl = slice(h * X_HEAD_DIM, (h + 1) * X_HEAD_DIM)
        s = _dot_nt(q[:, sl], k_ref[:, sl]) * (X_HEAD_DIM ** -0.5)
        o_ref[:, sl] = _dot(_softmax_rows(s), v_ref[:, sl]).astype(BF16)


def _attn_prompt(x, hcat, mk, mv, w_out, w_cq, g_x, tm=512):
    B, T, _ = x.shape
    tile = pl.BlockSpec((None, tm, D_MODEL), lambda b, t: (b, t, 0))
    kv = pl.BlockSpec((None, N_MEM, D_MODEL), lambda b, t: (b, 0, 0))
    return pl.pallas_call(
        _attn_prompt_kernel,
        grid=(B, T // tm),
        in_specs=[tile, tile, kv, kv, _const_spec((D_MODEL, D_MODEL)),
                  _const_spec((D_MODEL, D_MODEL)), _const_spec((1, D_MODEL))],
        out_specs=[tile, tile],
        out_shape=[jax.ShapeDtypeStruct((B, T, D_MODEL), F32),
                   jax.ShapeDtypeStruct((B, T, D_MODEL), BF16)],
        compiler_params=_params("parallel", "parallel"),
    )(x, hcat, mk, mv, w_out, w_cq, g_x)


FF_CHUNK = D_FF // 2


def _ffn_kernel(x1_ref, o_ref, wco_ref, wg_ref, wu_ref, wd_ref, gffn_ref, gfin_ref, y_ref):
    x2 = x1_ref[...] + _dot(o_ref[...], wco_ref[...])
    hf = _rms(x2, gffn_ref[...]).astype(BF16)
    acc = x2
    for c0 in range(0, D_FF, FF_CHUNK):
        gate = jnp.dot(hf, wg_ref[:, c0:c0 + FF_CHUNK], preferred_element_type=F32)
        up = jnp.dot(hf, wu_ref[:, c0:c0 + FF_CHUNK], preferred_element_type=F32)
        acc = acc + _dot(_silu(gate) * up, wd_ref[c0:c0 + FF_CHUNK, :])
    y_ref[...] = _rms(acc, gfin_ref[...])


def _ffn(x1, o, w_co, w_gate, w_up, w_down, g_ffn, g_final, tm):
    rows = x1.shape[0]
    row_spec = pl.BlockSpec((tm, D_MODEL), lambda i: (i, 0))
    return pl.pallas_call(
        _ffn_kernel,
        grid=(rows // tm,),
        in_specs=[row_spec, row_spec, _const_spec((D_MODEL, D_MODEL)),
                  _const_spec((D_MODEL, D_FF)), _const_spec((D_MODEL, D_FF)),
                  _const_spec((D_FF, D_MODEL)), _const_spec((1, D_MODEL)), _const_spec((1, D_MODEL))],
        out_specs=row_spec,
        out_shape=jax.ShapeDtypeStruct((rows, D_MODEL), F32),
        compiler_params=_params("parallel"),
    )(x1, o, w_co, w_gate, w_up, w_down, g_ffn, g_final)


def _inproj_sample_kernel(x_ref, g_ref, wn_ref, wt_ref, wgt_ref, un_ref, ut_ref, gs_ref):
    xn = _rms(x_ref[...], g_ref[...]).astype(BF16)
    un_ref[...] = jnp.dot(xn, wn_ref[...], preferred_element_type=F32)
    ut_ref[...] = lax.dot_general(xn, wt_ref[...], NT_DIMS, preferred_element_type=F32)
    gs_ref[...] = lax.dot_general(xn, wgt_ref[...], NT_DIMS, preferred_element_type=F32)


def _inproj_sample(x, g_mix, wn, wt, wgt):
    rows = x.shape[0]
    full = lambda cols: pl.BlockSpec((rows, cols), lambda i: (0, 0))
    return pl.pallas_call(
        _inproj_sample_kernel,
        grid=(1,),
        in_specs=[_const_spec((rows, D_MODEL)), _const_spec((1, D_MODEL)),
                  _const_spec((D_MODEL, N_COLS)), _const_spec((T_ROWS, D_MODEL)),
                  _const_spec((GT_ROWS, D_MODEL))],
        out_specs=[full(N_COLS), full(T_ROWS), full(GT_ROWS)],
        out_shape=[jax.ShapeDtypeStruct((rows, N_COLS), F32),
                   jax.ShapeDtypeStruct((rows, T_ROWS), F32),
                   jax.ShapeDtypeStruct((rows, GT_ROWS), F32)],
        compiler_params=_params("arbitrary"),
    )(x, g_mix, wn, wt, wgt)


def _mixer_sample_kernel(un_ref, ut_ref, gs_ref, conv_ref, C_ref, n_ref, m_ref, S_ref, bg_ref,
                         wconv_ref, bconv_ref, gmh_ref, grh_ref, cos_ref, sin_ref, rtab_ref,
                         h_ref, convo_ref, Co_ref, no_ref, mo_ref, So_ref,
                         q_s, k_s, vws_s, qr_s, kr_s, qc_s, qs_s, carry_s, *, bb):
    k_scale = HEAD_DIM ** -0.5
    uqk = un_ref[:, N_QK:N_RQ]
    conv = (bconv_ref[...] + wconv_ref[0:1, :] * conv_ref[0] + wconv_ref[1:2, :] * conv_ref[1]
            + wconv_ref[2:3, :] * conv_ref[2] + wconv_ref[3:4, :] * uqk)
    convo_ref[0] = conv_ref[1]
    convo_ref[1] = conv_ref[2]
    convo_ref[2] = uqk
    qk_act = _silu(conv)
    q_s[...] = qk_act[:, 0:MIX]
    k_s[...] = qk_act[:, MIX:2 * MIX] * k_scale

    gates = gs_ref[...] + bg_ref[...]
    ig = gates[:, 0:HEADS]
    lf = _log_sigmoid(gates[:, 8:8 + HEADS])
    inter = lf + m_ref[...]
    m_t = jnp.maximum(inter, ig)
    ws = jnp.exp(ig - m_t)
    w_in = jnp.exp(inter - m_t)
    mo_ref[...] = m_t
    carry_s[:, 0:HEADS] = w_in

    cos_f = cos_ref[...]
    sin_s = sin_ref[...]
    for h in range(HEADS):
        lo = h * HEAD_DIM
        hs = slice(lo, lo + HEAD_DIM)
        vws_s[:, hs] = ut_ref[:, T_V + lo:T_V + lo + HEAD_DIM] * ws[:, h:h + 1]
        qr_s[:, hs] = _rope(un_ref[:, N_RQ + lo:N_RQ + lo + HEAD_DIM], cos_f, sin_s)
        kr_s[:, hs] = _rope(un_ref[:, N_RK + lo:N_RK + lo + HEAD_DIM], cos_f, sin_s) * k_scale

    for b in range(bb):
        row = slice(b, b + 1)
        for h in range(HEADS):
            lo = h * HEAD_DIM
            hs = slice(lo, lo + HEAD_DIM)
            c_h = C_ref[b, h]
            q8 = jnp.broadcast_to(q_s[row, hs], (8, HEAD_DIM))
            qc_s[row, hs] = _dot_nt(q8, c_h)[0:1]
            v_col = jnp.broadcast_to(vws_s[row, hs], (HEAD_DIM, HEAD_DIM)).T
            Co_ref[b, h] = carry_s[row, h:h + 1] * c_h + v_col * k_s[row, hs]
            s_h = S_ref[b, h]
            qr8 = jnp.broadcast_to(qr_s[row, hs], (8, HEAD_DIM))
            qs_s[row, hs] = _dot(qr8, s_h)[0:1]
            k_col = jnp.broadcast_to(kr_s[row, hs] * rtab_ref[1:2, h:h + 1], (HEAD_DIM, HEAD_DIM)).T
            So_ref[b, h] = (rtab_ref[2:3, h:h + 1] * s_h
                            + k_col * ut_ref[row, T_RV + lo:T_RV + lo + HEAD_DIM])

    for h in range(HEADS):
        lo = h * HEAD_DIM
        hs = slice(lo, lo + HEAD_DIM)
        q_h, k_h, n_h = q_s[:, hs], k_s[:, hs], n_ref[:, hs]
        ws_h, w_in_h = ws[:, h:h + 1], w_in[:, h:h + 1]
        wts = ws_h * jnp.sum(q_h * k_h, axis=1, keepdims=True)
        num = wts * ut_ref[:, T_V + lo:T_V + lo + HEAD_DIM] + w_in_h * qc_s[:, hs]
        den = wts + w_in_h * jnp.sum(n_h * q_h, axis=1, keepdims=True)
        hm = num / jnp.maximum(jnp.abs(den), jnp.exp(-m_t[:, h:h + 1]))
        mo = ut_ref[:, T_O + lo:T_O + lo + HEAD_DIM]
        h_ref[:, hs] = _head_norm(hm) * gmh_ref[:, hs] * jax.nn.sigmoid(mo)
        no_ref[:, hs] = w_in_h * n_h + ws_h * k_h

        att = jnp.sum(qr_s[:, hs] * kr_s[:, hs], axis=1, keepdims=True) * rtab_ref[3:4, h:h + 1]
        o = (att * ut_ref[:, T_RV + lo:T_RV + lo + HEAD_DIM]
             + rtab_ref[0:1, h:h + 1] * qs_s[:, hs])
        rg = ut_ref[:, T_RG + lo:T_RG + lo + HEAD_DIM]
        h_ref[:, MIX + lo:MIX + lo + HEAD_DIM] = _head_norm(o) * grh_ref[:, hs] * _silu(rg)


def _mixer_sample(un, ut, gs, conv_t, C, n2, m, S, bg, w_conv, b_conv, g_mh, g_rh, cos_f, sin_s, rtab,
                  bb=8):
    B = un.shape[0]
    rows = lambda cols: pl.BlockSpec((bb, cols), lambda i: (i, 0))
    mats = pl.BlockSpec((bb, HEADS, HEAD_DIM, HEAD_DIM), lambda i: (i, 0, 0, 0))
    convs = pl.BlockSpec((CONV_W - 1, bb, 2 * MIX), lambda i: (0, i, 0))
    in_specs = [rows(N_COLS), rows(T_ROWS), rows(GT_ROWS), convs, mats, rows(MIX), rows(HEADS), mats,
                _const_spec((1, GT_ROWS)), _const_spec((CONV_W, 2 * MIX)), _const_spec((1, 2 * MIX)),
                _const_spec((1, MIX)), _const_spec((1, MIX)),
                _const_spec((1, HEAD_DIM)), _const_spec((1, HEAD_DIM)), _const_spec((8, HEAD_DIM))]
    out_specs = [rows(D_MODEL), convs, mats, rows(MIX), rows(HEADS), mats]
    out_shape = [jax.ShapeDtypeStruct((B, D_MODEL), F32),
                 jax.ShapeDtypeStruct((CONV_W - 1, B, 2 * MIX), F32),
                 jax.ShapeDtypeStruct((B, HEADS, HEAD_DIM, HEAD_DIM), F32),
                 jax.ShapeDtypeStruct((B, MIX), F32),
                 jax.ShapeDtypeStruct((B, HEADS), F32),
                 jax.ShapeDtypeStruct((B, HEADS, HEAD_DIM, HEAD_DIM), F32)]
    scratch = [pltpu.VMEM((bb, MIX), F32) for _ in range(7)] + [pltpu.VMEM((bb, HEAD_DIM), F32)]
    return pl.pallas_call(
        functools.partial(_mixer_sample_kernel, bb=bb),
        grid=(B // bb,),
        in_specs=in_specs, out_specs=out_specs, out_shape=out_shape,
        scratch_shapes=scratch,
        compiler_params=_params("parallel"),
    )(un, ut, gs, conv_t, C, n2, m, S, bg, w_conv, b_conv, g_mh, g_rh, cos_f, sin_s, rtab)


def _outq_sample_kernel(x_ref, h_ref, wout_ref, wcq_ref, gx_ref, x1_ref, q_ref):
    x1 = x_ref[...] + _dot(h_ref[...], wout_ref[...])
    x1_ref[...] = x1
    q_ref[...] = _dot(_rms(x1, gx_ref[...]), wcq_ref[...])


def _outq_sample(x, hcat, w_out, w_cq, g_x):
    rows = x.shape[0]
    full = _const_spec((rows, D_MODEL))
    return pl.pallas_call(
        _outq_sample_kernel,
        grid=(1,),
        in_specs=[full, full, _const_spec((D_MODEL, D_MODEL)), _const_spec((D_MODEL, D_MODEL)),
                  _const_spec((1, D_MODEL))],
        out_specs=[pl.BlockSpec((rows, D_MODEL), lambda i: (0, 0))] * 2,
        out_shape=[jax.ShapeDtypeStruct((rows, D_MODEL), F32)] * 2,
        compiler_params=_params("arbitrary"),
    )(x, hcat, w_out, w_cq, g_x)


def _attn_sample_kernel(q_ref, k_ref, v_ref, o_ref, *, ba):
    r_id = lax.broadcasted_iota(jnp.int32, (QROWS, KV_ROWS), 0)
    n_id = lax.broadcasted_iota(jnp.int32, (QROWS, KV_ROWS), 1)
    own = (n_id & 7) == (r_id >> 1) + 4 * (r_id & 1)
    low_half = (lax.broadcasted_iota(jnp.int32, (1, KV_ROWS), 1) & 4) == 0
    for j in range(ba):
        z = _dot_nt(q_ref[j], k_ref[j])
        zc = jnp.sum(jnp.where(own, z, 0.0), axis=0, keepdims=True)
        other = jnp.where(low_half, pltpu.roll(zc, KV_ROWS - 4, 1), pltpu.roll(zc, 4, 1))
        s = (zc + other) * (X_HEAD_DIM ** -0.5)
        p = _softmax_rows(jnp.where(own, s, -jnp.inf))
        o_ref[j] = _dot(p, v_ref[j])


def _attn_sample(q, ck, cv, ba=4):
    B = q.shape[0]
    kv = pl.BlockSpec((ba, KV_ROWS, 128), lambda i: (i, 0, 0))
    rows = pl.BlockSpec((ba, QROWS, 128), lambda i: (i, 0, 0))
    return pl.pallas_call(
        functools.partial(_attn_sample_kernel, ba=ba),
        grid=(B // ba,),
        in_specs=[rows, kv, kv],
        out_specs=rows,
        out_shape=jax.ShapeDtypeStruct((B, QROWS, 128), F32),
        compiler_params=_params("parallel"),
    )(q.reshape(B, QROWS, 128), ck, cv).reshape(B, D_MODEL)


def _kv_rows(cache):
    B = cache.shape[0]
    c5 = cache.reshape(B, N_MEM, X_HEADS, 2, 128)
    return jnp.transpose(c5, (0, 1, 3, 2, 4)).reshape(B, KV_ROWS, 128)


def _kv_from_rows(rows):
    B = rows.shape[0]
    r5 = rows.reshape(B, N_MEM, 2, X_HEADS, 128)
    return jnp.transpose(r5, (0, 1, 3, 2, 4)).reshape(1, B, N_MEM, X_HEADS, X_HEAD_DIM)


def _rope_tables(pos):
    half = HEAD_DIM // 2
    inv = ROPE_THETA ** (-np.arange(half, dtype=np.float64) / half)
    ang = np.asarray(pos, np.float64)[:, None] * inv[None, :]
    cos, sin = np.cos(ang), np.sin(ang)
    return (np.concatenate([cos, cos], axis=-1).astype(np.float32),
            np.concatenate([-sin, sin], axis=-1).astype(np.float32))


def _retention_tables(L):
    lg = np.log1p(-np.exp2(-5.0 - np.arange(HEADS, dtype=np.float64)))
    t = np.arange(L, dtype=np.float64)
    diff = t[:, None] - t[None, :]
    decay = np.where(diff >= 0, np.exp(lg[:, None, None] * np.maximum(diff, 0.0)), 0.0)
    q_dec = np.exp(lg[:, None] * (t + 1.0))
    k_dec = np.exp(lg[:, None] * (L - 1.0 - t))
    chunk_dec = np.exp(lg * L)
    return tuple(a.astype(np.float32) for a in (decay, q_dec, k_dec, chunk_dec))


def _lanes(a, n):
    xp = np if isinstance(a, np.ndarray) else jnp
    return xp.broadcast_to(a[..., None], a.shape + (n,))


def _pad_rows(a, rows):
    return np.pad(a, ((0, rows - a.shape[0]),) + ((0, 0),) * (a.ndim - 1))


def kernel(x_prompt, x_sample, cache_mem_k, cache_mem_v, state_mlstm_conv, state_mlstm_C, state_mlstm_n, state_mlstm_m, state_ret_S, mem_prompt, w_in, b_gate, w_conv, b_conv, g_mix, g_mhead, g_rhead, w_out, g_xattn, g_mem, w_ck, w_cv, w_cq, w_co, g_ffn, w_gate, w_up, w_down, g_final):
    Bp, Tp, _ = x_prompt.shape
    Bs = x_sample.shape[0]
    l = 0
    n_m = 4 * MIX
    wi = w_in[l]
    w_gates = wi[:, n_m:n_m + 2 * HEADS]
    w_ret = wi[:, n_m + 2 * HEADS:]
    row = lambda a: a.reshape(1, -1)
    bf = lambda a: a.astype(BF16)
    g_mix_r, g_mh_r, g_rh_r = row(g_mix[l]), row(g_mhead[l]), row(g_rhead[l])
    g_x_r, g_mem_r, g_ffn_r, g_fin_r = row(g_xattn[l]), row(g_mem[l]), row(g_ffn[l]), row(g_final)
    b_conv_r = row(b_conv[l])
    w_out_b, w_cq_b, w_co_b = bf(w_out[l]), bf(w_cq[l]), bf(w_co[l])
    w_gate_b, w_up_b, w_down_b = bf(w_gate[l]), bf(w_up[l]), bf(w_down[l])

    mk_b, mv_b, mk_rows, mv_rows = _memkv(mem_prompt, g_mem_r, bf(w_ck[l]), bf(w_cv[l]))
    wn = bf(jnp.concatenate([wi[:, :2 * MIX], w_ret[:, :2 * MIX]], axis=1))
    wt = bf(jnp.concatenate([wi[:, 2 * MIX:n_m], w_ret[:, 2 * MIX:]], axis=1).T)
    gate_rows = lambda a: jnp.concatenate(
        [a[:HEADS], jnp.zeros((8 - HEADS,) + a.shape[1:], F32),
         a[HEADS:], jnp.zeros((8 - HEADS,) + a.shape[1:], F32)], axis=0)
    wgt = bf(gate_rows(w_gates.T))
    bg_rows = gate_rows(b_gate[l][:, None])
    bgt = jnp.broadcast_to(bg_rows, (GT_ROWS, CHUNK))
    cos_p, sin_p = _rope_tables(np.arange(Tp))
    decay, q_dec, k_dec, chunk_dec = _retention_tables(CHUNK)
    hcat_p, conv_p, C_p, n_p, m_p, S_p = _mixer_prompt(
        x_prompt, wn, wt, wgt, g_mix_r, bgt, w_conv[l], b_conv_r,
        _lanes(g_mhead[l], CHUNK), _lanes(g_rhead[l], CHUNK), cos_p, sin_p,
        np.swapaxes(decay, 1, 2), _pad_rows(q_dec, 8), _lanes(k_dec, HEAD_DIM),
        _pad_rows(_lanes(chunk_dec, HEAD_DIM), 8))
    x1_p, o_p = _attn_prompt(x_prompt, hcat_p, mk_b, mv_b, w_out_b, w_cq_b, g_x_r)
    y_p = _ffn(x1_p.reshape(Bp * Tp, D_MODEL), o_p.reshape(Bp * Tp, D_MODEL),
               w_co_b, w_gate_b, w_up_b, w_down_b, g_ffn_r, g_fin_r, tm=512)

    xs = x_sample.reshape(Bs, D_MODEL)
    un_s, ut_s, gs_s = _inproj_sample(xs, g_mix_r, wn, wt, wgt)
    cos_s, sin_s = _rope_tables(PAST_LEN + np.arange(1))
    decay1, q_dec1, k_dec1, chunk_dec1 = _retention_tables(1)
    rtab = np.zeros((8, HEAD_DIM), np.float32)
    rtab[:4, :HEADS] = np.stack([q_dec1[:, 0], k_dec1[:, 0], chunk_dec1, decay1[:, 0, 0]])
    hcat_s, conv_s, C_s, n_s, m_s, S_s = _mixer_sample(
        un_s, ut_s, gs_s, jnp.transpose(state_mlstm_conv[l], (1, 0, 2)), state_mlstm_C[l],
        state_mlstm_n[l].reshape(Bs, MIX), state_mlstm_m[l], state_ret_S[l],
        bg_rows.reshape(1, GT_ROWS), w_conv[l], b_conv_r, g_mh_r, g_rh_r, cos_s, sin_s, rtab)
    x1_s, q_s = _outq_sample(xs, hcat_s, w_out_b, w_cq_b, g_x_r)
    o_s = _attn_sample(q_s, _kv_rows(cache_mem_k[l]), _kv_rows(cache_mem_v[l]))
    y_s = _ffn(x1_s, o_s, w_co_b, w_gate_b, w_up_b, w_down_b, g_ffn_r, g_fin_r, tm=Bs)

    return (y_p.reshape(Bp, Tp, D_MODEL), y_s.reshape(Bs, 1, D_MODEL),
            _kv_from_rows(mk_rows), _kv_from_rows(mv_rows),
            conv_p[None], C_p[None], n_p[None], m_p[None, :, :HEADS, 0], S_p[None],
            jnp.transpose(conv_s, (1, 0, 2))[None], C_s[None],
            n_s.reshape(1, Bs, HEADS, HEAD_DIM), m_s[None], S_s[None])
```

```python
import functools

import jax
import jax.numpy as jnp
import numpy as np
from jax import lax
from jax.experimental import pallas as pl
from jax.experimental.pallas import tpu as pltpu

F32 = jnp.float32
BF16 = jnp.bfloat16

D_MODEL = 1024
HEADS = 4
HEAD_DIM = 128
MIX = HEADS * HEAD_DIM
CONV_W = 4
CHUNK = 128
N_MEM = 256
X_HEADS = 4
X_HEAD_DIM = 256
D_FF = 2816
ROPE_THETA = 10000.0
EPS = 1e-6
PAST_LEN = 16384
KV_ROWS = N_MEM * 2 * X_HEADS
QROWS = 2 * X_HEADS

N_QK, N_RQ, N_RK, N_COLS = 0, 1024, 1536, 2048
T_V, T_O, T_RV, T_RG, T_ROWS = 0, 512, 1024, 1536, 2048
GT_ROWS = 16
UOFF = 8

V7X_VMEM_LIMIT = 56 * 1024 * 1024

NT_DIMS = (((1,), (1,)), ((), ()))
TN_DIMS = (((0,), (0,)), ((), ()))


def _dot(a, b):
    return jnp.dot(a.astype(BF16), b.astype(BF16), preferred_element_type=F32)


def _dot_nt(a, b):
    return lax.dot_general(a.astype(BF16), b.astype(BF16), NT_DIMS, preferred_element_type=F32)


def _dot_tn(a, b):
    return lax.dot_general(a.astype(BF16), b.astype(BF16), TN_DIMS, preferred_element_type=F32)


def _rms(x, g):
    return x * lax.rsqrt(jnp.mean(x * x, axis=-1, keepdims=True) + EPS) * g


def _head_norm(h):
    return h * lax.rsqrt(jnp.mean(h * h, axis=-1, keepdims=True) + EPS)


def _silu(x):
    return x * jax.nn.sigmoid(x)


def _log_sigmoid(x):
    return jnp.minimum(x, 0.0) - jnp.log1p(jnp.exp(-jnp.abs(x)))


def _rope(x, cos_full, sin_signed):
    return x * cos_full + pltpu.roll(x, HEAD_DIM // 2, 1) * sin_signed


def _const_spec(shape):
    zeros = (0,) * len(shape)
    return pl.BlockSpec(shape, lambda *_: zeros, pipeline_mode=pl.Buffered(1))


def _params(*sem):
    return pltpu.CompilerParams(dimension_semantics=sem, vmem_limit_bytes=V7X_VMEM_LIMIT)


def _memkv_kernel(mem_ref, g_ref, wk_ref, wv_ref, kb_ref, vb_ref, krows_ref, vrows_ref):
    mn = _rms(mem_ref[...], g_ref[...]).astype(BF16)
    for w_ref, b_ref, rows_ref in ((wk_ref, kb_ref, krows_ref), (wv_ref, vb_ref, vrows_ref)):
        proj = jnp.dot(mn, w_ref[...], preferred_element_type=F32)
        b_ref[...] = proj.astype(BF16)
        for h in range(X_HEADS):
            for c in range(2):
                lane0 = h * X_HEAD_DIM + c * 128
                rows_ref[pl.ds(c * X_HEADS + h, N_MEM, stride=QROWS), :] = proj[:, lane0:lane0 + 128]


def _memkv(mem, g_mem, wk, wv):
    B = mem.shape[0]
    tok = pl.BlockSpec((None, N_MEM, D_MODEL), lambda b: (b, 0, 0))
    rows = pl.BlockSpec((None, KV_ROWS, 128), lambda b: (b, 0, 0))
    return pl.pallas_call(
        _memkv_kernel,
        grid=(B,),
        in_specs=[tok, _const_spec((1, D_MODEL)),
                  _const_spec((D_MODEL, D_MODEL)), _const_spec((D_MODEL, D_MODEL))],
        out_specs=[tok, tok, rows, rows],
        out_shape=[jax.ShapeDtypeStruct((B, N_MEM, D_MODEL), BF16)] * 2
        + [jax.ShapeDtypeStruct((B, KV_ROWS, 128), F32)] * 2,
        compiler_params=_params("parallel"),
    )(mem, g_mem, wk, wv)


def _mixer_prompt_kernel(x_ref, wn_ref, wt_ref, wgt_ref, gmix_ref, bgt_ref, wconv_ref, bconv_ref,
                         gmh_ref, grh_ref, cos_ref, sin_ref, decay_ref, qdec_ref, kdec_ref, cdec_ref,
                         h_ref, conv_ref, C_ref, n_ref, m_ref, S_ref,
                         un_ref, ut_ref, gt_ref, st_ref, *, tt):
    L = CHUNK
    t = pl.program_id(1)
    chunks = range(tt // L)

    @pl.when(t == 0)
    def _():
        C_ref[...] = jnp.zeros_like(C_ref)
        st_ref[...] = jnp.zeros_like(st_ref)
        n_ref[...] = jnp.zeros_like(n_ref)
        m_ref[...] = jnp.zeros_like(m_ref)
        un_ref[0:UOFF, :] = jnp.zeros((UOFF, N_COLS), F32)

    @pl.when(t > 0)
    def _():
        un_ref[0:UOFF, N_QK:N_RQ] = un_ref[tt:tt + UOFF, N_QK:N_RQ]

    k_scale = HEAD_DIM ** -0.5
    xn = _rms(x_ref[...], gmix_ref[...]).astype(BF16)
    for c0 in range(0, N_COLS, 2 * MIX):
        un_ref[UOFF:UOFF + tt, c0:c0 + 2 * MIX] = jnp.dot(xn, wn_ref[:, c0:c0 + 2 * MIX],
                                                          preferred_element_type=F32)
    for r0 in range(0, T_ROWS, MIX):
        res = lax.dot_general(wt_ref[r0:r0 + MIX, :], xn, NT_DIMS, preferred_element_type=F32)
        for c in chunks:
            ut_ref[c, r0:r0 + MIX, :] = res[:, c * L:(c + 1) * L]
    gates_t = lax.dot_general(wgt_ref[...], xn, NT_DIMS, preferred_element_type=F32)
    for c in chunks:
        gt_ref[c] = gates_t[:, c * L:(c + 1) * L]

    src_id = lax.broadcasted_iota(jnp.int32, (L, L), 0)
    tgt_id = lax.broadcasted_iota(jnp.int32, (L, L), 1)
    causal = src_id <= tgt_id
    triu_bf = jnp.where(causal, 1.0, 0.0).astype(BF16)
    heads = range(HEADS)
    hcol = lambda base, h: slice(base + h * HEAD_DIM, base + (h + 1) * HEAD_DIM)

    def chunk_body(c, carry_unused):
        r0 = pl.multiple_of(c * L, L)
        rows = pl.ds(UOFF + r0, L)
        trows = pl.ds(r0, L)
        g_t = gt_ref[c] + bgt_ref[...]
        ig = g_t[0:8]
        lf = _log_sigmoid(g_t[8:16])
        lf_hi = lf.astype(BF16)
        r1 = lf - lf_hi.astype(F32)
        lf_mid = r1.astype(BF16)
        lf_lo = (r1 - lf_mid.astype(F32)).astype(BF16)
        bc = (jnp.dot(lf_hi, triu_bf, preferred_element_type=F32)
              + jnp.dot(lf_mid, triu_bf, preferred_element_type=F32)
              + jnp.dot(lf_lo, triu_bf, preferred_element_type=F32))
        m_prev = m_ref[...]
        inter = bc + m_prev
        b_last = bc[:, L - 1:L]
        g_w = b_last - bc + ig
        m_new = jnp.maximum(b_last + m_prev, jnp.max(g_w, axis=1, keepdims=True))
        ws = jnp.exp(g_w - m_new)
        carry = jnp.exp(b_last + m_prev - m_new)
        a_n = jnp.concatenate([ig - bc, jnp.zeros((L - 8, L), F32)], axis=0).T

        def conv_act(col):
            win = un_ref[pl.ds(r0, L + UOFF), col:col + HEAD_DIM]
            acc = bconv_ref[:, col:col + HEAD_DIM]
            for j in range(CONV_W):
                s0 = UOFF - (CONV_W - 1) + j
                acc = acc + win[s0:s0 + L] * wconv_ref[j:j + 1, col:col + HEAD_DIM]
            return _silu(acc)

        cos_f, sin_s = cos_ref[trows, :], sin_ref[trows, :]
        q = [conv_act(N_QK + h * HEAD_DIM) for h in heads]
        kb = [(conv_act(N_QK + MIX + h * HEAD_DIM) * k_scale).astype(BF16) for h in heads]
        qb = [a.astype(BF16) for a in q]
        rqb = [_rope(un_ref[rows, hcol(N_RQ, h)], cos_f, sin_s).astype(BF16) for h in heads]
        rk = [_rope(un_ref[rows, hcol(N_RK, h)], cos_f, sin_s) * k_scale for h in heads]
        v_t = [ut_ref[c, hcol(T_V, h), :] for h in heads]
        rvb = [ut_ref[c, hcol(T_RV, h), :].astype(BF16) for h in heads]
        c_old = [C_ref[h] for h in heads]
        s_old = [st_ref[h] for h in heads]
        n_old = [n_ref[h:h + 1, :] for h in heads]
        qk = [_dot_nt(kb[h], qb[h]) for h in heads]
        att = [_dot_nt(rk[h], rqb[h]) for h in heads]
        c_q = [_dot_nt(c_old[h], qb[h]) for h in heads]
        s_q = [_dot_nt(s_old[h], rqb[h]) for h in heads]
        n_q = [_dot_nt(jnp.broadcast_to(n_old[h], (8, HEAD_DIM)), qb[h])[0:1] for h in heads]
        d_c = [_dot(v_t[h] * ws[h:h + 1, :], kb[h]) for h in heads]
        d_n = [_dot(jnp.broadcast_to(ws[h:h + 1, :], (8, L)), kb[h])[0:1] for h in heads]
        d_s = [_dot(rvb[h], rk[h] * kdec_ref[h]) for h in heads]
        m_t, w_in, wts = [], [], []
        for h in heads:
            dmat = a_n[:, h:h + 1] + bc[h:h + 1, :]
            dmat = jnp.where(causal, dmat, -jnp.inf)
            m_t.append(jnp.maximum(inter[h:h + 1, :], jnp.max(dmat, axis=0, keepdims=True)))
            wts.append(jnp.exp(dmat - m_t[h]) * qk[h])
            w_in.append(jnp.exp(inter[h:h + 1, :] - m_t[h]))
        att_w = [att[h] * decay_ref[h] for h in heads]
        v_p = [_dot(v_t[h], wts[h]) for h in heads]
        v_a = [_dot(rvb[h], att_w[h]) for h in heads]
        for h in heads:
            num = v_p[h] + w_in[h] * c_q[h]
            den = jnp.sum(wts[h], axis=0, keepdims=True) + w_in[h] * n_q[h]
            hm = num / jnp.maximum(jnp.abs(den), jnp.exp(-m_t[h]))
            hm = hm * lax.rsqrt(jnp.mean(hm * hm, axis=0, keepdims=True) + EPS)
            hm = hm * gmh_ref[hcol(0, h), :] * jax.nn.sigmoid(ut_ref[c, hcol(T_O, h), :])
            h_ref[trows, hcol(0, h)] = hm.T.astype(BF16)
            carry_h = carry[h:h + 1, :]
            C_ref[h] = carry_h * c_old[h] + d_c[h]
            n_ref[h:h + 1, :] = carry_h * n_old[h] + d_n[h]
            o = v_a[h] + qdec_ref[h:h + 1, :] * s_q[h]
            st_ref[h] = cdec_ref[h:h + 1, :] * s_old[h] + d_s[h]
            hr = o * lax.rsqrt(jnp.mean(o * o, axis=0, keepdims=True) + EPS)
            hr = hr * grh_ref[hcol(0, h), :] * _silu(ut_ref[c, hcol(T_RG, h), :])
            h_ref[trows, hcol(MIX, h)] = hr.T.astype(BF16)

        m_ref[...] = m_new
        return carry_unused

    lax.fori_loop(0, tt // L, chunk_body, 0)

    @pl.when(t == pl.num_programs(1) - 1)
    def _():
        conv_ref[...] = un_ref[UOFF + tt - (CONV_W - 1):UOFF + tt, N_QK:N_RQ]
        for h in heads:
            S_ref[h] = st_ref[h].T


def _mixer_prompt(x, wn, wt, wgt, g_mix, bgt, w_conv, b_conv, gmh_cols, grh_cols, cos_f, sin_s,
                  decay_t, qdec_rows, kdec_cols, cdec_rows, tt=512):
    B, T, _ = x.shape
    tile = lambda b, t: (b, t, 0)
    per_b3 = lambda b, t: (b, 0, 0)
    per_b4 = lambda b, t: (b, 0, 0, 0)
    in_specs = [
        pl.BlockSpec((None, tt, D_MODEL), tile),
        _const_spec((D_MODEL, N_COLS)), _const_spec((T_ROWS, D_MODEL)), _const_spec((GT_ROWS, D_MODEL)),
        _const_spec((1, D_MODEL)), _const_spec((GT_ROWS, CHUNK)),
        _const_spec((CONV_W, 2 * MIX)), _const_spec((1, 2 * MIX)),
        _const_spec((MIX, CHUNK)), _const_spec((MIX, CHUNK)),
        pl.BlockSpec((tt, HEAD_DIM), lambda b, t: (t, 0)),
        pl.BlockSpec((tt, HEAD_DIM), lambda b, t: (t, 0)),
        _const_spec((HEADS, CHUNK, CHUNK)),
        _const_spec((8, CHUNK)), _const_spec((HEADS, CHUNK, HEAD_DIM)), _const_spec((8, HEAD_DIM)),
    ]
    out_specs = [
        pl.BlockSpec((None, tt, D_MODEL), tile),
        pl.BlockSpec((None, CONV_W - 1, 2 * MIX), per_b3),
        pl.BlockSpec((None, HEADS, HEAD_DIM, HEAD_DIM), per_b4),
        pl.BlockSpec((None, HEADS, HEAD_DIM), per_b3),
        pl.BlockSpec((None, 8, CHUNK), per_b3),
        pl.BlockSpec((None, HEADS, HEAD_DIM, HEAD_DIM), per_b4),
    ]
    out_shape = [
        jax.ShapeDtypeStruct((B, T, D_MODEL), BF16),
        jax.ShapeDtypeStruct((B, CONV_W - 1, 2 * MIX), F32),
        jax.ShapeDtypeStruct((B, HEADS, HEAD_DIM, HEAD_DIM), F32),
        jax.ShapeDtypeStruct((B, HEADS, HEAD_DIM), F32),
        jax.ShapeDtypeStruct((B, 8, CHUNK), F32),
        jax.ShapeDtypeStruct((B, HEADS, HEAD_DIM, HEAD_DIM), F32),
    ]
    n_chunks = tt // CHUNK
    scratch = [
        pltpu.VMEM((tt + UOFF, N_COLS), F32),
        pltpu.VMEM((n_chunks, T_ROWS, CHUNK), F32),
        pltpu.VMEM((n_chunks, GT_ROWS, CHUNK), F32),
        pltpu.VMEM((HEADS, HEAD_DIM, HEAD_DIM), F32),
    ]
    return pl.pallas_call(
        functools.partial(_mixer_prompt_kernel, tt=tt),
        grid=(B, T // tt),
        in_specs=in_specs, out_specs=out_specs, out_shape=out_shape,
        scratch_shapes=scratch,
        compiler_params=_params("parallel", "arbitrary"),
    )(x, wn, wt, wgt, g_mix, bgt, w_conv, b_conv, gmh_cols, grh_cols, cos_f, sin_s,
      decay_t, qdec_rows, kdec_cols, cdec_rows)


def _attn_prompt_kernel(x_ref, h_ref, k_ref, v_ref, wout_ref, wcq_ref, gx_ref, x1_ref, o_ref):
    x1 = x_ref[...] + jnp.dot(h_ref[...], wout_ref[...], preferred_element_type=F32)
    x1_ref[...] = x1
    q = jnp.dot(_rms(x1, gx_ref[...]).astype(BF16), wcq_ref[...],
                preferred_element_type=F32).astype(BF16)
    heads = range(X_HEADS)
    sl = [slice(h * X_HEAD_DIM, (h + 1) * X_HEAD_DIM) for h in heads]
    s = [_dot_nt(q[:, sl[h]], k_ref[:, sl[h]]) * (X_HEAD_DIM ** -0.5) for h in heads]
    e = [jnp.exp(s[h] - jnp.max(s[h], axis=-1, keepdims=True)) for h in heads]
    p = [e[h] / jnp.sum(e[h], axis=-1, keepdims=True) for h in heads]
    for h in heads:
        o_ref[:, sl[h]] = _dot(p[h], v_ref[:, sl[h]]).astype(BF16)


def _attn_prompt(x, hcat, mk, mv, w_out, w_cq, g_x, tm=512):
    B, T, _ = x.shape
    tile = pl.BlockSpec((None, tm, D_MODEL), lambda b, t: (b, t, 0))
    kv = pl.BlockSpec((None, N_MEM, D_MODEL), lambda b, t: (b, 0, 0))
    return pl.pallas_call(
        _attn_prompt_kernel,
        grid=(B, T // tm),
        in_specs=[tile, tile, kv, kv, _const_spec((D_MODEL, D_MODEL)),
                  _const_spec((D_MODEL, D_MODEL)), _const_spec((1, D_MODEL))],
        out_specs=[tile, tile],
        out_shape=[jax.ShapeDtypeStruct((B, T, D_MODEL), F32),
                   jax.ShapeDtypeStruct((B, T, D_MODEL), BF16)],
        compiler_params=_params("parallel", "parallel"),
    )(x, hcat, mk, mv, w_out, w_cq, g_x)


FF_CHUNK = D_FF // 2


def _ffn_kernel(x1_ref, o_ref, wco_ref, wg_ref, wu_ref, wd_ref, gffn_ref, gfin_ref, y_ref):
    x2 = x1_ref[...] + _dot(o_ref[...], wco_ref[...])
    hf = _rms(x2, gffn_ref[...]).astype(BF16)
    acc = x2
    for c0 in range(0, D_FF, FF_CHUNK):
        gate = jnp.dot(hf, wg_ref[:, c0:c0 + FF_CHUNK], preferred_element_type=F32)
        up = jnp.dot(hf, wu_ref[:, c0:c0 + FF_CHUNK], preferred_element_type=F32)
        acc = acc + _dot(_silu(gate) * up, wd_ref[c0:c0 + FF_CHUNK, :])
    y_ref[...] = _rms(acc, gfin_ref[...])


def _ffn(x1, o, w_co, w_gate, w_up, w_down, g_ffn, g_final, tm):
    rows = x1.shape[0]
    row_spec = pl.BlockSpec((tm, D_MODEL), lambda i: (i, 0))
    return pl.pallas_call(
        _ffn_kernel,
        grid=(rows // tm,),
        in_specs=[row_spec, row_spec, _const_spec((D_MODEL, D_MODEL)),
                  _const_spec((D_MODEL, D_FF)), _const_spec((D_MODEL, D_FF)),
                  _const_spec((D_FF, D_MODEL)), _const_spec((1, D_MODEL)), _const_spec((1, D_MODEL))],
        out_specs=row_spec,
        out_shape=jax.ShapeDtypeStruct((rows, D_MODEL), F32),
        compiler_params=_params("parallel"),
    )(x1, o, w_co, w_gate, w_up, w_down, g_ffn, g_final)


def _inproj_sample_kernel(x_ref, g_ref, wn_ref, wt_ref, wgt_ref, un_ref, ut_ref, gs_ref):
    xn = _rms(x_ref[...], g_ref[...]).astype(BF16)
    un_ref[...] = jnp.dot(xn, wn_ref[...], preferred_element_type=F32)
    ut_ref[...] = lax.dot_general(xn, wt_ref[...], NT_DIMS, preferred_element_type=F32)
    gs_ref[...] = lax.dot_general(xn, wgt_ref[...], NT_DIMS, preferred_element_type=F32)


def _inproj_sample(x, g_mix, wn, wt, wgt):
    rows = x.shape[0]
    full = lambda cols: pl.BlockSpec((rows, cols), lambda i: (0, 0))
    return pl.pallas_call(
        _inproj_sample_kernel,
        grid=(1,),
        in_specs=[_const_spec((rows, D_MODEL)), _const_spec((1, D_MODEL)),
                  _const_spec((D_MODEL, N_COLS)), _const_spec((T_ROWS, D_MODEL)),
                  _const_spec((GT_ROWS, D_MODEL))],
        out_specs=[full(N_COLS), full(T_ROWS), full(GT_ROWS)],
        out_shape=[jax.ShapeDtypeStruct((rows, N_COLS), F32),
                   jax.ShapeDtypeStruct((rows, T_ROWS), F32),
                   jax.ShapeDtypeStruct((rows, GT_ROWS), F32)],
        compiler_params=_params("arbitrary"),
    )(x, g_mix, wn, wt, wgt)


def _mixer_sample_kernel(un_ref, ut_ref, gs_ref, conv_ref, C_ref, n_ref, m_ref, S_ref, bg_ref,
                         wconv_ref, bconv_ref, gmh_ref, grh_ref, cos_ref, sin_ref, rtab_ref,
                         h_ref, convo_ref, Co_ref, no_ref, mo_ref, So_ref,
                         q_s, k_s, vws_s, qr_s, kr_s, qc_s, qs_s, carry_s, *, bb):
    k_scale = HEAD_DIM ** -0.5
    uqk = un_ref[:, N_QK:N_RQ]
    conv = (bconv_ref[...] + wconv_ref[0:1, :] * conv_ref[0] + wconv_ref[1:2, :] * conv_ref[1]
            + wconv_ref[2:3, :] * conv_ref[2] + wconv_ref[3:4, :] * uqk)
    convo_ref[0] = conv_ref[1]
    convo_ref[1] = conv_ref[2]
    convo_ref[2] = uqk
    qk_act = _silu(conv)
    q_s[...] = qk_act[:, 0:MIX]
    k_s[...] = qk_act[:, MIX:2 * MIX] * k_scale

    gates = gs_ref[...] + bg_ref[...]
    ig = gates[:, 0:HEADS]
    lf = _log_sigmoid(gates[:, 8:8 + HEADS])
    inter = lf + m_ref[...]
    m_t = jnp.maximum(inter, ig)
    ws = jnp.exp(ig - m_t)
    w_in = jnp.exp(inter - m_t)
    mo_ref[...] = m_t
    carry_s[:, 0:HEADS] = w_in

    cos_f = cos_ref[...]
    sin_s = sin_ref[...]
    for h in range(HEADS):
        lo = h * HEAD_DIM
        hs = slice(lo, lo + HEAD_DIM)
        vws_s[:, hs] = ut_ref[:, T_V + lo:T_V + lo + HEAD_DIM] * ws[:, h:h + 1]
        qr_s[:, hs] = _rope(un_ref[:, N_RQ + lo:N_RQ + lo + HEAD_DIM], cos_f, sin_s)
        kr_s[:, hs] = _rope(un_ref[:, N_RK + lo:N_RK + lo + HEAD_DIM], cos_f, sin_s) * k_scale

    heads = range(HEADS)
    seqs = range(bb)
    hsl = [slice(h * HEAD_DIM, (h + 1) * HEAD_DIM) for h in heads]
    seq_id = lax.broadcasted_iota(jnp.int32, (bb, HEAD_DIM), 0)
    for h in heads:
        q_h, k_h, vws_h = q_s[:, hsl[h]], k_s[:, hsl[h]], vws_s[:, hsl[h]]
        qr_h = qr_s[:, hsl[h]]
        krd_h = kr_s[:, hsl[h]] * rtab_ref[1:2, h:h + 1]
        rv_h = ut_ref[:, T_RV + h * HEAD_DIM:T_RV + (h + 1) * HEAD_DIM]
        c_old = [C_ref[b, h] for b in seqs]
        s_old = [S_ref[b, h] for b in seqs]
        q_c = [_dot_nt(q_h, c_old[b]) for b in seqs]
        q_st = [_dot(qr_h, s_old[b]) for b in seqs]
        d_c = [_dot_tn(jnp.where(seq_id == b, vws_h, 0.0), k_h) for b in seqs]
        d_s = [_dot_tn(jnp.where(seq_id == b, krd_h, 0.0), rv_h) for b in seqs]
        for b in seqs:
            row = slice(b, b + 1)
            qc_s[row, hsl[h]] = q_c[b][row]
            qs_s[row, hsl[h]] = q_st[b][row]
            Co_ref[b, h] = carry_s[row, h:h + 1] * c_old[b] + d_c[b]
            So_ref[b, h] = rtab_ref[2:3, h:h + 1] * s_old[b] + d_s[b]

    for h in range(HEADS):
        lo = h * HEAD_DIM
        hs = slice(lo, lo + HEAD_DIM)
        q_h, k_h, n_h = q_s[:, hs], k_s[:, hs], n_ref[:, hs]
        ws_h, w_in_h = ws[:, h:h + 1], w_in[:, h:h + 1]
        wts = ws_h * jnp.sum(q_h * k_h, axis=1, keepdims=True)
        num = wts * ut_ref[:, T_V + lo:T_V + lo + HEAD_DIM] + w_in_h * qc_s[:, hs]
        den = wts + w_in_h * jnp.sum(n_h * q_h, axis=1, keepdims=True)
        hm = num / jnp.maximum(jnp.abs(den), jnp.exp(-m_t[:, h:h + 1]))
        mo = ut_ref[:, T_O + lo:T_O + lo + HEAD_DIM]
        h_ref[:, hs] = _head_norm(hm) * gmh_ref[:, hs] * jax.nn.sigmoid(mo)
        no_ref[:, hs] = w_in_h * n_h + ws_h * k_h

        att = jnp.sum(qr_s[:, hs] * kr_s[:, hs], axis=1, keepdims=True) * rtab_ref[3:4, h:h + 1]
        o = (att * ut_ref[:, T_RV + lo:T_RV + lo + HEAD_DIM]
             + rtab_ref[0:1, h:h + 1] * qs_s[:, hs])
        rg = ut_ref[:, T_RG + lo:T_RG + lo + HEAD_DIM]
        h_ref[:, MIX + lo:MIX + lo + HEAD_DIM] = _head_norm(o) * grh_ref[:, hs] * _silu(rg)


def _mixer_sample(un, ut, gs, conv_t, C, n2, m, S, bg, w_conv, b_conv, g_mh, g_rh, cos_f, sin_s, rtab,
                  bb=8):
    B = un.shape[0]
    rows = lambda cols: pl.BlockSpec((bb, cols), lambda i: (i, 0))
    mats = pl.BlockSpec((bb, HEADS, HEAD_DIM, HEAD_DIM), lambda i: (i, 0, 0, 0))
    convs = pl.BlockSpec((CONV_W - 1, bb, 2 * MIX), lambda i: (0, i, 0))
    in_specs = [rows(N_COLS), rows(T_ROWS), rows(GT_ROWS), convs, mats, rows(MIX), rows(HEADS), mats,
                _const_spec((1, GT_ROWS)), _const_spec((CONV_W, 2 * MIX)), _const_spec((1, 2 * MIX)),
                _const_spec((1, MIX)), _const_spec((1, MIX)),
                _const_spec((1, HEAD_DIM)), _const_spec((1, HEAD_DIM)), _const_spec((8, HEAD_DIM))]
    out_specs = [rows(D_MODEL), convs, mats, rows(MIX), rows(HEADS), mats]
    out_shape = [jax.ShapeDtypeStruct((B, D_MODEL), F32),
                 jax.ShapeDtypeStruct((CONV_W - 1, B, 2 * MIX), F32),
                 jax.ShapeDtypeStruct((B, HEADS, HEAD_DIM, HEAD_DIM), F32),
                 jax.ShapeDtypeStruct((B, MIX), F32),
                 jax.ShapeDtypeStruct((B, HEADS), F32),
                 jax.ShapeDtypeStruct((B, HEADS, HEAD_DIM, HEAD_DIM), F32)]
    scratch = [pltpu.VMEM((bb, MIX), F32) for _ in range(7)] + [pltpu.VMEM((bb, HEAD_DIM), F32)]
    return pl.pallas_call(
        functools.partial(_mixer_sample_kernel, bb=bb),
        grid=(B // bb,),
        in_specs=in_specs, out_specs=out_specs, out_shape=out_shape,
        scratch_shapes=scratch,
        compiler_params=_params("parallel"),
    )(un, ut, gs, conv_t, C, n2, m, S, bg, w_conv, b_conv, g_mh, g_rh, cos_f, sin_s, rtab)


def _outq_sample_kernel(x_ref, h_ref, wout_ref, wcq_ref, gx_ref, x1_ref, q_ref):
    x1 = x_ref[...] + _dot(h_ref[...], wout_ref[...])
    x1_ref[...] = x1
    q_ref[...] = _dot(_rms(x1, gx_ref[...]), wcq_ref[...])


def _outq_sample(x, hcat, w_out, w_cq, g_x):
    rows = x.shape[0]
    full = _const_spec((rows, D_MODEL))
    return pl.pallas_call(
        _outq_sample_kernel,
        grid=(1,),
        in_specs=[full, full, _const_spec((D_MODEL, D_MODEL)), _const_spec((D_MODEL, D_MODEL)),
                  _const_spec((1, D_MODEL))],
        out_specs=[pl.BlockSpec((rows, D_MODEL), lambda i: (0, 0))] * 2,
        out_shape=[jax.ShapeDtypeStruct((rows, D_MODEL), F32)] * 2,
        compiler_params=_params("arbitrary"),
    )(x, hcat, w_out, w_cq, g_x)


def _attn_sample_kernel(q_ref, k_ref, v_ref, o_ref, *, ba):
    r_id = lax.broadcasted_iota(jnp.int32, (QROWS, KV_ROWS), 0)
    n_id = lax.broadcasted_iota(jnp.int32, (QROWS, KV_ROWS), 1)
    own = (n_id & 7) == (r_id >> 1) + 4 * (r_id & 1)
    low_half = (lax.broadcasted_iota(jnp.int32, (1, KV_ROWS), 1) & 4) == 0
    seqs = range(ba)
    z = [_dot_nt(q_ref[j], k_ref[j]) for j in seqs]
    zc = [jnp.sum(jnp.where(own, z[j], 0.0), axis=0, keepdims=True) for j in seqs]
    other = [jnp.where(low_half, pltpu.roll(zc[j], KV_ROWS - 4, 1), pltpu.roll(zc[j], 4, 1))
             for j in seqs]
    s = [jnp.where(own, (zc[j] + other[j]) * (X_HEAD_DIM ** -0.5), -jnp.inf) for j in seqs]
    m = [jnp.max(s[j], axis=-1, keepdims=True) for j in seqs]
    e = [jnp.exp(s[j] - m[j]) for j in seqs]
    den = [jnp.sum(e[j], axis=-1, keepdims=True) for j in seqs]
    for j in seqs:
        o_ref[j] = _dot(e[j] / den[j], v_ref[j])


def _attn_sample(q, ck, cv, ba=8):
    B = q.shape[0]
    kv = pl.BlockSpec((ba, KV_ROWS, 128), lambda i: (i, 0, 0))
    rows = pl.BlockSpec((ba, QROWS, 128), lambda i: (i, 0, 0))
    return pl.pallas_call(
        functools.partial(_attn_sample_kernel, ba=ba),
        grid=(B // ba,),
        in_specs=[rows, kv, kv],
        out_specs=rows,
        out_shape=jax.ShapeDtypeStruct((B, QROWS, 128), F32),
        compiler_params=_params("parallel"),
    )(q.reshape(B, QROWS, 128), ck, cv).reshape(B, D_MODEL)


def _kv_rows(cache):
    B = cache.shape[0]
    c5 = cache.reshape(B, N_MEM, X_HEADS, 2, 128)
    return jnp.transpose(c5, (0, 1, 3, 2, 4)).reshape(B, KV_ROWS, 128)


def _kv_from_rows(rows):
    B = rows.shape[0]
    r5 = rows.reshape(B, N_MEM, 2, X_HEADS, 128)
    return jnp.transpose(r5, (0, 1, 3, 2, 4)).reshape(1, B, N_MEM, X_HEADS, X_HEAD_DIM)


def _rope_tables(pos):
    half = HEAD_DIM // 2
    inv = ROPE_THETA ** (-np.arange(half, dtype=np.float64) / half)
    ang = np.asarray(pos, np.float64)[:, None] * inv[None, :]
    cos, sin = np.cos(ang), np.sin(ang)
    return (np.concatenate([cos, cos], axis=-1).astype(np.float32),
            np.concatenate([-sin, sin], axis=-1).astype(np.float32))


def _retention_tables(L):
    lg = np.log1p(-np.exp2(-5.0 - np.arange(HEADS, dtype=np.float64)))
    t = np.arange(L, dtype=np.float64)
    diff = t[:, None] - t[None, :]
    decay = np.where(diff >= 0, np.exp(lg[:, None, None] * np.maximum(diff, 0.0)), 0.0)
    q_dec = np.exp(lg[:, None] * (t + 1.0))
    k_dec = np.exp(lg[:, None] * (L - 1.0 - t))
    chunk_dec = np.exp(lg * L)
    return tuple(a.astype(np.float32) for a in (decay, q_dec, k_dec, chunk_dec))


def _lanes(a, n):
    xp = np if isinstance(a, np.ndarray) else jnp
    return xp.broadcast_to(a[..., None], a.shape + (n,))


def _pad_rows(a, rows):
    return np.pad(a, ((0, rows - a.shape[0]),) + ((0, 0),) * (a.ndim - 1))


def kernel(x_prompt, x_sample, cache_mem_k, cache_mem_v, state_mlstm_conv, state_mlstm_C, state_mlstm_n, state_mlstm_m, state_ret_S, mem_prompt, w_in, b_gate, w_conv, b_conv, g_mix, g_mhead, g_rhead, w_out, g_xattn, g_mem, w_ck, w_cv, w_cq, w_co, g_ffn, w_gate, w_up, w_down, g_final):
    Bp, Tp, _ = x_prompt.shape
    Bs = x_sample.shape[0]
    l = 0
    n_m = 4 * MIX
    wi = w_in[l]
    w_gates = wi[:, n_m:n_m + 2 * HEADS]
    w_ret = wi[:, n_m + 2 * HEADS:]
    row = lambda a: a.reshape(1, -1)
    bf = lambda a: a.astype(BF16)
    g_mix_r, g_mh_r, g_rh_r = row(g_mix[l]), row(g_mhead[l]), row(g_rhead[l])
    g_x_r, g_mem_r, g_ffn_r, g_fin_r = row(g_xattn[l]), row(g_mem[l]), row(g_ffn[l]), row(g_final)
    b_conv_r = row(b_conv[l])
    w_out_b, w_cq_b, w_co_b = bf(w_out[l]), bf(w_cq[l]), bf(w_co[l])
    w_gate_b, w_up_b, w_down_b = bf(w_gate[l]), bf(w_up[l]), bf(w_down[l])

    mk_b, mv_b, mk_rows, mv_rows = _memkv(mem_prompt, g_mem_r, bf(w_ck[l]), bf(w_cv[l]))
    wn = bf(jnp.concatenate([wi[:, :2 * MIX], w_ret[:, :2 * MIX]], axis=1))
    wt = bf(jnp.concatenate([wi[:, 2 * MIX:n_m], w_ret[:, 2 * MIX:]], axis=1).T)
    gate_rows = lambda a: jnp.concatenate(
        [a[:HEADS], jnp.zeros((8 - HEADS,) + a.shape[1:], F32),
         a[HEADS:], jnp.zeros((8 - HEADS,) + a.shape[1:], F32)], axis=0)
    wgt = bf(gate_rows(w_gates.T))
    bg_rows = gate_rows(b_gate[l][:, None])
    bgt = jnp.broadcast_to(bg_rows, (GT_ROWS, CHUNK))
    cos_p, sin_p = _rope_tables(np.arange(Tp))
    decay, q_dec, k_dec, chunk_dec = _retention_tables(CHUNK)
    hcat_p, conv_p, C_p, n_p, m_p, S_p = _mixer_prompt(
        x_prompt, wn, wt, wgt, g_mix_r, bgt, w_conv[l], b_conv_r,
        _lanes(g_mhead[l], CHUNK), _lanes(g_rhead[l], CHUNK), cos_p, sin_p,
        np.swapaxes(decay, 1, 2), _pad_rows(q_dec, 8), _lanes(k_dec, HEAD_DIM),
        _pad_rows(_lanes(chunk_dec, HEAD_DIM), 8))
    x1_p, o_p = _attn_prompt(x_prompt, hcat_p, mk_b, mv_b, w_out_b, w_cq_b, g_x_r)
    y_p = _ffn(x1_p.reshape(Bp * Tp, D_MODEL), o_p.reshape(Bp * Tp, D_MODEL),
               w_co_b, w_gate_b, w_up_b, w_down_b, g_ffn_r, g_fin_r, tm=512)

    xs = x_sample.reshape(Bs, D_MODEL)
    un_s, ut_s, gs_s = _inproj_sample(xs, g_mix_r, wn, wt, wgt)
    cos_s, sin_s = _rope_tables(PAST_LEN + np.arange(1))
    decay1, q_dec1, k_dec1, chunk_dec1 = _retention_tables(1)
    rtab = np.zeros((8, HEAD_DIM), np.float32)
    rtab[:4, :HEADS] = np.stack([q_dec1[:, 0], k_dec1[:, 0], chunk_dec1, decay1[:, 0, 0]])
    hcat_s, conv_s, C_s, n_s, m_s, S_s = _mixer_sample(
        un_s, ut_s, gs_s, jnp.transpose(state_mlstm_conv[l], (1, 0, 2)), state_mlstm_C[l],
        state_mlstm_n[l].reshape(Bs, MIX), state_mlstm_m[l], state_ret_S[l],
        bg_rows.reshape(1, GT_ROWS), w_conv[l], b_conv_r, g_mh_r, g_rh_r, cos_s, sin_s, rtab)
    x1_s, q_s = _outq_sample(xs, hcat_s, w_out_b, w_cq_b, g_x_r)
    o_s = _attn_sample(q_s, _kv_rows(cache_mem_k[l]), _kv_rows(cache_mem_v[l]))
    y_s = _ffn(x1_s, o_s, w_co_b, w_gate_b, w_up_b, w_down_b, g_ffn_r, g_fin_r, tm=Bs)

    return (y_p.reshape(Bp, Tp, D_MODEL), y_s.reshape(Bs, 1, D_MODEL),
            _kv_from_rows(mk_rows), _kv_from_rows(mv_rows),
            conv_p[None], C_p[None], n_p[None], m_p[None, :, :HEADS, 0], S_p[None],
            jnp.transpose(conv_s, (1, 0, 2))[None], C_s[None],
            n_s.reshape(1, Bs, HEADS, HEAD_DIM), m_s[None], S_s[None])
```

```python
import functools

import jax
import jax.numpy as jnp
import numpy as np
from jax import lax
from jax.experimental import pallas as pl
from jax.experimental.pallas import tpu as pltpu

F32 = jnp.float32
BF16 = jnp.bfloat16

D_MODEL = 1024
HEADS = 4
HEAD_DIM = 128
MIX = HEADS * HEAD_DIM
CONV_W = 4
CHUNK = 128
N_MEM = 256
X_HEADS = 4
X_HEAD_DIM = 256
D_FF = 2816
ROPE_THETA = 10000.0
EPS = 1e-6
PAST_LEN = 16384
KV_ROWS = N_MEM * 2 * X_HEADS
QROWS = 2 * X_HEADS

N_QK, N_RQ, N_RK, N_COLS = 0, 1024, 1536, 2048
T_V, T_O, T_RV, T_RG, T_ROWS = 0, 512, 1024, 1536, 2048
GT_ROWS = 16
UOFF = 8
ROW_GROUPS = 2

V7X_VMEM_LIMIT = 56 * 1024 * 1024

NT_DIMS = (((1,), (1,)), ((), ()))
TN_DIMS = (((0,), (0,)), ((), ()))


def _dot(a, b):
    return jnp.dot(a.astype(BF16), b.astype(BF16), preferred_element_type=F32)


def _dot_nt(a, b):
    return lax.dot_general(a.astype(BF16), b.astype(BF16), NT_DIMS, preferred_element_type=F32)


def _dot_tn(a, b):
    return lax.dot_general(a.astype(BF16), b.astype(BF16), TN_DIMS, preferred_element_type=F32)


def _rms(x, g):
    return x * lax.rsqrt(jnp.mean(x * x, axis=-1, keepdims=True) + EPS) * g


def _head_norm(h):
    return h * lax.rsqrt(jnp.mean(h * h, axis=-1, keepdims=True) + EPS)


def _silu(x):
    return x * jax.nn.sigmoid(x)


def _log_sigmoid(x):
    return jnp.minimum(x, 0.0) - jnp.log1p(jnp.exp(-jnp.abs(x)))


def _rope(x, cos_full, sin_signed):
    return x * cos_full + pltpu.roll(x, HEAD_DIM // 2, 1) * sin_signed


def _const_spec(shape):
    zeros = (0,) * len(shape)
    return pl.BlockSpec(shape, lambda *_: zeros, pipeline_mode=pl.Buffered(1))


def _params(*sem):
    return pltpu.CompilerParams(dimension_semantics=sem, vmem_limit_bytes=V7X_VMEM_LIMIT)


def _memkv_kernel(mem_ref, g_ref, wk_ref, wv_ref, kb_ref, vb_ref, krows_ref, vrows_ref):
    mn = _rms(mem_ref[...], g_ref[...]).astype(BF16)
    for w_ref, b_ref, rows_ref in ((wk_ref, kb_ref, krows_ref), (wv_ref, vb_ref, vrows_ref)):
        proj = jnp.dot(mn, w_ref[...], preferred_element_type=F32)
        b_ref[...] = proj.astype(BF16)
        for h in range(X_HEADS):
            for c in range(2):
                lane0 = h * X_HEAD_DIM + c * 128
                rows_ref[pl.ds(c * X_HEADS + h, N_MEM, stride=QROWS), :] = proj[:, lane0:lane0 + 128]


def _memkv(mem, g_mem, wk, wv):
    B = mem.shape[0]
    tok = pl.BlockSpec((None, N_MEM, D_MODEL), lambda b: (b, 0, 0))
    rows = pl.BlockSpec((None, KV_ROWS, 128), lambda b: (b, 0, 0))
    return pl.pallas_call(
        _memkv_kernel,
        grid=(B,),
        in_specs=[tok, _const_spec((1, D_MODEL)),
                  _const_spec((D_MODEL, D_MODEL)), _const_spec((D_MODEL, D_MODEL))],
        out_specs=[tok, tok, rows, rows],
        out_shape=[jax.ShapeDtypeStruct((B, N_MEM, D_MODEL), BF16)] * 2
        + [jax.ShapeDtypeStruct((B, KV_ROWS, 128), F32)] * 2,
        compiler_params=_params("parallel"),
    )(mem, g_mem, wk, wv)


def _mixer_prompt_kernel(x_ref, wn_ref, wt_ref, wgt_ref, gmix_ref, bgt_ref, wconv_ref, bconv_ref,
                         gmh_ref, grh_ref, cos_ref, sin_ref, decay_ref, qdec_ref, kdec_ref, cdec_ref,
                         h_ref, conv_ref, C_ref, n_ref, m_ref, S_ref,
                         un_ref, ut_ref, gt_ref, st_ref, *, tt):
    L = CHUNK
    t = pl.program_id(1)
    chunks = range(tt // L)

    @pl.when(t == 0)
    def _():
        C_ref[...] = jnp.zeros_like(C_ref)
        st_ref[...] = jnp.zeros_like(st_ref)
        n_ref[...] = jnp.zeros_like(n_ref)
        m_ref[...] = jnp.zeros_like(m_ref)
        un_ref[0:UOFF, :] = jnp.zeros((UOFF, N_COLS), F32)

    @pl.when(t > 0)
    def _():
        un_ref[0:UOFF, N_QK:N_RQ] = un_ref[tt:tt + UOFF, N_QK:N_RQ]

    k_scale = HEAD_DIM ** -0.5
    xn = _rms(x_ref[...], gmix_ref[...]).astype(BF16)
    for c0 in range(0, N_COLS, 2 * MIX):
        un_ref[UOFF:UOFF + tt, c0:c0 + 2 * MIX] = jnp.dot(xn, wn_ref[:, c0:c0 + 2 * MIX],
                                                          preferred_element_type=F32)
    for r0 in range(0, T_ROWS, MIX):
        res = lax.dot_general(wt_ref[r0:r0 + MIX, :], xn, NT_DIMS, preferred_element_type=F32)
        for c in chunks:
            ut_ref[c, r0:r0 + MIX, :] = res[:, c * L:(c + 1) * L]
    gates_t = lax.dot_general(wgt_ref[...], xn, NT_DIMS, preferred_element_type=F32)
    for c in chunks:
        gt_ref[c] = gates_t[:, c * L:(c + 1) * L]

    src_id = lax.broadcasted_iota(jnp.int32, (L, L), 0)
    tgt_id = lax.broadcasted_iota(jnp.int32, (L, L), 1)
    causal = src_id <= tgt_id
    triu_bf = jnp.where(causal, 1.0, 0.0).astype(BF16)
    heads = range(HEADS)
    hcol = lambda base, h: slice(base + h * HEAD_DIM, base + (h + 1) * HEAD_DIM)

    def chunk_body(c, carry_unused):
        r0 = pl.multiple_of(c * L, L)
        rows = pl.ds(UOFF + r0, L)
        trows = pl.ds(r0, L)
        g_t = gt_ref[c] + bgt_ref[...]
        ig = g_t[0:8]
        lf = _log_sigmoid(g_t[8:16])
        lf_hi = lf.astype(BF16)
        r1 = lf - lf_hi.astype(F32)
        lf_mid = r1.astype(BF16)
        lf_lo = (r1 - lf_mid.astype(F32)).astype(BF16)
        bc = (jnp.dot(lf_hi, triu_bf, preferred_element_type=F32)
              + jnp.dot(lf_mid, triu_bf, preferred_element_type=F32)
              + jnp.dot(lf_lo, triu_bf, preferred_element_type=F32))
        m_prev = m_ref[...]
        inter = bc + m_prev
        b_last = bc[:, L - 1:L]
        g_w = b_last - bc + ig
        m_new = jnp.maximum(b_last + m_prev, jnp.max(g_w, axis=1, keepdims=True))
        ws = jnp.exp(g_w - m_new)
        carry = jnp.exp(b_last + m_prev - m_new)
        a_n = jnp.concatenate([ig - bc, jnp.zeros((L - 8, L), F32)], axis=0).T

        def conv_act(col):
            win = un_ref[pl.ds(r0, L + UOFF), col:col + HEAD_DIM]
            acc = bconv_ref[:, col:col + HEAD_DIM]
            for j in range(CONV_W):
                s0 = UOFF - (CONV_W - 1) + j
                acc = acc + win[s0:s0 + L] * wconv_ref[j:j + 1, col:col + HEAD_DIM]
            return _silu(acc)

        cos_f, sin_s = cos_ref[trows, :], sin_ref[trows, :]
        q = [conv_act(N_QK + h * HEAD_DIM) for h in heads]
        kb = [(conv_act(N_QK + MIX + h * HEAD_DIM) * k_scale).astype(BF16) for h in heads]
        qb = [a.astype(BF16) for a in q]
        rqb = [_rope(un_ref[rows, hcol(N_RQ, h)], cos_f, sin_s).astype(BF16) for h in heads]
        rk = [_rope(un_ref[rows, hcol(N_RK, h)], cos_f, sin_s) * k_scale for h in heads]
        v_t = [ut_ref[c, hcol(T_V, h), :] for h in heads]
        rvb = [ut_ref[c, hcol(T_RV, h), :].astype(BF16) for h in heads]
        c_old = [C_ref[h] for h in heads]
        s_old = [st_ref[h] for h in heads]
        n_old = [n_ref[h:h + 1, :] for h in heads]
        qk = [_dot_nt(kb[h], qb[h]) for h in heads]
        att = [_dot_nt(rk[h], rqb[h]) for h in heads]
        c_q = [_dot_nt(c_old[h], qb[h]) for h in heads]
        s_q = [_dot_nt(s_old[h], rqb[h]) for h in heads]
        n_q = [_dot_nt(jnp.broadcast_to(n_old[h], (8, HEAD_DIM)), qb[h])[0:1] for h in heads]
        d_c = [_dot(v_t[h] * ws[h:h + 1, :], kb[h]) for h in heads]
        d_n = [_dot(jnp.broadcast_to(ws[h:h + 1, :], (8, L)), kb[h])[0:1] for h in heads]
        d_s = [_dot(rvb[h], rk[h] * kdec_ref[h]) for h in heads]
        m_t, w_in, wts = [], [], []
        for h in heads:
            dmat = a_n[:, h:h + 1] + bc[h:h + 1, :]
            dmat = jnp.where(causal, dmat, -jnp.inf)
            m_t.append(jnp.maximum(inter[h:h + 1, :], jnp.max(dmat, axis=0, keepdims=True)))
            wts.append(jnp.exp(dmat - m_t[h]) * qk[h])
            w_in.append(jnp.exp(inter[h:h + 1, :] - m_t[h]))
        att_w = [att[h] * decay_ref[h] for h in heads]
        v_p = [_dot(v_t[h], wts[h]) for h in heads]
        v_a = [_dot(rvb[h], att_w[h]) for h in heads]
        for h in heads:
            num = v_p[h] + w_in[h] * c_q[h]
            den = jnp.sum(wts[h], axis=0, keepdims=True) + w_in[h] * n_q[h]
            hm = num / jnp.maximum(jnp.abs(den), jnp.exp(-m_t[h]))
            hm = hm * lax.rsqrt(jnp.mean(hm * hm, axis=0, keepdims=True) + EPS)
            hm = hm * gmh_ref[hcol(0, h), :] * jax.nn.sigmoid(ut_ref[c, hcol(T_O, h), :])
            h_ref[trows, hcol(0, h)] = hm.T.astype(BF16)
            carry_h = carry[h:h + 1, :]
            C_ref[h] = carry_h * c_old[h] + d_c[h]
            n_ref[h:h + 1, :] = carry_h * n_old[h] + d_n[h]
            o = v_a[h] + qdec_ref[h:h + 1, :] * s_q[h]
            st_ref[h] = cdec_ref[h:h + 1, :] * s_old[h] + d_s[h]
            hr = o * lax.rsqrt(jnp.mean(o * o, axis=0, keepdims=True) + EPS)
            hr = hr * grh_ref[hcol(0, h), :] * _silu(ut_ref[c, hcol(T_RG, h), :])
            h_ref[trows, hcol(MIX, h)] = hr.T.astype(BF16)

        m_ref[...] = m_new
        return carry_unused

    lax.fori_loop(0, tt // L, chunk_body, 0)

    @pl.when(t == pl.num_programs(1) - 1)
    def _():
        conv_ref[...] = un_ref[UOFF + tt - (CONV_W - 1):UOFF + tt, N_QK:N_RQ]
        for h in heads:
            S_ref[h] = st_ref[h].T


def _mixer_prompt(x, wn, wt, wgt, g_mix, bgt, w_conv, b_conv, gmh_cols, grh_cols, cos_f, sin_s,
                  decay_t, qdec_rows, kdec_cols, cdec_rows, tt=512):
    B, T, _ = x.shape
    tile = lambda b, t: (b, t, 0)
    per_b3 = lambda b, t: (b, 0, 0)
    per_b4 = lambda b, t: (b, 0, 0, 0)
    in_specs = [
        pl.BlockSpec((None, tt, D_MODEL), tile),
        _const_spec((D_MODEL, N_COLS)), _const_spec((T_ROWS, D_MODEL)), _const_spec((GT_ROWS, D_MODEL)),
        _const_spec((1, D_MODEL)), _const_spec((GT_ROWS, CHUNK)),
        _const_spec((CONV_W, 2 * MIX)), _const_spec((1, 2 * MIX)),
        _const_spec((MIX, CHUNK)), _const_spec((MIX, CHUNK)),
        pl.BlockSpec((tt, HEAD_DIM), lambda b, t: (t, 0)),
        pl.BlockSpec((tt, HEAD_DIM), lambda b, t: (t, 0)),
        _const_spec((HEADS, CHUNK, CHUNK)),
        _const_spec((8, CHUNK)), _const_spec((HEADS, CHUNK, HEAD_DIM)), _const_spec((8, HEAD_DIM)),
    ]
    out_specs = [
        pl.BlockSpec((None, tt, D_MODEL), tile),
        pl.BlockSpec((None, CONV_W - 1, 2 * MIX), per_b3),
        pl.BlockSpec((None, HEADS, HEAD_DIM, HEAD_DIM), per_b4),
        pl.BlockSpec((None, HEADS, HEAD_DIM), per_b3),
        pl.BlockSpec((None, 8, CHUNK), per_b3),
        pl.BlockSpec((None, HEADS, HEAD_DIM, HEAD_DIM), per_b4),
    ]
    out_shape = [
        jax.ShapeDtypeStruct((B, T, D_MODEL), BF16),
        jax.ShapeDtypeStruct((B, CONV_W - 1, 2 * MIX), F32),
        jax.ShapeDtypeStruct((B, HEADS, HEAD_DIM, HEAD_DIM), F32),
        jax.ShapeDtypeStruct((B, HEADS, HEAD_DIM), F32),
        jax.ShapeDtypeStruct((B, 8, CHUNK), F32),
        jax.ShapeDtypeStruct((B, HEADS, HEAD_DIM, HEAD_DIM), F32),
    ]
    n_chunks = tt // CHUNK
    scratch = [
        pltpu.VMEM((tt + UOFF, N_COLS), F32),
        pltpu.VMEM((n_chunks, T_ROWS, CHUNK), F32),
        pltpu.VMEM((n_chunks, GT_ROWS, CHUNK), F32),
        pltpu.VMEM((HEADS, HEAD_DIM, HEAD_DIM), F32),
    ]
    return pl.pallas_call(
        functools.partial(_mixer_prompt_kernel, tt=tt),
        grid=(B, T // tt),
        in_specs=in_specs, out_specs=out_specs, out_shape=out_shape,
        scratch_shapes=scratch,
        compiler_params=_params("parallel", "arbitrary"),
    )(x, wn, wt, wgt, g_mix, bgt, w_conv, b_conv, gmh_cols, grh_cols, cos_f, sin_s,
      decay_t, qdec_rows, kdec_cols, cdec_rows)


def _attn_prompt_kernel(x_ref, h_ref, k_ref, v_ref, wout_ref, wcq_ref, gx_ref, x1_ref, o_ref):
    tm = x_ref.shape[0]
    rows = [slice(r, r + tm // ROW_GROUPS) for r in range(0, tm, tm // ROW_GROUPS)]
    x1 = [x_ref[r, :] + jnp.dot(h_ref[r, :], wout_ref[...], preferred_element_type=F32)
          for r in rows]
    for r, a in zip(rows, x1):
        x1_ref[r, :] = a
    xq = [_rms(a, gx_ref[...]).astype(BF16) for a in x1]
    q = [jnp.dot(a, wcq_ref[...], preferred_element_type=F32).astype(BF16) for a in xq]
    sl = [slice(h * X_HEAD_DIM, (h + 1) * X_HEAD_DIM) for h in range(X_HEADS)]
    items = [(g, h) for g in range(ROW_GROUPS) for h in range(X_HEADS)]
    s = [_dot_nt(q[g][:, sl[h]], k_ref[:, sl[h]]) * (X_HEAD_DIM ** -0.5) for g, h in items]
    e = [jnp.exp(a - jnp.max(a, axis=-1, keepdims=True)) for a in s]
    p = [a / jnp.sum(a, axis=-1, keepdims=True) for a in e]
    for (g, h), a in zip(items, p):
        o_ref[rows[g], sl[h]] = _dot(a, v_ref[:, sl[h]]).astype(BF16)


def _attn_prompt(x, hcat, mk, mv, w_out, w_cq, g_x, tm=512):
    B, T, _ = x.shape
    tile = pl.BlockSpec((None, tm, D_MODEL), lambda b, t: (b, t, 0))
    kv = pl.BlockSpec((None, N_MEM, D_MODEL), lambda b, t: (b, 0, 0))
    return pl.pallas_call(
        _attn_prompt_kernel,
        grid=(B, T // tm),
        in_specs=[tile, tile, kv, kv, _const_spec((D_MODEL, D_MODEL)),
                  _const_spec((D_MODEL, D_MODEL)), _const_spec((1, D_MODEL))],
        out_specs=[tile, tile],
        out_shape=[jax.ShapeDtypeStruct((B, T, D_MODEL), F32),
                   jax.ShapeDtypeStruct((B, T, D_MODEL), BF16)],
        compiler_params=_params("parallel", "parallel"),
    )(x, hcat, mk, mv, w_out, w_cq, g_x)


FF_CHUNK = D_FF // 2


def _ffn_kernel(x1_ref, o_ref, wco_ref, wg_ref, wu_ref, wd_ref, gffn_ref, gfin_ref, y_ref, *,
                n_groups):
    tm = x1_ref.shape[0]
    rows = [slice(r, r + tm // n_groups) for r in range(0, tm, tm // n_groups)]
    acc = [x1_ref[r, :] + _dot(o_ref[r, :], wco_ref[...]) for r in rows]
    hf = [_rms(a, gffn_ref[...]).astype(BF16) for a in acc]
    for c0 in range(0, D_FF, FF_CHUNK):
        gate = [jnp.dot(h, wg_ref[:, c0:c0 + FF_CHUNK], preferred_element_type=F32) for h in hf]
        up = [jnp.dot(h, wu_ref[:, c0:c0 + FF_CHUNK], preferred_element_type=F32) for h in hf]
        act = [_silu(g) * u for g, u in zip(gate, up)]
        acc = [a + _dot(p, wd_ref[c0:c0 + FF_CHUNK, :]) for a, p in zip(acc, act)]
    for r, a in zip(rows, acc):
        y_ref[r, :] = _rms(a, gfin_ref[...])


def _ffn(x1, o, w_co, w_gate, w_up, w_down, g_ffn, g_final, tm):
    rows = x1.shape[0]
    row_spec = pl.BlockSpec((tm, D_MODEL), lambda i: (i, 0))
    return pl.pallas_call(
        functools.partial(_ffn_kernel, n_groups=ROW_GROUPS if tm >= 512 else 1),
        grid=(rows // tm,),
        in_specs=[row_spec, row_spec, _const_spec((D_MODEL, D_MODEL)),
                  _const_spec((D_MODEL, D_FF)), _const_spec((D_MODEL, D_FF)),
                  _const_spec((D_FF, D_MODEL)), _const_spec((1, D_MODEL)), _const_spec((1, D_MODEL))],
        out_specs=row_spec,
        out_shape=jax.ShapeDtypeStruct((rows, D_MODEL), F32),
        compiler_params=_params("parallel"),
    )(x1, o, w_co, w_gate, w_up, w_down, g_ffn, g_final)


def _inproj_sample_kernel(x_ref, g_ref, wn_ref, wt_ref, wgt_ref, un_ref, ut_ref, gs_ref):
    xn = _rms(x_ref[...], g_ref[...]).astype(BF16)
    un_ref[...] = jnp.dot(xn, wn_ref[...], preferred_element_type=F32)
    ut_ref[...] = lax.dot_general(xn, wt_ref[...], NT_DIMS, preferred_element_type=F32)
    gs_ref[...] = lax.dot_general(xn, wgt_ref[...], NT_DIMS, preferred_element_type=F32)


def _inproj_sample(x, g_mix, wn, wt, wgt):
    rows = x.shape[0]
    full = lambda cols: pl.BlockSpec((rows, cols), lambda i: (0, 0))
    return pl.pallas_call(
        _inproj_sample_kernel,
        grid=(1,),
        in_specs=[_const_spec((rows, D_MODEL)), _const_spec((1, D_MODEL)),
                  _const_spec((D_MODEL, N_COLS)), _const_spec((T_ROWS, D_MODEL)),
                  _const_spec((GT_ROWS, D_MODEL))],
        out_specs=[full(N_COLS), full(T_ROWS), full(GT_ROWS)],
        out_shape=[jax.ShapeDtypeStruct((rows, N_COLS), F32),
                   jax.ShapeDtypeStruct((rows, T_ROWS), F32),
                   jax.ShapeDtypeStruct((rows, GT_ROWS), F32)],
        compiler_params=_params("arbitrary"),
    )(x, g_mix, wn, wt, wgt)


def _mixer_sample_kernel(un_ref, ut_ref, gs_ref, conv_ref, C_ref, n_ref, m_ref, S_ref, bg_ref,
                         wconv_ref, bconv_ref, gmh_ref, grh_ref, cos_ref, sin_ref, rtab_ref,
                         h_ref, convo_ref, Co_ref, no_ref, mo_ref, So_ref,
                         q_s, k_s, vws_s, qr_s, kr_s, qc_s, qs_s, carry_s, *, bb):
    k_scale = HEAD_DIM ** -0.5
    uqk = un_ref[:, N_QK:N_RQ]
    conv = (bconv_ref[...] + wconv_ref[0:1, :] * conv_ref[0] + wconv_ref[1:2, :] * conv_ref[1]
            + wconv_ref[2:3, :] * conv_ref[2] + wconv_ref[3:4, :] * uqk)
    convo_ref[0] = conv_ref[1]
    convo_ref[1] = conv_ref[2]
    convo_ref[2] = uqk
    qk_act = _silu(conv)
    q_s[...] = qk_act[:, 0:MIX]
    k_s[...] = qk_act[:, MIX:2 * MIX] * k_scale

    gates = gs_ref[...] + bg_ref[...]
    ig = gates[:, 0:HEADS]
    lf = _log_sigmoid(gates[:, 8:8 + HEADS])
    inter = lf + m_ref[...]
    m_t = jnp.maximum(inter, ig)
    ws = jnp.exp(ig - m_t)
    w_in = jnp.exp(inter - m_t)
    mo_ref[...] = m_t
    carry_s[:, 0:HEADS] = w_in

    cos_f = cos_ref[...]
    sin_s = sin_ref[...]
    for h in range(HEADS):
        lo = h * HEAD_DIM
        hs = slice(lo, lo + HEAD_DIM)
        vws_s[:, hs] = ut_ref[:, T_V + lo:T_V + lo + HEAD_DIM] * ws[:, h:h + 1]
        qr_s[:, hs] = _rope(un_ref[:, N_RQ + lo:N_RQ + lo + HEAD_DIM], cos_f, sin_s)
        kr_s[:, hs] = _rope(un_ref[:, N_RK + lo:N_RK + lo + HEAD_DIM], cos_f, sin_s) * k_scale

    heads = range(HEADS)
    seqs = range(bb)
    hsl = [slice(h * HEAD_DIM, (h + 1) * HEAD_DIM) for h in heads]
    seq_id = lax.broadcasted_iota(jnp.int32, (bb, HEAD_DIM), 0)
    for h in heads:
        q_h, k_h, vws_h = q_s[:, hsl[h]], k_s[:, hsl[h]], vws_s[:, hsl[h]]
        qr_h = qr_s[:, hsl[h]]
        krd_h = kr_s[:, hsl[h]] * rtab_ref[1:2, h:h + 1]
        rv_h = ut_ref[:, T_RV + h * HEAD_DIM:T_RV + (h + 1) * HEAD_DIM]
        c_old = [C_ref[b, h] for b in seqs]
        s_old = [S_ref[b, h] for b in seqs]
        q_c = [_dot_nt(q_h, c_old[b]) for b in seqs]
        q_st = [_dot(qr_h, s_old[b]) for b in seqs]
        d_c = [_dot_tn(jnp.where(seq_id == b, vws_h, 0.0), k_h) for b in seqs]
        d_s = [_dot_tn(jnp.where(seq_id == b, krd_h, 0.0), rv_h) for b in seqs]
        for b in seqs:
            row = slice(b, b + 1)
            qc_s[row, hsl[h]] = q_c[b][row]
            qs_s[row, hsl[h]] = q_st[b][row]
            Co_ref[b, h] = carry_s[row, h:h + 1] * c_old[b] + d_c[b]
            So_ref[b, h] = rtab_ref[2:3, h:h + 1] * s_old[b] + d_s[b]

    for h in range(HEADS):
        lo = h * HEAD_DIM
        hs = slice(lo, lo + HEAD_DIM)
        q_h, k_h, n_h = q_s[:, hs], k_s[:, hs], n_ref[:, hs]
        ws_h, w_in_h = ws[:, h:h + 1], w_in[:, h:h + 1]
        wts = ws_h * jnp.sum(q_h * k_h, axis=1, keepdims=True)
        num = wts * ut_ref[:, T_V + lo:T_V + lo + HEAD_DIM] + w_in_h * qc_s[:, hs]
        den = wts + w_in_h * jnp.sum(n_h * q_h, axis=1, keepdims=True)
        hm = num / jnp.maximum(jnp.abs(den), jnp.exp(-m_t[:, h:h + 1]))
        mo = ut_ref[:, T_O + lo:T_O + lo + HEAD_DIM]
        h_ref[:, hs] = _head_norm(hm) * gmh_ref[:, hs] * jax.nn.sigmoid(mo)
        no_ref[:, hs] = w_in_h * n_h + ws_h * k_h

        att = jnp.sum(qr_s[:, hs] * kr_s[:, hs], axis=1, keepdims=True) * rtab_ref[3:4, h:h + 1]
        o = (att * ut_ref[:, T_RV + lo:T_RV + lo + HEAD_DIM]
             + rtab_ref[0:1, h:h + 1] * qs_s[:, hs])
        rg = ut_ref[:, T_RG + lo:T_RG + lo + HEAD_DIM]
        h_ref[:, MIX + lo:MIX + lo + HEAD_DIM] = _head_norm(o) * grh_ref[:, hs] * _silu(rg)


def _mixer_sample(un, ut, gs, conv_t, C, n2, m, S, bg, w_conv, b_conv, g_mh, g_rh, cos_f, sin_s, rtab,
                  bb=8):
    B = un.shape[0]
    rows = lambda cols: pl.BlockSpec((bb, cols), lambda i: (i, 0))
    mats = pl.BlockSpec((bb, HEADS, HEAD_DIM, HEAD_DIM), lambda i: (i, 0, 0, 0))
    convs = pl.BlockSpec((CONV_W - 1, bb, 2 * MIX), lambda i: (0, i, 0))
    in_specs = [rows(N_COLS), rows(T_ROWS), rows(GT_ROWS), convs, mats, rows(MIX), rows(HEADS), mats,
                _const_spec((1, GT_ROWS)), _const_spec((CONV_W, 2 * MIX)), _const_spec((1, 2 * MIX)),
                _const_spec((1, MIX)), _const_spec((1, MIX)),
                _const_spec((1, HEAD_DIM)), _const_spec((1, HEAD_DIM)), _const_spec((8, HEAD_DIM))]
    out_specs = [rows(D_MODEL), convs, mats, rows(MIX), rows(HEADS), mats]
    out_shape = [jax.ShapeDtypeStruct((B, D_MODEL), F32),
                 jax.ShapeDtypeStruct((CONV_W - 1, B, 2 * MIX), F32),
                 jax.ShapeDtypeStruct((B, HEADS, HEAD_DIM, HEAD_DIM), F32),
                 jax.ShapeDtypeStruct((B, MIX), F32),
                 jax.ShapeDtypeStruct((B, HEADS), F32),
                 jax.ShapeDtypeStruct((B, HEADS, HEAD_DIM, HEAD_DIM), F32)]
    scratch = [pltpu.VMEM((bb, MIX), F32) for _ in range(7)] + [pltpu.VMEM((bb, HEAD_DIM), F32)]
    return pl.pallas_call(
        functools.partial(_mixer_sample_kernel, bb=bb),
        grid=(B // bb,),
        in_specs=in_specs, out_specs=out_specs, out_shape=out_shape,
        scratch_shapes=scratch,
        compiler_params=_params("parallel"),
    )(un, ut, gs, conv_t, C, n2, m, S, bg, w_conv, b_conv, g_mh, g_rh, cos_f, sin_s, rtab)


def _outq_sample_kernel(x_ref, h_ref, wout_ref, wcq_ref, gx_ref, x1_ref, q_ref):
    x1 = x_ref[...] + _dot(h_ref[...], wout_ref[...])
    x1_ref[...] = x1
    q_ref[...] = _dot(_rms(x1, gx_ref[...]), wcq_ref[...])


def _outq_sample(x, hcat, w_out, w_cq, g_x):
    rows = x.shape[0]
    full = _const_spec((rows, D_MODEL))
    return pl.pallas_call(
        _outq_sample_kernel,
        grid=(1,),
        in_specs=[full, full, _const_spec((D_MODEL, D_MODEL)), _const_spec((D_MODEL, D_MODEL)),
                  _const_spec((1, D_MODEL))],
        out_specs=[pl.BlockSpec((rows, D_MODEL), lambda i: (0, 0))] * 2,
        out_shape=[jax.ShapeDtypeStruct((rows, D_MODEL), F32)] * 2,
        compiler_params=_params("arbitrary"),
    )(x, hcat, w_out, w_cq, g_x)


def _attn_sample_kernel(q_ref, k_ref, v_ref, o_ref, *, ba):
    r_id = lax.broadcasted_iota(jnp.int32, (QROWS, KV_ROWS), 0)
    n_id = lax.broadcasted_iota(jnp.int32, (QROWS, KV_ROWS), 1)
    own = (n_id & 7) == (r_id >> 1) + 4 * (r_id & 1)
    low_half = (lax.broadcasted_iota(jnp.int32, (1, KV_ROWS), 1) & 4) == 0
    seqs = range(ba)
    z = [_dot_nt(q_ref[j], k_ref[j]) for j in seqs]
    zc = [jnp.sum(jnp.where(own, z[j], 0.0), axis=0, keepdims=True) for j in seqs]
    other = [jnp.where(low_half, pltpu.roll(zc[j], KV_ROWS - 4, 1), pltpu.roll(zc[j], 4, 1))
             for j in seqs]
    s = [jnp.where(own, (zc[j] + other[j]) * (X_HEAD_DIM ** -0.5), -jnp.inf) for j in seqs]
    m = [jnp.max(s[j], axis=-1, keepdims=True) for j in seqs]
    e = [jnp.exp(s[j] - m[j]) for j in seqs]
    den = [jnp.sum(e[j], axis=-1, keepdims=True) for j in seqs]
    for j in seqs:
        o_ref[j] = _dot(e[j] / den[j], v_ref[j])


def _attn_sample(q, ck, cv, ba=8):
    B = q.shape[0]
    kv = pl.BlockSpec((ba, KV_ROWS, 128), lambda i: (i, 0, 0))
    rows = pl.BlockSpec((ba, QROWS, 128), lambda i: (i, 0, 0))
    return pl.pallas_call(
        functools.partial(_attn_sample_kernel, ba=ba),
        grid=(B // ba,),
        in_specs=[rows, kv, kv],
        out_specs=rows,
        out_shape=jax.ShapeDtypeStruct((B, QROWS, 128), F32),
        compiler_params=_params("parallel"),
    )(q.reshape(B, QROWS, 128), ck, cv).reshape(B, D_MODEL)


def _kv_rows(cache):
    B = cache.shape[0]
    c5 = cache.reshape(B, N_MEM, X_HEADS, 2, 128)
    return jnp.transpose(c5, (0, 1, 3, 2, 4)).reshape(B, KV_ROWS, 128)


def _kv_from_rows(rows):
    B = rows.shape[0]
    r5 = rows.reshape(B, N_MEM, 2, X_HEADS, 128)
    return jnp.transpose(r5, (0, 1, 3, 2, 4)).reshape(1, B, N_MEM, X_HEADS, X_HEAD_DIM)


def _rope_tables(pos):
    half = HEAD_DIM // 2
    inv = ROPE_THETA ** (-np.arange(half, dtype=np.float64) / half)
    ang = np.asarray(pos, np.float64)[:, None] * inv[None, :]
    cos, sin = np.cos(ang), np.sin(ang)
    return (np.concatenate([cos, cos], axis=-1).astype(np.float32),
            np.concatenate([-sin, sin], axis=-1).astype(np.float32))


def _retention_tables(L):
    lg = np.log1p(-np.exp2(-5.0 - np.arange(HEADS, dtype=np.float64)))
    t = np.arange(L, dtype=np.float64)
    diff = t[:, None] - t[None, :]
    decay = np.where(diff >= 0, np.exp(lg[:, None, None] * np.maximum(diff, 0.0)), 0.0)
    q_dec = np.exp(lg[:, None] * (t + 1.0))
    k_dec = np.exp(lg[:, None] * (L - 1.0 - t))
    chunk_dec = np.exp(lg * L)
    return tuple(a.astype(np.float32) for a in (decay, q_dec, k_dec, chunk_dec))


def _lanes(a, n):
    xp = np if isinstance(a, np.ndarray) else jnp
    return xp.broadcast_to(a[..., None], a.shape + (n,))


def _pad_rows(a, rows):
    return np.pad(a, ((0, rows - a.shape[0]),) + ((0, 0),) * (a.ndim - 1))


def kernel(x_prompt, x_sample, cache_mem_k, cache_mem_v, state_mlstm_conv, state_mlstm_C, state_mlstm_n, state_mlstm_m, state_ret_S, mem_prompt, w_in, b_gate, w_conv, b_conv, g_mix, g_mhead, g_rhead, w_out, g_xattn, g_mem, w_ck, w_cv, w_cq, w_co, g_ffn, w_gate, w_up, w_down, g_final):
    Bp, Tp, _ = x_prompt.shape
    Bs = x_sample.shape[0]
    l = 0
    n_m = 4 * MIX
    wi = w_in[l]
    w_gates = wi[:, n_m:n_m + 2 * HEADS]
    w_ret = wi[:, n_m + 2 * HEADS:]
    row = lambda a: a.reshape(1, -1)
    bf = lambda a: a.astype(BF16)
    g_mix_r, g_mh_r, g_rh_r = row(g_mix[l]), row(g_mhead[l]), row(g_rhead[l])
    g_x_r, g_mem_r, g_ffn_r, g_fin_r = row(g_xattn[l]), row(g_mem[l]), row(g_ffn[l]), row(g_final)
    b_conv_r = row(b_conv[l])
    w_out_b, w_cq_b, w_co_b = bf(w_out[l]), bf(w_cq[l]), bf(w_co[l])
    w_gate_b, w_up_b, w_down_b = bf(w_gate[l]), bf(w_up[l]), bf(w_down[l])

    mk_b, mv_b, mk_rows, mv_rows = _memkv(mem_prompt, g_mem_r, bf(w_ck[l]), bf(w_cv[l]))
    wn = bf(jnp.concatenate([wi[:, :2 * MIX], w_ret[:, :2 * MIX]], axis=1))
    wt = bf(jnp.concatenate([wi[:, 2 * MIX:n_m], w_ret[:, 2 * MIX:]], axis=1).T)
    gate_rows = lambda a: jnp.concatenate(
        [a[:HEADS], jnp.zeros((8 - HEADS,) + a.shape[1:], F32),
         a[HEADS:], jnp.zeros((8 - HEADS,) + a.shape[1:], F32)], axis=0)
    wgt = bf(gate_rows(w_gates.T))
    bg_rows = gate_rows(b_gate[l][:, None])
    bgt = jnp.broadcast_to(bg_rows, (GT_ROWS, CHUNK))
    cos_p, sin_p = _rope_tables(np.arange(Tp))
    decay, q_dec, k_dec, chunk_dec = _retention_tables(CHUNK)
    hcat_p, conv_p, C_p, n_p, m_p, S_p = _mixer_prompt(
        x_prompt, wn, wt, wgt, g_mix_r, bgt, w_conv[l], b_conv_r,
        _lanes(g_mhead[l], CHUNK), _lanes(g_rhead[l], CHUNK), cos_p, sin_p,
        np.swapaxes(decay, 1, 2), _pad_rows(q_dec, 8), _lanes(k_dec, HEAD_DIM),
        _pad_rows(_lanes(chunk_dec, HEAD_DIM), 8))
    x1_p, o_p = _attn_prompt(x_prompt, hcat_p, mk_b, mv_b, w_out_b, w_cq_b, g_x_r)
    y_p = _ffn(x1_p.reshape(Bp * Tp, D_MODEL), o_p.reshape(Bp * Tp, D_MODEL),
               w_co_b, w_gate_b, w_up_b, w_down_b, g_ffn_r, g_fin_r, tm=512)

    xs = x_sample.reshape(Bs, D_MODEL)
    un_s, ut_s, gs_s = _inproj_sample(xs, g_mix_r, wn, wt, wgt)
    cos_s, sin_s = _rope_tables(PAST_LEN + np.arange(1))
    decay1, q_dec1, k_dec1, chunk_dec1 = _retention_tables(1)
    rtab = np.zeros((8, HEAD_DIM), np.float32)
    rtab[:4, :HEADS] = np.stack([q_dec1[:, 0], k_dec1[:, 0], chunk_dec1, decay1[:, 0, 0]])
    hcat_s, conv_s, C_s, n_s, m_s, S_s = _mixer_sample(
        un_s, ut_s, gs_s, jnp.transpose(state_mlstm_conv[l], (1, 0, 2)), state_mlstm_C[l],
        state_mlstm_n[l].reshape(Bs, MIX), state_mlstm_m[l], state_ret_S[l],
        bg_rows.reshape(1, GT_ROWS), w_conv[l], b_conv_r, g_mh_r, g_rh_r, cos_s, sin_s, rtab)
    x1_s, q_s = _outq_sample(xs, hcat_s, w_out_b, w_cq_b, g_x_r)
    o_s = _attn_sample(q_s, _kv_rows(cache_mem_k[l]), _kv_rows(cache_mem_v[l]))
    y_s = _ffn(x1_s, o_s, w_co_b, w_gate_b, w_up_b, w_down_b, g_ffn_r, g_fin_r, tm=Bs)

    return (y_p.reshape(Bp, Tp, D_MODEL), y_s.reshape(Bs, 1, D_MODEL),
            _kv_from_rows(mk_rows), _kv_from_rows(mv_rows),
            conv_p[None], C_p[None], n_p[None], m_p[None, :, :HEADS, 0], S_p[None],
            jnp.transpose(conv_s, (1, 0, 2))[None], C_s[None],
            n_s.reshape(1, Bs, HEADS, HEAD_DIM), m_s[None], S_s[None])
```

```python
import functools

import jax
import jax.numpy as jnp
import numpy as np
from jax import lax
from jax.experimental import pallas as pl
from jax.experimental.pallas import tpu as pltpu

F32 = jnp.float32
BF16 = jnp.bfloat16

D_MODEL = 1024
HEADS = 4
HEAD_DIM = 128
MIX = HEADS * HEAD_DIM
CONV_W = 4
CHUNK = 128
N_MEM = 256
X_HEADS = 4
X_HEAD_DIM = 256
D_FF = 2816
ROPE_THETA = 10000.0
EPS = 1e-6
PAST_LEN = 16384
KV_ROWS = N_MEM * 2 * X_HEADS
QROWS = 2 * X_HEADS

N_QK, N_RQ, N_RK, N_COLS = 0, 1024, 1536, 2048
T_V, T_O, T_RV, T_RG, T_ROWS = 0, 512, 1024, 1536, 2048
GT_ROWS = 16
UOFF = 8
ROW_GROUPS = 2

V7X_VMEM_LIMIT = 56 * 1024 * 1024

NT_DIMS = (((1,), (1,)), ((), ()))
TN_DIMS = (((0,), (0,)), ((), ()))


def _dot(a, b):
    return jnp.dot(a.astype(BF16), b.astype(BF16), preferred_element_type=F32)


def _dot_nt(a, b):
    return lax.dot_general(a.astype(BF16), b.astype(BF16), NT_DIMS, preferred_element_type=F32)


def _dot_tn(a, b):
    return lax.dot_general(a.astype(BF16), b.astype(BF16), TN_DIMS, preferred_element_type=F32)


def _rms(x, g):
    return x * lax.rsqrt(jnp.mean(x * x, axis=-1, keepdims=True) + EPS) * g


def _head_norm(h):
    return h * lax.rsqrt(jnp.mean(h * h, axis=-1, keepdims=True) + EPS)


def _silu(x):
    return x * jax.nn.sigmoid(x)


def _log_sigmoid(x):
    return jnp.minimum(x, 0.0) - jnp.log1p(jnp.exp(-jnp.abs(x)))


def _rope(x, cos_full, sin_signed):
    return x * cos_full + pltpu.roll(x, HEAD_DIM // 2, 1) * sin_signed


def _const_spec(shape):
    zeros = (0,) * len(shape)
    return pl.BlockSpec(shape, lambda *_: zeros, pipeline_mode=pl.Buffered(1))


def _params(*sem):
    return pltpu.CompilerParams(dimension_semantics=sem, vmem_limit_bytes=V7X_VMEM_LIMIT)


def _memkv_kernel(mem_ref, g_ref, wk_ref, wv_ref, kb_ref, vb_ref, krows_ref, vrows_ref):
    mn = _rms(mem_ref[...], g_ref[...]).astype(BF16)
    for w_ref, b_ref, rows_ref in ((wk_ref, kb_ref, krows_ref), (wv_ref, vb_ref, vrows_ref)):
        proj = jnp.dot(mn, w_ref[...], preferred_element_type=F32)
        b_ref[...] = proj.astype(BF16)
        for h in range(X_HEADS):
            for c in range(2):
                lane0 = h * X_HEAD_DIM + c * 128
                rows_ref[pl.ds(c * X_HEADS + h, N_MEM, stride=QROWS), :] = proj[:, lane0:lane0 + 128]


def _memkv(mem, g_mem, wk, wv):
    B = mem.shape[0]
    tok = pl.BlockSpec((None, N_MEM, D_MODEL), lambda b: (b, 0, 0))
    rows = pl.BlockSpec((None, KV_ROWS, 128), lambda b: (b, 0, 0))
    return pl.pallas_call(
        _memkv_kernel,
        grid=(B,),
        in_specs=[tok, _const_spec((1, D_MODEL)),
                  _const_spec((D_MODEL, D_MODEL)), _const_spec((D_MODEL, D_MODEL))],
        out_specs=[tok, tok, rows, rows],
        out_shape=[jax.ShapeDtypeStruct((B, N_MEM, D_MODEL), BF16)] * 2
        + [jax.ShapeDtypeStruct((B, KV_ROWS, 128), F32)] * 2,
        compiler_params=_params("parallel"),
    )(mem, g_mem, wk, wv)


def _mixer_prompt_kernel(x_ref, wn_ref, wt_ref, wgt_ref, gmix_ref, bgt_ref, wconv_ref, bconv_ref,
                         gmh_ref, grh_ref, cos_ref, sin_ref, decay_ref, qdec_ref, kdec_ref, cdec_ref,
                         h_ref, conv_ref, C_ref, n_ref, m_ref, S_ref,
                         un_a, ut_a, gt_a, un_b, ut_b, gt_b, tail_ref, st_ref, *, tt, tiles_per_seq):
    L = CHUNK
    s = pl.program_id(0)
    t = (jnp.maximum(s, 1) - 1) % tiles_per_seq
    chunks = range(tt // L)
    slot_a, slot_b = (un_a, ut_a, gt_a), (un_b, ut_b, gt_b)

    @pl.when(s == 0)
    def _():
        for ref in slot_b + (tail_ref,):
            ref[...] = jnp.zeros_like(ref)

    @pl.when(t == 0)
    def _():
        C_ref[...] = jnp.zeros_like(C_ref)
        st_ref[...] = jnp.zeros_like(st_ref)
        n_ref[...] = jnp.zeros_like(n_ref)
        m_ref[...] = jnp.zeros_like(m_ref)

    k_scale = HEAD_DIM ** -0.5

    def project(slot, xn, part):
        un_ref, ut_ref, gt_ref = slot
        c0 = part * MIX
        un_ref[UOFF:UOFF + tt, c0:c0 + MIX] = jnp.dot(xn, wn_ref[:, c0:c0 + MIX],
                                                      preferred_element_type=F32)
        res = lax.dot_general(wt_ref[c0:c0 + MIX, :], xn, NT_DIMS, preferred_element_type=F32)
        for c in chunks:
            ut_ref[c, c0:c0 + MIX, :] = res[:, c * L:(c + 1) * L]
        if part == 0:
            gates_t = lax.dot_general(wgt_ref[...], xn, NT_DIMS, preferred_element_type=F32)
            for c in chunks:
                gt_ref[c] = gates_t[:, c * L:(c + 1) * L]

    src_id = lax.broadcasted_iota(jnp.int32, (L, L), 0)
    tgt_id = lax.broadcasted_iota(jnp.int32, (L, L), 1)
    causal = src_id <= tgt_id
    triu_bf = jnp.where(causal, 1.0, 0.0).astype(BF16)
    heads = range(HEADS)
    hcol = lambda base, h: slice(base + h * HEAD_DIM, base + (h + 1) * HEAD_DIM)

    def chunk_body(c, slot):
        un_ref, ut_ref, gt_ref = slot
        r0 = c * L
        rows = pl.ds(UOFF + r0, L)
        trows = pl.ds(r0, L)
        g_t = gt_ref[c] + bgt_ref[...]
        ig = g_t[0:8]
        lf = _log_sigmoid(g_t[8:16])
        lf_hi = lf.astype(BF16)
        r1 = lf - lf_hi.astype(F32)
        lf_mid = r1.astype(BF16)
        lf_lo = (r1 - lf_mid.astype(F32)).astype(BF16)
        bc = (jnp.dot(lf_hi, triu_bf, preferred_element_type=F32)
              + jnp.dot(lf_mid, triu_bf, preferred_element_type=F32)
              + jnp.dot(lf_lo, triu_bf, preferred_element_type=F32))
        m_prev = m_ref[...]
        inter = bc + m_prev
        b_last = bc[:, L - 1:L]
        g_w = b_last - bc + ig
        m_new = jnp.maximum(b_last + m_prev, jnp.max(g_w, axis=1, keepdims=True))
        ws = jnp.exp(g_w - m_new)
        carry = jnp.exp(b_last + m_prev - m_new)
        a_n = jnp.concatenate([ig - bc, jnp.zeros((L - 8, L), F32)], axis=0).T

        def conv_act(col):
            win = un_ref[pl.ds(r0, L + UOFF), col:col + HEAD_DIM]
            acc = bconv_ref[:, col:col + HEAD_DIM]
            for j in range(CONV_W):
                s0 = UOFF - (CONV_W - 1) + j
                acc = acc + win[s0:s0 + L] * wconv_ref[j:j + 1, col:col + HEAD_DIM]
            return _silu(acc)

        cos_f, sin_s = cos_ref[trows, :], sin_ref[trows, :]
        q = [conv_act(N_QK + h * HEAD_DIM) for h in heads]
        kb = [(conv_act(N_QK + MIX + h * HEAD_DIM) * k_scale).astype(BF16) for h in heads]
        qb = [a.astype(BF16) for a in q]
        rqb = [_rope(un_ref[rows, hcol(N_RQ, h)], cos_f, sin_s).astype(BF16) for h in heads]
        rk = [_rope(un_ref[rows, hcol(N_RK, h)], cos_f, sin_s) * k_scale for h in heads]
        v_t = [ut_ref[c, hcol(T_V, h), :] for h in heads]
        rvb = [ut_ref[c, hcol(T_RV, h), :].astype(BF16) for h in heads]
        c_old = [C_ref[h] for h in heads]
        s_old = [st_ref[h] for h in heads]
        n_old = [n_ref[h:h + 1, :] for h in heads]
        qk = [_dot_nt(kb[h], qb[h]) for h in heads]
        att = [_dot_nt(rk[h], rqb[h]) for h in heads]
        c_q = [_dot_nt(c_old[h], qb[h]) for h in heads]
        s_q = [_dot_nt(s_old[h], rqb[h]) for h in heads]
        n_q = [_dot_nt(jnp.broadcast_to(n_old[h], (8, HEAD_DIM)), qb[h])[0:1] for h in heads]
        d_c = [_dot(v_t[h] * ws[h:h + 1, :], kb[h]) for h in heads]
        d_n = [_dot(jnp.broadcast_to(ws[h:h + 1, :], (8, L)), kb[h])[0:1] for h in heads]
        d_s = [_dot(rvb[h], rk[h] * kdec_ref[h]) for h in heads]
        m_t, w_in, wts = [], [], []
        for h in heads:
            dmat = a_n[:, h:h + 1] + bc[h:h + 1, :]
            dmat = jnp.where(causal, dmat, -jnp.inf)
            m_t.append(jnp.maximum(inter[h:h + 1, :], jnp.max(dmat, axis=0, keepdims=True)))
            wts.append(jnp.exp(dmat - m_t[h]) * qk[h])
            w_in.append(jnp.exp(inter[h:h + 1, :] - m_t[h]))
        att_w = [att[h] * decay_ref[h] for h in heads]
        v_p = [_dot(v_t[h], wts[h]) for h in heads]
        v_a = [_dot(rvb[h], att_w[h]) for h in heads]
        for h in heads:
            num = v_p[h] + w_in[h] * c_q[h]
            den = jnp.sum(wts[h], axis=0, keepdims=True) + w_in[h] * n_q[h]
            hm = num / jnp.maximum(jnp.abs(den), jnp.exp(-m_t[h]))
            hm = hm * lax.rsqrt(jnp.mean(hm * hm, axis=0, keepdims=True) + EPS)
            hm = hm * gmh_ref[hcol(0, h), :] * jax.nn.sigmoid(ut_ref[c, hcol(T_O, h), :])
            h_ref[trows, hcol(0, h)] = hm.T.astype(BF16)
            carry_h = carry[h:h + 1, :]
            C_ref[h] = carry_h * c_old[h] + d_c[h]
            n_ref[h:h + 1, :] = carry_h * n_old[h] + d_n[h]
            o = v_a[h] + qdec_ref[h:h + 1, :] * s_q[h]
            st_ref[h] = cdec_ref[h:h + 1, :] * s_old[h] + d_s[h]
            hr = o * lax.rsqrt(jnp.mean(o * o, axis=0, keepdims=True) + EPS)
            hr = hr * grh_ref[hcol(0, h), :] * _silu(ut_ref[c, hcol(T_RG, h), :])
            h_ref[trows, hcol(MIX, h)] = hr.T.astype(BF16)

        m_ref[...] = m_new

    def step(proj_slot, scan_slot):
        un_ref = scan_slot[0]
        un_ref[0:UOFF, N_QK:N_RQ] = jnp.where(t == 0, 0.0, tail_ref[...])
        xn = _rms(x_ref[...], gmix_ref[...]).astype(BF16)
        for c in chunks:
            project(proj_slot, xn, c)
            chunk_body(c, scan_slot)
        tail_ref[...] = un_ref[tt:tt + UOFF, N_QK:N_RQ]

    @pl.when(s % 2 == 0)
    def _():
        step(slot_a, slot_b)

    @pl.when(s % 2 == 1)
    def _():
        step(slot_b, slot_a)

    @pl.when(t == tiles_per_seq - 1)
    def _():
        conv_ref[...] = tail_ref[UOFF - (CONV_W - 1):UOFF, :]
        for h in heads:
            S_ref[h] = st_ref[h].T


def _mixer_prompt(x, wn, wt, wgt, g_mix, bgt, w_conv, b_conv, gmh_cols, grh_cols, cos_f, sin_s,
                  decay_t, qdec_rows, kdec_cols, cdec_rows, tt=512):
    B, T, _ = x.shape
    tps = T // tt
    n_tiles = B * tps
    proj = lambda s: jnp.minimum(s, n_tiles - 1)
    scan = lambda s: jnp.maximum(s, 1) - 1
    per_b3 = lambda s: (scan(s) // tps, 0, 0)
    per_b4 = lambda s: (scan(s) // tps, 0, 0, 0)
    in_specs = [
        pl.BlockSpec((None, tt, D_MODEL), lambda s: (proj(s) // tps, proj(s) % tps, 0)),
        _const_spec((D_MODEL, N_COLS)), _const_spec((T_ROWS, D_MODEL)), _const_spec((GT_ROWS, D_MODEL)),
        _const_spec((1, D_MODEL)), _const_spec((GT_ROWS, CHUNK)),
        _const_spec((CONV_W, 2 * MIX)), _const_spec((1, 2 * MIX)),
        _const_spec((MIX, CHUNK)), _const_spec((MIX, CHUNK)),
        pl.BlockSpec((tt, HEAD_DIM), lambda s: (scan(s) % tps, 0)),
        pl.BlockSpec((tt, HEAD_DIM), lambda s: (scan(s) % tps, 0)),
        _const_spec((HEADS, CHUNK, CHUNK)),
        _const_spec((8, CHUNK)), _const_spec((HEADS, CHUNK, HEAD_DIM)), _const_spec((8, HEAD_DIM)),
    ]
    out_specs = [
        pl.BlockSpec((None, tt, D_MODEL), lambda s: (scan(s) // tps, scan(s) % tps, 0)),
        pl.BlockSpec((None, CONV_W - 1, 2 * MIX), per_b3),
        pl.BlockSpec((None, HEADS, HEAD_DIM, HEAD_DIM), per_b4),
        pl.BlockSpec((None, HEADS, HEAD_DIM), per_b3),
        pl.BlockSpec((None, 8, CHUNK), per_b3),
        pl.BlockSpec((None, HEADS, HEAD_DIM, HEAD_DIM), per_b4),
    ]
    out_shape = [
        jax.ShapeDtypeStruct((B, T, D_MODEL), BF16),
        jax.ShapeDtypeStruct((B, CONV_W - 1, 2 * MIX), F32),
        jax.ShapeDtypeStruct((B, HEADS, HEAD_DIM, HEAD_DIM), F32),
        jax.ShapeDtypeStruct((B, HEADS, HEAD_DIM), F32),
        jax.ShapeDtypeStruct((B, 8, CHUNK), F32),
        jax.ShapeDtypeStruct((B, HEADS, HEAD_DIM, HEAD_DIM), F32),
    ]
    n_chunks = tt // CHUNK
    slot = [
        pltpu.VMEM((tt + UOFF, N_COLS), F32),
        pltpu.VMEM((n_chunks, T_ROWS, CHUNK), F32),
        pltpu.VMEM((n_chunks, GT_ROWS, CHUNK), F32),
    ]
    scratch = slot + slot + [
        pltpu.VMEM((UOFF, 2 * MIX), F32),
        pltpu.VMEM((HEADS, HEAD_DIM, HEAD_DIM), F32),
    ]
    return pl.pallas_call(
        functools.partial(_mixer_prompt_kernel, tt=tt, tiles_per_seq=tps),
        grid=(n_tiles + 1,),
        in_specs=in_specs, out_specs=out_specs, out_shape=out_shape,
        scratch_shapes=scratch,
        compiler_params=_params("arbitrary"),
    )(x, wn, wt, wgt, g_mix, bgt, w_conv, b_conv, gmh_cols, grh_cols, cos_f, sin_s,
      decay_t, qdec_rows, kdec_cols, cdec_rows)


def _attn_prompt_kernel(x_ref, h_ref, k_ref, v_ref, wout_ref, wcq_ref, gx_ref, x1_ref, o_ref):
    tm = x_ref.shape[0]
    rows = [slice(r, r + tm // ROW_GROUPS) for r in range(0, tm, tm // ROW_GROUPS)]
    x1 = [x_ref[r, :] + jnp.dot(h_ref[r, :], wout_ref[...], preferred_element_type=F32)
          for r in rows]
    for r, a in zip(rows, x1):
        x1_ref[r, :] = a
    xq = [_rms(a, gx_ref[...]).astype(BF16) for a in x1]
    q = [jnp.dot(a, wcq_ref[...], preferred_element_type=F32).astype(BF16) for a in xq]
    sl = [slice(h * X_HEAD_DIM, (h + 1) * X_HEAD_DIM) for h in range(X_HEADS)]
    items = [(g, h) for g in range(ROW_GROUPS) for h in range(X_HEADS)]
    s = [_dot_nt(q[g][:, sl[h]], k_ref[:, sl[h]]) * (X_HEAD_DIM ** -0.5) for g, h in items]
    e = [jnp.exp(a - jnp.max(a, axis=-1, keepdims=True)) for a in s]
    p = [a / jnp.sum(a, axis=-1, keepdims=True) for a in e]
    for (g, h), a in zip(items, p):
        o_ref[rows[g], sl[h]] = _dot(a, v_ref[:, sl[h]]).astype(BF16)


def _attn_prompt(x, hcat, mk, mv, w_out, w_cq, g_x, tm=512):
    B, T, _ = x.shape
    tile = pl.BlockSpec((None, tm, D_MODEL), lambda b, t: (b, t, 0))
    kv = pl.BlockSpec((None, N_MEM, D_MODEL), lambda b, t: (b, 0, 0))
    return pl.pallas_call(
        _attn_prompt_kernel,
        grid=(B, T // tm),
        in_specs=[tile, tile, kv, kv, _const_spec((D_MODEL, D_MODEL)),
                  _const_spec((D_MODEL, D_MODEL)), _const_spec((1, D_MODEL))],
        out_specs=[tile, tile],
        out_shape=[jax.ShapeDtypeStruct((B, T, D_MODEL), F32),
                   jax.ShapeDtypeStruct((B, T, D_MODEL), BF16)],
        compiler_params=_params("parallel", "parallel"),
    )(x, hcat, mk, mv, w_out, w_cq, g_x)


FF_CHUNK = D_FF // 2


def _ffn_kernel(x1_ref, o_ref, wco_ref, wg_ref, wu_ref, wd_ref, gffn_ref, gfin_ref, y_ref, *,
                n_groups):
    tm = x1_ref.shape[0]
    rows = [slice(r, r + tm // n_groups) for r in range(0, tm, tm // n_groups)]
    acc = [x1_ref[r, :] + _dot(o_ref[r, :], wco_ref[...]) for r in rows]
    hf = [_rms(a, gffn_ref[...]).astype(BF16) for a in acc]
    for c0 in range(0, D_FF, FF_CHUNK):
        gate = [jnp.dot(h, wg_ref[:, c0:c0 + FF_CHUNK], preferred_element_type=F32) for h in hf]
        up = [jnp.dot(h, wu_ref[:, c0:c0 + FF_CHUNK], preferred_element_type=F32) for h in hf]
        act = [_silu(g) * u for g, u in zip(gate, up)]
        acc = [a + _dot(p, wd_ref[c0:c0 + FF_CHUNK, :]) for a, p in zip(acc, act)]
    for r, a in zip(rows, acc):
        y_ref[r, :] = _rms(a, gfin_ref[...])


def _ffn(x1, o, w_co, w_gate, w_up, w_down, g_ffn, g_final, tm):
    rows = x1.shape[0]
    row_spec = pl.BlockSpec((tm, D_MODEL), lambda i: (i, 0))
    return pl.pallas_call(
        functools.partial(_ffn_kernel, n_groups=ROW_GROUPS if tm >= 512 else 1),
        grid=(rows // tm,),
        in_specs=[row_spec, row_spec, _const_spec((D_MODEL, D_MODEL)),
                  _const_spec((D_MODEL, D_FF)), _const_spec((D_MODEL, D_FF)),
                  _const_spec((D_FF, D_MODEL)), _const_spec((1, D_MODEL)), _const_spec((1, D_MODEL))],
        out_specs=row_spec,
        out_shape=jax.ShapeDtypeStruct((rows, D_MODEL), F32),
        compiler_params=_params("parallel"),
    )(x1, o, w_co, w_gate, w_up, w_down, g_ffn, g_final)


def _inproj_sample_kernel(x_ref, g_ref, wn_ref, wt_ref, wgt_ref, un_ref, ut_ref, gs_ref):
    xn = _rms(x_ref[...], g_ref[...]).astype(BF16)
    un_ref[...] = jnp.dot(xn, wn_ref[...], preferred_element_type=F32)
    ut_ref[...] = lax.dot_general(xn, wt_ref[...], NT_DIMS, preferred_element_type=F32)
    gs_ref[...] = lax.dot_general(xn, wgt_ref[...], NT_DIMS, preferred_element_type=F32)


def _inproj_sample(x, g_mix, wn, wt, wgt):
    rows = x.shape[0]
    full = lambda cols: pl.BlockSpec((rows, cols), lambda i: (0, 0))
    return pl.pallas_call(
        _inproj_sample_kernel,
        grid=(1,),
        in_specs=[_const_spec((rows, D_MODEL)), _const_spec((1, D_MODEL)),
                  _const_spec((D_MODEL, N_COLS)), _const_spec((T_ROWS, D_MODEL)),
                  _const_spec((GT_ROWS, D_MODEL))],
        out_specs=[full(N_COLS), full(T_ROWS), full(GT_ROWS)],
        out_shape=[jax.ShapeDtypeStruct((rows, N_COLS), F32),
                   jax.ShapeDtypeStruct((rows, T_ROWS), F32),
                   jax.ShapeDtypeStruct((rows, GT_ROWS), F32)],
        compiler_params=_params("arbitrary"),
    )(x, g_mix, wn, wt, wgt)


def _mixer_sample_kernel(un_ref, ut_ref, gs_ref, conv_ref, C_ref, n_ref, m_ref, S_ref, bg_ref,
                         wconv_ref, bconv_ref, gmh_ref, grh_ref, cos_ref, sin_ref, rtab_ref,
                         h_ref, convo_ref, Co_ref, no_ref, mo_ref, So_ref,
                         q_s, k_s, vws_s, qr_s, kr_s, qc_s, qs_s, carry_s, *, bb):
    k_scale = HEAD_DIM ** -0.5
    uqk = un_ref[:, N_QK:N_RQ]
    conv = (bconv_ref[...] + wconv_ref[0:1, :] * conv_ref[0] + wconv_ref[1:2, :] * conv_ref[1]
            + wconv_ref[2:3, :] * conv_ref[2] + wconv_ref[3:4, :] * uqk)
    convo_ref[0] = conv_ref[1]
    convo_ref[1] = conv_ref[2]
    convo_ref[2] = uqk
    qk_act = _silu(conv)
    q_s[...] = qk_act[:, 0:MIX]
    k_s[...] = qk_act[:, MIX:2 * MIX] * k_scale

    gates = gs_ref[...] + bg_ref[...]
    ig = gates[:, 0:HEADS]
    lf = _log_sigmoid(gates[:, 8:8 + HEADS])
    inter = lf + m_ref[...]
    m_t = jnp.maximum(inter, ig)
    ws = jnp.exp(ig - m_t)
    w_in = jnp.exp(inter - m_t)
    mo_ref[...] = m_t
    carry_s[:, 0:HEADS] = w_in

    cos_f = cos_ref[...]
    sin_s = sin_ref[...]
    for h in range(HEADS):
        lo = h * HEAD_DIM
        hs = slice(lo, lo + HEAD_DIM)
        vws_s[:, hs] = ut_ref[:, T_V + lo:T_V + lo + HEAD_DIM] * ws[:, h:h + 1]
        qr_s[:, hs] = _rope(un_ref[:, N_RQ + lo:N_RQ + lo + HEAD_DIM], cos_f, sin_s)
        kr_s[:, hs] = _rope(un_ref[:, N_RK + lo:N_RK + lo + HEAD_DIM], cos_f, sin_s) * k_scale

    heads = range(HEADS)
    seqs = range(bb)
    hsl = [slice(h * HEAD_DIM, (h + 1) * HEAD_DIM) for h in heads]
    seq_id = lax.broadcasted_iota(jnp.int32, (bb, HEAD_DIM), 0)
    for h in heads:
        q_h, k_h, vws_h = q_s[:, hsl[h]], k_s[:, hsl[h]], vws_s[:, hsl[h]]
        qr_h = qr_s[:, hsl[h]]
        krd_h = kr_s[:, hsl[h]] * rtab_ref[1:2, h:h + 1]
        rv_h = ut_ref[:, T_RV + h * HEAD_DIM:T_RV + (h + 1) * HEAD_DIM]
        c_old = [C_ref[b, h] for b in seqs]
        s_old = [S_ref[b, h] for b in seqs]
        q_c = [_dot_nt(q_h, c_old[b]) for b in seqs]
        q_st = [_dot(qr_h, s_old[b]) for b in seqs]
        d_c = [_dot_tn(jnp.where(seq_id == b, vws_h, 0.0), k_h) for b in seqs]
        d_s = [_dot_tn(jnp.where(seq_id == b, krd_h, 0.0), rv_h) for b in seqs]
        for b in seqs:
            row = slice(b, b + 1)
            qc_s[row, hsl[h]] = q_c[b][row]
            qs_s[row, hsl[h]] = q_st[b][row]
            Co_ref[b, h] = carry_s[row, h:h + 1] * c_old[b] + d_c[b]
            So_ref[b, h] = rtab_ref[2:3, h:h + 1] * s_old[b] + d_s[b]

    for h in range(HEADS):
        lo = h * HEAD_DIM
        hs = slice(lo, lo + HEAD_DIM)
        q_h, k_h, n_h = q_s[:, hs], k_s[:, hs], n_ref[:, hs]
        ws_h, w_in_h = ws[:, h:h + 1], w_in[:, h:h + 1]
        wts = ws_h * jnp.sum(q_h * k_h, axis=1, keepdims=True)
        num = wts * ut_ref[:, T_V + lo:T_V + lo + HEAD_DIM] + w_in_h * qc_s[:, hs]
        den = wts + w_in_h * jnp.sum(n_h * q_h, axis=1, keepdims=True)
        hm = num / jnp.maximum(jnp.abs(den), jnp.exp(-m_t[:, h:h + 1]))
        mo = ut_ref[:, T_O + lo:T_O + lo + HEAD_DIM]
        h_ref[:, hs] = _head_norm(hm) * gmh_ref[:, hs] * jax.nn.sigmoid(mo)
        no_ref[:, hs] = w_in_h * n_h + ws_h * k_h

        att = jnp.sum(qr_s[:, hs] * kr_s[:, hs], axis=1, keepdims=True) * rtab_ref[3:4, h:h + 1]
        o = (att * ut_ref[:, T_RV + lo:T_RV + lo + HEAD_DIM]
             + rtab_ref[0:1, h:h + 1] * qs_s[:, hs])
        rg = ut_ref[:, T_RG + lo:T_RG + lo + HEAD_DIM]
        h_ref[:, MIX + lo:MIX + lo + HEAD_DIM] = _head_norm(o) * grh_ref[:, hs] * _silu(rg)


def _mixer_sample(un, ut, gs, conv_t, C, n2, m, S, bg, w_conv, b_conv, g_mh, g_rh, cos_f, sin_s, rtab,
                  bb=8):
    B = un.shape[0]
    rows = lambda cols: pl.BlockSpec((bb, cols), lambda i: (i, 0))
    mats = pl.BlockSpec((bb, HEADS, HEAD_DIM, HEAD_DIM), lambda i: (i, 0, 0, 0))
    convs = pl.BlockSpec((CONV_W - 1, bb, 2 * MIX), lambda i: (0, i, 0))
    in_specs = [rows(N_COLS), rows(T_ROWS), rows(GT_ROWS), convs, mats, rows(MIX), rows(HEADS), mats,
                _const_spec((1, GT_ROWS)), _const_spec((CONV_W, 2 * MIX)), _const_spec((1, 2 * MIX)),
                _const_spec((1, MIX)), _const_spec((1, MIX)),
                _const_spec((1, HEAD_DIM)), _const_spec((1, HEAD_DIM)), _const_spec((8, HEAD_DIM))]
    out_specs = [rows(D_MODEL), convs, mats, rows(MIX), rows(HEADS), mats]
    out_shape = [jax.ShapeDtypeStruct((B, D_MODEL), F32),
                 jax.ShapeDtypeStruct((CONV_W - 1, B, 2 * MIX), F32),
                 jax.ShapeDtypeStruct((B, HEADS, HEAD_DIM, HEAD_DIM), F32),
                 jax.ShapeDtypeStruct((B, MIX), F32),
                 jax.ShapeDtypeStruct((B, HEADS), F32),
                 jax.ShapeDtypeStruct((B, HEADS, HEAD_DIM, HEAD_DIM), F32)]
    scratch = [pltpu.VMEM((bb, MIX), F32) for _ in range(7)] + [pltpu.VMEM((bb, HEAD_DIM), F32)]
    return pl.pallas_call(
        functools.partial(_mixer_sample_kernel, bb=bb),
        grid=(B // bb,),
        in_specs=in_specs, out_specs=out_specs, out_shape=out_shape,
        scratch_shapes=scratch,
        compiler_params=_params("parallel"),
    )(un, ut, gs, conv_t, C, n2, m, S, bg, w_conv, b_conv, g_mh, g_rh, cos_f, sin_s, rtab)


def _outq_sample_kernel(x_ref, h_ref, wout_ref, wcq_ref, gx_ref, x1_ref, q_ref):
    x1 = x_ref[...] + _dot(h_ref[...], wout_ref[...])
    x1_ref[...] = x1
    q_ref[...] = _dot(_rms(x1, gx_ref[...]), wcq_ref[...])


def _outq_sample(x, hcat, w_out, w_cq, g_x):
    rows = x.shape[0]
    full = _const_spec((rows, D_MODEL))
    return pl.pallas_call(
        _outq_sample_kernel,
        grid=(1,),
        in_specs=[full, full, _const_spec((D_MODEL, D_MODEL)), _const_spec((D_MODEL, D_MODEL)),
                  _const_spec((1, D_MODEL))],
        out_specs=[pl.BlockSpec((rows, D_MODEL), lambda i: (0, 0))] * 2,
        out_shape=[jax.ShapeDtypeStruct((rows, D_MODEL), F32)] * 2,
        compiler_params=_params("arbitrary"),
    )(x, hcat, w_out, w_cq, g_x)


def _attn_sample_kernel(q_ref, k_ref, v_ref, o_ref, *, ba):
    r_id = lax.broadcasted_iota(jnp.int32, (QROWS, KV_ROWS), 0)
    n_id = lax.broadcasted_iota(jnp.int32, (QROWS, KV_ROWS), 1)
    own = (n_id & 7) == (r_id >> 1) + 4 * (r_id & 1)
    low_half = (lax.broadcasted_iota(jnp.int32, (1, KV_ROWS), 1) & 4) == 0
    seqs = range(ba)
    z = [_dot_nt(q_ref[j], k_ref[j]) for j in seqs]
    zc = [jnp.sum(jnp.where(own, z[j], 0.0), axis=0, keepdims=True) for j in seqs]
    other = [jnp.where(low_half, pltpu.roll(zc[j], KV_ROWS - 4, 1), pltpu.roll(zc[j], 4, 1))
             for j in seqs]
    s = [jnp.where(own, (zc[j] + other[j]) * (X_HEAD_DIM ** -0.5), -jnp.inf) for j in seqs]
    m = [jnp.max(s[j], axis=-1, keepdims=True) for j in seqs]
    e = [jnp.exp(s[j] - m[j]) for j in seqs]
    den = [jnp.sum(e[j], axis=-1, keepdims=True) for j in seqs]
    for j in seqs:
        o_ref[j] = _dot(e[j] / den[j], v_ref[j])


def _attn_sample(q, ck, cv, ba=8):
    B = q.shape[0]
    kv = pl.BlockSpec((ba, KV_ROWS, 128), lambda i: (i, 0, 0))
    rows = pl.BlockSpec((ba, QROWS, 128), lambda i: (i, 0, 0))
    return pl.pallas_call(
        functools.partial(_attn_sample_kernel, ba=ba),
        grid=(B // ba,),
        in_specs=[rows, kv, kv],
        out_specs=rows,
        out_shape=jax.ShapeDtypeStruct((B, QROWS, 128), F32),
        compiler_params=_params("parallel"),
    )(q.reshape(B, QROWS, 128), ck, cv).reshape(B, D_MODEL)


def _kv_rows(cache):
    B = cache.shape[0]
    c5 = cache.reshape(B, N_MEM, X_HEADS, 2, 128)
    return jnp.transpose(c5, (0, 1, 3, 2, 4)).reshape(B, KV_ROWS, 128)


def _kv_from_rows(rows):
    B = rows.shape[0]
    r5 = rows.reshape(B, N_MEM, 2, X_HEADS, 128)
    return jnp.transpose(r5, (0, 1, 3, 2, 4)).reshape(1, B, N_MEM, X_HEADS, X_HEAD_DIM)


def _rope_tables(pos):
    half = HEAD_DIM // 2
    inv = ROPE_THETA ** (-np.arange(half, dtype=np.float64) / half)
    ang = np.asarray(pos, np.float64)[:, None] * inv[None, :]
    cos, sin = np.cos(ang), np.sin(ang)
    return (np.concatenate([cos, cos], axis=-1).astype(np.float32),
            np.concatenate([-sin, sin], axis=-1).astype(np.float32))


def _retention_tables(L):
    lg = np.log1p(-np.exp2(-5.0 - np.arange(HEADS, dtype=np.float64)))
    t = np.arange(L, dtype=np.float64)
    diff = t[:, None] - t[None, :]
    decay = np.where(diff >= 0, np.exp(lg[:, None, None] * np.maximum(diff, 0.0)), 0.0)
    q_dec = np.exp(lg[:, None] * (t + 1.0))
    k_dec = np.exp(lg[:, None] * (L - 1.0 - t))
    chunk_dec = np.exp(lg * L)
    return tuple(a.astype(np.float32) for a in (decay, q_dec, k_dec, chunk_dec))


def _lanes(a, n):
    xp = np if isinstance(a, np.ndarray) else jnp
    return xp.broadcast_to(a[..., None], a.shape + (n,))


def _pad_rows(a, rows):
    return np.pad(a, ((0, rows - a.shape[0]),) + ((0, 0),) * (a.ndim - 1))


def kernel(x_prompt, x_sample, cache_mem_k, cache_mem_v, state_mlstm_conv, state_mlstm_C, state_mlstm_n, state_mlstm_m, state_ret_S, mem_prompt, w_in, b_gate, w_conv, b_conv, g_mix, g_mhead, g_rhead, w_out, g_xattn, g_mem, w_ck, w_cv, w_cq, w_co, g_ffn, w_gate, w_up, w_down, g_final):
    Bp, Tp, _ = x_prompt.shape
    Bs = x_sample.shape[0]
    l = 0
    n_m = 4 * MIX
    wi = w_in[l]
    w_gates = wi[:, n_m:n_m + 2 * HEADS]
    w_ret = wi[:, n_m + 2 * HEADS:]
    row = lambda a: a.reshape(1, -1)
    bf = lambda a: a.astype(BF16)
    g_mix_r, g_mh_r, g_rh_r = row(g_mix[l]), row(g_mhead[l]), row(g_rhead[l])
    g_x_r, g_mem_r, g_ffn_r, g_fin_r = row(g_xattn[l]), row(g_mem[l]), row(g_ffn[l]), row(g_final)
    b_conv_r = row(b_conv[l])
    w_out_b, w_cq_b, w_co_b = bf(w_out[l]), bf(w_cq[l]), bf(w_co[l])
    w_gate_b, w_up_b, w_down_b = bf(w_gate[l]), bf(w_up[l]), bf(w_down[l])

    mk_b, mv_b, mk_rows, mv_rows = _memkv(mem_prompt, g_mem_r, bf(w_ck[l]), bf(w_cv[l]))
    wn = bf(jnp.concatenate([wi[:, :2 * MIX], w_ret[:, :2 * MIX]], axis=1))
    wt = bf(jnp.concatenate([wi[:, 2 * MIX:n_m], w_ret[:, 2 * MIX:]], axis=1).T)
    gate_rows = lambda a: jnp.concatenate(
        [a[:HEADS], jnp.zeros((8 - HEADS,) + a.shape[1:], F32),
         a[HEADS:], jnp.zeros((8 - HEADS,) + a.shape[1:], F32)], axis=0)
    wgt = bf(gate_rows(w_gates.T))
    bg_rows = gate_rows(b_gate[l][:, None])
    bgt = jnp.broadcast_to(bg_rows, (GT_ROWS, CHUNK))
    cos_p, sin_p = _rope_tables(np.arange(Tp))
    decay, q_dec, k_dec, chunk_dec = _retention_tables(CHUNK)
    hcat_p, conv_p, C_p, n_p, m_p, S_p = _mixer_prompt(
        x_prompt, wn, wt, wgt, g_mix_r, bgt, w_conv[l], b_conv_r,
        _lanes(g_mhead[l], CHUNK), _lanes(g_rhead[l], CHUNK), cos_p, sin_p,
        np.swapaxes(decay, 1, 2), _pad_rows(q_dec, 8), _lanes(k_dec, HEAD_DIM),
        _pad_rows(_lanes(chunk_dec, HEAD_DIM), 8))
    x1_p, o_p = _attn_prompt(x_prompt, hcat_p, mk_b, mv_b, w_out_b, w_cq_b, g_x_r)
    y_p = _ffn(x1_p.reshape(Bp * Tp, D_MODEL), o_p.reshape(Bp * Tp, D_MODEL),
               w_co_b, w_gate_b, w_up_b, w_down_b, g_ffn_r, g_fin_r, tm=512)

    xs = x_sample.reshape(Bs, D_MODEL)
    un_s, ut_s, gs_s = _inproj_sample(xs, g_mix_r, wn, wt, wgt)
    cos_s, sin_s = _rope_tables(PAST_LEN + np.arange(1))
    decay1, q_dec1, k_dec1, chunk_dec1 = _retention_tables(1)
    rtab = np.zeros((8, HEAD_DIM), np.float32)
    rtab[:4, :HEADS] = np.stack([q_dec1[:, 0], k_dec1[:, 0], chunk_dec1, decay1[:, 0, 0]])
    hcat_s, conv_s, C_s, n_s, m_s, S_s = _mixer_sample(
        un_s, ut_s, gs_s, jnp.transpose(state_mlstm_conv[l], (1, 0, 2)), state_mlstm_C[l],
        state_mlstm_n[l].reshape(Bs, MIX), state_mlstm_m[l], state_ret_S[l],
        bg_rows.reshape(1, GT_ROWS), w_conv[l], b_conv_r, g_mh_r, g_rh_r, cos_s, sin_s, rtab)
    x1_s, q_s = _outq_sample(xs, hcat_s, w_out_b, w_cq_b, g_x_r)
    o_s = _attn_sample(q_s, _kv_rows(cache_mem_k[l]), _kv_rows(cache_mem_v[l]))
    y_s = _ffn(x1_s, o_s, w_co_b, w_gate_b, w_up_b, w_down_b, g_ffn_r, g_fin_r, tm=Bs)

    return (y_p.reshape(Bp, Tp, D_MODEL), y_s.reshape(Bs, 1, D_MODEL),
            _kv_from_rows(mk_rows), _kv_from_rows(mv_rows),
            conv_p[None], C_p[None], n_p[None], m_p[None, :, :HEADS, 0], S_p[None],
            jnp.transpose(conv_s, (1, 0, 2))[None], C_s[None],
            n_s.reshape(1, Bs, HEADS, HEAD_DIM), m_s[None], S_s[None])
```

```python
import functools

import jax
import jax.numpy as jnp
import numpy as np
from jax import lax
from jax.experimental import pallas as pl
from jax.experimental.pallas import tpu as pltpu

F32 = jnp.float32
BF16 = jnp.bfloat16

D_MODEL = 1024
HEADS = 4
HEAD_DIM = 128
MIX = HEADS * HEAD_DIM
CONV_W = 4
CHUNK = 128
N_MEM = 256
X_HEADS = 4
X_HEAD_DIM = 256
D_FF = 2816
ROPE_THETA = 10000.0
EPS = 1e-6
PAST_LEN = 16384
KV_ROWS = N_MEM * 2 * X_HEADS
QROWS = 2 * X_HEADS

N_QK, N_RQ, N_RK, N_COLS = 0, 1024, 1536, 2048
T_V, T_O, T_RV, T_RG, T_ROWS = 0, 512, 1024, 1536, 2048
GT_ROWS = 16
UOFF = 8
ROW_GROUPS = 2

V7X_VMEM_LIMIT = 56 * 1024 * 1024

NT_DIMS = (((1,), (1,)), ((), ()))
TN_DIMS = (((0,), (0,)), ((), ()))


def _dot(a, b):
    return jnp.dot(a.astype(BF16), b.astype(BF16), preferred_element_type=F32)


def _dot_nt(a, b):
    return lax.dot_general(a.astype(BF16), b.astype(BF16), NT_DIMS, preferred_element_type=F32)


def _dot_tn(a, b):
    return lax.dot_general(a.astype(BF16), b.astype(BF16), TN_DIMS, preferred_element_type=F32)


def _rms(x, g):
    return x * lax.rsqrt(jnp.mean(x * x, axis=-1, keepdims=True) + EPS) * g


def _head_norm(h):
    return h * lax.rsqrt(jnp.mean(h * h, axis=-1, keepdims=True) + EPS)


def _silu(x):
    return x * jax.nn.sigmoid(x)


def _log_sigmoid(x):
    return jnp.minimum(x, 0.0) - jnp.log1p(jnp.exp(-jnp.abs(x)))


def _rope(x, cos_full, sin_signed):
    return x * cos_full + pltpu.roll(x, HEAD_DIM // 2, 1) * sin_signed


def _const_spec(shape):
    zeros = (0,) * len(shape)
    return pl.BlockSpec(shape, lambda *_: zeros, pipeline_mode=pl.Buffered(1))


def _params(*sem):
    return pltpu.CompilerParams(dimension_semantics=sem, vmem_limit_bytes=V7X_VMEM_LIMIT)


def _memkv_kernel(mem_ref, g_ref, wk_ref, wv_ref, kb_ref, vb_ref, krows_ref, vrows_ref):
    mn = _rms(mem_ref[...], g_ref[...]).astype(BF16)
    for w_ref, b_ref, rows_ref in ((wk_ref, kb_ref, krows_ref), (wv_ref, vb_ref, vrows_ref)):
        proj = jnp.dot(mn, w_ref[...], preferred_element_type=F32)
        b_ref[...] = proj.astype(BF16)
        for h in range(X_HEADS):
            for c in range(2):
                lane0 = h * X_HEAD_DIM + c * 128
                rows_ref[pl.ds(c * X_HEADS + h, N_MEM, stride=QROWS), :] = proj[:, lane0:lane0 + 128]


def _memkv(mem, g_mem, wk, wv):
    B = mem.shape[0]
    tok = pl.BlockSpec((None, N_MEM, D_MODEL), lambda b: (b, 0, 0))
    rows = pl.BlockSpec((None, KV_ROWS, 128), lambda b: (b, 0, 0))
    return pl.pallas_call(
        _memkv_kernel,
        grid=(B,),
        in_specs=[tok, _const_spec((1, D_MODEL)),
                  _const_spec((D_MODEL, D_MODEL)), _const_spec((D_MODEL, D_MODEL))],
        out_specs=[tok, tok, rows, rows],
        out_shape=[jax.ShapeDtypeStruct((B, N_MEM, D_MODEL), BF16)] * 2
        + [jax.ShapeDtypeStruct((B, KV_ROWS, 128), F32)] * 2,
        compiler_params=_params("parallel"),
    )(mem, g_mem, wk, wv)


def _mixer_prompt_kernel(x_ref, wn_ref, wt_ref, wgt_ref, gmix_ref, bgt_ref, wconv_ref, bconv_ref,
                         gmh_ref, grh_ref, cos_ref, sin_ref, decay_ref, qdec_ref, kdec_ref, cdec_ref,
                         h_ref, conv_ref, C_ref, n_ref, m_ref, S_ref,
                         un_a, ut_a, gt_a, un_b, ut_b, gt_b, tail_ref, st_ref, *, tt, tiles_per_seq):
    L = CHUNK
    s = pl.program_id(0)
    t = (jnp.maximum(s, 1) - 1) % tiles_per_seq
    chunks = range(tt // L)
    slot_a, slot_b = (un_a, ut_a, gt_a), (un_b, ut_b, gt_b)

    @pl.when(s == 0)
    def _():
        for ref in slot_b + (tail_ref,):
            ref[...] = jnp.zeros_like(ref)

    @pl.when(t == 0)
    def _():
        C_ref[...] = jnp.zeros_like(C_ref)
        st_ref[...] = jnp.zeros_like(st_ref)
        n_ref[...] = jnp.zeros_like(n_ref)
        m_ref[...] = jnp.zeros_like(m_ref)

    k_scale = HEAD_DIM ** -0.5

    def project(slot, xn, part):
        un_ref, ut_ref, gt_ref = slot
        c0 = part * MIX
        un_ref[UOFF:UOFF + tt, c0:c0 + MIX] = jnp.dot(xn, wn_ref[:, c0:c0 + MIX],
                                                      preferred_element_type=F32)
        res = lax.dot_general(wt_ref[c0:c0 + MIX, :], xn, NT_DIMS, preferred_element_type=F32)
        for c in chunks:
            ut_ref[c, c0:c0 + MIX, :] = res[:, c * L:(c + 1) * L]
        if part == 0:
            gates_t = lax.dot_general(wgt_ref[...], xn, NT_DIMS, preferred_element_type=F32)
            for c in chunks:
                gt_ref[c] = gates_t[:, c * L:(c + 1) * L]

    src_id = lax.broadcasted_iota(jnp.int32, (L, L), 0)
    tgt_id = lax.broadcasted_iota(jnp.int32, (L, L), 1)
    causal = src_id <= tgt_id
    triu_bf = jnp.where(causal, 1.0, 0.0).astype(BF16)
    heads = range(HEADS)
    hcol = lambda base, h: slice(base + h * HEAD_DIM, base + (h + 1) * HEAD_DIM)

    def chunk_body(c, slot):
        un_ref, ut_ref, gt_ref = slot
        r0 = c * L
        rows = pl.ds(UOFF + r0, L)
        trows = pl.ds(r0, L)
        g_t = gt_ref[c] + bgt_ref[...]
        ig = g_t[0:8]
        lf = _log_sigmoid(g_t[8:16])
        lf_hi = lf.astype(BF16)
        r1 = lf - lf_hi.astype(F32)
        lf_mid = r1.astype(BF16)
        lf_lo = (r1 - lf_mid.astype(F32)).astype(BF16)
        bc = (jnp.dot(lf_hi, triu_bf, preferred_element_type=F32)
              + jnp.dot(lf_mid, triu_bf, preferred_element_type=F32)
              + jnp.dot(lf_lo, triu_bf, preferred_element_type=F32))
        m_prev = m_ref[...]
        inter = bc + m_prev
        b_last = bc[:, L - 1:L]
        g_w = b_last - bc + ig
        m_new = jnp.maximum(b_last + m_prev, jnp.max(g_w, axis=1, keepdims=True))
        ws = jnp.exp(g_w - m_new)
        carry = jnp.exp(b_last + m_prev - m_new)
        a_n = jnp.concatenate([ig - bc, jnp.zeros((L - 8, L), F32)], axis=0).T

        def conv_act(col):
            win = un_ref[pl.ds(r0, L + UOFF), col:col + HEAD_DIM]
            acc = bconv_ref[:, col:col + HEAD_DIM]
            for j in range(CONV_W):
                s0 = UOFF - (CONV_W - 1) + j
                acc = acc + win[s0:s0 + L] * wconv_ref[j:j + 1, col:col + HEAD_DIM]
            return _silu(acc)

        cos_f, sin_s = cos_ref[trows, :], sin_ref[trows, :]
        q = [conv_act(N_QK + h * HEAD_DIM) for h in heads]
        kb = [(conv_act(N_QK + MIX + h * HEAD_DIM) * k_scale).astype(BF16) for h in heads]
        qb = [a.astype(BF16) for a in q]
        rqb = [_rope(un_ref[rows, hcol(N_RQ, h)], cos_f, sin_s).astype(BF16) for h in heads]
        rk = [_rope(un_ref[rows, hcol(N_RK, h)], cos_f, sin_s) * k_scale for h in heads]
        v_t = [ut_ref[c, hcol(T_V, h), :] for h in heads]
        rvb = [ut_ref[c, hcol(T_RV, h), :].astype(BF16) for h in heads]
        c_old = [C_ref[h] for h in heads]
        s_old = [st_ref[h] for h in heads]
        n_old = [n_ref[h:h + 1, :] for h in heads]
        qk = [_dot_nt(kb[h], qb[h]) for h in heads]
        att = [_dot_nt(rk[h], rqb[h]) for h in heads]
        c_q = [_dot_nt(c_old[h], qb[h]) for h in heads]
        s_q = [_dot_nt(s_old[h], rqb[h]) for h in heads]
        n_q = [_dot_nt(jnp.broadcast_to(n_old[h], (8, HEAD_DIM)), qb[h])[0:1] for h in heads]
        d_c = [_dot(v_t[h] * ws[h:h + 1, :], kb[h]) for h in heads]
        d_n = [_dot(jnp.broadcast_to(ws[h:h + 1, :], (8, L)), kb[h])[0:1] for h in heads]
        d_s = [_dot(rvb[h], rk[h] * kdec_ref[h]) for h in heads]
        m_t, w_in, wts = [], [], []
        for h in heads:
            dmat = a_n[:, h:h + 1] + bc[h:h + 1, :]
            dmat = jnp.where(causal, dmat, -jnp.inf)
            m_t.append(jnp.maximum(inter[h:h + 1, :], jnp.max(dmat, axis=0, keepdims=True)))
            wts.append(jnp.exp(dmat - m_t[h]) * qk[h])
            w_in.append(jnp.exp(inter[h:h + 1, :] - m_t[h]))
        att_w = [att[h] * decay_ref[h] for h in heads]
        v_p = [_dot(v_t[h], wts[h]) for h in heads]
        v_a = [_dot(rvb[h], att_w[h]) for h in heads]
        for h in heads:
            num = v_p[h] + w_in[h] * c_q[h]
            den = jnp.sum(wts[h], axis=0, keepdims=True) + w_in[h] * n_q[h]
            hm = num / jnp.maximum(jnp.abs(den), jnp.exp(-m_t[h]))
            hm = hm * lax.rsqrt(jnp.mean(hm * hm, axis=0, keepdims=True) + EPS)
            hm = hm * gmh_ref[hcol(0, h), :] * jax.nn.sigmoid(ut_ref[c, hcol(T_O, h), :])
            h_ref[trows, hcol(0, h)] = hm.T.astype(BF16)
            carry_h = carry[h:h + 1, :]
            C_ref[h] = carry_h * c_old[h] + d_c[h]
            n_ref[h:h + 1, :] = carry_h * n_old[h] + d_n[h]
            o = v_a[h] + qdec_ref[h:h + 1, :] * s_q[h]
            st_ref[h] = cdec_ref[h:h + 1, :] * s_old[h] + d_s[h]
            hr = o * lax.rsqrt(jnp.mean(o * o, axis=0, keepdims=True) + EPS)
            hr = hr * grh_ref[hcol(0, h), :] * _silu(ut_ref[c, hcol(T_RG, h), :])
            h_ref[trows, hcol(MIX, h)] = hr.T.astype(BF16)

        m_ref[...] = m_new

    def step(proj_slot, scan_slot):
        un_ref = scan_slot[0]
        un_ref[0:UOFF, N_QK:N_RQ] = jnp.where(t == 0, 0.0, tail_ref[...])
        xn = _rms(x_ref[...], gmix_ref[...]).astype(BF16)
        for c in chunks:
            project(proj_slot, xn, c)
            chunk_body(c, scan_slot)
        tail_ref[...] = un_ref[tt:tt + UOFF, N_QK:N_RQ]

    @pl.when(s % 2 == 0)
    def _():
        step(slot_a, slot_b)

    @pl.when(s % 2 == 1)
    def _():
        step(slot_b, slot_a)

    @pl.when(t == tiles_per_seq - 1)
    def _():
        conv_ref[...] = tail_ref[UOFF - (CONV_W - 1):UOFF, :]
        for h in heads:
            S_ref[h] = st_ref[h].T


def _mixer_prompt(x, wn, wt, wgt, g_mix, bgt, w_conv, b_conv, gmh_cols, grh_cols, cos_f, sin_s,
                  decay_t, qdec_rows, kdec_cols, cdec_rows, tt=512):
    B, T, _ = x.shape
    tps = T // tt
    n_tiles = B * tps
    proj = lambda s: jnp.minimum(s, n_tiles - 1)
    scan = lambda s: jnp.maximum(s, 1) - 1
    per_b3 = lambda s: (scan(s) // tps, 0, 0)
    per_b4 = lambda s: (scan(s) // tps, 0, 0, 0)
    in_specs = [
        pl.BlockSpec((None, tt, D_MODEL), lambda s: (proj(s) // tps, proj(s) % tps, 0)),
        _const_spec((D_MODEL, N_COLS)), _const_spec((T_ROWS, D_MODEL)), _const_spec((GT_ROWS, D_MODEL)),
        _const_spec((1, D_MODEL)), _const_spec((GT_ROWS, CHUNK)),
        _const_spec((CONV_W, 2 * MIX)), _const_spec((1, 2 * MIX)),
        _const_spec((MIX, CHUNK)), _const_spec((MIX, CHUNK)),
        pl.BlockSpec((tt, HEAD_DIM), lambda s: (scan(s) % tps, 0)),
        pl.BlockSpec((tt, HEAD_DIM), lambda s: (scan(s) % tps, 0)),
        _const_spec((HEADS, CHUNK, CHUNK)),
        _const_spec((8, CHUNK)), _const_spec((HEADS, CHUNK, HEAD_DIM)), _const_spec((8, HEAD_DIM)),
    ]
    out_specs = [
        pl.BlockSpec((None, tt, D_MODEL), lambda s: (scan(s) // tps, scan(s) % tps, 0)),
        pl.BlockSpec((None, CONV_W - 1, 2 * MIX), per_b3),
        pl.BlockSpec((None, HEADS, HEAD_DIM, HEAD_DIM), per_b4),
        pl.BlockSpec((None, HEADS, HEAD_DIM), per_b3),
        pl.BlockSpec((None, 8, CHUNK), per_b3),
        pl.BlockSpec((None, HEADS, HEAD_DIM, HEAD_DIM), per_b4),
    ]
    out_shape = [
        jax.ShapeDtypeStruct((B, T, D_MODEL), BF16),
        jax.ShapeDtypeStruct((B, CONV_W - 1, 2 * MIX), F32),
        jax.ShapeDtypeStruct((B, HEADS, HEAD_DIM, HEAD_DIM), F32),
        jax.ShapeDtypeStruct((B, HEADS, HEAD_DIM), F32),
        jax.ShapeDtypeStruct((B, 8, CHUNK), F32),
        jax.ShapeDtypeStruct((B, HEADS, HEAD_DIM, HEAD_DIM), F32),
    ]
    n_chunks = tt // CHUNK
    slot = [
        pltpu.VMEM((tt + UOFF, N_COLS), F32),
        pltpu.VMEM((n_chunks, T_ROWS, CHUNK), F32),
        pltpu.VMEM((n_chunks, GT_ROWS, CHUNK), F32),
    ]
    scratch = slot + slot + [
        pltpu.VMEM((UOFF, 2 * MIX), F32),
        pltpu.VMEM((HEADS, HEAD_DIM, HEAD_DIM), F32),
    ]
    return pl.pallas_call(
        functools.partial(_mixer_prompt_kernel, tt=tt, tiles_per_seq=tps),
        grid=(n_tiles + 1,),
        in_specs=in_specs, out_specs=out_specs, out_shape=out_shape,
        scratch_shapes=scratch,
        compiler_params=_params("arbitrary"),
    )(x, wn, wt, wgt, g_mix, bgt, w_conv, b_conv, gmh_cols, grh_cols, cos_f, sin_s,
      decay_t, qdec_rows, kdec_cols, cdec_rows)


def _attn_prompt_kernel(x_ref, h_ref, k_ref, v_ref, wout_ref, wcq_ref, gx_ref, x1_ref, o_ref):
    tm = x_ref.shape[0]
    rows = [slice(r, r + tm // ROW_GROUPS) for r in range(0, tm, tm // ROW_GROUPS)]
    x1 = [x_ref[r, :] + jnp.dot(h_ref[r, :], wout_ref[...], preferred_element_type=F32)
          for r in rows]
    for r, a in zip(rows, x1):
        x1_ref[r, :] = a
    xq = [_rms(a, gx_ref[...]).astype(BF16) for a in x1]
    q = [jnp.dot(a, wcq_ref[...], preferred_element_type=F32).astype(BF16) for a in xq]
    sl = [slice(h * X_HEAD_DIM, (h + 1) * X_HEAD_DIM) for h in range(X_HEADS)]
    items = [(g, h) for g in range(ROW_GROUPS) for h in range(X_HEADS)]
    s = [_dot_nt(q[g][:, sl[h]], k_ref[:, sl[h]]) * (X_HEAD_DIM ** -0.5) for g, h in items]
    e = [jnp.exp(a - jnp.max(a, axis=-1, keepdims=True)) for a in s]
    p = [a / jnp.sum(a, axis=-1, keepdims=True) for a in e]
    for (g, h), a in zip(items, p):
        o_ref[rows[g], sl[h]] = _dot(a, v_ref[:, sl[h]]).astype(BF16)


def _attn_prompt(x, hcat, mk, mv, w_out, w_cq, g_x, tm=512):
    B, T, _ = x.shape
    tile = pl.BlockSpec((None, tm, D_MODEL), lambda b, t: (b, t, 0))
    kv = pl.BlockSpec((None, N_MEM, D_MODEL), lambda b, t: (b, 0, 0))
    return pl.pallas_call(
        _attn_prompt_kernel,
        grid=(B, T // tm),
        in_specs=[tile, tile, kv, kv, _const_spec((D_MODEL, D_MODEL)),
                  _const_spec((D_MODEL, D_MODEL)), _const_spec((1, D_MODEL))],
        out_specs=[tile, tile],
        out_shape=[jax.ShapeDtypeStruct((B, T, D_MODEL), F32),
                   jax.ShapeDtypeStruct((B, T, D_MODEL), BF16)],
        compiler_params=_params("parallel", "parallel"),
    )(x, hcat, mk, mv, w_out, w_cq, g_x)


V7X_MXU_DIM = 256
FF_CHUNKS = ((0, 6 * V7X_MXU_DIM), (6 * V7X_MXU_DIM, D_FF))


def _ffn_kernel(x1_ref, o_ref, wco_ref, wg_ref, wu_ref, wd_ref, gffn_ref, gfin_ref, y_ref, *,
                n_groups):
    tm = x1_ref.shape[0]
    rows = [slice(r, r + tm // n_groups) for r in range(0, tm, tm // n_groups)]
    acc = [x1_ref[r, :] + _dot(o_ref[r, :], wco_ref[...]) for r in rows]
    hf = [_rms(a, gffn_ref[...]).astype(BF16) for a in acc]
    for c0, c1 in FF_CHUNKS:
        gate = [jnp.dot(h, wg_ref[:, c0:c1], preferred_element_type=F32) for h in hf]
        up = [jnp.dot(h, wu_ref[:, c0:c1], preferred_element_type=F32) for h in hf]
        act = [_silu(g) * u for g, u in zip(gate, up)]
        acc = [a + _dot(p, wd_ref[c0:c1, :]) for a, p in zip(acc, act)]
    for r, a in zip(rows, acc):
        y_ref[r, :] = _rms(a, gfin_ref[...])


def _ffn(x1, o, w_co, w_gate, w_up, w_down, g_ffn, g_final, tm):
    rows = x1.shape[0]
    row_spec = pl.BlockSpec((tm, D_MODEL), lambda i: (i, 0))
    return pl.pallas_call(
        functools.partial(_ffn_kernel, n_groups=ROW_GROUPS if tm >= 512 else 1),
        grid=(rows // tm,),
        in_specs=[row_spec, row_spec, _const_spec((D_MODEL, D_MODEL)),
                  _const_spec((D_MODEL, D_FF)), _const_spec((D_MODEL, D_FF)),
                  _const_spec((D_FF, D_MODEL)), _const_spec((1, D_MODEL)), _const_spec((1, D_MODEL))],
        out_specs=row_spec,
        out_shape=jax.ShapeDtypeStruct((rows, D_MODEL), F32),
        compiler_params=_params("parallel"),
    )(x1, o, w_co, w_gate, w_up, w_down, g_ffn, g_final)


def _inproj_sample_kernel(x_ref, g_ref, wn_ref, wt_ref, wgt_ref, un_ref, ut_ref, gs_ref):
    xn = _rms(x_ref[...], g_ref[...]).astype(BF16)
    un_ref[...] = jnp.dot(xn, wn_ref[...], preferred_element_type=F32)
    ut_ref[...] = lax.dot_general(xn, wt_ref[...], NT_DIMS, preferred_element_type=F32)
    gs_ref[...] = lax.dot_general(xn, wgt_ref[...], NT_DIMS, preferred_element_type=F32)


def _inproj_sample(x, g_mix, wn, wt, wgt):
    rows = x.shape[0]
    full = lambda cols: pl.BlockSpec((rows, cols), lambda i: (0, 0))
    return pl.pallas_call(
        _inproj_sample_kernel,
        grid=(1,),
        in_specs=[_const_spec((rows, D_MODEL)), _const_spec((1, D_MODEL)),
                  _const_spec((D_MODEL, N_COLS)), _const_spec((T_ROWS, D_MODEL)),
                  _const_spec((GT_ROWS, D_MODEL))],
        out_specs=[full(N_COLS), full(T_ROWS), full(GT_ROWS)],
        out_shape=[jax.ShapeDtypeStruct((rows, N_COLS), F32),
                   jax.ShapeDtypeStruct((rows, T_ROWS), F32),
                   jax.ShapeDtypeStruct((rows, GT_ROWS), F32)],
        compiler_params=_params("arbitrary"),
    )(x, g_mix, wn, wt, wgt)


def _mixer_sample_kernel(un_ref, ut_ref, gs_ref, conv_ref, C_ref, n_ref, m_ref, S_ref, bg_ref,
                         wconv_ref, bconv_ref, gmh_ref, grh_ref, cos_ref, sin_ref, rtab_ref,
                         h_ref, convo_ref, Co_ref, no_ref, mo_ref, So_ref,
                         q_s, k_s, vws_s, qr_s, kr_s, qc_s, qs_s, carry_s, *, bb):
    k_scale = HEAD_DIM ** -0.5
    uqk = un_ref[:, N_QK:N_RQ]
    conv = (bconv_ref[...] + wconv_ref[0:1, :] * conv_ref[0] + wconv_ref[1:2, :] * conv_ref[1]
            + wconv_ref[2:3, :] * conv_ref[2] + wconv_ref[3:4, :] * uqk)
    convo_ref[0] = conv_ref[1]
    convo_ref[1] = conv_ref[2]
    convo_ref[2] = uqk
    qk_act = _silu(conv)
    q_s[...] = qk_act[:, 0:MIX]
    k_s[...] = qk_act[:, MIX:2 * MIX] * k_scale

    gates = gs_ref[...] + bg_ref[...]
    ig = gates[:, 0:HEADS]
    lf = _log_sigmoid(gates[:, 8:8 + HEADS])
    inter = lf + m_ref[...]
    m_t = jnp.maximum(inter, ig)
    ws = jnp.exp(ig - m_t)
    w_in = jnp.exp(inter - m_t)
    mo_ref[...] = m_t
    carry_s[:, 0:HEADS] = w_in

    cos_f = cos_ref[...]
    sin_s = sin_ref[...]
    for h in range(HEADS):
        lo = h * HEAD_DIM
        hs = slice(lo, lo + HEAD_DIM)
        vws_s[:, hs] = ut_ref[:, T_V + lo:T_V + lo + HEAD_DIM] * ws[:, h:h + 1]
        qr_s[:, hs] = _rope(un_ref[:, N_RQ + lo:N_RQ + lo + HEAD_DIM], cos_f, sin_s)
        kr_s[:, hs] = _rope(un_ref[:, N_RK + lo:N_RK + lo + HEAD_DIM], cos_f, sin_s) * k_scale

    heads = range(HEADS)
    seqs = range(bb)
    hsl = [slice(h * HEAD_DIM, (h + 1) * HEAD_DIM) for h in heads]
    seq_id = lax.broadcasted_iota(jnp.int32, (bb, HEAD_DIM), 0)
    for h in heads:
        q_h, k_h, vws_h = q_s[:, hsl[h]], k_s[:, hsl[h]], vws_s[:, hsl[h]]
        qr_h = qr_s[:, hsl[h]]
        krd_h = kr_s[:, hsl[h]] * rtab_ref[1:2, h:h + 1]
        rv_h = ut_ref[:, T_RV + h * HEAD_DIM:T_RV + (h + 1) * HEAD_DIM]
        c_old = [C_ref[b, h] for b in seqs]
        s_old = [S_ref[b, h] for b in seqs]
        q_c = [_dot_nt(q_h, c_old[b]) for b in seqs]
        q_st = [_dot(qr_h, s_old[b]) for b in seqs]
        d_c = [_dot_tn(jnp.where(seq_id == b, vws_h, 0.0), k_h) for b in seqs]
        d_s = [_dot_tn(jnp.where(seq_id == b, krd_h, 0.0), rv_h) for b in seqs]
        for b in seqs:
            row = slice(b, b + 1)
            qc_s[row, hsl[h]] = q_c[b][row]
            qs_s[row, hsl[h]] = q_st[b][row]
            Co_ref[b, h] = carry_s[row, h:h + 1] * c_old[b] + d_c[b]
            So_ref[b, h] = rtab_ref[2:3, h:h + 1] * s_old[b] + d_s[b]

    for h in range(HEADS):
        lo = h * HEAD_DIM
        hs = slice(lo, lo + HEAD_DIM)
        q_h, k_h, n_h = q_s[:, hs], k_s[:, hs], n_ref[:, hs]
        ws_h, w_in_h = ws[:, h:h + 1], w_in[:, h:h + 1]
        wts = ws_h * jnp.sum(q_h * k_h, axis=1, keepdims=True)
        num = wts * ut_ref[:, T_V + lo:T_V + lo + HEAD_DIM] + w_in_h * qc_s[:, hs]
        den = wts + w_in_h * jnp.sum(n_h * q_h, axis=1, keepdims=True)
        hm = num / jnp.maximum(jnp.abs(den), jnp.exp(-m_t[:, h:h + 1]))
        mo = ut_ref[:, T_O + lo:T_O + lo + HEAD_DIM]
        h_ref[:, hs] = _head_norm(hm) * gmh_ref[:, hs] * jax.nn.sigmoid(mo)
        no_ref[:, hs] = w_in_h * n_h + ws_h * k_h

        att = jnp.sum(qr_s[:, hs] * kr_s[:, hs], axis=1, keepdims=True) * rtab_ref[3:4, h:h + 1]
        o = (att * ut_ref[:, T_RV + lo:T_RV + lo + HEAD_DIM]
             + rtab_ref[0:1, h:h + 1] * qs_s[:, hs])
        rg = ut_ref[:, T_RG + lo:T_RG + lo + HEAD_DIM]
        h_ref[:, MIX + lo:MIX + lo + HEAD_DIM] = _head_norm(o) * grh_ref[:, hs] * _silu(rg)


def _mixer_sample(un, ut, gs, conv_t, C, n2, m, S, bg, w_conv, b_conv, g_mh, g_rh, cos_f, sin_s, rtab,
                  bb=8):
    B = un.shape[0]
    rows = lambda cols: pl.BlockSpec((bb, cols), lambda i: (i, 0))
    mats = pl.BlockSpec((bb, HEADS, HEAD_DIM, HEAD_DIM), lambda i: (i, 0, 0, 0))
    convs = pl.BlockSpec((CONV_W - 1, bb, 2 * MIX), lambda i: (0, i, 0))
    in_specs = [rows(N_COLS), rows(T_ROWS), rows(GT_ROWS), convs, mats, rows(MIX), rows(HEADS), mats,
                _const_spec((1, GT_ROWS)), _const_spec((CONV_W, 2 * MIX)), _const_spec((1, 2 * MIX)),
                _const_spec((1, MIX)), _const_spec((1, MIX)),
                _const_spec((1, HEAD_DIM)), _const_spec((1, HEAD_DIM)), _const_spec((8, HEAD_DIM))]
    out_specs = [rows(D_MODEL), convs, mats, rows(MIX), rows(HEADS), mats]
    out_shape = [jax.ShapeDtypeStruct((B, D_MODEL), F32),
                 jax.ShapeDtypeStruct((CONV_W - 1, B, 2 * MIX), F32),
                 jax.ShapeDtypeStruct((B, HEADS, HEAD_DIM, HEAD_DIM), F32),
                 jax.ShapeDtypeStruct((B, MIX), F32),
                 jax.ShapeDtypeStruct((B, HEADS), F32),
                 jax.ShapeDtypeStruct((B, HEADS, HEAD_DIM, HEAD_DIM), F32)]
    scratch = [pltpu.VMEM((bb, MIX), F32) for _ in range(7)] + [pltpu.VMEM((bb, HEAD_DIM), F32)]
    return pl.pallas_call(
        functools.partial(_mixer_sample_kernel, bb=bb),
        grid=(B // bb,),
        in_specs=in_specs, out_specs=out_specs, out_shape=out_shape,
        scratch_shapes=scratch,
        compiler_params=_params("parallel"),
    )(un, ut, gs, conv_t, C, n2, m, S, bg, w_conv, b_conv, g_mh, g_rh, cos_f, sin_s, rtab)


def _outq_sample_kernel(x_ref, h_ref, wout_ref, wcq_ref, gx_ref, x1_ref, q_ref):
    x1 = x_ref[...] + _dot(h_ref[...], wout_ref[...])
    x1_ref[...] = x1
    q_ref[...] = _dot(_rms(x1, gx_ref[...]), wcq_ref[...])


def _outq_sample(x, hcat, w_out, w_cq, g_x):
    rows = x.shape[0]
    full = _const_spec((rows, D_MODEL))
    return pl.pallas_call(
        _outq_sample_kernel,
        grid=(1,),
        in_specs=[full, full, _const_spec((D_MODEL, D_MODEL)), _const_spec((D_MODEL, D_MODEL)),
                  _const_spec((1, D_MODEL))],
        out_specs=[pl.BlockSpec((rows, D_MODEL), lambda i: (0, 0))] * 2,
        out_shape=[jax.ShapeDtypeStruct((rows, D_MODEL), F32)] * 2,
        compiler_params=_params("arbitrary"),
    )(x, hcat, w_out, w_cq, g_x)


def _attn_sample_kernel(q_ref, k_ref, v_ref, o_ref, *, ba):
    r_id = lax.broadcasted_iota(jnp.int32, (QROWS, KV_ROWS), 0)
    n_id = lax.broadcasted_iota(jnp.int32, (QROWS, KV_ROWS), 1)
    own = (n_id & 7) == (r_id >> 1) + 4 * (r_id & 1)
    low_half = (lax.broadcasted_iota(jnp.int32, (1, KV_ROWS), 1) & 4) == 0
    seqs = range(ba)
    z = [_dot_nt(q_ref[j], k_ref[j]) for j in seqs]
    zc = [jnp.sum(jnp.where(own, z[j], 0.0), axis=0, keepdims=True) for j in seqs]
    other = [jnp.where(low_half, pltpu.roll(zc[j], KV_ROWS - 4, 1), pltpu.roll(zc[j], 4, 1))
             for j in seqs]
    s = [jnp.where(own, (zc[j] + other[j]) * (X_HEAD_DIM ** -0.5), -jnp.inf) for j in seqs]
    m = [jnp.max(s[j], axis=-1, keepdims=True) for j in seqs]
    e = [jnp.exp(s[j] - m[j]) for j in seqs]
    den = [jnp.sum(e[j], axis=-1, keepdims=True) for j in seqs]
    for j in seqs:
        o_ref[j] = _dot(e[j] / den[j], v_ref[j])


def _attn_sample(q, ck, cv, ba=8):
    B = q.shape[0]
    kv = pl.BlockSpec((ba, KV_ROWS, 128), lambda i: (i, 0, 0))
    rows = pl.BlockSpec((ba, QROWS, 128), lambda i: (i, 0, 0))
    return pl.pallas_call(
        functools.partial(_attn_sample_kernel, ba=ba),
        grid=(B // ba,),
        in_specs=[rows, kv, kv],
        out_specs=rows,
        out_shape=jax.ShapeDtypeStruct((B, QROWS, 128), F32),
        compiler_params=_params("parallel"),
    )(q.reshape(B, QROWS, 128), ck, cv).reshape(B, D_MODEL)


def _kv_rows(cache):
    B = cache.shape[0]
    c5 = cache.reshape(B, N_MEM, X_HEADS, 2, 128)
    return jnp.transpose(c5, (0, 1, 3, 2, 4)).reshape(B, KV_ROWS, 128)


def _kv_from_rows(rows):
    B = rows.shape[0]
    r5 = rows.reshape(B, N_MEM, 2, X_HEADS, 128)
    return jnp.transpose(r5, (0, 1, 3, 2, 4)).reshape(1, B, N_MEM, X_HEADS, X_HEAD_DIM)


def _rope_tables(pos):
    half = HEAD_DIM // 2
    inv = ROPE_THETA ** (-np.arange(half, dtype=np.float64) / half)
    ang = np.asarray(pos, np.float64)[:, None] * inv[None, :]
    cos, sin = np.cos(ang), np.sin(ang)
    return (np.concatenate([cos, cos], axis=-1).astype(np.float32),
            np.concatenate([-sin, sin], axis=-1).astype(np.float32))


def _retention_tables(L):
    lg = np.log1p(-np.exp2(-5.0 - np.arange(HEADS, dtype=np.float64)))
    t = np.arange(L, dtype=np.float64)
    diff = t[:, None] - t[None, :]
    decay = np.where(diff >= 0, np.exp(lg[:, None, None] * np.maximum(diff, 0.0)), 0.0)
    q_dec = np.exp(lg[:, None] * (t + 1.0))
    k_dec = np.exp(lg[:, None] * (L - 1.0 - t))
    chunk_dec = np.exp(lg * L)
    return tuple(a.astype(np.float32) for a in (decay, q_dec, k_dec, chunk_dec))


def _lanes(a, n):
    xp = np if isinstance(a, np.ndarray) else jnp
    return xp.broadcast_to(a[..., None], a.shape + (n,))


def _pad_rows(a, rows):
    return np.pad(a, ((0, rows - a.shape[0]),) + ((0, 0),) * (a.ndim - 1))


def kernel(x_prompt, x_sample, cache_mem_k, cache_mem_v, state_mlstm_conv, state_mlstm_C, state_mlstm_n, state_mlstm_m, state_ret_S, mem_prompt, w_in, b_gate, w_conv, b_conv, g_mix, g_mhead, g_rhead, w_out, g_xattn, g_mem, w_ck, w_cv, w_cq, w_co, g_ffn, w_gate, w_up, w_down, g_final):
    Bp, Tp, _ = x_prompt.shape
    Bs = x_sample.shape[0]
    l = 0
    n_m = 4 * MIX
    wi = w_in[l]
    w_gates = wi[:, n_m:n_m + 2 * HEADS]
    w_ret = wi[:, n_m + 2 * HEADS:]
    row = lambda a: a.reshape(1, -1)
    bf = lambda a: a.astype(BF16)
    g_mix_r, g_mh_r, g_rh_r = row(g_mix[l]), row(g_mhead[l]), row(g_rhead[l])
    g_x_r, g_mem_r, g_ffn_r, g_fin_r = row(g_xattn[l]), row(g_mem[l]), row(g_ffn[l]), row(g_final)
    b_conv_r = row(b_conv[l])
    w_out_b, w_cq_b, w_co_b = bf(w_out[l]), bf(w_cq[l]), bf(w_co[l])
    w_gate_b, w_up_b, w_down_b = bf(w_gate[l]), bf(w_up[l]), bf(w_down[l])

    mk_b, mv_b, mk_rows, mv_rows = _memkv(mem_prompt, g_mem_r, bf(w_ck[l]), bf(w_cv[l]))
    wn = bf(jnp.concatenate([wi[:, :2 * MIX], w_ret[:, :2 * MIX]], axis=1))
    wt = bf(jnp.concatenate([wi[:, 2 * MIX:n_m], w_ret[:, 2 * MIX:]], axis=1).T)
    gate_rows = lambda a: jnp.concatenate(
        [a[:HEADS], jnp.zeros((8 - HEADS,) + a.shape[1:], F32),
         a[HEADS:], jnp.zeros((8 - HEADS,) + a.shape[1:], F32)], axis=0)
    wgt = bf(gate_rows(w_gates.T))
    bg_rows = gate_rows(b_gate[l][:, None])
    bgt = jnp.broadcast_to(bg_rows, (GT_ROWS, CHUNK))
    cos_p, sin_p = _rope_tables(np.arange(Tp))
    decay, q_dec, k_dec, chunk_dec = _retention_tables(CHUNK)
    hcat_p, conv_p, C_p, n_p, m_p, S_p = _mixer_prompt(
        x_prompt, wn, wt, wgt, g_mix_r, bgt, w_conv[l], b_conv_r,
        _lanes(g_mhead[l], CHUNK), _lanes(g_rhead[l], CHUNK), cos_p, sin_p,
        np.swapaxes(decay, 1, 2), _pad_rows(q_dec, 8), _lanes(k_dec, HEAD_DIM),
        _pad_rows(_lanes(chunk_dec, HEAD_DIM), 8))
    x1_p, o_p = _attn_prompt(x_prompt, hcat_p, mk_b, mv_b, w_out_b, w_cq_b, g_x_r)
    y_p = _ffn(x1_p.reshape(Bp * Tp, D_MODEL), o_p.reshape(Bp * Tp, D_MODEL),
               w_co_b, w_gate_b, w_up_b, w_down_b, g_ffn_r, g_fin_r, tm=512)

    xs = x_sample.reshape(Bs, D_MODEL)
    un_s, ut_s, gs_s = _inproj_sample(xs, g_mix_r, wn, wt, wgt)
    cos_s, sin_s = _rope_tables(PAST_LEN + np.arange(1))
    decay1, q_dec1, k_dec1, chunk_dec1 = _retention_tables(1)
    rtab = np.zeros((8, HEAD_DIM), np.float32)
    rtab[:4, :HEADS] = np.stack([q_dec1[:, 0], k_dec1[:, 0], chunk_dec1, decay1[:, 0, 0]])
    hcat_s, conv_s, C_s, n_s, m_s, S_s = _mixer_sample(
        un_s, ut_s, gs_s, jnp.transpose(state_mlstm_conv[l], (1, 0, 2)), state_mlstm_C[l],
        state_mlstm_n[l].reshape(Bs, MIX), state_mlstm_m[l], state_ret_S[l],
        bg_rows.reshape(1, GT_ROWS), w_conv[l], b_conv_r, g_mh_r, g_rh_r, cos_s, sin_s, rtab)
    x1_s, q_s = _outq_sample(xs, hcat_s, w_out_b, w_cq_b, g_x_r)
    o_s = _attn_sample(q_s, _kv_rows(cache_mem_k[l]), _kv_rows(cache_mem_v[l]))
    y_s = _ffn(x1_s, o_s, w_co_b, w_gate_b, w_up_b, w_down_b, g_ffn_r, g_fin_r, tm=Bs)

    return (y_p.reshape(Bp, Tp, D_MODEL), y_s.reshape(Bs, 1, D_MODEL),
            _kv_from_rows(mk_rows), _kv_from_rows(mv_rows),
            conv_p[None], C_p[None], n_p[None], m_p[None, :, :HEADS, 0], S_p[None],
            jnp.transpose(conv_s, (1, 0, 2))[None], C_s[None],
            n_s.reshape(1, Bs, HEADS, HEAD_DIM), m_s[None], S_s[None])
```

```python
import functools

import jax
import jax.numpy as jnp
import numpy as np
from jax import lax
from jax.experimental import pallas as pl
from jax.experimental.pallas import tpu as pltpu

F32 = jnp.float32
BF16 = jnp.bfloat16

D_MODEL = 1024
HEADS = 4
HEAD_DIM = 128
MIX = HEADS * HEAD_DIM
CONV_W = 4
CHUNK = 128
N_MEM = 256
X_HEADS = 4
X_HEAD_DIM = 256
D_FF = 2816
ROPE_THETA = 10000.0
EPS = 1e-6
PAST_LEN = 16384
KV_ROWS = N_MEM * 2 * X_HEADS
QROWS = 2 * X_HEADS

N_QK, N_RQ, N_RK, N_COLS = 0, 1024, 1536, 2048
T_V, T_O, T_RV, T_RG, T_ROWS = 0, 512, 1024, 1536, 2048
GT_ROWS = 16
UOFF = 8
ROW_GROUPS = 2

V7X_VMEM_LIMIT = 56 * 1024 * 1024

NT_DIMS = (((1,), (1,)), ((), ()))
TN_DIMS = (((0,), (0,)), ((), ()))


def _dot(a, b):
    return jnp.dot(a.astype(BF16), b.astype(BF16), preferred_element_type=F32)


def _dot_nt(a, b):
    return lax.dot_general(a.astype(BF16), b.astype(BF16), NT_DIMS, preferred_element_type=F32)


def _dot_tn(a, b):
    return lax.dot_general(a.astype(BF16), b.astype(BF16), TN_DIMS, preferred_element_type=F32)


def _rms(x, g):
    return x * lax.rsqrt(jnp.mean(x * x, axis=-1, keepdims=True) + EPS) * g


def _head_norm(h):
    return h * lax.rsqrt(jnp.mean(h * h, axis=-1, keepdims=True) + EPS)


def _silu(x):
    return x * jax.nn.sigmoid(x)


def _log_sigmoid(x):
    return jnp.minimum(x, 0.0) - jnp.log1p(jnp.exp(-jnp.abs(x)))


def _rope(x, cos_full, sin_signed):
    return x * cos_full + pltpu.roll(x, HEAD_DIM // 2, 1) * sin_signed


def _const_spec(shape):
    zeros = (0,) * len(shape)
    return pl.BlockSpec(shape, lambda *_: zeros, pipeline_mode=pl.Buffered(1))


def _params(*sem):
    return pltpu.CompilerParams(dimension_semantics=sem, vmem_limit_bytes=V7X_VMEM_LIMIT)


def _memkv_kernel(mem_ref, g_ref, wk_ref, wv_ref, kb_ref, vb_ref, krows_ref, vrows_ref):
    mn = _rms(mem_ref[...], g_ref[...]).astype(BF16)
    for w_ref, b_ref, rows_ref in ((wk_ref, kb_ref, krows_ref), (wv_ref, vb_ref, vrows_ref)):
        proj = jnp.dot(mn, w_ref[...], preferred_element_type=F32)
        b_ref[...] = proj.astype(BF16)
        for h in range(X_HEADS):
            for c in range(2):
                lane0 = h * X_HEAD_DIM + c * 128
                rows_ref[pl.ds(c * X_HEADS + h, N_MEM, stride=QROWS), :] = proj[:, lane0:lane0 + 128]


def _memkv(mem, g_mem, wk, wv):
    B = mem.shape[0]
    tok = pl.BlockSpec((None, N_MEM, D_MODEL), lambda b: (b, 0, 0))
    rows = pl.BlockSpec((None, KV_ROWS, 128), lambda b: (b, 0, 0))
    return pl.pallas_call(
        _memkv_kernel,
        grid=(B,),
        in_specs=[tok, _const_spec((1, D_MODEL)),
                  _const_spec((D_MODEL, D_MODEL)), _const_spec((D_MODEL, D_MODEL))],
        out_specs=[tok, tok, rows, rows],
        out_shape=[jax.ShapeDtypeStruct((B, N_MEM, D_MODEL), BF16)] * 2
        + [jax.ShapeDtypeStruct((B, KV_ROWS, 128), F32)] * 2,
        compiler_params=_params("parallel"),
    )(mem, g_mem, wk, wv)


def _mixer_prompt_kernel(x_ref, wn_ref, wt_ref, wgt_ref, gmix_ref, bgt_ref, wconv_ref, bconv_ref,
                         gmh_ref, grh_ref, cos_ref, sin_ref, decay_ref, qdec_ref, kdec_ref, cdec_ref,
                         h_ref, conv_ref, C_ref, n_ref, m_ref, S_ref,
                         un_a, ut_a, gt_a, un_b, ut_b, gt_b, tail_ref, st_ref, *, tt, tiles_per_seq):
    L = CHUNK
    s = pl.program_id(0)
    t = (jnp.maximum(s, 1) - 1) % tiles_per_seq
    chunks = range(tt // L)
    slot_a, slot_b = (un_a, ut_a, gt_a), (un_b, ut_b, gt_b)

    @pl.when(s == 0)
    def _():
        for ref in slot_b + (tail_ref,):
            ref[...] = jnp.zeros_like(ref)

    @pl.when(t == 0)
    def _():
        C_ref[...] = jnp.zeros_like(C_ref)
        st_ref[...] = jnp.zeros_like(st_ref)
        n_ref[...] = jnp.zeros_like(n_ref)
        m_ref[...] = jnp.zeros_like(m_ref)

    k_scale = HEAD_DIM ** -0.5

    def project(slot, xn, part):
        un_ref, ut_ref, gt_ref = slot
        c0 = part * MIX
        un_ref[UOFF:UOFF + tt, c0:c0 + MIX] = jnp.dot(xn, wn_ref[:, c0:c0 + MIX],
                                                      preferred_element_type=F32)
        res = lax.dot_general(wt_ref[c0:c0 + MIX, :], xn, NT_DIMS, preferred_element_type=F32)
        for c in chunks:
            ut_ref[c, c0:c0 + MIX, :] = res[:, c * L:(c + 1) * L]
        if part == 0:
            gates_t = lax.dot_general(wgt_ref[...], xn, NT_DIMS, preferred_element_type=F32)
            for c in chunks:
                gt_ref[c] = gates_t[:, c * L:(c + 1) * L]

    src_id = lax.broadcasted_iota(jnp.int32, (L, L), 0)
    tgt_id = lax.broadcasted_iota(jnp.int32, (L, L), 1)
    causal = src_id <= tgt_id
    triu_bf = jnp.where(causal, 1.0, 0.0).astype(BF16)
    heads = range(HEADS)
    hcol = lambda base, h: slice(base + h * HEAD_DIM, base + (h + 1) * HEAD_DIM)

    def chunk_body(c, slot):
        un_ref, ut_ref, gt_ref = slot
        r0 = c * L
        rows = pl.ds(UOFF + r0, L)
        trows = pl.ds(r0, L)
        g_t = gt_ref[c] + bgt_ref[...]
        ig = g_t[0:8]
        lf = _log_sigmoid(g_t[8:16])
        lf_hi = lf.astype(BF16)
        r1 = lf - lf_hi.astype(F32)
        lf_mid = r1.astype(BF16)
        lf_lo = (r1 - lf_mid.astype(F32)).astype(BF16)
        bc = (jnp.dot(lf_hi, triu_bf, preferred_element_type=F32)
              + jnp.dot(lf_mid, triu_bf, preferred_element_type=F32)
              + jnp.dot(lf_lo, triu_bf, preferred_element_type=F32))
        m_prev = m_ref[...]
        inter = bc + m_prev
        b_last = bc[:, L - 1:L]
        g_w = b_last - bc + ig
        m_new = jnp.maximum(b_last + m_prev, jnp.max(g_w, axis=1, keepdims=True))
        ws = jnp.exp(g_w - m_new)
        carry = jnp.exp(b_last + m_prev - m_new)
        a_n = jnp.concatenate([ig - bc, jnp.zeros((L - 8, L), F32)], axis=0).T

        def conv_act(col):
            win = un_ref[pl.ds(r0, L + UOFF), col:col + HEAD_DIM]
            acc = bconv_ref[:, col:col + HEAD_DIM]
            for j in range(CONV_W):
                s0 = UOFF - (CONV_W - 1) + j
                acc = acc + win[s0:s0 + L] * wconv_ref[j:j + 1, col:col + HEAD_DIM]
            return _silu(acc)

        cos_f, sin_s = cos_ref[trows, :], sin_ref[trows, :]
        q = [conv_act(N_QK + h * HEAD_DIM) for h in heads]
        kb = [(conv_act(N_QK + MIX + h * HEAD_DIM) * k_scale).astype(BF16) for h in heads]
        qb = [a.astype(BF16) for a in q]
        rqb = [_rope(un_ref[rows, hcol(N_RQ, h)], cos_f, sin_s).astype(BF16) for h in heads]
        rk = [_rope(un_ref[rows, hcol(N_RK, h)], cos_f, sin_s) * k_scale for h in heads]
        v_t = [ut_ref[c, hcol(T_V, h), :] for h in heads]
        rvb = [ut_ref[c, hcol(T_RV, h), :].astype(BF16) for h in heads]
        c_old = [C_ref[h] for h in heads]
        s_old = [st_ref[h] for h in heads]
        n_old = [n_ref[h:h + 1, :] for h in heads]
        qk = [_dot_nt(kb[h], qb[h]) for h in heads]
        att = [_dot_nt(rk[h], rqb[h]) for h in heads]
        c_q = [_dot_nt(c_old[h], qb[h]) for h in heads]
        s_q = [_dot_nt(s_old[h], rqb[h]) for h in heads]
        n_q = [_dot_nt(jnp.broadcast_to(n_old[h], (8, HEAD_DIM)), qb[h])[0:1] for h in heads]
        d_c = [_dot(v_t[h] * ws[h:h + 1, :], kb[h]) for h in heads]
        d_n = [_dot(jnp.broadcast_to(ws[h:h + 1, :], (8, L)), kb[h])[0:1] for h in heads]
        d_s = [_dot(rvb[h], rk[h] * kdec_ref[h]) for h in heads]
        m_t, w_in, wts = [], [], []
        for h in heads:
            dmat = a_n[:, h:h + 1] + bc[h:h + 1, :]
            dmat = jnp.where(causal, dmat, -jnp.inf)
            m_t.append(jnp.maximum(inter[h:h + 1, :], jnp.max(dmat, axis=0, keepdims=True)))
            wts.append(jnp.exp(dmat - m_t[h]) * qk[h])
            w_in.append(jnp.exp(inter[h:h + 1, :] - m_t[h]))
        att_w = [att[h] * decay_ref[h] for h in heads]
        v_p = [_dot(v_t[h], wts[h]) for h in heads]
        v_a = [_dot(rvb[h], att_w[h]) for h in heads]
        for h in heads:
            num = v_p[h] + w_in[h] * c_q[h]
            den = jnp.sum(wts[h], axis=0, keepdims=True) + w_in[h] * n_q[h]
            hm = num / jnp.maximum(jnp.abs(den), jnp.exp(-m_t[h]))
            hm = hm * lax.rsqrt(jnp.mean(hm * hm, axis=0, keepdims=True) + EPS)
            hm = hm * gmh_ref[hcol(0, h), :] * jax.nn.sigmoid(ut_ref[c, hcol(T_O, h), :])
            h_ref[trows, hcol(0, h)] = hm.T.astype(BF16)
            carry_h = carry[h:h + 1, :]
            C_ref[h] = carry_h * c_old[h] + d_c[h]
            n_ref[h:h + 1, :] = carry_h * n_old[h] + d_n[h]
            o = v_a[h] + qdec_ref[h:h + 1, :] * s_q[h]
            st_ref[h] = cdec_ref[h:h + 1, :] * s_old[h] + d_s[h]
            hr = o * lax.rsqrt(jnp.mean(o * o, axis=0, keepdims=True) + EPS)
            hr = hr * grh_ref[hcol(0, h), :] * _silu(ut_ref[c, hcol(T_RG, h), :])
            h_ref[trows, hcol(MIX, h)] = hr.T.astype(BF16)

        m_ref[...] = m_new

    def step(proj_slot, scan_slot):
        un_ref = scan_slot[0]
        un_ref[0:UOFF, N_QK:N_RQ] = jnp.where(t == 0, 0.0, tail_ref[...])
        xn = _rms(x_ref[...], gmix_ref[...]).astype(BF16)
        for c in chunks:
            project(proj_slot, xn, c)
            chunk_body(c, scan_slot)
        tail_ref[...] = un_ref[tt:tt + UOFF, N_QK:N_RQ]

    @pl.when(s % 2 == 0)
    def _():
        step(slot_a, slot_b)

    @pl.when(s % 2 == 1)
    def _():
        step(slot_b, slot_a)

    @pl.when(t == tiles_per_seq - 1)
    def _():
        conv_ref[...] = tail_ref[UOFF - (CONV_W - 1):UOFF, :]
        for h in heads:
            S_ref[h] = st_ref[h].T


def _mixer_prompt(x, wn, wt, wgt, g_mix, bgt, w_conv, b_conv, gmh_cols, grh_cols, cos_f, sin_s,
                  decay_t, qdec_rows, kdec_cols, cdec_rows, tt=512):
    B, T, _ = x.shape
    tps = T // tt
    n_tiles = B * tps
    proj = lambda s: jnp.minimum(s, n_tiles - 1)
    scan = lambda s: jnp.maximum(s, 1) - 1
    per_b3 = lambda s: (scan(s) // tps, 0, 0)
    per_b4 = lambda s: (scan(s) // tps, 0, 0, 0)
    in_specs = [
        pl.BlockSpec((None, tt, D_MODEL), lambda s: (proj(s) // tps, proj(s) % tps, 0)),
        _const_spec((D_MODEL, N_COLS)), _const_spec((T_ROWS, D_MODEL)), _const_spec((GT_ROWS, D_MODEL)),
        _const_spec((1, D_MODEL)), _const_spec((GT_ROWS, CHUNK)),
        _const_spec((CONV_W, 2 * MIX)), _const_spec((1, 2 * MIX)),
        _const_spec((MIX, CHUNK)), _const_spec((MIX, CHUNK)),
        pl.BlockSpec((tt, HEAD_DIM), lambda s: (scan(s) % tps, 0)),
        pl.BlockSpec((tt, HEAD_DIM), lambda s: (scan(s) % tps, 0)),
        _const_spec((HEADS, CHUNK, CHUNK)),
        _const_spec((8, CHUNK)), _const_spec((HEADS, CHUNK, HEAD_DIM)), _const_spec((8, HEAD_DIM)),
    ]
    out_specs = [
        pl.BlockSpec((None, tt, D_MODEL), lambda s: (scan(s) // tps, scan(s) % tps, 0)),
        pl.BlockSpec((None, CONV_W - 1, 2 * MIX), per_b3),
        pl.BlockSpec((None, HEADS, HEAD_DIM, HEAD_DIM), per_b4),
        pl.BlockSpec((None, HEADS, HEAD_DIM), per_b3),
        pl.BlockSpec((None, 8, CHUNK), per_b3),
        pl.BlockSpec((None, HEADS, HEAD_DIM, HEAD_DIM), per_b4),
    ]
    out_shape = [
        jax.ShapeDtypeStruct((B, T, D_MODEL), BF16),
        jax.ShapeDtypeStruct((B, CONV_W - 1, 2 * MIX), F32),
        jax.ShapeDtypeStruct((B, HEADS, HEAD_DIM, HEAD_DIM), F32),
        jax.ShapeDtypeStruct((B, HEADS, HEAD_DIM), F32),
        jax.ShapeDtypeStruct((B, 8, CHUNK), F32),
        jax.ShapeDtypeStruct((B, HEADS, HEAD_DIM, HEAD_DIM), F32),
    ]
    n_chunks = tt // CHUNK
    slot = [
        pltpu.VMEM((tt + UOFF, N_COLS), F32),
        pltpu.VMEM((n_chunks, T_ROWS, CHUNK), F32),
        pltpu.VMEM((n_chunks, GT_ROWS, CHUNK), F32),
    ]
    scratch = slot + slot + [
        pltpu.VMEM((UOFF, 2 * MIX), F32),
        pltpu.VMEM((HEADS, HEAD_DIM, HEAD_DIM), F32),
    ]
    return pl.pallas_call(
        functools.partial(_mixer_prompt_kernel, tt=tt, tiles_per_seq=tps),
        grid=(n_tiles + 1,),
        in_specs=in_specs, out_specs=out_specs, out_shape=out_shape,
        scratch_shapes=scratch,
        compiler_params=_params("arbitrary"),
    )(x, wn, wt, wgt, g_mix, bgt, w_conv, b_conv, gmh_cols, grh_cols, cos_f, sin_s,
      decay_t, qdec_rows, kdec_cols, cdec_rows)


def _attn_prompt_kernel(x_ref, h_ref, k_ref, v_ref, wout_ref, wcq_ref, gx_ref, qs_ref, ck_ref, cv_ref,
                        x1_ref, o_ref, os_ref):
    sample_stages = _sample_attention(qs_ref, ck_ref, cv_ref, os_ref)
    tm = x_ref.shape[0]
    rows = [slice(r, r + tm // ROW_GROUPS) for r in range(0, tm, tm // ROW_GROUPS)]
    x1 = [x_ref[r, :] + jnp.dot(h_ref[r, :], wout_ref[...], preferred_element_type=F32)
          for r in rows]
    next(sample_stages)
    for r, a in zip(rows, x1):
        x1_ref[r, :] = a
    xq = [_rms(a, gx_ref[...]).astype(BF16) for a in x1]
    q = [jnp.dot(a, wcq_ref[...], preferred_element_type=F32).astype(BF16) for a in xq]
    next(sample_stages)
    sl = [slice(h * X_HEAD_DIM, (h + 1) * X_HEAD_DIM) for h in range(X_HEADS)]
    items = [(g, h) for g in range(ROW_GROUPS) for h in range(X_HEADS)]
    s = [_dot_nt(q[g][:, sl[h]], k_ref[:, sl[h]]) * (X_HEAD_DIM ** -0.5) for g, h in items]
    next(sample_stages)
    e = [jnp.exp(a - jnp.max(a, axis=-1, keepdims=True)) for a in s]
    p = [a / jnp.sum(a, axis=-1, keepdims=True) for a in e]
    for (g, h), a in zip(items, p):
        o_ref[rows[g], sl[h]] = _dot(a, v_ref[:, sl[h]]).astype(BF16)


def _attn_prompt(x, hcat, mk, mv, w_out, w_cq, g_x, q_s, ck_s, cv_s, tm=512):
    B, T, _ = x.shape
    Bs = q_s.shape[0]
    steps = B * (T // tm)
    ba = Bs // steps
    assert ba * steps == Bs
    tile = pl.BlockSpec((None, tm, D_MODEL), lambda b, t: (b, t, 0))
    kv = pl.BlockSpec((None, N_MEM, D_MODEL), lambda b, t: (b, 0, 0))
    step = lambda b, t: (b * (T // tm) + t, 0, 0)
    s_rows = pl.BlockSpec((ba, QROWS, 128), step)
    s_kv = pl.BlockSpec((ba, KV_ROWS, 128), step)
    x1, o, o_s = pl.pallas_call(
        _attn_prompt_kernel,
        grid=(B, T // tm),
        in_specs=[tile, tile, kv, kv, _const_spec((D_MODEL, D_MODEL)),
                  _const_spec((D_MODEL, D_MODEL)), _const_spec((1, D_MODEL)), s_rows, s_kv, s_kv],
        out_specs=[tile, tile, s_rows],
        out_shape=[jax.ShapeDtypeStruct((B, T, D_MODEL), F32),
                   jax.ShapeDtypeStruct((B, T, D_MODEL), BF16),
                   jax.ShapeDtypeStruct((Bs, QROWS, 128), F32)],
        compiler_params=_params("parallel", "parallel"),
    )(x, hcat, mk, mv, w_out, w_cq, g_x, q_s.reshape(Bs, QROWS, 128), ck_s, cv_s)
    return x1, o, o_s.reshape(Bs, D_MODEL)


V7X_MXU_DIM = 256
FF_CHUNKS = ((0, 6 * V7X_MXU_DIM), (6 * V7X_MXU_DIM, D_FF))


def _ffn_kernel(x1_ref, o_ref, wco_ref, wg_ref, wu_ref, wd_ref, gffn_ref, gfin_ref, y_ref, *,
                n_groups):
    tm = x1_ref.shape[0]
    rows = [slice(r, r + tm // n_groups) for r in range(0, tm, tm // n_groups)]
    acc = [x1_ref[r, :] + _dot(o_ref[r, :], wco_ref[...]) for r in rows]
    hf = [_rms(a, gffn_ref[...]).astype(BF16) for a in acc]
    for c0, c1 in FF_CHUNKS:
        gate = [jnp.dot(h, wg_ref[:, c0:c1], preferred_element_type=F32) for h in hf]
        up = [jnp.dot(h, wu_ref[:, c0:c1], preferred_element_type=F32) for h in hf]
        act = [_silu(g) * u for g, u in zip(gate, up)]
        acc = [a + _dot(p, wd_ref[c0:c1, :]) for a, p in zip(acc, act)]
    for r, a in zip(rows, acc):
        y_ref[r, :] = _rms(a, gfin_ref[...])


def _ffn(x1, o, w_co, w_gate, w_up, w_down, g_ffn, g_final, tm):
    rows = x1.shape[0]
    row_spec = pl.BlockSpec((tm, D_MODEL), lambda i: (i, 0))
    return pl.pallas_call(
        functools.partial(_ffn_kernel, n_groups=ROW_GROUPS if tm >= 512 else 1),
        grid=(rows // tm,),
        in_specs=[row_spec, row_spec, _const_spec((D_MODEL, D_MODEL)),
                  _const_spec((D_MODEL, D_FF)), _const_spec((D_MODEL, D_FF)),
                  _const_spec((D_FF, D_MODEL)), _const_spec((1, D_MODEL)), _const_spec((1, D_MODEL))],
        out_specs=row_spec,
        out_shape=jax.ShapeDtypeStruct((rows, D_MODEL), F32),
        compiler_params=_params("parallel"),
    )(x1, o, w_co, w_gate, w_up, w_down, g_ffn, g_final)


def _inproj_sample_kernel(x_ref, g_ref, wn_ref, wt_ref, wgt_ref, un_ref, ut_ref, gs_ref):
    xn = _rms(x_ref[...], g_ref[...]).astype(BF16)
    un_ref[...] = jnp.dot(xn, wn_ref[...], preferred_element_type=F32)
    ut_ref[...] = lax.dot_general(xn, wt_ref[...], NT_DIMS, preferred_element_type=F32)
    gs_ref[...] = lax.dot_general(xn, wgt_ref[...], NT_DIMS, preferred_element_type=F32)


def _inproj_sample(x, g_mix, wn, wt, wgt):
    rows = x.shape[0]
    full = lambda cols: pl.BlockSpec((rows, cols), lambda i: (0, 0))
    return pl.pallas_call(
        _inproj_sample_kernel,
        grid=(1,),
        in_specs=[_const_spec((rows, D_MODEL)), _const_spec((1, D_MODEL)),
                  _const_spec((D_MODEL, N_COLS)), _const_spec((T_ROWS, D_MODEL)),
                  _const_spec((GT_ROWS, D_MODEL))],
        out_specs=[full(N_COLS), full(T_ROWS), full(GT_ROWS)],
        out_shape=[jax.ShapeDtypeStruct((rows, N_COLS), F32),
                   jax.ShapeDtypeStruct((rows, T_ROWS), F32),
                   jax.ShapeDtypeStruct((rows, GT_ROWS), F32)],
        compiler_params=_params("arbitrary"),
    )(x, g_mix, wn, wt, wgt)


def _mixer_sample_kernel(un_ref, ut_ref, gs_ref, conv_ref, C_ref, n_ref, m_ref, S_ref, bg_ref,
                         wconv_ref, bconv_ref, gmh_ref, grh_ref, cos_ref, sin_ref, rtab_ref,
                         h_ref, convo_ref, Co_ref, no_ref, mo_ref, So_ref,
                         q_s, k_s, vws_s, qr_s, kr_s, qc_s, qs_s, carry_s, *, bb):
    k_scale = HEAD_DIM ** -0.5
    uqk = un_ref[:, N_QK:N_RQ]
    conv = (bconv_ref[...] + wconv_ref[0:1, :] * conv_ref[0] + wconv_ref[1:2, :] * conv_ref[1]
            + wconv_ref[2:3, :] * conv_ref[2] + wconv_ref[3:4, :] * uqk)
    convo_ref[0] = conv_ref[1]
    convo_ref[1] = conv_ref[2]
    convo_ref[2] = uqk
    qk_act = _silu(conv)
    q_s[...] = qk_act[:, 0:MIX]
    k_s[...] = qk_act[:, MIX:2 * MIX] * k_scale

    gates = gs_ref[...] + bg_ref[...]
    ig = gates[:, 0:HEADS]
    lf = _log_sigmoid(gates[:, 8:8 + HEADS])
    inter = lf + m_ref[...]
    m_t = jnp.maximum(inter, ig)
    ws = jnp.exp(ig - m_t)
    w_in = jnp.exp(inter - m_t)
    mo_ref[...] = m_t
    carry_s[:, 0:HEADS] = w_in

    cos_f = cos_ref[...]
    sin_s = sin_ref[...]
    for h in range(HEADS):
        lo = h * HEAD_DIM
        hs = slice(lo, lo + HEAD_DIM)
        vws_s[:, hs] = ut_ref[:, T_V + lo:T_V + lo + HEAD_DIM] * ws[:, h:h + 1]
        qr_s[:, hs] = _rope(un_ref[:, N_RQ + lo:N_RQ + lo + HEAD_DIM], cos_f, sin_s)
        kr_s[:, hs] = _rope(un_ref[:, N_RK + lo:N_RK + lo + HEAD_DIM], cos_f, sin_s) * k_scale

    heads = range(HEADS)
    seqs = range(bb)
    hsl = [slice(h * HEAD_DIM, (h + 1) * HEAD_DIM) for h in heads]
    seq_id = lax.broadcasted_iota(jnp.int32, (bb, HEAD_DIM), 0)
    for h in heads:
        q_h, k_h, vws_h = q_s[:, hsl[h]], k_s[:, hsl[h]], vws_s[:, hsl[h]]
        qr_h = qr_s[:, hsl[h]]
        krd_h = kr_s[:, hsl[h]] * rtab_ref[1:2, h:h + 1]
        rv_h = ut_ref[:, T_RV + h * HEAD_DIM:T_RV + (h + 1) * HEAD_DIM]
        c_old = [C_ref[b, h] for b in seqs]
        s_old = [S_ref[b, h] for b in seqs]
        q_c = [_dot_nt(q_h, c_old[b]) for b in seqs]
        q_st = [_dot(qr_h, s_old[b]) for b in seqs]
        d_c = [_dot_tn(jnp.where(seq_id == b, vws_h, 0.0), k_h) for b in seqs]
        d_s = [_dot_tn(jnp.where(seq_id == b, krd_h, 0.0), rv_h) for b in seqs]
        for b in seqs:
            row = slice(b, b + 1)
            qc_s[row, hsl[h]] = q_c[b][row]
            qs_s[row, hsl[h]] = q_st[b][row]
            Co_ref[b, h] = carry_s[row, h:h + 1] * c_old[b] + d_c[b]
            So_ref[b, h] = rtab_ref[2:3, h:h + 1] * s_old[b] + d_s[b]

    for h in range(HEADS):
        lo = h * HEAD_DIM
        hs = slice(lo, lo + HEAD_DIM)
        q_h, k_h, n_h = q_s[:, hs], k_s[:, hs], n_ref[:, hs]
        ws_h, w_in_h = ws[:, h:h + 1], w_in[:, h:h + 1]
        wts = ws_h * jnp.sum(q_h * k_h, axis=1, keepdims=True)
        num = wts * ut_ref[:, T_V + lo:T_V + lo + HEAD_DIM] + w_in_h * qc_s[:, hs]
        den = wts + w_in_h * jnp.sum(n_h * q_h, axis=1, keepdims=True)
        hm = num / jnp.maximum(jnp.abs(den), jnp.exp(-m_t[:, h:h + 1]))
        mo = ut_ref[:, T_O + lo:T_O + lo + HEAD_DIM]
        h_ref[:, hs] = _head_norm(hm) * gmh_ref[:, hs] * jax.nn.sigmoid(mo)
        no_ref[:, hs] = w_in_h * n_h + ws_h * k_h

        att = jnp.sum(qr_s[:, hs] * kr_s[:, hs], axis=1, keepdims=True) * rtab_ref[3:4, h:h + 1]
        o = (att * ut_ref[:, T_RV + lo:T_RV + lo + HEAD_DIM]
             + rtab_ref[0:1, h:h + 1] * qs_s[:, hs])
        rg = ut_ref[:, T_RG + lo:T_RG + lo + HEAD_DIM]
        h_ref[:, MIX + lo:MIX + lo + HEAD_DIM] = _head_norm(o) * grh_ref[:, hs] * _silu(rg)


def _mixer_sample(un, ut, gs, conv_t, C, n2, m, S, bg, w_conv, b_conv, g_mh, g_rh, cos_f, sin_s, rtab,
                  bb=8):
    B = un.shape[0]
    rows = lambda cols: pl.BlockSpec((bb, cols), lambda i: (i, 0))
    mats = pl.BlockSpec((bb, HEADS, HEAD_DIM, HEAD_DIM), lambda i: (i, 0, 0, 0))
    convs = pl.BlockSpec((CONV_W - 1, bb, 2 * MIX), lambda i: (0, i, 0))
    in_specs = [rows(N_COLS), rows(T_ROWS), rows(GT_ROWS), convs, mats, rows(MIX), rows(HEADS), mats,
                _const_spec((1, GT_ROWS)), _const_spec((CONV_W, 2 * MIX)), _const_spec((1, 2 * MIX)),
                _const_spec((1, MIX)), _const_spec((1, MIX)),
                _const_spec((1, HEAD_DIM)), _const_spec((1, HEAD_DIM)), _const_spec((8, HEAD_DIM))]
    out_specs = [rows(D_MODEL), convs, mats, rows(MIX), rows(HEADS), mats]
    out_shape = [jax.ShapeDtypeStruct((B, D_MODEL), F32),
                 jax.ShapeDtypeStruct((CONV_W - 1, B, 2 * MIX), F32),
                 jax.ShapeDtypeStruct((B, HEADS, HEAD_DIM, HEAD_DIM), F32),
                 jax.ShapeDtypeStruct((B, MIX), F32),
                 jax.ShapeDtypeStruct((B, HEADS), F32),
                 jax.ShapeDtypeStruct((B, HEADS, HEAD_DIM, HEAD_DIM), F32)]
    scratch = [pltpu.VMEM((bb, MIX), F32) for _ in range(7)] + [pltpu.VMEM((bb, HEAD_DIM), F32)]
    return pl.pallas_call(
        functools.partial(_mixer_sample_kernel, bb=bb),
        grid=(B // bb,),
        in_specs=in_specs, out_specs=out_specs, out_shape=out_shape,
        scratch_shapes=scratch,
        compiler_params=_params("parallel"),
    )(un, ut, gs, conv_t, C, n2, m, S, bg, w_conv, b_conv, g_mh, g_rh, cos_f, sin_s, rtab)


def _outq_sample_kernel(x_ref, h_ref, wout_ref, wcq_ref, gx_ref, x1_ref, q_ref):
    x1 = x_ref[...] + _dot(h_ref[...], wout_ref[...])
    x1_ref[...] = x1
    q_ref[...] = _dot(_rms(x1, gx_ref[...]), wcq_ref[...])


def _outq_sample(x, hcat, w_out, w_cq, g_x):
    rows = x.shape[0]
    full = _const_spec((rows, D_MODEL))
    return pl.pallas_call(
        _outq_sample_kernel,
        grid=(1,),
        in_specs=[full, full, _const_spec((D_MODEL, D_MODEL)), _const_spec((D_MODEL, D_MODEL)),
                  _const_spec((1, D_MODEL))],
        out_specs=[pl.BlockSpec((rows, D_MODEL), lambda i: (0, 0))] * 2,
        out_shape=[jax.ShapeDtypeStruct((rows, D_MODEL), F32)] * 2,
        compiler_params=_params("arbitrary"),
    )(x, hcat, w_out, w_cq, g_x)


def _sample_attention(q_ref, k_ref, v_ref, o_ref):
    r_id = lax.broadcasted_iota(jnp.int32, (QROWS, KV_ROWS), 0)
    n_id = lax.broadcasted_iota(jnp.int32, (QROWS, KV_ROWS), 1)
    own = (n_id & 7) == (r_id >> 1) + 4 * (r_id & 1)
    low_half = (lax.broadcasted_iota(jnp.int32, (1, KV_ROWS), 1) & 4) == 0
    seqs = range(q_ref.shape[0])
    z = [_dot_nt(q_ref[j], k_ref[j]) for j in seqs]
    zc = [jnp.sum(jnp.where(own, z[j], 0.0), axis=0, keepdims=True) for j in seqs]
    yield
    other = [jnp.where(low_half, pltpu.roll(zc[j], KV_ROWS - 4, 1), pltpu.roll(zc[j], 4, 1))
             for j in seqs]
    s = [jnp.where(own, (zc[j] + other[j]) * (X_HEAD_DIM ** -0.5), -jnp.inf) for j in seqs]
    m = [jnp.max(s[j], axis=-1, keepdims=True) for j in seqs]
    e = [jnp.exp(s[j] - m[j]) for j in seqs]
    den = [jnp.sum(e[j], axis=-1, keepdims=True) for j in seqs]
    yield
    for j in seqs:
        o_ref[j] = _dot(e[j] / den[j], v_ref[j])
    yield


def _kv_rows(cache):
    B = cache.shape[0]
    c5 = cache.reshape(B, N_MEM, X_HEADS, 2, 128)
    return jnp.transpose(c5, (0, 1, 3, 2, 4)).reshape(B, KV_ROWS, 128)


def _kv_from_rows(rows):
    B = rows.shape[0]
    r5 = rows.reshape(B, N_MEM, 2, X_HEADS, 128)
    return jnp.transpose(r5, (0, 1, 3, 2, 4)).reshape(1, B, N_MEM, X_HEADS, X_HEAD_DIM)


def _rope_tables(pos):
    half = HEAD_DIM // 2
    inv = ROPE_THETA ** (-np.arange(half, dtype=np.float64) / half)
    ang = np.asarray(pos, np.float64)[:, None] * inv[None, :]
    cos, sin = np.cos(ang), np.sin(ang)
    return (np.concatenate([cos, cos], axis=-1).astype(np.float32),
            np.concatenate([-sin, sin], axis=-1).astype(np.float32))


def _retention_tables(L):
    lg = np.log1p(-np.exp2(-5.0 - np.arange(HEADS, dtype=np.float64)))
    t = np.arange(L, dtype=np.float64)
    diff = t[:, None] - t[None, :]
    decay = np.where(diff >= 0, np.exp(lg[:, None, None] * np.maximum(diff, 0.0)), 0.0)
    q_dec = np.exp(lg[:, None] * (t + 1.0))
    k_dec = np.exp(lg[:, None] * (L - 1.0 - t))
    chunk_dec = np.exp(lg * L)
    return tuple(a.astype(np.float32) for a in (decay, q_dec, k_dec, chunk_dec))


def _lanes(a, n):
    xp = np if isinstance(a, np.ndarray) else jnp
    return xp.broadcast_to(a[..., None], a.shape + (n,))


def _pad_rows(a, rows):
    return np.pad(a, ((0, rows - a.shape[0]),) + ((0, 0),) * (a.ndim - 1))


def kernel(x_prompt, x_sample, cache_mem_k, cache_mem_v, state_mlstm_conv, state_mlstm_C, state_mlstm_n, state_mlstm_m, state_ret_S, mem_prompt, w_in, b_gate, w_conv, b_conv, g_mix, g_mhead, g_rhead, w_out, g_xattn, g_mem, w_ck, w_cv, w_cq, w_co, g_ffn, w_gate, w_up, w_down, g_final):
    Bp, Tp, _ = x_prompt.shape
    Bs = x_sample.shape[0]
    l = 0
    n_m = 4 * MIX
    wi = w_in[l]
    w_gates = wi[:, n_m:n_m + 2 * HEADS]
    w_ret = wi[:, n_m + 2 * HEADS:]
    row = lambda a: a.reshape(1, -1)
    bf = lambda a: a.astype(BF16)
    g_mix_r, g_mh_r, g_rh_r = row(g_mix[l]), row(g_mhead[l]), row(g_rhead[l])
    g_x_r, g_mem_r, g_ffn_r, g_fin_r = row(g_xattn[l]), row(g_mem[l]), row(g_ffn[l]), row(g_final)
    b_conv_r = row(b_conv[l])
    w_out_b, w_cq_b, w_co_b = bf(w_out[l]), bf(w_cq[l]), bf(w_co[l])
    w_gate_b, w_up_b, w_down_b = bf(w_gate[l]), bf(w_up[l]), bf(w_down[l])

    mk_b, mv_b, mk_rows, mv_rows = _memkv(mem_prompt, g_mem_r, bf(w_ck[l]), bf(w_cv[l]))
    wn = bf(jnp.concatenate([wi[:, :2 * MIX], w_ret[:, :2 * MIX]], axis=1))
    wt = bf(jnp.concatenate([wi[:, 2 * MIX:n_m], w_ret[:, 2 * MIX:]], axis=1).T)
    gate_rows = lambda a: jnp.concatenate(
        [a[:HEADS], jnp.zeros((8 - HEADS,) + a.shape[1:], F32),
         a[HEADS:], jnp.zeros((8 - HEADS,) + a.shape[1:], F32)], axis=0)
    wgt = bf(gate_rows(w_gates.T))
    bg_rows = gate_rows(b_gate[l][:, None])
    bgt = jnp.broadcast_to(bg_rows, (GT_ROWS, CHUNK))
    cos_p, sin_p = _rope_tables(np.arange(Tp))
    decay, q_dec, k_dec, chunk_dec = _retention_tables(CHUNK)
    hcat_p, conv_p, C_p, n_p, m_p, S_p = _mixer_prompt(
        x_prompt, wn, wt, wgt, g_mix_r, bgt, w_conv[l], b_conv_r,
        _lanes(g_mhead[l], CHUNK), _lanes(g_rhead[l], CHUNK), cos_p, sin_p,
        np.swapaxes(decay, 1, 2), _pad_rows(q_dec, 8), _lanes(k_dec, HEAD_DIM),
        _pad_rows(_lanes(chunk_dec, HEAD_DIM), 8))

    xs = x_sample.reshape(Bs, D_MODEL)
    un_s, ut_s, gs_s = _inproj_sample(xs, g_mix_r, wn, wt, wgt)
    cos_s, sin_s = _rope_tables(PAST_LEN + np.arange(1))
    decay1, q_dec1, k_dec1, chunk_dec1 = _retention_tables(1)
    rtab = np.zeros((8, HEAD_DIM), np.float32)
    rtab[:4, :HEADS] = np.stack([q_dec1[:, 0], k_dec1[:, 0], chunk_dec1, decay1[:, 0, 0]])
    hcat_s, conv_s, C_s, n_s, m_s, S_s = _mixer_sample(
        un_s, ut_s, gs_s, jnp.transpose(state_mlstm_conv[l], (1, 0, 2)), state_mlstm_C[l],
        state_mlstm_n[l].reshape(Bs, MIX), state_mlstm_m[l], state_ret_S[l],
        bg_rows.reshape(1, GT_ROWS), w_conv[l], b_conv_r, g_mh_r, g_rh_r, cos_s, sin_s, rtab)
    x1_s, q_s = _outq_sample(xs, hcat_s, w_out_b, w_cq_b, g_x_r)

    x1_p, o_p, o_s = _attn_prompt(x_prompt, hcat_p, mk_b, mv_b, w_out_b, w_cq_b, g_x_r,
                                  q_s, _kv_rows(cache_mem_k[l]), _kv_rows(cache_mem_v[l]))
    y_p = _ffn(x1_p.reshape(Bp * Tp, D_MODEL), o_p.reshape(Bp * Tp, D_MODEL),
               w_co_b, w_gate_b, w_up_b, w_down_b, g_ffn_r, g_fin_r, tm=512)
    y_s = _ffn(x1_s, o_s, w_co_b, w_gate_b, w_up_b, w_down_b, g_ffn_r, g_fin_r, tm=Bs)

    return (y_p.reshape(Bp, Tp, D_MODEL), y_s.reshape(Bs, 1, D_MODEL),
            _kv_from_rows(mk_rows), _kv_from_rows(mv_rows),
            conv_p[None], C_p[None], n_p[None], m_p[None, :, :HEADS, 0], S_p[None],
            jnp.transpose(conv_s, (1, 0, 2))[None], C_s[None],
            n_s.reshape(1, Bs, HEADS, HEAD_DIM), m_s[None], S_s[None])
```

```python
import functools

import jax
import jax.numpy as jnp
import numpy as np
from jax import lax
from jax.experimental import pallas as pl
from jax.experimental.pallas import tpu as pltpu

F32 = jnp.float32
BF16 = jnp.bfloat16

D_MODEL = 1024
HEADS = 4
HEAD_DIM = 128
MIX = HEADS * HEAD_DIM
CONV_W = 4
CHUNK = 128
N_MEM = 256
X_HEADS = 4
X_HEAD_DIM = 256
D_FF = 2816
ROPE_THETA = 10000.0
EPS = 1e-6
PAST_LEN = 16384
KV_ROWS = N_MEM * 2 * X_HEADS
QROWS = 2 * X_HEADS

N_QK, N_RQ, N_RK, N_COLS = 0, 1024, 1536, 2048
T_V, T_O, T_RV, T_RG, T_ROWS = 0, 512, 1024, 1536, 2048
GT_ROWS = 16
UOFF = 8
ROW_GROUPS = 2

V7X_VMEM_LIMIT = 56 * 1024 * 1024

NT_DIMS = (((1,), (1,)), ((), ()))
TN_DIMS = (((0,), (0,)), ((), ()))


def _dot(a, b):
    return jnp.dot(a.astype(BF16), b.astype(BF16), preferred_element_type=F32)


def _dot_nt(a, b):
    return lax.dot_general(a.astype(BF16), b.astype(BF16), NT_DIMS, preferred_element_type=F32)


def _dot_tn(a, b):
    return lax.dot_general(a.astype(BF16), b.astype(BF16), TN_DIMS, preferred_element_type=F32)


def _rms(x, g):
    return x * lax.rsqrt(jnp.mean(x * x, axis=-1, keepdims=True) + EPS) * g


def _head_norm(h):
    return h * lax.rsqrt(jnp.mean(h * h, axis=-1, keepdims=True) + EPS)


def _silu(x):
    return x * jax.nn.sigmoid(x)


def _log_sigmoid(x):
    return jnp.minimum(x, 0.0) - jnp.log1p(jnp.exp(-jnp.abs(x)))


def _rope(x, cos_full, sin_signed):
    return x * cos_full + pltpu.roll(x, HEAD_DIM // 2, 1) * sin_signed


def _const_spec(shape):
    zeros = (0,) * len(shape)
    return pl.BlockSpec(shape, lambda *_: zeros, pipeline_mode=pl.Buffered(1))


def _params(*sem):
    return pltpu.CompilerParams(dimension_semantics=sem, vmem_limit_bytes=V7X_VMEM_LIMIT)


def _memkv_kernel(mem_ref, g_ref, wk_ref, wv_ref, kb_ref, vb_ref, krows_ref, vrows_ref):
    mn = _rms(mem_ref[...], g_ref[...]).astype(BF16)
    for w_ref, b_ref, rows_ref in ((wk_ref, kb_ref, krows_ref), (wv_ref, vb_ref, vrows_ref)):
        proj = jnp.dot(mn, w_ref[...], preferred_element_type=F32)
        b_ref[...] = proj.astype(BF16)
        for h in range(X_HEADS):
            for c in range(2):
                lane0 = h * X_HEAD_DIM + c * 128
                rows_ref[pl.ds(c * X_HEADS + h, N_MEM, stride=QROWS), :] = proj[:, lane0:lane0 + 128]


def _memkv(mem, g_mem, wk, wv):
    B = mem.shape[0]
    tok = pl.BlockSpec((None, N_MEM, D_MODEL), lambda b: (b, 0, 0))
    rows = pl.BlockSpec((None, KV_ROWS, 128), lambda b: (b, 0, 0))
    return pl.pallas_call(
        _memkv_kernel,
        grid=(B,),
        in_specs=[tok, _const_spec((1, D_MODEL)),
                  _const_spec((D_MODEL, D_MODEL)), _const_spec((D_MODEL, D_MODEL))],
        out_specs=[tok, tok, rows, rows],
        out_shape=[jax.ShapeDtypeStruct((B, N_MEM, D_MODEL), BF16)] * 2
        + [jax.ShapeDtypeStruct((B, KV_ROWS, 128), F32)] * 2,
        compiler_params=_params("parallel"),
    )(mem, g_mem, wk, wv)


def _mixer_prompt_kernel(x_ref, wn_ref, wt_ref, wgt_ref, gmix_ref, bgt_ref, wconv_ref, bconv_ref,
                         gmh_ref, grh_ref, cos_ref, sin_ref, decay_ref, qdec_ref, kdec_ref, cdec_ref,
                         h_ref, conv_ref, C_ref, n_ref, m_ref, S_ref,
                         un_a, ut_a, gt_a, un_b, ut_b, gt_b, tail_ref, st_ref, *, tt, tiles_per_seq):
    L = CHUNK
    s = pl.program_id(0)
    t = (jnp.maximum(s, 1) - 1) % tiles_per_seq
    chunks = range(tt // L)
    slot_a, slot_b = (un_a, ut_a, gt_a), (un_b, ut_b, gt_b)

    @pl.when(s == 0)
    def _():
        for ref in slot_b + (tail_ref,):
            ref[...] = jnp.zeros_like(ref)

    @pl.when(t == 0)
    def _():
        C_ref[...] = jnp.zeros_like(C_ref)
        st_ref[...] = jnp.zeros_like(st_ref)
        n_ref[...] = jnp.zeros_like(n_ref)
        m_ref[...] = jnp.zeros_like(m_ref)

    k_scale = HEAD_DIM ** -0.5

    def project(slot, xn, part):
        un_ref, ut_ref, gt_ref = slot
        c0 = part * MIX
        un_ref[UOFF:UOFF + tt, c0:c0 + MIX] = jnp.dot(xn, wn_ref[:, c0:c0 + MIX],
                                                      preferred_element_type=F32)
        res = lax.dot_general(wt_ref[c0:c0 + MIX, :], xn, NT_DIMS, preferred_element_type=F32)
        for c in chunks:
            ut_ref[c, c0:c0 + MIX, :] = res[:, c * L:(c + 1) * L]
        if part == 0:
            gates_t = lax.dot_general(wgt_ref[...], xn, NT_DIMS, preferred_element_type=F32)
            for c in chunks:
                gt_ref[c] = gates_t[:, c * L:(c + 1) * L]

    src_id = lax.broadcasted_iota(jnp.int32, (L, L), 0)
    tgt_id = lax.broadcasted_iota(jnp.int32, (L, L), 1)
    causal = src_id <= tgt_id
    triu_bf = jnp.where(causal, 1.0, 0.0).astype(BF16)
    heads = range(HEADS)
    hcol = lambda base, h: slice(base + h * HEAD_DIM, base + (h + 1) * HEAD_DIM)

    def chunk_body(c, slot):
        un_ref, ut_ref, gt_ref = slot
        r0 = c * L
        rows = pl.ds(UOFF + r0, L)
        trows = pl.ds(r0, L)
        g_t = gt_ref[c] + bgt_ref[...]
        ig = g_t[0:8]
        lf = _log_sigmoid(g_t[8:16])
        lf_hi = lf.astype(BF16)
        r1 = lf - lf_hi.astype(F32)
        lf_mid = r1.astype(BF16)
        lf_lo = (r1 - lf_mid.astype(F32)).astype(BF16)
        bc = (jnp.dot(lf_hi, triu_bf, preferred_element_type=F32)
              + jnp.dot(lf_mid, triu_bf, preferred_element_type=F32)
              + jnp.dot(lf_lo, triu_bf, preferred_element_type=F32))
        m_prev = m_ref[...]
        inter = bc + m_prev
        b_last = bc[:, L - 1:L]
        g_w = b_last - bc + ig
        m_new = jnp.maximum(b_last + m_prev, jnp.max(g_w, axis=1, keepdims=True))
        ws = jnp.exp(g_w - m_new)
        carry = jnp.exp(b_last + m_prev - m_new)
        a_n = jnp.concatenate([ig - bc, jnp.zeros((L - 8, L), F32)], axis=0).T

        def conv_act(col):
            win = un_ref[pl.ds(r0, L + UOFF), col:col + HEAD_DIM]
            acc = bconv_ref[:, col:col + HEAD_DIM]
            for j in range(CONV_W):
                back = CONV_W - 1 - j
                tap = win if back == 0 else pltpu.roll(win, back, 0)
                acc = acc + tap[UOFF:UOFF + L] * wconv_ref[j:j + 1, col:col + HEAD_DIM]
            return _silu(acc)

        cos_f, sin_s = cos_ref[trows, :], sin_ref[trows, :]
        q = [conv_act(N_QK + h * HEAD_DIM) for h in heads]
        kb = [(conv_act(N_QK + MIX + h * HEAD_DIM) * k_scale).astype(BF16) for h in heads]
        qb = [a.astype(BF16) for a in q]
        rqb = [_rope(un_ref[rows, hcol(N_RQ, h)], cos_f, sin_s).astype(BF16) for h in heads]
        rk = [_rope(un_ref[rows, hcol(N_RK, h)], cos_f, sin_s) * k_scale for h in heads]
        v_t = [ut_ref[c, hcol(T_V, h), :] for h in heads]
        rvb = [ut_ref[c, hcol(T_RV, h), :].astype(BF16) for h in heads]
        c_old = [C_ref[h] for h in heads]
        s_old = [st_ref[h] for h in heads]
        n_old = [n_ref[h:h + 1, :] for h in heads]
        qk = [_dot_nt(kb[h], qb[h]) for h in heads]
        att = [_dot_nt(rk[h], rqb[h]) for h in heads]
        c_q = [_dot_nt(c_old[h], qb[h]) for h in heads]
        s_q = [_dot_nt(s_old[h], rqb[h]) for h in heads]
        n_q = [_dot_nt(jnp.broadcast_to(n_old[h], (8, HEAD_DIM)), qb[h])[0:1] for h in heads]
        d_c = [_dot(v_t[h] * ws[h:h + 1, :], kb[h]) for h in heads]
        d_n = [_dot(jnp.broadcast_to(ws[h:h + 1, :], (8, L)), kb[h])[0:1] for h in heads]
        d_s = [_dot(rvb[h], rk[h] * kdec_ref[h]) for h in heads]
        m_t, w_in, wts = [], [], []
        for h in heads:
            dmat = a_n[:, h:h + 1] + bc[h:h + 1, :]
            dmat = jnp.where(causal, dmat, -jnp.inf)
            m_t.append(jnp.maximum(inter[h:h + 1, :], jnp.max(dmat, axis=0, keepdims=True)))
            wts.append(jnp.exp(dmat - m_t[h]) * qk[h])
            w_in.append(jnp.exp(inter[h:h + 1, :] - m_t[h]))
        att_w = [att[h] * decay_ref[h] for h in heads]
        v_p = [_dot(v_t[h], wts[h]) for h in heads]
        v_a = [_dot(rvb[h], att_w[h]) for h in heads]
        for h in heads:
            num = v_p[h] + w_in[h] * c_q[h]
            den = jnp.sum(wts[h], axis=0, keepdims=True) + w_in[h] * n_q[h]
            hm = num / jnp.maximum(jnp.abs(den), jnp.exp(-m_t[h]))
            hm = hm * lax.rsqrt(jnp.mean(hm * hm, axis=0, keepdims=True) + EPS)
            hm = hm * gmh_ref[hcol(0, h), :] * jax.nn.sigmoid(ut_ref[c, hcol(T_O, h), :])
            h_ref[trows, hcol(0, h)] = hm.T.astype(BF16)
            carry_h = carry[h:h + 1, :]
            C_ref[h] = carry_h * c_old[h] + d_c[h]
            n_ref[h:h + 1, :] = carry_h * n_old[h] + d_n[h]
            o = v_a[h] + qdec_ref[h:h + 1, :] * s_q[h]
            st_ref[h] = cdec_ref[h:h + 1, :] * s_old[h] + d_s[h]
            hr = o * lax.rsqrt(jnp.mean(o * o, axis=0, keepdims=True) + EPS)
            hr = hr * grh_ref[hcol(0, h), :] * _silu(ut_ref[c, hcol(T_RG, h), :])
            h_ref[trows, hcol(MIX, h)] = hr.T.astype(BF16)

        m_ref[...] = m_new

    def step(proj_slot, scan_slot):
        un_ref = scan_slot[0]
        un_ref[0:UOFF, N_QK:N_RQ] = jnp.where(t == 0, 0.0, tail_ref[...])
        xn = _rms(x_ref[...], gmix_ref[...]).astype(BF16)
        for c in chunks:
            project(proj_slot, xn, c)
            chunk_body(c, scan_slot)
        tail_ref[...] = un_ref[tt:tt + UOFF, N_QK:N_RQ]

    @pl.when(s % 2 == 0)
    def _():
        step(slot_a, slot_b)

    @pl.when(s % 2 == 1)
    def _():
        step(slot_b, slot_a)

    @pl.when(t == tiles_per_seq - 1)
    def _():
        conv_ref[...] = tail_ref[UOFF - (CONV_W - 1):UOFF, :]
        for h in heads:
            S_ref[h] = st_ref[h].T


def _mixer_prompt(x, wn, wt, wgt, g_mix, bgt, w_conv, b_conv, gmh_cols, grh_cols, cos_f, sin_s,
                  decay_t, qdec_rows, kdec_cols, cdec_rows, tt=512):
    B, T, _ = x.shape
    tps = T // tt
    n_tiles = B * tps
    proj = lambda s: jnp.minimum(s, n_tiles - 1)
    scan = lambda s: jnp.maximum(s, 1) - 1
    per_b3 = lambda s: (scan(s) // tps, 0, 0)
    per_b4 = lambda s: (scan(s) // tps, 0, 0, 0)
    in_specs = [
        pl.BlockSpec((None, tt, D_MODEL), lambda s: (proj(s) // tps, proj(s) % tps, 0)),
        _const_spec((D_MODEL, N_COLS)), _const_spec((T_ROWS, D_MODEL)), _const_spec((GT_ROWS, D_MODEL)),
        _const_spec((1, D_MODEL)), _const_spec((GT_ROWS, CHUNK)),
        _const_spec((CONV_W, 2 * MIX)), _const_spec((1, 2 * MIX)),
        _const_spec((MIX, CHUNK)), _const_spec((MIX, CHUNK)),
        pl.BlockSpec((tt, HEAD_DIM), lambda s: (scan(s) % tps, 0)),
        pl.BlockSpec((tt, HEAD_DIM), lambda s: (scan(s) % tps, 0)),
        _const_spec((HEADS, CHUNK, CHUNK)),
        _const_spec((8, CHUNK)), _const_spec((HEADS, CHUNK, HEAD_DIM)), _const_spec((8, HEAD_DIM)),
    ]
    out_specs = [
        pl.BlockSpec((None, tt, D_MODEL), lambda s: (scan(s) // tps, scan(s) % tps, 0)),
        pl.BlockSpec((None, CONV_W - 1, 2 * MIX), per_b3),
        pl.BlockSpec((None, HEADS, HEAD_DIM, HEAD_DIM), per_b4),
        pl.BlockSpec((None, HEADS, HEAD_DIM), per_b3),
        pl.BlockSpec((None, 8, CHUNK), per_b3),
        pl.BlockSpec((None, HEADS, HEAD_DIM, HEAD_DIM), per_b4),
    ]
    out_shape = [
        jax.ShapeDtypeStruct((B, T, D_MODEL), BF16),
        jax.ShapeDtypeStruct((B, CONV_W - 1, 2 * MIX), F32),
        jax.ShapeDtypeStruct((B, HEADS, HEAD_DIM, HEAD_DIM), F32),
        jax.ShapeDtypeStruct((B, HEADS, HEAD_DIM), F32),
        jax.ShapeDtypeStruct((B, 8, CHUNK), F32),
        jax.ShapeDtypeStruct((B, HEADS, HEAD_DIM, HEAD_DIM), F32),
    ]
    n_chunks = tt // CHUNK
    slot = [
        pltpu.VMEM((tt + UOFF, N_COLS), F32),
        pltpu.VMEM((n_chunks, T_ROWS, CHUNK), F32),
        pltpu.VMEM((n_chunks, GT_ROWS, CHUNK), F32),
    ]
    scratch = slot + slot + [
        pltpu.VMEM((UOFF, 2 * MIX), F32),
        pltpu.VMEM((HEADS, HEAD_DIM, HEAD_DIM), F32),
    ]
    return pl.pallas_call(
        functools.partial(_mixer_prompt_kernel, tt=tt, tiles_per_seq=tps),
        grid=(n_tiles + 1,),
        in_specs=in_specs, out_specs=out_specs, out_shape=out_shape,
        scratch_shapes=scratch,
        compiler_params=_params("arbitrary"),
    )(x, wn, wt, wgt, g_mix, bgt, w_conv, b_conv, gmh_cols, grh_cols, cos_f, sin_s,
      decay_t, qdec_rows, kdec_cols, cdec_rows)


def _attn_prompt_kernel(x_ref, h_ref, k_ref, v_ref, wout_ref, wcq_ref, gx_ref, qs_ref, ck_ref, cv_ref,
                        x1_ref, o_ref, os_ref):
    sample_stages = _sample_attention(qs_ref, ck_ref, cv_ref, os_ref)
    tm = x_ref.shape[0]
    rows = [slice(r, r + tm // ROW_GROUPS) for r in range(0, tm, tm // ROW_GROUPS)]
    x1 = [x_ref[r, :] + jnp.dot(h_ref[r, :], wout_ref[...], preferred_element_type=F32)
          for r in rows]
    next(sample_stages)
    for r, a in zip(rows, x1):
        x1_ref[r, :] = a
    xq = [_rms(a, gx_ref[...]).astype(BF16) for a in x1]
    q = [jnp.dot(a, wcq_ref[...], preferred_element_type=F32).astype(BF16) for a in xq]
    next(sample_stages)
    sl = [slice(h * X_HEAD_DIM, (h + 1) * X_HEAD_DIM) for h in range(X_HEADS)]
    items = [(g, h) for g in range(ROW_GROUPS) for h in range(X_HEADS)]
    s = [_dot_nt(q[g][:, sl[h]], k_ref[:, sl[h]]) * (X_HEAD_DIM ** -0.5) for g, h in items]
    next(sample_stages)
    e = [jnp.exp(a - jnp.max(a, axis=-1, keepdims=True)) for a in s]
    p = [a / jnp.sum(a, axis=-1, keepdims=True) for a in e]
    for (g, h), a in zip(items, p):
        o_ref[rows[g], sl[h]] = _dot(a, v_ref[:, sl[h]]).astype(BF16)


def _attn_prompt(x, hcat, mk, mv, w_out, w_cq, g_x, q_s, ck_s, cv_s, tm=512):
    B, T, _ = x.shape
    Bs = q_s.shape[0]
    steps = B * (T // tm)
    ba = Bs // steps
    assert ba * steps == Bs
    tile = pl.BlockSpec((None, tm, D_MODEL), lambda b, t: (b, t, 0))
    kv = pl.BlockSpec((None, N_MEM, D_MODEL), lambda b, t: (b, 0, 0))
    step = lambda b, t: (b * (T // tm) + t, 0, 0)
    s_rows = pl.BlockSpec((ba, QROWS, 128), step)
    s_kv = pl.BlockSpec((ba, KV_ROWS, 128), step)
    x1, o, o_s = pl.pallas_call(
        _attn_prompt_kernel,
        grid=(B, T // tm),
        in_specs=[tile, tile, kv, kv, _const_spec((D_MODEL, D_MODEL)),
                  _const_spec((D_MODEL, D_MODEL)), _const_spec((1, D_MODEL)), s_rows, s_kv, s_kv],
        out_specs=[tile, tile, s_rows],
        out_shape=[jax.ShapeDtypeStruct((B, T, D_MODEL), F32),
                   jax.ShapeDtypeStruct((B, T, D_MODEL), BF16),
                   jax.ShapeDtypeStruct((Bs, QROWS, 128), F32)],
        compiler_params=_params("parallel", "parallel"),
    )(x, hcat, mk, mv, w_out, w_cq, g_x, q_s.reshape(Bs, QROWS, 128), ck_s, cv_s)
    return x1, o, o_s.reshape(Bs, D_MODEL)


V7X_MXU_DIM = 256
FF_CHUNKS = ((0, 6 * V7X_MXU_DIM), (6 * V7X_MXU_DIM, D_FF))


def _ffn_kernel(x1_ref, o_ref, wco_ref, wg_ref, wu_ref, wd_ref, gffn_ref, gfin_ref, y_ref, *,
                n_groups):
    tm = x1_ref.shape[0]
    rows = [slice(r, r + tm // n_groups) for r in range(0, tm, tm // n_groups)]
    acc = [x1_ref[r, :] + _dot(o_ref[r, :], wco_ref[...]) for r in rows]
    hf = [_rms(a, gffn_ref[...]).astype(BF16) for a in acc]
    for c0, c1 in FF_CHUNKS:
        gate = [jnp.dot(h, wg_ref[:, c0:c1], preferred_element_type=F32) for h in hf]
        up = [jnp.dot(h, wu_ref[:, c0:c1], preferred_element_type=F32) for h in hf]
        act = [_silu(g) * u for g, u in zip(gate, up)]
        acc = [a + _dot(p, wd_ref[c0:c1, :]) for a, p in zip(acc, act)]
    for r, a in zip(rows, acc):
        y_ref[r, :] = _rms(a, gfin_ref[...])


def _ffn(x1, o, w_co, w_gate, w_up, w_down, g_ffn, g_final, tm):
    rows = x1.shape[0]
    row_spec = pl.BlockSpec((tm, D_MODEL), lambda i: (i, 0))
    return pl.pallas_call(
        functools.partial(_ffn_kernel, n_groups=ROW_GROUPS if tm >= 512 else 1),
        grid=(rows // tm,),
        in_specs=[row_spec, row_spec, _const_spec((D_MODEL, D_MODEL)),
                  _const_spec((D_MODEL, D_FF)), _const_spec((D_MODEL, D_FF)),
                  _const_spec((D_FF, D_MODEL)), _const_spec((1, D_MODEL)), _const_spec((1, D_MODEL))],
        out_specs=row_spec,
        out_shape=jax.ShapeDtypeStruct((rows, D_MODEL), F32),
        compiler_params=_params("parallel"),
    )(x1, o, w_co, w_gate, w_up, w_down, g_ffn, g_final)


def _inproj_sample_kernel(x_ref, g_ref, wn_ref, wt_ref, wgt_ref, un_ref, ut_ref, gs_ref):
    xn = _rms(x_ref[...], g_ref[...]).astype(BF16)
    un_ref[...] = jnp.dot(xn, wn_ref[...], preferred_element_type=F32)
    ut_ref[...] = lax.dot_general(xn, wt_ref[...], NT_DIMS, preferred_element_type=F32)
    gs_ref[...] = lax.dot_general(xn, wgt_ref[...], NT_DIMS, preferred_element_type=F32)


def _inproj_sample(x, g_mix, wn, wt, wgt):
    rows = x.shape[0]
    full = lambda cols: pl.BlockSpec((rows, cols), lambda i: (0, 0))
    return pl.pallas_call(
        _inproj_sample_kernel,
        grid=(1,),
        in_specs=[_const_spec((rows, D_MODEL)), _const_spec((1, D_MODEL)),
                  _const_spec((D_MODEL, N_COLS)), _const_spec((T_ROWS, D_MODEL)),
                  _const_spec((GT_ROWS, D_MODEL))],
        out_specs=[full(N_COLS), full(T_ROWS), full(GT_ROWS)],
        out_shape=[jax.ShapeDtypeStruct((rows, N_COLS), F32),
                   jax.ShapeDtypeStruct((rows, T_ROWS), F32),
                   jax.ShapeDtypeStruct((rows, GT_ROWS), F32)],
        compiler_params=_params("arbitrary"),
    )(x, g_mix, wn, wt, wgt)


def _mixer_sample_kernel(un_ref, ut_ref, gs_ref, conv_ref, C_ref, n_ref, m_ref, S_ref, bg_ref,
                         wconv_ref, bconv_ref, gmh_ref, grh_ref, cos_ref, sin_ref, rtab_ref,
                         h_ref, convo_ref, Co_ref, no_ref, mo_ref, So_ref,
                         q_s, k_s, vws_s, qr_s, kr_s, qc_s, qs_s, carry_s, *, bb):
    k_scale = HEAD_DIM ** -0.5
    uqk = un_ref[:, N_QK:N_RQ]
    conv = (bconv_ref[...] + wconv_ref[0:1, :] * conv_ref[0] + wconv_ref[1:2, :] * conv_ref[1]
            + wconv_ref[2:3, :] * conv_ref[2] + wconv_ref[3:4, :] * uqk)
    convo_ref[0] = conv_ref[1]
    convo_ref[1] = conv_ref[2]
    convo_ref[2] = uqk
    qk_act = _silu(conv)
    q_s[...] = qk_act[:, 0:MIX]
    k_s[...] = qk_act[:, MIX:2 * MIX] * k_scale

    gates = gs_ref[...] + bg_ref[...]
    ig = gates[:, 0:HEADS]
    lf = _log_sigmoid(gates[:, 8:8 + HEADS])
    inter = lf + m_ref[...]
    m_t = jnp.maximum(inter, ig)
    ws = jnp.exp(ig - m_t)
    w_in = jnp.exp(inter - m_t)
    mo_ref[...] = m_t
    carry_s[:, 0:HEADS] = w_in

    cos_f = cos_ref[...]
    sin_s = sin_ref[...]
    for h in range(HEADS):
        lo = h * HEAD_DIM
        hs = slice(lo, lo + HEAD_DIM)
        vws_s[:, hs] = ut_ref[:, T_V + lo:T_V + lo + HEAD_DIM] * ws[:, h:h + 1]
        qr_s[:, hs] = _rope(un_ref[:, N_RQ + lo:N_RQ + lo + HEAD_DIM], cos_f, sin_s)
        kr_s[:, hs] = _rope(un_ref[:, N_RK + lo:N_RK + lo + HEAD_DIM], cos_f, sin_s) * k_scale

    heads = range(HEADS)
    seqs = range(bb)
    hsl = [slice(h * HEAD_DIM, (h + 1) * HEAD_DIM) for h in heads]
    seq_id = lax.broadcasted_iota(jnp.int32, (bb, HEAD_DIM), 0)
    for h in heads:
        q_h, k_h, vws_h = q_s[:, hsl[h]], k_s[:, hsl[h]], vws_s[:, hsl[h]]
        qr_h = qr_s[:, hsl[h]]
        krd_h = kr_s[:, hsl[h]] * rtab_ref[1:2, h:h + 1]
        rv_h = ut_ref[:, T_RV + h * HEAD_DIM:T_RV + (h + 1) * HEAD_DIM]
        c_old = [C_ref[b, h] for b in seqs]
        s_old = [S_ref[b, h] for b in seqs]
        q_c = [_dot_nt(q_h, c_old[b]) for b in seqs]
        q_st = [_dot(qr_h, s_old[b]) for b in seqs]
        d_c = [_dot_tn(jnp.where(seq_id == b, vws_h, 0.0), k_h) for b in seqs]
        d_s = [_dot_tn(jnp.where(seq_id == b, krd_h, 0.0), rv_h) for b in seqs]
        for b in seqs:
            row = slice(b, b + 1)
            qc_s[row, hsl[h]] = q_c[b][row]
            qs_s[row, hsl[h]] = q_st[b][row]
            Co_ref[b, h] = carry_s[row, h:h + 1] * c_old[b] + d_c[b]
            So_ref[b, h] = rtab_ref[2:3, h:h + 1] * s_old[b] + d_s[b]

    for h in range(HEADS):
        lo = h * HEAD_DIM
        hs = slice(lo, lo + HEAD_DIM)
        q_h, k_h, n_h = q_s[:, hs], k_s[:, hs], n_ref[:, hs]
        ws_h, w_in_h = ws[:, h:h + 1], w_in[:, h:h + 1]
        wts = ws_h * jnp.sum(q_h * k_h, axis=1, keepdims=True)
        num = wts * ut_ref[:, T_V + lo:T_V + lo + HEAD_DIM] + w_in_h * qc_s[:, hs]
        den = wts + w_in_h * jnp.sum(n_h * q_h, axis=1, keepdims=True)
        hm = num / jnp.maximum(jnp.abs(den), jnp.exp(-m_t[:, h:h + 1]))
        mo = ut_ref[:, T_O + lo:T_O + lo + HEAD_DIM]
        h_ref[:, hs] = _head_norm(hm) * gmh_ref[:, hs] * jax.nn.sigmoid(mo)
        no_ref[:, hs] = w_in_h * n_h + ws_h * k_h

        att = jnp.sum(qr_s[:, hs] * kr_s[:, hs], axis=1, keepdims=True) * rtab_ref[3:4, h:h + 1]
        o = (att * ut_ref[:, T_RV + lo:T_RV + lo + HEAD_DIM]
             + rtab_ref[0:1, h:h + 1] * qs_s[:, hs])
        rg = ut_ref[:, T_RG + lo:T_RG + lo + HEAD_DIM]
        h_ref[:, MIX + lo:MIX + lo + HEAD_DIM] = _head_norm(o) * grh_ref[:, hs] * _silu(rg)


def _mixer_sample(un, ut, gs, conv_t, C, n2, m, S, bg, w_conv, b_conv, g_mh, g_rh, cos_f, sin_s, rtab,
                  bb=8):
    B = un.shape[0]
    rows = lambda cols: pl.BlockSpec((bb, cols), lambda i: (i, 0))
    mats = pl.BlockSpec((bb, HEADS, HEAD_DIM, HEAD_DIM), lambda i: (i, 0, 0, 0))
    convs = pl.BlockSpec((CONV_W - 1, bb, 2 * MIX), lambda i: (0, i, 0))
    in_specs = [rows(N_COLS), rows(T_ROWS), rows(GT_ROWS), convs, mats, rows(MIX), rows(HEADS), mats,
                _const_spec((1, GT_ROWS)), _const_spec((CONV_W, 2 * MIX)), _const_spec((1, 2 * MIX)),
                _const_spec((1, MIX)), _const_spec((1, MIX)),
                _const_spec((1, HEAD_DIM)), _const_spec((1, HEAD_DIM)), _const_spec((8, HEAD_DIM))]
    out_specs = [rows(D_MODEL), convs, mats, rows(MIX), rows(HEADS), mats]
    out_shape = [jax.ShapeDtypeStruct((B, D_MODEL), F32),
                 jax.ShapeDtypeStruct((CONV_W - 1, B, 2 * MIX), F32),
                 jax.ShapeDtypeStruct((B, HEADS, HEAD_DIM, HEAD_DIM), F32),
                 jax.ShapeDtypeStruct((B, MIX), F32),
                 jax.ShapeDtypeStruct((B, HEADS), F32),
                 jax.ShapeDtypeStruct((B, HEADS, HEAD_DIM, HEAD_DIM), F32)]
    scratch = [pltpu.VMEM((bb, MIX), F32) for _ in range(7)] + [pltpu.VMEM((bb, HEAD_DIM), F32)]
    return pl.pallas_call(
        functools.partial(_mixer_sample_kernel, bb=bb),
        grid=(B // bb,),
        in_specs=in_specs, out_specs=out_specs, out_shape=out_shape,
        scratch_shapes=scratch,
        compiler_params=_params("parallel"),
    )(un, ut, gs, conv_t, C, n2, m, S, bg, w_conv, b_conv, g_mh, g_rh, cos_f, sin_s, rtab)


def _outq_sample_kernel(x_ref, h_ref, wout_ref, wcq_ref, gx_ref, x1_ref, q_ref):
    x1 = x_ref[...] + _dot(h_ref[...], wout_ref[...])
    x1_ref[...] = x1
    q_ref[...] = _dot(_rms(x1, gx_ref[...]), wcq_ref[...])


def _outq_sample(x, hcat, w_out, w_cq, g_x):
    rows = x.shape[0]
    full = _const_spec((rows, D_MODEL))
    return pl.pallas_call(
        _outq_sample_kernel,
        grid=(1,),
        in_specs=[full, full, _const_spec((D_MODEL, D_MODEL)), _const_spec((D_MODEL, D_MODEL)),
                  _const_spec((1, D_MODEL))],
        out_specs=[pl.BlockSpec((rows, D_MODEL), lambda i: (0, 0))] * 2,
        out_shape=[jax.ShapeDtypeStruct((rows, D_MODEL), F32)] * 2,
        compiler_params=_params("arbitrary"),
    )(x, hcat, w_out, w_cq, g_x)


def _sample_attention(q_ref, k_ref, v_ref, o_ref):
    r_id = lax.broadcasted_iota(jnp.int32, (QROWS, KV_ROWS), 0)
    n_id = lax.broadcasted_iota(jnp.int32, (QROWS, KV_ROWS), 1)
    own = (n_id & 7) == (r_id >> 1) + 4 * (r_id & 1)
    low_half = (lax.broadcasted_iota(jnp.int32, (1, KV_ROWS), 1) & 4) == 0
    seqs = range(q_ref.shape[0])
    z = [_dot_nt(q_ref[j], k_ref[j]) for j in seqs]
    zc = [jnp.sum(jnp.where(own, z[j], 0.0), axis=0, keepdims=True) for j in seqs]
    yield
    other = [jnp.where(low_half, pltpu.roll(zc[j], KV_ROWS - 4, 1), pltpu.roll(zc[j], 4, 1))
             for j in seqs]
    s = [jnp.where(own, (zc[j] + other[j]) * (X_HEAD_DIM ** -0.5), -jnp.inf) for j in seqs]
    m = [jnp.max(s[j], axis=-1, keepdims=True) for j in seqs]
    e = [jnp.exp(s[j] - m[j]) for j in seqs]
    den = [jnp.sum(e[j], axis=-1, keepdims=True) for j in seqs]
    yield
    for j in seqs:
        o_ref[j] = _dot(e[j] / den[j], v_ref[j])
    yield


def _kv_rows(cache):
    B = cache.shape[0]
    c5 = cache.reshape(B, N_MEM, X_HEADS, 2, 128)
    return jnp.transpose(c5, (0, 1, 3, 2, 4)).reshape(B, KV_ROWS, 128)


def _kv_from_rows(rows):
    B = rows.shape[0]
    r5 = rows.reshape(B, N_MEM, 2, X_HEADS, 128)
    return jnp.transpose(r5, (0, 1, 3, 2, 4)).reshape(1, B, N_MEM, X_HEADS, X_HEAD_DIM)


def _rope_tables(pos):
    half = HEAD_DIM // 2
    inv = ROPE_THETA ** (-np.arange(half, dtype=np.float64) / half)
    ang = np.asarray(pos, np.float64)[:, None] * inv[None, :]
    cos, sin = np.cos(ang), np.sin(ang)
    return (np.concatenate([cos, cos], axis=-1).astype(np.float32),
            np.concatenate([-sin, sin], axis=-1).astype(np.float32))


def _retention_tables(L):
    lg = np.log1p(-np.exp2(-5.0 - np.arange(HEADS, dtype=np.float64)))
    t = np.arange(L, dtype=np.float64)
    diff = t[:, None] - t[None, :]
    decay = np.where(diff >= 0, np.exp(lg[:, None, None] * np.maximum(diff, 0.0)), 0.0)
    q_dec = np.exp(lg[:, None] * (t + 1.0))
    k_dec = np.exp(lg[:, None] * (L - 1.0 - t))
    chunk_dec = np.exp(lg * L)
    return tuple(a.astype(np.float32) for a in (decay, q_dec, k_dec, chunk_dec))


def _lanes(a, n):
    xp = np if isinstance(a, np.ndarray) else jnp
    return xp.broadcast_to(a[..., None], a.shape + (n,))


def _pad_rows(a, rows):
    return np.pad(a, ((0, rows - a.shape[0]),) + ((0, 0),) * (a.ndim - 1))


def kernel(x_prompt, x_sample, cache_mem_k, cache_mem_v, state_mlstm_conv, state_mlstm_C, state_mlstm_n, state_mlstm_m, state_ret_S, mem_prompt, w_in, b_gate, w_conv, b_conv, g_mix, g_mhead, g_rhead, w_out, g_xattn, g_mem, w_ck, w_cv, w_cq, w_co, g_ffn, w_gate, w_up, w_down, g_final):
    Bp, Tp, _ = x_prompt.shape
    Bs = x_sample.shape[0]
    l = 0
    n_m = 4 * MIX
    wi = w_in[l]
    w_gates = wi[:, n_m:n_m + 2 * HEADS]
    w_ret = wi[:, n_m + 2 * HEADS:]
    row = lambda a: a.reshape(1, -1)
    bf = lambda a: a.astype(BF16)
    g_mix_r, g_mh_r, g_rh_r = row(g_mix[l]), row(g_mhead[l]), row(g_rhead[l])
    g_x_r, g_mem_r, g_ffn_r, g_fin_r = row(g_xattn[l]), row(g_mem[l]), row(g_ffn[l]), row(g_final)
    b_conv_r = row(b_conv[l])
    w_out_b, w_cq_b, w_co_b = bf(w_out[l]), bf(w_cq[l]), bf(w_co[l])
    w_gate_b, w_up_b, w_down_b = bf(w_gate[l]), bf(w_up[l]), bf(w_down[l])

    mk_b, mv_b, mk_rows, mv_rows = _memkv(mem_prompt, g_mem_r, bf(w_ck[l]), bf(w_cv[l]))
    wn = bf(jnp.concatenate([wi[:, :2 * MIX], w_ret[:, :2 * MIX]], axis=1))
    wt = bf(jnp.concatenate([wi[:, 2 * MIX:n_m], w_ret[:, 2 * MIX:]], axis=1).T)
    gate_rows = lambda a: jnp.concatenate(
        [a[:HEADS], jnp.zeros((8 - HEADS,) + a.shape[1:], F32),
         a[HEADS:], jnp.zeros((8 - HEADS,) + a.shape[1:], F32)], axis=0)
    wgt = bf(gate_rows(w_gates.T))
    bg_rows = gate_rows(b_gate[l][:, None])
    bgt = jnp.broadcast_to(bg_rows, (GT_ROWS, CHUNK))
    cos_p, sin_p = _rope_tables(np.arange(Tp))
    decay, q_dec, k_dec, chunk_dec = _retention_tables(CHUNK)
    hcat_p, conv_p, C_p, n_p, m_p, S_p = _mixer_prompt(
        x_prompt, wn, wt, wgt, g_mix_r, bgt, w_conv[l], b_conv_r,
        _lanes(g_mhead[l], CHUNK), _lanes(g_rhead[l], CHUNK), cos_p, sin_p,
        np.swapaxes(decay, 1, 2), _pad_rows(q_dec, 8), _lanes(k_dec, HEAD_DIM),
        _pad_rows(_lanes(chunk_dec, HEAD_DIM), 8))

    xs = x_sample.reshape(Bs, D_MODEL)
    un_s, ut_s, gs_s = _inproj_sample(xs, g_mix_r, wn, wt, wgt)
    cos_s, sin_s = _rope_tables(PAST_LEN + np.arange(1))
    decay1, q_dec1, k_dec1, chunk_dec1 = _retention_tables(1)
    rtab = np.zeros((8, HEAD_DIM), np.float32)
    rtab[:4, :HEADS] = np.stack([q_dec1[:, 0], k_dec1[:, 0], chunk_dec1, decay1[:, 0, 0]])
    hcat_s, conv_s, C_s, n_s, m_s, S_s = _mixer_sample(
        un_s, ut_s, gs_s, jnp.transpose(state_mlstm_conv[l], (1, 0, 2)), state_mlstm_C[l],
        state_mlstm_n[l].reshape(Bs, MIX), state_mlstm_m[l], state_ret_S[l],
        bg_rows.reshape(1, GT_ROWS), w_conv[l], b_conv_r, g_mh_r, g_rh_r, cos_s, sin_s, rtab)
    x1_s, q_s = _outq_sample(xs, hcat_s, w_out_b, w_cq_b, g_x_r)

    x1_p, o_p, o_s = _attn_prompt(x_prompt, hcat_p, mk_b, mv_b, w_out_b, w_cq_b, g_x_r,
                                  q_s, _kv_rows(cache_mem_k[l]), _kv_rows(cache_mem_v[l]))
    y_p = _ffn(x1_p.reshape(Bp * Tp, D_MODEL), o_p.reshape(Bp * Tp, D_MODEL),
               w_co_b, w_gate_b, w_up_b, w_down_b, g_ffn_r, g_fin_r, tm=512)
    y_s = _ffn(x1_s, o_s, w_co_b, w_gate_b, w_up_b, w_down_b, g_ffn_r, g_fin_r, tm=Bs)

    return (y_p.reshape(Bp, Tp, D_MODEL), y_s.reshape(Bs, 1, D_MODEL),
            _kv_from_rows(mk_rows), _kv_from_rows(mv_rows),
            conv_p[None], C_p[None], n_p[None], m_p[None, :, :HEADS, 0], S_p[None],
            jnp.transpose(conv_s, (1, 0, 2))[None], C_s[None],
            n_s.reshape(1, Bs, HEADS, HEAD_DIM), m_s[None], S_s[None])
```

```python
import functools

import jax
import jax.numpy as jnp
import numpy as np
from jax import lax
from jax.experimental import pallas as pl
from jax.experimental.pallas import tpu as pltpu

F32 = jnp.float32
BF16 = jnp.bfloat16

D_MODEL = 1024
HEADS = 4
HEAD_DIM = 128
MIX = HEADS * HEAD_DIM
CONV_W = 4
CHUNK = 128
N_MEM = 256
X_HEADS = 4
X_HEAD_DIM = 256
D_FF = 2816
ROPE_THETA = 10000.0
EPS = 1e-6
PAST_LEN = 16384
KV_ROWS = N_MEM * 2 * X_HEADS
QROWS = 2 * X_HEADS

N_QK, N_RQ, N_RK, N_COLS = 0, 1024, 1536, 2048
T_V, T_O, T_RV, T_RG, T_ROWS = 0, 512, 1024, 1536, 2048
GT_ROWS = 16
UOFF = 8
ROW_GROUPS = 2

V7X_VMEM_LIMIT = 56 * 1024 * 1024

NT_DIMS = (((1,), (1,)), ((), ()))
TN_DIMS = (((0,), (0,)), ((), ()))


def _dot(a, b):
    return jnp.dot(a.astype(BF16), b.astype(BF16), preferred_element_type=F32)


def _dot_nt(a, b):
    return lax.dot_general(a.astype(BF16), b.astype(BF16), NT_DIMS, preferred_element_type=F32)


def _dot_tn(a, b):
    return lax.dot_general(a.astype(BF16), b.astype(BF16), TN_DIMS, preferred_element_type=F32)


def _rms(x, g):
    return x * lax.rsqrt(jnp.mean(x * x, axis=-1, keepdims=True) + EPS) * g


def _head_norm(h):
    return h * lax.rsqrt(jnp.mean(h * h, axis=-1, keepdims=True) + EPS)


def _silu(x):
    return x * jax.nn.sigmoid(x)


def _log_sigmoid(x):
    return jnp.minimum(x, 0.0) - jnp.log1p(jnp.exp(-jnp.abs(x)))


def _rope(x, cos_full, sin_signed):
    return x * cos_full + pltpu.roll(x, HEAD_DIM // 2, 1) * sin_signed


def _const_spec(shape):
    zeros = (0,) * len(shape)
    return pl.BlockSpec(shape, lambda *_: zeros, pipeline_mode=pl.Buffered(1))


def _params(*sem):
    return pltpu.CompilerParams(dimension_semantics=sem, vmem_limit_bytes=V7X_VMEM_LIMIT)


def _memkv_kernel(mem_ref, g_ref, wk_ref, wv_ref, kb_ref, vb_ref, krows_ref, vrows_ref):
    mn = _rms(mem_ref[...], g_ref[...]).astype(BF16)
    for w_ref, b_ref, rows_ref in ((wk_ref, kb_ref, krows_ref), (wv_ref, vb_ref, vrows_ref)):
        proj = jnp.dot(mn, w_ref[...], preferred_element_type=F32)
        b_ref[...] = proj.astype(BF16)
        for h in range(X_HEADS):
            for c in range(2):
                lane0 = h * X_HEAD_DIM + c * 128
                rows_ref[pl.ds(c * X_HEADS + h, N_MEM, stride=QROWS), :] = proj[:, lane0:lane0 + 128]


def _memkv(mem, g_mem, wk, wv):
    B = mem.shape[0]
    tok = pl.BlockSpec((None, N_MEM, D_MODEL), lambda b: (b, 0, 0))
    rows = pl.BlockSpec((None, KV_ROWS, 128), lambda b: (b, 0, 0))
    return pl.pallas_call(
        _memkv_kernel,
        grid=(B,),
        in_specs=[tok, _const_spec((1, D_MODEL)),
                  _const_spec((D_MODEL, D_MODEL)), _const_spec((D_MODEL, D_MODEL))],
        out_specs=[tok, tok, rows, rows],
        out_shape=[jax.ShapeDtypeStruct((B, N_MEM, D_MODEL), BF16)] * 2
        + [jax.ShapeDtypeStruct((B, KV_ROWS, 128), F32)] * 2,
        compiler_params=_params("parallel"),
    )(mem, g_mem, wk, wv)


def _mixer_prompt_kernel(x_ref, xs_ref, wn_ref, wt_ref, wgt_ref, wout_ref, gmix_ref, bgt_ref,
                         wconv_ref, bconv_ref, gmh_ref, grh_ref, cos_ref, sin_ref,
                         decay_ref, qdec_ref, kdec_ref, cdec_ref,
                         x1_ref, conv_ref, C_ref, n_ref, m_ref, S_ref,
                         un_a, ut_a, gt_a, un_b, ut_b, gt_b, tail_ref, st_ref, h_ref,
                         *, tt, tiles_per_seq):
    L = CHUNK
    s = pl.program_id(0)
    t = (jnp.maximum(s, 1) - 1) % tiles_per_seq
    chunks = range(tt // L)
    slot_a, slot_b = (un_a, ut_a, gt_a), (un_b, ut_b, gt_b)

    @pl.when(s == 0)
    def _():
        for ref in slot_b + (tail_ref,):
            ref[...] = jnp.zeros_like(ref)

    @pl.when(t == 0)
    def _():
        C_ref[...] = jnp.zeros_like(C_ref)
        st_ref[...] = jnp.zeros_like(st_ref)
        n_ref[...] = jnp.zeros_like(n_ref)
        m_ref[...] = jnp.zeros_like(m_ref)

    k_scale = HEAD_DIM ** -0.5

    def project(slot, xn, part):
        un_ref, ut_ref, gt_ref = slot
        c0 = part * MIX
        un_ref[UOFF:UOFF + tt, c0:c0 + MIX] = jnp.dot(xn, wn_ref[:, c0:c0 + MIX],
                                                      preferred_element_type=F32)
        res = lax.dot_general(wt_ref[c0:c0 + MIX, :], xn, NT_DIMS, preferred_element_type=F32)
        for c in chunks:
            ut_ref[c, c0:c0 + MIX, :] = res[:, c * L:(c + 1) * L]
        if part == 0:
            gates_t = lax.dot_general(wgt_ref[...], xn, NT_DIMS, preferred_element_type=F32)
            for c in chunks:
                gt_ref[c] = gates_t[:, c * L:(c + 1) * L]

    src_id = lax.broadcasted_iota(jnp.int32, (L, L), 0)
    tgt_id = lax.broadcasted_iota(jnp.int32, (L, L), 1)
    causal = src_id <= tgt_id
    triu_bf = jnp.where(causal, 1.0, 0.0).astype(BF16)
    heads = range(HEADS)
    hcol = lambda base, h: slice(base + h * HEAD_DIM, base + (h + 1) * HEAD_DIM)

    def chunk_body(c, slot):
        un_ref, ut_ref, gt_ref = slot
        r0 = c * L
        rows = pl.ds(UOFF + r0, L)
        trows = pl.ds(r0, L)
        g_t = gt_ref[c] + bgt_ref[...]
        ig = g_t[0:8]
        lf = _log_sigmoid(g_t[8:16])
        lf_hi = lf.astype(BF16)
        r1 = lf - lf_hi.astype(F32)
        lf_mid = r1.astype(BF16)
        lf_lo = (r1 - lf_mid.astype(F32)).astype(BF16)
        bc = (jnp.dot(lf_hi, triu_bf, preferred_element_type=F32)
              + jnp.dot(lf_mid, triu_bf, preferred_element_type=F32)
              + jnp.dot(lf_lo, triu_bf, preferred_element_type=F32))
        m_prev = m_ref[...]
        inter = bc + m_prev
        b_last = bc[:, L - 1:L]
        g_w = b_last - bc + ig
        m_new = jnp.maximum(b_last + m_prev, jnp.max(g_w, axis=1, keepdims=True))
        ws = jnp.exp(g_w - m_new)
        carry = jnp.exp(b_last + m_prev - m_new)
        a_n = jnp.concatenate([ig - bc, jnp.zeros((L - 8, L), F32)], axis=0).T

        def conv_act(col):
            win = un_ref[pl.ds(r0, L + UOFF), col:col + HEAD_DIM]
            acc = bconv_ref[:, col:col + HEAD_DIM]
            for j in range(CONV_W):
                back = CONV_W - 1 - j
                tap = win if back == 0 else pltpu.roll(win, back, 0)
                acc = acc + tap[UOFF:UOFF + L] * wconv_ref[j:j + 1, col:col + HEAD_DIM]
            return _silu(acc)

        cos_f, sin_s = cos_ref[trows, :], sin_ref[trows, :]
        q = [conv_act(N_QK + h * HEAD_DIM) for h in heads]
        kb = [(conv_act(N_QK + MIX + h * HEAD_DIM) * k_scale).astype(BF16) for h in heads]
        qb = [a.astype(BF16) for a in q]
        rqb = [_rope(un_ref[rows, hcol(N_RQ, h)], cos_f, sin_s).astype(BF16) for h in heads]
        rk = [_rope(un_ref[rows, hcol(N_RK, h)], cos_f, sin_s) * k_scale for h in heads]
        v_t = [ut_ref[c, hcol(T_V, h), :] for h in heads]
        rvb = [ut_ref[c, hcol(T_RV, h), :].astype(BF16) for h in heads]
        c_old = [C_ref[h] for h in heads]
        s_old = [st_ref[h] for h in heads]
        n_old = [n_ref[h:h + 1, :] for h in heads]
        qk = [_dot_nt(kb[h], qb[h]) for h in heads]
        att = [_dot_nt(rk[h], rqb[h]) for h in heads]
        c_q = [_dot_nt(c_old[h], qb[h]) for h in heads]
        s_q = [_dot_nt(s_old[h], rqb[h]) for h in heads]
        n_q = [_dot_nt(jnp.broadcast_to(n_old[h], (8, HEAD_DIM)), qb[h])[0:1] for h in heads]
        d_c = [_dot(v_t[h] * ws[h:h + 1, :], kb[h]) for h in heads]
        d_n = [_dot(jnp.broadcast_to(ws[h:h + 1, :], (8, L)), kb[h])[0:1] for h in heads]
        d_s = [_dot(rvb[h], rk[h] * kdec_ref[h]) for h in heads]
        m_t, w_in, wts = [], [], []
        for h in heads:
            dmat = a_n[:, h:h + 1] + bc[h:h + 1, :]
            dmat = jnp.where(causal, dmat, -jnp.inf)
            m_t.append(jnp.maximum(inter[h:h + 1, :], jnp.max(dmat, axis=0, keepdims=True)))
            wts.append(jnp.exp(dmat - m_t[h]) * qk[h])
            w_in.append(jnp.exp(inter[h:h + 1, :] - m_t[h]))
        att_w = [att[h] * decay_ref[h] for h in heads]
        v_p = [_dot(v_t[h], wts[h]) for h in heads]
        v_a = [_dot(rvb[h], att_w[h]) for h in heads]
        for h in heads:
            num = v_p[h] + w_in[h] * c_q[h]
            den = jnp.sum(wts[h], axis=0, keepdims=True) + w_in[h] * n_q[h]
            hm = num / jnp.maximum(jnp.abs(den), jnp.exp(-m_t[h]))
            hm = hm * lax.rsqrt(jnp.mean(hm * hm, axis=0, keepdims=True) + EPS)
            hm = hm * gmh_ref[hcol(0, h), :] * jax.nn.sigmoid(ut_ref[c, hcol(T_O, h), :])
            h_ref[trows, hcol(0, h)] = hm.T.astype(BF16)
            carry_h = carry[h:h + 1, :]
            C_ref[h] = carry_h * c_old[h] + d_c[h]
            n_ref[h:h + 1, :] = carry_h * n_old[h] + d_n[h]
            o = v_a[h] + qdec_ref[h:h + 1, :] * s_q[h]
            st_ref[h] = cdec_ref[h:h + 1, :] * s_old[h] + d_s[h]
            hr = o * lax.rsqrt(jnp.mean(o * o, axis=0, keepdims=True) + EPS)
            hr = hr * grh_ref[hcol(0, h), :] * _silu(ut_ref[c, hcol(T_RG, h), :])
            h_ref[trows, hcol(MIX, h)] = hr.T.astype(BF16)

        m_ref[...] = m_new

    def step(proj_slot, scan_slot):
        un_ref = scan_slot[0]
        un_ref[0:UOFF, N_QK:N_RQ] = jnp.where(t == 0, 0.0, tail_ref[...])
        xn = _rms(x_ref[...], gmix_ref[...]).astype(BF16)
        for c in chunks:
            project(proj_slot, xn, c)
            chunk_body(c, scan_slot)
        tail_ref[...] = un_ref[tt:tt + UOFF, N_QK:N_RQ]
        x1_ref[...] = xs_ref[...] + jnp.dot(h_ref[...], wout_ref[...], preferred_element_type=F32)

    @pl.when(s % 2 == 0)
    def _():
        step(slot_a, slot_b)

    @pl.when(s % 2 == 1)
    def _():
        step(slot_b, slot_a)

    @pl.when(t == tiles_per_seq - 1)
    def _():
        conv_ref[...] = tail_ref[UOFF - (CONV_W - 1):UOFF, :]
        for h in heads:
            S_ref[h] = st_ref[h].T


def _mixer_prompt(x, wn, wt, wgt, w_out, g_mix, bgt, w_conv, b_conv, gmh_cols, grh_cols, cos_f, sin_s,
                  decay_t, qdec_rows, kdec_cols, cdec_rows, tt=512):
    B, T, _ = x.shape
    tps = T // tt
    n_tiles = B * tps
    proj = lambda s: jnp.minimum(s, n_tiles - 1)
    scan = lambda s: jnp.maximum(s, 1) - 1
    per_b3 = lambda s: (scan(s) // tps, 0, 0)
    per_b4 = lambda s: (scan(s) // tps, 0, 0, 0)
    scan_tile = pl.BlockSpec((None, tt, D_MODEL), lambda s: (scan(s) // tps, scan(s) % tps, 0))
    in_specs = [
        pl.BlockSpec((None, tt, D_MODEL), lambda s: (proj(s) // tps, proj(s) % tps, 0)),
        scan_tile,
        _const_spec((D_MODEL, N_COLS)), _const_spec((T_ROWS, D_MODEL)), _const_spec((GT_ROWS, D_MODEL)),
        _const_spec((D_MODEL, D_MODEL)),
        _const_spec((1, D_MODEL)), _const_spec((GT_ROWS, CHUNK)),
        _const_spec((CONV_W, 2 * MIX)), _const_spec((1, 2 * MIX)),
        _const_spec((MIX, CHUNK)), _const_spec((MIX, CHUNK)),
        pl.BlockSpec((tt, HEAD_DIM), lambda s: (scan(s) % tps, 0)),
        pl.BlockSpec((tt, HEAD_DIM), lambda s: (scan(s) % tps, 0)),
        _const_spec((HEADS, CHUNK, CHUNK)),
        _const_spec((8, CHUNK)), _const_spec((HEADS, CHUNK, HEAD_DIM)), _const_spec((8, HEAD_DIM)),
    ]
    out_specs = [
        scan_tile,
        pl.BlockSpec((None, CONV_W - 1, 2 * MIX), per_b3),
        pl.BlockSpec((None, HEADS, HEAD_DIM, HEAD_DIM), per_b4),
        pl.BlockSpec((None, HEADS, HEAD_DIM), per_b3),
        pl.BlockSpec((None, 8, CHUNK), per_b3),
        pl.BlockSpec((None, HEADS, HEAD_DIM, HEAD_DIM), per_b4),
    ]
    out_shape = [
        jax.ShapeDtypeStruct((B, T, D_MODEL), F32),
        jax.ShapeDtypeStruct((B, CONV_W - 1, 2 * MIX), F32),
        jax.ShapeDtypeStruct((B, HEADS, HEAD_DIM, HEAD_DIM), F32),
        jax.ShapeDtypeStruct((B, HEADS, HEAD_DIM), F32),
        jax.ShapeDtypeStruct((B, 8, CHUNK), F32),
        jax.ShapeDtypeStruct((B, HEADS, HEAD_DIM, HEAD_DIM), F32),
    ]
    n_chunks = tt // CHUNK
    slot = [
        pltpu.VMEM((tt + UOFF, N_COLS), F32),
        pltpu.VMEM((n_chunks, T_ROWS, CHUNK), F32),
        pltpu.VMEM((n_chunks, GT_ROWS, CHUNK), F32),
    ]
    scratch = slot + slot + [
        pltpu.VMEM((UOFF, 2 * MIX), F32),
        pltpu.VMEM((HEADS, HEAD_DIM, HEAD_DIM), F32),
        pltpu.VMEM((tt, D_MODEL), BF16),
    ]
    return pl.pallas_call(
        functools.partial(_mixer_prompt_kernel, tt=tt, tiles_per_seq=tps),
        grid=(n_tiles + 1,),
        in_specs=in_specs, out_specs=out_specs, out_shape=out_shape,
        scratch_shapes=scratch,
        compiler_params=_params("arbitrary"),
    )(x, x, wn, wt, wgt, w_out, g_mix, bgt, w_conv, b_conv, gmh_cols, grh_cols, cos_f, sin_s,
      decay_t, qdec_rows, kdec_cols, cdec_rows)


def _attn_prompt_kernel(x1_ref, k_ref, v_ref, wcq_ref, gx_ref, qs_ref, ck_ref, cv_ref, o_ref, os_ref):
    sample_stages = _sample_attention(qs_ref, ck_ref, cv_ref, os_ref)
    tm = x1_ref.shape[0]
    rows = [slice(r, r + tm // ROW_GROUPS) for r in range(0, tm, tm // ROW_GROUPS)]
    xq = [_rms(x1_ref[r, :], gx_ref[...]).astype(BF16) for r in rows]
    next(sample_stages)
    q = [jnp.dot(a, wcq_ref[...], preferred_element_type=F32).astype(BF16) for a in xq]
    next(sample_stages)
    sl = [slice(h * X_HEAD_DIM, (h + 1) * X_HEAD_DIM) for h in range(X_HEADS)]
    items = [(g, h) for g in range(ROW_GROUPS) for h in range(X_HEADS)]
    s = [_dot_nt(q[g][:, sl[h]], k_ref[:, sl[h]]) * (X_HEAD_DIM ** -0.5) for g, h in items]
    next(sample_stages)
    e = [jnp.exp(a - jnp.max(a, axis=-1, keepdims=True)) for a in s]
    p = [a / jnp.sum(a, axis=-1, keepdims=True) for a in e]
    for (g, h), a in zip(items, p):
        o_ref[rows[g], sl[h]] = _dot(a, v_ref[:, sl[h]]).astype(BF16)


def _attn_prompt(x1, mk, mv, w_cq, g_x, q_s, ck_s, cv_s, tm=512):
    B, T, _ = x1.shape
    Bs = q_s.shape[0]
    steps = B * (T // tm)
    ba = Bs // steps
    assert ba * steps == Bs
    tile = pl.BlockSpec((None, tm, D_MODEL), lambda b, t: (b, t, 0))
    kv = pl.BlockSpec((None, N_MEM, D_MODEL), lambda b, t: (b, 0, 0))
    step = lambda b, t: (b * (T // tm) + t, 0, 0)
    s_rows = pl.BlockSpec((ba, QROWS, 128), step)
    s_kv = pl.BlockSpec((ba, KV_ROWS, 128), step)
    o, o_s = pl.pallas_call(
        _attn_prompt_kernel,
        grid=(B, T // tm),
        in_specs=[tile, kv, kv, _const_spec((D_MODEL, D_MODEL)), _const_spec((1, D_MODEL)),
                  s_rows, s_kv, s_kv],
        out_specs=[tile, s_rows],
        out_shape=[jax.ShapeDtypeStruct((B, T, D_MODEL), BF16),
                   jax.ShapeDtypeStruct((Bs, QROWS, 128), F32)],
        compiler_params=_params("parallel", "parallel"),
    )(x1, mk, mv, w_cq, g_x, q_s.reshape(Bs, QROWS, 128), ck_s, cv_s)
    return o, o_s.reshape(Bs, D_MODEL)


V7X_MXU_DIM = 256
FF_CHUNKS = ((0, 6 * V7X_MXU_DIM), (6 * V7X_MXU_DIM, D_FF))


def _ffn_kernel(x1_ref, o_ref, wco_ref, wg_ref, wu_ref, wd_ref, gffn_ref, gfin_ref, y_ref, *,
                n_groups):
    tm = x1_ref.shape[0]
    rows = [slice(r, r + tm // n_groups) for r in range(0, tm, tm // n_groups)]
    acc = [x1_ref[r, :] + _dot(o_ref[r, :], wco_ref[...]) for r in rows]
    hf = [_rms(a, gffn_ref[...]).astype(BF16) for a in acc]
    for c0, c1 in FF_CHUNKS:
        gate = [jnp.dot(h, wg_ref[:, c0:c1], preferred_element_type=F32) for h in hf]
        up = [jnp.dot(h, wu_ref[:, c0:c1], preferred_element_type=F32) for h in hf]
        act = [_silu(g) * u for g, u in zip(gate, up)]
        acc = [a + _dot(p, wd_ref[c0:c1, :]) for a, p in zip(acc, act)]
    for r, a in zip(rows, acc):
        y_ref[r, :] = _rms(a, gfin_ref[...])


def _ffn(x1, o, w_co, w_gate, w_up, w_down, g_ffn, g_final, tm):
    rows = x1.shape[0]
    row_spec = pl.BlockSpec((tm, D_MODEL), lambda i: (i, 0))
    return pl.pallas_call(
        functools.partial(_ffn_kernel, n_groups=ROW_GROUPS if tm >= 512 else 1),
        grid=(rows // tm,),
        in_specs=[row_spec, row_spec, _const_spec((D_MODEL, D_MODEL)),
                  _const_spec((D_MODEL, D_FF)), _const_spec((D_MODEL, D_FF)),
                  _const_spec((D_FF, D_MODEL)), _const_spec((1, D_MODEL)), _const_spec((1, D_MODEL))],
        out_specs=row_spec,
        out_shape=jax.ShapeDtypeStruct((rows, D_MODEL), F32),
        compiler_params=_params("parallel"),
    )(x1, o, w_co, w_gate, w_up, w_down, g_ffn, g_final)


def _inproj_sample_kernel(x_ref, g_ref, wn_ref, wt_ref, wgt_ref, un_ref, ut_ref, gs_ref):
    xn = _rms(x_ref[...], g_ref[...]).astype(BF16)
    un_ref[...] = jnp.dot(xn, wn_ref[...], preferred_element_type=F32)
    ut_ref[...] = lax.dot_general(xn, wt_ref[...], NT_DIMS, preferred_element_type=F32)
    gs_ref[...] = lax.dot_general(xn, wgt_ref[...], NT_DIMS, preferred_element_type=F32)


def _inproj_sample(x, g_mix, wn, wt, wgt):
    rows = x.shape[0]
    full = lambda cols: pl.BlockSpec((rows, cols), lambda i: (0, 0))
    return pl.pallas_call(
        _inproj_sample_kernel,
        grid=(1,),
        in_specs=[_const_spec((rows, D_MODEL)), _const_spec((1, D_MODEL)),
                  _const_spec((D_MODEL, N_COLS)), _const_spec((T_ROWS, D_MODEL)),
                  _const_spec((GT_ROWS, D_MODEL))],
        out_specs=[full(N_COLS), full(T_ROWS), full(GT_ROWS)],
        out_shape=[jax.ShapeDtypeStruct((rows, N_COLS), F32),
                   jax.ShapeDtypeStruct((rows, T_ROWS), F32),
                   jax.ShapeDtypeStruct((rows, GT_ROWS), F32)],
        compiler_params=_params("arbitrary"),
    )(x, g_mix, wn, wt, wgt)


def _mixer_sample_kernel(un_ref, ut_ref, gs_ref, conv_ref, C_ref, n_ref, m_ref, S_ref, bg_ref,
                         wconv_ref, bconv_ref, gmh_ref, grh_ref, cos_ref, sin_ref, rtab_ref,
                         h_ref, convo_ref, Co_ref, no_ref, mo_ref, So_ref,
                         q_s, k_s, vws_s, qr_s, kr_s, qc_s, qs_s, carry_s, *, bb):
    k_scale = HEAD_DIM ** -0.5
    uqk = un_ref[:, N_QK:N_RQ]
    conv = (bconv_ref[...] + wconv_ref[0:1, :] * conv_ref[0] + wconv_ref[1:2, :] * conv_ref[1]
            + wconv_ref[2:3, :] * conv_ref[2] + wconv_ref[3:4, :] * uqk)
    convo_ref[0] = conv_ref[1]
    convo_ref[1] = conv_ref[2]
    convo_ref[2] = uqk
    qk_act = _silu(conv)
    q_s[...] = qk_act[:, 0:MIX]
    k_s[...] = qk_act[:, MIX:2 * MIX] * k_scale

    gates = gs_ref[...] + bg_ref[...]
    ig = gates[:, 0:HEADS]
    lf = _log_sigmoid(gates[:, 8:8 + HEADS])
    inter = lf + m_ref[...]
    m_t = jnp.maximum(inter, ig)
    ws = jnp.exp(ig - m_t)
    w_in = jnp.exp(inter - m_t)
    mo_ref[...] = m_t
    carry_s[:, 0:HEADS] = w_in

    cos_f = cos_ref[...]
    sin_s = sin_ref[...]
    for h in range(HEADS):
        lo = h * HEAD_DIM
        hs = slice(lo, lo + HEAD_DIM)
        vws_s[:, hs] = ut_ref[:, T_V + lo:T_V + lo + HEAD_DIM] * ws[:, h:h + 1]
        qr_s[:, hs] = _rope(un_ref[:, N_RQ + lo:N_RQ + lo + HEAD_DIM], cos_f, sin_s)
        kr_s[:, hs] = _rope(un_ref[:, N_RK + lo:N_RK + lo + HEAD_DIM], cos_f, sin_s) * k_scale

    heads = range(HEADS)
    seqs = range(bb)
    hsl = [slice(h * HEAD_DIM, (h + 1) * HEAD_DIM) for h in heads]
    seq_id = lax.broadcasted_iota(jnp.int32, (bb, HEAD_DIM), 0)
    for h in heads:
        q_h, k_h, vws_h = q_s[:, hsl[h]], k_s[:, hsl[h]], vws_s[:, hsl[h]]
        qr_h = qr_s[:, hsl[h]]
        krd_h = kr_s[:, hsl[h]] * rtab_ref[1:2, h:h + 1]
        rv_h = ut_ref[:, T_RV + h * HEAD_DIM:T_RV + (h + 1) * HEAD_DIM]
        c_old = [C_ref[b, h] for b in seqs]
        s_old = [S_ref[b, h] for b in seqs]
        q_c = [_dot_nt(q_h, c_old[b]) for b in seqs]
        q_st = [_dot(qr_h, s_old[b]) for b in seqs]
        d_c = [_dot_tn(jnp.where(seq_id == b, vws_h, 0.0), k_h) for b in seqs]
        d_s = [_dot_tn(jnp.where(seq_id == b, krd_h, 0.0), rv_h) for b in seqs]
        for b in seqs:
            row = slice(b, b + 1)
            qc_s[row, hsl[h]] = q_c[b][row]
            qs_s[row, hsl[h]] = q_st[b][row]
            Co_ref[b, h] = carry_s[row, h:h + 1] * c_old[b] + d_c[b]
            So_ref[b, h] = rtab_ref[2:3, h:h + 1] * s_old[b] + d_s[b]

    for h in range(HEADS):
        lo = h * HEAD_DIM
        hs = slice(lo, lo + HEAD_DIM)
        q_h, k_h, n_h = q_s[:, hs], k_s[:, hs], n_ref[:, hs]
        ws_h, w_in_h = ws[:, h:h + 1], w_in[:, h:h + 1]
        wts = ws_h * jnp.sum(q_h * k_h, axis=1, keepdims=True)
        num = wts * ut_ref[:, T_V + lo:T_V + lo + HEAD_DIM] + w_in_h * qc_s[:, hs]
        den = wts + w_in_h * jnp.sum(n_h * q_h, axis=1, keepdims=True)
        hm = num / jnp.maximum(jnp.abs(den), jnp.exp(-m_t[:, h:h + 1]))
        mo = ut_ref[:, T_O + lo:T_O + lo + HEAD_DIM]
        h_ref[:, hs] = _head_norm(hm) * gmh_ref[:, hs] * jax.nn.sigmoid(mo)
        no_ref[:, hs] = w_in_h * n_h + ws_h * k_h

        att = jnp.sum(qr_s[:, hs] * kr_s[:, hs], axis=1, keepdims=True) * rtab_ref[3:4, h:h + 1]
        o = (att * ut_ref[:, T_RV + lo:T_RV + lo + HEAD_DIM]
             + rtab_ref[0:1, h:h + 1] * qs_s[:, hs])
        rg = ut_ref[:, T_RG + lo:T_RG + lo + HEAD_DIM]
        h_ref[:, MIX + lo:MIX + lo + HEAD_DIM] = _head_norm(o) * grh_ref[:, hs] * _silu(rg)


def _mixer_sample(un, ut, gs, conv_t, C, n2, m, S, bg, w_conv, b_conv, g_mh, g_rh, cos_f, sin_s, rtab,
                  bb=8):
    B = un.shape[0]
    rows = lambda cols: pl.BlockSpec((bb, cols), lambda i: (i, 0))
    mats = pl.BlockSpec((bb, HEADS, HEAD_DIM, HEAD_DIM), lambda i: (i, 0, 0, 0))
    convs = pl.BlockSpec((CONV_W - 1, bb, 2 * MIX), lambda i: (0, i, 0))
    in_specs = [rows(N_COLS), rows(T_ROWS), rows(GT_ROWS), convs, mats, rows(MIX), rows(HEADS), mats,
                _const_spec((1, GT_ROWS)), _const_spec((CONV_W, 2 * MIX)), _const_spec((1, 2 * MIX)),
                _const_spec((1, MIX)), _const_spec((1, MIX)),
                _const_spec((1, HEAD_DIM)), _const_spec((1, HEAD_DIM)), _const_spec((8, HEAD_DIM))]
    out_specs = [rows(D_MODEL), convs, mats, rows(MIX), rows(HEADS), mats]
    out_shape = [jax.ShapeDtypeStruct((B, D_MODEL), F32),
                 jax.ShapeDtypeStruct((CONV_W - 1, B, 2 * MIX), F32),
                 jax.ShapeDtypeStruct((B, HEADS, HEAD_DIM, HEAD_DIM), F32),
                 jax.ShapeDtypeStruct((B, MIX), F32),
                 jax.ShapeDtypeStruct((B, HEADS), F32),
                 jax.ShapeDtypeStruct((B, HEADS, HEAD_DIM, HEAD_DIM), F32)]
    scratch = [pltpu.VMEM((bb, MIX), F32) for _ in range(7)] + [pltpu.VMEM((bb, HEAD_DIM), F32)]
    return pl.pallas_call(
        functools.partial(_mixer_sample_kernel, bb=bb),
        grid=(B // bb,),
        in_specs=in_specs, out_specs=out_specs, out_shape=out_shape,
        scratch_shapes=scratch,
        compiler_params=_params("parallel"),
    )(un, ut, gs, conv_t, C, n2, m, S, bg, w_conv, b_conv, g_mh, g_rh, cos_f, sin_s, rtab)


def _outq_sample_kernel(x_ref, h_ref, wout_ref, wcq_ref, gx_ref, x1_ref, q_ref):
    x1 = x_ref[...] + _dot(h_ref[...], wout_ref[...])
    x1_ref[...] = x1
    q_ref[...] = _dot(_rms(x1, gx_ref[...]), wcq_ref[...])


def _outq_sample(x, hcat, w_out, w_cq, g_x):
    rows = x.shape[0]
    full = _const_spec((rows, D_MODEL))
    return pl.pallas_call(
        _outq_sample_kernel,
        grid=(1,),
        in_specs=[full, full, _const_spec((D_MODEL, D_MODEL)), _const_spec((D_MODEL, D_MODEL)),
                  _const_spec((1, D_MODEL))],
        out_specs=[pl.BlockSpec((rows, D_MODEL), lambda i: (0, 0))] * 2,
        out_shape=[jax.ShapeDtypeStruct((rows, D_MODEL), F32)] * 2,
        compiler_params=_params("arbitrary"),
    )(x, hcat, w_out, w_cq, g_x)


def _sample_attention(q_ref, k_ref, v_ref, o_ref):
    r_id = lax.broadcasted_iota(jnp.int32, (QROWS, KV_ROWS), 0)
    n_id = lax.broadcasted_iota(jnp.int32, (QROWS, KV_ROWS), 1)
    own = (n_id & 7) == (r_id >> 1) + 4 * (r_id & 1)
    low_half = (lax.broadcasted_iota(jnp.int32, (1, KV_ROWS), 1) & 4) == 0
    seqs = range(q_ref.shape[0])
    z = [_dot_nt(q_ref[j], k_ref[j]) for j in seqs]
    zc = [jnp.sum(jnp.where(own, z[j], 0.0), axis=0, keepdims=True) for j in seqs]
    yield
    other = [jnp.where(low_half, pltpu.roll(zc[j], KV_ROWS - 4, 1), pltpu.roll(zc[j], 4, 1))
             for j in seqs]
    s = [jnp.where(own, (zc[j] + other[j]) * (X_HEAD_DIM ** -0.5), -jnp.inf) for j in seqs]
    m = [jnp.max(s[j], axis=-1, keepdims=True) for j in seqs]
    e = [jnp.exp(s[j] - m[j]) for j in seqs]
    den = [jnp.sum(e[j], axis=-1, keepdims=True) for j in seqs]
    yield
    for j in seqs:
        o_ref[j] = _dot(e[j] / den[j], v_ref[j])
    yield


def _kv_rows(cache):
    B = cache.shape[0]
    c5 = cache.reshape(B, N_MEM, X_HEADS, 2, 128)
    return jnp.transpose(c5, (0, 1, 3, 2, 4)).reshape(B, KV_ROWS, 128)


def _kv_from_rows(rows):
    B = rows.shape[0]
    r5 = rows.reshape(B, N_MEM, 2, X_HEADS, 128)
    return jnp.transpose(r5, (0, 1, 3, 2, 4)).reshape(1, B, N_MEM, X_HEADS, X_HEAD_DIM)


def _rope_tables(pos):
    half = HEAD_DIM // 2
    inv = ROPE_THETA ** (-np.arange(half, dtype=np.float64) / half)
    ang = np.asarray(pos, np.float64)[:, None] * inv[None, :]
    cos, sin = np.cos(ang), np.sin(ang)
    return (np.concatenate([cos, cos], axis=-1).astype(np.float32),
            np.concatenate([-sin, sin], axis=-1).astype(np.float32))


def _retention_tables(L):
    lg = np.log1p(-np.exp2(-5.0 - np.arange(HEADS, dtype=np.float64)))
    t = np.arange(L, dtype=np.float64)
    diff = t[:, None] - t[None, :]
    decay = np.where(diff >= 0, np.exp(lg[:, None, None] * np.maximum(diff, 0.0)), 0.0)
    q_dec = np.exp(lg[:, None] * (t + 1.0))
    k_dec = np.exp(lg[:, None] * (L - 1.0 - t))
    chunk_dec = np.exp(lg * L)
    return tuple(a.astype(np.float32) for a in (decay, q_dec, k_dec, chunk_dec))


def _lanes(a, n):
    xp = np if isinstance(a, np.ndarray) else jnp
    return xp.broadcast_to(a[..., None], a.shape + (n,))


def _pad_rows(a, rows):
    return np.pad(a, ((0, rows - a.shape[0]),) + ((0, 0),) * (a.ndim - 1))


def kernel(x_prompt, x_sample, cache_mem_k, cache_mem_v, state_mlstm_conv, state_mlstm_C, state_mlstm_n, state_mlstm_m, state_ret_S, mem_prompt, w_in, b_gate, w_conv, b_conv, g_mix, g_mhead, g_rhead, w_out, g_xattn, g_mem, w_ck, w_cv, w_cq, w_co, g_ffn, w_gate, w_up, w_down, g_final):
    Bp, Tp, _ = x_prompt.shape
    Bs = x_sample.shape[0]
    l = 0
    n_m = 4 * MIX
    wi = w_in[l]
    w_gates = wi[:, n_m:n_m + 2 * HEADS]
    w_ret = wi[:, n_m + 2 * HEADS:]
    row = lambda a: a.reshape(1, -1)
    bf = lambda a: a.astype(BF16)
    g_mix_r, g_mh_r, g_rh_r = row(g_mix[l]), row(g_mhead[l]), row(g_rhead[l])
    g_x_r, g_mem_r, g_ffn_r, g_fin_r = row(g_xattn[l]), row(g_mem[l]), row(g_ffn[l]), row(g_final)
    b_conv_r = row(b_conv[l])
    w_out_b, w_cq_b, w_co_b = bf(w_out[l]), bf(w_cq[l]), bf(w_co[l])
    w_gate_b, w_up_b, w_down_b = bf(w_gate[l]), bf(w_up[l]), bf(w_down[l])

    mk_b, mv_b, mk_rows, mv_rows = _memkv(mem_prompt, g_mem_r, bf(w_ck[l]), bf(w_cv[l]))
    wn = bf(jnp.concatenate([wi[:, :2 * MIX], w_ret[:, :2 * MIX]], axis=1))
    wt = bf(jnp.concatenate([wi[:, 2 * MIX:n_m], w_ret[:, 2 * MIX:]], axis=1).T)
    gate_rows = lambda a: jnp.concatenate(
        [a[:HEADS], jnp.zeros((8 - HEADS,) + a.shape[1:], F32),
         a[HEADS:], jnp.zeros((8 - HEADS,) + a.shape[1:], F32)], axis=0)
    wgt = bf(gate_rows(w_gates.T))
    bg_rows = gate_rows(b_gate[l][:, None])
    bgt = jnp.broadcast_to(bg_rows, (GT_ROWS, CHUNK))
    cos_p, sin_p = _rope_tables(np.arange(Tp))
    decay, q_dec, k_dec, chunk_dec = _retention_tables(CHUNK)
    x1_p, conv_p, C_p, n_p, m_p, S_p = _mixer_prompt(
        x_prompt, wn, wt, wgt, w_out_b, g_mix_r, bgt, w_conv[l], b_conv_r,
        _lanes(g_mhead[l], CHUNK), _lanes(g_rhead[l], CHUNK), cos_p, sin_p,
        np.swapaxes(decay, 1, 2), _pad_rows(q_dec, 8), _lanes(k_dec, HEAD_DIM),
        _pad_rows(_lanes(chunk_dec, HEAD_DIM), 8))

    xs = x_sample.reshape(Bs, D_MODEL)
    un_s, ut_s, gs_s = _inproj_sample(xs, g_mix_r, wn, wt, wgt)
    cos_s, sin_s = _rope_tables(PAST_LEN + np.arange(1))
    decay1, q_dec1, k_dec1, chunk_dec1 = _retention_tables(1)
    rtab = np.zeros((8, HEAD_DIM), np.float32)
    rtab[:4, :HEADS] = np.stack([q_dec1[:, 0], k_dec1[:, 0], chunk_dec1, decay1[:, 0, 0]])
    hcat_s, conv_s, C_s, n_s, m_s, S_s = _mixer_sample(
        un_s, ut_s, gs_s, jnp.transpose(state_mlstm_conv[l], (1, 0, 2)), state_mlstm_C[l],
        state_mlstm_n[l].reshape(Bs, MIX), state_mlstm_m[l], state_ret_S[l],
        bg_rows.reshape(1, GT_ROWS), w_conv[l], b_conv_r, g_mh_r, g_rh_r, cos_s, sin_s, rtab)
    x1_s, q_s = _outq_sample(xs, hcat_s, w_out_b, w_cq_b, g_x_r)

    o_p, o_s = _attn_prompt(x1_p, mk_b, mv_b, w_cq_b, g_x_r,
                            q_s, _kv_rows(cache_mem_k[l]), _kv_rows(cache_mem_v[l]))
    y_p = _ffn(x1_p.reshape(Bp * Tp, D_MODEL), o_p.reshape(Bp * Tp, D_MODEL),
               w_co_b, w_gate_b, w_up_b, w_down_b, g_ffn_r, g_fin_r, tm=512)
    y_s = _ffn(x1_s, o_s, w_co_b, w_gate_b, w_up_b, w_down_b, g_ffn_r, g_fin_r, tm=Bs)

    return (y_p.reshape(Bp, Tp, D_MODEL), y_s.reshape(Bs, 1, D_MODEL),
            _kv_from_rows(mk_rows), _kv_from_rows(mv_rows),
            conv_p[None], C_p[None], n_p[None], m_p[None, :, :HEADS, 0], S_p[None],
            jnp.transpose(conv_s, (1, 0, 2))[None], C_s[None],
            n_s.reshape(1, Bs, HEADS, HEAD_DIM), m_s[None], S_s[None])
```

```python
import functools

import jax
import jax.numpy as jnp
import numpy as np
from jax import lax
from jax.experimental import pallas as pl
from jax.experimental.pallas import tpu as pltpu

F32 = jnp.float32
BF16 = jnp.bfloat16

D_MODEL = 1024
HEADS = 4
HEAD_DIM = 128
MIX = HEADS * HEAD_DIM
CONV_W = 4
CHUNK = 128
N_MEM = 256
X_HEADS = 4
X_HEAD_DIM = 256
D_FF = 2816
ROPE_THETA = 10000.0
EPS = 1e-6
PAST_LEN = 16384
KV_ROWS = N_MEM * 2 * X_HEADS
QROWS = 2 * X_HEADS

N_QK, N_RQ, N_RK, N_COLS = 0, 1024, 1536, 2048
T_V, T_O, T_RV, T_RG, T_ROWS = 0, 512, 1024, 1536, 2048
GT_ROWS = 16
UOFF = 8
ROW_GROUPS = 2

V7X_VMEM_LIMIT = 56 * 1024 * 1024

NT_DIMS = (((1,), (1,)), ((), ()))
TN_DIMS = (((0,), (0,)), ((), ()))


def _dot(a, b):
    return jnp.dot(a.astype(BF16), b.astype(BF16), preferred_element_type=F32)


def _dot_nt(a, b):
    return lax.dot_general(a.astype(BF16), b.astype(BF16), NT_DIMS, preferred_element_type=F32)


def _dot_tn(a, b):
    return lax.dot_general(a.astype(BF16), b.astype(BF16), TN_DIMS, preferred_element_type=F32)


def _rms(x, g):
    return x * lax.rsqrt(jnp.mean(x * x, axis=-1, keepdims=True) + EPS) * g


def _head_norm(h):
    return h * lax.rsqrt(jnp.mean(h * h, axis=-1, keepdims=True) + EPS)


def _silu(x):
    return x * jax.nn.sigmoid(x)


def _log_sigmoid(x):
    return jnp.minimum(x, 0.0) - jnp.log1p(jnp.exp(-jnp.abs(x)))


def _rope(x, cos_full, sin_signed):
    return x * cos_full + pltpu.roll(x, HEAD_DIM // 2, 1) * sin_signed


def _const_spec(shape):
    zeros = (0,) * len(shape)
    return pl.BlockSpec(shape, lambda *_: zeros, pipeline_mode=pl.Buffered(1))


def _params(*sem):
    return pltpu.CompilerParams(dimension_semantics=sem, vmem_limit_bytes=V7X_VMEM_LIMIT)


def _memkv_kernel(mem_ref, g_ref, wk_ref, wv_ref, kb_ref, vb_ref, krows_ref, vrows_ref):
    mn = _rms(mem_ref[...], g_ref[...]).astype(BF16)
    for w_ref, b_ref, rows_ref in ((wk_ref, kb_ref, krows_ref), (wv_ref, vb_ref, vrows_ref)):
        proj = jnp.dot(mn, w_ref[...], preferred_element_type=F32)
        b_ref[...] = proj.astype(BF16)
        for h in range(X_HEADS):
            for c in range(2):
                lane0 = h * X_HEAD_DIM + c * 128
                rows_ref[pl.ds(c * X_HEADS + h, N_MEM, stride=QROWS), :] = proj[:, lane0:lane0 + 128]


def _memkv(mem, g_mem, wk, wv):
    B = mem.shape[0]
    tok = pl.BlockSpec((None, N_MEM, D_MODEL), lambda b: (b, 0, 0))
    rows = pl.BlockSpec((None, KV_ROWS, 128), lambda b: (b, 0, 0))
    return pl.pallas_call(
        _memkv_kernel,
        grid=(B,),
        in_specs=[tok, _const_spec((1, D_MODEL)),
                  _const_spec((D_MODEL, D_MODEL)), _const_spec((D_MODEL, D_MODEL))],
        out_specs=[tok, tok, rows, rows],
        out_shape=[jax.ShapeDtypeStruct((B, N_MEM, D_MODEL), BF16)] * 2
        + [jax.ShapeDtypeStruct((B, KV_ROWS, 128), F32)] * 2,
        compiler_params=_params("parallel"),
    )(mem, g_mem, wk, wv)


def _mixer_prompt_kernel(x_ref, xs_ref, wn_ref, wt_ref, wgt_ref, wout_ref, gmix_ref, bgt_ref,
                         wconv_ref, bconv_ref, gmh_ref, grh_ref, cos_ref, sin_ref,
                         decay_ref, qdec_ref, kdec_ref, cdec_ref,
                         x1_ref, conv_ref, C_ref, n_ref, m_ref, S_ref,
                         un_a, ut_a, gt_a, un_b, ut_b, gt_b, tail_ref, st_ref, h_ref,
                         *, tt, tiles_per_seq):
    L = CHUNK
    s = pl.program_id(0)
    t = (jnp.maximum(s, 1) - 1) % tiles_per_seq
    chunks = range(tt // L)
    slot_a, slot_b = (un_a, ut_a, gt_a), (un_b, ut_b, gt_b)

    @pl.when(s == 0)
    def _():
        for ref in slot_b + (tail_ref,):
            ref[...] = jnp.zeros_like(ref)

    @pl.when(t == 0)
    def _():
        C_ref[...] = jnp.zeros_like(C_ref)
        st_ref[...] = jnp.zeros_like(st_ref)
        n_ref[...] = jnp.zeros_like(n_ref)
        m_ref[...] = jnp.zeros_like(m_ref)

    k_scale = HEAD_DIM ** -0.5

    def project(slot, xn, part):
        un_ref, ut_ref, gt_ref = slot
        c0 = part * MIX
        un_ref[UOFF:UOFF + tt, c0:c0 + MIX] = jnp.dot(xn, wn_ref[:, c0:c0 + MIX],
                                                      preferred_element_type=F32)
        res = lax.dot_general(wt_ref[c0:c0 + MIX, :], xn, NT_DIMS, preferred_element_type=F32)
        for c in chunks:
            ut_ref[c, c0:c0 + MIX, :] = res[:, c * L:(c + 1) * L]
        if part == 0:
            gates_t = lax.dot_general(wgt_ref[...], xn, NT_DIMS, preferred_element_type=F32)
            for c in chunks:
                gt_ref[c] = gates_t[:, c * L:(c + 1) * L]

    src_id = lax.broadcasted_iota(jnp.int32, (L, L), 0)
    tgt_id = lax.broadcasted_iota(jnp.int32, (L, L), 1)
    causal = src_id <= tgt_id
    triu_bf = jnp.where(causal, 1.0, 0.0).astype(BF16)
    heads = range(HEADS)
    hcol = lambda base, h: slice(base + h * HEAD_DIM, base + (h + 1) * HEAD_DIM)

    def chunk_body(c, slot):
        un_ref, ut_ref, gt_ref = slot
        r0 = c * L
        rows = pl.ds(UOFF + r0, L)
        trows = pl.ds(r0, L)
        g_t = gt_ref[c] + bgt_ref[...]
        ig = g_t[0:8]
        lf = _log_sigmoid(g_t[8:16])
        lf_hi = lf.astype(BF16)
        r1 = lf - lf_hi.astype(F32)
        lf_mid = r1.astype(BF16)
        lf_lo = (r1 - lf_mid.astype(F32)).astype(BF16)
        bc = (jnp.dot(lf_hi, triu_bf, preferred_element_type=F32)
              + jnp.dot(lf_mid, triu_bf, preferred_element_type=F32)
              + jnp.dot(lf_lo, triu_bf, preferred_element_type=F32))
        m_prev = m_ref[...]
        inter = bc + m_prev
        b_last = bc[:, L - 1:L]
        g_w = b_last - bc + ig
        m_new = jnp.maximum(b_last + m_prev, jnp.max(g_w, axis=1, keepdims=True))
        ws = jnp.exp(g_w - m_new)
        carry = jnp.exp(b_last + m_prev - m_new)
        a_n = jnp.concatenate([ig - bc, jnp.zeros((L - 8, L), F32)], axis=0).T

        def conv_act(col):
            win = un_ref[pl.ds(r0, L + UOFF), col:col + HEAD_DIM]
            acc = bconv_ref[:, col:col + HEAD_DIM]
            for j in range(CONV_W):
                back = CONV_W - 1 - j
                tap = win if back == 0 else pltpu.roll(win, back, 0)
                acc = acc + tap[UOFF:UOFF + L] * wconv_ref[j:j + 1, col:col + HEAD_DIM]
            return _silu(acc)

        cos_f, sin_s = cos_ref[trows, :], sin_ref[trows, :]
        q = [conv_act(N_QK + h * HEAD_DIM) for h in heads]
        kb = [(conv_act(N_QK + MIX + h * HEAD_DIM) * k_scale).astype(BF16) for h in heads]
        qb = [a.astype(BF16) for a in q]
        rqb = [_rope(un_ref[rows, hcol(N_RQ, h)], cos_f, sin_s).astype(BF16) for h in heads]
        rk = [_rope(un_ref[rows, hcol(N_RK, h)], cos_f, sin_s) * k_scale for h in heads]
        v_t = [ut_ref[c, hcol(T_V, h), :] for h in heads]
        rvb = [ut_ref[c, hcol(T_RV, h), :].astype(BF16) for h in heads]
        c_old = [C_ref[h] for h in heads]
        s_old = [st_ref[h] for h in heads]
        n_old = [n_ref[h:h + 1, :] for h in heads]
        qk = [_dot_nt(kb[h], qb[h]) for h in heads]
        att = [_dot_nt(rk[h], rqb[h]) for h in heads]
        c_q = [_dot_nt(c_old[h], qb[h]) for h in heads]
        s_q = [_dot_nt(s_old[h], rqb[h]) for h in heads]
        n_q = [_dot_nt(jnp.broadcast_to(n_old[h], (8, HEAD_DIM)), qb[h])[0:1] for h in heads]
        d_c = [_dot(v_t[h] * ws[h:h + 1, :], kb[h]) for h in heads]
        d_n = [_dot(jnp.broadcast_to(ws[h:h + 1, :], (8, L)), kb[h])[0:1] for h in heads]
        d_s = [_dot(rvb[h], rk[h] * kdec_ref[h]) for h in heads]
        m_t, w_in, wts = [], [], []
        for h in heads:
            dmat = a_n[:, h:h + 1] + bc[h:h + 1, :]
            dmat = jnp.where(causal, dmat, -jnp.inf)
            m_t.append(jnp.maximum(inter[h:h + 1, :], jnp.max(dmat, axis=0, keepdims=True)))
            wts.append(jnp.exp(dmat - m_t[h]) * qk[h])
            w_in.append(jnp.exp(inter[h:h + 1, :] - m_t[h]))
        att_w = [att[h] * decay_ref[h] for h in heads]
        v_p = [_dot(v_t[h], wts[h]) for h in heads]
        v_a = [_dot(rvb[h], att_w[h]) for h in heads]
        for h in heads:
            num = v_p[h] + w_in[h] * c_q[h]
            den = jnp.sum(wts[h], axis=0, keepdims=True) + w_in[h] * n_q[h]
            hm = num / jnp.maximum(jnp.abs(den), jnp.exp(-m_t[h]))
            hm = hm * lax.rsqrt(jnp.mean(hm * hm, axis=0, keepdims=True) + EPS)
            hm = hm * gmh_ref[hcol(0, h), :] * jax.nn.sigmoid(ut_ref[c, hcol(T_O, h), :])
            h_ref[trows, hcol(0, h)] = hm.T.astype(BF16)
            carry_h = carry[h:h + 1, :]
            C_ref[h] = carry_h * c_old[h] + d_c[h]
            n_ref[h:h + 1, :] = carry_h * n_old[h] + d_n[h]
            o = v_a[h] + qdec_ref[h:h + 1, :] * s_q[h]
            st_ref[h] = cdec_ref[h:h + 1, :] * s_old[h] + d_s[h]
            hr = o * lax.rsqrt(jnp.mean(o * o, axis=0, keepdims=True) + EPS)
            hr = hr * grh_ref[hcol(0, h), :] * _silu(ut_ref[c, hcol(T_RG, h), :])
            h_ref[trows, hcol(MIX, h)] = hr.T.astype(BF16)

        m_ref[...] = m_new

    def step(proj_slot, scan_slot):
        un_ref = scan_slot[0]
        un_ref[0:UOFF, N_QK:N_RQ] = jnp.where(t == 0, 0.0, tail_ref[...])
        xn = _rms(x_ref[...], gmix_ref[...]).astype(BF16)
        for c in chunks:
            project(proj_slot, xn, c)
            chunk_body(c, scan_slot)
        tail_ref[...] = un_ref[tt:tt + UOFF, N_QK:N_RQ]
        x1_ref[...] = xs_ref[...] + jnp.dot(h_ref[...], wout_ref[...], preferred_element_type=F32)

    @pl.when(s % 2 == 0)
    def _():
        step(slot_a, slot_b)

    @pl.when(s % 2 == 1)
    def _():
        step(slot_b, slot_a)

    @pl.when(t == tiles_per_seq - 1)
    def _():
        conv_ref[...] = tail_ref[UOFF - (CONV_W - 1):UOFF, :]
        for h in heads:
            S_ref[h] = st_ref[h].T


def _mixer_prompt(x, wn, wt, wgt, w_out, g_mix, bgt, w_conv, b_conv, gmh_cols, grh_cols, cos_f, sin_s,
                  decay_t, qdec_rows, kdec_cols, cdec_rows, tt=512):
    B, T, _ = x.shape
    tps = T // tt
    n_tiles = B * tps
    proj = lambda s: jnp.minimum(s, n_tiles - 1)
    scan = lambda s: jnp.maximum(s, 1) - 1
    per_b3 = lambda s: (scan(s) // tps, 0, 0)
    per_b4 = lambda s: (scan(s) // tps, 0, 0, 0)
    scan_tile = pl.BlockSpec((None, tt, D_MODEL), lambda s: (scan(s) // tps, scan(s) % tps, 0))
    in_specs = [
        pl.BlockSpec((None, tt, D_MODEL), lambda s: (proj(s) // tps, proj(s) % tps, 0)),
        scan_tile,
        _const_spec((D_MODEL, N_COLS)), _const_spec((T_ROWS, D_MODEL)), _const_spec((GT_ROWS, D_MODEL)),
        _const_spec((D_MODEL, D_MODEL)),
        _const_spec((1, D_MODEL)), _const_spec((GT_ROWS, CHUNK)),
        _const_spec((CONV_W, 2 * MIX)), _const_spec((1, 2 * MIX)),
        _const_spec((MIX, CHUNK)), _const_spec((MIX, CHUNK)),
        pl.BlockSpec((tt, HEAD_DIM), lambda s: (scan(s) % tps, 0)),
        pl.BlockSpec((tt, HEAD_DIM), lambda s: (scan(s) % tps, 0)),
        _const_spec((HEADS, CHUNK, CHUNK)),
        _const_spec((8, CHUNK)), _const_spec((HEADS, CHUNK, HEAD_DIM)), _const_spec((8, HEAD_DIM)),
    ]
    out_specs = [
        scan_tile,
        pl.BlockSpec((None, CONV_W - 1, 2 * MIX), per_b3),
        pl.BlockSpec((None, HEADS, HEAD_DIM, HEAD_DIM), per_b4),
        pl.BlockSpec((None, HEADS, HEAD_DIM), per_b3),
        pl.BlockSpec((None, 8, CHUNK), per_b3),
        pl.BlockSpec((None, HEADS, HEAD_DIM, HEAD_DIM), per_b4),
    ]
    out_shape = [
        jax.ShapeDtypeStruct((B, T, D_MODEL), F32),
        jax.ShapeDtypeStruct((B, CONV_W - 1, 2 * MIX), F32),
        jax.ShapeDtypeStruct((B, HEADS, HEAD_DIM, HEAD_DIM), F32),
        jax.ShapeDtypeStruct((B, HEADS, HEAD_DIM), F32),
        jax.ShapeDtypeStruct((B, 8, CHUNK), F32),
        jax.ShapeDtypeStruct((B, HEADS, HEAD_DIM, HEAD_DIM), F32),
    ]
    n_chunks = tt // CHUNK
    slot = [
        pltpu.VMEM((tt + UOFF, N_COLS), F32),
        pltpu.VMEM((n_chunks, T_ROWS, CHUNK), F32),
        pltpu.VMEM((n_chunks, GT_ROWS, CHUNK), F32),
    ]
    scratch = slot + slot + [
        pltpu.VMEM((UOFF, 2 * MIX), F32),
        pltpu.VMEM((HEADS, HEAD_DIM, HEAD_DIM), F32),
        pltpu.VMEM((tt, D_MODEL), BF16),
    ]
    return pl.pallas_call(
        functools.partial(_mixer_prompt_kernel, tt=tt, tiles_per_seq=tps),
        grid=(n_tiles + 1,),
        in_specs=in_specs, out_specs=out_specs, out_shape=out_shape,
        scratch_shapes=scratch,
        compiler_params=_params("arbitrary"),
    )(x, x, wn, wt, wgt, w_out, g_mix, bgt, w_conv, b_conv, gmh_cols, grh_cols, cos_f, sin_s,
      decay_t, qdec_rows, kdec_cols, cdec_rows)


def _attn_prompt_kernel(x1_ref, k_ref, v_ref, wcq_ref, gx_ref, o_ref):
    tm = x1_ref.shape[0]
    rows = [slice(r, r + tm // ROW_GROUPS) for r in range(0, tm, tm // ROW_GROUPS)]
    xq = [_rms(x1_ref[r, :], gx_ref[...]).astype(BF16) for r in rows]
    q = [jnp.dot(a, wcq_ref[...], preferred_element_type=F32).astype(BF16) for a in xq]
    sl = [slice(h * X_HEAD_DIM, (h + 1) * X_HEAD_DIM) for h in range(X_HEADS)]
    items = [(g, h) for g in range(ROW_GROUPS) for h in range(X_HEADS)]
    s = [_dot_nt(q[g][:, sl[h]], k_ref[:, sl[h]]) * (X_HEAD_DIM ** -0.5) for g, h in items]
    e = [jnp.exp(a - jnp.max(a, axis=-1, keepdims=True)) for a in s]
    p = [a / jnp.sum(a, axis=-1, keepdims=True) for a in e]
    for (g, h), a in zip(items, p):
        o_ref[rows[g], sl[h]] = _dot(a, v_ref[:, sl[h]]).astype(BF16)


def _attn_prompt(x1, mk, mv, w_cq, g_x, tm=512):
    B, T, _ = x1.shape
    tile = pl.BlockSpec((None, tm, D_MODEL), lambda b, t: (b, t, 0))
    kv = pl.BlockSpec((None, N_MEM, D_MODEL), lambda b, t: (b, 0, 0))
    return pl.pallas_call(
        _attn_prompt_kernel,
        grid=(B, T // tm),
        in_specs=[tile, kv, kv, _const_spec((D_MODEL, D_MODEL)), _const_spec((1, D_MODEL))],
        out_specs=tile,
        out_shape=jax.ShapeDtypeStruct((B, T, D_MODEL), BF16),
        compiler_params=_params("parallel", "parallel"),
    )(x1, mk, mv, w_cq, g_x)


V7X_MXU_DIM = 256
FF_CHUNKS = ((0, 6 * V7X_MXU_DIM), (6 * V7X_MXU_DIM, D_FF))


def _ffn_kernel(x1_ref, o_ref, wco_ref, wg_ref, wu_ref, wd_ref, gffn_ref, gfin_ref, *rest, n_groups):
    if len(rest) == 1:
        (y_ref,) = rest
        sample_stages = iter(())
    else:
        qs_ref, ck_ref, cv_ref, y_ref, os_ref = rest
        sample_stages = _sample_attention(qs_ref, ck_ref, cv_ref, os_ref)
    tm = x1_ref.shape[0]
    rows = [slice(r, r + tm // n_groups) for r in range(0, tm, tm // n_groups)]
    acc = [x1_ref[r, :] + _dot(o_ref[r, :], wco_ref[...]) for r in rows]
    hf = [_rms(a, gffn_ref[...]).astype(BF16) for a in acc]
    for c0, c1 in FF_CHUNKS:
        next(sample_stages, None)
        gate = [jnp.dot(h, wg_ref[:, c0:c1], preferred_element_type=F32) for h in hf]
        up = [jnp.dot(h, wu_ref[:, c0:c1], preferred_element_type=F32) for h in hf]
        act = [_silu(g) * u for g, u in zip(gate, up)]
        acc = [a + _dot(p, wd_ref[c0:c1, :]) for a, p in zip(acc, act)]
    next(sample_stages, None)
    for r, a in zip(rows, acc):
        y_ref[r, :] = _rms(a, gfin_ref[...])


def _ffn(x1, o, w_co, w_gate, w_up, w_down, g_ffn, g_final, tm, sample=None):
    rows = x1.shape[0]
    steps = rows // tm
    row_spec = pl.BlockSpec((tm, D_MODEL), lambda i: (i, 0))
    in_specs = [row_spec, row_spec, _const_spec((D_MODEL, D_MODEL)),
                _const_spec((D_MODEL, D_FF)), _const_spec((D_MODEL, D_FF)),
                _const_spec((D_FF, D_MODEL)), _const_spec((1, D_MODEL)), _const_spec((1, D_MODEL))]
    out_specs = [row_spec]
    out_shape = [jax.ShapeDtypeStruct((rows, D_MODEL), F32)]
    operands = [x1, o, w_co, w_gate, w_up, w_down, g_ffn, g_final]
    if sample is not None:
        q_s, ck_s, cv_s = sample
        Bs = q_s.shape[0]
        ba = Bs // steps
        assert ba * steps == Bs
        s_rows = pl.BlockSpec((ba, QROWS, 128), lambda i: (i, 0, 0))
        s_kv = pl.BlockSpec((ba, KV_ROWS, 128), lambda i: (i, 0, 0))
        in_specs += [s_rows, s_kv, s_kv]
        out_specs += [s_rows]
        out_shape += [jax.ShapeDtypeStruct((Bs, QROWS, 128), F32)]
        operands += [q_s.reshape(Bs, QROWS, 128), ck_s, cv_s]
    outs = pl.pallas_call(
        functools.partial(_ffn_kernel, n_groups=ROW_GROUPS if tm >= 512 else 1),
        grid=(steps,),
        in_specs=in_specs, out_specs=out_specs, out_shape=out_shape,
        compiler_params=_params("parallel"),
    )(*operands)
    if sample is None:
        return outs[0]
    return outs[0], outs[1].reshape(Bs, D_MODEL)


def _inproj_sample_kernel(x_ref, g_ref, wn_ref, wt_ref, wgt_ref, un_ref, ut_ref, gs_ref):
    xn = _rms(x_ref[...], g_ref[...]).astype(BF16)
    un_ref[...] = jnp.dot(xn, wn_ref[...], preferred_element_type=F32)
    ut_ref[...] = lax.dot_general(xn, wt_ref[...], NT_DIMS, preferred_element_type=F32)
    gs_ref[...] = lax.dot_general(xn, wgt_ref[...], NT_DIMS, preferred_element_type=F32)


def _inproj_sample(x, g_mix, wn, wt, wgt):
    rows = x.shape[0]
    full = lambda cols: pl.BlockSpec((rows, cols), lambda i: (0, 0))
    return pl.pallas_call(
        _inproj_sample_kernel,
        grid=(1,),
        in_specs=[_const_spec((rows, D_MODEL)), _const_spec((1, D_MODEL)),
                  _const_spec((D_MODEL, N_COLS)), _const_spec((T_ROWS, D_MODEL)),
                  _const_spec((GT_ROWS, D_MODEL))],
        out_specs=[full(N_COLS), full(T_ROWS), full(GT_ROWS)],
        out_shape=[jax.ShapeDtypeStruct((rows, N_COLS), F32),
                   jax.ShapeDtypeStruct((rows, T_ROWS), F32),
                   jax.ShapeDtypeStruct((rows, GT_ROWS), F32)],
        compiler_params=_params("arbitrary"),
    )(x, g_mix, wn, wt, wgt)


def _mixer_sample_kernel(un_ref, ut_ref, gs_ref, conv_ref, C_ref, n_ref, m_ref, S_ref, bg_ref,
                         wconv_ref, bconv_ref, gmh_ref, grh_ref, cos_ref, sin_ref, rtab_ref,
                         h_ref, convo_ref, Co_ref, no_ref, mo_ref, So_ref,
                         q_s, k_s, vws_s, qr_s, kr_s, qc_s, qs_s, carry_s, *, bb):
    k_scale = HEAD_DIM ** -0.5
    uqk = un_ref[:, N_QK:N_RQ]
    conv = (bconv_ref[...] + wconv_ref[0:1, :] * conv_ref[0] + wconv_ref[1:2, :] * conv_ref[1]
            + wconv_ref[2:3, :] * conv_ref[2] + wconv_ref[3:4, :] * uqk)
    convo_ref[0] = conv_ref[1]
    convo_ref[1] = conv_ref[2]
    convo_ref[2] = uqk
    qk_act = _silu(conv)
    q_s[...] = qk_act[:, 0:MIX]
    k_s[...] = qk_act[:, MIX:2 * MIX] * k_scale

    gates = gs_ref[...] + bg_ref[...]
    ig = gates[:, 0:HEADS]
    lf = _log_sigmoid(gates[:, 8:8 + HEADS])
    inter = lf + m_ref[...]
    m_t = jnp.maximum(inter, ig)
    ws = jnp.exp(ig - m_t)
    w_in = jnp.exp(inter - m_t)
    mo_ref[...] = m_t
    carry_s[:, 0:HEADS] = w_in

    cos_f = cos_ref[...]
    sin_s = sin_ref[...]
    for h in range(HEADS):
        lo = h * HEAD_DIM
        hs = slice(lo, lo + HEAD_DIM)
        vws_s[:, hs] = ut_ref[:, T_V + lo:T_V + lo + HEAD_DIM] * ws[:, h:h + 1]
        qr_s[:, hs] = _rope(un_ref[:, N_RQ + lo:N_RQ + lo + HEAD_DIM], cos_f, sin_s)
        kr_s[:, hs] = _rope(un_ref[:, N_RK + lo:N_RK + lo + HEAD_DIM], cos_f, sin_s) * k_scale

    heads = range(HEADS)
    seqs = range(bb)
    hsl = [slice(h * HEAD_DIM, (h + 1) * HEAD_DIM) for h in heads]
    seq_id = lax.broadcasted_iota(jnp.int32, (bb, HEAD_DIM), 0)
    for h in heads:
        q_h, k_h, vws_h = q_s[:, hsl[h]], k_s[:, hsl[h]], vws_s[:, hsl[h]]
        qr_h = qr_s[:, hsl[h]]
        krd_h = kr_s[:, hsl[h]] * rtab_ref[1:2, h:h + 1]
        rv_h = ut_ref[:, T_RV + h * HEAD_DIM:T_RV + (h + 1) * HEAD_DIM]
        c_old = [C_ref[b, h] for b in seqs]
        s_old = [S_ref[b, h] for b in seqs]
        q_c = [_dot_nt(q_h, c_old[b]) for b in seqs]
        q_st = [_dot(qr_h, s_old[b]) for b in seqs]
        d_c = [_dot_tn(jnp.where(seq_id == b, vws_h, 0.0), k_h) for b in seqs]
        d_s = [_dot_tn(jnp.where(seq_id == b, krd_h, 0.0), rv_h) for b in seqs]
        for b in seqs:
            row = slice(b, b + 1)
            qc_s[row, hsl[h]] = q_c[b][row]
            qs_s[row, hsl[h]] = q_st[b][row]
            Co_ref[b, h] = carry_s[row, h:h + 1] * c_old[b] + d_c[b]
            So_ref[b, h] = rtab_ref[2:3, h:h + 1] * s_old[b] + d_s[b]

    for h in range(HEADS):
        lo = h * HEAD_DIM
        hs = slice(lo, lo + HEAD_DIM)
        q_h, k_h, n_h = q_s[:, hs], k_s[:, hs], n_ref[:, hs]
        ws_h, w_in_h = ws[:, h:h + 1], w_in[:, h:h + 1]
        wts = ws_h * jnp.sum(q_h * k_h, axis=1, keepdims=True)
        num = wts * ut_ref[:, T_V + lo:T_V + lo + HEAD_DIM] + w_in_h * qc_s[:, hs]
        den = wts + w_in_h * jnp.sum(n_h * q_h, axis=1, keepdims=True)
        hm = num / jnp.maximum(jnp.abs(den), jnp.exp(-m_t[:, h:h + 1]))
        mo = ut_ref[:, T_O + lo:T_O + lo + HEAD_DIM]
        h_ref[:, hs] = _head_norm(hm) * gmh_ref[:, hs] * jax.nn.sigmoid(mo)
        no_ref[:, hs] = w_in_h * n_h + ws_h * k_h

        att = jnp.sum(qr_s[:, hs] * kr_s[:, hs], axis=1, keepdims=True) * rtab_ref[3:4, h:h + 1]
        o = (att * ut_ref[:, T_RV + lo:T_RV + lo + HEAD_DIM]
             + rtab_ref[0:1, h:h + 1] * qs_s[:, hs])
        rg = ut_ref[:, T_RG + lo:T_RG + lo + HEAD_DIM]
        h_ref[:, MIX + lo:MIX + lo + HEAD_DIM] = _head_norm(o) * grh_ref[:, hs] * _silu(rg)


def _mixer_sample(un, ut, gs, conv_t, C, n2, m, S, bg, w_conv, b_conv, g_mh, g_rh, cos_f, sin_s, rtab,
                  bb=8):
    B = un.shape[0]
    rows = lambda cols: pl.BlockSpec((bb, cols), lambda i: (i, 0))
    mats = pl.BlockSpec((bb, HEADS, HEAD_DIM, HEAD_DIM), lambda i: (i, 0, 0, 0))
    convs = pl.BlockSpec((CONV_W - 1, bb, 2 * MIX), lambda i: (0, i, 0))
    in_specs = [rows(N_COLS), rows(T_ROWS), rows(GT_ROWS), convs, mats, rows(MIX), rows(HEADS), mats,
                _const_spec((1, GT_ROWS)), _const_spec((CONV_W, 2 * MIX)), _const_spec((1, 2 * MIX)),
                _const_spec((1, MIX)), _const_spec((1, MIX)),
                _const_spec((1, HEAD_DIM)), _const_spec((1, HEAD_DIM)), _const_spec((8, HEAD_DIM))]
    out_specs = [rows(D_MODEL), convs, mats, rows(MIX), rows(HEADS), mats]
    out_shape = [jax.ShapeDtypeStruct((B, D_MODEL), F32),
                 jax.ShapeDtypeStruct((CONV_W - 1, B, 2 * MIX), F32),
                 jax.ShapeDtypeStruct((B, HEADS, HEAD_DIM, HEAD_DIM), F32),
                 jax.ShapeDtypeStruct((B, MIX), F32),
                 jax.ShapeDtypeStruct((B, HEADS), F32),
                 jax.ShapeDtypeStruct((B, HEADS, HEAD_DIM, HEAD_DIM), F32)]
    scratch = [pltpu.VMEM((bb, MIX), F32) for _ in range(7)] + [pltpu.VMEM((bb, HEAD_DIM), F32)]
    return pl.pallas_call(
        functools.partial(_mixer_sample_kernel, bb=bb),
        grid=(B // bb,),
        in_specs=in_specs, out_specs=out_specs, out_shape=out_shape,
        scratch_shapes=scratch,
        compiler_params=_params("parallel"),
    )(un, ut, gs, conv_t, C, n2, m, S, bg, w_conv, b_conv, g_mh, g_rh, cos_f, sin_s, rtab)


def _outq_sample_kernel(x_ref, h_ref, wout_ref, wcq_ref, gx_ref, x1_ref, q_ref):
    x1 = x_ref[...] + _dot(h_ref[...], wout_ref[...])
    x1_ref[...] = x1
    q_ref[...] = _dot(_rms(x1, gx_ref[...]), wcq_ref[...])


def _outq_sample(x, hcat, w_out, w_cq, g_x):
    rows = x.shape[0]
    full = _const_spec((rows, D_MODEL))
    return pl.pallas_call(
        _outq_sample_kernel,
        grid=(1,),
        in_specs=[full, full, _const_spec((D_MODEL, D_MODEL)), _const_spec((D_MODEL, D_MODEL)),
                  _const_spec((1, D_MODEL))],
        out_specs=[pl.BlockSpec((rows, D_MODEL), lambda i: (0, 0))] * 2,
        out_shape=[jax.ShapeDtypeStruct((rows, D_MODEL), F32)] * 2,
        compiler_params=_params("arbitrary"),
    )(x, hcat, w_out, w_cq, g_x)


def _sample_attention(q_ref, k_ref, v_ref, o_ref):
    r_id = lax.broadcasted_iota(jnp.int32, (QROWS, KV_ROWS), 0)
    n_id = lax.broadcasted_iota(jnp.int32, (QROWS, KV_ROWS), 1)
    own = (n_id & 7) == (r_id >> 1) + 4 * (r_id & 1)
    low_half = (lax.broadcasted_iota(jnp.int32, (1, KV_ROWS), 1) & 4) == 0
    seqs = range(q_ref.shape[0])
    z = [_dot_nt(q_ref[j], k_ref[j]) for j in seqs]
    zc = [jnp.sum(jnp.where(own, z[j], 0.0), axis=0, keepdims=True) for j in seqs]
    yield
    other = [jnp.where(low_half, pltpu.roll(zc[j], KV_ROWS - 4, 1), pltpu.roll(zc[j], 4, 1))
             for j in seqs]
    s = [jnp.where(own, (zc[j] + other[j]) * (X_HEAD_DIM ** -0.5), -jnp.inf) for j in seqs]
    m = [jnp.max(s[j], axis=-1, keepdims=True) for j in seqs]
    e = [jnp.exp(s[j] - m[j]) for j in seqs]
    den = [jnp.sum(e[j], axis=-1, keepdims=True) for j in seqs]
    yield
    for j in seqs:
        o_ref[j] = _dot(e[j] / den[j], v_ref[j])
    yield


def _kv_rows(cache):
    B = cache.shape[0]
    c5 = cache.reshape(B, N_MEM, X_HEADS, 2, 128)
    return jnp.transpose(c5, (0, 1, 3, 2, 4)).reshape(B, KV_ROWS, 128)


def _kv_from_rows(rows):
    B = rows.shape[0]
    r5 = rows.reshape(B, N_MEM, 2, X_HEADS, 128)
    return jnp.transpose(r5, (0, 1, 3, 2, 4)).reshape(1, B, N_MEM, X_HEADS, X_HEAD_DIM)


def _rope_tables(pos):
    half = HEAD_DIM // 2
    inv = ROPE_THETA ** (-np.arange(half, dtype=np.float64) / half)
    ang = np.asarray(pos, np.float64)[:, None] * inv[None, :]
    cos, sin = np.cos(ang), np.sin(ang)
    return (np.concatenate([cos, cos], axis=-1).astype(np.float32),
            np.concatenate([-sin, sin], axis=-1).astype(np.float32))


def _retention_tables(L):
    lg = np.log1p(-np.exp2(-5.0 - np.arange(HEADS, dtype=np.float64)))
    t = np.arange(L, dtype=np.float64)
    diff = t[:, None] - t[None, :]
    decay = np.where(diff >= 0, np.exp(lg[:, None, None] * np.maximum(diff, 0.0)), 0.0)
    q_dec = np.exp(lg[:, None] * (t + 1.0))
    k_dec = np.exp(lg[:, None] * (L - 1.0 - t))
    chunk_dec = np.exp(lg * L)
    return tuple(a.astype(np.float32) for a in (decay, q_dec, k_dec, chunk_dec))


def _lanes(a, n):
    xp = np if isinstance(a, np.ndarray) else jnp
    return xp.broadcast_to(a[..., None], a.shape + (n,))


def _pad_rows(a, rows):
    return np.pad(a, ((0, rows - a.shape[0]),) + ((0, 0),) * (a.ndim - 1))


def kernel(x_prompt, x_sample, cache_mem_k, cache_mem_v, state_mlstm_conv, state_mlstm_C, state_mlstm_n, state_mlstm_m, state_ret_S, mem_prompt, w_in, b_gate, w_conv, b_conv, g_mix, g_mhead, g_rhead, w_out, g_xattn, g_mem, w_ck, w_cv, w_cq, w_co, g_ffn, w_gate, w_up, w_down, g_final):
    Bp, Tp, _ = x_prompt.shape
    Bs = x_sample.shape[0]
    l = 0
    n_m = 4 * MIX
    wi = w_in[l]
    w_gates = wi[:, n_m:n_m + 2 * HEADS]
    w_ret = wi[:, n_m + 2 * HEADS:]
    row = lambda a: a.reshape(1, -1)
    bf = lambda a: a.astype(BF16)
    g_mix_r, g_mh_r, g_rh_r = row(g_mix[l]), row(g_mhead[l]), row(g_rhead[l])
    g_x_r, g_mem_r, g_ffn_r, g_fin_r = row(g_xattn[l]), row(g_mem[l]), row(g_ffn[l]), row(g_final)
    b_conv_r = row(b_conv[l])
    w_out_b, w_cq_b, w_co_b = bf(w_out[l]), bf(w_cq[l]), bf(w_co[l])
    w_gate_b, w_up_b, w_down_b = bf(w_gate[l]), bf(w_up[l]), bf(w_down[l])

    mk_b, mv_b, mk_rows, mv_rows = _memkv(mem_prompt, g_mem_r, bf(w_ck[l]), bf(w_cv[l]))
    wn = bf(jnp.concatenate([wi[:, :2 * MIX], w_ret[:, :2 * MIX]], axis=1))
    wt = bf(jnp.concatenate([wi[:, 2 * MIX:n_m], w_ret[:, 2 * MIX:]], axis=1).T)
    gate_rows = lambda a: jnp.concatenate(
        [a[:HEADS], jnp.zeros((8 - HEADS,) + a.shape[1:], F32),
         a[HEADS:], jnp.zeros((8 - HEADS,) + a.shape[1:], F32)], axis=0)
    wgt = bf(gate_rows(w_gates.T))
    bg_rows = gate_rows(b_gate[l][:, None])
    bgt = jnp.broadcast_to(bg_rows, (GT_ROWS, CHUNK))
    cos_p, sin_p = _rope_tables(np.arange(Tp))
    decay, q_dec, k_dec, chunk_dec = _retention_tables(CHUNK)
    x1_p, conv_p, C_p, n_p, m_p, S_p = _mixer_prompt(
        x_prompt, wn, wt, wgt, w_out_b, g_mix_r, bgt, w_conv[l], b_conv_r,
        _lanes(g_mhead[l], CHUNK), _lanes(g_rhead[l], CHUNK), cos_p, sin_p,
        np.swapaxes(decay, 1, 2), _pad_rows(q_dec, 8), _lanes(k_dec, HEAD_DIM),
        _pad_rows(_lanes(chunk_dec, HEAD_DIM), 8))

    xs = x_sample.reshape(Bs, D_MODEL)
    un_s, ut_s, gs_s = _inproj_sample(xs, g_mix_r, wn, wt, wgt)
    cos_s, sin_s = _rope_tables(PAST_LEN + np.arange(1))
    decay1, q_dec1, k_dec1, chunk_dec1 = _retention_tables(1)
    rtab = np.zeros((8, HEAD_DIM), np.float32)
    rtab[:4, :HEADS] = np.stack([q_dec1[:, 0], k_dec1[:, 0], chunk_dec1, decay1[:, 0, 0]])
    hcat_s, conv_s, C_s, n_s, m_s, S_s = _mixer_sample(
        un_s, ut_s, gs_s, jnp.transpose(state_mlstm_conv[l], (1, 0, 2)), state_mlstm_C[l],
        state_mlstm_n[l].reshape(Bs, MIX), state_mlstm_m[l], state_ret_S[l],
        bg_rows.reshape(1, GT_ROWS), w_conv[l], b_conv_r, g_mh_r, g_rh_r, cos_s, sin_s, rtab)
    x1_s, q_s = _outq_sample(xs, hcat_s, w_out_b, w_cq_b, g_x_r)

    o_p = _attn_prompt(x1_p, mk_b, mv_b, w_cq_b, g_x_r)
    y_p, o_s = _ffn(x1_p.reshape(Bp * Tp, D_MODEL), o_p.reshape(Bp * Tp, D_MODEL),
                    w_co_b, w_gate_b, w_up_b, w_down_b, g_ffn_r, g_fin_r, tm=512,
                    sample=(q_s, _kv_rows(cache_mem_k[l]), _kv_rows(cache_mem_v[l])))
    y_s = _ffn(x1_s, o_s, w_co_b, w_gate_b, w_up_b, w_down_b, g_ffn_r, g_fin_r, tm=Bs)

    return (y_p.reshape(Bp, Tp, D_MODEL), y_s.reshape(Bs, 1, D_MODEL),
            _kv_from_rows(mk_rows), _kv_from_rows(mv_rows),
            conv_p[None], C_p[None], n_p[None], m_p[None, :, :HEADS, 0], S_p[None],
            jnp.transpose(conv_s, (1, 0, 2))[None], C_s[None],
            n_s.reshape(1, Bs, HEADS, HEAD_DIM), m_s[None], S_s[None])
```

```python
import functools

import jax
import jax.numpy as jnp
import numpy as np
from jax import lax
from jax.experimental import pallas as pl
from jax.experimental.pallas import tpu as pltpu

F32 = jnp.float32
BF16 = jnp.bfloat16

D_MODEL = 1024
HEADS = 4
HEAD_DIM = 128
MIX = HEADS * HEAD_DIM
CONV_W = 4
CHUNK = 128
N_MEM = 256
X_HEADS = 4
X_HEAD_DIM = 256
D_FF = 2816
ROPE_THETA = 10000.0
EPS = 1e-6
PAST_LEN = 16384
KV_ROWS = N_MEM * 2 * X_HEADS
QROWS = 2 * X_HEADS

N_QK, N_RQ, N_RK, N_COLS = 0, 1024, 1536, 2048
T_V, T_O, T_RV, T_RG, T_ROWS = 0, 512, 1024, 1536, 2048
GT_ROWS = 16
UOFF = 8
ROW_GROUPS = 2

V7X_VMEM_LIMIT = 56 * 1024 * 1024

NT_DIMS = (((1,), (1,)), ((), ()))
TN_DIMS = (((0,), (0,)), ((), ()))


def _dot(a, b):
    return jnp.dot(a.astype(BF16), b.astype(BF16), preferred_element_type=F32)


def _dot_nt(a, b):
    return lax.dot_general(a.astype(BF16), b.astype(BF16), NT_DIMS, preferred_element_type=F32)


def _dot_tn(a, b):
    return lax.dot_general(a.astype(BF16), b.astype(BF16), TN_DIMS, preferred_element_type=F32)


def _rms(x, g):
    return x * lax.rsqrt(jnp.mean(x * x, axis=-1, keepdims=True) + EPS) * g


def _head_norm(h):
    return h * lax.rsqrt(jnp.mean(h * h, axis=-1, keepdims=True) + EPS)


def _silu(x):
    return x * jax.nn.sigmoid(x)


def _log_sigmoid(x):
    return jnp.minimum(x, 0.0) - jnp.log1p(jnp.exp(-jnp.abs(x)))


def _rope(x, cos_full, sin_signed):
    return x * cos_full + pltpu.roll(x, HEAD_DIM // 2, 1) * sin_signed


def _const_spec(shape):
    zeros = (0,) * len(shape)
    return pl.BlockSpec(shape, lambda *_: zeros, pipeline_mode=pl.Buffered(1))


def _params(*sem):
    return pltpu.CompilerParams(dimension_semantics=sem, vmem_limit_bytes=V7X_VMEM_LIMIT)


def _memkv_kernel(mem_ref, g_ref, wk_ref, wv_ref, kb_ref, vb_ref, krows_ref, vrows_ref):
    mn = _rms(mem_ref[...], g_ref[...]).astype(BF16)
    for w_ref, b_ref, rows_ref in ((wk_ref, kb_ref, krows_ref), (wv_ref, vb_ref, vrows_ref)):
        proj = jnp.dot(mn, w_ref[...], preferred_element_type=F32)
        b_ref[...] = proj.astype(BF16)
        for h in range(X_HEADS):
            for c in range(2):
                lane0 = h * X_HEAD_DIM + c * 128
                rows_ref[pl.ds(c * X_HEADS + h, N_MEM, stride=QROWS), :] = proj[:, lane0:lane0 + 128]


def _memkv(mem, g_mem, wk, wv):
    B = mem.shape[0]
    tok = pl.BlockSpec((None, N_MEM, D_MODEL), lambda b: (b, 0, 0))
    rows = pl.BlockSpec((None, KV_ROWS, 128), lambda b: (b, 0, 0))
    return pl.pallas_call(
        _memkv_kernel,
        grid=(B,),
        in_specs=[tok, _const_spec((1, D_MODEL)),
                  _const_spec((D_MODEL, D_MODEL)), _const_spec((D_MODEL, D_MODEL))],
        out_specs=[tok, tok, rows, rows],
        out_shape=[jax.ShapeDtypeStruct((B, N_MEM, D_MODEL), BF16)] * 2
        + [jax.ShapeDtypeStruct((B, KV_ROWS, 128), F32)] * 2,
        compiler_params=_params("parallel"),
    )(mem, g_mem, wk, wv)


def _mixer_prompt_kernel(x_ref, xs_ref, wn_ref, wt_ref, wgt_ref, wout_ref, gmix_ref, bgt_ref,
                         wconv_ref, bconv_ref, gmh_ref, grh_ref, cos_ref, sin_ref,
                         decay_ref, qdec_ref, kdec_ref, cdec_ref,
                         x1_ref, conv_ref, C_ref, n_ref, m_ref, S_ref,
                         un_a, ut_a, gt_a, un_b, ut_b, gt_b, tail_ref, st_ref, h_ref,
                         *, tt, tiles_per_seq):
    L = CHUNK
    s = pl.program_id(0)
    t = (jnp.maximum(s, 1) - 1) % tiles_per_seq
    chunks = range(tt // L)
    slot_a, slot_b = (un_a, ut_a, gt_a), (un_b, ut_b, gt_b)

    @pl.when(s == 0)
    def _():
        for ref in slot_b + (tail_ref,):
            ref[...] = jnp.zeros_like(ref)

    @pl.when(t == 0)
    def _():
        C_ref[...] = jnp.zeros_like(C_ref)
        st_ref[...] = jnp.zeros_like(st_ref)
        n_ref[...] = jnp.zeros_like(n_ref)
        m_ref[...] = jnp.zeros_like(m_ref)

    k_scale = HEAD_DIM ** -0.5

    def project(slot, xn, part):
        un_ref, ut_ref, gt_ref = slot
        c0 = part * MIX
        un_ref[UOFF:UOFF + tt, c0:c0 + MIX] = jnp.dot(xn, wn_ref[:, c0:c0 + MIX],
                                                      preferred_element_type=F32)
        res = lax.dot_general(wt_ref[c0:c0 + MIX, :], xn, NT_DIMS, preferred_element_type=F32)
        for c in chunks:
            ut_ref[c, c0:c0 + MIX, :] = res[:, c * L:(c + 1) * L]
        if part == 0:
            gates_t = lax.dot_general(wgt_ref[...], xn, NT_DIMS, preferred_element_type=F32)
            for c in chunks:
                gt_ref[c] = gates_t[:, c * L:(c + 1) * L]

    src_id = lax.broadcasted_iota(jnp.int32, (L, L), 0)
    tgt_id = lax.broadcasted_iota(jnp.int32, (L, L), 1)
    causal = src_id <= tgt_id
    triu_bf = jnp.where(causal, 1.0, 0.0).astype(BF16)
    heads = range(HEADS)
    hcol = lambda base, h: slice(base + h * HEAD_DIM, base + (h + 1) * HEAD_DIM)

    def chunk_body(c, slot):
        un_ref, ut_ref, gt_ref = slot
        r0 = c * L
        rows = pl.ds(UOFF + r0, L)
        trows = pl.ds(r0, L)
        g_t = gt_ref[c] + bgt_ref[...]
        ig = g_t[0:8]
        lf = _log_sigmoid(g_t[8:16])
        lf_hi = lf.astype(BF16)
        r1 = lf - lf_hi.astype(F32)
        lf_mid = r1.astype(BF16)
        lf_lo = (r1 - lf_mid.astype(F32)).astype(BF16)
        bc = (jnp.dot(lf_hi, triu_bf, preferred_element_type=F32)
              + jnp.dot(lf_mid, triu_bf, preferred_element_type=F32)
              + jnp.dot(lf_lo, triu_bf, preferred_element_type=F32))
        m_prev = m_ref[...]
        inter = bc + m_prev
        b_last = bc[:, L - 1:L]
        g_w = b_last - bc + ig
        m_new = jnp.maximum(b_last + m_prev, jnp.max(g_w, axis=1, keepdims=True))
        ws = jnp.exp(g_w - m_new)
        carry = jnp.exp(b_last + m_prev - m_new)
        a_n = jnp.concatenate([ig - bc, jnp.zeros((L - 8, L), F32)], axis=0).T

        def conv_act(col):
            win = un_ref[pl.ds(r0, L + UOFF), col:col + HEAD_DIM]
            acc = bconv_ref[:, col:col + HEAD_DIM]
            for j in range(CONV_W):
                back = CONV_W - 1 - j
                tap = win if back == 0 else pltpu.roll(win, back, 0)
                acc = acc + tap[UOFF:UOFF + L] * wconv_ref[j:j + 1, col:col + HEAD_DIM]
            return _silu(acc)

        cos_f, sin_s = cos_ref[trows, :], sin_ref[trows, :]
        q = [conv_act(N_QK + h * HEAD_DIM) for h in heads]
        kb = [(conv_act(N_QK + MIX + h * HEAD_DIM) * k_scale).astype(BF16) for h in heads]
        qb = [a.astype(BF16) for a in q]
        rqb = [_rope(un_ref[rows, hcol(N_RQ, h)], cos_f, sin_s).astype(BF16) for h in heads]
        rk = [_rope(un_ref[rows, hcol(N_RK, h)], cos_f, sin_s) * k_scale for h in heads]
        v_t = [ut_ref[c, hcol(T_V, h), :] for h in heads]
        rvb = [ut_ref[c, hcol(T_RV, h), :].astype(BF16) for h in heads]
        c_old = [C_ref[h] for h in heads]
        s_old = [st_ref[h] for h in heads]
        n_old = [n_ref[h:h + 1, :] for h in heads]
        qk = [_dot_nt(kb[h], qb[h]) for h in heads]
        att = [_dot_nt(rk[h], rqb[h]) for h in heads]
        c_q = [_dot_nt(c_old[h], qb[h]) for h in heads]
        s_q = [_dot_nt(s_old[h], rqb[h]) for h in heads]
        n_q = [_dot_nt(jnp.broadcast_to(n_old[h], (8, HEAD_DIM)), qb[h])[0:1] for h in heads]
        d_c = [_dot(v_t[h] * ws[h:h + 1, :], kb[h]) for h in heads]
        d_n = [_dot(jnp.broadcast_to(ws[h:h + 1, :], (8, L)), kb[h])[0:1] for h in heads]
        d_s = [_dot(rvb[h], rk[h] * kdec_ref[h]) for h in heads]
        m_t, w_in, wts = [], [], []
        for h in heads:
            dmat = a_n[:, h:h + 1] + bc[h:h + 1, :]
            dmat = jnp.where(causal, dmat, -jnp.inf)
            m_t.append(jnp.maximum(inter[h:h + 1, :], jnp.max(dmat, axis=0, keepdims=True)))
            wts.append(jnp.exp(dmat - m_t[h]) * qk[h])
            w_in.append(jnp.exp(inter[h:h + 1, :] - m_t[h]))
        att_w = [att[h] * decay_ref[h] for h in heads]
        v_p = [_dot(v_t[h], wts[h]) for h in heads]
        v_a = [_dot(rvb[h], att_w[h]) for h in heads]
        for h in heads:
            num = v_p[h] + w_in[h] * c_q[h]
            den = jnp.sum(wts[h], axis=0, keepdims=True) + w_in[h] * n_q[h]
            hm = num / jnp.maximum(jnp.abs(den), jnp.exp(-m_t[h]))
            hm = hm * lax.rsqrt(jnp.mean(hm * hm, axis=0, keepdims=True) + EPS)
            hm = hm * gmh_ref[hcol(0, h), :] * jax.nn.sigmoid(ut_ref[c, hcol(T_O, h), :])
            h_ref[trows, hcol(0, h)] = hm.T.astype(BF16)
            carry_h = carry[h:h + 1, :]
            C_ref[h] = carry_h * c_old[h] + d_c[h]
            n_ref[h:h + 1, :] = carry_h * n_old[h] + d_n[h]
            o = v_a[h] + qdec_ref[h:h + 1, :] * s_q[h]
            st_ref[h] = cdec_ref[h:h + 1, :] * s_old[h] + d_s[h]
            hr = o * lax.rsqrt(jnp.mean(o * o, axis=0, keepdims=True) + EPS)
            hr = hr * grh_ref[hcol(0, h), :] * _silu(ut_ref[c, hcol(T_RG, h), :])
            h_ref[trows, hcol(MIX, h)] = hr.T.astype(BF16)

        m_ref[...] = m_new

    def step(proj_slot, scan_slot):
        un_ref = scan_slot[0]
        un_ref[0:UOFF, N_QK:N_RQ] = jnp.where(t == 0, 0.0, tail_ref[...])
        xn = _rms(x_ref[...], gmix_ref[...]).astype(BF16)
        for c in chunks:
            project(proj_slot, xn, c)
            chunk_body(c, scan_slot)
        tail_ref[...] = un_ref[tt:tt + UOFF, N_QK:N_RQ]
        x1_ref[...] = xs_ref[...] + jnp.dot(h_ref[...], wout_ref[...], preferred_element_type=F32)

    @pl.when(s % 2 == 0)
    def _():
        step(slot_a, slot_b)

    @pl.when(s % 2 == 1)
    def _():
        step(slot_b, slot_a)

    @pl.when(t == tiles_per_seq - 1)
    def _():
        conv_ref[...] = tail_ref[UOFF - (CONV_W - 1):UOFF, :]
        for h in heads:
            S_ref[h] = st_ref[h].T


def _mixer_prompt(x, wn, wt, wgt, w_out, g_mix, bgt, w_conv, b_conv, gmh_cols, grh_cols, cos_f, sin_s,
                  decay_t, qdec_rows, kdec_cols, cdec_rows, tt=512):
    B, T, _ = x.shape
    tps = T // tt
    n_tiles = B * tps
    proj = lambda s: jnp.minimum(s, n_tiles - 1)
    scan = lambda s: jnp.maximum(s, 1) - 1
    per_b3 = lambda s: (scan(s) // tps, 0, 0)
    per_b4 = lambda s: (scan(s) // tps, 0, 0, 0)
    scan_tile = pl.BlockSpec((None, tt, D_MODEL), lambda s: (scan(s) // tps, scan(s) % tps, 0))
    in_specs = [
        pl.BlockSpec((None, tt, D_MODEL), lambda s: (proj(s) // tps, proj(s) % tps, 0)),
        scan_tile,
        _const_spec((D_MODEL, N_COLS)), _const_spec((T_ROWS, D_MODEL)), _const_spec((GT_ROWS, D_MODEL)),
        _const_spec((D_MODEL, D_MODEL)),
        _const_spec((1, D_MODEL)), _const_spec((GT_ROWS, CHUNK)),
        _const_spec((CONV_W, 2 * MIX)), _const_spec((1, 2 * MIX)),
        _const_spec((MIX, CHUNK)), _const_spec((MIX, CHUNK)),
        pl.BlockSpec((tt, HEAD_DIM), lambda s: (scan(s) % tps, 0)),
        pl.BlockSpec((tt, HEAD_DIM), lambda s: (scan(s) % tps, 0)),
        _const_spec((HEADS, CHUNK, CHUNK)),
        _const_spec((8, CHUNK)), _const_spec((HEADS, CHUNK, HEAD_DIM)), _const_spec((8, HEAD_DIM)),
    ]
    out_specs = [
        scan_tile,
        pl.BlockSpec((None, CONV_W - 1, 2 * MIX), per_b3),
        pl.BlockSpec((None, HEADS, HEAD_DIM, HEAD_DIM), per_b4),
        pl.BlockSpec((None, HEADS, HEAD_DIM), per_b3),
        pl.BlockSpec((None, 8, CHUNK), per_b3),
        pl.BlockSpec((None, HEADS, HEAD_DIM, HEAD_DIM), per_b4),
    ]
    out_shape = [
        jax.ShapeDtypeStruct((B, T, D_MODEL), F32),
        jax.ShapeDtypeStruct((B, CONV_W - 1, 2 * MIX), F32),
        jax.ShapeDtypeStruct((B, HEADS, HEAD_DIM, HEAD_DIM), F32),
        jax.ShapeDtypeStruct((B, HEADS, HEAD_DIM), F32),
        jax.ShapeDtypeStruct((B, 8, CHUNK), F32),
        jax.ShapeDtypeStruct((B, HEADS, HEAD_DIM, HEAD_DIM), F32),
    ]
    n_chunks = tt // CHUNK
    slot = [
        pltpu.VMEM((tt + UOFF, N_COLS), F32),
        pltpu.VMEM((n_chunks, T_ROWS, CHUNK), F32),
        pltpu.VMEM((n_chunks, GT_ROWS, CHUNK), F32),
    ]
    scratch = slot + slot + [
        pltpu.VMEM((UOFF, 2 * MIX), F32),
        pltpu.VMEM((HEADS, HEAD_DIM, HEAD_DIM), F32),
        pltpu.VMEM((tt, D_MODEL), BF16),
    ]
    return pl.pallas_call(
        functools.partial(_mixer_prompt_kernel, tt=tt, tiles_per_seq=tps),
        grid=(n_tiles + 1,),
        in_specs=in_specs, out_specs=out_specs, out_shape=out_shape,
        scratch_shapes=scratch,
        compiler_params=_params("arbitrary"),
    )(x, x, wn, wt, wgt, w_out, g_mix, bgt, w_conv, b_conv, gmh_cols, grh_cols, cos_f, sin_s,
      decay_t, qdec_rows, kdec_cols, cdec_rows)


def _attn_prompt_kernel(x1_ref, k_ref, v_ref, wcq_ref, gx_ref, *rest, n_sample_steps):
    sample_in, o_ref = rest[:SAMPLE_MIXER_INPUTS], rest[SAMPLE_MIXER_INPUTS]
    sample_out_and_scratch = rest[SAMPLE_MIXER_INPUTS + 1:]
    step = pl.program_id(0) * pl.num_programs(1) + pl.program_id(1)

    @pl.when(step < n_sample_steps)
    def _():
        _mixer_sample_kernel(*sample_in, *sample_out_and_scratch, bb=SAMPLE_BLOCK)

    tm = x1_ref.shape[0]
    rows = [slice(r, r + tm // ROW_GROUPS) for r in range(0, tm, tm // ROW_GROUPS)]
    xq = [_rms(x1_ref[r, :], gx_ref[...]).astype(BF16) for r in rows]
    q = [jnp.dot(a, wcq_ref[...], preferred_element_type=F32).astype(BF16) for a in xq]
    sl = [slice(h * X_HEAD_DIM, (h + 1) * X_HEAD_DIM) for h in range(X_HEADS)]
    items = [(g, h) for g in range(ROW_GROUPS) for h in range(X_HEADS)]
    s = [_dot_nt(q[g][:, sl[h]], k_ref[:, sl[h]]) * (X_HEAD_DIM ** -0.5) for g, h in items]
    e = [jnp.exp(a - jnp.max(a, axis=-1, keepdims=True)) for a in s]
    p = [a / jnp.sum(a, axis=-1, keepdims=True) for a in e]
    for (g, h), a in zip(items, p):
        o_ref[rows[g], sl[h]] = _dot(a, v_ref[:, sl[h]]).astype(BF16)


def _attn_prompt(x1, mk, mv, w_cq, g_x, sample_mixer_operands, tm=512):
    B, T, _ = x1.shape
    tps = T // tm
    Bs = sample_mixer_operands[0].shape[0]
    n_sample_steps = Bs // SAMPLE_BLOCK
    assert n_sample_steps <= B * tps
    block_of = lambda b, t: jnp.minimum(b * tps + t, n_sample_steps - 1)
    s_in, s_out, s_shape, s_scratch = _sample_mixer_specs(Bs, block_of)
    tile = pl.BlockSpec((None, tm, D_MODEL), lambda b, t: (b, t, 0))
    kv = pl.BlockSpec((None, N_MEM, D_MODEL), lambda b, t: (b, 0, 0))
    outs = pl.pallas_call(
        functools.partial(_attn_prompt_kernel, n_sample_steps=n_sample_steps),
        grid=(B, tps),
        in_specs=[tile, kv, kv, _const_spec((D_MODEL, D_MODEL)), _const_spec((1, D_MODEL))] + s_in,
        out_specs=[tile] + s_out,
        out_shape=[jax.ShapeDtypeStruct((B, T, D_MODEL), BF16)] + s_shape,
        scratch_shapes=s_scratch,
        compiler_params=_params("arbitrary", "arbitrary"),
    )(x1, mk, mv, w_cq, g_x, *sample_mixer_operands)
    return outs[0], outs[1:]


V7X_MXU_DIM = 256
FF_CHUNKS = ((0, 6 * V7X_MXU_DIM), (6 * V7X_MXU_DIM, D_FF))


def _ffn_kernel(x1_ref, o_ref, wco_ref, wg_ref, wu_ref, wd_ref, gffn_ref, gfin_ref, *rest, n_groups):
    if len(rest) == 1:
        (y_ref,) = rest
        sample_stages = iter(())
    else:
        qs_ref, ck_ref, cv_ref, y_ref, os_ref = rest
        sample_stages = _sample_attention(qs_ref, ck_ref, cv_ref, os_ref)
    tm = x1_ref.shape[0]
    rows = [slice(r, r + tm // n_groups) for r in range(0, tm, tm // n_groups)]
    acc = [x1_ref[r, :] + _dot(o_ref[r, :], wco_ref[...]) for r in rows]
    hf = [_rms(a, gffn_ref[...]).astype(BF16) for a in acc]
    for c0, c1 in FF_CHUNKS:
        next(sample_stages, None)
        gate = [jnp.dot(h, wg_ref[:, c0:c1], preferred_element_type=F32) for h in hf]
        up = [jnp.dot(h, wu_ref[:, c0:c1], preferred_element_type=F32) for h in hf]
        act = [_silu(g) * u for g, u in zip(gate, up)]
        acc = [a + _dot(p, wd_ref[c0:c1, :]) for a, p in zip(acc, act)]
    next(sample_stages, None)
    for r, a in zip(rows, acc):
        y_ref[r, :] = _rms(a, gfin_ref[...])


def _ffn(x1, o, w_co, w_gate, w_up, w_down, g_ffn, g_final, tm, sample=None):
    rows = x1.shape[0]
    steps = rows // tm
    row_spec = pl.BlockSpec((tm, D_MODEL), lambda i: (i, 0))
    in_specs = [row_spec, row_spec, _const_spec((D_MODEL, D_MODEL)),
                _const_spec((D_MODEL, D_FF)), _const_spec((D_MODEL, D_FF)),
                _const_spec((D_FF, D_MODEL)), _const_spec((1, D_MODEL)), _const_spec((1, D_MODEL))]
    out_specs = [row_spec]
    out_shape = [jax.ShapeDtypeStruct((rows, D_MODEL), F32)]
    operands = [x1, o, w_co, w_gate, w_up, w_down, g_ffn, g_final]
    if sample is not None:
        q_s, ck_s, cv_s = sample
        Bs = q_s.shape[0]
        ba = Bs // steps
        assert ba * steps == Bs
        s_rows = pl.BlockSpec((ba, QROWS, 128), lambda i: (i, 0, 0))
        s_kv = pl.BlockSpec((ba, KV_ROWS, 128), lambda i: (i, 0, 0))
        in_specs += [s_rows, s_kv, s_kv]
        out_specs += [s_rows]
        out_shape += [jax.ShapeDtypeStruct((Bs, QROWS, 128), F32)]
        operands += [q_s.reshape(Bs, QROWS, 128), ck_s, cv_s]
    outs = pl.pallas_call(
        functools.partial(_ffn_kernel, n_groups=ROW_GROUPS if tm >= 512 else 1),
        grid=(steps,),
        in_specs=in_specs, out_specs=out_specs, out_shape=out_shape,
        compiler_params=_params("parallel"),
    )(*operands)
    if sample is None:
        return outs[0]
    return outs[0], outs[1].reshape(Bs, D_MODEL)


def _inproj_sample_kernel(x_ref, g_ref, wn_ref, wt_ref, wgt_ref, un_ref, ut_ref, gs_ref):
    xn = _rms(x_ref[...], g_ref[...]).astype(BF16)
    un_ref[...] = jnp.dot(xn, wn_ref[...], preferred_element_type=F32)
    ut_ref[...] = lax.dot_general(xn, wt_ref[...], NT_DIMS, preferred_element_type=F32)
    gs_ref[...] = lax.dot_general(xn, wgt_ref[...], NT_DIMS, preferred_element_type=F32)


def _inproj_sample(x, g_mix, wn, wt, wgt):
    rows = x.shape[0]
    full = lambda cols: pl.BlockSpec((rows, cols), lambda i: (0, 0))
    return pl.pallas_call(
        _inproj_sample_kernel,
        grid=(1,),
        in_specs=[_const_spec((rows, D_MODEL)), _const_spec((1, D_MODEL)),
                  _const_spec((D_MODEL, N_COLS)), _const_spec((T_ROWS, D_MODEL)),
                  _const_spec((GT_ROWS, D_MODEL))],
        out_specs=[full(N_COLS), full(T_ROWS), full(GT_ROWS)],
        out_shape=[jax.ShapeDtypeStruct((rows, N_COLS), F32),
                   jax.ShapeDtypeStruct((rows, T_ROWS), F32),
                   jax.ShapeDtypeStruct((rows, GT_ROWS), F32)],
        compiler_params=_params("arbitrary"),
    )(x, g_mix, wn, wt, wgt)


def _mixer_sample_kernel(un_ref, ut_ref, gs_ref, conv_ref, C_ref, n_ref, m_ref, S_ref, bg_ref,
                         wconv_ref, bconv_ref, gmh_ref, grh_ref, cos_ref, sin_ref, rtab_ref,
                         h_ref, convo_ref, Co_ref, no_ref, mo_ref, So_ref,
                         q_s, k_s, vws_s, qr_s, kr_s, qc_s, qs_s, carry_s, *, bb):
    k_scale = HEAD_DIM ** -0.5
    uqk = un_ref[:, N_QK:N_RQ]
    conv = (bconv_ref[...] + wconv_ref[0:1, :] * conv_ref[0] + wconv_ref[1:2, :] * conv_ref[1]
            + wconv_ref[2:3, :] * conv_ref[2] + wconv_ref[3:4, :] * uqk)
    convo_ref[0] = conv_ref[1]
    convo_ref[1] = conv_ref[2]
    convo_ref[2] = uqk
    qk_act = _silu(conv)
    q_s[...] = qk_act[:, 0:MIX]
    k_s[...] = qk_act[:, MIX:2 * MIX] * k_scale

    gates = gs_ref[...] + bg_ref[...]
    ig = gates[:, 0:HEADS]
    lf = _log_sigmoid(gates[:, 8:8 + HEADS])
    inter = lf + m_ref[...]
    m_t = jnp.maximum(inter, ig)
    ws = jnp.exp(ig - m_t)
    w_in = jnp.exp(inter - m_t)
    mo_ref[...] = m_t
    carry_s[:, 0:HEADS] = w_in

    cos_f = cos_ref[...]
    sin_s = sin_ref[...]
    for h in range(HEADS):
        lo = h * HEAD_DIM
        hs = slice(lo, lo + HEAD_DIM)
        vws_s[:, hs] = ut_ref[:, T_V + lo:T_V + lo + HEAD_DIM] * ws[:, h:h + 1]
        qr_s[:, hs] = _rope(un_ref[:, N_RQ + lo:N_RQ + lo + HEAD_DIM], cos_f, sin_s)
        kr_s[:, hs] = _rope(un_ref[:, N_RK + lo:N_RK + lo + HEAD_DIM], cos_f, sin_s) * k_scale

    heads = range(HEADS)
    seqs = range(bb)
    hsl = [slice(h * HEAD_DIM, (h + 1) * HEAD_DIM) for h in heads]
    seq_id = lax.broadcasted_iota(jnp.int32, (bb, HEAD_DIM), 0)
    for h in heads:
        q_h, k_h, vws_h = q_s[:, hsl[h]], k_s[:, hsl[h]], vws_s[:, hsl[h]]
        qr_h = qr_s[:, hsl[h]]
        krd_h = kr_s[:, hsl[h]] * rtab_ref[1:2, h:h + 1]
        rv_h = ut_ref[:, T_RV + h * HEAD_DIM:T_RV + (h + 1) * HEAD_DIM]
        c_old = [C_ref[b, h] for b in seqs]
        s_old = [S_ref[b, h] for b in seqs]
        q_c = [_dot_nt(q_h, c_old[b]) for b in seqs]
        q_st = [_dot(qr_h, s_old[b]) for b in seqs]
        d_c = [_dot_tn(jnp.where(seq_id == b, vws_h, 0.0), k_h) for b in seqs]
        d_s = [_dot_tn(jnp.where(seq_id == b, krd_h, 0.0), rv_h) for b in seqs]
        for b in seqs:
            row = slice(b, b + 1)
            qc_s[row, hsl[h]] = q_c[b][row]
            qs_s[row, hsl[h]] = q_st[b][row]
            Co_ref[b, h] = carry_s[row, h:h + 1] * c_old[b] + d_c[b]
            So_ref[b, h] = rtab_ref[2:3, h:h + 1] * s_old[b] + d_s[b]

    for h in range(HEADS):
        lo = h * HEAD_DIM
        hs = slice(lo, lo + HEAD_DIM)
        q_h, k_h, n_h = q_s[:, hs], k_s[:, hs], n_ref[:, hs]
        ws_h, w_in_h = ws[:, h:h + 1], w_in[:, h:h + 1]
        wts = ws_h * jnp.sum(q_h * k_h, axis=1, keepdims=True)
        num = wts * ut_ref[:, T_V + lo:T_V + lo + HEAD_DIM] + w_in_h * qc_s[:, hs]
        den = wts + w_in_h * jnp.sum(n_h * q_h, axis=1, keepdims=True)
        hm = num / jnp.maximum(jnp.abs(den), jnp.exp(-m_t[:, h:h + 1]))
        mo = ut_ref[:, T_O + lo:T_O + lo + HEAD_DIM]
        h_ref[:, hs] = _head_norm(hm) * gmh_ref[:, hs] * jax.nn.sigmoid(mo)
        no_ref[:, hs] = w_in_h * n_h + ws_h * k_h

        att = jnp.sum(qr_s[:, hs] * kr_s[:, hs], axis=1, keepdims=True) * rtab_ref[3:4, h:h + 1]
        o = (att * ut_ref[:, T_RV + lo:T_RV + lo + HEAD_DIM]
             + rtab_ref[0:1, h:h + 1] * qs_s[:, hs])
        rg = ut_ref[:, T_RG + lo:T_RG + lo + HEAD_DIM]
        h_ref[:, MIX + lo:MIX + lo + HEAD_DIM] = _head_norm(o) * grh_ref[:, hs] * _silu(rg)


SAMPLE_BLOCK = 8
SAMPLE_MIXER_INPUTS = 16


def _sample_mixer_specs(B, block_of):
    bb = SAMPLE_BLOCK
    rows = lambda cols: pl.BlockSpec((bb, cols), lambda *g: (block_of(*g), 0))
    mats = pl.BlockSpec((bb, HEADS, HEAD_DIM, HEAD_DIM), lambda *g: (block_of(*g), 0, 0, 0))
    convs = pl.BlockSpec((CONV_W - 1, bb, 2 * MIX), lambda *g: (0, block_of(*g), 0))
    in_specs = [rows(N_COLS), rows(T_ROWS), rows(GT_ROWS), convs, mats, rows(MIX), rows(HEADS), mats,
                _const_spec((1, GT_ROWS)), _const_spec((CONV_W, 2 * MIX)), _const_spec((1, 2 * MIX)),
                _const_spec((1, MIX)), _const_spec((1, MIX)),
                _const_spec((1, HEAD_DIM)), _const_spec((1, HEAD_DIM)), _const_spec((8, HEAD_DIM))]
    assert len(in_specs) == SAMPLE_MIXER_INPUTS
    out_specs = [rows(D_MODEL), convs, mats, rows(MIX), rows(HEADS), mats]
    out_shape = [jax.ShapeDtypeStruct((B, D_MODEL), F32),
                 jax.ShapeDtypeStruct((CONV_W - 1, B, 2 * MIX), F32),
                 jax.ShapeDtypeStruct((B, HEADS, HEAD_DIM, HEAD_DIM), F32),
                 jax.ShapeDtypeStruct((B, MIX), F32),
                 jax.ShapeDtypeStruct((B, HEADS), F32),
                 jax.ShapeDtypeStruct((B, HEADS, HEAD_DIM, HEAD_DIM), F32)]
    scratch = [pltpu.VMEM((bb, MIX), F32) for _ in range(7)] + [pltpu.VMEM((bb, HEAD_DIM), F32)]
    return in_specs, out_specs, out_shape, scratch


def _outq_sample_kernel(x_ref, h_ref, wout_ref, wcq_ref, gx_ref, x1_ref, q_ref):
    x1 = x_ref[...] + _dot(h_ref[...], wout_ref[...])
    x1_ref[...] = x1
    q_ref[...] = _dot(_rms(x1, gx_ref[...]), wcq_ref[...])


def _outq_sample(x, hcat, w_out, w_cq, g_x):
    rows = x.shape[0]
    full = _const_spec((rows, D_MODEL))
    return pl.pallas_call(
        _outq_sample_kernel,
        grid=(1,),
        in_specs=[full, full, _const_spec((D_MODEL, D_MODEL)), _const_spec((D_MODEL, D_MODEL)),
                  _const_spec((1, D_MODEL))],
        out_specs=[pl.BlockSpec((rows, D_MODEL), lambda i: (0, 0))] * 2,
        out_shape=[jax.ShapeDtypeStruct((rows, D_MODEL), F32)] * 2,
        compiler_params=_params("arbitrary"),
    )(x, hcat, w_out, w_cq, g_x)


def _sample_attention(q_ref, k_ref, v_ref, o_ref):
    r_id = lax.broadcasted_iota(jnp.int32, (QROWS, KV_ROWS), 0)
    n_id = lax.broadcasted_iota(jnp.int32, (QROWS, KV_ROWS), 1)
    own = (n_id & 7) == (r_id >> 1) + 4 * (r_id & 1)
    low_half = (lax.broadcasted_iota(jnp.int32, (1, KV_ROWS), 1) & 4) == 0
    seqs = range(q_ref.shape[0])
    z = [_dot_nt(q_ref[j], k_ref[j]) for j in seqs]
    zc = [jnp.sum(jnp.where(own, z[j], 0.0), axis=0, keepdims=True) for j in seqs]
    yield
    other = [jnp.where(low_half, pltpu.roll(zc[j], KV_ROWS - 4, 1), pltpu.roll(zc[j], 4, 1))
             for j in seqs]
    s = [jnp.where(own, (zc[j] + other[j]) * (X_HEAD_DIM ** -0.5), -jnp.inf) for j in seqs]
    m = [jnp.max(s[j], axis=-1, keepdims=True) for j in seqs]
    e = [jnp.exp(s[j] - m[j]) for j in seqs]
    den = [jnp.sum(e[j], axis=-1, keepdims=True) for j in seqs]
    yield
    for j in seqs:
        o_ref[j] = _dot(e[j] / den[j], v_ref[j])
    yield


def _kv_rows(cache):
    B = cache.shape[0]
    c5 = cache.reshape(B, N_MEM, X_HEADS, 2, 128)
    return jnp.transpose(c5, (0, 1, 3, 2, 4)).reshape(B, KV_ROWS, 128)


def _kv_from_rows(rows):
    B = rows.shape[0]
    r5 = rows.reshape(B, N_MEM, 2, X_HEADS, 128)
    return jnp.transpose(r5, (0, 1, 3, 2, 4)).reshape(1, B, N_MEM, X_HEADS, X_HEAD_DIM)


def _rope_tables(pos):
    half = HEAD_DIM // 2
    inv = ROPE_THETA ** (-np.arange(half, dtype=np.float64) / half)
    ang = np.asarray(pos, np.float64)[:, None] * inv[None, :]
    cos, sin = np.cos(ang), np.sin(ang)
    return (np.concatenate([cos, cos], axis=-1).astype(np.float32),
            np.concatenate([-sin, sin], axis=-1).astype(np.float32))


def _retention_tables(L):
    lg = np.log1p(-np.exp2(-5.0 - np.arange(HEADS, dtype=np.float64)))
    t = np.arange(L, dtype=np.float64)
    diff = t[:, None] - t[None, :]
    decay = np.where(diff >= 0, np.exp(lg[:, None, None] * np.maximum(diff, 0.0)), 0.0)
    q_dec = np.exp(lg[:, None] * (t + 1.0))
    k_dec = np.exp(lg[:, None] * (L - 1.0 - t))
    chunk_dec = np.exp(lg * L)
    return tuple(a.astype(np.float32) for a in (decay, q_dec, k_dec, chunk_dec))


def _lanes(a, n):
    xp = np if isinstance(a, np.ndarray) else jnp
    return xp.broadcast_to(a[..., None], a.shape + (n,))


def _pad_rows(a, rows):
    return np.pad(a, ((0, rows - a.shape[0]),) + ((0, 0),) * (a.ndim - 1))


def kernel(x_prompt, x_sample, cache_mem_k, cache_mem_v, state_mlstm_conv, state_mlstm_C, state_mlstm_n, state_mlstm_m, state_ret_S, mem_prompt, w_in, b_gate, w_conv, b_conv, g_mix, g_mhead, g_rhead, w_out, g_xattn, g_mem, w_ck, w_cv, w_cq, w_co, g_ffn, w_gate, w_up, w_down, g_final):
    Bp, Tp, _ = x_prompt.shape
    Bs = x_sample.shape[0]
    l = 0
    n_m = 4 * MIX
    wi = w_in[l]
    w_gates = wi[:, n_m:n_m + 2 * HEADS]
    w_ret = wi[:, n_m + 2 * HEADS:]
    row = lambda a: a.reshape(1, -1)
    bf = lambda a: a.astype(BF16)
    g_mix_r, g_mh_r, g_rh_r = row(g_mix[l]), row(g_mhead[l]), row(g_rhead[l])
    g_x_r, g_mem_r, g_ffn_r, g_fin_r = row(g_xattn[l]), row(g_mem[l]), row(g_ffn[l]), row(g_final)
    b_conv_r = row(b_conv[l])
    w_out_b, w_cq_b, w_co_b = bf(w_out[l]), bf(w_cq[l]), bf(w_co[l])
    w_gate_b, w_up_b, w_down_b = bf(w_gate[l]), bf(w_up[l]), bf(w_down[l])

    mk_b, mv_b, mk_rows, mv_rows = _memkv(mem_prompt, g_mem_r, bf(w_ck[l]), bf(w_cv[l]))
    wn = bf(jnp.concatenate([wi[:, :2 * MIX], w_ret[:, :2 * MIX]], axis=1))
    wt = bf(jnp.concatenate([wi[:, 2 * MIX:n_m], w_ret[:, 2 * MIX:]], axis=1).T)
    gate_rows = lambda a: jnp.concatenate(
        [a[:HEADS], jnp.zeros((8 - HEADS,) + a.shape[1:], F32),
         a[HEADS:], jnp.zeros((8 - HEADS,) + a.shape[1:], F32)], axis=0)
    wgt = bf(gate_rows(w_gates.T))
    bg_rows = gate_rows(b_gate[l][:, None])
    bgt = jnp.broadcast_to(bg_rows, (GT_ROWS, CHUNK))
    cos_p, sin_p = _rope_tables(np.arange(Tp))
    decay, q_dec, k_dec, chunk_dec = _retention_tables(CHUNK)
    x1_p, conv_p, C_p, n_p, m_p, S_p = _mixer_prompt(
        x_prompt, wn, wt, wgt, w_out_b, g_mix_r, bgt, w_conv[l], b_conv_r,
        _lanes(g_mhead[l], CHUNK), _lanes(g_rhead[l], CHUNK), cos_p, sin_p,
        np.swapaxes(decay, 1, 2), _pad_rows(q_dec, 8), _lanes(k_dec, HEAD_DIM),
        _pad_rows(_lanes(chunk_dec, HEAD_DIM), 8))

    xs = x_sample.reshape(Bs, D_MODEL)
    un_s, ut_s, gs_s = _inproj_sample(xs, g_mix_r, wn, wt, wgt)
    cos_s, sin_s = _rope_tables(PAST_LEN + np.arange(1))
    decay1, q_dec1, k_dec1, chunk_dec1 = _retention_tables(1)
    rtab = np.zeros((8, HEAD_DIM), np.float32)
    rtab[:4, :HEADS] = np.stack([q_dec1[:, 0], k_dec1[:, 0], chunk_dec1, decay1[:, 0, 0]])
    sample_mixer_operands = (
        un_s, ut_s, gs_s, jnp.transpose(state_mlstm_conv[l], (1, 0, 2)), state_mlstm_C[l],
        state_mlstm_n[l].reshape(Bs, MIX), state_mlstm_m[l], state_ret_S[l],
        bg_rows.reshape(1, GT_ROWS), w_conv[l], b_conv_r, g_mh_r, g_rh_r, cos_s, sin_s, rtab)

    o_p, (hcat_s, conv_s, C_s, n_s, m_s, S_s) = _attn_prompt(
        x1_p, mk_b, mv_b, w_cq_b, g_x_r, sample_mixer_operands)
    x1_s, q_s = _outq_sample(xs, hcat_s, w_out_b, w_cq_b, g_x_r)
    y_p, o_s = _ffn(x1_p.reshape(Bp * Tp, D_MODEL), o_p.reshape(Bp * Tp, D_MODEL),
                    w_co_b, w_gate_b, w_up_b, w_down_b, g_ffn_r, g_fin_r, tm=512,
                    sample=(q_s, _kv_rows(cache_mem_k[l]), _kv_rows(cache_mem_v[l])))
    y_s = _ffn(x1_s, o_s, w_co_b, w_gate_b, w_up_b, w_down_b, g_ffn_r, g_fin_r, tm=Bs)

    return (y_p.reshape(Bp, Tp, D_MODEL), y_s.reshape(Bs, 1, D_MODEL),
            _kv_from_rows(mk_rows), _kv_from_rows(mv_rows),
            conv_p[None], C_p[None], n_p[None], m_p[None, :, :HEADS, 0], S_p[None],
            jnp.transpose(conv_s, (1, 0, 2))[None], C_s[None],
            n_s.reshape(1, Bs, HEADS, HEAD_DIM), m_s[None], S_s[None])
```

```python
import functools

import jax
import jax.numpy as jnp
import numpy as np
from jax import lax
from jax.experimental import pallas as pl
from jax.experimental.pallas import tpu as pltpu

F32 = jnp.float32
BF16 = jnp.bfloat16

D_MODEL = 1024
HEADS = 4
HEAD_DIM = 128
MIX = HEADS * HEAD_DIM
CONV_W = 4
CHUNK = 128
N_MEM = 256
X_HEADS = 4
X_HEAD_DIM = 256
D_FF = 2816
ROPE_THETA = 10000.0
EPS = 1e-6
PAST_LEN = 16384
KV_ROWS = N_MEM * 2 * X_HEADS
QROWS = 2 * X_HEADS

N_QK, N_RQ, N_RK, N_COLS = 0, 1024, 1536, 2048
T_V, T_O, T_RV, T_RG, T_ROWS = 0, 512, 1024, 1536, 2048
GT_ROWS = 16
UOFF = 8
ROW_GROUPS = 2

V7X_VMEM_LIMIT = 56 * 1024 * 1024

NT_DIMS = (((1,), (1,)), ((), ()))
TN_DIMS = (((0,), (0,)), ((), ()))


def _dot(a, b):
    return jnp.dot(a.astype(BF16), b.astype(BF16), preferred_element_type=F32)


def _dot_nt(a, b):
    return lax.dot_general(a.astype(BF16), b.astype(BF16), NT_DIMS, preferred_element_type=F32)


def _dot_tn(a, b):
    return lax.dot_general(a.astype(BF16), b.astype(BF16), TN_DIMS, preferred_element_type=F32)


def _rms(x, g):
    return x * lax.rsqrt(jnp.mean(x * x, axis=-1, keepdims=True) + EPS) * g


def _head_norm(h):
    return h * lax.rsqrt(jnp.mean(h * h, axis=-1, keepdims=True) + EPS)


def _silu(x):
    return x * jax.nn.sigmoid(x)


def _log_sigmoid(x):
    return jnp.minimum(x, 0.0) - jnp.log1p(jnp.exp(-jnp.abs(x)))


def _rope(x, cos_full, sin_signed):
    return x * cos_full + pltpu.roll(x, HEAD_DIM // 2, 1) * sin_signed


def _const_spec(shape):
    zeros = (0,) * len(shape)
    return pl.BlockSpec(shape, lambda *_: zeros, pipeline_mode=pl.Buffered(1))


def _params(*sem):
    return pltpu.CompilerParams(dimension_semantics=sem, vmem_limit_bytes=V7X_VMEM_LIMIT)


def _memkv_kernel(mem_ref, g_ref, wk_ref, wv_ref, kb_ref, vb_ref, krows_ref, vrows_ref):
    mn = _rms(mem_ref[...], g_ref[...]).astype(BF16)
    for w_ref, b_ref, rows_ref in ((wk_ref, kb_ref, krows_ref), (wv_ref, vb_ref, vrows_ref)):
        proj = jnp.dot(mn, w_ref[...], preferred_element_type=F32)
        b_ref[...] = proj.astype(BF16)
        for h in range(X_HEADS):
            for c in range(2):
                lane0 = h * X_HEAD_DIM + c * 128
                rows_ref[pl.ds(c * X_HEADS + h, N_MEM, stride=QROWS), :] = proj[:, lane0:lane0 + 128]


def _memkv(mem, g_mem, wk, wv):
    B = mem.shape[0]
    tok = pl.BlockSpec((None, N_MEM, D_MODEL), lambda b: (b, 0, 0))
    rows = pl.BlockSpec((None, KV_ROWS, 128), lambda b: (b, 0, 0))
    return pl.pallas_call(
        _memkv_kernel,
        grid=(B,),
        in_specs=[tok, _const_spec((1, D_MODEL)),
                  _const_spec((D_MODEL, D_MODEL)), _const_spec((D_MODEL, D_MODEL))],
        out_specs=[tok, tok, rows, rows],
        out_shape=[jax.ShapeDtypeStruct((B, N_MEM, D_MODEL), BF16)] * 2
        + [jax.ShapeDtypeStruct((B, KV_ROWS, 128), F32)] * 2,
        compiler_params=_params("parallel"),
    )(mem, g_mem, wk, wv)


def _mixer_prompt_kernel(x_ref, xs_ref, wn_ref, wt_ref, wgt_ref, wout_ref, gmix_ref, bgt_ref,
                         wconv_ref, bconv_ref, gmh_ref, grh_ref, cos_ref, sin_ref,
                         decay_ref, qdec_ref, kdec_ref, cdec_ref, wg32_ref, wu32_ref, wd32_ref,
                         x1_ref, conv_ref, C_ref, n_ref, m_ref, S_ref, wg16_ref, wu16_ref, wd16_ref,
                         un_a, ut_a, gt_a, un_b, ut_b, gt_b, tail_ref, st_ref, h_ref,
                         *, tt, tiles_per_seq):
    wg16_ref[...] = wg32_ref[...].astype(BF16)
    wu16_ref[...] = wu32_ref[...].astype(BF16)
    wd16_ref[...] = wd32_ref[...].astype(BF16)
    L = CHUNK
    s = pl.program_id(0)
    t = (jnp.maximum(s, 1) - 1) % tiles_per_seq
    chunks = range(tt // L)
    slot_a, slot_b = (un_a, ut_a, gt_a), (un_b, ut_b, gt_b)

    @pl.when(s == 0)
    def _():
        for ref in slot_b + (tail_ref,):
            ref[...] = jnp.zeros_like(ref)

    @pl.when(t == 0)
    def _():
        C_ref[...] = jnp.zeros_like(C_ref)
        st_ref[...] = jnp.zeros_like(st_ref)
        n_ref[...] = jnp.zeros_like(n_ref)
        m_ref[...] = jnp.zeros_like(m_ref)

    k_scale = HEAD_DIM ** -0.5

    def project(slot, xn, part):
        un_ref, ut_ref, gt_ref = slot
        c0 = part * MIX
        un_ref[UOFF:UOFF + tt, c0:c0 + MIX] = jnp.dot(xn, wn_ref[:, c0:c0 + MIX],
                                                      preferred_element_type=F32)
        res = lax.dot_general(wt_ref[c0:c0 + MIX, :], xn, NT_DIMS, preferred_element_type=F32)
        for c in chunks:
            ut_ref[c, c0:c0 + MIX, :] = res[:, c * L:(c + 1) * L]
        if part == 0:
            gates_t = lax.dot_general(wgt_ref[...], xn, NT_DIMS, preferred_element_type=F32)
            for c in chunks:
                gt_ref[c] = gates_t[:, c * L:(c + 1) * L]

    src_id = lax.broadcasted_iota(jnp.int32, (L, L), 0)
    tgt_id = lax.broadcasted_iota(jnp.int32, (L, L), 1)
    causal = src_id <= tgt_id
    triu_bf = jnp.where(causal, 1.0, 0.0).astype(BF16)
    heads = range(HEADS)
    hcol = lambda base, h: slice(base + h * HEAD_DIM, base + (h + 1) * HEAD_DIM)

    def chunk_body(c, slot):
        un_ref, ut_ref, gt_ref = slot
        r0 = c * L
        rows = pl.ds(UOFF + r0, L)
        trows = pl.ds(r0, L)
        g_t = gt_ref[c] + bgt_ref[...]
        ig = g_t[0:8]
        lf = _log_sigmoid(g_t[8:16])
        lf_hi = lf.astype(BF16)
        r1 = lf - lf_hi.astype(F32)
        lf_mid = r1.astype(BF16)
        lf_lo = (r1 - lf_mid.astype(F32)).astype(BF16)
        bc = (jnp.dot(lf_hi, triu_bf, preferred_element_type=F32)
              + jnp.dot(lf_mid, triu_bf, preferred_element_type=F32)
              + jnp.dot(lf_lo, triu_bf, preferred_element_type=F32))
        m_prev = m_ref[...]
        inter = bc + m_prev
        b_last = bc[:, L - 1:L]
        g_w = b_last - bc + ig
        m_new = jnp.maximum(b_last + m_prev, jnp.max(g_w, axis=1, keepdims=True))
        ws = jnp.exp(g_w - m_new)
        carry = jnp.exp(b_last + m_prev - m_new)
        a_n = jnp.concatenate([ig - bc, jnp.zeros((L - 8, L), F32)], axis=0).T

        def conv_act(col):
            win = un_ref[pl.ds(r0, L + UOFF), col:col + HEAD_DIM]
            acc = bconv_ref[:, col:col + HEAD_DIM]
            for j in range(CONV_W):
                back = CONV_W - 1 - j
                tap = win if back == 0 else pltpu.roll(win, back, 0)
                acc = acc + tap[UOFF:UOFF + L] * wconv_ref[j:j + 1, col:col + HEAD_DIM]
            return _silu(acc)

        cos_f, sin_s = cos_ref[trows, :], sin_ref[trows, :]
        q = [conv_act(N_QK + h * HEAD_DIM) for h in heads]
        kb = [(conv_act(N_QK + MIX + h * HEAD_DIM) * k_scale).astype(BF16) for h in heads]
        qb = [a.astype(BF16) for a in q]
        rqb = [_rope(un_ref[rows, hcol(N_RQ, h)], cos_f, sin_s).astype(BF16) for h in heads]
        rk = [_rope(un_ref[rows, hcol(N_RK, h)], cos_f, sin_s) * k_scale for h in heads]
        v_t = [ut_ref[c, hcol(T_V, h), :] for h in heads]
        rvb = [ut_ref[c, hcol(T_RV, h), :].astype(BF16) for h in heads]
        c_old = [C_ref[h] for h in heads]
        s_old = [st_ref[h] for h in heads]
        n_old = [n_ref[h:h + 1, :] for h in heads]
        qk = [_dot_nt(kb[h], qb[h]) for h in heads]
        att = [_dot_nt(rk[h], rqb[h]) for h in heads]
        c_q = [_dot_nt(c_old[h], qb[h]) for h in heads]
        s_q = [_dot_nt(s_old[h], rqb[h]) for h in heads]
        n_q = [_dot_nt(jnp.broadcast_to(n_old[h], (8, HEAD_DIM)), qb[h])[0:1] for h in heads]
        d_c = [_dot(v_t[h] * ws[h:h + 1, :], kb[h]) for h in heads]
        d_n = [_dot(jnp.broadcast_to(ws[h:h + 1, :], (8, L)), kb[h])[0:1] for h in heads]
        d_s = [_dot(rvb[h], rk[h] * kdec_ref[h]) for h in heads]
        m_t, w_in, wts = [], [], []
        for h in heads:
            dmat = a_n[:, h:h + 1] + bc[h:h + 1, :]
            dmat = jnp.where(causal, dmat, -jnp.inf)
            m_t.append(jnp.maximum(inter[h:h + 1, :], jnp.max(dmat, axis=0, keepdims=True)))
            wts.append(jnp.exp(dmat - m_t[h]) * qk[h])
            w_in.append(jnp.exp(inter[h:h + 1, :] - m_t[h]))
        att_w = [att[h] * decay_ref[h] for h in heads]
        v_p = [_dot(v_t[h], wts[h]) for h in heads]
        v_a = [_dot(rvb[h], att_w[h]) for h in heads]
        for h in heads:
            num = v_p[h] + w_in[h] * c_q[h]
            den = jnp.sum(wts[h], axis=0, keepdims=True) + w_in[h] * n_q[h]
            hm = num / jnp.maximum(jnp.abs(den), jnp.exp(-m_t[h]))
            hm = hm * lax.rsqrt(jnp.mean(hm * hm, axis=0, keepdims=True) + EPS)
            hm = hm * gmh_ref[hcol(0, h), :] * jax.nn.sigmoid(ut_ref[c, hcol(T_O, h), :])
            h_ref[trows, hcol(0, h)] = hm.T.astype(BF16)
            carry_h = carry[h:h + 1, :]
            C_ref[h] = carry_h * c_old[h] + d_c[h]
            n_ref[h:h + 1, :] = carry_h * n_old[h] + d_n[h]
            o = v_a[h] + qdec_ref[h:h + 1, :] * s_q[h]
            st_ref[h] = cdec_ref[h:h + 1, :] * s_old[h] + d_s[h]
            hr = o * lax.rsqrt(jnp.mean(o * o, axis=0, keepdims=True) + EPS)
            hr = hr * grh_ref[hcol(0, h), :] * _silu(ut_ref[c, hcol(T_RG, h), :])
            h_ref[trows, hcol(MIX, h)] = hr.T.astype(BF16)

        m_ref[...] = m_new

    def step(proj_slot, scan_slot):
        un_ref = scan_slot[0]
        un_ref[0:UOFF, N_QK:N_RQ] = jnp.where(t == 0, 0.0, tail_ref[...])
        xn = _rms(x_ref[...], gmix_ref[...]).astype(BF16)
        for c in chunks:
            project(proj_slot, xn, c)
            chunk_body(c, scan_slot)
        tail_ref[...] = un_ref[tt:tt + UOFF, N_QK:N_RQ]
        x1_ref[...] = xs_ref[...] + jnp.dot(h_ref[...], wout_ref[...], preferred_element_type=F32)

    @pl.when(s % 2 == 0)
    def _():
        step(slot_a, slot_b)

    @pl.when(s % 2 == 1)
    def _():
        step(slot_b, slot_a)

    @pl.when(t == tiles_per_seq - 1)
    def _():
        conv_ref[...] = tail_ref[UOFF - (CONV_W - 1):UOFF, :]
        for h in heads:
            S_ref[h] = st_ref[h].T


def _mixer_prompt(x, wn, wt, wgt, w_out, g_mix, bgt, w_conv, b_conv, gmh_cols, grh_cols, cos_f, sin_s,
                  decay_t, qdec_rows, kdec_cols, cdec_rows, w_gate, w_up, w_down, tt=512):
    B, T, _ = x.shape
    tps = T // tt
    n_tiles = B * tps
    proj = lambda s: jnp.minimum(s, n_tiles - 1)
    scan = lambda s: jnp.maximum(s, 1) - 1
    per_b3 = lambda s: (scan(s) // tps, 0, 0)
    per_b4 = lambda s: (scan(s) // tps, 0, 0, 0)
    scan_tile = pl.BlockSpec((None, tt, D_MODEL), lambda s: (scan(s) // tps, scan(s) % tps, 0))
    in_specs = [
        pl.BlockSpec((None, tt, D_MODEL), lambda s: (proj(s) // tps, proj(s) % tps, 0)),
        scan_tile,
        _const_spec((D_MODEL, N_COLS)), _const_spec((T_ROWS, D_MODEL)), _const_spec((GT_ROWS, D_MODEL)),
        _const_spec((D_MODEL, D_MODEL)),
        _const_spec((1, D_MODEL)), _const_spec((GT_ROWS, CHUNK)),
        _const_spec((CONV_W, 2 * MIX)), _const_spec((1, 2 * MIX)),
        _const_spec((MIX, CHUNK)), _const_spec((MIX, CHUNK)),
        pl.BlockSpec((tt, HEAD_DIM), lambda s: (scan(s) % tps, 0)),
        pl.BlockSpec((tt, HEAD_DIM), lambda s: (scan(s) % tps, 0)),
        _const_spec((HEADS, CHUNK, CHUNK)),
        _const_spec((8, CHUNK)), _const_spec((HEADS, CHUNK, HEAD_DIM)), _const_spec((8, HEAD_DIM)),
    ]
    ff_blocks = D_FF // HEAD_DIM
    assert ff_blocks <= n_tiles + 1
    early = lambda s: jnp.minimum(s, ff_blocks - 1)
    late = lambda s: jnp.maximum(s - (n_tiles + 1 - ff_blocks), 0)
    cast_specs = [pl.BlockSpec((D_MODEL, HEAD_DIM), lambda s: (0, early(s))),
                  pl.BlockSpec((D_MODEL, HEAD_DIM), lambda s: (0, late(s))),
                  pl.BlockSpec((HEAD_DIM, D_MODEL), lambda s: (late(s), 0))]
    in_specs += cast_specs
    out_specs = [
        scan_tile,
        pl.BlockSpec((None, CONV_W - 1, 2 * MIX), per_b3),
        pl.BlockSpec((None, HEADS, HEAD_DIM, HEAD_DIM), per_b4),
        pl.BlockSpec((None, HEADS, HEAD_DIM), per_b3),
        pl.BlockSpec((None, 8, CHUNK), per_b3),
        pl.BlockSpec((None, HEADS, HEAD_DIM, HEAD_DIM), per_b4),
    ] + cast_specs
    out_shape = [
        jax.ShapeDtypeStruct((B, T, D_MODEL), F32),
        jax.ShapeDtypeStruct((B, CONV_W - 1, 2 * MIX), F32),
        jax.ShapeDtypeStruct((B, HEADS, HEAD_DIM, HEAD_DIM), F32),
        jax.ShapeDtypeStruct((B, HEADS, HEAD_DIM), F32),
        jax.ShapeDtypeStruct((B, 8, CHUNK), F32),
        jax.ShapeDtypeStruct((B, HEADS, HEAD_DIM, HEAD_DIM), F32),
        jax.ShapeDtypeStruct((D_MODEL, D_FF), BF16), jax.ShapeDtypeStruct((D_MODEL, D_FF), BF16),
        jax.ShapeDtypeStruct((D_FF, D_MODEL), BF16),
    ]
    n_chunks = tt // CHUNK
    slot = [
        pltpu.VMEM((tt + UOFF, N_COLS), F32),
        pltpu.VMEM((n_chunks, T_ROWS, CHUNK), F32),
        pltpu.VMEM((n_chunks, GT_ROWS, CHUNK), F32),
    ]
    scratch = slot + slot + [
        pltpu.VMEM((UOFF, 2 * MIX), F32),
        pltpu.VMEM((HEADS, HEAD_DIM, HEAD_DIM), F32),
        pltpu.VMEM((tt, D_MODEL), BF16),
    ]
    return pl.pallas_call(
        functools.partial(_mixer_prompt_kernel, tt=tt, tiles_per_seq=tps),
        grid=(n_tiles + 1,),
        in_specs=in_specs, out_specs=out_specs, out_shape=out_shape,
        scratch_shapes=scratch,
        compiler_params=_params("arbitrary"),
    )(x, x, wn, wt, wgt, w_out, g_mix, bgt, w_conv, b_conv, gmh_cols, grh_cols, cos_f, sin_s,
      decay_t, qdec_rows, kdec_cols, cdec_rows, w_gate, w_up, w_down)


def _attn_prompt_kernel(x1_ref, k_ref, v_ref, wcq_ref, gx_ref, *rest, n_sample_steps):
    sample_in, o_ref = rest[:SAMPLE_MIXER_INPUTS], rest[SAMPLE_MIXER_INPUTS]
    sample_out_and_scratch = rest[SAMPLE_MIXER_INPUTS + 1:]
    step = pl.program_id(0) * pl.num_programs(1) + pl.program_id(1)

    @pl.when(step < n_sample_steps)
    def _():
        _mixer_sample_kernel(*sample_in, *sample_out_and_scratch, bb=SAMPLE_BLOCK)

    tm = x1_ref.shape[0]
    rows = [slice(r, r + tm // ROW_GROUPS) for r in range(0, tm, tm // ROW_GROUPS)]
    xq = [_rms(x1_ref[r, :], gx_ref[...]).astype(BF16) for r in rows]
    q = [jnp.dot(a, wcq_ref[...], preferred_element_type=F32).astype(BF16) for a in xq]
    sl = [slice(h * X_HEAD_DIM, (h + 1) * X_HEAD_DIM) for h in range(X_HEADS)]
    items = [(g, h) for g in range(ROW_GROUPS) for h in range(X_HEADS)]
    s = [_dot_nt(q[g][:, sl[h]], k_ref[:, sl[h]]) * (X_HEAD_DIM ** -0.5) for g, h in items]
    e = [jnp.exp(a - jnp.max(a, axis=-1, keepdims=True)) for a in s]
    p = [a / jnp.sum(a, axis=-1, keepdims=True) for a in e]
    for (g, h), a in zip(items, p):
        o_ref[rows[g], sl[h]] = _dot(a, v_ref[:, sl[h]]).astype(BF16)


def _attn_prompt(x1, mk, mv, w_cq, g_x, sample_mixer_operands, tm=512):
    B, T, _ = x1.shape
    tps = T // tm
    Bs = sample_mixer_operands[0].shape[0]
    n_sample_steps = Bs // SAMPLE_BLOCK
    assert n_sample_steps <= B * tps
    block_of = lambda b, t: jnp.minimum(b * tps + t, n_sample_steps - 1)
    s_in, s_out, s_shape, s_scratch = _sample_mixer_specs(Bs, block_of)
    tile = pl.BlockSpec((None, tm, D_MODEL), lambda b, t: (b, t, 0))
    kv = pl.BlockSpec((None, N_MEM, D_MODEL), lambda b, t: (b, 0, 0))
    outs = pl.pallas_call(
        functools.partial(_attn_prompt_kernel, n_sample_steps=n_sample_steps),
        grid=(B, tps),
        in_specs=[tile, kv, kv, _const_spec((D_MODEL, D_MODEL)), _const_spec((1, D_MODEL))] + s_in,
        out_specs=[tile] + s_out,
        out_shape=[jax.ShapeDtypeStruct((B, T, D_MODEL), BF16)] + s_shape,
        scratch_shapes=s_scratch,
        compiler_params=_params("arbitrary", "arbitrary"),
    )(x1, mk, mv, w_cq, g_x, *sample_mixer_operands)
    return outs[0], outs[1:]


V7X_MXU_DIM = 256
FF_CHUNKS = ((0, 6 * V7X_MXU_DIM), (6 * V7X_MXU_DIM, D_FF))


def _ffn_kernel(x1_ref, o_ref, wco_ref, wg_ref, wu_ref, wd_ref, gffn_ref, gfin_ref, *rest, n_groups):
    if len(rest) == 1:
        (y_ref,) = rest
        sample_stages = iter(())
    else:
        qs_ref, ck_ref, cv_ref, y_ref, os_ref = rest
        sample_stages = _sample_attention(qs_ref, ck_ref, cv_ref, os_ref)
    tm = x1_ref.shape[0]
    rows = [slice(r, r + tm // n_groups) for r in range(0, tm, tm // n_groups)]
    acc = [x1_ref[r, :] + _dot(o_ref[r, :], wco_ref[...]) for r in rows]
    hf = [_rms(a, gffn_ref[...]).astype(BF16) for a in acc]
    for c0, c1 in FF_CHUNKS:
        next(sample_stages, None)
        gate = [jnp.dot(h, wg_ref[:, c0:c1], preferred_element_type=F32) for h in hf]
        up = [jnp.dot(h, wu_ref[:, c0:c1], preferred_element_type=F32) for h in hf]
        act = [_silu(g) * u for g, u in zip(gate, up)]
        acc = [a + _dot(p, wd_ref[c0:c1, :]) for a, p in zip(acc, act)]
    next(sample_stages, None)
    for r, a in zip(rows, acc):
        y_ref[r, :] = _rms(a, gfin_ref[...])


def _ffn(x1, o, w_co, w_gate, w_up, w_down, g_ffn, g_final, tm, sample=None):
    rows = x1.shape[0]
    steps = rows // tm
    row_spec = pl.BlockSpec((tm, D_MODEL), lambda i: (i, 0))
    in_specs = [row_spec, row_spec, _const_spec((D_MODEL, D_MODEL)),
                _const_spec((D_MODEL, D_FF)), _const_spec((D_MODEL, D_FF)),
                _const_spec((D_FF, D_MODEL)), _const_spec((1, D_MODEL)), _const_spec((1, D_MODEL))]
    out_specs = [row_spec]
    out_shape = [jax.ShapeDtypeStruct((rows, D_MODEL), F32)]
    operands = [x1, o, w_co, w_gate, w_up, w_down, g_ffn, g_final]
    if sample is not None:
        q_s, ck_s, cv_s = sample
        Bs = q_s.shape[0]
        ba = Bs // steps
        assert ba * steps == Bs
        s_rows = pl.BlockSpec((ba, QROWS, 128), lambda i: (i, 0, 0))
        s_kv = pl.BlockSpec((ba, KV_ROWS, 128), lambda i: (i, 0, 0))
        in_specs += [s_rows, s_kv, s_kv]
        out_specs += [s_rows]
        out_shape += [jax.ShapeDtypeStruct((Bs, QROWS, 128), F32)]
        operands += [q_s.reshape(Bs, QROWS, 128), ck_s, cv_s]
    outs = pl.pallas_call(
        functools.partial(_ffn_kernel, n_groups=ROW_GROUPS if tm >= 512 else 1),
        grid=(steps,),
        in_specs=in_specs, out_specs=out_specs, out_shape=out_shape,
        compiler_params=_params("parallel"),
    )(*operands)
    if sample is None:
        return outs[0]
    return outs[0], outs[1].reshape(Bs, D_MODEL)


def _inproj_sample_kernel(x_ref, g_ref, wn_ref, wt_ref, wgt_ref, un_ref, ut_ref, gs_ref):
    xn = _rms(x_ref[...], g_ref[...]).astype(BF16)
    un_ref[...] = jnp.dot(xn, wn_ref[...], preferred_element_type=F32)
    ut_ref[...] = lax.dot_general(xn, wt_ref[...], NT_DIMS, preferred_element_type=F32)
    gs_ref[...] = lax.dot_general(xn, wgt_ref[...], NT_DIMS, preferred_element_type=F32)


def _inproj_sample(x, g_mix, wn, wt, wgt):
    rows = x.shape[0]
    full = lambda cols: pl.BlockSpec((rows, cols), lambda i: (0, 0))
    return pl.pallas_call(
        _inproj_sample_kernel,
        grid=(1,),
        in_specs=[_const_spec((rows, D_MODEL)), _const_spec((1, D_MODEL)),
                  _const_spec((D_MODEL, N_COLS)), _const_spec((T_ROWS, D_MODEL)),
                  _const_spec((GT_ROWS, D_MODEL))],
        out_specs=[full(N_COLS), full(T_ROWS), full(GT_ROWS)],
        out_shape=[jax.ShapeDtypeStruct((rows, N_COLS), F32),
                   jax.ShapeDtypeStruct((rows, T_ROWS), F32),
                   jax.ShapeDtypeStruct((rows, GT_ROWS), F32)],
        compiler_params=_params("arbitrary"),
    )(x, g_mix, wn, wt, wgt)


def _mixer_sample_kernel(un_ref, ut_ref, gs_ref, conv_ref, C_ref, n_ref, m_ref, S_ref, bg_ref,
                         wconv_ref, bconv_ref, gmh_ref, grh_ref, cos_ref, sin_ref, rtab_ref,
                         h_ref, convo_ref, Co_ref, no_ref, mo_ref, So_ref,
                         q_s, k_s, vws_s, qr_s, kr_s, qc_s, qs_s, carry_s, *, bb):
    k_scale = HEAD_DIM ** -0.5
    uqk = un_ref[:, N_QK:N_RQ]
    conv = (bconv_ref[...] + wconv_ref[0:1, :] * conv_ref[0] + wconv_ref[1:2, :] * conv_ref[1]
            + wconv_ref[2:3, :] * conv_ref[2] + wconv_ref[3:4, :] * uqk)
    convo_ref[0] = conv_ref[1]
    convo_ref[1] = conv_ref[2]
    convo_ref[2] = uqk
    qk_act = _silu(conv)
    q_s[...] = qk_act[:, 0:MIX]
    k_s[...] = qk_act[:, MIX:2 * MIX] * k_scale

    gates = gs_ref[...] + bg_ref[...]
    ig = gates[:, 0:HEADS]
    lf = _log_sigmoid(gates[:, 8:8 + HEADS])
    inter = lf + m_ref[...]
    m_t = jnp.maximum(inter, ig)
    ws = jnp.exp(ig - m_t)
    w_in = jnp.exp(inter - m_t)
    mo_ref[...] = m_t
    carry_s[:, 0:HEADS] = w_in

    cos_f = cos_ref[...]
    sin_s = sin_ref[...]
    for h in range(HEADS):
        lo = h * HEAD_DIM
        hs = slice(lo, lo + HEAD_DIM)
        vws_s[:, hs] = ut_ref[:, T_V + lo:T_V + lo + HEAD_DIM] * ws[:, h:h + 1]
        qr_s[:, hs] = _rope(un_ref[:, N_RQ + lo:N_RQ + lo + HEAD_DIM], cos_f, sin_s)
        kr_s[:, hs] = _rope(un_ref[:, N_RK + lo:N_RK + lo + HEAD_DIM], cos_f, sin_s) * k_scale

    heads = range(HEADS)
    seqs = range(bb)
    hsl = [slice(h * HEAD_DIM, (h + 1) * HEAD_DIM) for h in heads]
    seq_id = lax.broadcasted_iota(jnp.int32, (bb, HEAD_DIM), 0)
    for h in heads:
        q_h, k_h, vws_h = q_s[:, hsl[h]], k_s[:, hsl[h]], vws_s[:, hsl[h]]
        qr_h = qr_s[:, hsl[h]]
        krd_h = kr_s[:, hsl[h]] * rtab_ref[1:2, h:h + 1]
        rv_h = ut_ref[:, T_RV + h * HEAD_DIM:T_RV + (h + 1) * HEAD_DIM]
        c_old = [C_ref[b, h] for b in seqs]
        s_old = [S_ref[b, h] for b in seqs]
        q_c = [_dot_nt(q_h, c_old[b]) for b in seqs]
        q_st = [_dot(qr_h, s_old[b]) for b in seqs]
        d_c = [_dot_tn(jnp.where(seq_id == b, vws_h, 0.0), k_h) for b in seqs]
        d_s = [_dot_tn(jnp.where(seq_id == b, krd_h, 0.0), rv_h) for b in seqs]
        for b in seqs:
            row = slice(b, b + 1)
            qc_s[row, hsl[h]] = q_c[b][row]
            qs_s[row, hsl[h]] = q_st[b][row]
            Co_ref[b, h] = carry_s[row, h:h + 1] * c_old[b] + d_c[b]
            So_ref[b, h] = rtab_ref[2:3, h:h + 1] * s_old[b] + d_s[b]

    for h in range(HEADS):
        lo = h * HEAD_DIM
        hs = slice(lo, lo + HEAD_DIM)
        q_h, k_h, n_h = q_s[:, hs], k_s[:, hs], n_ref[:, hs]
        ws_h, w_in_h = ws[:, h:h + 1], w_in[:, h:h + 1]
        wts = ws_h * jnp.sum(q_h * k_h, axis=1, keepdims=True)
        num = wts * ut_ref[:, T_V + lo:T_V + lo + HEAD_DIM] + w_in_h * qc_s[:, hs]
        den = wts + w_in_h * jnp.sum(n_h * q_h, axis=1, keepdims=True)
        hm = num / jnp.maximum(jnp.abs(den), jnp.exp(-m_t[:, h:h + 1]))
        mo = ut_ref[:, T_O + lo:T_O + lo + HEAD_DIM]
        h_ref[:, hs] = _head_norm(hm) * gmh_ref[:, hs] * jax.nn.sigmoid(mo)
        no_ref[:, hs] = w_in_h * n_h + ws_h * k_h

        att = jnp.sum(qr_s[:, hs] * kr_s[:, hs], axis=1, keepdims=True) * rtab_ref[3:4, h:h + 1]
        o = (att * ut_ref[:, T_RV + lo:T_RV + lo + HEAD_DIM]
             + rtab_ref[0:1, h:h + 1] * qs_s[:, hs])
        rg = ut_ref[:, T_RG + lo:T_RG + lo + HEAD_DIM]
        h_ref[:, MIX + lo:MIX + lo + HEAD_DIM] = _head_norm(o) * grh_ref[:, hs] * _silu(rg)


SAMPLE_BLOCK = 8
SAMPLE_MIXER_INPUTS = 16


def _sample_mixer_specs(B, block_of):
    bb = SAMPLE_BLOCK
    rows = lambda cols: pl.BlockSpec((bb, cols), lambda *g: (block_of(*g), 0))
    mats = pl.BlockSpec((bb, HEADS, HEAD_DIM, HEAD_DIM), lambda *g: (block_of(*g), 0, 0, 0))
    convs = pl.BlockSpec((CONV_W - 1, bb, 2 * MIX), lambda *g: (0, block_of(*g), 0))
    in_specs = [rows(N_COLS), rows(T_ROWS), rows(GT_ROWS), convs, mats, rows(MIX), rows(HEADS), mats,
                _const_spec((1, GT_ROWS)), _const_spec((CONV_W, 2 * MIX)), _const_spec((1, 2 * MIX)),
                _const_spec((1, MIX)), _const_spec((1, MIX)),
                _const_spec((1, HEAD_DIM)), _const_spec((1, HEAD_DIM)), _const_spec((8, HEAD_DIM))]
    assert len(in_specs) == SAMPLE_MIXER_INPUTS
    out_specs = [rows(D_MODEL), convs, mats, rows(MIX), rows(HEADS), mats]
    out_shape = [jax.ShapeDtypeStruct((B, D_MODEL), F32),
                 jax.ShapeDtypeStruct((CONV_W - 1, B, 2 * MIX), F32),
                 jax.ShapeDtypeStruct((B, HEADS, HEAD_DIM, HEAD_DIM), F32),
                 jax.ShapeDtypeStruct((B, MIX), F32),
                 jax.ShapeDtypeStruct((B, HEADS), F32),
                 jax.ShapeDtypeStruct((B, HEADS, HEAD_DIM, HEAD_DIM), F32)]
    scratch = [pltpu.VMEM((bb, MIX), F32) for _ in range(7)] + [pltpu.VMEM((bb, HEAD_DIM), F32)]
    return in_specs, out_specs, out_shape, scratch


def _outq_sample_kernel(x_ref, h_ref, wout_ref, wcq_ref, gx_ref, x1_ref, q_ref):
    x1 = x_ref[...] + _dot(h_ref[...], wout_ref[...])
    x1_ref[...] = x1
    q_ref[...] = _dot(_rms(x1, gx_ref[...]), wcq_ref[...])


def _outq_sample(x, hcat, w_out, w_cq, g_x):
    rows = x.shape[0]
    full = _const_spec((rows, D_MODEL))
    return pl.pallas_call(
        _outq_sample_kernel,
        grid=(1,),
        in_specs=[full, full, _const_spec((D_MODEL, D_MODEL)), _const_spec((D_MODEL, D_MODEL)),
                  _const_spec((1, D_MODEL))],
        out_specs=[pl.BlockSpec((rows, D_MODEL), lambda i: (0, 0))] * 2,
        out_shape=[jax.ShapeDtypeStruct((rows, D_MODEL), F32)] * 2,
        compiler_params=_params("arbitrary"),
    )(x, hcat, w_out, w_cq, g_x)


def _sample_attention(q_ref, k_ref, v_ref, o_ref):
    r_id = lax.broadcasted_iota(jnp.int32, (QROWS, KV_ROWS), 0)
    n_id = lax.broadcasted_iota(jnp.int32, (QROWS, KV_ROWS), 1)
    own = (n_id & 7) == (r_id >> 1) + 4 * (r_id & 1)
    low_half = (lax.broadcasted_iota(jnp.int32, (1, KV_ROWS), 1) & 4) == 0
    seqs = range(q_ref.shape[0])
    z = [_dot_nt(q_ref[j], k_ref[j]) for j in seqs]
    zc = [jnp.sum(jnp.where(own, z[j], 0.0), axis=0, keepdims=True) for j in seqs]
    yield
    other = [jnp.where(low_half, pltpu.roll(zc[j], KV_ROWS - 4, 1), pltpu.roll(zc[j], 4, 1))
             for j in seqs]
    s = [jnp.where(own, (zc[j] + other[j]) * (X_HEAD_DIM ** -0.5), -jnp.inf) for j in seqs]
    m = [jnp.max(s[j], axis=-1, keepdims=True) for j in seqs]
    e = [jnp.exp(s[j] - m[j]) for j in seqs]
    den = [jnp.sum(e[j], axis=-1, keepdims=True) for j in seqs]
    yield
    for j in seqs:
        o_ref[j] = _dot(e[j] / den[j], v_ref[j])
    yield


def _kv_rows(cache):
    B = cache.shape[0]
    c5 = cache.reshape(B, N_MEM, X_HEADS, 2, 128)
    return jnp.transpose(c5, (0, 1, 3, 2, 4)).reshape(B, KV_ROWS, 128)


def _kv_from_rows(rows):
    B = rows.shape[0]
    r5 = rows.reshape(B, N_MEM, 2, X_HEADS, 128)
    return jnp.transpose(r5, (0, 1, 3, 2, 4)).reshape(1, B, N_MEM, X_HEADS, X_HEAD_DIM)


def _rope_tables(pos):
    half = HEAD_DIM // 2
    inv = ROPE_THETA ** (-np.arange(half, dtype=np.float64) / half)
    ang = np.asarray(pos, np.float64)[:, None] * inv[None, :]
    cos, sin = np.cos(ang), np.sin(ang)
    return (np.concatenate([cos, cos], axis=-1).astype(np.float32),
            np.concatenate([-sin, sin], axis=-1).astype(np.float32))


def _retention_tables(L):
    lg = np.log1p(-np.exp2(-5.0 - np.arange(HEADS, dtype=np.float64)))
    t = np.arange(L, dtype=np.float64)
    diff = t[:, None] - t[None, :]
    decay = np.where(diff >= 0, np.exp(lg[:, None, None] * np.maximum(diff, 0.0)), 0.0)
    q_dec = np.exp(lg[:, None] * (t + 1.0))
    k_dec = np.exp(lg[:, None] * (L - 1.0 - t))
    chunk_dec = np.exp(lg * L)
    return tuple(a.astype(np.float32) for a in (decay, q_dec, k_dec, chunk_dec))


def _lanes(a, n):
    xp = np if isinstance(a, np.ndarray) else jnp
    return xp.broadcast_to(a[..., None], a.shape + (n,))


def _pad_rows(a, rows):
    return np.pad(a, ((0, rows - a.shape[0]),) + ((0, 0),) * (a.ndim - 1))


def kernel(x_prompt, x_sample, cache_mem_k, cache_mem_v, state_mlstm_conv, state_mlstm_C, state_mlstm_n, state_mlstm_m, state_ret_S, mem_prompt, w_in, b_gate, w_conv, b_conv, g_mix, g_mhead, g_rhead, w_out, g_xattn, g_mem, w_ck, w_cv, w_cq, w_co, g_ffn, w_gate, w_up, w_down, g_final):
    Bp, Tp, _ = x_prompt.shape
    Bs = x_sample.shape[0]
    l = 0
    n_m = 4 * MIX
    wi = w_in[l]
    w_gates = wi[:, n_m:n_m + 2 * HEADS]
    w_ret = wi[:, n_m + 2 * HEADS:]
    row = lambda a: a.reshape(1, -1)
    bf = lambda a: a.astype(BF16)
    g_mix_r, g_mh_r, g_rh_r = row(g_mix[l]), row(g_mhead[l]), row(g_rhead[l])
    g_x_r, g_mem_r, g_ffn_r, g_fin_r = row(g_xattn[l]), row(g_mem[l]), row(g_ffn[l]), row(g_final)
    b_conv_r = row(b_conv[l])
    w_out_b, w_cq_b, w_co_b = bf(w_out[l]), bf(w_cq[l]), bf(w_co[l])

    mk_b, mv_b, mk_rows, mv_rows = _memkv(mem_prompt, g_mem_r, bf(w_ck[l]), bf(w_cv[l]))
    wn = bf(jnp.concatenate([wi[:, :2 * MIX], w_ret[:, :2 * MIX]], axis=1))
    wt = bf(jnp.concatenate([wi[:, 2 * MIX:n_m], w_ret[:, 2 * MIX:]], axis=1).T)
    gate_rows = lambda a: jnp.concatenate(
        [a[:HEADS], jnp.zeros((8 - HEADS,) + a.shape[1:], F32),
         a[HEADS:], jnp.zeros((8 - HEADS,) + a.shape[1:], F32)], axis=0)
    wgt = bf(gate_rows(w_gates.T))
    bg_rows = gate_rows(b_gate[l][:, None])
    bgt = jnp.broadcast_to(bg_rows, (GT_ROWS, CHUNK))
    cos_p, sin_p = _rope_tables(np.arange(Tp))
    decay, q_dec, k_dec, chunk_dec = _retention_tables(CHUNK)
    x1_p, conv_p, C_p, n_p, m_p, S_p, w_gate_b, w_up_b, w_down_b = _mixer_prompt(
        x_prompt, wn, wt, wgt, w_out_b, g_mix_r, bgt, w_conv[l], b_conv_r,
        _lanes(g_mhead[l], CHUNK), _lanes(g_rhead[l], CHUNK), cos_p, sin_p,
        np.swapaxes(decay, 1, 2), _pad_rows(q_dec, 8), _lanes(k_dec, HEAD_DIM),
        _pad_rows(_lanes(chunk_dec, HEAD_DIM), 8), w_gate[l], w_up[l], w_down[l])

    xs = x_sample.reshape(Bs, D_MODEL)
    un_s, ut_s, gs_s = _inproj_sample(xs, g_mix_r, wn, wt, wgt)
    cos_s, sin_s = _rope_tables(PAST_LEN + np.arange(1))
    decay1, q_dec1, k_dec1, chunk_dec1 = _retention_tables(1)
    rtab = np.zeros((8, HEAD_DIM), np.float32)
    rtab[:4, :HEADS] = np.stack([q_dec1[:, 0], k_dec1[:, 0], chunk_dec1, decay1[:, 0, 0]])
    sample_mixer_operands = (
        un_s, ut_s, gs_s, jnp.transpose(state_mlstm_conv[l], (1, 0, 2)), state_mlstm_C[l],
        state_mlstm_n[l].reshape(Bs, MIX), state_mlstm_m[l], state_ret_S[l],
        bg_rows.reshape(1, GT_ROWS), w_conv[l], b_conv_r, g_mh_r, g_rh_r, cos_s, sin_s, rtab)

    o_p, (hcat_s, conv_s, C_s, n_s, m_s, S_s) = _attn_prompt(
        x1_p, mk_b, mv_b, w_cq_b, g_x_r, sample_mixer_operands)
    x1_s, q_s = _outq_sample(xs, hcat_s, w_out_b, w_cq_b, g_x_r)
    y_p, o_s = _ffn(x1_p.reshape(Bp * Tp, D_MODEL), o_p.reshape(Bp * Tp, D_MODEL),
                    w_co_b, w_gate_b, w_up_b, w_down_b, g_ffn_r, g_fin_r, tm=512,
                    sample=(q_s, _kv_rows(cache_mem_k[l]), _kv_rows(cache_mem_v[l])))
    y_s = _ffn(x1_s, o_s, w_co_b, w_gate_b, w_up_b, w_down_b, g_ffn_r, g_fin_r, tm=Bs)

    return (y_p.reshape(Bp, Tp, D_MODEL), y_s.reshape(Bs, 1, D_MODEL),
            _kv_from_rows(mk_rows), _kv_from_rows(mv_rows),
            conv_p[None], C_p[None], n_p[None], m_p[None, :, :HEADS, 0], S_p[None],
            jnp.transpose(conv_s, (1, 0, 2))[None], C_s[None],
            n_s.reshape(1, Bs, HEADS, HEAD_DIM), m_s[None], S_s[None])
```

```python
import functools

import jax
import jax.numpy as jnp
import numpy as np
from jax import lax
from jax.experimental import pallas as pl
from jax.experimental.pallas import tpu as pltpu

F32 = jnp.float32
BF16 = jnp.bfloat16

D_MODEL = 1024
HEADS = 4
HEAD_DIM = 128
MIX = HEADS * HEAD_DIM
CONV_W = 4
CHUNK = 128
N_MEM = 256
X_HEADS = 4
X_HEAD_DIM = 256
D_FF = 2816
ROPE_THETA = 10000.0
EPS = 1e-6
PAST_LEN = 16384
KV_ROWS = N_MEM * 2 * X_HEADS
QROWS = 2 * X_HEADS

N_QK, N_RQ, N_RK, N_COLS = 0, 1024, 1536, 2048
T_V, T_O, T_RV, T_RG, T_ROWS = 0, 512, 1024, 1536, 2048
GT_ROWS = 16
UOFF = 8
ROW_GROUPS = 2

V7X_VMEM_LIMIT = 56 * 1024 * 1024

NT_DIMS = (((1,), (1,)), ((), ()))
TN_DIMS = (((0,), (0,)), ((), ()))


def _dot(a, b):
    return jnp.dot(a.astype(BF16), b.astype(BF16), preferred_element_type=F32)


def _dot_nt(a, b):
    return lax.dot_general(a.astype(BF16), b.astype(BF16), NT_DIMS, preferred_element_type=F32)


def _dot_tn(a, b):
    return lax.dot_general(a.astype(BF16), b.astype(BF16), TN_DIMS, preferred_element_type=F32)


def _rms(x, g):
    return x * lax.rsqrt(jnp.mean(x * x, axis=-1, keepdims=True) + EPS) * g


def _head_norm(h):
    return h * lax.rsqrt(jnp.mean(h * h, axis=-1, keepdims=True) + EPS)


def _silu(x):
    return x * jax.nn.sigmoid(x)


def _log_sigmoid(x):
    return jnp.minimum(x, 0.0) - jnp.log1p(jnp.exp(-jnp.abs(x)))


def _rope(x, cos_full, sin_signed):
    return x * cos_full + pltpu.roll(x, HEAD_DIM // 2, 1) * sin_signed


def _const_spec(shape):
    zeros = (0,) * len(shape)
    return pl.BlockSpec(shape, lambda *_: zeros, pipeline_mode=pl.Buffered(1))


def _params(*sem):
    return pltpu.CompilerParams(dimension_semantics=sem, vmem_limit_bytes=V7X_VMEM_LIMIT)


def _memkv_kernel(mem_ref, g_ref, wk_ref, wv_ref, kb_ref, vb_ref, krows_ref, vrows_ref):
    mn = _rms(mem_ref[...], g_ref[...]).astype(BF16)
    for w_ref, b_ref, rows_ref in ((wk_ref, kb_ref, krows_ref), (wv_ref, vb_ref, vrows_ref)):
        proj = jnp.dot(mn, w_ref[...], preferred_element_type=F32)
        b_ref[...] = proj.astype(BF16)
        for h in range(X_HEADS):
            for c in range(2):
                lane0 = h * X_HEAD_DIM + c * 128
                rows_ref[pl.ds(c * X_HEADS + h, N_MEM, stride=QROWS), :] = proj[:, lane0:lane0 + 128]


def _memkv(mem, g_mem, wk, wv):
    B = mem.shape[0]
    tok = pl.BlockSpec((None, N_MEM, D_MODEL), lambda b: (b, 0, 0))
    rows = pl.BlockSpec((None, KV_ROWS, 128), lambda b: (b, 0, 0))
    return pl.pallas_call(
        _memkv_kernel,
        grid=(B,),
        in_specs=[tok, _const_spec((1, D_MODEL)),
                  _const_spec((D_MODEL, D_MODEL)), _const_spec((D_MODEL, D_MODEL))],
        out_specs=[tok, tok, rows, rows],
        out_shape=[jax.ShapeDtypeStruct((B, N_MEM, D_MODEL), BF16)] * 2
        + [jax.ShapeDtypeStruct((B, KV_ROWS, 128), F32)] * 2,
        compiler_params=_params("parallel"),
    )(mem, g_mem, wk, wv)


def _mixer_prompt_kernel(x_ref, xs_ref, wn_ref, wt_ref, wgt_ref, wout_ref, gmix_ref, bgt_ref,
                         wconv_ref, bconv_ref, gmh_ref, grh_ref, cos_ref, sin_ref,
                         decay_ref, qdec_ref, kdec_ref, cdec_ref, wg32_ref, wu32_ref, wd32_ref,
                         x1_ref, conv_ref, C_ref, n_ref, m_ref, S_ref, wg16_ref, wu16_ref, wd16_ref,
                         un_a, ut_a, gt_a, un_b, ut_b, gt_b, tail_ref, st_ref, h_ref,
                         *, tt, tiles_per_seq, n_steps):
    wg16_ref[...] = wg32_ref[...].astype(BF16)
    wu16_ref[...] = wu32_ref[...].astype(BF16)
    wd16_ref[...] = wd32_ref[...].astype(BF16)
    L = CHUNK
    s = pl.program_id(0)
    t = (jnp.maximum(s, 1) - 1) % tiles_per_seq
    chunks = range(tt // L)
    slot_a, slot_b = (un_a, ut_a, gt_a), (un_b, ut_b, gt_b)

    @pl.when(s == 0)
    def _():
        tail_ref[...] = jnp.zeros_like(tail_ref)

    @pl.when(t == 0)
    def _():
        C_ref[...] = jnp.zeros_like(C_ref)
        st_ref[...] = jnp.zeros_like(st_ref)
        n_ref[...] = jnp.zeros_like(n_ref)
        m_ref[...] = jnp.zeros_like(m_ref)

    k_scale = HEAD_DIM ** -0.5

    def project(slot, xn, part):
        un_ref, ut_ref, gt_ref = slot
        c0 = part * MIX
        un_ref[UOFF:UOFF + tt, c0:c0 + MIX] = jnp.dot(xn, wn_ref[:, c0:c0 + MIX],
                                                      preferred_element_type=F32)
        res = lax.dot_general(wt_ref[c0:c0 + MIX, :], xn, NT_DIMS, preferred_element_type=F32)
        for c in chunks:
            ut_ref[c, c0:c0 + MIX, :] = res[:, c * L:(c + 1) * L]
        if part == 0:
            gates_t = lax.dot_general(wgt_ref[...], xn, NT_DIMS, preferred_element_type=F32)
            for c in chunks:
                gt_ref[c] = gates_t[:, c * L:(c + 1) * L]

    src_id = lax.broadcasted_iota(jnp.int32, (L, L), 0)
    tgt_id = lax.broadcasted_iota(jnp.int32, (L, L), 1)
    causal = src_id <= tgt_id
    triu_bf = jnp.where(causal, 1.0, 0.0).astype(BF16)
    heads = range(HEADS)
    hcol = lambda base, h: slice(base + h * HEAD_DIM, base + (h + 1) * HEAD_DIM)

    def chunk_body(c, slot):
        un_ref, ut_ref, gt_ref = slot
        r0 = c * L
        rows = pl.ds(UOFF + r0, L)
        trows = pl.ds(r0, L)
        g_t = gt_ref[c] + bgt_ref[...]
        ig = g_t[0:8]
        lf = _log_sigmoid(g_t[8:16])
        lf_hi = lf.astype(BF16)
        r1 = lf - lf_hi.astype(F32)
        lf_mid = r1.astype(BF16)
        lf_lo = (r1 - lf_mid.astype(F32)).astype(BF16)
        bc = (jnp.dot(lf_hi, triu_bf, preferred_element_type=F32)
              + jnp.dot(lf_mid, triu_bf, preferred_element_type=F32)
              + jnp.dot(lf_lo, triu_bf, preferred_element_type=F32))
        m_prev = m_ref[...]
        inter = bc + m_prev
        b_last = bc[:, L - 1:L]
        g_w = b_last - bc + ig
        m_new = jnp.maximum(b_last + m_prev, jnp.max(g_w, axis=1, keepdims=True))
        ws = jnp.exp(g_w - m_new)
        carry = jnp.exp(b_last + m_prev - m_new)
        a_n = jnp.concatenate([ig - bc, jnp.zeros((L - 8, L), F32)], axis=0).T

        def conv_act(col):
            win = un_ref[pl.ds(r0, L + UOFF), col:col + HEAD_DIM]
            acc = bconv_ref[:, col:col + HEAD_DIM]
            for j in range(CONV_W):
                back = CONV_W - 1 - j
                tap = win if back == 0 else pltpu.roll(win, back, 0)
                acc = acc + tap[UOFF:UOFF + L] * wconv_ref[j:j + 1, col:col + HEAD_DIM]
            return _silu(acc)

        cos_f, sin_s = cos_ref[trows, :], sin_ref[trows, :]
        q = [conv_act(N_QK + h * HEAD_DIM) for h in heads]
        kb = [(conv_act(N_QK + MIX + h * HEAD_DIM) * k_scale).astype(BF16) for h in heads]
        qb = [a.astype(BF16) for a in q]
        rqb = [_rope(un_ref[rows, hcol(N_RQ, h)], cos_f, sin_s).astype(BF16) for h in heads]
        rk = [_rope(un_ref[rows, hcol(N_RK, h)], cos_f, sin_s) * k_scale for h in heads]
        v_t = [ut_ref[c, hcol(T_V, h), :] for h in heads]
        rvb = [ut_ref[c, hcol(T_RV, h), :].astype(BF16) for h in heads]
        c_old = [C_ref[h] for h in heads]
        s_old = [st_ref[h] for h in heads]
        n_old = [n_ref[h:h + 1, :] for h in heads]
        qk = [_dot_nt(kb[h], qb[h]) for h in heads]
        att = [_dot_nt(rk[h], rqb[h]) for h in heads]
        c_q = [_dot_nt(c_old[h], qb[h]) for h in heads]
        s_q = [_dot_nt(s_old[h], rqb[h]) for h in heads]
        n_q = [_dot_nt(jnp.broadcast_to(n_old[h], (8, HEAD_DIM)), qb[h])[0:1] for h in heads]
        d_c = [_dot(v_t[h] * ws[h:h + 1, :], kb[h]) for h in heads]
        d_n = [_dot(jnp.broadcast_to(ws[h:h + 1, :], (8, L)), kb[h])[0:1] for h in heads]
        d_s = [_dot(rvb[h], rk[h] * kdec_ref[h]) for h in heads]
        m_t, w_in, wts = [], [], []
        for h in heads:
            dmat = a_n[:, h:h + 1] + bc[h:h + 1, :]
            dmat = jnp.where(causal, dmat, -jnp.inf)
            m_t.append(jnp.maximum(inter[h:h + 1, :], jnp.max(dmat, axis=0, keepdims=True)))
            wts.append(jnp.exp(dmat - m_t[h]) * qk[h])
            w_in.append(jnp.exp(inter[h:h + 1, :] - m_t[h]))
        att_w = [att[h] * decay_ref[h] for h in heads]
        v_p = [_dot(v_t[h], wts[h]) for h in heads]
        v_a = [_dot(rvb[h], att_w[h]) for h in heads]
        for h in heads:
            num = v_p[h] + w_in[h] * c_q[h]
            den = jnp.sum(wts[h], axis=0, keepdims=True) + w_in[h] * n_q[h]
            hm = num / jnp.maximum(jnp.abs(den), jnp.exp(-m_t[h]))
            hm = hm * lax.rsqrt(jnp.mean(hm * hm, axis=0, keepdims=True) + EPS)
            hm = hm * gmh_ref[hcol(0, h), :] * jax.nn.sigmoid(ut_ref[c, hcol(T_O, h), :])
            h_ref[trows, hcol(0, h)] = hm.T.astype(BF16)
            carry_h = carry[h:h + 1, :]
            C_ref[h] = carry_h * c_old[h] + d_c[h]
            n_ref[h:h + 1, :] = carry_h * n_old[h] + d_n[h]
            o = v_a[h] + qdec_ref[h:h + 1, :] * s_q[h]
            st_ref[h] = cdec_ref[h:h + 1, :] * s_old[h] + d_s[h]
            hr = o * lax.rsqrt(jnp.mean(o * o, axis=0, keepdims=True) + EPS)
            hr = hr * grh_ref[hcol(0, h), :] * _silu(ut_ref[c, hcol(T_RG, h), :])
            h_ref[trows, hcol(MIX, h)] = hr.T.astype(BF16)

        m_ref[...] = m_new

    def step(proj_slot, scan_slot):
        if scan_slot is not None:
            un_ref = scan_slot[0]
            un_ref[0:UOFF, N_QK:N_RQ] = jnp.where(t == 0, 0.0, tail_ref[...])
        if proj_slot is not None:
            xn = _rms(x_ref[...], gmix_ref[...]).astype(BF16)
        for c in chunks:
            if proj_slot is not None:
                project(proj_slot, xn, c)
            if scan_slot is not None:
                chunk_body(c, scan_slot)
        if scan_slot is not None:
            tail_ref[...] = un_ref[tt:tt + UOFF, N_QK:N_RQ]
            x1_ref[...] = xs_ref[...] + jnp.dot(h_ref[...], wout_ref[...], preferred_element_type=F32)

    last = n_steps - 1
    inner = (s > 0) & (s < last)

    @pl.when(s == 0)
    def _():
        step(slot_a, None)

    @pl.when(inner & (s % 2 == 0))
    def _():
        step(slot_a, slot_b)

    @pl.when(inner & (s % 2 == 1))
    def _():
        step(slot_b, slot_a)

    @pl.when(s == last)
    def _():
        step(None, slot_a if (last - 1) % 2 == 0 else slot_b)

    @pl.when(t == tiles_per_seq - 1)
    def _():
        conv_ref[...] = tail_ref[UOFF - (CONV_W - 1):UOFF, :]
        for h in heads:
            S_ref[h] = st_ref[h].T


def _mixer_prompt(x, wn, wt, wgt, w_out, g_mix, bgt, w_conv, b_conv, gmh_cols, grh_cols, cos_f, sin_s,
                  decay_t, qdec_rows, kdec_cols, cdec_rows, w_gate, w_up, w_down, tt=512):
    B, T, _ = x.shape
    tps = T // tt
    n_tiles = B * tps
    proj = lambda s: jnp.minimum(s, n_tiles - 1)
    scan = lambda s: jnp.maximum(s, 1) - 1
    per_b3 = lambda s: (scan(s) // tps, 0, 0)
    per_b4 = lambda s: (scan(s) // tps, 0, 0, 0)
    scan_tile = pl.BlockSpec((None, tt, D_MODEL), lambda s: (scan(s) // tps, scan(s) % tps, 0))
    in_specs = [
        pl.BlockSpec((None, tt, D_MODEL), lambda s: (proj(s) // tps, proj(s) % tps, 0)),
        scan_tile,
        _const_spec((D_MODEL, N_COLS)), _const_spec((T_ROWS, D_MODEL)), _const_spec((GT_ROWS, D_MODEL)),
        _const_spec((D_MODEL, D_MODEL)),
        _const_spec((1, D_MODEL)), _const_spec((GT_ROWS, CHUNK)),
        _const_spec((CONV_W, 2 * MIX)), _const_spec((1, 2 * MIX)),
        _const_spec((MIX, CHUNK)), _const_spec((MIX, CHUNK)),
        pl.BlockSpec((tt, HEAD_DIM), lambda s: (scan(s) % tps, 0)),
        pl.BlockSpec((tt, HEAD_DIM), lambda s: (scan(s) % tps, 0)),
        _const_spec((HEADS, CHUNK, CHUNK)),
        _const_spec((8, CHUNK)), _const_spec((HEADS, CHUNK, HEAD_DIM)), _const_spec((8, HEAD_DIM)),
    ]
    ff_blocks = D_FF // HEAD_DIM
    assert ff_blocks <= n_tiles + 1
    early = lambda s: jnp.minimum(s, ff_blocks - 1)
    late = lambda s: jnp.maximum(s - (n_tiles + 1 - ff_blocks), 0)
    cast_specs = [pl.BlockSpec((D_MODEL, HEAD_DIM), lambda s: (0, early(s))),
                  pl.BlockSpec((D_MODEL, HEAD_DIM), lambda s: (0, late(s))),
                  pl.BlockSpec((HEAD_DIM, D_MODEL), lambda s: (late(s), 0))]
    in_specs += cast_specs
    out_specs = [
        scan_tile,
        pl.BlockSpec((None, CONV_W - 1, 2 * MIX), per_b3),
        pl.BlockSpec((None, HEADS, HEAD_DIM, HEAD_DIM), per_b4),
        pl.BlockSpec((None, HEADS, HEAD_DIM), per_b3),
        pl.BlockSpec((None, 8, CHUNK), per_b3),
        pl.BlockSpec((None, HEADS, HEAD_DIM, HEAD_DIM), per_b4),
    ] + cast_specs
    out_shape = [
        jax.ShapeDtypeStruct((B, T, D_MODEL), F32),
        jax.ShapeDtypeStruct((B, CONV_W - 1, 2 * MIX), F32),
        jax.ShapeDtypeStruct((B, HEADS, HEAD_DIM, HEAD_DIM), F32),
        jax.ShapeDtypeStruct((B, HEADS, HEAD_DIM), F32),
        jax.ShapeDtypeStruct((B, 8, CHUNK), F32),
        jax.ShapeDtypeStruct((B, HEADS, HEAD_DIM, HEAD_DIM), F32),
        jax.ShapeDtypeStruct((D_MODEL, D_FF), BF16), jax.ShapeDtypeStruct((D_MODEL, D_FF), BF16),
        jax.ShapeDtypeStruct((D_FF, D_MODEL), BF16),
    ]
    n_chunks = tt // CHUNK
    slot = [
        pltpu.VMEM((tt + UOFF, N_COLS), F32),
        pltpu.VMEM((n_chunks, T_ROWS, CHUNK), F32),
        pltpu.VMEM((n_chunks, GT_ROWS, CHUNK), F32),
    ]
    scratch = slot + slot + [
        pltpu.VMEM((UOFF, 2 * MIX), F32),
        pltpu.VMEM((HEADS, HEAD_DIM, HEAD_DIM), F32),
        pltpu.VMEM((tt, D_MODEL), BF16),
    ]
    return pl.pallas_call(
        functools.partial(_mixer_prompt_kernel, tt=tt, tiles_per_seq=tps, n_steps=n_tiles + 1),
        grid=(n_tiles + 1,),
        in_specs=in_specs, out_specs=out_specs, out_shape=out_shape,
        scratch_shapes=scratch,
        compiler_params=_params("arbitrary"),
    )(x, x, wn, wt, wgt, w_out, g_mix, bgt, w_conv, b_conv, gmh_cols, grh_cols, cos_f, sin_s,
      decay_t, qdec_rows, kdec_cols, cdec_rows, w_gate, w_up, w_down)


def _attn_prompt_kernel(x1_ref, k_ref, v_ref, wcq_ref, gx_ref, *rest, n_sample_steps):
    sample_in, o_ref = rest[:SAMPLE_MIXER_INPUTS], rest[SAMPLE_MIXER_INPUTS]
    sample_out_and_scratch = rest[SAMPLE_MIXER_INPUTS + 1:]
    step = pl.program_id(0) * pl.num_programs(1) + pl.program_id(1)

    @pl.when(step < n_sample_steps)
    def _():
        _mixer_sample_kernel(*sample_in, *sample_out_and_scratch, bb=SAMPLE_BLOCK)

    tm = x1_ref.shape[0]
    rows = [slice(r, r + tm // ROW_GROUPS) for r in range(0, tm, tm // ROW_GROUPS)]
    xq = [_rms(x1_ref[r, :], gx_ref[...]).astype(BF16) for r in rows]
    q = [jnp.dot(a, wcq_ref[...], preferred_element_type=F32).astype(BF16) for a in xq]
    sl = [slice(h * X_HEAD_DIM, (h + 1) * X_HEAD_DIM) for h in range(X_HEADS)]
    items = [(g, h) for g in range(ROW_GROUPS) for h in range(X_HEADS)]
    s = [_dot_nt(q[g][:, sl[h]], k_ref[:, sl[h]]) * (X_HEAD_DIM ** -0.5) for g, h in items]
    e = [jnp.exp(a - jnp.max(a, axis=-1, keepdims=True)) for a in s]
    p = [a / jnp.sum(a, axis=-1, keepdims=True) for a in e]
    for (g, h), a in zip(items, p):
        o_ref[rows[g], sl[h]] = _dot(a, v_ref[:, sl[h]]).astype(BF16)


def _attn_prompt(x1, mk, mv, w_cq, g_x, sample_mixer_operands, tm=512):
    B, T, _ = x1.shape
    tps = T // tm
    Bs = sample_mixer_operands[0].shape[0]
    n_sample_steps = Bs // SAMPLE_BLOCK
    assert n_sample_steps <= B * tps
    block_of = lambda b, t: jnp.minimum(b * tps + t, n_sample_steps - 1)
    s_in, s_out, s_shape, s_scratch = _sample_mixer_specs(Bs, block_of)
    tile = pl.BlockSpec((None, tm, D_MODEL), lambda b, t: (b, t, 0))
    kv = pl.BlockSpec((None, N_MEM, D_MODEL), lambda b, t: (b, 0, 0))
    outs = pl.pallas_call(
        functools.partial(_attn_prompt_kernel, n_sample_steps=n_sample_steps),
        grid=(B, tps),
        in_specs=[tile, kv, kv, _const_spec((D_MODEL, D_MODEL)), _const_spec((1, D_MODEL))] + s_in,
        out_specs=[tile] + s_out,
        out_shape=[jax.ShapeDtypeStruct((B, T, D_MODEL), BF16)] + s_shape,
        scratch_shapes=s_scratch,
        compiler_params=_params("arbitrary", "arbitrary"),
    )(x1, mk, mv, w_cq, g_x, *sample_mixer_operands)
    return outs[0], outs[1:]


V7X_MXU_DIM = 256
FF_CHUNKS = ((0, 6 * V7X_MXU_DIM), (6 * V7X_MXU_DIM, D_FF))


def _ffn_kernel(x1_ref, o_ref, wco_ref, wg_ref, wu_ref, wd_ref, gffn_ref, gfin_ref, *rest, n_groups):
    if len(rest) == 1:
        (y_ref,) = rest
        sample_stages = iter(())
    else:
        qs_ref, ck_ref, cv_ref, y_ref, os_ref = rest
        sample_stages = _sample_attention(qs_ref, ck_ref, cv_ref, os_ref)
    tm = x1_ref.shape[0]
    rows = [slice(r, r + tm // n_groups) for r in range(0, tm, tm // n_groups)]
    acc = [x1_ref[r, :] + _dot(o_ref[r, :], wco_ref[...]) for r in rows]
    hf = [_rms(a, gffn_ref[...]).astype(BF16) for a in acc]
    for c0, c1 in FF_CHUNKS:
        next(sample_stages, None)
        gate = [jnp.dot(h, wg_ref[:, c0:c1], preferred_element_type=F32) for h in hf]
        up = [jnp.dot(h, wu_ref[:, c0:c1], preferred_element_type=F32) for h in hf]
        act = [_silu(g) * u for g, u in zip(gate, up)]
        acc = [a + _dot(p, wd_ref[c0:c1, :]) for a, p in zip(acc, act)]
    next(sample_stages, None)
    for r, a in zip(rows, acc):
        y_ref[r, :] = _rms(a, gfin_ref[...])


def _ffn(x1, o, w_co, w_gate, w_up, w_down, g_ffn, g_final, tm, sample=None):
    rows = x1.shape[0]
    steps = rows // tm
    row_spec = pl.BlockSpec((tm, D_MODEL), lambda i: (i, 0))
    in_specs = [row_spec, row_spec, _const_spec((D_MODEL, D_MODEL)),
                _const_spec((D_MODEL, D_FF)), _const_spec((D_MODEL, D_FF)),
                _const_spec((D_FF, D_MODEL)), _const_spec((1, D_MODEL)), _const_spec((1, D_MODEL))]
    out_specs = [row_spec]
    out_shape = [jax.ShapeDtypeStruct((rows, D_MODEL), F32)]
    operands = [x1, o, w_co, w_gate, w_up, w_down, g_ffn, g_final]
    if sample is not None:
        q_s, ck_s, cv_s = sample
        Bs = q_s.shape[0]
        ba = Bs // steps
        assert ba * steps == Bs
        s_rows = pl.BlockSpec((ba, QROWS, 128), lambda i: (i, 0, 0))
        s_kv = pl.BlockSpec((ba, KV_ROWS, 128), lambda i: (i, 0, 0))
        in_specs += [s_rows, s_kv, s_kv]
        out_specs += [s_rows]
        out_shape += [jax.ShapeDtypeStruct((Bs, QROWS, 128), F32)]
        operands += [q_s.reshape(Bs, QROWS, 128), ck_s, cv_s]
    outs = pl.pallas_call(
        functools.partial(_ffn_kernel, n_groups=ROW_GROUPS if tm >= 512 else 1),
        grid=(steps,),
        in_specs=in_specs, out_specs=out_specs, out_shape=out_shape,
        compiler_params=_params("parallel"),
    )(*operands)
    if sample is None:
        return outs[0]
    return outs[0], outs[1].reshape(Bs, D_MODEL)


def _inproj_sample_kernel(x_ref, g_ref, wn_ref, wt_ref, wgt_ref, un_ref, ut_ref, gs_ref):
    xn = _rms(x_ref[...], g_ref[...]).astype(BF16)
    un_ref[...] = jnp.dot(xn, wn_ref[...], preferred_element_type=F32)
    ut_ref[...] = lax.dot_general(xn, wt_ref[...], NT_DIMS, preferred_element_type=F32)
    gs_ref[...] = lax.dot_general(xn, wgt_ref[...], NT_DIMS, preferred_element_type=F32)


def _inproj_sample(x, g_mix, wn, wt, wgt):
    rows = x.shape[0]
    full = lambda cols: pl.BlockSpec((rows, cols), lambda i: (0, 0))
    return pl.pallas_call(
        _inproj_sample_kernel,
        grid=(1,),
        in_specs=[_const_spec((rows, D_MODEL)), _const_spec((1, D_MODEL)),
                  _const_spec((D_MODEL, N_COLS)), _const_spec((T_ROWS, D_MODEL)),
                  _const_spec((GT_ROWS, D_MODEL))],
        out_specs=[full(N_COLS), full(T_ROWS), full(GT_ROWS)],
        out_shape=[jax.ShapeDtypeStruct((rows, N_COLS), F32),
                   jax.ShapeDtypeStruct((rows, T_ROWS), F32),
                   jax.ShapeDtypeStruct((rows, GT_ROWS), F32)],
        compiler_params=_params("arbitrary"),
    )(x, g_mix, wn, wt, wgt)


def _mixer_sample_kernel(un_ref, ut_ref, gs_ref, conv_ref, C_ref, n_ref, m_ref, S_ref, bg_ref,
                         wconv_ref, bconv_ref, gmh_ref, grh_ref, cos_ref, sin_ref, rtab_ref,
                         h_ref, convo_ref, Co_ref, no_ref, mo_ref, So_ref,
                         q_s, k_s, vws_s, qr_s, kr_s, qc_s, qs_s, carry_s, *, bb):
    k_scale = HEAD_DIM ** -0.5
    uqk = un_ref[:, N_QK:N_RQ]
    conv = (bconv_ref[...] + wconv_ref[0:1, :] * conv_ref[0] + wconv_ref[1:2, :] * conv_ref[1]
            + wconv_ref[2:3, :] * conv_ref[2] + wconv_ref[3:4, :] * uqk)
    convo_ref[0] = conv_ref[1]
    convo_ref[1] = conv_ref[2]
    convo_ref[2] = uqk
    qk_act = _silu(conv)
    q_s[...] = qk_act[:, 0:MIX]
    k_s[...] = qk_act[:, MIX:2 * MIX] * k_scale

    gates = gs_ref[...] + bg_ref[...]
    ig = gates[:, 0:HEADS]
    lf = _log_sigmoid(gates[:, 8:8 + HEADS])
    inter = lf + m_ref[...]
    m_t = jnp.maximum(inter, ig)
    ws = jnp.exp(ig - m_t)
    w_in = jnp.exp(inter - m_t)
    mo_ref[...] = m_t
    carry_s[:, 0:HEADS] = w_in

    cos_f = cos_ref[...]
    sin_s = sin_ref[...]
    for h in range(HEADS):
        lo = h * HEAD_DIM
        hs = slice(lo, lo + HEAD_DIM)
        vws_s[:, hs] = ut_ref[:, T_V + lo:T_V + lo + HEAD_DIM] * ws[:, h:h + 1]
        qr_s[:, hs] = _rope(un_ref[:, N_RQ + lo:N_RQ + lo + HEAD_DIM], cos_f, sin_s)
        kr_s[:, hs] = _rope(un_ref[:, N_RK + lo:N_RK + lo + HEAD_DIM], cos_f, sin_s) * k_scale

    heads = range(HEADS)
    seqs = range(bb)
    hsl = [slice(h * HEAD_DIM, (h + 1) * HEAD_DIM) for h in heads]
    seq_id = lax.broadcasted_iota(jnp.int32, (bb, HEAD_DIM), 0)
    for h in heads:
        q_h, k_h, vws_h = q_s[:, hsl[h]], k_s[:, hsl[h]], vws_s[:, hsl[h]]
        qr_h = qr_s[:, hsl[h]]
        krd_h = kr_s[:, hsl[h]] * rtab_ref[1:2, h:h + 1]
        rv_h = ut_ref[:, T_RV + h * HEAD_DIM:T_RV + (h + 1) * HEAD_DIM]
        c_old = [C_ref[b, h] for b in seqs]
        s_old = [S_ref[b, h] for b in seqs]
        q_c = [_dot_nt(q_h, c_old[b]) for b in seqs]
        q_st = [_dot(qr_h, s_old[b]) for b in seqs]
        d_c = [_dot_tn(jnp.where(seq_id == b, vws_h, 0.0), k_h) for b in seqs]
        d_s = [_dot_tn(jnp.where(seq_id == b, krd_h, 0.0), rv_h) for b in seqs]
        for b in seqs:
            row = slice(b, b + 1)
            qc_s[row, hsl[h]] = q_c[b][row]
            qs_s[row, hsl[h]] = q_st[b][row]
            Co_ref[b, h] = carry_s[row, h:h + 1] * c_old[b] + d_c[b]
            So_ref[b, h] = rtab_ref[2:3, h:h + 1] * s_old[b] + d_s[b]

    for h in range(HEADS):
        lo = h * HEAD_DIM
        hs = slice(lo, lo + HEAD_DIM)
        q_h, k_h, n_h = q_s[:, hs], k_s[:, hs], n_ref[:, hs]
        ws_h, w_in_h = ws[:, h:h + 1], w_in[:, h:h + 1]
        wts = ws_h * jnp.sum(q_h * k_h, axis=1, keepdims=True)
        num = wts * ut_ref[:, T_V + lo:T_V + lo + HEAD_DIM] + w_in_h * qc_s[:, hs]
        den = wts + w_in_h * jnp.sum(n_h * q_h, axis=1, keepdims=True)
        hm = num / jnp.maximum(jnp.abs(den), jnp.exp(-m_t[:, h:h + 1]))
        mo = ut_ref[:, T_O + lo:T_O + lo + HEAD_DIM]
        h_ref[:, hs] = _head_norm(hm) * gmh_ref[:, hs] * jax.nn.sigmoid(mo)
        no_ref[:, hs] = w_in_h * n_h + ws_h * k_h

        att = jnp.sum(qr_s[:, hs] * kr_s[:, hs], axis=1, keepdims=True) * rtab_ref[3:4, h:h + 1]
        o = (att * ut_ref[:, T_RV + lo:T_RV + lo + HEAD_DIM]
             + rtab_ref[0:1, h:h + 1] * qs_s[:, hs])
        rg = ut_ref[:, T_RG + lo:T_RG + lo + HEAD_DIM]
        h_ref[:, MIX + lo:MIX + lo + HEAD_DIM] = _head_norm(o) * grh_ref[:, hs] * _silu(rg)


SAMPLE_BLOCK = 8
SAMPLE_MIXER_INPUTS = 16


def _sample_mixer_specs(B, block_of):
    bb = SAMPLE_BLOCK
    rows = lambda cols: pl.BlockSpec((bb, cols), lambda *g: (block_of(*g), 0))
    mats = pl.BlockSpec((bb, HEADS, HEAD_DIM, HEAD_DIM), lambda *g: (block_of(*g), 0, 0, 0))
    convs = pl.BlockSpec((CONV_W - 1, bb, 2 * MIX), lambda *g: (0, block_of(*g), 0))
    in_specs = [rows(N_COLS), rows(T_ROWS), rows(GT_ROWS), convs, mats, rows(MIX), rows(HEADS), mats,
                _const_spec((1, GT_ROWS)), _const_spec((CONV_W, 2 * MIX)), _const_spec((1, 2 * MIX)),
                _const_spec((1, MIX)), _const_spec((1, MIX)),
                _const_spec((1, HEAD_DIM)), _const_spec((1, HEAD_DIM)), _const_spec((8, HEAD_DIM))]
    assert len(in_specs) == SAMPLE_MIXER_INPUTS
    out_specs = [rows(D_MODEL), convs, mats, rows(MIX), rows(HEADS), mats]
    out_shape = [jax.ShapeDtypeStruct((B, D_MODEL), F32),
                 jax.ShapeDtypeStruct((CONV_W - 1, B, 2 * MIX), F32),
                 jax.ShapeDtypeStruct((B, HEADS, HEAD_DIM, HEAD_DIM), F32),
                 jax.ShapeDtypeStruct((B, MIX), F32),
                 jax.ShapeDtypeStruct((B, HEADS), F32),
                 jax.ShapeDtypeStruct((B, HEADS, HEAD_DIM, HEAD_DIM), F32)]
    scratch = [pltpu.VMEM((bb, MIX), F32) for _ in range(7)] + [pltpu.VMEM((bb, HEAD_DIM), F32)]
    return in_specs, out_specs, out_shape, scratch


def _outq_sample_kernel(x_ref, h_ref, wout_ref, wcq_ref, gx_ref, x1_ref, q_ref):
    x1 = x_ref[...] + _dot(h_ref[...], wout_ref[...])
    x1_ref[...] = x1
    q_ref[...] = _dot(_rms(x1, gx_ref[...]), wcq_ref[...])


def _outq_sample(x, hcat, w_out, w_cq, g_x):
    rows = x.shape[0]
    full = _const_spec((rows, D_MODEL))
    return pl.pallas_call(
        _outq_sample_kernel,
        grid=(1,),
        in_specs=[full, full, _const_spec((D_MODEL, D_MODEL)), _const_spec((D_MODEL, D_MODEL)),
                  _const_spec((1, D_MODEL))],
        out_specs=[pl.BlockSpec((rows, D_MODEL), lambda i: (0, 0))] * 2,
        out_shape=[jax.ShapeDtypeStruct((rows, D_MODEL), F32)] * 2,
        compiler_params=_params("arbitrary"),
    )(x, hcat, w_out, w_cq, g_x)


def _sample_attention(q_ref, k_ref, v_ref, o_ref):
    r_id = lax.broadcasted_iota(jnp.int32, (QROWS, KV_ROWS), 0)
    n_id = lax.broadcasted_iota(jnp.int32, (QROWS, KV_ROWS), 1)
    own = (n_id & 7) == (r_id >> 1) + 4 * (r_id & 1)
    low_half = (lax.broadcasted_iota(jnp.int32, (1, KV_ROWS), 1) & 4) == 0
    seqs = range(q_ref.shape[0])
    z = [_dot_nt(q_ref[j], k_ref[j]) for j in seqs]
    zc = [jnp.sum(jnp.where(own, z[j], 0.0), axis=0, keepdims=True) for j in seqs]
    yield
    other = [jnp.where(low_half, pltpu.roll(zc[j], KV_ROWS - 4, 1), pltpu.roll(zc[j], 4, 1))
             for j in seqs]
    s = [jnp.where(own, (zc[j] + other[j]) * (X_HEAD_DIM ** -0.5), -jnp.inf) for j in seqs]
    m = [jnp.max(s[j], axis=-1, keepdims=True) for j in seqs]
    e = [jnp.exp(s[j] - m[j]) for j in seqs]
    den = [jnp.sum(e[j], axis=-1, keepdims=True) for j in seqs]
    yield
    for j in seqs:
        o_ref[j] = _dot(e[j] / den[j], v_ref[j])
    yield


def _kv_rows(cache):
    B = cache.shape[0]
    c5 = cache.reshape(B, N_MEM, X_HEADS, 2, 128)
    return jnp.transpose(c5, (0, 1, 3, 2, 4)).reshape(B, KV_ROWS, 128)


def _kv_from_rows(rows):
    B = rows.shape[0]
    r5 = rows.reshape(B, N_MEM, 2, X_HEADS, 128)
    return jnp.transpose(r5, (0, 1, 3, 2, 4)).reshape(1, B, N_MEM, X_HEADS, X_HEAD_DIM)


def _rope_tables(pos):
    half = HEAD_DIM // 2
    inv = ROPE_THETA ** (-np.arange(half, dtype=np.float64) / half)
    ang = np.asarray(pos, np.float64)[:, None] * inv[None, :]
    cos, sin = np.cos(ang), np.sin(ang)
    return (np.concatenate([cos, cos], axis=-1).astype(np.float32),
            np.concatenate([-sin, sin], axis=-1).astype(np.float32))


def _retention_tables(L):
    lg = np.log1p(-np.exp2(-5.0 - np.arange(HEADS, dtype=np.float64)))
    t = np.arange(L, dtype=np.float64)
    diff = t[:, None] - t[None, :]
    decay = np.where(diff >= 0, np.exp(lg[:, None, None] * np.maximum(diff, 0.0)), 0.0)
    q_dec = np.exp(lg[:, None] * (t + 1.0))
    k_dec = np.exp(lg[:, None] * (L - 1.0 - t))
    chunk_dec = np.exp(lg * L)
    return tuple(a.astype(np.float32) for a in (decay, q_dec, k_dec, chunk_dec))


def _lanes(a, n):
    xp = np if isinstance(a, np.ndarray) else jnp
    return xp.broadcast_to(a[..., None], a.shape + (n,))


def _pad_rows(a, rows):
    return np.pad(a, ((0, rows - a.shape[0]),) + ((0, 0),) * (a.ndim - 1))


def kernel(x_prompt, x_sample, cache_mem_k, cache_mem_v, state_mlstm_conv, state_mlstm_C, state_mlstm_n, state_mlstm_m, state_ret_S, mem_prompt, w_in, b_gate, w_conv, b_conv, g_mix, g_mhead, g_rhead, w_out, g_xattn, g_mem, w_ck, w_cv, w_cq, w_co, g_ffn, w_gate, w_up, w_down, g_final):
    Bp, Tp, _ = x_prompt.shape
    Bs = x_sample.shape[0]
    l = 0
    n_m = 4 * MIX
    wi = w_in[l]
    w_gates = wi[:, n_m:n_m + 2 * HEADS]
    w_ret = wi[:, n_m + 2 * HEADS:]
    row = lambda a: a.reshape(1, -1)
    bf = lambda a: a.astype(BF16)
    g_mix_r, g_mh_r, g_rh_r = row(g_mix[l]), row(g_mhead[l]), row(g_rhead[l])
    g_x_r, g_mem_r, g_ffn_r, g_fin_r = row(g_xattn[l]), row(g_mem[l]), row(g_ffn[l]), row(g_final)
    b_conv_r = row(b_conv[l])
    w_out_b, w_cq_b, w_co_b = bf(w_out[l]), bf(w_cq[l]), bf(w_co[l])

    mk_b, mv_b, mk_rows, mv_rows = _memkv(mem_prompt, g_mem_r, bf(w_ck[l]), bf(w_cv[l]))
    wn = bf(jnp.concatenate([wi[:, :2 * MIX], w_ret[:, :2 * MIX]], axis=1))
    wt = bf(jnp.concatenate([wi[:, 2 * MIX:n_m], w_ret[:, 2 * MIX:]], axis=1).T)
    gate_rows = lambda a: jnp.concatenate(
        [a[:HEADS], jnp.zeros((8 - HEADS,) + a.shape[1:], F32),
         a[HEADS:], jnp.zeros((8 - HEADS,) + a.shape[1:], F32)], axis=0)
    wgt = bf(gate_rows(w_gates.T))
    bg_rows = gate_rows(b_gate[l][:, None])
    bgt = jnp.broadcast_to(bg_rows, (GT_ROWS, CHUNK))
    cos_p, sin_p = _rope_tables(np.arange(Tp))
    decay, q_dec, k_dec, chunk_dec = _retention_tables(CHUNK)
    x1_p, conv_p, C_p, n_p, m_p, S_p, w_gate_b, w_up_b, w_down_b = _mixer_prompt(
        x_prompt, wn, wt, wgt, w_out_b, g_mix_r, bgt, w_conv[l], b_conv_r,
        _lanes(g_mhead[l], CHUNK), _lanes(g_rhead[l], CHUNK), cos_p, sin_p,
        np.swapaxes(decay, 1, 2), _pad_rows(q_dec, 8), _lanes(k_dec, HEAD_DIM),
        _pad_rows(_lanes(chunk_dec, HEAD_DIM), 8), w_gate[l], w_up[l], w_down[l])

    xs = x_sample.reshape(Bs, D_MODEL)
    un_s, ut_s, gs_s = _inproj_sample(xs, g_mix_r, wn, wt, wgt)
    cos_s, sin_s = _rope_tables(PAST_LEN + np.arange(1))
    decay1, q_dec1, k_dec1, chunk_dec1 = _retention_tables(1)
    rtab = np.zeros((8, HEAD_DIM), np.float32)
    rtab[:4, :HEADS] = np.stack([q_dec1[:, 0], k_dec1[:, 0], chunk_dec1, decay1[:, 0, 0]])
    sample_mixer_operands = (
        un_s, ut_s, gs_s, jnp.transpose(state_mlstm_conv[l], (1, 0, 2)), state_mlstm_C[l],
        state_mlstm_n[l].reshape(Bs, MIX), state_mlstm_m[l], state_ret_S[l],
        bg_rows.reshape(1, GT_ROWS), w_conv[l], b_conv_r, g_mh_r, g_rh_r, cos_s, sin_s, rtab)

    o_p, (hcat_s, conv_s, C_s, n_s, m_s, S_s) = _attn_prompt(
        x1_p, mk_b, mv_b, w_cq_b, g_x_r, sample_mixer_operands)
    x1_s, q_s = _outq_sample(xs, hcat_s, w_out_b, w_cq_b, g_x_r)
    y_p, o_s = _ffn(x1_p.reshape(Bp * Tp, D_MODEL), o_p.reshape(Bp * Tp, D_MODEL),
                    w_co_b, w_gate_b, w_up_b, w_down_b, g_ffn_r, g_fin_r, tm=512,
                    sample=(q_s, _kv_rows(cache_mem_k[l]), _kv_rows(cache_mem_v[l])))
    y_s = _ffn(x1_s, o_s, w_co_b, w_gate_b, w_up_b, w_down_b, g_ffn_r, g_fin_r, tm=Bs)

    return (y_p.reshape(Bp, Tp, D_MODEL), y_s.reshape(Bs, 1, D_MODEL),
            _kv_from_rows(mk_rows), _kv_from_rows(mv_rows),
            conv_p[None], C_p[None], n_p[None], m_p[None, :, :HEADS, 0], S_p[None],
            jnp.transpose(conv_s, (1, 0, 2))[None], C_s[None],
            n_s.reshape(1, Bs, HEADS, HEAD_DIM), m_s[None], S_s[None])
```

```python
import functools

import jax
import jax.numpy as jnp
import numpy as np
from jax import lax
from jax.experimental import pallas as pl
from jax.experimental.pallas import tpu as pltpu

F32 = jnp.float32
BF16 = jnp.bfloat16

D_MODEL = 1024
HEADS = 4
HEAD_DIM = 128
MIX = HEADS * HEAD_DIM
CONV_W = 4
CHUNK = 128
N_MEM = 256
X_HEADS = 4
X_HEAD_DIM = 256
D_FF = 2816
ROPE_THETA = 10000.0
EPS = 1e-6
PAST_LEN = 16384
KV_ROWS = N_MEM * 2 * X_HEADS
QROWS = 2 * X_HEADS

N_QK, N_RQ, N_RK, N_COLS = 0, 1024, 1536, 2048
T_V, T_O, T_RV, T_RG, T_ROWS = 0, 512, 1024, 1536, 2048
GT_ROWS = 16
UOFF = 8
ROW_GROUPS = 2

V7X_VMEM_LIMIT = 56 * 1024 * 1024

NT_DIMS = (((1,), (1,)), ((), ()))
TN_DIMS = (((0,), (0,)), ((), ()))


def _dot(a, b):
    return jnp.dot(a.astype(BF16), b.astype(BF16), preferred_element_type=F32)


def _dot_nt(a, b):
    return lax.dot_general(a.astype(BF16), b.astype(BF16), NT_DIMS, preferred_element_type=F32)


def _dot_tn(a, b):
    return lax.dot_general(a.astype(BF16), b.astype(BF16), TN_DIMS, preferred_element_type=F32)


def _rms(x, g):
    return x * lax.rsqrt(jnp.mean(x * x, axis=-1, keepdims=True) + EPS) * g


def _head_norm(h):
    return h * lax.rsqrt(jnp.mean(h * h, axis=-1, keepdims=True) + EPS)


def _silu(x):
    return x * jax.nn.sigmoid(x)


def _log_sigmoid(x):
    return jnp.minimum(x, 0.0) - jnp.log1p(jnp.exp(-jnp.abs(x)))


def _rope(x, cos_full, sin_signed):
    return x * cos_full + pltpu.roll(x, HEAD_DIM // 2, 1) * sin_signed


def _const_spec(shape):
    zeros = (0,) * len(shape)
    return pl.BlockSpec(shape, lambda *_: zeros, pipeline_mode=pl.Buffered(1))


def _params(*sem):
    return pltpu.CompilerParams(dimension_semantics=sem, vmem_limit_bytes=V7X_VMEM_LIMIT)


def _memkv_kernel(mem_ref, g_ref, wk_ref, wv_ref, kb_ref, vb_ref, krows_ref, vrows_ref):
    mn = _rms(mem_ref[...], g_ref[...]).astype(BF16)
    for w_ref, b_ref, rows_ref in ((wk_ref, kb_ref, krows_ref), (wv_ref, vb_ref, vrows_ref)):
        proj = jnp.dot(mn, w_ref[...], preferred_element_type=F32)
        b_ref[...] = proj.astype(BF16)
        for h in range(X_HEADS):
            for c in range(2):
                lane0 = h * X_HEAD_DIM + c * 128
                rows_ref[pl.ds(c * X_HEADS + h, N_MEM, stride=QROWS), :] = proj[:, lane0:lane0 + 128]


def _memkv(mem, g_mem, wk, wv):
    B = mem.shape[0]
    tok = pl.BlockSpec((None, N_MEM, D_MODEL), lambda b: (b, 0, 0))
    rows = pl.BlockSpec((None, KV_ROWS, 128), lambda b: (b, 0, 0))
    return pl.pallas_call(
        _memkv_kernel,
        grid=(B,),
        in_specs=[tok, _const_spec((1, D_MODEL)),
                  _const_spec((D_MODEL, D_MODEL)), _const_spec((D_MODEL, D_MODEL))],
        out_specs=[tok, tok, rows, rows],
        out_shape=[jax.ShapeDtypeStruct((B, N_MEM, D_MODEL), BF16)] * 2
        + [jax.ShapeDtypeStruct((B, KV_ROWS, 128), F32)] * 2,
        compiler_params=_params("parallel"),
    )(mem, g_mem, wk, wv)


def _mixer_prompt_kernel(x_ref, xs_ref, wn_ref, wt_ref, wgt_ref, wout_ref, gmix_ref, bgt_ref,
                         wconv_ref, bconv_ref, gmh_ref, grh_ref, cos_ref, sin_ref,
                         decay_ref, qdec_ref, kdec_ref, cdec_ref, wg32_ref, wu32_ref, wd32_ref,
                         x1_ref, conv_ref, C_ref, n_ref, m_ref, S_ref, wg16_ref, wu16_ref, wd16_ref,
                         un2, ut2, gt2, tail_ref, st_ref, h_ref,
                         *, tt, tiles_per_seq):
    wg16_ref[...] = wg32_ref[...].astype(BF16)
    wu16_ref[...] = wu32_ref[...].astype(BF16)
    wd16_ref[...] = wd32_ref[...].astype(BF16)
    L = CHUNK
    s = pl.program_id(0)
    t = (jnp.maximum(s, 1) - 1) % tiles_per_seq
    chunks = range(tt // L)
    slot_of = lambda i: (un2.at[i], ut2.at[i], gt2.at[i])

    @pl.when(s == 0)
    def _():
        for ref in slot_of(1) + (tail_ref,):
            ref[...] = jnp.zeros_like(ref)

    @pl.when(t == 0)
    def _():
        C_ref[...] = jnp.zeros_like(C_ref)
        st_ref[...] = jnp.zeros_like(st_ref)
        n_ref[...] = jnp.zeros_like(n_ref)
        m_ref[...] = jnp.zeros_like(m_ref)

    k_scale = HEAD_DIM ** -0.5

    def project(slot, xn, part):
        un_ref, ut_ref, gt_ref = slot
        c0 = part * MIX
        un_ref[UOFF:UOFF + tt, c0:c0 + MIX] = jnp.dot(xn, wn_ref[:, c0:c0 + MIX],
                                                      preferred_element_type=F32)
        res = lax.dot_general(wt_ref[c0:c0 + MIX, :], xn, NT_DIMS, preferred_element_type=F32)
        for c in chunks:
            ut_ref[c, c0:c0 + MIX, :] = res[:, c * L:(c + 1) * L]
        if part == 0:
            gates_t = lax.dot_general(wgt_ref[...], xn, NT_DIMS, preferred_element_type=F32)
            for c in chunks:
                gt_ref[c] = gates_t[:, c * L:(c + 1) * L]

    src_id = lax.broadcasted_iota(jnp.int32, (L, L), 0)
    tgt_id = lax.broadcasted_iota(jnp.int32, (L, L), 1)
    causal = src_id <= tgt_id
    triu_bf = jnp.where(causal, 1.0, 0.0).astype(BF16)
    heads = range(HEADS)
    hcol = lambda base, h: slice(base + h * HEAD_DIM, base + (h + 1) * HEAD_DIM)

    def chunk_body(c, slot):
        un_ref, ut_ref, gt_ref = slot
        r0 = c * L
        rows = pl.ds(UOFF + r0, L)
        trows = pl.ds(r0, L)
        g_t = gt_ref[c] + bgt_ref[...]
        ig = g_t[0:8]
        lf = _log_sigmoid(g_t[8:16])
        lf_hi = lf.astype(BF16)
        r1 = lf - lf_hi.astype(F32)
        lf_mid = r1.astype(BF16)
        lf_lo = (r1 - lf_mid.astype(F32)).astype(BF16)
        bc = (jnp.dot(lf_hi, triu_bf, preferred_element_type=F32)
              + jnp.dot(lf_mid, triu_bf, preferred_element_type=F32)
              + jnp.dot(lf_lo, triu_bf, preferred_element_type=F32))
        m_prev = m_ref[...]
        inter = bc + m_prev
        b_last = bc[:, L - 1:L]
        g_w = b_last - bc + ig
        m_new = jnp.maximum(b_last + m_prev, jnp.max(g_w, axis=1, keepdims=True))
        ws = jnp.exp(g_w - m_new)
        carry = jnp.exp(b_last + m_prev - m_new)
        a_n = jnp.concatenate([ig - bc, jnp.zeros((L - 8, L), F32)], axis=0).T

        def conv_act(col):
            win = un_ref[pl.ds(r0, L + UOFF), col:col + HEAD_DIM]
            acc = bconv_ref[:, col:col + HEAD_DIM]
            for j in range(CONV_W):
                back = CONV_W - 1 - j
                tap = win if back == 0 else pltpu.roll(win, back, 0)
                acc = acc + tap[UOFF:UOFF + L] * wconv_ref[j:j + 1, col:col + HEAD_DIM]
            return _silu(acc)

        cos_f, sin_s = cos_ref[trows, :], sin_ref[trows, :]
        q = [conv_act(N_QK + h * HEAD_DIM) for h in heads]
        kb = [(conv_act(N_QK + MIX + h * HEAD_DIM) * k_scale).astype(BF16) for h in heads]
        qb = [a.astype(BF16) for a in q]
        rqb = [_rope(un_ref[rows, hcol(N_RQ, h)], cos_f, sin_s).astype(BF16) for h in heads]
        rk = [_rope(un_ref[rows, hcol(N_RK, h)], cos_f, sin_s) * k_scale for h in heads]
        v_t = [ut_ref[c, hcol(T_V, h), :] for h in heads]
        rvb = [ut_ref[c, hcol(T_RV, h), :].astype(BF16) for h in heads]
        c_old = [C_ref[h] for h in heads]
        s_old = [st_ref[h] for h in heads]
        n_old = [n_ref[h:h + 1, :] for h in heads]
        qk = [_dot_nt(kb[h], qb[h]) for h in heads]
        att = [_dot_nt(rk[h], rqb[h]) for h in heads]
        c_q = [_dot_nt(c_old[h], qb[h]) for h in heads]
        s_q = [_dot_nt(s_old[h], rqb[h]) for h in heads]
        n_q = [_dot_nt(jnp.broadcast_to(n_old[h], (8, HEAD_DIM)), qb[h])[0:1] for h in heads]
        d_c = [_dot(v_t[h] * ws[h:h + 1, :], kb[h]) for h in heads]
        d_n = [_dot(jnp.broadcast_to(ws[h:h + 1, :], (8, L)), kb[h])[0:1] for h in heads]
        d_s = [_dot(rvb[h], rk[h] * kdec_ref[h]) for h in heads]
        m_t, w_in, wts = [], [], []
        for h in heads:
            dmat = a_n[:, h:h + 1] + bc[h:h + 1, :]
            dmat = jnp.where(causal, dmat, -jnp.inf)
            m_t.append(jnp.maximum(inter[h:h + 1, :], jnp.max(dmat, axis=0, keepdims=True)))
            wts.append(jnp.exp(dmat - m_t[h]) * qk[h])
            w_in.append(jnp.exp(inter[h:h + 1, :] - m_t[h]))
        att_w = [att[h] * decay_ref[h] for h in heads]
        v_p = [_dot(v_t[h], wts[h]) for h in heads]
        v_a = [_dot(rvb[h], att_w[h]) for h in heads]
        for h in heads:
            num = v_p[h] + w_in[h] * c_q[h]
            den = jnp.sum(wts[h], axis=0, keepdims=True) + w_in[h] * n_q[h]
            hm = num / jnp.maximum(jnp.abs(den), jnp.exp(-m_t[h]))
            hm = hm * lax.rsqrt(jnp.mean(hm * hm, axis=0, keepdims=True) + EPS)
            hm = hm * gmh_ref[hcol(0, h), :] * jax.nn.sigmoid(ut_ref[c, hcol(T_O, h), :])
            h_ref[trows, hcol(0, h)] = hm.T.astype(BF16)
            carry_h = carry[h:h + 1, :]
            C_ref[h] = carry_h * c_old[h] + d_c[h]
            n_ref[h:h + 1, :] = carry_h * n_old[h] + d_n[h]
            o = v_a[h] + qdec_ref[h:h + 1, :] * s_q[h]
            st_ref[h] = cdec_ref[h:h + 1, :] * s_old[h] + d_s[h]
            hr = o * lax.rsqrt(jnp.mean(o * o, axis=0, keepdims=True) + EPS)
            hr = hr * grh_ref[hcol(0, h), :] * _silu(ut_ref[c, hcol(T_RG, h), :])
            h_ref[trows, hcol(MIX, h)] = hr.T.astype(BF16)

        m_ref[...] = m_new

    def step(proj_slot, scan_slot):
        un_ref = scan_slot[0]
        un_ref[0:UOFF, N_QK:N_RQ] = jnp.where(t == 0, 0.0, tail_ref[...])
        xn = _rms(x_ref[...], gmix_ref[...]).astype(BF16)
        for c in chunks:
            project(proj_slot, xn, c)
            chunk_body(c, scan_slot)
        tail_ref[...] = un_ref[tt:tt + UOFF, N_QK:N_RQ]
        x1_ref[...] = xs_ref[...] + jnp.dot(h_ref[...], wout_ref[...], preferred_element_type=F32)

    step(slot_of(s % 2), slot_of(1 - s % 2))

    @pl.when(t == tiles_per_seq - 1)
    def _():
        conv_ref[...] = tail_ref[UOFF - (CONV_W - 1):UOFF, :]
        for h in heads:
            S_ref[h] = st_ref[h].T


def _mixer_prompt(x, wn, wt, wgt, w_out, g_mix, bgt, w_conv, b_conv, gmh_cols, grh_cols, cos_f, sin_s,
                  decay_t, qdec_rows, kdec_cols, cdec_rows, w_gate, w_up, w_down, tt=512):
    B, T, _ = x.shape
    tps = T // tt
    n_tiles = B * tps
    proj = lambda s: jnp.minimum(s, n_tiles - 1)
    scan = lambda s: jnp.maximum(s, 1) - 1
    per_b3 = lambda s: (scan(s) // tps, 0, 0)
    per_b4 = lambda s: (scan(s) // tps, 0, 0, 0)
    scan_tile = pl.BlockSpec((None, tt, D_MODEL), lambda s: (scan(s) // tps, scan(s) % tps, 0))
    in_specs = [
        pl.BlockSpec((None, tt, D_MODEL), lambda s: (proj(s) // tps, proj(s) % tps, 0)),
        scan_tile,
        _const_spec((D_MODEL, N_COLS)), _const_spec((T_ROWS, D_MODEL)), _const_spec((GT_ROWS, D_MODEL)),
        _const_spec((D_MODEL, D_MODEL)),
        _const_spec((1, D_MODEL)), _const_spec((GT_ROWS, CHUNK)),
        _const_spec((CONV_W, 2 * MIX)), _const_spec((1, 2 * MIX)),
        _const_spec((MIX, CHUNK)), _const_spec((MIX, CHUNK)),
        pl.BlockSpec((tt, HEAD_DIM), lambda s: (scan(s) % tps, 0)),
        pl.BlockSpec((tt, HEAD_DIM), lambda s: (scan(s) % tps, 0)),
        _const_spec((HEADS, CHUNK, CHUNK)),
        _const_spec((8, CHUNK)), _const_spec((HEADS, CHUNK, HEAD_DIM)), _const_spec((8, HEAD_DIM)),
    ]
    ff_blocks = D_FF // HEAD_DIM
    assert ff_blocks <= n_tiles + 1
    early = lambda s: jnp.minimum(s, ff_blocks - 1)
    late = lambda s: jnp.maximum(s - (n_tiles + 1 - ff_blocks), 0)
    cast_specs = [pl.BlockSpec((D_MODEL, HEAD_DIM), lambda s: (0, early(s))),
                  pl.BlockSpec((D_MODEL, HEAD_DIM), lambda s: (0, late(s))),
                  pl.BlockSpec((HEAD_DIM, D_MODEL), lambda s: (late(s), 0))]
    in_specs += cast_specs
    out_specs = [
        scan_tile,
        pl.BlockSpec((None, CONV_W - 1, 2 * MIX), per_b3),
        pl.BlockSpec((None, HEADS, HEAD_DIM, HEAD_DIM), per_b4),
        pl.BlockSpec((None, HEADS, HEAD_DIM), per_b3),
        pl.BlockSpec((None, 8, CHUNK), per_b3),
        pl.BlockSpec((None, HEADS, HEAD_DIM, HEAD_DIM), per_b4),
    ] + cast_specs
    out_shape = [
        jax.ShapeDtypeStruct((B, T, D_MODEL), F32),
        jax.ShapeDtypeStruct((B, CONV_W - 1, 2 * MIX), F32),
        jax.ShapeDtypeStruct((B, HEADS, HEAD_DIM, HEAD_DIM), F32),
        jax.ShapeDtypeStruct((B, HEADS, HEAD_DIM), F32),
        jax.ShapeDtypeStruct((B, 8, CHUNK), F32),
        jax.ShapeDtypeStruct((B, HEADS, HEAD_DIM, HEAD_DIM), F32),
        jax.ShapeDtypeStruct((D_MODEL, D_FF), BF16), jax.ShapeDtypeStruct((D_MODEL, D_FF), BF16),
        jax.ShapeDtypeStruct((D_FF, D_MODEL), BF16),
    ]
    n_chunks = tt // CHUNK
    scratch = [
        pltpu.VMEM((2, tt + UOFF, N_COLS), F32),
        pltpu.VMEM((2, n_chunks, T_ROWS, CHUNK), F32),
        pltpu.VMEM((2, n_chunks, GT_ROWS, CHUNK), F32),
    ] + [
        pltpu.VMEM((UOFF, 2 * MIX), F32),
        pltpu.VMEM((HEADS, HEAD_DIM, HEAD_DIM), F32),
        pltpu.VMEM((tt, D_MODEL), BF16),
    ]
    return pl.pallas_call(
        functools.partial(_mixer_prompt_kernel, tt=tt, tiles_per_seq=tps),
        grid=(n_tiles + 1,),
        in_specs=in_specs, out_specs=out_specs, out_shape=out_shape,
        scratch_shapes=scratch,
        compiler_params=_params("arbitrary"),
    )(x, x, wn, wt, wgt, w_out, g_mix, bgt, w_conv, b_conv, gmh_cols, grh_cols, cos_f, sin_s,
      decay_t, qdec_rows, kdec_cols, cdec_rows, w_gate, w_up, w_down)


def _attn_prompt_kernel(x1_ref, k_ref, v_ref, wcq_ref, gx_ref, *rest, n_sample_steps):
    sample_in, o_ref = rest[:SAMPLE_MIXER_INPUTS], rest[SAMPLE_MIXER_INPUTS]
    sample_out_and_scratch = rest[SAMPLE_MIXER_INPUTS + 1:]
    step = pl.program_id(0) * pl.num_programs(1) + pl.program_id(1)

    @pl.when(step < n_sample_steps)
    def _():
        _mixer_sample_kernel(*sample_in, *sample_out_and_scratch, bb=SAMPLE_BLOCK)

    tm = x1_ref.shape[0]
    rows = [slice(r, r + tm // ROW_GROUPS) for r in range(0, tm, tm // ROW_GROUPS)]
    xq = [_rms(x1_ref[r, :], gx_ref[...]).astype(BF16) for r in rows]
    q = [jnp.dot(a, wcq_ref[...], preferred_element_type=F32).astype(BF16) for a in xq]
    sl = [slice(h * X_HEAD_DIM, (h + 1) * X_HEAD_DIM) for h in range(X_HEADS)]
    items = [(g, h) for g in range(ROW_GROUPS) for h in range(X_HEADS)]
    s = [_dot_nt(q[g][:, sl[h]], k_ref[:, sl[h]]) * (X_HEAD_DIM ** -0.5) for g, h in items]
    e = [jnp.exp(a - jnp.max(a, axis=-1, keepdims=True)) for a in s]
    p = [a / jnp.sum(a, axis=-1, keepdims=True) for a in e]
    for (g, h), a in zip(items, p):
        o_ref[rows[g], sl[h]] = _dot(a, v_ref[:, sl[h]]).astype(BF16)


def _attn_prompt(x1, mk, mv, w_cq, g_x, sample_mixer_operands, tm=512):
    B, T, _ = x1.shape
    tps = T // tm
    Bs = sample_mixer_operands[0].shape[0]
    n_sample_steps = Bs // SAMPLE_BLOCK
    assert n_sample_steps <= B * tps
    block_of = lambda b, t: jnp.minimum(b * tps + t, n_sample_steps - 1)
    s_in, s_out, s_shape, s_scratch = _sample_mixer_specs(Bs, block_of)
    tile = pl.BlockSpec((None, tm, D_MODEL), lambda b, t: (b, t, 0))
    kv = pl.BlockSpec((None, N_MEM, D_MODEL), lambda b, t: (b, 0, 0))
    outs = pl.pallas_call(
        functools.partial(_attn_prompt_kernel, n_sample_steps=n_sample_steps),
        grid=(B, tps),
        in_specs=[tile, kv, kv, _const_spec((D_MODEL, D_MODEL)), _const_spec((1, D_MODEL))] + s_in,
        out_specs=[tile] + s_out,
        out_shape=[jax.ShapeDtypeStruct((B, T, D_MODEL), BF16)] + s_shape,
        scratch_shapes=s_scratch,
        compiler_params=_params("arbitrary", "arbitrary"),
    )(x1, mk, mv, w_cq, g_x, *sample_mixer_operands)
    return outs[0], outs[1:]


V7X_MXU_DIM = 256
FF_CHUNKS = ((0, 6 * V7X_MXU_DIM), (6 * V7X_MXU_DIM, D_FF))


def _ffn_kernel(x1_ref, o_ref, wco_ref, wg_ref, wu_ref, wd_ref, gffn_ref, gfin_ref, *rest, n_groups):
    if len(rest) == 1:
        (y_ref,) = rest
        sample_stages = iter(())
    else:
        qs_ref, ck_ref, cv_ref, y_ref, os_ref = rest
        sample_stages = _sample_attention(qs_ref, ck_ref, cv_ref, os_ref)
    tm = x1_ref.shape[0]
    rows = [slice(r, r + tm // n_groups) for r in range(0, tm, tm // n_groups)]
    acc = [x1_ref[r, :] + _dot(o_ref[r, :], wco_ref[...]) for r in rows]
    hf = [_rms(a, gffn_ref[...]).astype(BF16) for a in acc]
    for c0, c1 in FF_CHUNKS:
        next(sample_stages, None)
        gate = [jnp.dot(h, wg_ref[:, c0:c1], preferred_element_type=F32) for h in hf]
        up = [jnp.dot(h, wu_ref[:, c0:c1], preferred_element_type=F32) for h in hf]
        act = [_silu(g) * u for g, u in zip(gate, up)]
        acc = [a + _dot(p, wd_ref[c0:c1, :]) for a, p in zip(acc, act)]
    next(sample_stages, None)
    for r, a in zip(rows, acc):
        y_ref[r, :] = _rms(a, gfin_ref[...])


def _ffn(x1, o, w_co, w_gate, w_up, w_down, g_ffn, g_final, tm, sample=None):
    rows = x1.shape[0]
    steps = rows // tm
    row_spec = pl.BlockSpec((tm, D_MODEL), lambda i: (i, 0))
    in_specs = [row_spec, row_spec, _const_spec((D_MODEL, D_MODEL)),
                _const_spec((D_MODEL, D_FF)), _const_spec((D_MODEL, D_FF)),
                _const_spec((D_FF, D_MODEL)), _const_spec((1, D_MODEL)), _const_spec((1, D_MODEL))]
    out_specs = [row_spec]
    out_shape = [jax.ShapeDtypeStruct((rows, D_MODEL), F32)]
    operands = [x1, o, w_co, w_gate, w_up, w_down, g_ffn, g_final]
    if sample is not None:
        q_s, ck_s, cv_s = sample
        Bs = q_s.shape[0]
        ba = Bs // steps
        assert ba * steps == Bs
        s_rows = pl.BlockSpec((ba, QROWS, 128), lambda i: (i, 0, 0))
        s_kv = pl.BlockSpec((ba, KV_ROWS, 128), lambda i: (i, 0, 0))
        in_specs += [s_rows, s_kv, s_kv]
        out_specs += [s_rows]
        out_shape += [jax.ShapeDtypeStruct((Bs, QROWS, 128), F32)]
        operands += [q_s.reshape(Bs, QROWS, 128), ck_s, cv_s]
    outs = pl.pallas_call(
        functools.partial(_ffn_kernel, n_groups=ROW_GROUPS if tm >= 512 else 1),
        grid=(steps,),
        in_specs=in_specs, out_specs=out_specs, out_shape=out_shape,
        compiler_params=_params("parallel"),
    )(*operands)
    if sample is None:
        return outs[0]
    return outs[0], outs[1].reshape(Bs, D_MODEL)


def _inproj_sample_kernel(x_ref, g_ref, wn_ref, wt_ref, wgt_ref, un_ref, ut_ref, gs_ref):
    xn = _rms(x_ref[...], g_ref[...]).astype(BF16)
    un_ref[...] = jnp.dot(xn, wn_ref[...], preferred_element_type=F32)
    ut_ref[...] = lax.dot_general(xn, wt_ref[...], NT_DIMS, preferred_element_type=F32)
    gs_ref[...] = lax.dot_general(xn, wgt_ref[...], NT_DIMS, preferred_element_type=F32)


def _inproj_sample(x, g_mix, wn, wt, wgt):
    rows = x.shape[0]
    full = lambda cols: pl.BlockSpec((rows, cols), lambda i: (0, 0))
    return pl.pallas_call(
        _inproj_sample_kernel,
        grid=(1,),
        in_specs=[_const_spec((rows, D_MODEL)), _const_spec((1, D_MODEL)),
                  _const_spec((D_MODEL, N_COLS)), _const_spec((T_ROWS, D_MODEL)),
                  _const_spec((GT_ROWS, D_MODEL))],
        out_specs=[full(N_COLS), full(T_ROWS), full(GT_ROWS)],
        out_shape=[jax.ShapeDtypeStruct((rows, N_COLS), F32),
                   jax.ShapeDtypeStruct((rows, T_ROWS), F32),
                   jax.ShapeDtypeStruct((rows, GT_ROWS), F32)],
        compiler_params=_params("arbitrary"),
    )(x, g_mix, wn, wt, wgt)


def _mixer_sample_kernel(un_ref, ut_ref, gs_ref, conv_ref, C_ref, n_ref, m_ref, S_ref, bg_ref,
                         wconv_ref, bconv_ref, gmh_ref, grh_ref, cos_ref, sin_ref, rtab_ref,
                         h_ref, convo_ref, Co_ref, no_ref, mo_ref, So_ref,
                         q_s, k_s, vws_s, qr_s, kr_s, qc_s, qs_s, carry_s, *, bb):
    k_scale = HEAD_DIM ** -0.5
    uqk = un_ref[:, N_QK:N_RQ]
    conv = (bconv_ref[...] + wconv_ref[0:1, :] * conv_ref[0] + wconv_ref[1:2, :] * conv_ref[1]
            + wconv_ref[2:3, :] * conv_ref[2] + wconv_ref[3:4, :] * uqk)
    convo_ref[0] = conv_ref[1]
    convo_ref[1] = conv_ref[2]
    convo_ref[2] = uqk
    qk_act = _silu(conv)
    q_s[...] = qk_act[:, 0:MIX]
    k_s[...] = qk_act[:, MIX:2 * MIX] * k_scale

    gates = gs_ref[...] + bg_ref[...]
    ig = gates[:, 0:HEADS]
    lf = _log_sigmoid(gates[:, 8:8 + HEADS])
    inter = lf + m_ref[...]
    m_t = jnp.maximum(inter, ig)
    ws = jnp.exp(ig - m_t)
    w_in = jnp.exp(inter - m_t)
    mo_ref[...] = m_t
    carry_s[:, 0:HEADS] = w_in

    cos_f = cos_ref[...]
    sin_s = sin_ref[...]
    for h in range(HEADS):
        lo = h * HEAD_DIM
        hs = slice(lo, lo + HEAD_DIM)
        vws_s[:, hs] = ut_ref[:, T_V + lo:T_V + lo + HEAD_DIM] * ws[:, h:h + 1]
        qr_s[:, hs] = _rope(un_ref[:, N_RQ + lo:N_RQ + lo + HEAD_DIM], cos_f, sin_s)
        kr_s[:, hs] = _rope(un_ref[:, N_RK + lo:N_RK + lo + HEAD_DIM], cos_f, sin_s) * k_scale

    heads = range(HEADS)
    seqs = range(bb)
    hsl = [slice(h * HEAD_DIM, (h + 1) * HEAD_DIM) for h in heads]
    seq_id = lax.broadcasted_iota(jnp.int32, (bb, HEAD_DIM), 0)
    for h in heads:
        q_h, k_h, vws_h = q_s[:, hsl[h]], k_s[:, hsl[h]], vws_s[:, hsl[h]]
        qr_h = qr_s[:, hsl[h]]
        krd_h = kr_s[:, hsl[h]] * rtab_ref[1:2, h:h + 1]
        rv_h = ut_ref[:, T_RV + h * HEAD_DIM:T_RV + (h + 1) * HEAD_DIM]
        c_old = [C_ref[b, h] for b in seqs]
        s_old = [S_ref[b, h] for b in seqs]
        q_c = [_dot_nt(q_h, c_old[b]) for b in seqs]
        q_st = [_dot(qr_h, s_old[b]) for b in seqs]
        d_c = [_dot_tn(jnp.where(seq_id == b, vws_h, 0.0), k_h) for b in seqs]
        d_s = [_dot_tn(jnp.where(seq_id == b, krd_h, 0.0), rv_h) for b in seqs]
        for b in seqs:
            row = slice(b, b + 1)
            qc_s[row, hsl[h]] = q_c[b][row]
            qs_s[row, hsl[h]] = q_st[b][row]
            Co_ref[b, h] = carry_s[row, h:h + 1] * c_old[b] + d_c[b]
            So_ref[b, h] = rtab_ref[2:3, h:h + 1] * s_old[b] + d_s[b]

    for h in range(HEADS):
        lo = h * HEAD_DIM
        hs = slice(lo, lo + HEAD_DIM)
        q_h, k_h, n_h = q_s[:, hs], k_s[:, hs], n_ref[:, hs]
        ws_h, w_in_h = ws[:, h:h + 1], w_in[:, h:h + 1]
        wts = ws_h * jnp.sum(q_h * k_h, axis=1, keepdims=True)
        num = wts * ut_ref[:, T_V + lo:T_V + lo + HEAD_DIM] + w_in_h * qc_s[:, hs]
        den = wts + w_in_h * jnp.sum(n_h * q_h, axis=1, keepdims=True)
        hm = num / jnp.maximum(jnp.abs(den), jnp.exp(-m_t[:, h:h + 1]))
        mo = ut_ref[:, T_O + lo:T_O + lo + HEAD_DIM]
        h_ref[:, hs] = _head_norm(hm) * gmh_ref[:, hs] * jax.nn.sigmoid(mo)
        no_ref[:, hs] = w_in_h * n_h + ws_h * k_h

        att = jnp.sum(qr_s[:, hs] * kr_s[:, hs], axis=1, keepdims=True) * rtab_ref[3:4, h:h + 1]
        o = (att * ut_ref[:, T_RV + lo:T_RV + lo + HEAD_DIM]
             + rtab_ref[0:1, h:h + 1] * qs_s[:, hs])
        rg = ut_ref[:, T_RG + lo:T_RG + lo + HEAD_DIM]
        h_ref[:, MIX + lo:MIX + lo + HEAD_DIM] = _head_norm(o) * grh_ref[:, hs] * _silu(rg)


SAMPLE_BLOCK = 8
SAMPLE_MIXER_INPUTS = 16


def _sample_mixer_specs(B, block_of):
    bb = SAMPLE_BLOCK
    rows = lambda cols: pl.BlockSpec((bb, cols), lambda *g: (block_of(*g), 0))
    mats = pl.BlockSpec((bb, HEADS, HEAD_DIM, HEAD_DIM), lambda *g: (block_of(*g), 0, 0, 0))
    convs = pl.BlockSpec((CONV_W - 1, bb, 2 * MIX), lambda *g: (0, block_of(*g), 0))
    in_specs = [rows(N_COLS), rows(T_ROWS), rows(GT_ROWS), convs, mats, rows(MIX), rows(HEADS), mats,
                _const_spec((1, GT_ROWS)), _const_spec((CONV_W, 2 * MIX)), _const_spec((1, 2 * MIX)),
                _const_spec((1, MIX)), _const_spec((1, MIX)),
                _const_spec((1, HEAD_DIM)), _const_spec((1, HEAD_DIM)), _const_spec((8, HEAD_DIM))]
    assert len(in_specs) == SAMPLE_MIXER_INPUTS
    out_specs = [rows(D_MODEL), convs, mats, rows(MIX), rows(HEADS), mats]
    out_shape = [jax.ShapeDtypeStruct((B, D_MODEL), F32),
                 jax.ShapeDtypeStruct((CONV_W - 1, B, 2 * MIX), F32),
                 jax.ShapeDtypeStruct((B, HEADS, HEAD_DIM, HEAD_DIM), F32),
                 jax.ShapeDtypeStruct((B, MIX), F32),
                 jax.ShapeDtypeStruct((B, HEADS), F32),
                 jax.ShapeDtypeStruct((B, HEADS, HEAD_DIM, HEAD_DIM), F32)]
    scratch = [pltpu.VMEM((bb, MIX), F32) for _ in range(7)] + [pltpu.VMEM((bb, HEAD_DIM), F32)]
    return in_specs, out_specs, out_shape, scratch


def _outq_sample_kernel(x_ref, h_ref, wout_ref, wcq_ref, gx_ref, x1_ref, q_ref):
    x1 = x_ref[...] + _dot(h_ref[...], wout_ref[...])
    x1_ref[...] = x1
    q_ref[...] = _dot(_rms(x1, gx_ref[...]), wcq_ref[...])


def _outq_sample(x, hcat, w_out, w_cq, g_x):
    rows = x.shape[0]
    full = _const_spec((rows, D_MODEL))
    return pl.pallas_call(
        _outq_sample_kernel,
        grid=(1,),
        in_specs=[full, full, _const_spec((D_MODEL, D_MODEL)), _const_spec((D_MODEL, D_MODEL)),
                  _const_spec((1, D_MODEL))],
        out_specs=[pl.BlockSpec((rows, D_MODEL), lambda i: (0, 0))] * 2,
        out_shape=[jax.ShapeDtypeStruct((rows, D_MODEL), F32)] * 2,
        compiler_params=_params("arbitrary"),
    )(x, hcat, w_out, w_cq, g_x)


def _sample_attention(q_ref, k_ref, v_ref, o_ref):
    r_id = lax.broadcasted_iota(jnp.int32, (QROWS, KV_ROWS), 0)
    n_id = lax.broadcasted_iota(jnp.int32, (QROWS, KV_ROWS), 1)
    own = (n_id & 7) == (r_id >> 1) + 4 * (r_id & 1)
    low_half = (lax.broadcasted_iota(jnp.int32, (1, KV_ROWS), 1) & 4) == 0
    seqs = range(q_ref.shape[0])
    z = [_dot_nt(q_ref[j], k_ref[j]) for j in seqs]
    zc = [jnp.sum(jnp.where(own, z[j], 0.0), axis=0, keepdims=True) for j in seqs]
    yield
    other = [jnp.where(low_half, pltpu.roll(zc[j], KV_ROWS - 4, 1), pltpu.roll(zc[j], 4, 1))
             for j in seqs]
    s = [jnp.where(own, (zc[j] + other[j]) * (X_HEAD_DIM ** -0.5), -jnp.inf) for j in seqs]
    m = [jnp.max(s[j], axis=-1, keepdims=True) for j in seqs]
    e = [jnp.exp(s[j] - m[j]) for j in seqs]
    den = [jnp.sum(e[j], axis=-1, keepdims=True) for j in seqs]
    yield
    for j in seqs:
        o_ref[j] = _dot(e[j] / den[j], v_ref[j])
    yield


def _kv_rows(cache):
    B = cache.shape[0]
    c5 = cache.reshape(B, N_MEM, X_HEADS, 2, 128)
    return jnp.transpose(c5, (0, 1, 3, 2, 4)).reshape(B, KV_ROWS, 128)


def _kv_from_rows(rows):
    B = rows.shape[0]
    r5 = rows.reshape(B, N_MEM, 2, X_HEADS, 128)
    return jnp.transpose(r5, (0, 1, 3, 2, 4)).reshape(1, B, N_MEM, X_HEADS, X_HEAD_DIM)


def _rope_tables(pos):
    half = HEAD_DIM // 2
    inv = ROPE_THETA ** (-np.arange(half, dtype=np.float64) / half)
    ang = np.asarray(pos, np.float64)[:, None] * inv[None, :]
    cos, sin = np.cos(ang), np.sin(ang)
    return (np.concatenate([cos, cos], axis=-1).astype(np.float32),
            np.concatenate([-sin, sin], axis=-1).astype(np.float32))


def _retention_tables(L):
    lg = np.log1p(-np.exp2(-5.0 - np.arange(HEADS, dtype=np.float64)))
    t = np.arange(L, dtype=np.float64)
    diff = t[:, None] - t[None, :]
    decay = np.where(diff >= 0, np.exp(lg[:, None, None] * np.maximum(diff, 0.0)), 0.0)
    q_dec = np.exp(lg[:, None] * (t + 1.0))
    k_dec = np.exp(lg[:, None] * (L - 1.0 - t))
    chunk_dec = np.exp(lg * L)
    return tuple(a.astype(np.float32) for a in (decay, q_dec, k_dec, chunk_dec))


def _lanes(a, n):
    xp = np if isinstance(a, np.ndarray) else jnp
    return xp.broadcast_to(a[..., None], a.shape + (n,))


def _pad_rows(a, rows):
    return np.pad(a, ((0, rows - a.shape[0]),) + ((0, 0),) * (a.ndim - 1))


def kernel(x_prompt, x_sample, cache_mem_k, cache_mem_v, state_mlstm_conv, state_mlstm_C, state_mlstm_n, state_mlstm_m, state_ret_S, mem_prompt, w_in, b_gate, w_conv, b_conv, g_mix, g_mhead, g_rhead, w_out, g_xattn, g_mem, w_ck, w_cv, w_cq, w_co, g_ffn, w_gate, w_up, w_down, g_final):
    Bp, Tp, _ = x_prompt.shape
    Bs = x_sample.shape[0]
    l = 0
    n_m = 4 * MIX
    wi = w_in[l]
    w_gates = wi[:, n_m:n_m + 2 * HEADS]
    w_ret = wi[:, n_m + 2 * HEADS:]
    row = lambda a: a.reshape(1, -1)
    bf = lambda a: a.astype(BF16)
    g_mix_r, g_mh_r, g_rh_r = row(g_mix[l]), row(g_mhead[l]), row(g_rhead[l])
    g_x_r, g_mem_r, g_ffn_r, g_fin_r = row(g_xattn[l]), row(g_mem[l]), row(g_ffn[l]), row(g_final)
    b_conv_r = row(b_conv[l])
    w_out_b, w_cq_b, w_co_b = bf(w_out[l]), bf(w_cq[l]), bf(w_co[l])

    mk_b, mv_b, mk_rows, mv_rows = _memkv(mem_prompt, g_mem_r, bf(w_ck[l]), bf(w_cv[l]))
    wn = bf(jnp.concatenate([wi[:, :2 * MIX], w_ret[:, :2 * MIX]], axis=1))
    wt = bf(jnp.concatenate([wi[:, 2 * MIX:n_m], w_ret[:, 2 * MIX:]], axis=1).T)
    gate_rows = lambda a: jnp.concatenate(
        [a[:HEADS], jnp.zeros((8 - HEADS,) + a.shape[1:], F32),
         a[HEADS:], jnp.zeros((8 - HEADS,) + a.shape[1:], F32)], axis=0)
    wgt = bf(gate_rows(w_gates.T))
    bg_rows = gate_rows(b_gate[l][:, None])
    bgt = jnp.broadcast_to(bg_rows, (GT_ROWS, CHUNK))
    cos_p, sin_p = _rope_tables(np.arange(Tp))
    decay, q_dec, k_dec, chunk_dec = _retention_tables(CHUNK)
    x1_p, conv_p, C_p, n_p, m_p, S_p, w_gate_b, w_up_b, w_down_b = _mixer_prompt(
        x_prompt, wn, wt, wgt, w_out_b, g_mix_r, bgt, w_conv[l], b_conv_r,
        _lanes(g_mhead[l], CHUNK), _lanes(g_rhead[l], CHUNK), cos_p, sin_p,
        np.swapaxes(decay, 1, 2), _pad_rows(q_dec, 8), _lanes(k_dec, HEAD_DIM),
        _pad_rows(_lanes(chunk_dec, HEAD_DIM), 8), w_gate[l], w_up[l], w_down[l])

    xs = x_sample.reshape(Bs, D_MODEL)
    un_s, ut_s, gs_s = _inproj_sample(xs, g_mix_r, wn, wt, wgt)
    cos_s, sin_s = _rope_tables(PAST_LEN + np.arange(1))
    decay1, q_dec1, k_dec1, chunk_dec1 = _retention_tables(1)
    rtab = np.zeros((8, HEAD_DIM), np.float32)
    rtab[:4, :HEADS] = np.stack([q_dec1[:, 0], k_dec1[:, 0], chunk_dec1, decay1[:, 0, 0]])
    sample_mixer_operands = (
        un_s, ut_s, gs_s, jnp.transpose(state_mlstm_conv[l], (1, 0, 2)), state_mlstm_C[l],
        state_mlstm_n[l].reshape(Bs, MIX), state_mlstm_m[l], state_ret_S[l],
        bg_rows.reshape(1, GT_ROWS), w_conv[l], b_conv_r, g_mh_r, g_rh_r, cos_s, sin_s, rtab)

    o_p, (hcat_s, conv_s, C_s, n_s, m_s, S_s) = _attn_prompt(
        x1_p, mk_b, mv_b, w_cq_b, g_x_r, sample_mixer_operands)
    x1_s, q_s = _outq_sample(xs, hcat_s, w_out_b, w_cq_b, g_x_r)
    y_p, o_s = _ffn(x1_p.reshape(Bp * Tp, D_MODEL), o_p.reshape(Bp * Tp, D_MODEL),
                    w_co_b, w_gate_b, w_up_b, w_down_b, g_ffn_r, g_fin_r, tm=512,
                    sample=(q_s, _kv_rows(cache_mem_k[l]), _kv_rows(cache_mem_v[l])))
    y_s = _ffn(x1_s, o_s, w_co_b, w_gate_b, w_up_b, w_down_b, g_ffn_r, g_fin_r, tm=Bs)

    return (y_p.reshape(Bp, Tp, D_MODEL), y_s.reshape(Bs, 1, D_MODEL),
            _kv_from_rows(mk_rows), _kv_from_rows(mv_rows),
            conv_p[None], C_p[None], n_p[None], m_p[None, :, :HEADS, 0], S_p[None],
            jnp.transpose(conv_s, (1, 0, 2))[None], C_s[None],
            n_s.reshape(1, Bs, HEADS, HEAD_DIM), m_s[None], S_s[None])
```

```python
import functools

import jax
import jax.numpy as jnp
import numpy as np
from jax import lax
from jax.experimental import pallas as pl
from jax.experimental.pallas import tpu as pltpu

F32 = jnp.float32
BF16 = jnp.bfloat16

D_MODEL = 1024
HEADS = 4
HEAD_DIM = 128
MIX = HEADS * HEAD_DIM
CONV_W = 4
CHUNK = 128
N_MEM = 256
X_HEADS = 4
X_HEAD_DIM = 256
D_FF = 2816
ROPE_THETA = 10000.0
EPS = 1e-6
PAST_LEN = 16384
KV_ROWS = N_MEM * 2 * X_HEADS
QROWS = 2 * X_HEADS

N_QK, N_RQ, N_RK, N_COLS = 0, 1024, 1536, 2048
T_V, T_O, T_RV, T_RG, T_ROWS = 0, 512, 1024, 1536, 2048
GT_ROWS = 16
UOFF = 8
ROW_GROUPS = 2

V7X_VMEM_LIMIT = 56 * 1024 * 1024

NT_DIMS = (((1,), (1,)), ((), ()))
TN_DIMS = (((0,), (0,)), ((), ()))


def _dot(a, b):
    return jnp.dot(a.astype(BF16), b.astype(BF16), preferred_element_type=F32)


def _dot_nt(a, b):
    return lax.dot_general(a.astype(BF16), b.astype(BF16), NT_DIMS, preferred_element_type=F32)


def _dot_tn(a, b):
    return lax.dot_general(a.astype(BF16), b.astype(BF16), TN_DIMS, preferred_element_type=F32)


def _rms(x, g):
    return x * lax.rsqrt(jnp.mean(x * x, axis=-1, keepdims=True) + EPS) * g


def _head_norm(h):
    return h * lax.rsqrt(jnp.mean(h * h, axis=-1, keepdims=True) + EPS)


def _silu(x):
    return x * jax.nn.sigmoid(x)


def _log_sigmoid(x):
    return jnp.minimum(x, 0.0) - jnp.log1p(jnp.exp(-jnp.abs(x)))


def _rope(x, cos_full, sin_signed):
    return x * cos_full + pltpu.roll(x, HEAD_DIM // 2, 1) * sin_signed


def _const_spec(shape):
    zeros = (0,) * len(shape)
    return pl.BlockSpec(shape, lambda *_: zeros, pipeline_mode=pl.Buffered(1))


def _params(*sem):
    return pltpu.CompilerParams(dimension_semantics=sem, vmem_limit_bytes=V7X_VMEM_LIMIT)


def _memkv_kernel(mem_ref, g_ref, wk_ref, wv_ref, kb_ref, vb_ref, krows_ref, vrows_ref):
    mn = _rms(mem_ref[...], g_ref[...]).astype(BF16)
    for w_ref, b_ref, rows_ref in ((wk_ref, kb_ref, krows_ref), (wv_ref, vb_ref, vrows_ref)):
        proj = jnp.dot(mn, w_ref[...], preferred_element_type=F32)
        b_ref[...] = proj.astype(BF16)
        for h in range(X_HEADS):
            for c in range(2):
                lane0 = h * X_HEAD_DIM + c * 128
                rows_ref[pl.ds(c * X_HEADS + h, N_MEM, stride=QROWS), :] = proj[:, lane0:lane0 + 128]


def _memkv(mem, g_mem, wk, wv):
    B = mem.shape[0]
    tok = pl.BlockSpec((None, N_MEM, D_MODEL), lambda b: (b, 0, 0))
    rows = pl.BlockSpec((None, KV_ROWS, 128), lambda b: (b, 0, 0))
    return pl.pallas_call(
        _memkv_kernel,
        grid=(B,),
        in_specs=[tok, _const_spec((1, D_MODEL)),
                  _const_spec((D_MODEL, D_MODEL)), _const_spec((D_MODEL, D_MODEL))],
        out_specs=[tok, tok, rows, rows],
        out_shape=[jax.ShapeDtypeStruct((B, N_MEM, D_MODEL), BF16)] * 2
        + [jax.ShapeDtypeStruct((B, KV_ROWS, 128), F32)] * 2,
        compiler_params=_params("parallel"),
    )(mem, g_mem, wk, wv)


def _mixer_prompt_kernel(x_ref, xs_ref, wn_ref, wt_ref, wgt_ref, wout_ref, gmix_ref, bgt_ref,
                         wconv_ref, bconv_ref, gmh_ref, grh_ref, cos_ref, sin_ref,
                         decay_ref, qdec_ref, kdec_ref, cdec_ref, wg32_ref, wu32_ref, wd32_ref,
                         x1_ref, conv_ref, C_ref, n_ref, m_ref, S_ref, wg16_ref, wu16_ref, wd16_ref,
                         un2, ut2, gt2, tail_ref, st_ref, h_ref, xn_ref,
                         *, tt, tiles_per_seq):
    wg16_ref[...] = wg32_ref[...].astype(BF16)
    wu16_ref[...] = wu32_ref[...].astype(BF16)
    wd16_ref[...] = wd32_ref[...].astype(BF16)
    L = CHUNK
    s = pl.program_id(0)
    t = (jnp.maximum(s, 1) - 1) % tiles_per_seq
    chunks = range(tt // L)
    slot_of = lambda i: (un2.at[i], ut2.at[i], gt2.at[i])

    @pl.when(s == 0)
    def _():
        for ref in slot_of(1) + (tail_ref,):
            ref[...] = jnp.zeros_like(ref)

    @pl.when(t == 0)
    def _():
        C_ref[...] = jnp.zeros_like(C_ref)
        st_ref[...] = jnp.zeros_like(st_ref)
        n_ref[...] = jnp.zeros_like(n_ref)
        m_ref[...] = jnp.zeros_like(m_ref)

    k_scale = HEAD_DIM ** -0.5

    def project(slot, part):
        un_ref, ut_ref, gt_ref = slot
        xn = xn_ref[...]
        un_ref[part, UOFF:UOFF + tt, :] = jnp.dot(xn, wn_ref[part], preferred_element_type=F32)
        res = lax.dot_general(wt_ref[part], xn, NT_DIMS, preferred_element_type=F32)
        r0 = pl.multiple_of(part * MIX, MIX)
        for c in chunks:
            ut_ref[c, pl.ds(r0, MIX), :] = res[:, c * L:(c + 1) * L]

    src_id = lax.broadcasted_iota(jnp.int32, (L, L), 0)
    tgt_id = lax.broadcasted_iota(jnp.int32, (L, L), 1)
    causal = src_id <= tgt_id
    triu_bf = jnp.where(causal, 1.0, 0.0).astype(BF16)
    heads = range(HEADS)
    hcol = lambda base, h: slice(base + h * HEAD_DIM, base + (h + 1) * HEAD_DIM)

    def chunk_body(c, slot):
        un_ref, ut_ref, gt_ref = slot
        r0 = pl.multiple_of(c * L, L)
        rows = pl.ds(UOFF + r0, L)
        trows = pl.ds(r0, L)
        g_t = gt_ref[c] + bgt_ref[...]
        ig = g_t[0:8]
        lf = _log_sigmoid(g_t[8:16])
        lf_hi = lf.astype(BF16)
        r1 = lf - lf_hi.astype(F32)
        lf_mid = r1.astype(BF16)
        lf_lo = (r1 - lf_mid.astype(F32)).astype(BF16)
        bc = (jnp.dot(lf_hi, triu_bf, preferred_element_type=F32)
              + jnp.dot(lf_mid, triu_bf, preferred_element_type=F32)
              + jnp.dot(lf_lo, triu_bf, preferred_element_type=F32))
        m_prev = m_ref[...]
        inter = bc + m_prev
        b_last = bc[:, L - 1:L]
        g_w = b_last - bc + ig
        m_new = jnp.maximum(b_last + m_prev, jnp.max(g_w, axis=1, keepdims=True))
        ws = jnp.exp(g_w - m_new)
        carry = jnp.exp(b_last + m_prev - m_new)
        a_n = jnp.concatenate([ig - bc, jnp.zeros((L - 8, L), F32)], axis=0).T

        def conv_act(col):
            win = un_ref[col // MIX, pl.ds(r0, L + UOFF), hcol(0, col % MIX // HEAD_DIM)]
            acc = bconv_ref[:, col:col + HEAD_DIM]
            for j in range(CONV_W):
                back = CONV_W - 1 - j
                tap = win if back == 0 else pltpu.roll(win, back, 0)
                acc = acc + tap[UOFF:UOFF + L] * wconv_ref[j:j + 1, col:col + HEAD_DIM]
            return _silu(acc)

        cos_f, sin_s = cos_ref[trows, :], sin_ref[trows, :]
        q = [conv_act(N_QK + h * HEAD_DIM) for h in heads]
        kb = [(conv_act(N_QK + MIX + h * HEAD_DIM) * k_scale).astype(BF16) for h in heads]
        qb = [a.astype(BF16) for a in q]
        rqb = [_rope(un_ref[N_RQ // MIX, rows, hcol(0, h)], cos_f, sin_s).astype(BF16) for h in heads]
        rk = [_rope(un_ref[N_RK // MIX, rows, hcol(0, h)], cos_f, sin_s) * k_scale for h in heads]
        v_t = [ut_ref[c, hcol(T_V, h), :] for h in heads]
        rvb = [ut_ref[c, hcol(T_RV, h), :].astype(BF16) for h in heads]
        c_old = [C_ref[h] for h in heads]
        s_old = [st_ref[h] for h in heads]
        n_old = [n_ref[h:h + 1, :] for h in heads]
        qk = [_dot_nt(kb[h], qb[h]) for h in heads]
        att = [_dot_nt(rk[h], rqb[h]) for h in heads]
        c_q = [_dot_nt(c_old[h], qb[h]) for h in heads]
        s_q = [_dot_nt(s_old[h], rqb[h]) for h in heads]
        n_q = [_dot_nt(jnp.broadcast_to(n_old[h], (8, HEAD_DIM)), qb[h])[0:1] for h in heads]
        d_c = [_dot(v_t[h] * ws[h:h + 1, :], kb[h]) for h in heads]
        d_n = [_dot(jnp.broadcast_to(ws[h:h + 1, :], (8, L)), kb[h])[0:1] for h in heads]
        d_s = [_dot(rvb[h], rk[h] * kdec_ref[h]) for h in heads]
        m_t, w_in, wts = [], [], []
        for h in heads:
            dmat = a_n[:, h:h + 1] + bc[h:h + 1, :]
            dmat = jnp.where(causal, dmat, -jnp.inf)
            m_t.append(jnp.maximum(inter[h:h + 1, :], jnp.max(dmat, axis=0, keepdims=True)))
            wts.append(jnp.exp(dmat - m_t[h]) * qk[h])
            w_in.append(jnp.exp(inter[h:h + 1, :] - m_t[h]))
        att_w = [att[h] * decay_ref[h] for h in heads]
        v_p = [_dot(v_t[h], wts[h]) for h in heads]
        v_a = [_dot(rvb[h], att_w[h]) for h in heads]
        for h in heads:
            num = v_p[h] + w_in[h] * c_q[h]
            den = jnp.sum(wts[h], axis=0, keepdims=True) + w_in[h] * n_q[h]
            hm = num / jnp.maximum(jnp.abs(den), jnp.exp(-m_t[h]))
            hm = hm * lax.rsqrt(jnp.mean(hm * hm, axis=0, keepdims=True) + EPS)
            hm = hm * gmh_ref[hcol(0, h), :] * jax.nn.sigmoid(ut_ref[c, hcol(T_O, h), :])
            h_ref[trows, hcol(0, h)] = hm.T.astype(BF16)
            carry_h = carry[h:h + 1, :]
            C_ref[h] = carry_h * c_old[h] + d_c[h]
            n_ref[h:h + 1, :] = carry_h * n_old[h] + d_n[h]
            o = v_a[h] + qdec_ref[h:h + 1, :] * s_q[h]
            st_ref[h] = cdec_ref[h:h + 1, :] * s_old[h] + d_s[h]
            hr = o * lax.rsqrt(jnp.mean(o * o, axis=0, keepdims=True) + EPS)
            hr = hr * grh_ref[hcol(0, h), :] * _silu(ut_ref[c, hcol(T_RG, h), :])
            h_ref[trows, hcol(MIX, h)] = hr.T.astype(BF16)

        m_ref[...] = m_new

    proj_slot, scan_slot = slot_of(s % 2), slot_of(1 - s % 2)
    un_ref = scan_slot[0]
    qk_blocks = range((N_RQ - N_QK) // MIX)
    for i in qk_blocks:
        un_ref[i, 0:UOFF, :] = jnp.where(t == 0, 0.0, tail_ref[:, i * MIX:(i + 1) * MIX])
    xn_ref[...] = _rms(x_ref[...], gmix_ref[...]).astype(BF16)
    gates_t = lax.dot_general(wgt_ref[...], xn_ref[...], NT_DIMS, preferred_element_type=F32)
    for c in chunks:
        proj_slot[2][c] = gates_t[:, c * L:(c + 1) * L]

    def quarter(c, carry_unused):
        project(proj_slot, c)
        chunk_body(c, scan_slot)
        return carry_unused

    lax.fori_loop(0, tt // L, quarter, 0)
    for i in qk_blocks:
        tail_ref[:, i * MIX:(i + 1) * MIX] = un_ref[i, tt:tt + UOFF, :]
    x1_ref[...] = xs_ref[...] + jnp.dot(h_ref[...], wout_ref[...], preferred_element_type=F32)

    @pl.when(t == tiles_per_seq - 1)
    def _():
        conv_ref[...] = tail_ref[UOFF - (CONV_W - 1):UOFF, :]
        for h in heads:
            S_ref[h] = st_ref[h].T


def _mixer_prompt(x, wn, wt, wgt, w_out, g_mix, bgt, w_conv, b_conv, gmh_cols, grh_cols, cos_f, sin_s,
                  decay_t, qdec_rows, kdec_cols, cdec_rows, w_gate, w_up, w_down, tt=512):
    B, T, _ = x.shape
    tps = T // tt
    n_tiles = B * tps
    proj = lambda s: jnp.minimum(s, n_tiles - 1)
    scan = lambda s: jnp.maximum(s, 1) - 1
    per_b3 = lambda s: (scan(s) // tps, 0, 0)
    per_b4 = lambda s: (scan(s) // tps, 0, 0, 0)
    scan_tile = pl.BlockSpec((None, tt, D_MODEL), lambda s: (scan(s) // tps, scan(s) % tps, 0))
    in_specs = [
        pl.BlockSpec((None, tt, D_MODEL), lambda s: (proj(s) // tps, proj(s) % tps, 0)),
        scan_tile,
        _const_spec((N_COLS // MIX, D_MODEL, MIX)), _const_spec((T_ROWS // MIX, MIX, D_MODEL)),
        _const_spec((GT_ROWS, D_MODEL)),
        _const_spec((D_MODEL, D_MODEL)),
        _const_spec((1, D_MODEL)), _const_spec((GT_ROWS, CHUNK)),
        _const_spec((CONV_W, 2 * MIX)), _const_spec((1, 2 * MIX)),
        _const_spec((MIX, CHUNK)), _const_spec((MIX, CHUNK)),
        pl.BlockSpec((tt, HEAD_DIM), lambda s: (scan(s) % tps, 0)),
        pl.BlockSpec((tt, HEAD_DIM), lambda s: (scan(s) % tps, 0)),
        _const_spec((HEADS, CHUNK, CHUNK)),
        _const_spec((8, CHUNK)), _const_spec((HEADS, CHUNK, HEAD_DIM)), _const_spec((8, HEAD_DIM)),
    ]
    ff_blocks = D_FF // HEAD_DIM
    assert ff_blocks <= n_tiles + 1
    early = lambda s: jnp.minimum(s, ff_blocks - 1)
    late = lambda s: jnp.maximum(s - (n_tiles + 1 - ff_blocks), 0)
    cast_specs = [pl.BlockSpec((D_MODEL, HEAD_DIM), lambda s: (0, early(s))),
                  pl.BlockSpec((D_MODEL, HEAD_DIM), lambda s: (0, late(s))),
                  pl.BlockSpec((HEAD_DIM, D_MODEL), lambda s: (late(s), 0))]
    in_specs += cast_specs
    out_specs = [
        scan_tile,
        pl.BlockSpec((None, CONV_W - 1, 2 * MIX), per_b3),
        pl.BlockSpec((None, HEADS, HEAD_DIM, HEAD_DIM), per_b4),
        pl.BlockSpec((None, HEADS, HEAD_DIM), per_b3),
        pl.BlockSpec((None, 8, CHUNK), per_b3),
        pl.BlockSpec((None, HEADS, HEAD_DIM, HEAD_DIM), per_b4),
    ] + cast_specs
    out_shape = [
        jax.ShapeDtypeStruct((B, T, D_MODEL), F32),
        jax.ShapeDtypeStruct((B, CONV_W - 1, 2 * MIX), F32),
        jax.ShapeDtypeStruct((B, HEADS, HEAD_DIM, HEAD_DIM), F32),
        jax.ShapeDtypeStruct((B, HEADS, HEAD_DIM), F32),
        jax.ShapeDtypeStruct((B, 8, CHUNK), F32),
        jax.ShapeDtypeStruct((B, HEADS, HEAD_DIM, HEAD_DIM), F32),
        jax.ShapeDtypeStruct((D_MODEL, D_FF), BF16), jax.ShapeDtypeStruct((D_MODEL, D_FF), BF16),
        jax.ShapeDtypeStruct((D_FF, D_MODEL), BF16),
    ]
    n_chunks = tt // CHUNK
    scratch = [
        pltpu.VMEM((2, N_COLS // MIX, tt + UOFF, MIX), F32),
        pltpu.VMEM((2, n_chunks, T_ROWS, CHUNK), F32),
        pltpu.VMEM((2, n_chunks, GT_ROWS, CHUNK), F32),
    ] + [
        pltpu.VMEM((UOFF, 2 * MIX), F32),
        pltpu.VMEM((HEADS, HEAD_DIM, HEAD_DIM), F32),
        pltpu.VMEM((tt, D_MODEL), BF16),
        pltpu.VMEM((tt, D_MODEL), BF16),
    ]
    return pl.pallas_call(
        functools.partial(_mixer_prompt_kernel, tt=tt, tiles_per_seq=tps),
        grid=(n_tiles + 1,),
        in_specs=in_specs, out_specs=out_specs, out_shape=out_shape,
        scratch_shapes=scratch,
        compiler_params=_params("arbitrary"),
    )(x, x, wn, wt.reshape(T_ROWS // MIX, MIX, D_MODEL), wgt, w_out, g_mix, bgt, w_conv, b_conv,
      gmh_cols, grh_cols, cos_f, sin_s,
      decay_t, qdec_rows, kdec_cols, cdec_rows, w_gate, w_up, w_down)


def _attn_prompt_kernel(x1_ref, k_ref, v_ref, wcq_ref, gx_ref, *rest, n_sample_steps):
    sample_in, o_ref = rest[:SAMPLE_MIXER_INPUTS], rest[SAMPLE_MIXER_INPUTS]
    sample_out_and_scratch = rest[SAMPLE_MIXER_INPUTS + 1:]
    step = pl.program_id(0) * pl.num_programs(1) + pl.program_id(1)

    @pl.when(step < n_sample_steps)
    def _():
        _mixer_sample_kernel(*sample_in, *sample_out_and_scratch, bb=SAMPLE_BLOCK)

    tm = x1_ref.shape[0]
    rows = [slice(r, r + tm // ROW_GROUPS) for r in range(0, tm, tm // ROW_GROUPS)]
    xq = [_rms(x1_ref[r, :], gx_ref[...]).astype(BF16) for r in rows]
    q = [jnp.dot(a, wcq_ref[...], preferred_element_type=F32).astype(BF16) for a in xq]
    sl = [slice(h * X_HEAD_DIM, (h + 1) * X_HEAD_DIM) for h in range(X_HEADS)]
    items = [(g, h) for g in range(ROW_GROUPS) for h in range(X_HEADS)]
    s = [_dot_nt(q[g][:, sl[h]], k_ref[:, sl[h]]) * (X_HEAD_DIM ** -0.5) for g, h in items]
    e = [jnp.exp(a - jnp.max(a, axis=-1, keepdims=True)) for a in s]
    p = [a / jnp.sum(a, axis=-1, keepdims=True) for a in e]
    for (g, h), a in zip(items, p):
        o_ref[rows[g], sl[h]] = _dot(a, v_ref[:, sl[h]]).astype(BF16)


def _attn_prompt(x1, mk, mv, w_cq, g_x, sample_mixer_operands, tm=512):
    B, T, _ = x1.shape
    tps = T // tm
    Bs = sample_mixer_operands[0].shape[0]
    n_sample_steps = Bs // SAMPLE_BLOCK
    assert n_sample_steps <= B * tps
    block_of = lambda b, t: jnp.minimum(b * tps + t, n_sample_steps - 1)
    s_in, s_out, s_shape, s_scratch = _sample_mixer_specs(Bs, block_of)
    tile = pl.BlockSpec((None, tm, D_MODEL), lambda b, t: (b, t, 0))
    kv = pl.BlockSpec((None, N_MEM, D_MODEL), lambda b, t: (b, 0, 0))
    outs = pl.pallas_call(
        functools.partial(_attn_prompt_kernel, n_sample_steps=n_sample_steps),
        grid=(B, tps),
        in_specs=[tile, kv, kv, _const_spec((D_MODEL, D_MODEL)), _const_spec((1, D_MODEL))] + s_in,
        out_specs=[tile] + s_out,
        out_shape=[jax.ShapeDtypeStruct((B, T, D_MODEL), BF16)] + s_shape,
        scratch_shapes=s_scratch,
        compiler_params=_params("arbitrary", "arbitrary"),
    )(x1, mk, mv, w_cq, g_x, *sample_mixer_operands)
    return outs[0], outs[1:]


V7X_MXU_DIM = 256
FF_CHUNKS = ((0, 6 * V7X_MXU_DIM), (6 * V7X_MXU_DIM, D_FF))


def _ffn_kernel(x1_ref, o_ref, wco_ref, wg_ref, wu_ref, wd_ref, gffn_ref, gfin_ref, *rest, n_groups):
    if len(rest) == 1:
        (y_ref,) = rest
        sample_stages = iter(())
    else:
        qs_ref, ck_ref, cv_ref, y_ref, os_ref = rest
        sample_stages = _sample_attention(qs_ref, ck_ref, cv_ref, os_ref)
    tm = x1_ref.shape[0]
    rows = [slice(r, r + tm // n_groups) for r in range(0, tm, tm // n_groups)]
    acc = [x1_ref[r, :] + _dot(o_ref[r, :], wco_ref[...]) for r in rows]
    hf = [_rms(a, gffn_ref[...]).astype(BF16) for a in acc]
    for c0, c1 in FF_CHUNKS:
        next(sample_stages, None)
        gate = [jnp.dot(h, wg_ref[:, c0:c1], preferred_element_type=F32) for h in hf]
        up = [jnp.dot(h, wu_ref[:, c0:c1], preferred_element_type=F32) for h in hf]
        act = [_silu(g) * u for g, u in zip(gate, up)]
        acc = [a + _dot(p, wd_ref[c0:c1, :]) for a, p in zip(acc, act)]
    next(sample_stages, None)
    for r, a in zip(rows, acc):
        y_ref[r, :] = _rms(a, gfin_ref[...])


def _ffn(x1, o, w_co, w_gate, w_up, w_down, g_ffn, g_final, tm, sample=None):
    rows = x1.shape[0]
    steps = rows // tm
    row_spec = pl.BlockSpec((tm, D_MODEL), lambda i: (i, 0))
    in_specs = [row_spec, row_spec, _const_spec((D_MODEL, D_MODEL)),
                _const_spec((D_MODEL, D_FF)), _const_spec((D_MODEL, D_FF)),
                _const_spec((D_FF, D_MODEL)), _const_spec((1, D_MODEL)), _const_spec((1, D_MODEL))]
    out_specs = [row_spec]
    out_shape = [jax.ShapeDtypeStruct((rows, D_MODEL), F32)]
    operands = [x1, o, w_co, w_gate, w_up, w_down, g_ffn, g_final]
    if sample is not None:
        q_s, ck_s, cv_s = sample
        Bs = q_s.shape[0]
        ba = Bs // steps
        assert ba * steps == Bs
        s_rows = pl.BlockSpec((ba, QROWS, 128), lambda i: (i, 0, 0))
        s_kv = pl.BlockSpec((ba, KV_ROWS, 128), lambda i: (i, 0, 0))
        in_specs += [s_rows, s_kv, s_kv]
        out_specs += [s_rows]
        out_shape += [jax.ShapeDtypeStruct((Bs, QROWS, 128), F32)]
        operands += [q_s.reshape(Bs, QROWS, 128), ck_s, cv_s]
    outs = pl.pallas_call(
        functools.partial(_ffn_kernel, n_groups=ROW_GROUPS if tm >= 512 else 1),
        grid=(steps,),
        in_specs=in_specs, out_specs=out_specs, out_shape=out_shape,
        compiler_params=_params("parallel"),
    )(*operands)
    if sample is None:
        return outs[0]
    return outs[0], outs[1].reshape(Bs, D_MODEL)


def _inproj_sample_kernel(x_ref, g_ref, wn_ref, wt_ref, wgt_ref, un_ref, ut_ref, gs_ref):
    xn = _rms(x_ref[...], g_ref[...]).astype(BF16)
    for i in range(N_COLS // MIX):
        un_ref[:, i * MIX:(i + 1) * MIX] = jnp.dot(xn, wn_ref[i], preferred_element_type=F32)
    ut_ref[...] = lax.dot_general(xn, wt_ref[...], NT_DIMS, preferred_element_type=F32)
    gs_ref[...] = lax.dot_general(xn, wgt_ref[...], NT_DIMS, preferred_element_type=F32)


def _inproj_sample(x, g_mix, wn, wt, wgt):
    rows = x.shape[0]
    full = lambda cols: pl.BlockSpec((rows, cols), lambda i: (0, 0))
    return pl.pallas_call(
        _inproj_sample_kernel,
        grid=(1,),
        in_specs=[_const_spec((rows, D_MODEL)), _const_spec((1, D_MODEL)),
                  _const_spec((N_COLS // MIX, D_MODEL, MIX)), _const_spec((T_ROWS, D_MODEL)),
                  _const_spec((GT_ROWS, D_MODEL))],
        out_specs=[full(N_COLS), full(T_ROWS), full(GT_ROWS)],
        out_shape=[jax.ShapeDtypeStruct((rows, N_COLS), F32),
                   jax.ShapeDtypeStruct((rows, T_ROWS), F32),
                   jax.ShapeDtypeStruct((rows, GT_ROWS), F32)],
        compiler_params=_params("arbitrary"),
    )(x, g_mix, wn, wt, wgt)


def _mixer_sample_kernel(un_ref, ut_ref, gs_ref, conv_ref, C_ref, n_ref, m_ref, S_ref, bg_ref,
                         wconv_ref, bconv_ref, gmh_ref, grh_ref, cos_ref, sin_ref, rtab_ref,
                         h_ref, convo_ref, Co_ref, no_ref, mo_ref, So_ref,
                         q_s, k_s, vws_s, qr_s, kr_s, qc_s, qs_s, carry_s, *, bb):
    k_scale = HEAD_DIM ** -0.5
    uqk = un_ref[:, N_QK:N_RQ]
    conv = (bconv_ref[...] + wconv_ref[0:1, :] * conv_ref[0] + wconv_ref[1:2, :] * conv_ref[1]
            + wconv_ref[2:3, :] * conv_ref[2] + wconv_ref[3:4, :] * uqk)
    convo_ref[0] = conv_ref[1]
    convo_ref[1] = conv_ref[2]
    convo_ref[2] = uqk
    qk_act = _silu(conv)
    q_s[...] = qk_act[:, 0:MIX]
    k_s[...] = qk_act[:, MIX:2 * MIX] * k_scale

    gates = gs_ref[...] + bg_ref[...]
    ig = gates[:, 0:HEADS]
    lf = _log_sigmoid(gates[:, 8:8 + HEADS])
    inter = lf + m_ref[...]
    m_t = jnp.maximum(inter, ig)
    ws = jnp.exp(ig - m_t)
    w_in = jnp.exp(inter - m_t)
    mo_ref[...] = m_t
    carry_s[:, 0:HEADS] = w_in

    cos_f = cos_ref[...]
    sin_s = sin_ref[...]
    for h in range(HEADS):
        lo = h * HEAD_DIM
        hs = slice(lo, lo + HEAD_DIM)
        vws_s[:, hs] = ut_ref[:, T_V + lo:T_V + lo + HEAD_DIM] * ws[:, h:h + 1]
        qr_s[:, hs] = _rope(un_ref[:, N_RQ + lo:N_RQ + lo + HEAD_DIM], cos_f, sin_s)
        kr_s[:, hs] = _rope(un_ref[:, N_RK + lo:N_RK + lo + HEAD_DIM], cos_f, sin_s) * k_scale

    heads = range(HEADS)
    seqs = range(bb)
    hsl = [slice(h * HEAD_DIM, (h + 1) * HEAD_DIM) for h in heads]
    seq_id = lax.broadcasted_iota(jnp.int32, (bb, HEAD_DIM), 0)
    for h in heads:
        q_h, k_h, vws_h = q_s[:, hsl[h]], k_s[:, hsl[h]], vws_s[:, hsl[h]]
        qr_h = qr_s[:, hsl[h]]
        krd_h = kr_s[:, hsl[h]] * rtab_ref[1:2, h:h + 1]
        rv_h = ut_ref[:, T_RV + h * HEAD_DIM:T_RV + (h + 1) * HEAD_DIM]
        c_old = [C_ref[b, h] for b in seqs]
        s_old = [S_ref[b, h] for b in seqs]
        q_c = [_dot_nt(q_h, c_old[b]) for b in seqs]
        q_st = [_dot(qr_h, s_old[b]) for b in seqs]
        d_c = [_dot_tn(jnp.where(seq_id == b, vws_h, 0.0), k_h) for b in seqs]
        d_s = [_dot_tn(jnp.where(seq_id == b, krd_h, 0.0), rv_h) for b in seqs]
        for b in seqs:
            row = slice(b, b + 1)
            qc_s[row, hsl[h]] = q_c[b][row]
            qs_s[row, hsl[h]] = q_st[b][row]
            Co_ref[b, h] = carry_s[row, h:h + 1] * c_old[b] + d_c[b]
            So_ref[b, h] = rtab_ref[2:3, h:h + 1] * s_old[b] + d_s[b]

    for h in range(HEADS):
        lo = h * HEAD_DIM
        hs = slice(lo, lo + HEAD_DIM)
        q_h, k_h, n_h = q_s[:, hs], k_s[:, hs], n_ref[:, hs]
        ws_h, w_in_h = ws[:, h:h + 1], w_in[:, h:h + 1]
        wts = ws_h * jnp.sum(q_h * k_h, axis=1, keepdims=True)
        num = wts * ut_ref[:, T_V + lo:T_V + lo + HEAD_DIM] + w_in_h * qc_s[:, hs]
        den = wts + w_in_h * jnp.sum(n_h * q_h, axis=1, keepdims=True)
        hm = num / jnp.maximum(jnp.abs(den), jnp.exp(-m_t[:, h:h + 1]))
        mo = ut_ref[:, T_O + lo:T_O + lo + HEAD_DIM]
        h_ref[:, hs] = _head_norm(hm) * gmh_ref[:, hs] * jax.nn.sigmoid(mo)
        no_ref[:, hs] = w_in_h * n_h + ws_h * k_h

        att = jnp.sum(qr_s[:, hs] * kr_s[:, hs], axis=1, keepdims=True) * rtab_ref[3:4, h:h + 1]
        o = (att * ut_ref[:, T_RV + lo:T_RV + lo + HEAD_DIM]
             + rtab_ref[0:1, h:h + 1] * qs_s[:, hs])
        rg = ut_ref[:, T_RG + lo:T_RG + lo + HEAD_DIM]
        h_ref[:, MIX + lo:MIX + lo + HEAD_DIM] = _head_norm(o) * grh_ref[:, hs] * _silu(rg)


SAMPLE_BLOCK = 8
SAMPLE_MIXER_INPUTS = 16


def _sample_mixer_specs(B, block_of):
    bb = SAMPLE_BLOCK
    rows = lambda cols: pl.BlockSpec((bb, cols), lambda *g: (block_of(*g), 0))
    mats = pl.BlockSpec((bb, HEADS, HEAD_DIM, HEAD_DIM), lambda *g: (block_of(*g), 0, 0, 0))
    convs = pl.BlockSpec((CONV_W - 1, bb, 2 * MIX), lambda *g: (0, block_of(*g), 0))
    in_specs = [rows(N_COLS), rows(T_ROWS), rows(GT_ROWS), convs, mats, rows(MIX), rows(HEADS), mats,
                _const_spec((1, GT_ROWS)), _const_spec((CONV_W, 2 * MIX)), _const_spec((1, 2 * MIX)),
                _const_spec((1, MIX)), _const_spec((1, MIX)),
                _const_spec((1, HEAD_DIM)), _const_spec((1, HEAD_DIM)), _const_spec((8, HEAD_DIM))]
    assert len(in_specs) == SAMPLE_MIXER_INPUTS
    out_specs = [rows(D_MODEL), convs, mats, rows(MIX), rows(HEADS), mats]
    out_shape = [jax.ShapeDtypeStruct((B, D_MODEL), F32),
                 jax.ShapeDtypeStruct((CONV_W - 1, B, 2 * MIX), F32),
                 jax.ShapeDtypeStruct((B, HEADS, HEAD_DIM, HEAD_DIM), F32),
                 jax.ShapeDtypeStruct((B, MIX), F32),
                 jax.ShapeDtypeStruct((B, HEADS), F32),
                 jax.ShapeDtypeStruct((B, HEADS, HEAD_DIM, HEAD_DIM), F32)]
    scratch = [pltpu.VMEM((bb, MIX), F32) for _ in range(7)] + [pltpu.VMEM((bb, HEAD_DIM), F32)]
    return in_specs, out_specs, out_shape, scratch


def _outq_sample_kernel(x_ref, h_ref, wout_ref, wcq_ref, gx_ref, x1_ref, q_ref):
    x1 = x_ref[...] + _dot(h_ref[...], wout_ref[...])
    x1_ref[...] = x1
    q_ref[...] = _dot(_rms(x1, gx_ref[...]), wcq_ref[...])


def _outq_sample(x, hcat, w_out, w_cq, g_x):
    rows = x.shape[0]
    full = _const_spec((rows, D_MODEL))
    return pl.pallas_call(
        _outq_sample_kernel,
        grid=(1,),
        in_specs=[full, full, _const_spec((D_MODEL, D_MODEL)), _const_spec((D_MODEL, D_MODEL)),
                  _const_spec((1, D_MODEL))],
        out_specs=[pl.BlockSpec((rows, D_MODEL), lambda i: (0, 0))] * 2,
        out_shape=[jax.ShapeDtypeStruct((rows, D_MODEL), F32)] * 2,
        compiler_params=_params("arbitrary"),
    )(x, hcat, w_out, w_cq, g_x)


def _sample_attention(q_ref, k_ref, v_ref, o_ref):
    r_id = lax.broadcasted_iota(jnp.int32, (QROWS, KV_ROWS), 0)
    n_id = lax.broadcasted_iota(jnp.int32, (QROWS, KV_ROWS), 1)
    own = (n_id & 7) == (r_id >> 1) + 4 * (r_id & 1)
    low_half = (lax.broadcasted_iota(jnp.int32, (1, KV_ROWS), 1) & 4) == 0
    seqs = range(q_ref.shape[0])
    z = [_dot_nt(q_ref[j], k_ref[j]) for j in seqs]
    zc = [jnp.sum(jnp.where(own, z[j], 0.0), axis=0, keepdims=True) for j in seqs]
    yield
    other = [jnp.where(low_half, pltpu.roll(zc[j], KV_ROWS - 4, 1), pltpu.roll(zc[j], 4, 1))
             for j in seqs]
    s = [jnp.where(own, (zc[j] + other[j]) * (X_HEAD_DIM ** -0.5), -jnp.inf) for j in seqs]
    m = [jnp.max(s[j], axis=-1, keepdims=True) for j in seqs]
    e = [jnp.exp(s[j] - m[j]) for j in seqs]
    den = [jnp.sum(e[j], axis=-1, keepdims=True) for j in seqs]
    yield
    for j in seqs:
        o_ref[j] = _dot(e[j] / den[j], v_ref[j])
    yield


def _kv_rows(cache):
    B = cache.shape[0]
    c5 = cache.reshape(B, N_MEM, X_HEADS, 2, 128)
    return jnp.transpose(c5, (0, 1, 3, 2, 4)).reshape(B, KV_ROWS, 128)


def _kv_from_rows(rows):
    B = rows.shape[0]
    r5 = rows.reshape(B, N_MEM, 2, X_HEADS, 128)
    return jnp.transpose(r5, (0, 1, 3, 2, 4)).reshape(1, B, N_MEM, X_HEADS, X_HEAD_DIM)


def _rope_tables(pos):
    half = HEAD_DIM // 2
    inv = ROPE_THETA ** (-np.arange(half, dtype=np.float64) / half)
    ang = np.asarray(pos, np.float64)[:, None] * inv[None, :]
    cos, sin = np.cos(ang), np.sin(ang)
    return (np.concatenate([cos, cos], axis=-1).astype(np.float32),
            np.concatenate([-sin, sin], axis=-1).astype(np.float32))


def _retention_tables(L):
    lg = np.log1p(-np.exp2(-5.0 - np.arange(HEADS, dtype=np.float64)))
    t = np.arange(L, dtype=np.float64)
    diff = t[:, None] - t[None, :]
    decay = np.where(diff >= 0, np.exp(lg[:, None, None] * np.maximum(diff, 0.0)), 0.0)
    q_dec = np.exp(lg[:, None] * (t + 1.0))
    k_dec = np.exp(lg[:, None] * (L - 1.0 - t))
    chunk_dec = np.exp(lg * L)
    return tuple(a.astype(np.float32) for a in (decay, q_dec, k_dec, chunk_dec))


def _lanes(a, n):
    xp = np if isinstance(a, np.ndarray) else jnp
    return xp.broadcast_to(a[..., None], a.shape + (n,))


def _pad_rows(a, rows):
    return np.pad(a, ((0, rows - a.shape[0]),) + ((0, 0),) * (a.ndim - 1))


def kernel(x_prompt, x_sample, cache_mem_k, cache_mem_v, state_mlstm_conv, state_mlstm_C, state_mlstm_n, state_mlstm_m, state_ret_S, mem_prompt, w_in, b_gate, w_conv, b_conv, g_mix, g_mhead, g_rhead, w_out, g_xattn, g_mem, w_ck, w_cv, w_cq, w_co, g_ffn, w_gate, w_up, w_down, g_final):
    Bp, Tp, _ = x_prompt.shape
    Bs = x_sample.shape[0]
    l = 0
    n_m = 4 * MIX
    wi = w_in[l]
    w_gates = wi[:, n_m:n_m + 2 * HEADS]
    w_ret = wi[:, n_m + 2 * HEADS:]
    row = lambda a: a.reshape(1, -1)
    bf = lambda a: a.astype(BF16)
    g_mix_r, g_mh_r, g_rh_r = row(g_mix[l]), row(g_mhead[l]), row(g_rhead[l])
    g_x_r, g_mem_r, g_ffn_r, g_fin_r = row(g_xattn[l]), row(g_mem[l]), row(g_ffn[l]), row(g_final)
    b_conv_r = row(b_conv[l])
    w_out_b, w_cq_b, w_co_b = bf(w_out[l]), bf(w_cq[l]), bf(w_co[l])

    mk_b, mv_b, mk_rows, mv_rows = _memkv(mem_prompt, g_mem_r, bf(w_ck[l]), bf(w_cv[l]))
    wn = bf(jnp.stack([wi[:, :MIX], wi[:, MIX:2 * MIX], w_ret[:, :MIX], w_ret[:, MIX:2 * MIX]]))
    wt = bf(jnp.concatenate([wi[:, 2 * MIX:n_m], w_ret[:, 2 * MIX:]], axis=1).T)
    gate_rows = lambda a: jnp.concatenate(
        [a[:HEADS], jnp.zeros((8 - HEADS,) + a.shape[1:], F32),
         a[HEADS:], jnp.zeros((8 - HEADS,) + a.shape[1:], F32)], axis=0)
    wgt = bf(gate_rows(w_gates.T))
    bg_rows = gate_rows(b_gate[l][:, None])
    bgt = jnp.broadcast_to(bg_rows, (GT_ROWS, CHUNK))
    cos_p, sin_p = _rope_tables(np.arange(Tp))
    decay, q_dec, k_dec, chunk_dec = _retention_tables(CHUNK)
    x1_p, conv_p, C_p, n_p, m_p, S_p, w_gate_b, w_up_b, w_down_b = _mixer_prompt(
        x_prompt, wn, wt, wgt, w_out_b, g_mix_r, bgt, w_conv[l], b_conv_r,
        _lanes(g_mhead[l], CHUNK), _lanes(g_rhead[l], CHUNK), cos_p, sin_p,
        np.swapaxes(decay, 1, 2), _pad_rows(q_dec, 8), _lanes(k_dec, HEAD_DIM),
        _pad_rows(_lanes(chunk_dec, HEAD_DIM), 8), w_gate[l], w_up[l], w_down[l])

    xs = x_sample.reshape(Bs, D_MODEL)
    un_s, ut_s, gs_s = _inproj_sample(xs, g_mix_r, wn, wt, wgt)
    cos_s, sin_s = _rope_tables(PAST_LEN + np.arange(1))
    decay1, q_dec1, k_dec1, chunk_dec1 = _retention_tables(1)
    rtab = np.zeros((8, HEAD_DIM), np.float32)
    rtab[:4, :HEADS] = np.stack([q_dec1[:, 0], k_dec1[:, 0], chunk_dec1, decay1[:, 0, 0]])
    sample_mixer_operands = (
        un_s, ut_s, gs_s, jnp.transpose(state_mlstm_conv[l], (1, 0, 2)), state_mlstm_C[l],
        state_mlstm_n[l].reshape(Bs, MIX), state_mlstm_m[l], state_ret_S[l],
        bg_rows.reshape(1, GT_ROWS), w_conv[l], b_conv_r, g_mh_r, g_rh_r, cos_s, sin_s, rtab)

    o_p, (hcat_s, conv_s, C_s, n_s, m_s, S_s) = _attn_prompt(
        x1_p, mk_b, mv_b, w_cq_b, g_x_r, sample_mixer_operands)
    x1_s, q_s = _outq_sample(xs, hcat_s, w_out_b, w_cq_b, g_x_r)
    y_p, o_s = _ffn(x1_p.reshape(Bp * Tp, D_MODEL), o_p.reshape(Bp * Tp, D_MODEL),
                    w_co_b, w_gate_b, w_up_b, w_down_b, g_ffn_r, g_fin_r, tm=512,
                    sample=(q_s, _kv_rows(cache_mem_k[l]), _kv_rows(cache_mem_v[l])))
    y_s = _ffn(x1_s, o_s, w_co_b, w_gate_b, w_up_b, w_down_b, g_ffn_r, g_fin_r, tm=Bs)

    return (y_p.reshape(Bp, Tp, D_MODEL), y_s.reshape(Bs, 1, D_MODEL),
            _kv_from_rows(mk_rows), _kv_from_rows(mv_rows),
            conv_p[None], C_p[None], n_p[None], m_p[None, :, :HEADS, 0], S_p[None],
            jnp.transpose(conv_s, (1, 0, 2))[None], C_s[None],
            n_s.reshape(1, Bs, HEADS, HEAD_DIM), m_s[None], S_s[None])
```

```python
import functools

import jax
import jax.numpy as jnp
import numpy as np
from jax import lax
from jax.experimental import pallas as pl
from jax.experimental.pallas import tpu as pltpu

F32 = jnp.float32
BF16 = jnp.bfloat16

D_MODEL = 1024
HEADS = 4
HEAD_DIM = 128
MIX = HEADS * HEAD_DIM
CONV_W = 4
CHUNK = 128
N_MEM = 256
X_HEADS = 4
X_HEAD_DIM = 256
D_FF = 2816
ROPE_THETA = 10000.0
EPS = 1e-6
PAST_LEN = 16384
V7X_LANES = 128
X_HALVES = X_HEAD_DIM // V7X_LANES
KV_ROWS = N_MEM * X_HALVES * X_HEADS
QROWS = X_HALVES * X_HEADS

N_QK, N_RQ, N_RK, N_COLS = 0, 1024, 1536, 2048
T_V, T_O, T_RV, T_RG, T_ROWS = 0, 512, 1024, 1536, 2048
GT_ROWS = 16
UOFF = 8
ROW_GROUPS = 2

V7X_VMEM_LIMIT = 56 * 1024 * 1024

NT_DIMS = (((1,), (1,)), ((), ()))
TN_DIMS = (((0,), (0,)), ((), ()))


def _dot(a, b):
    return jnp.dot(a.astype(BF16), b.astype(BF16), preferred_element_type=F32)


def _dot_nt(a, b):
    return lax.dot_general(a.astype(BF16), b.astype(BF16), NT_DIMS, preferred_element_type=F32)


def _dot_tn(a, b):
    return lax.dot_general(a.astype(BF16), b.astype(BF16), TN_DIMS, preferred_element_type=F32)


def _rms(x, g):
    return x * lax.rsqrt(jnp.mean(x * x, axis=-1, keepdims=True) + EPS) * g


def _head_norm(h):
    return h * lax.rsqrt(jnp.mean(h * h, axis=-1, keepdims=True) + EPS)


def _silu(x):
    return x * jax.nn.sigmoid(x)


def _log_sigmoid(x):
    return jnp.minimum(x, 0.0) - jnp.log1p(jnp.exp(-jnp.abs(x)))


def _rope(x, cos_full, sin_signed):
    return x * cos_full + pltpu.roll(x, HEAD_DIM // 2, 1) * sin_signed


def _const_spec(shape):
    zeros = (0,) * len(shape)
    return pl.BlockSpec(shape, lambda *_: zeros, pipeline_mode=pl.Buffered(1))


def _params(*sem):
    return pltpu.CompilerParams(dimension_semantics=sem, vmem_limit_bytes=V7X_VMEM_LIMIT)


def _memkv_kernel(mem_ref, g_ref, wk_ref, wv_ref, kb_ref, vb_ref, krows_ref, vrows_ref):
    mn = _rms(mem_ref[...], g_ref[...]).astype(BF16)
    for w_ref, b_ref, rows_ref in ((wk_ref, kb_ref, krows_ref), (wv_ref, vb_ref, vrows_ref)):
        proj = jnp.dot(mn, w_ref[...], preferred_element_type=F32)
        b_ref[...] = proj.astype(BF16)
        for h in range(X_HEADS):
            for c in range(X_HALVES):
                lane0 = h * X_HEAD_DIM + c * V7X_LANES
                rows_ref[pl.ds(c * X_HEADS + h, N_MEM, stride=QROWS), :] = proj[:, lane0:lane0 + V7X_LANES]


def _memkv(mem, g_mem, wk, wv):
    B = mem.shape[0]
    tok = pl.BlockSpec((None, N_MEM, D_MODEL), lambda b: (b, 0, 0))
    rows = pl.BlockSpec((None, KV_ROWS, V7X_LANES), lambda b: (b, 0, 0))
    return pl.pallas_call(
        _memkv_kernel,
        grid=(B,),
        in_specs=[tok, _const_spec((1, D_MODEL)),
                  _const_spec((D_MODEL, D_MODEL)), _const_spec((D_MODEL, D_MODEL))],
        out_specs=[tok, tok, rows, rows],
        out_shape=[jax.ShapeDtypeStruct((B, N_MEM, D_MODEL), BF16)] * 2
        + [jax.ShapeDtypeStruct((B, KV_ROWS, V7X_LANES), F32)] * 2,
        compiler_params=_params("parallel"),
    )(mem, g_mem, wk, wv)


def _mixer_prompt_kernel(x_ref, xs_ref, wn_ref, wt_ref, wgt_ref, wout_ref, gmix_ref, bgt_ref,
                         wconv_ref, bconv_ref, gmh_ref, grh_ref, cos_ref, sin_ref,
                         decay_ref, qdec_ref, kdec_ref, cdec_ref, wg32_ref, wu32_ref, wd32_ref,
                         x1_ref, conv_ref, C_ref, n_ref, m_ref, S_ref, wg16_ref, wu16_ref, wd16_ref,
                         un2, ut2, gt2, tail_ref, st_ref, h_ref,
                         *, tt, tiles_per_seq):
    wg16_ref[...] = wg32_ref[...].astype(BF16)
    wu16_ref[...] = wu32_ref[...].astype(BF16)
    wd16_ref[...] = wd32_ref[...].astype(BF16)
    L = CHUNK
    s = pl.program_id(0)
    t = (jnp.maximum(s, 1) - 1) % tiles_per_seq
    chunks = range(tt // L)
    slot_of = lambda i: (un2.at[i], ut2.at[i], gt2.at[i])

    @pl.when(s == 0)
    def _():
        for ref in slot_of(1) + (tail_ref,):
            ref[...] = jnp.zeros_like(ref)

    @pl.when(t == 0)
    def _():
        C_ref[...] = jnp.zeros_like(C_ref)
        st_ref[...] = jnp.zeros_like(st_ref)
        n_ref[...] = jnp.zeros_like(n_ref)
        m_ref[...] = jnp.zeros_like(m_ref)

    k_scale = HEAD_DIM ** -0.5

    def project(slot, xn, part):
        un_ref, ut_ref, gt_ref = slot
        c0 = part * MIX
        un_ref[UOFF:UOFF + tt, c0:c0 + MIX] = jnp.dot(xn, wn_ref[:, c0:c0 + MIX],
                                                      preferred_element_type=F32)
        res = lax.dot_general(wt_ref[c0:c0 + MIX, :], xn, NT_DIMS, preferred_element_type=F32)
        for c in chunks:
            ut_ref[c, c0:c0 + MIX, :] = res[:, c * L:(c + 1) * L]
        if part == 0:
            gates_t = lax.dot_general(wgt_ref[...], xn, NT_DIMS, preferred_element_type=F32)
            for c in chunks:
                gt_ref[c] = gates_t[:, c * L:(c + 1) * L]

    src_id = lax.broadcasted_iota(jnp.int32, (L, L), 0)
    tgt_id = lax.broadcasted_iota(jnp.int32, (L, L), 1)
    causal = src_id <= tgt_id
    triu_bf = jnp.where(causal, 1.0, 0.0).astype(BF16)
    heads = range(HEADS)
    hcol = lambda base, h: slice(base + h * HEAD_DIM, base + (h + 1) * HEAD_DIM)

    def chunk_body(c, slot):
        un_ref, ut_ref, gt_ref = slot
        r0 = c * L
        rows = pl.ds(UOFF + r0, L)
        trows = pl.ds(r0, L)
        g_t = gt_ref[c] + bgt_ref[...]
        ig = g_t[0:8]
        lf = _log_sigmoid(g_t[8:16])
        lf_hi = lf.astype(BF16)
        r1 = lf - lf_hi.astype(F32)
        lf_mid = r1.astype(BF16)
        lf_lo = (r1 - lf_mid.astype(F32)).astype(BF16)
        bc = (jnp.dot(lf_hi, triu_bf, preferred_element_type=F32)
              + jnp.dot(lf_mid, triu_bf, preferred_element_type=F32)
              + jnp.dot(lf_lo, triu_bf, preferred_element_type=F32))
        m_prev = m_ref[...]
        inter = bc + m_prev
        b_last = bc[:, L - 1:L]
        g_w = b_last - bc + ig
        m_new = jnp.maximum(b_last + m_prev, jnp.max(g_w, axis=1, keepdims=True))
        ws = jnp.exp(g_w - m_new)
        carry = jnp.exp(b_last + m_prev - m_new)
        a_n = jnp.concatenate([ig - bc, jnp.zeros((L - 8, L), F32)], axis=0).T

        def conv_act(col):
            win = un_ref[pl.ds(r0, L + UOFF), col:col + HEAD_DIM]
            acc = bconv_ref[:, col:col + HEAD_DIM]
            for j in range(CONV_W):
                back = CONV_W - 1 - j
                tap = win if back == 0 else pltpu.roll(win, back, 0)
                acc = acc + tap[UOFF:UOFF + L] * wconv_ref[j:j + 1, col:col + HEAD_DIM]
            return _silu(acc)

        cos_f, sin_s = cos_ref[trows, :], sin_ref[trows, :]
        q = [conv_act(N_QK + h * HEAD_DIM) for h in heads]
        kb = [(conv_act(N_QK + MIX + h * HEAD_DIM) * k_scale).astype(BF16) for h in heads]
        qb = [a.astype(BF16) for a in q]
        rqb = [_rope(un_ref[rows, hcol(N_RQ, h)], cos_f, sin_s).astype(BF16) for h in heads]
        rk = [_rope(un_ref[rows, hcol(N_RK, h)], cos_f, sin_s) * k_scale for h in heads]
        v_t = [ut_ref[c, hcol(T_V, h), :] for h in heads]
        rvb = [ut_ref[c, hcol(T_RV, h), :].astype(BF16) for h in heads]
        c_old = [C_ref[h] for h in heads]
        s_old = [st_ref[h] for h in heads]
        n_old = [n_ref[h:h + 1, :] for h in heads]
        qk = [_dot_nt(kb[h], qb[h]) for h in heads]
        att = [_dot_nt(rk[h], rqb[h]) for h in heads]
        c_q = [_dot_nt(c_old[h], qb[h]) for h in heads]
        s_q = [_dot_nt(s_old[h], rqb[h]) for h in heads]
        n_q = [_dot_nt(jnp.broadcast_to(n_old[h], (8, HEAD_DIM)), qb[h])[0:1] for h in heads]
        d_c = [_dot(v_t[h] * ws[h:h + 1, :], kb[h]) for h in heads]
        d_n = [_dot(jnp.broadcast_to(ws[h:h + 1, :], (8, L)), kb[h])[0:1] for h in heads]
        d_s = [_dot(rvb[h], rk[h] * kdec_ref[h]) for h in heads]
        m_t, w_in, wts = [], [], []
        for h in heads:
            dmat = a_n[:, h:h + 1] + bc[h:h + 1, :]
            dmat = jnp.where(causal, dmat, -jnp.inf)
            m_t.append(jnp.maximum(inter[h:h + 1, :], jnp.max(dmat, axis=0, keepdims=True)))
            wts.append(jnp.exp(dmat - m_t[h]) * qk[h])
            w_in.append(jnp.exp(inter[h:h + 1, :] - m_t[h]))
        att_w = [att[h] * decay_ref[h] for h in heads]
        v_p = [_dot(v_t[h], wts[h]) for h in heads]
        v_a = [_dot(rvb[h], att_w[h]) for h in heads]
        for h in heads:
            num = v_p[h] + w_in[h] * c_q[h]
            den = jnp.sum(wts[h], axis=0, keepdims=True) + w_in[h] * n_q[h]
            hm = num / jnp.maximum(jnp.abs(den), jnp.exp(-m_t[h]))
            hm = hm * lax.rsqrt(jnp.mean(hm * hm, axis=0, keepdims=True) + EPS)
            hm = hm * gmh_ref[hcol(0, h), :] * jax.nn.sigmoid(ut_ref[c, hcol(T_O, h), :])
            h_ref[trows, hcol(0, h)] = hm.T.astype(BF16)
            carry_h = carry[h:h + 1, :]
            C_ref[h] = carry_h * c_old[h] + d_c[h]
            n_ref[h:h + 1, :] = carry_h * n_old[h] + d_n[h]
            o = v_a[h] + qdec_ref[h:h + 1, :] * s_q[h]
            st_ref[h] = cdec_ref[h:h + 1, :] * s_old[h] + d_s[h]
            hr = o * lax.rsqrt(jnp.mean(o * o, axis=0, keepdims=True) + EPS)
            hr = hr * grh_ref[hcol(0, h), :] * _silu(ut_ref[c, hcol(T_RG, h), :])
            h_ref[trows, hcol(MIX, h)] = hr.T.astype(BF16)

        m_ref[...] = m_new

    def step(proj_slot, scan_slot):
        un_ref = scan_slot[0]
        un_ref[0:UOFF, N_QK:N_RQ] = jnp.where(t == 0, 0.0, tail_ref[...])
        xn = _rms(x_ref[...], gmix_ref[...]).astype(BF16)
        for c in chunks:
            project(proj_slot, xn, c)
            chunk_body(c, scan_slot)
        tail_ref[...] = un_ref[tt:tt + UOFF, N_QK:N_RQ]
        x1_ref[...] = xs_ref[...] + jnp.dot(h_ref[...], wout_ref[...], preferred_element_type=F32)

    step(slot_of(s % 2), slot_of(1 - s % 2))

    @pl.when(t == tiles_per_seq - 1)
    def _():
        conv_ref[...] = tail_ref[UOFF - (CONV_W - 1):UOFF, :]
        for h in heads:
            S_ref[h] = st_ref[h].T


def _mixer_prompt(x, wn, wt, wgt, w_out, g_mix, bgt, w_conv, b_conv, gmh_cols, grh_cols, cos_f, sin_s,
                  decay_t, qdec_rows, kdec_cols, cdec_rows, w_gate, w_up, w_down, tt=512):
    B, T, _ = x.shape
    tps = T // tt
    n_tiles = B * tps
    proj = lambda s: jnp.minimum(s, n_tiles - 1)
    scan = lambda s: jnp.maximum(s, 1) - 1
    per_b3 = lambda s: (scan(s) // tps, 0, 0)
    per_b4 = lambda s: (scan(s) // tps, 0, 0, 0)
    scan_tile = pl.BlockSpec((None, tt, D_MODEL), lambda s: (scan(s) // tps, scan(s) % tps, 0))
    in_specs = [
        pl.BlockSpec((None, tt, D_MODEL), lambda s: (proj(s) // tps, proj(s) % tps, 0)),
        scan_tile,
        _const_spec((D_MODEL, N_COLS)), _const_spec((T_ROWS, D_MODEL)), _const_spec((GT_ROWS, D_MODEL)),
        _const_spec((D_MODEL, D_MODEL)),
        _const_spec((1, D_MODEL)), _const_spec((GT_ROWS, CHUNK)),
        _const_spec((CONV_W, 2 * MIX)), _const_spec((1, 2 * MIX)),
        _const_spec((MIX, CHUNK)), _const_spec((MIX, CHUNK)),
        pl.BlockSpec((tt, HEAD_DIM), lambda s: (scan(s) % tps, 0)),
        pl.BlockSpec((tt, HEAD_DIM), lambda s: (scan(s) % tps, 0)),
        _const_spec((HEADS, CHUNK, CHUNK)),
        _const_spec((8, CHUNK)), _const_spec((HEADS, CHUNK, HEAD_DIM)), _const_spec((8, HEAD_DIM)),
    ]
    ff_blocks = D_FF // HEAD_DIM
    assert ff_blocks <= n_tiles + 1
    early = lambda s: jnp.minimum(s, ff_blocks - 1)
    late = lambda s: jnp.maximum(s - (n_tiles + 1 - ff_blocks), 0)
    cast_specs = [pl.BlockSpec((D_MODEL, HEAD_DIM), lambda s: (0, early(s))),
                  pl.BlockSpec((D_MODEL, HEAD_DIM), lambda s: (0, late(s))),
                  pl.BlockSpec((HEAD_DIM, D_MODEL), lambda s: (late(s), 0))]
    in_specs += cast_specs
    out_specs = [
        scan_tile,
        pl.BlockSpec((None, CONV_W - 1, 2 * MIX), per_b3),
        pl.BlockSpec((None, HEADS, HEAD_DIM, HEAD_DIM), per_b4),
        pl.BlockSpec((None, HEADS, HEAD_DIM), per_b3),
        pl.BlockSpec((None, 8, CHUNK), per_b3),
        pl.BlockSpec((None, HEADS, HEAD_DIM, HEAD_DIM), per_b4),
    ] + cast_specs
    out_shape = [
        jax.ShapeDtypeStruct((B, T, D_MODEL), F32),
        jax.ShapeDtypeStruct((B, CONV_W - 1, 2 * MIX), F32),
        jax.ShapeDtypeStruct((B, HEADS, HEAD_DIM, HEAD_DIM), F32),
        jax.ShapeDtypeStruct((B, HEADS, HEAD_DIM), F32),
        jax.ShapeDtypeStruct((B, 8, CHUNK), F32),
        jax.ShapeDtypeStruct((B, HEADS, HEAD_DIM, HEAD_DIM), F32),
        jax.ShapeDtypeStruct((D_MODEL, D_FF), BF16), jax.ShapeDtypeStruct((D_MODEL, D_FF), BF16),
        jax.ShapeDtypeStruct((D_FF, D_MODEL), BF16),
    ]
    n_chunks = tt // CHUNK
    scratch = [
        pltpu.VMEM((2, tt + UOFF, N_COLS), F32),
        pltpu.VMEM((2, n_chunks, T_ROWS, CHUNK), F32),
        pltpu.VMEM((2, n_chunks, GT_ROWS, CHUNK), F32),
    ] + [
        pltpu.VMEM((UOFF, 2 * MIX), F32),
        pltpu.VMEM((HEADS, HEAD_DIM, HEAD_DIM), F32),
        pltpu.VMEM((tt, D_MODEL), BF16),
    ]
    return pl.pallas_call(
        functools.partial(_mixer_prompt_kernel, tt=tt, tiles_per_seq=tps),
        grid=(n_tiles + 1,),
        in_specs=in_specs, out_specs=out_specs, out_shape=out_shape,
        scratch_shapes=scratch,
        compiler_params=_params("arbitrary"),
    )(x, x, wn, wt, wgt, w_out, g_mix, bgt, w_conv, b_conv, gmh_cols, grh_cols, cos_f, sin_s,
      decay_t, qdec_rows, kdec_cols, cdec_rows, w_gate, w_up, w_down)


def _attn_prompt_kernel(x1_ref, k_ref, v_ref, wcq_ref, gx_ref, *rest, n_sample_steps):
    sample_in, o_ref = rest[:SAMPLE_MIXER_INPUTS], rest[SAMPLE_MIXER_INPUTS]
    sample_out_and_scratch = rest[SAMPLE_MIXER_INPUTS + 1:]
    step = pl.program_id(0) * pl.num_programs(1) + pl.program_id(1)

    @pl.when(step < n_sample_steps)
    def _():
        _mixer_sample_kernel(*sample_in, *sample_out_and_scratch, bb=SAMPLE_BLOCK)

    tm = x1_ref.shape[0]
    rows = [slice(r, r + tm // ROW_GROUPS) for r in range(0, tm, tm // ROW_GROUPS)]
    xq = [_rms(x1_ref[r, :], gx_ref[...]).astype(BF16) for r in rows]
    q = [jnp.dot(a, wcq_ref[...], preferred_element_type=F32).astype(BF16) for a in xq]
    sl = [slice(h * X_HEAD_DIM, (h + 1) * X_HEAD_DIM) for h in range(X_HEADS)]
    items = [(g, h) for g in range(ROW_GROUPS) for h in range(X_HEADS)]
    s = [_dot_nt(q[g][:, sl[h]], k_ref[:, sl[h]]) * (X_HEAD_DIM ** -0.5) for g, h in items]
    e = [jnp.exp(a - jnp.max(a, axis=-1, keepdims=True)) for a in s]
    p = [a / jnp.sum(a, axis=-1, keepdims=True) for a in e]
    for (g, h), a in zip(items, p):
        o_ref[rows[g], sl[h]] = _dot(a, v_ref[:, sl[h]]).astype(BF16)


def _attn_prompt(x1, mk, mv, w_cq, g_x, sample_mixer_operands, tm=512):
    B, T, _ = x1.shape
    tps = T // tm
    Bs = sample_mixer_operands[0].shape[0]
    n_sample_steps = Bs // SAMPLE_BLOCK
    assert n_sample_steps <= B * tps
    block_of = lambda b, t: jnp.minimum(b * tps + t, n_sample_steps - 1)
    s_in, s_out, s_shape, s_scratch = _sample_mixer_specs(Bs, block_of)
    tile = pl.BlockSpec((None, tm, D_MODEL), lambda b, t: (b, t, 0))
    kv = pl.BlockSpec((None, N_MEM, D_MODEL), lambda b, t: (b, 0, 0))
    outs = pl.pallas_call(
        functools.partial(_attn_prompt_kernel, n_sample_steps=n_sample_steps),
        grid=(B, tps),
        in_specs=[tile, kv, kv, _const_spec((D_MODEL, D_MODEL)), _const_spec((1, D_MODEL))] + s_in,
        out_specs=[tile] + s_out,
        out_shape=[jax.ShapeDtypeStruct((B, T, D_MODEL), BF16)] + s_shape,
        scratch_shapes=s_scratch,
        compiler_params=_params("arbitrary", "arbitrary"),
    )(x1, mk, mv, w_cq, g_x, *sample_mixer_operands)
    return outs[0], outs[1:]


V7X_MXU_DIM = 256
FF_CHUNKS = ((0, 6 * V7X_MXU_DIM), (6 * V7X_MXU_DIM, D_FF))


def _ffn_kernel(x1_ref, o_ref, wco_ref, wg_ref, wu_ref, wd_ref, gffn_ref, gfin_ref, *rest, n_groups):
    if len(rest) == 1:
        (y_ref,) = rest
        sample_stages = iter(())
    else:
        qs_ref, ck_ref, cv_ref, y_ref, os_ref = rest
        sample_stages = _sample_attention(qs_ref, ck_ref, cv_ref, os_ref)
    tm = x1_ref.shape[0]
    rows = [slice(r, r + tm // n_groups) for r in range(0, tm, tm // n_groups)]
    acc = [x1_ref[r, :] + _dot(o_ref[r, :], wco_ref[...]) for r in rows]
    hf = [_rms(a, gffn_ref[...]).astype(BF16) for a in acc]
    for c0, c1 in FF_CHUNKS:
        next(sample_stages, None)
        gate = [jnp.dot(h, wg_ref[:, c0:c1], preferred_element_type=F32) for h in hf]
        up = [jnp.dot(h, wu_ref[:, c0:c1], preferred_element_type=F32) for h in hf]
        act = [_silu(g) * u for g, u in zip(gate, up)]
        acc = [a + _dot(p, wd_ref[c0:c1, :]) for a, p in zip(acc, act)]
    next(sample_stages, None)
    for r, a in zip(rows, acc):
        y_ref[r, :] = _rms(a, gfin_ref[...])


def _ffn(x1, o, w_co, w_gate, w_up, w_down, g_ffn, g_final, tm, sample=None):
    rows = x1.shape[0]
    steps = rows // tm
    row_spec = pl.BlockSpec((tm, D_MODEL), lambda i: (i, 0))
    in_specs = [row_spec, row_spec, _const_spec((D_MODEL, D_MODEL)),
                _const_spec((D_MODEL, D_FF)), _const_spec((D_MODEL, D_FF)),
                _const_spec((D_FF, D_MODEL)), _const_spec((1, D_MODEL)), _const_spec((1, D_MODEL))]
    out_specs = [row_spec]
    out_shape = [jax.ShapeDtypeStruct((rows, D_MODEL), F32)]
    operands = [x1, o, w_co, w_gate, w_up, w_down, g_ffn, g_final]
    if sample is not None:
        q_s, ck_s, cv_s = sample
        Bs = q_s.shape[0]
        ba = Bs // steps
        assert ba * steps == Bs
        s_rows = pl.BlockSpec((ba, QROWS, V7X_LANES), lambda i: (i, 0, 0))
        s_kv = pl.BlockSpec((ba, KV_ROWS, V7X_LANES), lambda i: (i, 0, 0))
        in_specs += [s_rows, s_kv, s_kv]
        out_specs += [s_rows]
        out_shape += [jax.ShapeDtypeStruct((Bs, QROWS, V7X_LANES), F32)]
        operands += [q_s.reshape(Bs, QROWS, V7X_LANES), ck_s, cv_s]
    outs = pl.pallas_call(
        functools.partial(_ffn_kernel, n_groups=ROW_GROUPS if tm >= 512 else 1),
        grid=(steps,),
        in_specs=in_specs, out_specs=out_specs, out_shape=out_shape,
        compiler_params=_params("parallel"),
    )(*operands)
    if sample is None:
        return outs[0]
    return outs[0], outs[1].reshape(Bs, D_MODEL)


def _inproj_sample_kernel(x_ref, g_ref, wn_ref, wt_ref, wgt_ref, un_ref, ut_ref, gs_ref):
    xn = _rms(x_ref[...], g_ref[...]).astype(BF16)
    un_ref[...] = jnp.dot(xn, wn_ref[...], preferred_element_type=F32)
    ut_ref[...] = lax.dot_general(xn, wt_ref[...], NT_DIMS, preferred_element_type=F32)
    gs_ref[...] = lax.dot_general(xn, wgt_ref[...], NT_DIMS, preferred_element_type=F32)


def _inproj_sample(x, g_mix, wn, wt, wgt):
    rows = x.shape[0]
    full = lambda cols: pl.BlockSpec((rows, cols), lambda i: (0, 0))
    return pl.pallas_call(
        _inproj_sample_kernel,
        grid=(1,),
        in_specs=[_const_spec((rows, D_MODEL)), _const_spec((1, D_MODEL)),
                  _const_spec((D_MODEL, N_COLS)), _const_spec((T_ROWS, D_MODEL)),
                  _const_spec((GT_ROWS, D_MODEL))],
        out_specs=[full(N_COLS), full(T_ROWS), full(GT_ROWS)],
        out_shape=[jax.ShapeDtypeStruct((rows, N_COLS), F32),
                   jax.ShapeDtypeStruct((rows, T_ROWS), F32),
                   jax.ShapeDtypeStruct((rows, GT_ROWS), F32)],
        compiler_params=_params("arbitrary"),
    )(x, g_mix, wn, wt, wgt)


def _mixer_sample_kernel(un_ref, ut_ref, gs_ref, conv_ref, C_ref, n_ref, m_ref, S_ref, bg_ref,
                         wconv_ref, bconv_ref, gmh_ref, grh_ref, cos_ref, sin_ref, rtab_ref,
                         h_ref, convo_ref, Co_ref, no_ref, mo_ref, So_ref,
                         q_s, k_s, vws_s, qr_s, kr_s, qc_s, qs_s, carry_s, *, bb):
    k_scale = HEAD_DIM ** -0.5
    uqk = un_ref[:, N_QK:N_RQ]
    conv = (bconv_ref[...] + wconv_ref[0:1, :] * conv_ref[0] + wconv_ref[1:2, :] * conv_ref[1]
            + wconv_ref[2:3, :] * conv_ref[2] + wconv_ref[3:4, :] * uqk)
    convo_ref[0] = conv_ref[1]
    convo_ref[1] = conv_ref[2]
    convo_ref[2] = uqk
    qk_act = _silu(conv)
    q_s[...] = qk_act[:, 0:MIX]
    k_s[...] = qk_act[:, MIX:2 * MIX] * k_scale

    gates = gs_ref[...] + bg_ref[...]
    ig = gates[:, 0:HEADS]
    lf = _log_sigmoid(gates[:, 8:8 + HEADS])
    inter = lf + m_ref[...]
    m_t = jnp.maximum(inter, ig)
    ws = jnp.exp(ig - m_t)
    w_in = jnp.exp(inter - m_t)
    mo_ref[...] = m_t
    carry_s[:, 0:HEADS] = w_in

    cos_f = cos_ref[...]
    sin_s = sin_ref[...]
    for h in range(HEADS):
        lo = h * HEAD_DIM
        hs = slice(lo, lo + HEAD_DIM)
        vws_s[:, hs] = ut_ref[:, T_V + lo:T_V + lo + HEAD_DIM] * ws[:, h:h + 1]
        qr_s[:, hs] = _rope(un_ref[:, N_RQ + lo:N_RQ + lo + HEAD_DIM], cos_f, sin_s)
        kr_s[:, hs] = _rope(un_ref[:, N_RK + lo:N_RK + lo + HEAD_DIM], cos_f, sin_s) * k_scale

    heads = range(HEADS)
    seqs = range(bb)
    hsl = [slice(h * HEAD_DIM, (h + 1) * HEAD_DIM) for h in heads]
    seq_id = lax.broadcasted_iota(jnp.int32, (bb, HEAD_DIM), 0)
    for h in heads:
        q_h, k_h, vws_h = q_s[:, hsl[h]], k_s[:, hsl[h]], vws_s[:, hsl[h]]
        qr_h = qr_s[:, hsl[h]]
        krd_h = kr_s[:, hsl[h]] * rtab_ref[1:2, h:h + 1]
        rv_h = ut_ref[:, T_RV + h * HEAD_DIM:T_RV + (h + 1) * HEAD_DIM]
        c_old = [C_ref[b, h] for b in seqs]
        s_old = [S_ref[b, h] for b in seqs]
        q_c = [_dot_nt(q_h, c_old[b]) for b in seqs]
        q_st = [_dot(qr_h, s_old[b]) for b in seqs]
        d_c = [_dot_tn(jnp.where(seq_id == b, vws_h, 0.0), k_h) for b in seqs]
        d_s = [_dot_tn(jnp.where(seq_id == b, krd_h, 0.0), rv_h) for b in seqs]
        for b in seqs:
            row = slice(b, b + 1)
            qc_s[row, hsl[h]] = q_c[b][row]
            qs_s[row, hsl[h]] = q_st[b][row]
            Co_ref[b, h] = carry_s[row, h:h + 1] * c_old[b] + d_c[b]
            So_ref[b, h] = rtab_ref[2:3, h:h + 1] * s_old[b] + d_s[b]

    for h in range(HEADS):
        lo = h * HEAD_DIM
        hs = slice(lo, lo + HEAD_DIM)
        q_h, k_h, n_h = q_s[:, hs], k_s[:, hs], n_ref[:, hs]
        ws_h, w_in_h = ws[:, h:h + 1], w_in[:, h:h + 1]
        wts = ws_h * jnp.sum(q_h * k_h, axis=1, keepdims=True)
        num = wts * ut_ref[:, T_V + lo:T_V + lo + HEAD_DIM] + w_in_h * qc_s[:, hs]
        den = wts + w_in_h * jnp.sum(n_h * q_h, axis=1, keepdims=True)
        hm = num / jnp.maximum(jnp.abs(den), jnp.exp(-m_t[:, h:h + 1]))
        mo = ut_ref[:, T_O + lo:T_O + lo + HEAD_DIM]
        h_ref[:, hs] = _head_norm(hm) * gmh_ref[:, hs] * jax.nn.sigmoid(mo)
        no_ref[:, hs] = w_in_h * n_h + ws_h * k_h

        att = jnp.sum(qr_s[:, hs] * kr_s[:, hs], axis=1, keepdims=True) * rtab_ref[3:4, h:h + 1]
        o = (att * ut_ref[:, T_RV + lo:T_RV + lo + HEAD_DIM]
             + rtab_ref[0:1, h:h + 1] * qs_s[:, hs])
        rg = ut_ref[:, T_RG + lo:T_RG + lo + HEAD_DIM]
        h_ref[:, MIX + lo:MIX + lo + HEAD_DIM] = _head_norm(o) * grh_ref[:, hs] * _silu(rg)


SAMPLE_BLOCK = 8
SAMPLE_MIXER_INPUTS = 16


def _sample_mixer_specs(B, block_of):
    bb = SAMPLE_BLOCK
    rows = lambda cols: pl.BlockSpec((bb, cols), lambda *g: (block_of(*g), 0))
    mats = pl.BlockSpec((bb, HEADS, HEAD_DIM, HEAD_DIM), lambda *g: (block_of(*g), 0, 0, 0))
    convs = pl.BlockSpec((CONV_W - 1, bb, 2 * MIX), lambda *g: (0, block_of(*g), 0))
    in_specs = [rows(N_COLS), rows(T_ROWS), rows(GT_ROWS), convs, mats, rows(MIX), rows(HEADS), mats,
                _const_spec((1, GT_ROWS)), _const_spec((CONV_W, 2 * MIX)), _const_spec((1, 2 * MIX)),
                _const_spec((1, MIX)), _const_spec((1, MIX)),
                _const_spec((1, HEAD_DIM)), _const_spec((1, HEAD_DIM)), _const_spec((8, HEAD_DIM))]
    assert len(in_specs) == SAMPLE_MIXER_INPUTS
    out_specs = [rows(D_MODEL), convs, mats, rows(MIX), rows(HEADS), mats]
    out_shape = [jax.ShapeDtypeStruct((B, D_MODEL), F32),
                 jax.ShapeDtypeStruct((CONV_W - 1, B, 2 * MIX), F32),
                 jax.ShapeDtypeStruct((B, HEADS, HEAD_DIM, HEAD_DIM), F32),
                 jax.ShapeDtypeStruct((B, MIX), F32),
                 jax.ShapeDtypeStruct((B, HEADS), F32),
                 jax.ShapeDtypeStruct((B, HEADS, HEAD_DIM, HEAD_DIM), F32)]
    scratch = [pltpu.VMEM((bb, MIX), F32) for _ in range(7)] + [pltpu.VMEM((bb, HEAD_DIM), F32)]
    return in_specs, out_specs, out_shape, scratch


def _outq_sample_kernel(x_ref, h_ref, wout_ref, wcq_ref, gx_ref, x1_ref, q_ref):
    x1 = x_ref[...] + _dot(h_ref[...], wout_ref[...])
    x1_ref[...] = x1
    q_ref[...] = _dot(_rms(x1, gx_ref[...]), wcq_ref[...])


def _outq_sample(x, hcat, w_out, w_cq, g_x):
    rows = x.shape[0]
    full = _const_spec((rows, D_MODEL))
    return pl.pallas_call(
        _outq_sample_kernel,
        grid=(1,),
        in_specs=[full, full, _const_spec((D_MODEL, D_MODEL)), _const_spec((D_MODEL, D_MODEL)),
                  _const_spec((1, D_MODEL))],
        out_specs=[pl.BlockSpec((rows, D_MODEL), lambda i: (0, 0))] * 2,
        out_shape=[jax.ShapeDtypeStruct((rows, D_MODEL), F32)] * 2,
        compiler_params=_params("arbitrary"),
    )(x, hcat, w_out, w_cq, g_x)


def _sample_attention(q_ref, k_ref, v_ref, o_ref):
    assert X_HALVES == 2
    r_id = lax.broadcasted_iota(jnp.int32, (QROWS, KV_ROWS), 0)
    n_id = lax.broadcasted_iota(jnp.int32, (QROWS, KV_ROWS), 1)
    own = (n_id % QROWS) == (r_id // X_HALVES) + X_HEADS * (r_id % X_HALVES)
    low_half = (lax.broadcasted_iota(jnp.int32, (1, KV_ROWS), 1) % QROWS) < X_HEADS
    seqs = range(q_ref.shape[0])
    z = [_dot_nt(q_ref[j], k_ref[j]) for j in seqs]
    zc = [jnp.sum(jnp.where(own, z[j], 0.0), axis=0, keepdims=True) for j in seqs]
    yield
    other = [jnp.where(low_half, pltpu.roll(zc[j], KV_ROWS - X_HEADS, 1),
                       pltpu.roll(zc[j], X_HEADS, 1)) for j in seqs]
    s = [jnp.where(own, (zc[j] + other[j]) * (X_HEAD_DIM ** -0.5), -jnp.inf) for j in seqs]
    m = [jnp.max(s[j], axis=-1, keepdims=True) for j in seqs]
    e = [jnp.exp(s[j] - m[j]) for j in seqs]
    den = [jnp.sum(e[j], axis=-1, keepdims=True) for j in seqs]
    yield
    for j in seqs:
        o_ref[j] = _dot(e[j] / den[j], v_ref[j])
    yield


def _kv_rows(cache):
    B = cache.shape[0]
    c5 = cache.reshape(B, N_MEM, X_HEADS, X_HALVES, V7X_LANES)
    return jnp.transpose(c5, (0, 1, 3, 2, 4)).reshape(B, KV_ROWS, V7X_LANES)


def _kv_from_rows(rows):
    B = rows.shape[0]
    r5 = rows.reshape(B, N_MEM, X_HALVES, X_HEADS, V7X_LANES)
    return jnp.transpose(r5, (0, 1, 3, 2, 4)).reshape(1, B, N_MEM, X_HEADS, X_HEAD_DIM)


def _rope_tables(pos):
    half = HEAD_DIM // 2
    inv = ROPE_THETA ** (-np.arange(half, dtype=np.float64) / half)
    ang = np.asarray(pos, np.float64)[:, None] * inv[None, :]
    cos, sin = np.cos(ang), np.sin(ang)
    return (np.concatenate([cos, cos], axis=-1).astype(np.float32),
            np.concatenate([-sin, sin], axis=-1).astype(np.float32))


def _retention_tables(L):
    lg = np.log1p(-np.exp2(-5.0 - np.arange(HEADS, dtype=np.float64)))
    t = np.arange(L, dtype=np.float64)
    diff = t[:, None] - t[None, :]
    decay = np.where(diff >= 0, np.exp(lg[:, None, None] * np.maximum(diff, 0.0)), 0.0)
    q_dec = np.exp(lg[:, None] * (t + 1.0))
    k_dec = np.exp(lg[:, None] * (L - 1.0 - t))
    chunk_dec = np.exp(lg * L)
    return tuple(a.astype(np.float32) for a in (decay, q_dec, k_dec, chunk_dec))


def _lanes(a, n):
    xp = np if isinstance(a, np.ndarray) else jnp
    return xp.broadcast_to(a[..., None], a.shape + (n,))


def _pad_rows(a, rows):
    return np.pad(a, ((0, rows - a.shape[0]),) + ((0, 0),) * (a.ndim - 1))


def kernel(x_prompt, x_sample, cache_mem_k, cache_mem_v, state_mlstm_conv, state_mlstm_C, state_mlstm_n, state_mlstm_m, state_ret_S, mem_prompt, w_in, b_gate, w_conv, b_conv, g_mix, g_mhead, g_rhead, w_out, g_xattn, g_mem, w_ck, w_cv, w_cq, w_co, g_ffn, w_gate, w_up, w_down, g_final):
    Bp, Tp, _ = x_prompt.shape
    Bs = x_sample.shape[0]
    l = 0
    n_m = 4 * MIX
    wi = w_in[l]
    w_gates = wi[:, n_m:n_m + 2 * HEADS]
    w_ret = wi[:, n_m + 2 * HEADS:]
    row = lambda a: a.reshape(1, -1)
    bf = lambda a: a.astype(BF16)
    g_mix_r, g_mh_r, g_rh_r = row(g_mix[l]), row(g_mhead[l]), row(g_rhead[l])
    g_x_r, g_mem_r, g_ffn_r, g_fin_r = row(g_xattn[l]), row(g_mem[l]), row(g_ffn[l]), row(g_final)
    b_conv_r = row(b_conv[l])
    w_out_b, w_cq_b, w_co_b = bf(w_out[l]), bf(w_cq[l]), bf(w_co[l])

    mk_b, mv_b, mk_rows, mv_rows = _memkv(mem_prompt, g_mem_r, bf(w_ck[l]), bf(w_cv[l]))
    wn = bf(jnp.concatenate([wi[:, :2 * MIX], w_ret[:, :2 * MIX]], axis=1))
    wt = bf(jnp.concatenate([wi[:, 2 * MIX:n_m], w_ret[:, 2 * MIX:]], axis=1).T)
    gate_rows = lambda a: jnp.concatenate(
        [a[:HEADS], jnp.zeros((8 - HEADS,) + a.shape[1:], F32),
         a[HEADS:], jnp.zeros((8 - HEADS,) + a.shape[1:], F32)], axis=0)
    wgt = bf(gate_rows(w_gates.T))
    bg_rows = gate_rows(b_gate[l][:, None])
    bgt = jnp.broadcast_to(bg_rows, (GT_ROWS, CHUNK))
    cos_p, sin_p = _rope_tables(np.arange(Tp))
    decay, q_dec, k_dec, chunk_dec = _retention_tables(CHUNK)
    x1_p, conv_p, C_p, n_p, m_p, S_p, w_gate_b, w_up_b, w_down_b = _mixer_prompt(
        x_prompt, wn, wt, wgt, w_out_b, g_mix_r, bgt, w_conv[l], b_conv_r,
        _lanes(g_mhead[l], CHUNK), _lanes(g_rhead[l], CHUNK), cos_p, sin_p,
        np.swapaxes(decay, 1, 2), _pad_rows(q_dec, 8), _lanes(k_dec, HEAD_DIM),
        _pad_rows(_lanes(chunk_dec, HEAD_DIM), 8), w_gate[l], w_up[l], w_down[l])

    xs = x_sample.reshape(Bs, D_MODEL)
    un_s, ut_s, gs_s = _inproj_sample(xs, g_mix_r, wn, wt, wgt)
    cos_s, sin_s = _rope_tables(PAST_LEN + np.arange(1))
    decay1, q_dec1, k_dec1, chunk_dec1 = _retention_tables(1)
    rtab = np.zeros((8, HEAD_DIM), np.float32)
    rtab[:4, :HEADS] = np.stack([q_dec1[:, 0], k_dec1[:, 0], chunk_dec1, decay1[:, 0, 0]])
    sample_mixer_operands = (
        un_s, ut_s, gs_s, jnp.transpose(state_mlstm_conv[l], (1, 0, 2)), state_mlstm_C[l],
        state_mlstm_n[l].reshape(Bs, MIX), state_mlstm_m[l], state_ret_S[l],
        bg_rows.reshape(1, GT_ROWS), w_conv[l], b_conv_r, g_mh_r, g_rh_r, cos_s, sin_s, rtab)

    o_p, (hcat_s, conv_s, C_s, n_s, m_s, S_s) = _attn_prompt(
        x1_p, mk_b, mv_b, w_cq_b, g_x_r, sample_mixer_operands)
    x1_s, q_s = _outq_sample(xs, hcat_s, w_out_b, w_cq_b, g_x_r)
    y_p, o_s = _ffn(x1_p.reshape(Bp * Tp, D_MODEL), o_p.reshape(Bp * Tp, D_MODEL),
                    w_co_b, w_gate_b, w_up_b, w_down_b, g_ffn_r, g_fin_r, tm=512,
                    sample=(q_s, _kv_rows(cache_mem_k[l]), _kv_rows(cache_mem_v[l])))
    y_s = _ffn(x1_s, o_s, w_co_b, w_gate_b, w_up_b, w_down_b, g_ffn_r, g_fin_r, tm=Bs)

    return (y_p.reshape(Bp, Tp, D_MODEL), y_s.reshape(Bs, 1, D_MODEL),
            _kv_from_rows(mk_rows), _kv_from_rows(mv_rows),
            conv_p[None], C_p[None], n_p[None], m_p[None, :, :HEADS, 0], S_p[None],
            jnp.transpose(conv_s, (1, 0, 2))[None], C_s[None],
            n_s.reshape(1, Bs, HEADS, HEAD_DIM), m_s[None], S_s[None])
```

```python
import functools

import jax
import jax.numpy as jnp
import numpy as np
from jax import lax
from jax.experimental import pallas as pl
from jax.experimental.pallas import tpu as pltpu

F32 = jnp.float32
BF16 = jnp.bfloat16

D_MODEL = 1024
HEADS = 4
HEAD_DIM = 128
MIX = HEADS * HEAD_DIM
CONV_W = 4
CHUNK = 128
N_MEM = 256
X_HEADS = 4
X_HEAD_DIM = 256
D_FF = 2816
ROPE_THETA = 10000.0
EPS = 1e-6
PAST_LEN = 16384
V7X_LANES = 128
X_HALVES = X_HEAD_DIM // V7X_LANES
KV_ROWS = N_MEM * X_HALVES * X_HEADS
QROWS = X_HALVES * X_HEADS

N_QK, N_RQ, N_RK, N_COLS = 0, 1024, 1536, 2048
T_V, T_O, T_RV, T_RG, T_ROWS = 0, 512, 1024, 1536, 2048
GT_ROWS = 16
UOFF = 8
ROW_GROUPS = 2

V7X_VMEM_LIMIT = 56 * 1024 * 1024

NT_DIMS = (((1,), (1,)), ((), ()))
TN_DIMS = (((0,), (0,)), ((), ()))


def _dot(a, b):
    return jnp.dot(a.astype(BF16), b.astype(BF16), preferred_element_type=F32)


def _dot_nt(a, b):
    return lax.dot_general(a.astype(BF16), b.astype(BF16), NT_DIMS, preferred_element_type=F32)


def _dot_tn(a, b):
    return lax.dot_general(a.astype(BF16), b.astype(BF16), TN_DIMS, preferred_element_type=F32)


def _rms(x, g):
    return x * lax.rsqrt(jnp.mean(x * x, axis=-1, keepdims=True) + EPS) * g


def _head_norm(h):
    return h * lax.rsqrt(jnp.mean(h * h, axis=-1, keepdims=True) + EPS)


def _silu(x):
    return x * jax.nn.sigmoid(x)


def _log_sigmoid(x):
    return jnp.minimum(x, 0.0) - jnp.log1p(jnp.exp(-jnp.abs(x)))


def _rope(x, cos_full, sin_signed):
    return x * cos_full + pltpu.roll(x, HEAD_DIM // 2, 1) * sin_signed


def _const_spec(shape):
    zeros = (0,) * len(shape)
    return pl.BlockSpec(shape, lambda *_: zeros, pipeline_mode=pl.Buffered(1))


def _params(*sem):
    return pltpu.CompilerParams(dimension_semantics=sem, vmem_limit_bytes=V7X_VMEM_LIMIT)


def _memkv_kernel(mem_ref, g_ref, wk_ref, wv_ref, kb_ref, vb_ref, krows_ref, vrows_ref):
    mn = _rms(mem_ref[...], g_ref[...]).astype(BF16)
    for w_ref, b_ref, rows_ref in ((wk_ref, kb_ref, krows_ref), (wv_ref, vb_ref, vrows_ref)):
        proj = jnp.dot(mn, w_ref[...], preferred_element_type=F32)
        b_ref[...] = proj.astype(BF16)
        for h in range(X_HEADS):
            for c in range(X_HALVES):
                lane0 = h * X_HEAD_DIM + c * V7X_LANES
                rows_ref[pl.ds(c * X_HEADS + h, N_MEM, stride=QROWS), :] = proj[:, lane0:lane0 + V7X_LANES]


def _memkv(mem, g_mem, wk, wv):
    B = mem.shape[0]
    tok = pl.BlockSpec((None, N_MEM, D_MODEL), lambda b: (b, 0, 0))
    rows = pl.BlockSpec((None, KV_ROWS, V7X_LANES), lambda b: (b, 0, 0))
    return pl.pallas_call(
        _memkv_kernel,
        grid=(B,),
        in_specs=[tok, _const_spec((1, D_MODEL)),
                  _const_spec((D_MODEL, D_MODEL)), _const_spec((D_MODEL, D_MODEL))],
        out_specs=[tok, tok, rows, rows],
        out_shape=[jax.ShapeDtypeStruct((B, N_MEM, D_MODEL), BF16)] * 2
        + [jax.ShapeDtypeStruct((B, KV_ROWS, V7X_LANES), F32)] * 2,
        compiler_params=_params("parallel"),
    )(mem, g_mem, wk, wv)


def _mixer_prompt_kernel(x_ref, xs_ref, wn_ref, wt_ref, wgt_ref, wout_ref, gmix_ref, bgt_ref,
                         wconv_ref, bconv_ref, gmh_ref, grh_ref, cos_ref, sin_ref,
                         decay_ref, qdec_ref, kdec_ref, cdec_ref, wg32_ref, wu32_ref, wd32_ref,
                         x1_ref, conv_ref, C_ref, n_ref, m_ref, S_ref, wg16_ref, wu16_ref, wd16_ref,
                         un2, ut2, gt2, tail_ref, st_ref, h_ref,
                         *, tt, tiles_per_seq):
    wg16_ref[...] = wg32_ref[...].astype(BF16)
    wu16_ref[...] = wu32_ref[...].astype(BF16)
    wd16_ref[...] = wd32_ref[...].astype(BF16)
    L = CHUNK
    s = pl.program_id(0)
    t = (jnp.maximum(s, 1) - 1) % tiles_per_seq
    chunks = range(tt // L)
    slot_of = lambda i: (un2.at[i], ut2.at[i], gt2.at[i])

    @pl.when(s == 0)
    def _():
        for ref in slot_of(1) + (tail_ref,):
            ref[...] = jnp.zeros_like(ref)

    @pl.when(t == 0)
    def _():
        C_ref[...] = jnp.zeros_like(C_ref)
        st_ref[...] = jnp.zeros_like(st_ref)
        n_ref[...] = jnp.zeros_like(n_ref)
        m_ref[...] = jnp.zeros_like(m_ref)

    k_scale = HEAD_DIM ** -0.5

    def project(slot, xn, part):
        un_ref, ut_ref, gt_ref = slot
        c0 = part * MIX
        un_ref[UOFF:UOFF + tt, c0:c0 + MIX] = jnp.dot(xn, wn_ref[:, c0:c0 + MIX],
                                                      preferred_element_type=F32)
        res = lax.dot_general(wt_ref[c0:c0 + MIX, :], xn, NT_DIMS, preferred_element_type=F32)
        for c in chunks:
            ut_ref[c, c0:c0 + MIX, :] = res[:, c * L:(c + 1) * L]
        if part == 0:
            gates_t = lax.dot_general(wgt_ref[...], xn, NT_DIMS, preferred_element_type=F32)
            for c in chunks:
                gt_ref[c] = gates_t[:, c * L:(c + 1) * L]

    src_id = lax.broadcasted_iota(jnp.int32, (L, L), 0)
    tgt_id = lax.broadcasted_iota(jnp.int32, (L, L), 1)
    causal = src_id <= tgt_id
    triu_bf = jnp.where(causal, 1.0, 0.0).astype(BF16)
    heads = range(HEADS)
    hcol = lambda base, h: slice(base + h * HEAD_DIM, base + (h + 1) * HEAD_DIM)

    def chunk_body(c, slot):
        un_ref, ut_ref, gt_ref = slot
        r0 = c * L
        rows = pl.ds(UOFF + r0, L)
        trows = pl.ds(r0, L)
        g_t = gt_ref[c] + bgt_ref[...]
        ig = g_t[0:8]
        lf = _log_sigmoid(g_t[8:16])
        lf_hi = lf.astype(BF16)
        r1 = lf - lf_hi.astype(F32)
        lf_mid = r1.astype(BF16)
        lf_lo = (r1 - lf_mid.astype(F32)).astype(BF16)
        bc = (jnp.dot(lf_hi, triu_bf, preferred_element_type=F32)
              + jnp.dot(lf_mid, triu_bf, preferred_element_type=F32)
              + jnp.dot(lf_lo, triu_bf, preferred_element_type=F32))
        m_prev = m_ref[...]
        inter = bc + m_prev
        b_last = bc[:, L - 1:L]
        g_w = b_last - bc + ig
        m_new = jnp.maximum(b_last + m_prev, jnp.max(g_w, axis=1, keepdims=True))
        ws = jnp.exp(g_w - m_new)
        carry = jnp.exp(b_last + m_prev - m_new)
        a_n = jnp.concatenate([ig - bc, jnp.zeros((L - 8, L), F32)], axis=0).T

        def conv_act(col):
            win = un_ref[pl.ds(r0, L + UOFF), col:col + HEAD_DIM]
            acc = bconv_ref[:, col:col + HEAD_DIM]
            for j in range(CONV_W):
                back = CONV_W - 1 - j
                tap = win if back == 0 else pltpu.roll(win, back, 0)
                acc = acc + tap[UOFF:UOFF + L] * wconv_ref[j:j + 1, col:col + HEAD_DIM]
            return _silu(acc)

        cos_f, sin_s = cos_ref[trows, :], sin_ref[trows, :]
        q = [conv_act(N_QK + h * HEAD_DIM) for h in heads]
        kb = [(conv_act(N_QK + MIX + h * HEAD_DIM) * k_scale).astype(BF16) for h in heads]
        qb = [a.astype(BF16) for a in q]
        rqb = [_rope(un_ref[rows, hcol(N_RQ, h)], cos_f, sin_s).astype(BF16) for h in heads]
        rk = [_rope(un_ref[rows, hcol(N_RK, h)], cos_f, sin_s) * k_scale for h in heads]
        v_t = [ut_ref[c, hcol(T_V, h), :] for h in heads]
        rvb = [ut_ref[c, hcol(T_RV, h), :].astype(BF16) for h in heads]
        c_old = [C_ref[h] for h in heads]
        s_old = [st_ref[h] for h in heads]
        n_old = [n_ref[h:h + 1, :] for h in heads]
        qk = [_dot_nt(kb[h], qb[h]) for h in heads]
        att = [_dot_nt(rk[h], rqb[h]) for h in heads]
        c_q = [_dot_nt(c_old[h], qb[h]) for h in heads]
        s_q = [_dot_nt(s_old[h], rqb[h]) for h in heads]
        n_q = [_dot_nt(jnp.broadcast_to(n_old[h], (8, HEAD_DIM)), qb[h])[0:1] for h in heads]
        d_c = [_dot(v_t[h] * ws[h:h + 1, :], kb[h]) for h in heads]
        d_n = [_dot(jnp.broadcast_to(ws[h:h + 1, :], (8, L)), kb[h])[0:1] for h in heads]
        d_s = [_dot(rvb[h], rk[h] * kdec_ref[h]) for h in heads]
        m_t, w_in, wts = [], [], []
        for h in heads:
            dmat = a_n[:, h:h + 1] + bc[h:h + 1, :]
            dmat = jnp.where(causal, dmat, -jnp.inf)
            m_t.append(jnp.maximum(inter[h:h + 1, :], jnp.max(dmat, axis=0, keepdims=True)))
            wts.append(jnp.exp(dmat - m_t[h]) * qk[h])
            w_in.append(jnp.exp(inter[h:h + 1, :] - m_t[h]))
        att_w = [att[h] * decay_ref[h] for h in heads]
        v_p = [_dot(v_t[h], wts[h]) for h in heads]
        v_a = [_dot(rvb[h], att_w[h]) for h in heads]
        for h in heads:
            num = v_p[h] + w_in[h] * c_q[h]
            den = jnp.sum(wts[h], axis=0, keepdims=True) + w_in[h] * n_q[h]
            hm = num / jnp.maximum(jnp.abs(den), jnp.exp(-m_t[h]))
            hm = hm * lax.rsqrt(jnp.mean(hm * hm, axis=0, keepdims=True) + EPS)
            hm = hm * gmh_ref[hcol(0, h), :] * jax.nn.sigmoid(ut_ref[c, hcol(T_O, h), :])
            h_ref[trows, hcol(0, h)] = hm.T.astype(BF16)
            carry_h = carry[h:h + 1, :]
            C_ref[h] = carry_h * c_old[h] + d_c[h]
            n_ref[h:h + 1, :] = carry_h * n_old[h] + d_n[h]
            o = v_a[h] + qdec_ref[h:h + 1, :] * s_q[h]
            st_ref[h] = cdec_ref[h:h + 1, :] * s_old[h] + d_s[h]
            hr = o * lax.rsqrt(jnp.mean(o * o, axis=0, keepdims=True) + EPS)
            hr = hr * grh_ref[hcol(0, h), :] * _silu(ut_ref[c, hcol(T_RG, h), :])
            h_ref[trows, hcol(MIX, h)] = hr.T.astype(BF16)

        m_ref[...] = m_new

    def step(proj_slot, scan_slot):
        un_ref = scan_slot[0]
        un_ref[0:UOFF, N_QK:N_RQ] = jnp.where(t == 0, 0.0, tail_ref[...])
        xn = _rms(x_ref[...], gmix_ref[...]).astype(BF16)
        for c in chunks:
            chunk_body(c, scan_slot)
            project(proj_slot, xn, c)
        tail_ref[...] = un_ref[tt:tt + UOFF, N_QK:N_RQ]
        x1_ref[...] = xs_ref[...] + jnp.dot(h_ref[...], wout_ref[...], preferred_element_type=F32)

    step(slot_of(s % 2), slot_of(1 - s % 2))

    @pl.when(t == tiles_per_seq - 1)
    def _():
        conv_ref[...] = tail_ref[UOFF - (CONV_W - 1):UOFF, :]
        for h in heads:
            S_ref[h] = st_ref[h].T


def _mixer_prompt(x, wn, wt, wgt, w_out, g_mix, bgt, w_conv, b_conv, gmh_cols, grh_cols, cos_f, sin_s,
                  decay_t, qdec_rows, kdec_cols, cdec_rows, w_gate, w_up, w_down, tt=512):
    B, T, _ = x.shape
    tps = T // tt
    n_tiles = B * tps
    proj = lambda s: jnp.minimum(s, n_tiles - 1)
    scan = lambda s: jnp.maximum(s, 1) - 1
    per_b3 = lambda s: (scan(s) // tps, 0, 0)
    per_b4 = lambda s: (scan(s) // tps, 0, 0, 0)
    scan_tile = pl.BlockSpec((None, tt, D_MODEL), lambda s: (scan(s) // tps, scan(s) % tps, 0))
    in_specs = [
        pl.BlockSpec((None, tt, D_MODEL), lambda s: (proj(s) // tps, proj(s) % tps, 0)),
        scan_tile,
        _const_spec((D_MODEL, N_COLS)), _const_spec((T_ROWS, D_MODEL)), _const_spec((GT_ROWS, D_MODEL)),
        _const_spec((D_MODEL, D_MODEL)),
        _const_spec((1, D_MODEL)), _const_spec((GT_ROWS, CHUNK)),
        _const_spec((CONV_W, 2 * MIX)), _const_spec((1, 2 * MIX)),
        _const_spec((MIX, CHUNK)), _const_spec((MIX, CHUNK)),
        pl.BlockSpec((tt, HEAD_DIM), lambda s: (scan(s) % tps, 0)),
        pl.BlockSpec((tt, HEAD_DIM), lambda s: (scan(s) % tps, 0)),
        _const_spec((HEADS, CHUNK, CHUNK)),
        _const_spec((8, CHUNK)), _const_spec((HEADS, CHUNK, HEAD_DIM)), _const_spec((8, HEAD_DIM)),
    ]
    ff_blocks = D_FF // HEAD_DIM
    assert ff_blocks <= n_tiles + 1
    early = lambda s: jnp.minimum(s, ff_blocks - 1)
    late = lambda s: jnp.maximum(s - (n_tiles + 1 - ff_blocks), 0)
    cast_specs = [pl.BlockSpec((D_MODEL, HEAD_DIM), lambda s: (0, early(s))),
                  pl.BlockSpec((D_MODEL, HEAD_DIM), lambda s: (0, late(s))),
                  pl.BlockSpec((HEAD_DIM, D_MODEL), lambda s: (late(s), 0))]
    in_specs += cast_specs
    out_specs = [
        scan_tile,
        pl.BlockSpec((None, CONV_W - 1, 2 * MIX), per_b3),
        pl.BlockSpec((None, HEADS, HEAD_DIM, HEAD_DIM), per_b4),
        pl.BlockSpec((None, HEADS, HEAD_DIM), per_b3),
        pl.BlockSpec((None, 8, CHUNK), per_b3),
        pl.BlockSpec((None, HEADS, HEAD_DIM, HEAD_DIM), per_b4),
    ] + cast_specs
    out_shape = [
        jax.ShapeDtypeStruct((B, T, D_MODEL), F32),
        jax.ShapeDtypeStruct((B, CONV_W - 1, 2 * MIX), F32),
        jax.ShapeDtypeStruct((B, HEADS, HEAD_DIM, HEAD_DIM), F32),
        jax.ShapeDtypeStruct((B, HEADS, HEAD_DIM), F32),
        jax.ShapeDtypeStruct((B, 8, CHUNK), F32),
        jax.ShapeDtypeStruct((B, HEADS, HEAD_DIM, HEAD_DIM), F32),
        jax.ShapeDtypeStruct((D_MODEL, D_FF), BF16), jax.ShapeDtypeStruct((D_MODEL, D_FF), BF16),
        jax.ShapeDtypeStruct((D_FF, D_MODEL), BF16),
    ]
    n_chunks = tt // CHUNK
    scratch = [
        pltpu.VMEM((2, tt + UOFF, N_COLS), F32),
        pltpu.VMEM((2, n_chunks, T_ROWS, CHUNK), F32),
        pltpu.VMEM((2, n_chunks, GT_ROWS, CHUNK), F32),
    ] + [
        pltpu.VMEM((UOFF, 2 * MIX), F32),
        pltpu.VMEM((HEADS, HEAD_DIM, HEAD_DIM), F32),
        pltpu.VMEM((tt, D_MODEL), BF16),
    ]
    return pl.pallas_call(
        functools.partial(_mixer_prompt_kernel, tt=tt, tiles_per_seq=tps),
        grid=(n_tiles + 1,),
        in_specs=in_specs, out_specs=out_specs, out_shape=out_shape,
        scratch_shapes=scratch,
        compiler_params=_params("arbitrary"),
    )(x, x, wn, wt, wgt, w_out, g_mix, bgt, w_conv, b_conv, gmh_cols, grh_cols, cos_f, sin_s,
      decay_t, qdec_rows, kdec_cols, cdec_rows, w_gate, w_up, w_down)


def _attn_prompt_kernel(x1_ref, k_ref, v_ref, wcq_ref, gx_ref, *rest, n_sample_steps):
    sample_in, o_ref = rest[:SAMPLE_MIXER_INPUTS], rest[SAMPLE_MIXER_INPUTS]
    sample_out_and_scratch = rest[SAMPLE_MIXER_INPUTS + 1:]
    step = pl.program_id(0) * pl.num_programs(1) + pl.program_id(1)

    @pl.when(step < n_sample_steps)
    def _():
        _mixer_sample_kernel(*sample_in, *sample_out_and_scratch, bb=SAMPLE_BLOCK)

    tm = x1_ref.shape[0]
    rows = [slice(r, r + tm // ROW_GROUPS) for r in range(0, tm, tm // ROW_GROUPS)]
    xq = [_rms(x1_ref[r, :], gx_ref[...]).astype(BF16) for r in rows]
    q = [jnp.dot(a, wcq_ref[...], preferred_element_type=F32).astype(BF16) for a in xq]
    sl = [slice(h * X_HEAD_DIM, (h + 1) * X_HEAD_DIM) for h in range(X_HEADS)]
    items = [(g, h) for g in range(ROW_GROUPS) for h in range(X_HEADS)]
    s = [_dot_nt(q[g][:, sl[h]], k_ref[:, sl[h]]) * (X_HEAD_DIM ** -0.5) for g, h in items]
    e = [jnp.exp(a - jnp.max(a, axis=-1, keepdims=True)) for a in s]
    p = [a / jnp.sum(a, axis=-1, keepdims=True) for a in e]
    for (g, h), a in zip(items, p):
        o_ref[rows[g], sl[h]] = _dot(a, v_ref[:, sl[h]]).astype(BF16)


def _attn_prompt(x1, mk, mv, w_cq, g_x, sample_mixer_operands, tm=512):
    B, T, _ = x1.shape
    tps = T // tm
    Bs = sample_mixer_operands[0].shape[0]
    n_sample_steps = Bs // SAMPLE_BLOCK
    assert n_sample_steps <= B * tps
    block_of = lambda b, t: jnp.minimum(b * tps + t, n_sample_steps - 1)
    s_in, s_out, s_shape, s_scratch = _sample_mixer_specs(Bs, block_of)
    tile = pl.BlockSpec((None, tm, D_MODEL), lambda b, t: (b, t, 0))
    kv = pl.BlockSpec((None, N_MEM, D_MODEL), lambda b, t: (b, 0, 0))
    outs = pl.pallas_call(
        functools.partial(_attn_prompt_kernel, n_sample_steps=n_sample_steps),
        grid=(B, tps),
        in_specs=[tile, kv, kv, _const_spec((D_MODEL, D_MODEL)), _const_spec((1, D_MODEL))] + s_in,
        out_specs=[tile] + s_out,
        out_shape=[jax.ShapeDtypeStruct((B, T, D_MODEL), BF16)] + s_shape,
        scratch_shapes=s_scratch,
        compiler_params=_params("arbitrary", "arbitrary"),
    )(x1, mk, mv, w_cq, g_x, *sample_mixer_operands)
    return outs[0], outs[1:]


V7X_MXU_DIM = 256
FF_CHUNKS = ((0, 6 * V7X_MXU_DIM), (6 * V7X_MXU_DIM, D_FF))


def _ffn_kernel(x1_ref, o_ref, wco_ref, wg_ref, wu_ref, wd_ref, gffn_ref, gfin_ref, *rest, n_groups):
    if len(rest) == 1:
        (y_ref,) = rest
        sample_stages = iter(())
    else:
        qs_ref, ck_ref, cv_ref, y_ref, os_ref = rest
        sample_stages = _sample_attention(qs_ref, ck_ref, cv_ref, os_ref)
    tm = x1_ref.shape[0]
    rows = [slice(r, r + tm // n_groups) for r in range(0, tm, tm // n_groups)]
    acc = [x1_ref[r, :] + _dot(o_ref[r, :], wco_ref[...]) for r in rows]
    hf = [_rms(a, gffn_ref[...]).astype(BF16) for a in acc]
    for c0, c1 in FF_CHUNKS:
        next(sample_stages, None)
        gate = [jnp.dot(h, wg_ref[:, c0:c1], preferred_element_type=F32) for h in hf]
        up = [jnp.dot(h, wu_ref[:, c0:c1], preferred_element_type=F32) for h in hf]
        act = [_silu(g) * u for g, u in zip(gate, up)]
        acc = [a + _dot(p, wd_ref[c0:c1, :]) for a, p in zip(acc, act)]
    next(sample_stages, None)
    for r, a in zip(rows, acc):
        y_ref[r, :] = _rms(a, gfin_ref[...])


def _ffn(x1, o, w_co, w_gate, w_up, w_down, g_ffn, g_final, tm, sample=None):
    rows = x1.shape[0]
    steps = rows // tm
    row_spec = pl.BlockSpec((tm, D_MODEL), lambda i: (i, 0))
    in_specs = [row_spec, row_spec, _const_spec((D_MODEL, D_MODEL)),
                _const_spec((D_MODEL, D_FF)), _const_spec((D_MODEL, D_FF)),
                _const_spec((D_FF, D_MODEL)), _const_spec((1, D_MODEL)), _const_spec((1, D_MODEL))]
    out_specs = [row_spec]
    out_shape = [jax.ShapeDtypeStruct((rows, D_MODEL), F32)]
    operands = [x1, o, w_co, w_gate, w_up, w_down, g_ffn, g_final]
    if sample is not None:
        q_s, ck_s, cv_s = sample
        Bs = q_s.shape[0]
        ba = Bs // steps
        assert ba * steps == Bs
        s_rows = pl.BlockSpec((ba, QROWS, V7X_LANES), lambda i: (i, 0, 0))
        s_kv = pl.BlockSpec((ba, KV_ROWS, V7X_LANES), lambda i: (i, 0, 0))
        in_specs += [s_rows, s_kv, s_kv]
        out_specs += [s_rows]
        out_shape += [jax.ShapeDtypeStruct((Bs, QROWS, V7X_LANES), F32)]
        operands += [q_s.reshape(Bs, QROWS, V7X_LANES), ck_s, cv_s]
    outs = pl.pallas_call(
        functools.partial(_ffn_kernel, n_groups=ROW_GROUPS if tm >= 512 else 1),
        grid=(steps,),
        in_specs=in_specs, out_specs=out_specs, out_shape=out_shape,
        compiler_params=_params("parallel"),
    )(*operands)
    if sample is None:
        return outs[0]
    return outs[0], outs[1].reshape(Bs, D_MODEL)


def _inproj_sample_kernel(x_ref, g_ref, wn_ref, wt_ref, wgt_ref, un_ref, ut_ref, gs_ref):
    xn = _rms(x_ref[...], g_ref[...]).astype(BF16)
    un_ref[...] = jnp.dot(xn, wn_ref[...], preferred_element_type=F32)
    ut_ref[...] = lax.dot_general(xn, wt_ref[...], NT_DIMS, preferred_element_type=F32)
    gs_ref[...] = lax.dot_general(xn, wgt_ref[...], NT_DIMS, preferred_element_type=F32)


def _inproj_sample(x, g_mix, wn, wt, wgt):
    rows = x.shape[0]
    full = lambda cols: pl.BlockSpec((rows, cols), lambda i: (0, 0))
    return pl.pallas_call(
        _inproj_sample_kernel,
        grid=(1,),
        in_specs=[_const_spec((rows, D_MODEL)), _const_spec((1, D_MODEL)),
                  _const_spec((D_MODEL, N_COLS)), _const_spec((T_ROWS, D_MODEL)),
                  _const_spec((GT_ROWS, D_MODEL))],
        out_specs=[full(N_COLS), full(T_ROWS), full(GT_ROWS)],
        out_shape=[jax.ShapeDtypeStruct((rows, N_COLS), F32),
                   jax.ShapeDtypeStruct((rows, T_ROWS), F32),
                   jax.ShapeDtypeStruct((rows, GT_ROWS), F32)],
        compiler_params=_params("arbitrary"),
    )(x, g_mix, wn, wt, wgt)


def _mixer_sample_kernel(un_ref, ut_ref, gs_ref, conv_ref, C_ref, n_ref, m_ref, S_ref, bg_ref,
                         wconv_ref, bconv_ref, gmh_ref, grh_ref, cos_ref, sin_ref, rtab_ref,
                         h_ref, convo_ref, Co_ref, no_ref, mo_ref, So_ref,
                         q_s, k_s, vws_s, qr_s, kr_s, qc_s, qs_s, carry_s, *, bb):
    k_scale = HEAD_DIM ** -0.5
    uqk = un_ref[:, N_QK:N_RQ]
    conv = (bconv_ref[...] + wconv_ref[0:1, :] * conv_ref[0] + wconv_ref[1:2, :] * conv_ref[1]
            + wconv_ref[2:3, :] * conv_ref[2] + wconv_ref[3:4, :] * uqk)
    convo_ref[0] = conv_ref[1]
    convo_ref[1] = conv_ref[2]
    convo_ref[2] = uqk
    qk_act = _silu(conv)
    q_s[...] = qk_act[:, 0:MIX]
    k_s[...] = qk_act[:, MIX:2 * MIX] * k_scale

    gates = gs_ref[...] + bg_ref[...]
    ig = gates[:, 0:HEADS]
    lf = _log_sigmoid(gates[:, 8:8 + HEADS])
    inter = lf + m_ref[...]
    m_t = jnp.maximum(inter, ig)
    ws = jnp.exp(ig - m_t)
    w_in = jnp.exp(inter - m_t)
    mo_ref[...] = m_t
    carry_s[:, 0:HEADS] = w_in

    cos_f = cos_ref[...]
    sin_s = sin_ref[...]
    for h in range(HEADS):
        lo = h * HEAD_DIM
        hs = slice(lo, lo + HEAD_DIM)
        vws_s[:, hs] = ut_ref[:, T_V + lo:T_V + lo + HEAD_DIM] * ws[:, h:h + 1]
        qr_s[:, hs] = _rope(un_ref[:, N_RQ + lo:N_RQ + lo + HEAD_DIM], cos_f, sin_s)
        kr_s[:, hs] = _rope(un_ref[:, N_RK + lo:N_RK + lo + HEAD_DIM], cos_f, sin_s) * k_scale

    heads = range(HEADS)
    seqs = range(bb)
    hsl = [slice(h * HEAD_DIM, (h + 1) * HEAD_DIM) for h in heads]
    seq_id = lax.broadcasted_iota(jnp.int32, (bb, HEAD_DIM), 0)
    for h in heads:
        q_h, k_h, vws_h = q_s[:, hsl[h]], k_s[:, hsl[h]], vws_s[:, hsl[h]]
        qr_h = qr_s[:, hsl[h]]
        krd_h = kr_s[:, hsl[h]] * rtab_ref[1:2, h:h + 1]
        rv_h = ut_ref[:, T_RV + h * HEAD_DIM:T_RV + (h + 1) * HEAD_DIM]
        c_old = [C_ref[b, h] for b in seqs]
        s_old = [S_ref[b, h] for b in seqs]
        q_c = [_dot_nt(q_h, c_old[b]) for b in seqs]
        q_st = [_dot(qr_h, s_old[b]) for b in seqs]
        d_c = [_dot_tn(jnp.where(seq_id == b, vws_h, 0.0), k_h) for b in seqs]
        d_s = [_dot_tn(jnp.where(seq_id == b, krd_h, 0.0), rv_h) for b in seqs]
        for b in seqs:
            row = slice(b, b + 1)
            qc_s[row, hsl[h]] = q_c[b][row]
            qs_s[row, hsl[h]] = q_st[b][row]
            Co_ref[b, h] = carry_s[row, h:h + 1] * c_old[b] + d_c[b]
            So_ref[b, h] = rtab_ref[2:3, h:h + 1] * s_old[b] + d_s[b]

    for h in range(HEADS):
        lo = h * HEAD_DIM
        hs = slice(lo, lo + HEAD_DIM)
        q_h, k_h, n_h = q_s[:, hs], k_s[:, hs], n_ref[:, hs]
        ws_h, w_in_h = ws[:, h:h + 1], w_in[:, h:h + 1]
        wts = ws_h * jnp.sum(q_h * k_h, axis=1, keepdims=True)
        num = wts * ut_ref[:, T_V + lo:T_V + lo + HEAD_DIM] + w_in_h * qc_s[:, hs]
        den = wts + w_in_h * jnp.sum(n_h * q_h, axis=1, keepdims=True)
        hm = num / jnp.maximum(jnp.abs(den), jnp.exp(-m_t[:, h:h + 1]))
        mo = ut_ref[:, T_O + lo:T_O + lo + HEAD_DIM]
        h_ref[:, hs] = _head_norm(hm) * gmh_ref[:, hs] * jax.nn.sigmoid(mo)
        no_ref[:, hs] = w_in_h * n_h + ws_h * k_h

        att = jnp.sum(qr_s[:, hs] * kr_s[:, hs], axis=1, keepdims=True) * rtab_ref[3:4, h:h + 1]
        o = (att * ut_ref[:, T_RV + lo:T_RV + lo + HEAD_DIM]
             + rtab_ref[0:1, h:h + 1] * qs_s[:, hs])
        rg = ut_ref[:, T_RG + lo:T_RG + lo + HEAD_DIM]
        h_ref[:, MIX + lo:MIX + lo + HEAD_DIM] = _head_norm(o) * grh_ref[:, hs] * _silu(rg)


SAMPLE_BLOCK = 8
SAMPLE_MIXER_INPUTS = 16


def _sample_mixer_specs(B, block_of):
    bb = SAMPLE_BLOCK
    rows = lambda cols: pl.BlockSpec((bb, cols), lambda *g: (block_of(*g), 0))
    mats = pl.BlockSpec((bb, HEADS, HEAD_DIM, HEAD_DIM), lambda *g: (block_of(*g), 0, 0, 0))
    convs = pl.BlockSpec((CONV_W - 1, bb, 2 * MIX), lambda *g: (0, block_of(*g), 0))
    in_specs = [rows(N_COLS), rows(T_ROWS), rows(GT_ROWS), convs, mats, rows(MIX), rows(HEADS), mats,
                _const_spec((1, GT_ROWS)), _const_spec((CONV_W, 2 * MIX)), _const_spec((1, 2 * MIX)),
                _const_spec((1, MIX)), _const_spec((1, MIX)),
                _const_spec((1, HEAD_DIM)), _const_spec((1, HEAD_DIM)), _const_spec((8, HEAD_DIM))]
    assert len(in_specs) == SAMPLE_MIXER_INPUTS
    out_specs = [rows(D_MODEL), convs, mats, rows(MIX), rows(HEADS), mats]
    out_shape = [jax.ShapeDtypeStruct((B, D_MODEL), F32),
                 jax.ShapeDtypeStruct((CONV_W - 1, B, 2 * MIX), F32),
                 jax.ShapeDtypeStruct((B, HEADS, HEAD_DIM, HEAD_DIM), F32),
                 jax.ShapeDtypeStruct((B, MIX), F32),
                 jax.ShapeDtypeStruct((B, HEADS), F32),
                 jax.ShapeDtypeStruct((B, HEADS, HEAD_DIM, HEAD_DIM), F32)]
    scratch = [pltpu.VMEM((bb, MIX), F32) for _ in range(7)] + [pltpu.VMEM((bb, HEAD_DIM), F32)]
    return in_specs, out_specs, out_shape, scratch


def _outq_sample_kernel(x_ref, h_ref, wout_ref, wcq_ref, gx_ref, x1_ref, q_ref):
    x1 = x_ref[...] + _dot(h_ref[...], wout_ref[...])
    x1_ref[...] = x1
    q_ref[...] = _dot(_rms(x1, gx_ref[...]), wcq_ref[...])


def _outq_sample(x, hcat, w_out, w_cq, g_x):
    rows = x.shape[0]
    full = _const_spec((rows, D_MODEL))
    return pl.pallas_call(
        _outq_sample_kernel,
        grid=(1,),
        in_specs=[full, full, _const_spec((D_MODEL, D_MODEL)), _const_spec((D_MODEL, D_MODEL)),
                  _const_spec((1, D_MODEL))],
        out_specs=[pl.BlockSpec((rows, D_MODEL), lambda i: (0, 0))] * 2,
        out_shape=[jax.ShapeDtypeStruct((rows, D_MODEL), F32)] * 2,
        compiler_params=_params("arbitrary"),
    )(x, hcat, w_out, w_cq, g_x)


def _sample_attention(q_ref, k_ref, v_ref, o_ref):
    assert X_HALVES == 2
    r_id = lax.broadcasted_iota(jnp.int32, (QROWS, KV_ROWS), 0)
    n_id = lax.broadcasted_iota(jnp.int32, (QROWS, KV_ROWS), 1)
    own = (n_id % QROWS) == (r_id // X_HALVES) + X_HEADS * (r_id % X_HALVES)
    low_half = (lax.broadcasted_iota(jnp.int32, (1, KV_ROWS), 1) % QROWS) < X_HEADS
    seqs = range(q_ref.shape[0])
    z = [_dot_nt(q_ref[j], k_ref[j]) for j in seqs]
    zc = [jnp.sum(jnp.where(own, z[j], 0.0), axis=0, keepdims=True) for j in seqs]
    yield
    other = [jnp.where(low_half, pltpu.roll(zc[j], KV_ROWS - X_HEADS, 1),
                       pltpu.roll(zc[j], X_HEADS, 1)) for j in seqs]
    s = [jnp.where(own, (zc[j] + other[j]) * (X_HEAD_DIM ** -0.5), -jnp.inf) for j in seqs]
    m = [jnp.max(s[j], axis=-1, keepdims=True) for j in seqs]
    e = [jnp.exp(s[j] - m[j]) for j in seqs]
    den = [jnp.sum(e[j], axis=-1, keepdims=True) for j in seqs]
    yield
    for j in seqs:
        o_ref[j] = _dot(e[j] / den[j], v_ref[j])
    yield


def _kv_rows(cache):
    B = cache.shape[0]
    c5 = cache.reshape(B, N_MEM, X_HEADS, X_HALVES, V7X_LANES)
    return jnp.transpose(c5, (0, 1, 3, 2, 4)).reshape(B, KV_ROWS, V7X_LANES)


def _kv_from_rows(rows):
    B = rows.shape[0]
    r5 = rows.reshape(B, N_MEM, X_HALVES, X_HEADS, V7X_LANES)
    return jnp.transpose(r5, (0, 1, 3, 2, 4)).reshape(1, B, N_MEM, X_HEADS, X_HEAD_DIM)


def _rope_tables(pos):
    half = HEAD_DIM // 2
    inv = ROPE_THETA ** (-np.arange(half, dtype=np.float64) / half)
    ang = np.asarray(pos, np.float64)[:, None] * inv[None, :]
    cos, sin = np.cos(ang), np.sin(ang)
    return (np.concatenate([cos, cos], axis=-1).astype(np.float32),
            np.concatenate([-sin, sin], axis=-1).astype(np.float32))


def _retention_tables(L):
    lg = np.log1p(-np.exp2(-5.0 - np.arange(HEADS, dtype=np.float64)))
    t = np.arange(L, dtype=np.float64)
    diff = t[:, None] - t[None, :]
    decay = np.where(diff >= 0, np.exp(lg[:, None, None] * np.maximum(diff, 0.0)), 0.0)
    q_dec = np.exp(lg[:, None] * (t + 1.0))
    k_dec = np.exp(lg[:, None] * (L - 1.0 - t))
    chunk_dec = np.exp(lg * L)
    return tuple(a.astype(np.float32) for a in (decay, q_dec, k_dec, chunk_dec))


def _lanes(a, n):
    xp = np if isinstance(a, np.ndarray) else jnp
    return xp.broadcast_to(a[..., None], a.shape + (n,))


def _pad_rows(a, rows):
    return np.pad(a, ((0, rows - a.shape[0]),) + ((0, 0),) * (a.ndim - 1))


def kernel(x_prompt, x_sample, cache_mem_k, cache_mem_v, state_mlstm_conv, state_mlstm_C, state_mlstm_n, state_mlstm_m, state_ret_S, mem_prompt, w_in, b_gate, w_conv, b_conv, g_mix, g_mhead, g_rhead, w_out, g_xattn, g_mem, w_ck, w_cv, w_cq, w_co, g_ffn, w_gate, w_up, w_down, g_final):
    Bp, Tp, _ = x_prompt.shape
    Bs = x_sample.shape[0]
    l = 0
    n_m = 4 * MIX
    wi = w_in[l]
    w_gates = wi[:, n_m:n_m + 2 * HEADS]
    w_ret = wi[:, n_m + 2 * HEADS:]
    row = lambda a: a.reshape(1, -1)
    bf = lambda a: a.astype(BF16)
    g_mix_r, g_mh_r, g_rh_r = row(g_mix[l]), row(g_mhead[l]), row(g_rhead[l])
    g_x_r, g_mem_r, g_ffn_r, g_fin_r = row(g_xattn[l]), row(g_mem[l]), row(g_ffn[l]), row(g_final)
    b_conv_r = row(b_conv[l])
    w_out_b, w_cq_b, w_co_b = bf(w_out[l]), bf(w_cq[l]), bf(w_co[l])

    mk_b, mv_b, mk_rows, mv_rows = _memkv(mem_prompt, g_mem_r, bf(w_ck[l]), bf(w_cv[l]))
    wn = bf(jnp.concatenate([wi[:, :2 * MIX], w_ret[:, :2 * MIX]], axis=1))
    wt = bf(jnp.concatenate([wi[:, 2 * MIX:n_m], w_ret[:, 2 * MIX:]], axis=1).T)
    gate_rows = lambda a: jnp.concatenate(
        [a[:HEADS], jnp.zeros((8 - HEADS,) + a.shape[1:], F32),
         a[HEADS:], jnp.zeros((8 - HEADS,) + a.shape[1:], F32)], axis=0)
    wgt = bf(gate_rows(w_gates.T))
    bg_rows = gate_rows(b_gate[l][:, None])
    bgt = jnp.broadcast_to(bg_rows, (GT_ROWS, CHUNK))
    cos_p, sin_p = _rope_tables(np.arange(Tp))
    decay, q_dec, k_dec, chunk_dec = _retention_tables(CHUNK)
    x1_p, conv_p, C_p, n_p, m_p, S_p, w_gate_b, w_up_b, w_down_b = _mixer_prompt(
        x_prompt, wn, wt, wgt, w_out_b, g_mix_r, bgt, w_conv[l], b_conv_r,
        _lanes(g_mhead[l], CHUNK), _lanes(g_rhead[l], CHUNK), cos_p, sin_p,
        np.swapaxes(decay, 1, 2), _pad_rows(q_dec, 8), _lanes(k_dec, HEAD_DIM),
        _pad_rows(_lanes(chunk_dec, HEAD_DIM), 8), w_gate[l], w_up[l], w_down[l])

    xs = x_sample.reshape(Bs, D_MODEL)
    un_s, ut_s, gs_s = _inproj_sample(xs, g_mix_r, wn, wt, wgt)
    cos_s, sin_s = _rope_tables(PAST_LEN + np.arange(1))
    decay1, q_dec1, k_dec1, chunk_dec1 = _retention_tables(1)
    rtab = np.zeros((8, HEAD_DIM), np.float32)
    rtab[:4, :HEADS] = np.stack([q_dec1[:, 0], k_dec1[:, 0], chunk_dec1, decay1[:, 0, 0]])
    sample_mixer_operands = (
        un_s, ut_s, gs_s, jnp.transpose(state_mlstm_conv[l], (1, 0, 2)), state_mlstm_C[l],
        state_mlstm_n[l].reshape(Bs, MIX), state_mlstm_m[l], state_ret_S[l],
        bg_rows.reshape(1, GT_ROWS), w_conv[l], b_conv_r, g_mh_r, g_rh_r, cos_s, sin_s, rtab)

    o_p, (hcat_s, conv_s, C_s, n_s, m_s, S_s) = _attn_prompt(
        x1_p, mk_b, mv_b, w_cq_b, g_x_r, sample_mixer_operands)
    x1_s, q_s = _outq_sample(xs, hcat_s, w_out_b, w_cq_b, g_x_r)
    y_p, o_s = _ffn(x1_p.reshape(Bp * Tp, D_MODEL), o_p.reshape(Bp * Tp, D_MODEL),
                    w_co_b, w_gate_b, w_up_b, w_down_b, g_ffn_r, g_fin_r, tm=512,
                    sample=(q_s, _kv_rows(cache_mem_k[l]), _kv_rows(cache_mem_v[l])))
    y_s = _ffn(x1_s, o_s, w_co_b, w_gate_b, w_up_b, w_down_b, g_ffn_r, g_fin_r, tm=Bs)

    return (y_p.reshape(Bp, Tp, D_MODEL), y_s.reshape(Bs, 1, D_MODEL),
            _kv_from_rows(mk_rows), _kv_from_rows(mv_rows),
            conv_p[None], C_p[None], n_p[None], m_p[None, :, :HEADS, 0], S_p[None],
            jnp.transpose(conv_s, (1, 0, 2))[None], C_s[None],
            n_s.reshape(1, Bs, HEADS, HEAD_DIM), m_s[None], S_s[None])
```

```python
import functools

import jax
import jax.numpy as jnp
import numpy as np
from jax import lax
from jax.experimental import pallas as pl
from jax.experimental.pallas import tpu as pltpu

F32 = jnp.float32
BF16 = jnp.bfloat16

D_MODEL = 1024
HEADS = 4
HEAD_DIM = 128
MIX = HEADS * HEAD_DIM
CONV_W = 4
CHUNK = 128
N_MEM = 256
X_HEADS = 4
X_HEAD_DIM = 256
D_FF = 2816
ROPE_THETA = 10000.0
EPS = 1e-6
PAST_LEN = 16384
V7X_LANES = 128
X_HALVES = X_HEAD_DIM // V7X_LANES
KV_ROWS = N_MEM * X_HALVES * X_HEADS
QROWS = X_HALVES * X_HEADS

N_QK, N_RQ, N_RK, N_COLS = 0, 1024, 1536, 2048
T_V, T_O, T_RV, T_RG, T_ROWS = 0, 512, 1024, 1536, 2048
GT_ROWS = 16
UOFF = 8
ROW_GROUPS = 2

V7X_VMEM_LIMIT = 56 * 1024 * 1024

NT_DIMS = (((1,), (1,)), ((), ()))
TN_DIMS = (((0,), (0,)), ((), ()))


def _dot(a, b):
    return jnp.dot(a.astype(BF16), b.astype(BF16), preferred_element_type=F32)


def _dot_nt(a, b):
    return lax.dot_general(a.astype(BF16), b.astype(BF16), NT_DIMS, preferred_element_type=F32)


def _dot_tn(a, b):
    return lax.dot_general(a.astype(BF16), b.astype(BF16), TN_DIMS, preferred_element_type=F32)


def _rms(x, g):
    return x * lax.rsqrt(jnp.mean(x * x, axis=-1, keepdims=True) + EPS) * g


def _head_norm(h):
    return h * lax.rsqrt(jnp.mean(h * h, axis=-1, keepdims=True) + EPS)


def _silu(x):
    return x * jax.nn.sigmoid(x)


def _log_sigmoid(x):
    return jnp.minimum(x, 0.0) - jnp.log1p(jnp.exp(-jnp.abs(x)))


def _rope(x, cos_full, sin_signed):
    return x * cos_full + pltpu.roll(x, HEAD_DIM // 2, 1) * sin_signed


def _const_spec(shape):
    zeros = (0,) * len(shape)
    return pl.BlockSpec(shape, lambda *_: zeros, pipeline_mode=pl.Buffered(1))


def _params(*sem):
    return pltpu.CompilerParams(dimension_semantics=sem, vmem_limit_bytes=V7X_VMEM_LIMIT)


def _memkv_kernel(mem_ref, g_ref, wk_ref, wv_ref, kb_ref, vb_ref, krows_ref, vrows_ref):
    mn = _rms(mem_ref[...], g_ref[...]).astype(BF16)
    for w_ref, b_ref, rows_ref in ((wk_ref, kb_ref, krows_ref), (wv_ref, vb_ref, vrows_ref)):
        proj = jnp.dot(mn, w_ref[...], preferred_element_type=F32)
        b_ref[...] = proj.astype(BF16)
        for h in range(X_HEADS):
            for c in range(X_HALVES):
                lane0 = h * X_HEAD_DIM + c * V7X_LANES
                rows_ref[pl.ds(c * X_HEADS + h, N_MEM, stride=QROWS), :] = proj[:, lane0:lane0 + V7X_LANES]


def _memkv(mem, g_mem, wk, wv):
    B = mem.shape[0]
    tok = pl.BlockSpec((None, N_MEM, D_MODEL), lambda b: (b, 0, 0))
    rows = pl.BlockSpec((None, KV_ROWS, V7X_LANES), lambda b: (b, 0, 0))
    return pl.pallas_call(
        _memkv_kernel,
        grid=(B,),
        in_specs=[tok, _const_spec((1, D_MODEL)),
                  _const_spec((D_MODEL, D_MODEL)), _const_spec((D_MODEL, D_MODEL))],
        out_specs=[tok, tok, rows, rows],
        out_shape=[jax.ShapeDtypeStruct((B, N_MEM, D_MODEL), BF16)] * 2
        + [jax.ShapeDtypeStruct((B, KV_ROWS, V7X_LANES), F32)] * 2,
        compiler_params=_params("parallel"),
    )(mem, g_mem, wk, wv)


def _mixer_prompt_kernel(x_ref, xs_ref, wn_ref, wt_ref, wgt_ref, wout_ref, gmix_ref, bgt_ref,
                         wconv_ref, bconv_ref, gmh_ref, grh_ref, cos_ref, sin_ref,
                         decay_ref, qdec_ref, kdec_ref, cdec_ref, wg32_ref, wu32_ref, wd32_ref,
                         x1_ref, conv_ref, C_ref, n_ref, m_ref, S_ref, wg16_ref, wu16_ref, wd16_ref,
                         un2, ut2, gt2, tail_ref, st_ref, h_ref,
                         *, tt, tiles_per_seq):
    wg16_ref[...] = wg32_ref[...].astype(BF16)
    wu16_ref[...] = wu32_ref[...].astype(BF16)
    wd16_ref[...] = wd32_ref[...].astype(BF16)
    L = CHUNK
    s = pl.program_id(0)
    t = (jnp.maximum(s, 1) - 1) % tiles_per_seq
    chunks = range(tt // L)
    slot_of = lambda i: (un2.at[i], ut2.at[i], gt2.at[i])

    @pl.when(s == 0)
    def _():
        for ref in slot_of(1) + (tail_ref,):
            ref[...] = jnp.zeros_like(ref)

    @pl.when(t == 0)
    def _():
        C_ref[...] = jnp.zeros_like(C_ref)
        st_ref[...] = jnp.zeros_like(st_ref)
        n_ref[...] = jnp.zeros_like(n_ref)
        m_ref[...] = jnp.zeros_like(m_ref)

    k_scale = HEAD_DIM ** -0.5

    def project(slot, xn, part):
        un_ref, ut_ref, gt_ref = slot
        c0 = part * MIX
        un_ref[UOFF:UOFF + tt, c0:c0 + MIX] = jnp.dot(xn, wn_ref[:, c0:c0 + MIX],
                                                      preferred_element_type=F32)
        res = lax.dot_general(wt_ref[c0:c0 + MIX, :], xn, NT_DIMS, preferred_element_type=F32)
        for c in chunks:
            ut_ref[c, c0:c0 + MIX, :] = res[:, c * L:(c + 1) * L]
        if part == 0:
            gates_t = lax.dot_general(wgt_ref[...], xn, NT_DIMS, preferred_element_type=F32)
            for c in chunks:
                gt_ref[c] = gates_t[:, c * L:(c + 1) * L]

    src_id = lax.broadcasted_iota(jnp.int32, (L, L), 0)
    tgt_id = lax.broadcasted_iota(jnp.int32, (L, L), 1)
    causal = src_id <= tgt_id
    triu_bf = jnp.where(causal, 1.0, 0.0).astype(BF16)
    heads = range(HEADS)
    hcol = lambda base, h: slice(base + h * HEAD_DIM, base + (h + 1) * HEAD_DIM)

    def chunk_body(c, slot):
        un_ref, ut_ref, gt_ref = slot
        r0 = c * L
        rows = pl.ds(UOFF + r0, L)
        trows = pl.ds(r0, L)
        g_t = gt_ref[c] + bgt_ref[...]
        ig = g_t[0:8]
        lf = _log_sigmoid(g_t[8:16])
        lf_hi = lf.astype(BF16)
        r1 = lf - lf_hi.astype(F32)
        lf_mid = r1.astype(BF16)
        lf_lo = (r1 - lf_mid.astype(F32)).astype(BF16)
        bc = (jnp.dot(lf_hi, triu_bf, preferred_element_type=F32)
              + jnp.dot(lf_mid, triu_bf, preferred_element_type=F32)
              + jnp.dot(lf_lo, triu_bf, preferred_element_type=F32))
        m_prev = m_ref[...]
        inter = bc + m_prev
        b_last = bc[:, L - 1:L]
        g_w = b_last - bc + ig
        m_new = jnp.maximum(b_last + m_prev, jnp.max(g_w, axis=1, keepdims=True))
        ws = jnp.exp(g_w - m_new)
        carry = jnp.exp(b_last + m_prev - m_new)
        a_n = jnp.concatenate([ig - bc, jnp.zeros((L - 8, L), F32)], axis=0).T

        def conv_act(col):
            win = un_ref[pl.ds(r0, L + UOFF), col:col + HEAD_DIM]
            acc = bconv_ref[:, col:col + HEAD_DIM]
            for j in range(CONV_W):
                back = CONV_W - 1 - j
                tap = win if back == 0 else pltpu.roll(win, back, 0)
                acc = acc + tap[UOFF:UOFF + L] * wconv_ref[j:j + 1, col:col + HEAD_DIM]
            return _silu(acc)

        cos_f, sin_s = cos_ref[trows, :], sin_ref[trows, :]
        q = [conv_act(N_QK + h * HEAD_DIM) for h in heads]
        kb = [(conv_act(N_QK + MIX + h * HEAD_DIM) * k_scale).astype(BF16) for h in heads]
        qb = [a.astype(BF16) for a in q]
        rqb = [_rope(un_ref[rows, hcol(N_RQ, h)], cos_f, sin_s).astype(BF16) for h in heads]
        rk = [_rope(un_ref[rows, hcol(N_RK, h)], cos_f, sin_s) * k_scale for h in heads]
        v_t = [ut_ref[c, hcol(T_V, h), :] for h in heads]
        rvb = [ut_ref[c, hcol(T_RV, h), :].astype(BF16) for h in heads]
        c_old = [C_ref[h] for h in heads]
        s_old = [st_ref[h] for h in heads]
        n_old = [n_ref[h:h + 1, :] for h in heads]
        qk = [_dot_nt(kb[h], qb[h]) for h in heads]
        att = [_dot_nt(rk[h], rqb[h]) for h in heads]
        c_q = [_dot_nt(c_old[h], qb[h]) for h in heads]
        s_q = [_dot_nt(s_old[h], rqb[h]) for h in heads]
        n_q = [_dot_nt(jnp.broadcast_to(n_old[h], (8, HEAD_DIM)), qb[h])[0:1] for h in heads]
        d_c = [_dot(v_t[h] * ws[h:h + 1, :], kb[h]) for h in heads]
        d_n = [_dot(jnp.broadcast_to(ws[h:h + 1, :], (8, L)), kb[h])[0:1] for h in heads]
        d_s = [_dot(rvb[h], rk[h] * kdec_ref[h]) for h in heads]
        m_t, w_in, wts = [], [], []
        for h in heads:
            dmat = a_n[:, h:h + 1] + bc[h:h + 1, :]
            dmat = jnp.where(causal, dmat, -jnp.inf)
            m_t.append(jnp.maximum(inter[h:h + 1, :], jnp.max(dmat, axis=0, keepdims=True)))
            wts.append(jnp.exp(dmat - m_t[h]) * qk[h])
            w_in.append(jnp.exp(inter[h:h + 1, :] - m_t[h]))
        att_w = [att[h] * decay_ref[h] for h in heads]
        v_p = [_dot(v_t[h], wts[h]) for h in heads]
        v_a = [_dot(rvb[h], att_w[h]) for h in heads]
        for h in heads:
            num = v_p[h] + w_in[h] * c_q[h]
            den = jnp.sum(wts[h], axis=0, keepdims=True) + w_in[h] * n_q[h]
            hm = num / jnp.maximum(jnp.abs(den), jnp.exp(-m_t[h]))
            hm = hm * lax.rsqrt(jnp.mean(hm * hm, axis=0, keepdims=True) + EPS)
            hm = hm * gmh_ref[hcol(0, h), :] * jax.nn.sigmoid(ut_ref[c, hcol(T_O, h), :])
            h_ref[trows, hcol(0, h)] = hm.T.astype(BF16)
            carry_h = carry[h:h + 1, :]
            C_ref[h] = carry_h * c_old[h] + d_c[h]
            n_ref[h:h + 1, :] = carry_h * n_old[h] + d_n[h]
            o = v_a[h] + qdec_ref[h:h + 1, :] * s_q[h]
            st_ref[h] = cdec_ref[h:h + 1, :] * s_old[h] + d_s[h]
            hr = o * lax.rsqrt(jnp.mean(o * o, axis=0, keepdims=True) + EPS)
            hr = hr * grh_ref[hcol(0, h), :] * _silu(ut_ref[c, hcol(T_RG, h), :])
            h_ref[trows, hcol(MIX, h)] = hr.T.astype(BF16)

        m_ref[...] = m_new

    def step(proj_slot, scan_slot):
        un_ref = scan_slot[0]
        un_ref[0:UOFF, N_QK:N_RQ] = jnp.where(t == 0, 0.0, tail_ref[...])
        xn = _rms(x_ref[...], gmix_ref[...]).astype(BF16)
        for c in chunks:
            chunk_body(c, scan_slot)
            project(proj_slot, xn, c)
        tail_ref[...] = un_ref[tt:tt + UOFF, N_QK:N_RQ]
        x1_ref[...] = xs_ref[...] + jnp.dot(h_ref[...], wout_ref[...], preferred_element_type=F32)

    step(slot_of(s % 2), slot_of(1 - s % 2))

    @pl.when(t == tiles_per_seq - 1)
    def _():
        conv_ref[...] = tail_ref[UOFF - (CONV_W - 1):UOFF, :]
        for h in heads:
            S_ref[h] = st_ref[h].T


def _mixer_prompt(x, wn, wt, wgt, w_out, g_mix, bgt, w_conv, b_conv, gmh_cols, grh_cols, cos_f, sin_s,
                  decay_t, qdec_rows, kdec_cols, cdec_rows, w_gate, w_up, w_down, tt=512):
    B, T, _ = x.shape
    tps = T // tt
    n_tiles = B * tps
    proj = lambda s: jnp.minimum(s, n_tiles - 1)
    scan = lambda s: jnp.maximum(s, 1) - 1
    per_b3 = lambda s: (scan(s) // tps, 0, 0)
    per_b4 = lambda s: (scan(s) // tps, 0, 0, 0)
    scan_tile = pl.BlockSpec((None, tt, D_MODEL), lambda s: (scan(s) // tps, scan(s) % tps, 0))
    in_specs = [
        pl.BlockSpec((None, tt, D_MODEL), lambda s: (proj(s) // tps, proj(s) % tps, 0)),
        scan_tile,
        _const_spec((D_MODEL, N_COLS)), _const_spec((T_ROWS, D_MODEL)), _const_spec((GT_ROWS, D_MODEL)),
        _const_spec((D_MODEL, D_MODEL)),
        _const_spec((1, D_MODEL)), _const_spec((GT_ROWS, CHUNK)),
        _const_spec((CONV_W, 2 * MIX)), _const_spec((1, 2 * MIX)),
        _const_spec((MIX, CHUNK)), _const_spec((MIX, CHUNK)),
        pl.BlockSpec((tt, HEAD_DIM), lambda s: (scan(s) % tps, 0)),
        pl.BlockSpec((tt, HEAD_DIM), lambda s: (scan(s) % tps, 0)),
        _const_spec((HEADS, CHUNK, CHUNK)),
        _const_spec((8, CHUNK)), _const_spec((HEADS, CHUNK, HEAD_DIM)), _const_spec((8, HEAD_DIM)),
    ]
    ff_blocks = D_FF // HEAD_DIM
    assert ff_blocks <= n_tiles + 1
    early = lambda s: jnp.minimum(s, ff_blocks - 1)
    late = lambda s: jnp.maximum(s - (n_tiles + 1 - ff_blocks), 0)
    cast_specs = [pl.BlockSpec((D_MODEL, HEAD_DIM), lambda s: (0, early(s))),
                  pl.BlockSpec((D_MODEL, HEAD_DIM), lambda s: (0, late(s))),
                  pl.BlockSpec((HEAD_DIM, D_MODEL), lambda s: (late(s), 0))]
    in_specs += cast_specs
    out_specs = [
        scan_tile,
        pl.BlockSpec((None, CONV_W - 1, 2 * MIX), per_b3),
        pl.BlockSpec((None, HEADS, HEAD_DIM, HEAD_DIM), per_b4),
        pl.BlockSpec((None, HEADS, HEAD_DIM), per_b3),
        pl.BlockSpec((None, 8, CHUNK), per_b3),
        pl.BlockSpec((None, HEADS, HEAD_DIM, HEAD_DIM), per_b4),
    ] + cast_specs
    out_shape = [
        jax.ShapeDtypeStruct((B, T, D_MODEL), F32),
        jax.ShapeDtypeStruct((B, CONV_W - 1, 2 * MIX), F32),
        jax.ShapeDtypeStruct((B, HEADS, HEAD_DIM, HEAD_DIM), F32),
        jax.ShapeDtypeStruct((B, HEADS, HEAD_DIM), F32),
        jax.ShapeDtypeStruct((B, 8, CHUNK), F32),
        jax.ShapeDtypeStruct((B, HEADS, HEAD_DIM, HEAD_DIM), F32),
        jax.ShapeDtypeStruct((D_MODEL, D_FF), BF16), jax.ShapeDtypeStruct((D_MODEL, D_FF), BF16),
        jax.ShapeDtypeStruct((D_FF, D_MODEL), BF16),
    ]
    n_chunks = tt // CHUNK
    scratch = [
        pltpu.VMEM((2, tt + UOFF, N_COLS), F32),
        pltpu.VMEM((2, n_chunks, T_ROWS, CHUNK), F32),
        pltpu.VMEM((2, n_chunks, GT_ROWS, CHUNK), F32),
    ] + [
        pltpu.VMEM((UOFF, 2 * MIX), F32),
        pltpu.VMEM((HEADS, HEAD_DIM, HEAD_DIM), F32),
        pltpu.VMEM((tt, D_MODEL), BF16),
    ]
    return pl.pallas_call(
        functools.partial(_mixer_prompt_kernel, tt=tt, tiles_per_seq=tps),
        grid=(n_tiles + 1,),
        in_specs=in_specs, out_specs=out_specs, out_shape=out_shape,
        scratch_shapes=scratch,
        compiler_params=_params("arbitrary"),
    )(x, x, wn, wt, wgt, w_out, g_mix, bgt, w_conv, b_conv, gmh_cols, grh_cols, cos_f, sin_s,
      decay_t, qdec_rows, kdec_cols, cdec_rows, w_gate, w_up, w_down)


def _attn_prompt_kernel(x1_ref, k_ref, v_ref, wcq_ref, gx_ref, *rest, sample_stride):
    sample_in, o_ref = rest[:SAMPLE_MIXER_INPUTS], rest[SAMPLE_MIXER_INPUTS]
    sample_out_and_scratch = rest[SAMPLE_MIXER_INPUTS + 1:]
    step = pl.program_id(0) * pl.num_programs(1) + pl.program_id(1)

    @pl.when(step % sample_stride == 0)
    def _():
        _mixer_sample_kernel(*sample_in, *sample_out_and_scratch, bb=SAMPLE_BLOCK)

    tm = x1_ref.shape[0]
    rows = [slice(r, r + tm // ROW_GROUPS) for r in range(0, tm, tm // ROW_GROUPS)]
    xq = [_rms(x1_ref[r, :], gx_ref[...]).astype(BF16) for r in rows]
    q = [jnp.dot(a, wcq_ref[...], preferred_element_type=F32).astype(BF16) for a in xq]
    sl = [slice(h * X_HEAD_DIM, (h + 1) * X_HEAD_DIM) for h in range(X_HEADS)]
    items = [(g, h) for g in range(ROW_GROUPS) for h in range(X_HEADS)]
    s = [_dot_nt(q[g][:, sl[h]], k_ref[:, sl[h]]) * (X_HEAD_DIM ** -0.5) for g, h in items]
    e = [jnp.exp(a - jnp.max(a, axis=-1, keepdims=True)) for a in s]
    p = [a / jnp.sum(a, axis=-1, keepdims=True) for a in e]
    for (g, h), a in zip(items, p):
        o_ref[rows[g], sl[h]] = _dot(a, v_ref[:, sl[h]]).astype(BF16)


def _attn_prompt(x1, mk, mv, w_cq, g_x, sample_mixer_operands, tm=512):
    B, T, _ = x1.shape
    tps = T // tm
    Bs = sample_mixer_operands[0].shape[0]
    n_blocks = Bs // SAMPLE_BLOCK
    sample_stride = (B * tps) // n_blocks
    assert sample_stride * n_blocks == B * tps
    block_of = lambda b, t: (b * tps + t) // sample_stride
    s_in, s_out, s_shape, s_scratch = _sample_mixer_specs(Bs, block_of)
    tile = pl.BlockSpec((None, tm, D_MODEL), lambda b, t: (b, t, 0))
    kv = pl.BlockSpec((None, N_MEM, D_MODEL), lambda b, t: (b, 0, 0))
    outs = pl.pallas_call(
        functools.partial(_attn_prompt_kernel, sample_stride=sample_stride),
        grid=(B, tps),
        in_specs=[tile, kv, kv, _const_spec((D_MODEL, D_MODEL)), _const_spec((1, D_MODEL))] + s_in,
        out_specs=[tile] + s_out,
        out_shape=[jax.ShapeDtypeStruct((B, T, D_MODEL), BF16)] + s_shape,
        scratch_shapes=s_scratch,
        compiler_params=_params("arbitrary", "arbitrary"),
    )(x1, mk, mv, w_cq, g_x, *sample_mixer_operands)
    return outs[0], outs[1:]


V7X_MXU_DIM = 256
FF_CHUNKS = ((0, 6 * V7X_MXU_DIM), (6 * V7X_MXU_DIM, D_FF))


def _ffn_kernel(x1_ref, o_ref, wco_ref, wg_ref, wu_ref, wd_ref, gffn_ref, gfin_ref, *rest, n_groups):
    if len(rest) == 1:
        (y_ref,) = rest
        sample_stages = iter(())
    else:
        qs_ref, ck_ref, cv_ref, y_ref, os_ref = rest
        sample_stages = _sample_attention(qs_ref, ck_ref, cv_ref, os_ref)
    tm = x1_ref.shape[0]
    rows = [slice(r, r + tm // n_groups) for r in range(0, tm, tm // n_groups)]
    acc = [x1_ref[r, :] + _dot(o_ref[r, :], wco_ref[...]) for r in rows]
    hf = [_rms(a, gffn_ref[...]).astype(BF16) for a in acc]
    for c0, c1 in FF_CHUNKS:
        next(sample_stages, None)
        gate = [jnp.dot(h, wg_ref[:, c0:c1], preferred_element_type=F32) for h in hf]
        up = [jnp.dot(h, wu_ref[:, c0:c1], preferred_element_type=F32) for h in hf]
        act = [_silu(g) * u for g, u in zip(gate, up)]
        acc = [a + _dot(p, wd_ref[c0:c1, :]) for a, p in zip(acc, act)]
    next(sample_stages, None)
    for r, a in zip(rows, acc):
        y_ref[r, :] = _rms(a, gfin_ref[...])


def _ffn(x1, o, w_co, w_gate, w_up, w_down, g_ffn, g_final, tm, sample=None):
    rows = x1.shape[0]
    steps = rows // tm
    row_spec = pl.BlockSpec((tm, D_MODEL), lambda i: (i, 0))
    in_specs = [row_spec, row_spec, _const_spec((D_MODEL, D_MODEL)),
                _const_spec((D_MODEL, D_FF)), _const_spec((D_MODEL, D_FF)),
                _const_spec((D_FF, D_MODEL)), _const_spec((1, D_MODEL)), _const_spec((1, D_MODEL))]
    out_specs = [row_spec]
    out_shape = [jax.ShapeDtypeStruct((rows, D_MODEL), F32)]
    operands = [x1, o, w_co, w_gate, w_up, w_down, g_ffn, g_final]
    if sample is not None:
        q_s, ck_s, cv_s = sample
        Bs = q_s.shape[0]
        ba = Bs // steps
        assert ba * steps == Bs
        s_rows = pl.BlockSpec((ba, QROWS, V7X_LANES), lambda i: (i, 0, 0))
        s_kv = pl.BlockSpec((ba, KV_ROWS, V7X_LANES), lambda i: (i, 0, 0))
        in_specs += [s_rows, s_kv, s_kv]
        out_specs += [s_rows]
        out_shape += [jax.ShapeDtypeStruct((Bs, QROWS, V7X_LANES), F32)]
        operands += [q_s.reshape(Bs, QROWS, V7X_LANES), ck_s, cv_s]
    outs = pl.pallas_call(
        functools.partial(_ffn_kernel, n_groups=ROW_GROUPS if tm >= 512 else 1),
        grid=(steps,),
        in_specs=in_specs, out_specs=out_specs, out_shape=out_shape,
        compiler_params=_params("parallel"),
    )(*operands)
    if sample is None:
        return outs[0]
    return outs[0], outs[1].reshape(Bs, D_MODEL)


def _inproj_sample_kernel(x_ref, g_ref, wn_ref, wt_ref, wgt_ref, un_ref, ut_ref, gs_ref):
    xn = _rms(x_ref[...], g_ref[...]).astype(BF16)
    un_ref[...] = jnp.dot(xn, wn_ref[...], preferred_element_type=F32)
    ut_ref[...] = lax.dot_general(xn, wt_ref[...], NT_DIMS, preferred_element_type=F32)
    gs_ref[...] = lax.dot_general(xn, wgt_ref[...], NT_DIMS, preferred_element_type=F32)


def _inproj_sample(x, g_mix, wn, wt, wgt):
    rows = x.shape[0]
    full = lambda cols: pl.BlockSpec((rows, cols), lambda i: (0, 0))
    return pl.pallas_call(
        _inproj_sample_kernel,
        grid=(1,),
        in_specs=[_const_spec((rows, D_MODEL)), _const_spec((1, D_MODEL)),
                  _const_spec((D_MODEL, N_COLS)), _const_spec((T_ROWS, D_MODEL)),
                  _const_spec((GT_ROWS, D_MODEL))],
        out_specs=[full(N_COLS), full(T_ROWS), full(GT_ROWS)],
        out_shape=[jax.ShapeDtypeStruct((rows, N_COLS), F32),
                   jax.ShapeDtypeStruct((rows, T_ROWS), F32),
                   jax.ShapeDtypeStruct((rows, GT_ROWS), F32)],
        compiler_params=_params("arbitrary"),
    )(x, g_mix, wn, wt, wgt)


def _mixer_sample_kernel(un_ref, ut_ref, gs_ref, conv_ref, C_ref, n_ref, m_ref, S_ref, bg_ref,
                         wconv_ref, bconv_ref, gmh_ref, grh_ref, cos_ref, sin_ref, rtab_ref,
                         h_ref, convo_ref, Co_ref, no_ref, mo_ref, So_ref,
                         q_s, k_s, vws_s, qr_s, kr_s, qc_s, qs_s, carry_s, *, bb):
    k_scale = HEAD_DIM ** -0.5
    uqk = un_ref[:, N_QK:N_RQ]
    conv = (bconv_ref[...] + wconv_ref[0:1, :] * conv_ref[0] + wconv_ref[1:2, :] * conv_ref[1]
            + wconv_ref[2:3, :] * conv_ref[2] + wconv_ref[3:4, :] * uqk)
    convo_ref[0] = conv_ref[1]
    convo_ref[1] = conv_ref[2]
    convo_ref[2] = uqk
    qk_act = _silu(conv)
    q_s[...] = qk_act[:, 0:MIX]
    k_s[...] = qk_act[:, MIX:2 * MIX] * k_scale

    gates = gs_ref[...] + bg_ref[...]
    ig = gates[:, 0:HEADS]
    lf = _log_sigmoid(gates[:, 8:8 + HEADS])
    inter = lf + m_ref[...]
    m_t = jnp.maximum(inter, ig)
    ws = jnp.exp(ig - m_t)
    w_in = jnp.exp(inter - m_t)
    mo_ref[...] = m_t
    carry_s[:, 0:HEADS] = w_in

    cos_f = cos_ref[...]
    sin_s = sin_ref[...]
    for h in range(HEADS):
        lo = h * HEAD_DIM
        hs = slice(lo, lo + HEAD_DIM)
        vws_s[:, hs] = ut_ref[:, T_V + lo:T_V + lo + HEAD_DIM] * ws[:, h:h + 1]
        qr_s[:, hs] = _rope(un_ref[:, N_RQ + lo:N_RQ + lo + HEAD_DIM], cos_f, sin_s)
        kr_s[:, hs] = _rope(un_ref[:, N_RK + lo:N_RK + lo + HEAD_DIM], cos_f, sin_s) * k_scale

    heads = range(HEADS)
    seqs = range(bb)
    hsl = [slice(h * HEAD_DIM, (h + 1) * HEAD_DIM) for h in heads]
    seq_id = lax.broadcasted_iota(jnp.int32, (bb, HEAD_DIM), 0)
    for h in heads:
        q_h, k_h, vws_h = q_s[:, hsl[h]], k_s[:, hsl[h]], vws_s[:, hsl[h]]
        qr_h = qr_s[:, hsl[h]]
        krd_h = kr_s[:, hsl[h]] * rtab_ref[1:2, h:h + 1]
        rv_h = ut_ref[:, T_RV + h * HEAD_DIM:T_RV + (h + 1) * HEAD_DIM]
        c_old = [C_ref[b, h] for b in seqs]
        s_old = [S_ref[b, h] for b in seqs]
        q_c = [_dot_nt(q_h, c_old[b]) for b in seqs]
        q_st = [_dot(qr_h, s_old[b]) for b in seqs]
        d_c = [_dot_tn(jnp.where(seq_id == b, vws_h, 0.0), k_h) for b in seqs]
        d_s = [_dot_tn(jnp.where(seq_id == b, krd_h, 0.0), rv_h) for b in seqs]
        for b in seqs:
            row = slice(b, b + 1)
            qc_s[row, hsl[h]] = q_c[b][row]
            qs_s[row, hsl[h]] = q_st[b][row]
            Co_ref[b, h] = carry_s[row, h:h + 1] * c_old[b] + d_c[b]
            So_ref[b, h] = rtab_ref[2:3, h:h + 1] * s_old[b] + d_s[b]

    for h in range(HEADS):
        lo = h * HEAD_DIM
        hs = slice(lo, lo + HEAD_DIM)
        q_h, k_h, n_h = q_s[:, hs], k_s[:, hs], n_ref[:, hs]
        ws_h, w_in_h = ws[:, h:h + 1], w_in[:, h:h + 1]
        wts = ws_h * jnp.sum(q_h * k_h, axis=1, keepdims=True)
        num = wts * ut_ref[:, T_V + lo:T_V + lo + HEAD_DIM] + w_in_h * qc_s[:, hs]
        den = wts + w_in_h * jnp.sum(n_h * q_h, axis=1, keepdims=True)
        hm = num / jnp.maximum(jnp.abs(den), jnp.exp(-m_t[:, h:h + 1]))
        mo = ut_ref[:, T_O + lo:T_O + lo + HEAD_DIM]
        h_ref[:, hs] = _head_norm(hm) * gmh_ref[:, hs] * jax.nn.sigmoid(mo)
        no_ref[:, hs] = w_in_h * n_h + ws_h * k_h

        att = jnp.sum(qr_s[:, hs] * kr_s[:, hs], axis=1, keepdims=True) * rtab_ref[3:4, h:h + 1]
        o = (att * ut_ref[:, T_RV + lo:T_RV + lo + HEAD_DIM]
             + rtab_ref[0:1, h:h + 1] * qs_s[:, hs])
        rg = ut_ref[:, T_RG + lo:T_RG + lo + HEAD_DIM]
        h_ref[:, MIX + lo:MIX + lo + HEAD_DIM] = _head_norm(o) * grh_ref[:, hs] * _silu(rg)


SAMPLE_BLOCK = 8
SAMPLE_MIXER_INPUTS = 16


def _sample_mixer_specs(B, block_of):
    bb = SAMPLE_BLOCK
    rows = lambda cols: pl.BlockSpec((bb, cols), lambda *g: (block_of(*g), 0))
    mats = pl.BlockSpec((bb, HEADS, HEAD_DIM, HEAD_DIM), lambda *g: (block_of(*g), 0, 0, 0))
    convs = pl.BlockSpec((CONV_W - 1, bb, 2 * MIX), lambda *g: (0, block_of(*g), 0))
    in_specs = [rows(N_COLS), rows(T_ROWS), rows(GT_ROWS), convs, mats, rows(MIX), rows(HEADS), mats,
                _const_spec((1, GT_ROWS)), _const_spec((CONV_W, 2 * MIX)), _const_spec((1, 2 * MIX)),
                _const_spec((1, MIX)), _const_spec((1, MIX)),
                _const_spec((1, HEAD_DIM)), _const_spec((1, HEAD_DIM)), _const_spec((8, HEAD_DIM))]
    assert len(in_specs) == SAMPLE_MIXER_INPUTS
    out_specs = [rows(D_MODEL), convs, mats, rows(MIX), rows(HEADS), mats]
    out_shape = [jax.ShapeDtypeStruct((B, D_MODEL), F32),
                 jax.ShapeDtypeStruct((CONV_W - 1, B, 2 * MIX), F32),
                 jax.ShapeDtypeStruct((B, HEADS, HEAD_DIM, HEAD_DIM), F32),
                 jax.ShapeDtypeStruct((B, MIX), F32),
                 jax.ShapeDtypeStruct((B, HEADS), F32),
                 jax.ShapeDtypeStruct((B, HEADS, HEAD_DIM, HEAD_DIM), F32)]
    scratch = [pltpu.VMEM((bb, MIX), F32) for _ in range(7)] + [pltpu.VMEM((bb, HEAD_DIM), F32)]
    return in_specs, out_specs, out_shape, scratch


def _outq_sample_kernel(x_ref, h_ref, wout_ref, wcq_ref, gx_ref, x1_ref, q_ref):
    x1 = x_ref[...] + _dot(h_ref[...], wout_ref[...])
    x1_ref[...] = x1
    q_ref[...] = _dot(_rms(x1, gx_ref[...]), wcq_ref[...])


def _outq_sample(x, hcat, w_out, w_cq, g_x):
    rows = x.shape[0]
    full = _const_spec((rows, D_MODEL))
    return pl.pallas_call(
        _outq_sample_kernel,
        grid=(1,),
        in_specs=[full, full, _const_spec((D_MODEL, D_MODEL)), _const_spec((D_MODEL, D_MODEL)),
                  _const_spec((1, D_MODEL))],
        out_specs=[pl.BlockSpec((rows, D_MODEL), lambda i: (0, 0))] * 2,
        out_shape=[jax.ShapeDtypeStruct((rows, D_MODEL), F32)] * 2,
        compiler_params=_params("arbitrary"),
    )(x, hcat, w_out, w_cq, g_x)


def _sample_attention(q_ref, k_ref, v_ref, o_ref):
    assert X_HALVES == 2
    r_id = lax.broadcasted_iota(jnp.int32, (QROWS, KV_ROWS), 0)
    n_id = lax.broadcasted_iota(jnp.int32, (QROWS, KV_ROWS), 1)
    own = (n_id % QROWS) == (r_id // X_HALVES) + X_HEADS * (r_id % X_HALVES)
    low_half = (lax.broadcasted_iota(jnp.int32, (1, KV_ROWS), 1) % QROWS) < X_HEADS
    seqs = range(q_ref.shape[0])
    z = [_dot_nt(q_ref[j], k_ref[j]) for j in seqs]
    zc = [jnp.sum(jnp.where(own, z[j], 0.0), axis=0, keepdims=True) for j in seqs]
    yield
    other = [jnp.where(low_half, pltpu.roll(zc[j], KV_ROWS - X_HEADS, 1),
                       pltpu.roll(zc[j], X_HEADS, 1)) for j in seqs]
    s = [jnp.where(own, (zc[j] + other[j]) * (X_HEAD_DIM ** -0.5), -jnp.inf) for j in seqs]
    m = [jnp.max(s[j], axis=-1, keepdims=True) for j in seqs]
    e = [jnp.exp(s[j] - m[j]) for j in seqs]
    den = [jnp.sum(e[j], axis=-1, keepdims=True) for j in seqs]
    yield
    for j in seqs:
        o_ref[j] = _dot(e[j] / den[j], v_ref[j])
    yield


def _kv_rows(cache):
    B = cache.shape[0]
    c5 = cache.reshape(B, N_MEM, X_HEADS, X_HALVES, V7X_LANES)
    return jnp.transpose(c5, (0, 1, 3, 2, 4)).reshape(B, KV_ROWS, V7X_LANES)


def _kv_from_rows(rows):
    B = rows.shape[0]
    r5 = rows.reshape(B, N_MEM, X_HALVES, X_HEADS, V7X_LANES)
    return jnp.transpose(r5, (0, 1, 3, 2, 4)).reshape(1, B, N_MEM, X_HEADS, X_HEAD_DIM)


def _rope_tables(pos):
    half = HEAD_DIM // 2
    inv = ROPE_THETA ** (-np.arange(half, dtype=np.float64) / half)
    ang = np.asarray(pos, np.float64)[:, None] * inv[None, :]
    cos, sin = np.cos(ang), np.sin(ang)
    return (np.concatenate([cos, cos], axis=-1).astype(np.float32),
            np.concatenate([-sin, sin], axis=-1).astype(np.float32))


def _retention_tables(L):
    lg = np.log1p(-np.exp2(-5.0 - np.arange(HEADS, dtype=np.float64)))
    t = np.arange(L, dtype=np.float64)
    diff = t[:, None] - t[None, :]
    decay = np.where(diff >= 0, np.exp(lg[:, None, None] * np.maximum(diff, 0.0)), 0.0)
    q_dec = np.exp(lg[:, None] * (t + 1.0))
    k_dec = np.exp(lg[:, None] * (L - 1.0 - t))
    chunk_dec = np.exp(lg * L)
    return tuple(a.astype(np.float32) for a in (decay, q_dec, k_dec, chunk_dec))


def _lanes(a, n):
    xp = np if isinstance(a, np.ndarray) else jnp
    return xp.broadcast_to(a[..., None], a.shape + (n,))


def _pad_rows(a, rows):
    return np.pad(a, ((0, rows - a.shape[0]),) + ((0, 0),) * (a.ndim - 1))


def kernel(x_prompt, x_sample, cache_mem_k, cache_mem_v, state_mlstm_conv, state_mlstm_C, state_mlstm_n, state_mlstm_m, state_ret_S, mem_prompt, w_in, b_gate, w_conv, b_conv, g_mix, g_mhead, g_rhead, w_out, g_xattn, g_mem, w_ck, w_cv, w_cq, w_co, g_ffn, w_gate, w_up, w_down, g_final):
    Bp, Tp, _ = x_prompt.shape
    Bs = x_sample.shape[0]
    l = 0
    n_m = 4 * MIX
    wi = w_in[l]
    w_gates = wi[:, n_m:n_m + 2 * HEADS]
    w_ret = wi[:, n_m + 2 * HEADS:]
    row = lambda a: a.reshape(1, -1)
    bf = lambda a: a.astype(BF16)
    g_mix_r, g_mh_r, g_rh_r = row(g_mix[l]), row(g_mhead[l]), row(g_rhead[l])
    g_x_r, g_mem_r, g_ffn_r, g_fin_r = row(g_xattn[l]), row(g_mem[l]), row(g_ffn[l]), row(g_final)
    b_conv_r = row(b_conv[l])
    w_out_b, w_cq_b, w_co_b = bf(w_out[l]), bf(w_cq[l]), bf(w_co[l])

    mk_b, mv_b, mk_rows, mv_rows = _memkv(mem_prompt, g_mem_r, bf(w_ck[l]), bf(w_cv[l]))
    wn = bf(jnp.concatenate([wi[:, :2 * MIX], w_ret[:, :2 * MIX]], axis=1))
    wt = bf(jnp.concatenate([wi[:, 2 * MIX:n_m], w_ret[:, 2 * MIX:]], axis=1).T)
    gate_rows = lambda a: jnp.concatenate(
        [a[:HEADS], jnp.zeros((8 - HEADS,) + a.shape[1:], F32),
         a[HEADS:], jnp.zeros((8 - HEADS,) + a.shape[1:], F32)], axis=0)
    wgt = bf(gate_rows(w_gates.T))
    bg_rows = gate_rows(b_gate[l][:, None])
    bgt = jnp.broadcast_to(bg_rows, (GT_ROWS, CHUNK))
    cos_p, sin_p = _rope_tables(np.arange(Tp))
    decay, q_dec, k_dec, chunk_dec = _retention_tables(CHUNK)
    x1_p, conv_p, C_p, n_p, m_p, S_p, w_gate_b, w_up_b, w_down_b = _mixer_prompt(
        x_prompt, wn, wt, wgt, w_out_b, g_mix_r, bgt, w_conv[l], b_conv_r,
        _lanes(g_mhead[l], CHUNK), _lanes(g_rhead[l], CHUNK), cos_p, sin_p,
        np.swapaxes(decay, 1, 2), _pad_rows(q_dec, 8), _lanes(k_dec, HEAD_DIM),
        _pad_rows(_lanes(chunk_dec, HEAD_DIM), 8), w_gate[l], w_up[l], w_down[l])

    xs = x_sample.reshape(Bs, D_MODEL)
    un_s, ut_s, gs_s = _inproj_sample(xs, g_mix_r, wn, wt, wgt)
    cos_s, sin_s = _rope_tables(PAST_LEN + np.arange(1))
    decay1, q_dec1, k_dec1, chunk_dec1 = _retention_tables(1)
    rtab = np.zeros((8, HEAD_DIM), np.float32)
    rtab[:4, :HEADS] = np.stack([q_dec1[:, 0], k_dec1[:, 0], chunk_dec1, decay1[:, 0, 0]])
    sample_mixer_operands = (
        un_s, ut_s, gs_s, jnp.transpose(state_mlstm_conv[l], (1, 0, 2)), state_mlstm_C[l],
        state_mlstm_n[l].reshape(Bs, MIX), state_mlstm_m[l], state_ret_S[l],
        bg_rows.reshape(1, GT_ROWS), w_conv[l], b_conv_r, g_mh_r, g_rh_r, cos_s, sin_s, rtab)

    o_p, (hcat_s, conv_s, C_s, n_s, m_s, S_s) = _attn_prompt(
        x1_p, mk_b, mv_b, w_cq_b, g_x_r, sample_mixer_operands)
    x1_s, q_s = _outq_sample(xs, hcat_s, w_out_b, w_cq_b, g_x_r)
    y_p, o_s = _ffn(x1_p.reshape(Bp * Tp, D_MODEL), o_p.reshape(Bp * Tp, D_MODEL),
                    w_co_b, w_gate_b, w_up_b, w_down_b, g_ffn_r, g_fin_r, tm=512,
                    sample=(q_s, _kv_rows(cache_mem_k[l]), _kv_rows(cache_mem_v[l])))
    y_s = _ffn(x1_s, o_s, w_co_b, w_gate_b, w_up_b, w_down_b, g_ffn_r, g_fin_r, tm=Bs)

    return (y_p.reshape(Bp, Tp, D_MODEL), y_s.reshape(Bs, 1, D_MODEL),
            _kv_from_rows(mk_rows), _kv_from_rows(mv_rows),
            conv_p[None], C_p[None], n_p[None], m_p[None, :, :HEADS, 0], S_p[None],
            jnp.transpose(conv_s, (1, 0, 2))[None], C_s[None],
            n_s.reshape(1, Bs, HEADS, HEAD_DIM), m_s[None], S_s[None])
```

```python
import functools

import jax
import jax.numpy as jnp
import numpy as np
from jax import lax
from jax.experimental import pallas as pl
from jax.experimental.pallas import tpu as pltpu

F32 = jnp.float32
BF16 = jnp.bfloat16

D_MODEL = 1024
HEADS = 4
HEAD_DIM = 128
MIX = HEADS * HEAD_DIM
CONV_W = 4
CHUNK = 128
N_MEM = 256
X_HEADS = 4
X_HEAD_DIM = 256
D_FF = 2816
ROPE_THETA = 10000.0
EPS = 1e-6
PAST_LEN = 16384
V7X_LANES = 128
X_HALVES = X_HEAD_DIM // V7X_LANES
KV_ROWS = N_MEM * X_HALVES * X_HEADS
QROWS = X_HALVES * X_HEADS

N_QK, N_RQ, N_RK, N_COLS = 0, 1024, 1536, 2048
T_V, T_O, T_RV, T_RG, T_ROWS = 0, 512, 1024, 1536, 2048
GT_ROWS = 16
UOFF = 8
ROW_GROUPS = 2

V7X_VMEM_LIMIT = 56 * 1024 * 1024

NT_DIMS = (((1,), (1,)), ((), ()))
TN_DIMS = (((0,), (0,)), ((), ()))


def _dot(a, b):
    return jnp.dot(a.astype(BF16), b.astype(BF16), preferred_element_type=F32)


def _dot_nt(a, b):
    return lax.dot_general(a.astype(BF16), b.astype(BF16), NT_DIMS, preferred_element_type=F32)


def _dot_tn(a, b):
    return lax.dot_general(a.astype(BF16), b.astype(BF16), TN_DIMS, preferred_element_type=F32)


def _rms(x, g):
    return x * lax.rsqrt(jnp.mean(x * x, axis=-1, keepdims=True) + EPS) * g


def _head_norm(h):
    return h * lax.rsqrt(jnp.mean(h * h, axis=-1, keepdims=True) + EPS)


def _silu(x):
    return x * jax.nn.sigmoid(x)


def _log_sigmoid(x):
    return jnp.minimum(x, 0.0) - jnp.log1p(jnp.exp(-jnp.abs(x)))


def _rope(x, cos_full, sin_signed):
    return x * cos_full + pltpu.roll(x, HEAD_DIM // 2, 1) * sin_signed


def _const_spec(shape):
    zeros = (0,) * len(shape)
    return pl.BlockSpec(shape, lambda *_: zeros, pipeline_mode=pl.Buffered(1))


def _params(*sem):
    return pltpu.CompilerParams(dimension_semantics=sem, vmem_limit_bytes=V7X_VMEM_LIMIT)


def _memkv_kernel(mem_ref, g_ref, wk_ref, wv_ref, kb_ref, vb_ref, krows_ref, vrows_ref):
    mn = _rms(mem_ref[...], g_ref[...]).astype(BF16)
    for w_ref, b_ref, rows_ref in ((wk_ref, kb_ref, krows_ref), (wv_ref, vb_ref, vrows_ref)):
        proj = jnp.dot(mn, w_ref[...], preferred_element_type=F32)
        b_ref[...] = proj.astype(BF16)
        for h in range(X_HEADS):
            for c in range(X_HALVES):
                lane0 = h * X_HEAD_DIM + c * V7X_LANES
                rows_ref[pl.ds(c * X_HEADS + h, N_MEM, stride=QROWS), :] = proj[:, lane0:lane0 + V7X_LANES]


def _memkv(mem, g_mem, wk, wv):
    B = mem.shape[0]
    tok = pl.BlockSpec((None, N_MEM, D_MODEL), lambda b: (b, 0, 0))
    rows = pl.BlockSpec((None, KV_ROWS, V7X_LANES), lambda b: (b, 0, 0))
    return pl.pallas_call(
        _memkv_kernel,
        grid=(B,),
        in_specs=[tok, _const_spec((1, D_MODEL)),
                  _const_spec((D_MODEL, D_MODEL)), _const_spec((D_MODEL, D_MODEL))],
        out_specs=[tok, tok, rows, rows],
        out_shape=[jax.ShapeDtypeStruct((B, N_MEM, D_MODEL), BF16)] * 2
        + [jax.ShapeDtypeStruct((B, KV_ROWS, V7X_LANES), F32)] * 2,
        compiler_params=_params("parallel"),
    )(mem, g_mem, wk, wv)


def _mixer_prompt_kernel(x_ref, xs_ref, wn_ref, wt_ref, wgt_ref, wout_ref, gmix_ref, bgt_ref,
                         wconv_ref, bconv_ref, gmh_ref, grh_ref, cos_ref, sin_ref,
                         decay_ref, qdec_ref, kdec_ref, cdec_ref, wg32_ref, wu32_ref, wd32_ref,
                         x1_ref, conv_ref, C_ref, n_ref, m_ref, S_ref, wg16_ref, wu16_ref, wd16_ref,
                         un2, ut2, gt2, tail_ref, st_ref, h_ref,
                         *, tt, tiles_per_seq):
    wg16_ref[...] = wg32_ref[...].astype(BF16)
    wu16_ref[...] = wu32_ref[...].astype(BF16)
    wd16_ref[...] = wd32_ref[...].astype(BF16)
    L = CHUNK
    s = pl.program_id(0)
    t = (jnp.maximum(s, 1) - 1) % tiles_per_seq
    chunks = range(tt // L)
    slot_of = lambda i: (un2.at[i], ut2.at[i], gt2.at[i])

    @pl.when(s == 0)
    def _():
        for ref in slot_of(1) + (tail_ref,):
            ref[...] = jnp.zeros_like(ref)

    @pl.when(t == 0)
    def _():
        C_ref[...] = jnp.zeros_like(C_ref)
        st_ref[...] = jnp.zeros_like(st_ref)
        n_ref[...] = jnp.zeros_like(n_ref)
        m_ref[...] = jnp.zeros_like(m_ref)

    k_scale = HEAD_DIM ** -0.5

    def project(slot, xn, part):
        un_ref, ut_ref, gt_ref = slot
        c0 = part * MIX
        un_ref[UOFF:UOFF + tt, c0:c0 + MIX] = jnp.dot(xn, wn_ref[:, c0:c0 + MIX],
                                                      preferred_element_type=F32)
        res = lax.dot_general(wt_ref[c0:c0 + MIX, :], xn, NT_DIMS, preferred_element_type=F32)
        for c in chunks:
            ut_ref[c, c0:c0 + MIX, :] = res[:, c * L:(c + 1) * L]
        if part == 0:
            gates_t = lax.dot_general(wgt_ref[...], xn, NT_DIMS, preferred_element_type=F32)
            for c in chunks:
                gt_ref[c] = gates_t[:, c * L:(c + 1) * L]

    src_id = lax.broadcasted_iota(jnp.int32, (L, L), 0)
    tgt_id = lax.broadcasted_iota(jnp.int32, (L, L), 1)
    causal = src_id <= tgt_id
    triu_bf = jnp.where(causal, 1.0, 0.0).astype(BF16)
    heads = range(HEADS)
    hcol = lambda base, h: slice(base + h * HEAD_DIM, base + (h + 1) * HEAD_DIM)

    def chunk_body(c, slot):
        un_ref, ut_ref, gt_ref = slot
        r0 = c * L
        rows = pl.ds(UOFF + r0, L)
        trows = pl.ds(r0, L)
        g_t = gt_ref[c] + bgt_ref[...]
        ig = g_t[0:8]
        lf = _log_sigmoid(g_t[8:16])
        lf_hi = lf.astype(BF16)
        r1 = lf - lf_hi.astype(F32)
        lf_mid = r1.astype(BF16)
        lf_lo = (r1 - lf_mid.astype(F32)).astype(BF16)
        bc = (jnp.dot(lf_hi, triu_bf, preferred_element_type=F32)
              + jnp.dot(lf_mid, triu_bf, preferred_element_type=F32)
              + jnp.dot(lf_lo, triu_bf, preferred_element_type=F32))
        m_prev = m_ref[...]
        inter = bc + m_prev
        b_last = bc[:, L - 1:L]
        g_w = b_last - bc + ig
        m_new = jnp.maximum(b_last + m_prev, jnp.max(g_w, axis=1, keepdims=True))
        ws = jnp.exp(g_w - m_new)
        carry = jnp.exp(b_last + m_prev - m_new)
        a_n = jnp.concatenate([ig - bc, jnp.zeros((L - 8, L), F32)], axis=0).T

        def conv_act(col):
            win = un_ref[pl.ds(r0, L + UOFF), col:col + HEAD_DIM]
            acc = bconv_ref[:, col:col + HEAD_DIM]
            for j in range(CONV_W):
                back = CONV_W - 1 - j
                tap = win if back == 0 else pltpu.roll(win, back, 0)
                acc = acc + tap[UOFF:UOFF + L] * wconv_ref[j:j + 1, col:col + HEAD_DIM]
            return _silu(acc)

        cos_f, sin_s = cos_ref[trows, :], sin_ref[trows, :]
        q = [conv_act(N_QK + h * HEAD_DIM) for h in heads]
        kb = [(conv_act(N_QK + MIX + h * HEAD_DIM) * k_scale).astype(BF16) for h in heads]
        qb = [a.astype(BF16) for a in q]
        rqb = [_rope(un_ref[rows, hcol(N_RQ, h)], cos_f, sin_s).astype(BF16) for h in heads]
        rk = [_rope(un_ref[rows, hcol(N_RK, h)], cos_f, sin_s) * k_scale for h in heads]
        v_t = [ut_ref[c, hcol(T_V, h), :] for h in heads]
        rvb = [ut_ref[c, hcol(T_RV, h), :].astype(BF16) for h in heads]
        c_old = [C_ref[h] for h in heads]
        s_old = [st_ref[h] for h in heads]
        n_old = [n_ref[h:h + 1, :] for h in heads]
        qk = [_dot_nt(kb[h], qb[h]) for h in heads]
        att = [_dot_nt(rk[h], rqb[h]) for h in heads]
        c_q = [_dot_nt(c_old[h], qb[h]) for h in heads]
        s_q = [_dot_nt(s_old[h], rqb[h]) for h in heads]
        n_q = [_dot_nt(jnp.broadcast_to(n_old[h], (8, HEAD_DIM)), qb[h])[0:1] for h in heads]
        d_c = [_dot(v_t[h] * ws[h:h + 1, :], kb[h]) for h in heads]
        d_n = [_dot(jnp.broadcast_to(ws[h:h + 1, :], (8, L)), kb[h])[0:1] for h in heads]
        d_s = [_dot(rvb[h], rk[h] * kdec_ref[h]) for h in heads]
        m_t, w_in, wts = [], [], []
        for h in heads:
            dmat = a_n[:, h:h + 1] + bc[h:h + 1, :]
            dmat = jnp.where(causal, dmat, -jnp.inf)
            m_t.append(jnp.maximum(inter[h:h + 1, :], jnp.max(dmat, axis=0, keepdims=True)))
            wts.append(jnp.exp(dmat - m_t[h]) * qk[h])
            w_in.append(jnp.exp(inter[h:h + 1, :] - m_t[h]))
        att_w = [att[h] * decay_ref[h] for h in heads]
        v_p = [_dot(v_t[h], wts[h]) for h in heads]
        v_a = [_dot(rvb[h], att_w[h]) for h in heads]
        for h in heads:
            num = v_p[h] + w_in[h] * c_q[h]
            den = jnp.sum(wts[h], axis=0, keepdims=True) + w_in[h] * n_q[h]
            hm = num / jnp.maximum(jnp.abs(den), jnp.exp(-m_t[h]))
            hm = hm * lax.rsqrt(jnp.mean(hm * hm, axis=0, keepdims=True) + EPS)
            hm = hm * gmh_ref[hcol(0, h), :] * jax.nn.sigmoid(ut_ref[c, hcol(T_O, h), :])
            h_ref[trows, hcol(0, h)] = hm.T.astype(BF16)
            carry_h = carry[h:h + 1, :]
            C_ref[h] = carry_h * c_old[h] + d_c[h]
            n_ref[h:h + 1, :] = carry_h * n_old[h] + d_n[h]
            o = v_a[h] + qdec_ref[h:h + 1, :] * s_q[h]
            st_ref[h] = cdec_ref[h:h + 1, :] * s_old[h] + d_s[h]
            hr = o * lax.rsqrt(jnp.mean(o * o, axis=0, keepdims=True) + EPS)
            hr = hr * grh_ref[hcol(0, h), :] * _silu(ut_ref[c, hcol(T_RG, h), :])
            h_ref[trows, hcol(MIX, h)] = hr.T.astype(BF16)

        m_ref[...] = m_new

    def step(proj_slot, scan_slot):
        un_ref = scan_slot[0]
        un_ref[0:UOFF, N_QK:N_RQ] = jnp.where(t == 0, 0.0, tail_ref[...])
        xn = _rms(x_ref[...], gmix_ref[...]).astype(BF16)
        for c in chunks:
            chunk_body(c, scan_slot)
            project(proj_slot, xn, c)
        tail_ref[...] = un_ref[tt:tt + UOFF, N_QK:N_RQ]
        x1_ref[...] = xs_ref[...] + jnp.dot(h_ref[...], wout_ref[...], preferred_element_type=F32)

    step(slot_of(s % 2), slot_of(1 - s % 2))

    @pl.when(t == tiles_per_seq - 1)
    def _():
        conv_ref[...] = tail_ref[UOFF - (CONV_W - 1):UOFF, :]
        for h in heads:
            S_ref[h] = st_ref[h].T


def _mixer_prompt(x, wn, wt, wgt, w_out, g_mix, bgt, w_conv, b_conv, gmh_cols, grh_cols, cos_f, sin_s,
                  decay_t, qdec_rows, kdec_cols, cdec_rows, w_gate, w_up, w_down, tt=512):
    B, T, _ = x.shape
    tps = T // tt
    n_tiles = B * tps
    proj = lambda s: jnp.minimum(s, n_tiles - 1)
    scan = lambda s: jnp.maximum(s, 1) - 1
    per_b3 = lambda s: (scan(s) // tps, 0, 0)
    per_b4 = lambda s: (scan(s) // tps, 0, 0, 0)
    scan_tile = pl.BlockSpec((None, tt, D_MODEL), lambda s: (scan(s) // tps, scan(s) % tps, 0))
    in_specs = [
        pl.BlockSpec((None, tt, D_MODEL), lambda s: (proj(s) // tps, proj(s) % tps, 0)),
        scan_tile,
        _const_spec((D_MODEL, N_COLS)), _const_spec((T_ROWS, D_MODEL)), _const_spec((GT_ROWS, D_MODEL)),
        _const_spec((D_MODEL, D_MODEL)),
        _const_spec((1, D_MODEL)), _const_spec((GT_ROWS, CHUNK)),
        _const_spec((CONV_W, 2 * MIX)), _const_spec((1, 2 * MIX)),
        _const_spec((MIX, CHUNK)), _const_spec((MIX, CHUNK)),
        pl.BlockSpec((tt, HEAD_DIM), lambda s: (scan(s) % tps, 0)),
        pl.BlockSpec((tt, HEAD_DIM), lambda s: (scan(s) % tps, 0)),
        _const_spec((HEADS, CHUNK, CHUNK)),
        _const_spec((8, CHUNK)), _const_spec((HEADS, CHUNK, HEAD_DIM)), _const_spec((8, HEAD_DIM)),
    ]
    ff_blocks = D_FF // HEAD_DIM
    assert ff_blocks <= n_tiles + 1
    early = lambda s: jnp.minimum(s, ff_blocks - 1)
    late = lambda s: jnp.maximum(s - (n_tiles + 1 - ff_blocks), 0)
    cast_specs = [pl.BlockSpec((D_MODEL, HEAD_DIM), lambda s: (0, early(s))),
                  pl.BlockSpec((D_MODEL, HEAD_DIM), lambda s: (0, late(s))),
                  pl.BlockSpec((HEAD_DIM, D_MODEL), lambda s: (late(s), 0))]
    in_specs += cast_specs
    out_specs = [
        scan_tile,
        pl.BlockSpec((None, CONV_W - 1, 2 * MIX), per_b3),
        pl.BlockSpec((None, HEADS, HEAD_DIM, HEAD_DIM), per_b4),
        pl.BlockSpec((None, HEADS, HEAD_DIM), per_b3),
        pl.BlockSpec((None, 8, CHUNK), per_b3),
        pl.BlockSpec((None, HEADS, HEAD_DIM, HEAD_DIM), per_b4),
    ] + cast_specs
    out_shape = [
        jax.ShapeDtypeStruct((B, T, D_MODEL), F32),
        jax.ShapeDtypeStruct((B, CONV_W - 1, 2 * MIX), F32),
        jax.ShapeDtypeStruct((B, HEADS, HEAD_DIM, HEAD_DIM), F32),
        jax.ShapeDtypeStruct((B, HEADS, HEAD_DIM), F32),
        jax.ShapeDtypeStruct((B, 8, CHUNK), F32),
        jax.ShapeDtypeStruct((B, HEADS, HEAD_DIM, HEAD_DIM), F32),
        jax.ShapeDtypeStruct((D_MODEL, D_FF), BF16), jax.ShapeDtypeStruct((D_MODEL, D_FF), BF16),
        jax.ShapeDtypeStruct((D_FF, D_MODEL), BF16),
    ]
    n_chunks = tt // CHUNK
    scratch = [
        pltpu.VMEM((2, tt + UOFF, N_COLS), F32),
        pltpu.VMEM((2, n_chunks, T_ROWS, CHUNK), F32),
        pltpu.VMEM((2, n_chunks, GT_ROWS, CHUNK), F32),
    ] + [
        pltpu.VMEM((UOFF, 2 * MIX), F32),
        pltpu.VMEM((HEADS, HEAD_DIM, HEAD_DIM), F32),
        pltpu.VMEM((tt, D_MODEL), BF16),
    ]
    return pl.pallas_call(
        functools.partial(_mixer_prompt_kernel, tt=tt, tiles_per_seq=tps),
        grid=(n_tiles + 1,),
        in_specs=in_specs, out_specs=out_specs, out_shape=out_shape,
        scratch_shapes=scratch,
        compiler_params=_params("arbitrary"),
    )(x, x, wn, wt, wgt, w_out, g_mix, bgt, w_conv, b_conv, gmh_cols, grh_cols, cos_f, sin_s,
      decay_t, qdec_rows, kdec_cols, cdec_rows, w_gate, w_up, w_down)


def _attn_prompt_kernel(x1_ref, k_ref, v_ref, wcq_ref, gx_ref, *rest, n_sample_steps):
    sample_in, o_ref = rest[:SAMPLE_MIXER_INPUTS], rest[SAMPLE_MIXER_INPUTS]
    sample_out_and_scratch = rest[SAMPLE_MIXER_INPUTS + 1:]
    step = pl.program_id(0) * pl.num_programs(1) + pl.program_id(1)

    @pl.when(step < n_sample_steps)
    def _():
        _mixer_sample_kernel(*sample_in, *sample_out_and_scratch, bb=SAMPLE_BLOCK)

    tm = x1_ref.shape[0]
    rows = [slice(r, r + tm // ROW_GROUPS) for r in range(0, tm, tm // ROW_GROUPS)]
    xq = [_rms(x1_ref[r, :], gx_ref[...]).astype(BF16) for r in rows]
    q = [jnp.dot(a, wcq_ref[...], preferred_element_type=F32).astype(BF16) for a in xq]
    sl = [slice(h * X_HEAD_DIM, (h + 1) * X_HEAD_DIM) for h in range(X_HEADS)]
    items = [(g, h) for g in range(ROW_GROUPS) for h in range(X_HEADS)]
    s = [_dot_nt(q[g][:, sl[h]], k_ref[:, sl[h]]) * (X_HEAD_DIM ** -0.5) for g, h in items]
    e = [jnp.exp(a - jnp.max(a, axis=-1, keepdims=True)) for a in s]
    p = [a / jnp.sum(a, axis=-1, keepdims=True) for a in e]
    for (g, h), a in zip(items, p):
        o_ref[rows[g], sl[h]] = _dot(a, v_ref[:, sl[h]]).astype(BF16)


def _attn_prompt(x1, mk, mv, w_cq, g_x, sample_mixer_operands, tm=1024):
    B, T, _ = x1.shape
    tps = T // tm
    Bs = sample_mixer_operands[0].shape[0]
    n_sample_steps = Bs // SAMPLE_BLOCK
    assert n_sample_steps <= B * tps
    block_of = lambda b, t: jnp.minimum(b * tps + t, n_sample_steps - 1)
    s_in, s_out, s_shape, s_scratch = _sample_mixer_specs(Bs, block_of)
    tile = pl.BlockSpec((None, tm, D_MODEL), lambda b, t: (b, t, 0))
    kv = pl.BlockSpec((None, N_MEM, D_MODEL), lambda b, t: (b, 0, 0))
    outs = pl.pallas_call(
        functools.partial(_attn_prompt_kernel, n_sample_steps=n_sample_steps),
        grid=(B, tps),
        in_specs=[tile, kv, kv, _const_spec((D_MODEL, D_MODEL)), _const_spec((1, D_MODEL))] + s_in,
        out_specs=[tile] + s_out,
        out_shape=[jax.ShapeDtypeStruct((B, T, D_MODEL), BF16)] + s_shape,
        scratch_shapes=s_scratch,
        compiler_params=_params("arbitrary", "arbitrary"),
    )(x1, mk, mv, w_cq, g_x, *sample_mixer_operands)
    return outs[0], outs[1:]


V7X_MXU_DIM = 256
FF_CHUNKS = ((0, 6 * V7X_MXU_DIM), (6 * V7X_MXU_DIM, D_FF))


def _ffn_kernel(x1_ref, o_ref, wco_ref, wg_ref, wu_ref, wd_ref, gffn_ref, gfin_ref, *rest, n_groups):
    if len(rest) == 1:
        (y_ref,) = rest
        sample_stages = iter(())
    else:
        qs_ref, ck_ref, cv_ref, y_ref, os_ref = rest
        sample_stages = _sample_attention(qs_ref, ck_ref, cv_ref, os_ref)
    tm = x1_ref.shape[0]
    rows = [slice(r, r + tm // n_groups) for r in range(0, tm, tm // n_groups)]
    acc = [x1_ref[r, :] + _dot(o_ref[r, :], wco_ref[...]) for r in rows]
    hf = [_rms(a, gffn_ref[...]).astype(BF16) for a in acc]
    for c0, c1 in FF_CHUNKS:
        next(sample_stages, None)
        gate = [jnp.dot(h, wg_ref[:, c0:c1], preferred_element_type=F32) for h in hf]
        up = [jnp.dot(h, wu_ref[:, c0:c1], preferred_element_type=F32) for h in hf]
        act = [_silu(g) * u for g, u in zip(gate, up)]
        acc = [a + _dot(p, wd_ref[c0:c1, :]) for a, p in zip(acc, act)]
    next(sample_stages, None)
    for r, a in zip(rows, acc):
        y_ref[r, :] = _rms(a, gfin_ref[...])


def _ffn(x1, o, w_co, w_gate, w_up, w_down, g_ffn, g_final, tm, sample=None):
    rows = x1.shape[0]
    steps = rows // tm
    row_spec = pl.BlockSpec((tm, D_MODEL), lambda i: (i, 0))
    in_specs = [row_spec, row_spec, _const_spec((D_MODEL, D_MODEL)),
                _const_spec((D_MODEL, D_FF)), _const_spec((D_MODEL, D_FF)),
                _const_spec((D_FF, D_MODEL)), _const_spec((1, D_MODEL)), _const_spec((1, D_MODEL))]
    out_specs = [row_spec]
    out_shape = [jax.ShapeDtypeStruct((rows, D_MODEL), F32)]
    operands = [x1, o, w_co, w_gate, w_up, w_down, g_ffn, g_final]
    if sample is not None:
        q_s, ck_s, cv_s = sample
        Bs = q_s.shape[0]
        ba = Bs // steps
        assert ba * steps == Bs
        s_rows = pl.BlockSpec((ba, QROWS, V7X_LANES), lambda i: (i, 0, 0))
        s_kv = pl.BlockSpec((ba, KV_ROWS, V7X_LANES), lambda i: (i, 0, 0))
        in_specs += [s_rows, s_kv, s_kv]
        out_specs += [s_rows]
        out_shape += [jax.ShapeDtypeStruct((Bs, QROWS, V7X_LANES), F32)]
        operands += [q_s.reshape(Bs, QROWS, V7X_LANES), ck_s, cv_s]
    outs = pl.pallas_call(
        functools.partial(_ffn_kernel, n_groups=ROW_GROUPS if tm >= 512 else 1),
        grid=(steps,),
        in_specs=in_specs, out_specs=out_specs, out_shape=out_shape,
        compiler_params=_params("parallel"),
    )(*operands)
    if sample is None:
        return outs[0]
    return outs[0], outs[1].reshape(Bs, D_MODEL)


def _inproj_sample_kernel(x_ref, g_ref, wn_ref, wt_ref, wgt_ref, un_ref, ut_ref, gs_ref):
    xn = _rms(x_ref[...], g_ref[...]).astype(BF16)
    un_ref[...] = jnp.dot(xn, wn_ref[...], preferred_element_type=F32)
    ut_ref[...] = lax.dot_general(xn, wt_ref[...], NT_DIMS, preferred_element_type=F32)
    gs_ref[...] = lax.dot_general(xn, wgt_ref[...], NT_DIMS, preferred_element_type=F32)


def _inproj_sample(x, g_mix, wn, wt, wgt):
    rows = x.shape[0]
    full = lambda cols: pl.BlockSpec((rows, cols), lambda i: (0, 0))
    return pl.pallas_call(
        _inproj_sample_kernel,
        grid=(1,),
        in_specs=[_const_spec((rows, D_MODEL)), _const_spec((1, D_MODEL)),
                  _const_spec((D_MODEL, N_COLS)), _const_spec((T_ROWS, D_MODEL)),
                  _const_spec((GT_ROWS, D_MODEL))],
        out_specs=[full(N_COLS), full(T_ROWS), full(GT_ROWS)],
        out_shape=[jax.ShapeDtypeStruct((rows, N_COLS), F32),
                   jax.ShapeDtypeStruct((rows, T_ROWS), F32),
                   jax.ShapeDtypeStruct((rows, GT_ROWS), F32)],
        compiler_params=_params("arbitrary"),
    )(x, g_mix, wn, wt, wgt)


def _mixer_sample_kernel(un_ref, ut_ref, gs_ref, conv_ref, C_ref, n_ref, m_ref, S_ref, bg_ref,
                         wconv_ref, bconv_ref, gmh_ref, grh_ref, cos_ref, sin_ref, rtab_ref,
                         h_ref, convo_ref, Co_ref, no_ref, mo_ref, So_ref,
                         q_s, k_s, vws_s, qr_s, kr_s, qc_s, qs_s, carry_s, *, bb):
    k_scale = HEAD_DIM ** -0.5
    uqk = un_ref[:, N_QK:N_RQ]
    conv = (bconv_ref[...] + wconv_ref[0:1, :] * conv_ref[0] + wconv_ref[1:2, :] * conv_ref[1]
            + wconv_ref[2:3, :] * conv_ref[2] + wconv_ref[3:4, :] * uqk)
    convo_ref[0] = conv_ref[1]
    convo_ref[1] = conv_ref[2]
    convo_ref[2] = uqk
    qk_act = _silu(conv)
    q_s[...] = qk_act[:, 0:MIX]
    k_s[...] = qk_act[:, MIX:2 * MIX] * k_scale

    gates = gs_ref[...] + bg_ref[...]
    ig = gates[:, 0:HEADS]
    lf = _log_sigmoid(gates[:, 8:8 + HEADS])
    inter = lf + m_ref[...]
    m_t = jnp.maximum(inter, ig)
    ws = jnp.exp(ig - m_t)
    w_in = jnp.exp(inter - m_t)
    mo_ref[...] = m_t
    carry_s[:, 0:HEADS] = w_in

    cos_f = cos_ref[...]
    sin_s = sin_ref[...]
    for h in range(HEADS):
        lo = h * HEAD_DIM
        hs = slice(lo, lo + HEAD_DIM)
        vws_s[:, hs] = ut_ref[:, T_V + lo:T_V + lo + HEAD_DIM] * ws[:, h:h + 1]
        qr_s[:, hs] = _rope(un_ref[:, N_RQ + lo:N_RQ + lo + HEAD_DIM], cos_f, sin_s)
        kr_s[:, hs] = _rope(un_ref[:, N_RK + lo:N_RK + lo + HEAD_DIM], cos_f, sin_s) * k_scale

    heads = range(HEADS)
    seqs = range(bb)
    hsl = [slice(h * HEAD_DIM, (h + 1) * HEAD_DIM) for h in heads]
    seq_id = lax.broadcasted_iota(jnp.int32, (bb, HEAD_DIM), 0)
    for h in heads:
        q_h, k_h, vws_h = q_s[:, hsl[h]], k_s[:, hsl[h]], vws_s[:, hsl[h]]
        qr_h = qr_s[:, hsl[h]]
        krd_h = kr_s[:, hsl[h]] * rtab_ref[1:2, h:h + 1]
        rv_h = ut_ref[:, T_RV + h * HEAD_DIM:T_RV + (h + 1) * HEAD_DIM]
        c_old = [C_ref[b, h] for b in seqs]
        s_old = [S_ref[b, h] for b in seqs]
        q_c = [_dot_nt(q_h, c_old[b]) for b in seqs]
        q_st = [_dot(qr_h, s_old[b]) for b in seqs]
        d_c = [_dot_tn(jnp.where(seq_id == b, vws_h, 0.0), k_h) for b in seqs]
        d_s = [_dot_tn(jnp.where(seq_id == b, krd_h, 0.0), rv_h) for b in seqs]
        for b in seqs:
            row = slice(b, b + 1)
            qc_s[row, hsl[h]] = q_c[b][row]
            qs_s[row, hsl[h]] = q_st[b][row]
            Co_ref[b, h] = carry_s[row, h:h + 1] * c_old[b] + d_c[b]
            So_ref[b, h] = rtab_ref[2:3, h:h + 1] * s_old[b] + d_s[b]

    for h in range(HEADS):
        lo = h * HEAD_DIM
        hs = slice(lo, lo + HEAD_DIM)
        q_h, k_h, n_h = q_s[:, hs], k_s[:, hs], n_ref[:, hs]
        ws_h, w_in_h = ws[:, h:h + 1], w_in[:, h:h + 1]
        wts = ws_h * jnp.sum(q_h * k_h, axis=1, keepdims=True)
        num = wts * ut_ref[:, T_V + lo:T_V + lo + HEAD_DIM] + w_in_h * qc_s[:, hs]
        den = wts + w_in_h * jnp.sum(n_h * q_h, axis=1, keepdims=True)
        hm = num / jnp.maximum(jnp.abs(den), jnp.exp(-m_t[:, h:h + 1]))
        mo = ut_ref[:, T_O + lo:T_O + lo + HEAD_DIM]
        h_ref[:, hs] = _head_norm(hm) * gmh_ref[:, hs] * jax.nn.sigmoid(mo)
        no_ref[:, hs] = w_in_h * n_h + ws_h * k_h

        att = jnp.sum(qr_s[:, hs] * kr_s[:, hs], axis=1, keepdims=True) * rtab_ref[3:4, h:h + 1]
        o = (att * ut_ref[:, T_RV + lo:T_RV + lo + HEAD_DIM]
             + rtab_ref[0:1, h:h + 1] * qs_s[:, hs])
        rg = ut_ref[:, T_RG + lo:T_RG + lo + HEAD_DIM]
        h_ref[:, MIX + lo:MIX + lo + HEAD_DIM] = _head_norm(o) * grh_ref[:, hs] * _silu(rg)


SAMPLE_BLOCK = 8
SAMPLE_MIXER_INPUTS = 16


def _sample_mixer_specs(B, block_of):
    bb = SAMPLE_BLOCK
    rows = lambda cols: pl.BlockSpec((bb, cols), lambda *g: (block_of(*g), 0))
    mats = pl.BlockSpec((bb, HEADS, HEAD_DIM, HEAD_DIM), lambda *g: (block_of(*g), 0, 0, 0))
    convs = pl.BlockSpec((CONV_W - 1, bb, 2 * MIX), lambda *g: (0, block_of(*g), 0))
    in_specs = [rows(N_COLS), rows(T_ROWS), rows(GT_ROWS), convs, mats, rows(MIX), rows(HEADS), mats,
                _const_spec((1, GT_ROWS)), _const_spec((CONV_W, 2 * MIX)), _const_spec((1, 2 * MIX)),
                _const_spec((1, MIX)), _const_spec((1, MIX)),
                _const_spec((1, HEAD_DIM)), _const_spec((1, HEAD_DIM)), _const_spec((8, HEAD_DIM))]
    assert len(in_specs) == SAMPLE_MIXER_INPUTS
    out_specs = [rows(D_MODEL), convs, mats, rows(MIX), rows(HEADS), mats]
    out_shape = [jax.ShapeDtypeStruct((B, D_MODEL), F32),
                 jax.ShapeDtypeStruct((CONV_W - 1, B, 2 * MIX), F32),
                 jax.ShapeDtypeStruct((B, HEADS, HEAD_DIM, HEAD_DIM), F32),
                 jax.ShapeDtypeStruct((B, MIX), F32),
                 jax.ShapeDtypeStruct((B, HEADS), F32),
                 jax.ShapeDtypeStruct((B, HEADS, HEAD_DIM, HEAD_DIM), F32)]
    scratch = [pltpu.VMEM((bb, MIX), F32) for _ in range(7)] + [pltpu.VMEM((bb, HEAD_DIM), F32)]
    return in_specs, out_specs, out_shape, scratch


def _outq_sample_kernel(x_ref, h_ref, wout_ref, wcq_ref, gx_ref, x1_ref, q_ref):
    x1 = x_ref[...] + _dot(h_ref[...], wout_ref[...])
    x1_ref[...] = x1
    q_ref[...] = _dot(_rms(x1, gx_ref[...]), wcq_ref[...])


def _outq_sample(x, hcat, w_out, w_cq, g_x):
    rows = x.shape[0]
    full = _const_spec((rows, D_MODEL))
    return pl.pallas_call(
        _outq_sample_kernel,
        grid=(1,),
        in_specs=[full, full, _const_spec((D_MODEL, D_MODEL)), _const_spec((D_MODEL, D_MODEL)),
                  _const_spec((1, D_MODEL))],
        out_specs=[pl.BlockSpec((rows, D_MODEL), lambda i: (0, 0))] * 2,
        out_shape=[jax.ShapeDtypeStruct((rows, D_MODEL), F32)] * 2,
        compiler_params=_params("arbitrary"),
    )(x, hcat, w_out, w_cq, g_x)


def _sample_attention(q_ref, k_ref, v_ref, o_ref):
    assert X_HALVES == 2
    r_id = lax.broadcasted_iota(jnp.int32, (QROWS, KV_ROWS), 0)
    n_id = lax.broadcasted_iota(jnp.int32, (QROWS, KV_ROWS), 1)
    own = (n_id % QROWS) == (r_id // X_HALVES) + X_HEADS * (r_id % X_HALVES)
    low_half = (lax.broadcasted_iota(jnp.int32, (1, KV_ROWS), 1) % QROWS) < X_HEADS
    seqs = range(q_ref.shape[0])
    z = [_dot_nt(q_ref[j], k_ref[j]) for j in seqs]
    zc = [jnp.sum(jnp.where(own, z[j], 0.0), axis=0, keepdims=True) for j in seqs]
    yield
    other = [jnp.where(low_half, pltpu.roll(zc[j], KV_ROWS - X_HEADS, 1),
                       pltpu.roll(zc[j], X_HEADS, 1)) for j in seqs]
    s = [jnp.where(own, (zc[j] + other[j]) * (X_HEAD_DIM ** -0.5), -jnp.inf) for j in seqs]
    m = [jnp.max(s[j], axis=-1, keepdims=True) for j in seqs]
    e = [jnp.exp(s[j] - m[j]) for j in seqs]
    den = [jnp.sum(e[j], axis=-1, keepdims=True) for j in seqs]
    yield
    for j in seqs:
        o_ref[j] = _dot(e[j] / den[j], v_ref[j])
    yield


def _kv_rows(cache):
    B = cache.shape[0]
    c5 = cache.reshape(B, N_MEM, X_HEADS, X_HALVES, V7X_LANES)
    return jnp.transpose(c5, (0, 1, 3, 2, 4)).reshape(B, KV_ROWS, V7X_LANES)


def _kv_from_rows(rows):
    B = rows.shape[0]
    r5 = rows.reshape(B, N_MEM, X_HALVES, X_HEADS, V7X_LANES)
    return jnp.transpose(r5, (0, 1, 3, 2, 4)).reshape(1, B, N_MEM, X_HEADS, X_HEAD_DIM)


def _rope_tables(pos):
    half = HEAD_DIM // 2
    inv = ROPE_THETA ** (-np.arange(half, dtype=np.float64) / half)
    ang = np.asarray(pos, np.float64)[:, None] * inv[None, :]
    cos, sin = np.cos(ang), np.sin(ang)
    return (np.concatenate([cos, cos], axis=-1).astype(np.float32),
            np.concatenate([-sin, sin], axis=-1).astype(np.float32))


def _retention_tables(L):
    lg = np.log1p(-np.exp2(-5.0 - np.arange(HEADS, dtype=np.float64)))
    t = np.arange(L, dtype=np.float64)
    diff = t[:, None] - t[None, :]
    decay = np.where(diff >= 0, np.exp(lg[:, None, None] * np.maximum(diff, 0.0)), 0.0)
    q_dec = np.exp(lg[:, None] * (t + 1.0))
    k_dec = np.exp(lg[:, None] * (L - 1.0 - t))
    chunk_dec = np.exp(lg * L)
    return tuple(a.astype(np.float32) for a in (decay, q_dec, k_dec, chunk_dec))


def _lanes(a, n):
    xp = np if isinstance(a, np.ndarray) else jnp
    return xp.broadcast_to(a[..., None], a.shape + (n,))


def _pad_rows(a, rows):
    return np.pad(a, ((0, rows - a.shape[0]),) + ((0, 0),) * (a.ndim - 1))


def kernel(x_prompt, x_sample, cache_mem_k, cache_mem_v, state_mlstm_conv, state_mlstm_C, state_mlstm_n, state_mlstm_m, state_ret_S, mem_prompt, w_in, b_gate, w_conv, b_conv, g_mix, g_mhead, g_rhead, w_out, g_xattn, g_mem, w_ck, w_cv, w_cq, w_co, g_ffn, w_gate, w_up, w_down, g_final):
    Bp, Tp, _ = x_prompt.shape
    Bs = x_sample.shape[0]
    l = 0
    n_m = 4 * MIX
    wi = w_in[l]
    w_gates = wi[:, n_m:n_m + 2 * HEADS]
    w_ret = wi[:, n_m + 2 * HEADS:]
    row = lambda a: a.reshape(1, -1)
    bf = lambda a: a.astype(BF16)
    g_mix_r, g_mh_r, g_rh_r = row(g_mix[l]), row(g_mhead[l]), row(g_rhead[l])
    g_x_r, g_mem_r, g_ffn_r, g_fin_r = row(g_xattn[l]), row(g_mem[l]), row(g_ffn[l]), row(g_final)
    b_conv_r = row(b_conv[l])
    w_out_b, w_cq_b, w_co_b = bf(w_out[l]), bf(w_cq[l]), bf(w_co[l])

    mk_b, mv_b, mk_rows, mv_rows = _memkv(mem_prompt, g_mem_r, bf(w_ck[l]), bf(w_cv[l]))
    wn = bf(jnp.concatenate([wi[:, :2 * MIX], w_ret[:, :2 * MIX]], axis=1))
    wt = bf(jnp.concatenate([wi[:, 2 * MIX:n_m], w_ret[:, 2 * MIX:]], axis=1).T)
    gate_rows = lambda a: jnp.concatenate(
        [a[:HEADS], jnp.zeros((8 - HEADS,) + a.shape[1:], F32),
         a[HEADS:], jnp.zeros((8 - HEADS,) + a.shape[1:], F32)], axis=0)
    wgt = bf(gate_rows(w_gates.T))
    bg_rows = gate_rows(b_gate[l][:, None])
    bgt = jnp.broadcast_to(bg_rows, (GT_ROWS, CHUNK))
    cos_p, sin_p = _rope_tables(np.arange(Tp))
    decay, q_dec, k_dec, chunk_dec = _retention_tables(CHUNK)
    x1_p, conv_p, C_p, n_p, m_p, S_p, w_gate_b, w_up_b, w_down_b = _mixer_prompt(
        x_prompt, wn, wt, wgt, w_out_b, g_mix_r, bgt, w_conv[l], b_conv_r,
        _lanes(g_mhead[l], CHUNK), _lanes(g_rhead[l], CHUNK), cos_p, sin_p,
        np.swapaxes(decay, 1, 2), _pad_rows(q_dec, 8), _lanes(k_dec, HEAD_DIM),
        _pad_rows(_lanes(chunk_dec, HEAD_DIM), 8), w_gate[l], w_up[l], w_down[l])

    xs = x_sample.reshape(Bs, D_MODEL)
    un_s, ut_s, gs_s = _inproj_sample(xs, g_mix_r, wn, wt, wgt)
    cos_s, sin_s = _rope_tables(PAST_LEN + np.arange(1))
    decay1, q_dec1, k_dec1, chunk_dec1 = _retention_tables(1)
    rtab = np.zeros((8, HEAD_DIM), np.float32)
    rtab[:4, :HEADS] = np.stack([q_dec1[:, 0], k_dec1[:, 0], chunk_dec1, decay1[:, 0, 0]])
    sample_mixer_operands = (
        un_s, ut_s, gs_s, jnp.transpose(state_mlstm_conv[l], (1, 0, 2)), state_mlstm_C[l],
        state_mlstm_n[l].reshape(Bs, MIX), state_mlstm_m[l], state_ret_S[l],
        bg_rows.reshape(1, GT_ROWS), w_conv[l], b_conv_r, g_mh_r, g_rh_r, cos_s, sin_s, rtab)

    o_p, (hcat_s, conv_s, C_s, n_s, m_s, S_s) = _attn_prompt(
        x1_p, mk_b, mv_b, w_cq_b, g_x_r, sample_mixer_operands)
    x1_s, q_s = _outq_sample(xs, hcat_s, w_out_b, w_cq_b, g_x_r)
    y_p, o_s = _ffn(x1_p.reshape(Bp * Tp, D_MODEL), o_p.reshape(Bp * Tp, D_MODEL),
                    w_co_b, w_gate_b, w_up_b, w_down_b, g_ffn_r, g_fin_r, tm=512,
                    sample=(q_s, _kv_rows(cache_mem_k[l]), _kv_rows(cache_mem_v[l])))
    y_s = _ffn(x1_s, o_s, w_co_b, w_gate_b, w_up_b, w_down_b, g_ffn_r, g_fin_r, tm=Bs)

    return (y_p.reshape(Bp, Tp, D_MODEL), y_s.reshape(Bs, 1, D_MODEL),
            _kv_from_rows(mk_rows), _kv_from_rows(mv_rows),
            conv_p[None], C_p[None], n_p[None], m_p[None, :, :HEADS, 0], S_p[None],
            jnp.transpose(conv_s, (1, 0, 2))[None], C_s[None],
            n_s.reshape(1, Bs, HEADS, HEAD_DIM), m_s[None], S_s[None])
```

```python
import functools

import jax
import jax.numpy as jnp
import numpy as np
from jax import lax
from jax.experimental import pallas as pl
from jax.experimental.pallas import tpu as pltpu

F32 = jnp.float32
BF16 = jnp.bfloat16

D_MODEL = 1024
HEADS = 4
HEAD_DIM = 128
MIX = HEADS * HEAD_DIM
CONV_W = 4
CHUNK = 128
N_MEM = 256
X_HEADS = 4
X_HEAD_DIM = 256
D_FF = 2816
ROPE_THETA = 10000.0
EPS = 1e-6
PAST_LEN = 16384
V7X_LANES = 128
X_HALVES = X_HEAD_DIM // V7X_LANES
KV_ROWS = N_MEM * X_HALVES * X_HEADS
QROWS = X_HALVES * X_HEADS

N_QK, N_RQ, N_RK, N_COLS = 0, 1024, 1536, 2048
T_V, T_O, T_RV, T_RG, T_ROWS = 0, 512, 1024, 1536, 2048
GT_ROWS = 16
UOFF = 8
ROW_GROUPS = 2

V7X_VMEM_LIMIT = 56 * 1024 * 1024

NT_DIMS = (((1,), (1,)), ((), ()))
TN_DIMS = (((0,), (0,)), ((), ()))


def _dot(a, b):
    return jnp.dot(a.astype(BF16), b.astype(BF16), preferred_element_type=F32)


def _dot_nt(a, b):
    return lax.dot_general(a.astype(BF16), b.astype(BF16), NT_DIMS, preferred_element_type=F32)


def _dot_tn(a, b):
    return lax.dot_general(a.astype(BF16), b.astype(BF16), TN_DIMS, preferred_element_type=F32)


def _rms(x, g):
    return x * lax.rsqrt(jnp.mean(x * x, axis=-1, keepdims=True) + EPS) * g


def _head_norm(h):
    return h * lax.rsqrt(jnp.mean(h * h, axis=-1, keepdims=True) + EPS)


def _silu(x):
    return x * jax.nn.sigmoid(x)


def _log_sigmoid(x):
    return jnp.minimum(x, 0.0) - jnp.log1p(jnp.exp(-jnp.abs(x)))


def _rope(x, cos_full, sin_signed):
    return x * cos_full + pltpu.roll(x, HEAD_DIM // 2, 1) * sin_signed


def _const_spec(shape):
    zeros = (0,) * len(shape)
    return pl.BlockSpec(shape, lambda *_: zeros, pipeline_mode=pl.Buffered(1))


def _params(*sem):
    return pltpu.CompilerParams(dimension_semantics=sem, vmem_limit_bytes=V7X_VMEM_LIMIT)


def _memkv_kernel(mem_ref, g_ref, wk_ref, wv_ref, kb_ref, vb_ref, krows_ref, vrows_ref):
    mn = _rms(mem_ref[...], g_ref[...]).astype(BF16)
    for w_ref, b_ref, rows_ref in ((wk_ref, kb_ref, krows_ref), (wv_ref, vb_ref, vrows_ref)):
        proj = jnp.dot(mn, w_ref[...], preferred_element_type=F32)
        b_ref[...] = proj.astype(BF16)
        for h in range(X_HEADS):
            for c in range(X_HALVES):
                lane0 = h * X_HEAD_DIM + c * V7X_LANES
                rows_ref[pl.ds(c * X_HEADS + h, N_MEM, stride=QROWS), :] = proj[:, lane0:lane0 + V7X_LANES]


def _memkv(mem, g_mem, wk, wv):
    B = mem.shape[0]
    tok = pl.BlockSpec((None, N_MEM, D_MODEL), lambda b: (b, 0, 0))
    rows = pl.BlockSpec((None, KV_ROWS, V7X_LANES), lambda b: (b, 0, 0))
    return pl.pallas_call(
        _memkv_kernel,
        grid=(B,),
        in_specs=[tok, _const_spec((1, D_MODEL)),
                  _const_spec((D_MODEL, D_MODEL)), _const_spec((D_MODEL, D_MODEL))],
        out_specs=[tok, tok, rows, rows],
        out_shape=[jax.ShapeDtypeStruct((B, N_MEM, D_MODEL), BF16)] * 2
        + [jax.ShapeDtypeStruct((B, KV_ROWS, V7X_LANES), F32)] * 2,
        compiler_params=pltpu.CompilerParams(
            dimension_semantics=("parallel",), vmem_limit_bytes=V7X_VMEM_LIMIT,
            allow_input_fusion=[False, False, True, True]),
    )(mem, g_mem, wk, wv)


def _mixer_prompt_kernel(x_ref, xs_ref, wn_ref, wt_ref, wgt_ref, wout_ref, gmix_ref, bgt_ref,
                         wconv_ref, bconv_ref, gmh_ref, grh_ref, cos_ref, sin_ref,
                         decay_ref, qdec_ref, kdec_ref, cdec_ref, wg32_ref, wu32_ref, wd32_ref,
                         x1_ref, conv_ref, C_ref, n_ref, m_ref, S_ref, wg16_ref, wu16_ref, wd16_ref,
                         un2, ut2, gt2, tail_ref, st_ref, h_ref,
                         *, tt, tiles_per_seq):
    wg16_ref[...] = wg32_ref[...].astype(BF16)
    wu16_ref[...] = wu32_ref[...].astype(BF16)
    wd16_ref[...] = wd32_ref[...].astype(BF16)
    L = CHUNK
    s = pl.program_id(0)
    t = (jnp.maximum(s, 1) - 1) % tiles_per_seq
    chunks = range(tt // L)
    slot_of = lambda i: (un2.at[i], ut2.at[i], gt2.at[i])

    @pl.when(s == 0)
    def _():
        for ref in slot_of(1) + (tail_ref,):
            ref[...] = jnp.zeros_like(ref)

    @pl.when(t == 0)
    def _():
        C_ref[...] = jnp.zeros_like(C_ref)
        st_ref[...] = jnp.zeros_like(st_ref)
        n_ref[...] = jnp.zeros_like(n_ref)
        m_ref[...] = jnp.zeros_like(m_ref)

    k_scale = HEAD_DIM ** -0.5

    def project(slot, xn, part):
        un_ref, ut_ref, gt_ref = slot
        c0 = part * MIX
        un_ref[UOFF:UOFF + tt, c0:c0 + MIX] = jnp.dot(xn, wn_ref[:, c0:c0 + MIX],
                                                      preferred_element_type=F32)
        res = lax.dot_general(wt_ref[c0:c0 + MIX, :], xn, NT_DIMS, preferred_element_type=F32)
        for c in chunks:
            ut_ref[c, c0:c0 + MIX, :] = res[:, c * L:(c + 1) * L]
        if part == 0:
            gates_t = lax.dot_general(wgt_ref[...], xn, NT_DIMS, preferred_element_type=F32)
            for c in chunks:
                gt_ref[c] = gates_t[:, c * L:(c + 1) * L]

    src_id = lax.broadcasted_iota(jnp.int32, (L, L), 0)
    tgt_id = lax.broadcasted_iota(jnp.int32, (L, L), 1)
    causal = src_id <= tgt_id
    triu_bf = jnp.where(causal, 1.0, 0.0).astype(BF16)
    heads = range(HEADS)
    hcol = lambda base, h: slice(base + h * HEAD_DIM, base + (h + 1) * HEAD_DIM)

    def chunk_body(c, slot):
        un_ref, ut_ref, gt_ref = slot
        r0 = c * L
        rows = pl.ds(UOFF + r0, L)
        trows = pl.ds(r0, L)
        g_t = gt_ref[c] + bgt_ref[...]
        ig = g_t[0:8]
        lf = _log_sigmoid(g_t[8:16])
        lf_hi = lf.astype(BF16)
        r1 = lf - lf_hi.astype(F32)
        lf_mid = r1.astype(BF16)
        lf_lo = (r1 - lf_mid.astype(F32)).astype(BF16)
        bc = (jnp.dot(lf_hi, triu_bf, preferred_element_type=F32)
              + jnp.dot(lf_mid, triu_bf, preferred_element_type=F32)
              + jnp.dot(lf_lo, triu_bf, preferred_element_type=F32))
        m_prev = m_ref[...]
        inter = bc + m_prev
        b_last = bc[:, L - 1:L]
        g_w = b_last - bc + ig
        m_new = jnp.maximum(b_last + m_prev, jnp.max(g_w, axis=1, keepdims=True))
        ws = jnp.exp(g_w - m_new)
        carry = jnp.exp(b_last + m_prev - m_new)
        a_n = jnp.concatenate([ig - bc, jnp.zeros((L - 8, L), F32)], axis=0).T

        def conv_act(col):
            win = un_ref[pl.ds(r0, L + UOFF), col:col + HEAD_DIM]
            acc = bconv_ref[:, col:col + HEAD_DIM]
            for j in range(CONV_W):
                back = CONV_W - 1 - j
                tap = win if back == 0 else pltpu.roll(win, back, 0)
                acc = acc + tap[UOFF:UOFF + L] * wconv_ref[j:j + 1, col:col + HEAD_DIM]
            return _silu(acc)

        cos_f, sin_s = cos_ref[trows, :], sin_ref[trows, :]
        q = [conv_act(N_QK + h * HEAD_DIM) for h in heads]
        kb = [(conv_act(N_QK + MIX + h * HEAD_DIM) * k_scale).astype(BF16) for h in heads]
        qb = [a.astype(BF16) for a in q]
        rqb = [_rope(un_ref[rows, hcol(N_RQ, h)], cos_f, sin_s).astype(BF16) for h in heads]
        rk = [_rope(un_ref[rows, hcol(N_RK, h)], cos_f, sin_s) * k_scale for h in heads]
        v_t = [ut_ref[c, hcol(T_V, h), :] for h in heads]
        rvb = [ut_ref[c, hcol(T_RV, h), :].astype(BF16) for h in heads]
        c_old = [C_ref[h] for h in heads]
        s_old = [st_ref[h] for h in heads]
        n_old = [n_ref[h:h + 1, :] for h in heads]
        qk = [_dot_nt(kb[h], qb[h]) for h in heads]
        att = [_dot_nt(rk[h], rqb[h]) for h in heads]
        c_q = [_dot_nt(c_old[h], qb[h]) for h in heads]
        s_q = [_dot_nt(s_old[h], rqb[h]) for h in heads]
        n_q = [_dot_nt(jnp.broadcast_to(n_old[h], (8, HEAD_DIM)), qb[h])[0:1] for h in heads]
        d_c = [_dot(v_t[h] * ws[h:h + 1, :], kb[h]) for h in heads]
        d_n = [_dot(jnp.broadcast_to(ws[h:h + 1, :], (8, L)), kb[h])[0:1] for h in heads]
        d_s = [_dot(rvb[h], rk[h] * kdec_ref[h]) for h in heads]
        m_t, w_in, wts = [], [], []
        for h in heads:
            dmat = a_n[:, h:h + 1] + bc[h:h + 1, :]
            dmat = jnp.where(causal, dmat, -jnp.inf)
            m_t.append(jnp.maximum(inter[h:h + 1, :], jnp.max(dmat, axis=0, keepdims=True)))
            wts.append(jnp.exp(dmat - m_t[h]) * qk[h])
            w_in.append(jnp.exp(inter[h:h + 1, :] - m_t[h]))
        att_w = [att[h] * decay_ref[h] for h in heads]
        v_p = [_dot(v_t[h], wts[h]) for h in heads]
        v_a = [_dot(rvb[h], att_w[h]) for h in heads]
        for h in heads:
            num = v_p[h] + w_in[h] * c_q[h]
            den = jnp.sum(wts[h], axis=0, keepdims=True) + w_in[h] * n_q[h]
            hm = num / jnp.maximum(jnp.abs(den), jnp.exp(-m_t[h]))
            hm = hm * lax.rsqrt(jnp.mean(hm * hm, axis=0, keepdims=True) + EPS)
            hm = hm * gmh_ref[hcol(0, h), :] * jax.nn.sigmoid(ut_ref[c, hcol(T_O, h), :])
            h_ref[trows, hcol(0, h)] = hm.T.astype(BF16)
            carry_h = carry[h:h + 1, :]
            C_ref[h] = carry_h * c_old[h] + d_c[h]
            n_ref[h:h + 1, :] = carry_h * n_old[h] + d_n[h]
            o = v_a[h] + qdec_ref[h:h + 1, :] * s_q[h]
            st_ref[h] = cdec_ref[h:h + 1, :] * s_old[h] + d_s[h]
            hr = o * lax.rsqrt(jnp.mean(o * o, axis=0, keepdims=True) + EPS)
            hr = hr * grh_ref[hcol(0, h), :] * _silu(ut_ref[c, hcol(T_RG, h), :])
            h_ref[trows, hcol(MIX, h)] = hr.T.astype(BF16)

        m_ref[...] = m_new

    def step(proj_slot, scan_slot):
        un_ref = scan_slot[0]
        un_ref[0:UOFF, N_QK:N_RQ] = jnp.where(t == 0, 0.0, tail_ref[...])
        xn = _rms(x_ref[...], gmix_ref[...]).astype(BF16)
        for c in chunks:
            chunk_body(c, scan_slot)
            project(proj_slot, xn, c)
        tail_ref[...] = un_ref[tt:tt + UOFF, N_QK:N_RQ]
        x1_ref[...] = xs_ref[...] + jnp.dot(h_ref[...], wout_ref[...], preferred_element_type=F32)

    step(slot_of(s % 2), slot_of(1 - s % 2))

    @pl.when(t == tiles_per_seq - 1)
    def _():
        conv_ref[...] = tail_ref[UOFF - (CONV_W - 1):UOFF, :]
        for h in heads:
            S_ref[h] = st_ref[h].T


def _mixer_prompt(x, wn, wt, wgt, w_out, g_mix, bgt, w_conv, b_conv, gmh_cols, grh_cols, cos_f, sin_s,
                  decay_t, qdec_rows, kdec_cols, cdec_rows, w_gate, w_up, w_down, tt=512):
    B, T, _ = x.shape
    tps = T // tt
    n_tiles = B * tps
    proj = lambda s: jnp.minimum(s, n_tiles - 1)
    scan = lambda s: jnp.maximum(s, 1) - 1
    per_b3 = lambda s: (scan(s) // tps, 0, 0)
    per_b4 = lambda s: (scan(s) // tps, 0, 0, 0)
    scan_tile = pl.BlockSpec((None, tt, D_MODEL), lambda s: (scan(s) // tps, scan(s) % tps, 0))
    in_specs = [
        pl.BlockSpec((None, tt, D_MODEL), lambda s: (proj(s) // tps, proj(s) % tps, 0)),
        scan_tile,
        _const_spec((D_MODEL, N_COLS)), _const_spec((T_ROWS, D_MODEL)), _const_spec((GT_ROWS, D_MODEL)),
        _const_spec((D_MODEL, D_MODEL)),
        _const_spec((1, D_MODEL)), _const_spec((GT_ROWS, CHUNK)),
        _const_spec((CONV_W, 2 * MIX)), _const_spec((1, 2 * MIX)),
        _const_spec((MIX, CHUNK)), _const_spec((MIX, CHUNK)),
        pl.BlockSpec((tt, HEAD_DIM), lambda s: (scan(s) % tps, 0)),
        pl.BlockSpec((tt, HEAD_DIM), lambda s: (scan(s) % tps, 0)),
        _const_spec((HEADS, CHUNK, CHUNK)),
        _const_spec((8, CHUNK)), _const_spec((HEADS, CHUNK, HEAD_DIM)), _const_spec((8, HEAD_DIM)),
    ]
    ff_blocks = D_FF // HEAD_DIM
    assert ff_blocks <= n_tiles + 1
    early = lambda s: jnp.minimum(s, ff_blocks - 1)
    late = lambda s: jnp.maximum(s - (n_tiles + 1 - ff_blocks), 0)
    cast_specs = [pl.BlockSpec((D_MODEL, HEAD_DIM), lambda s: (0, early(s))),
                  pl.BlockSpec((D_MODEL, HEAD_DIM), lambda s: (0, late(s))),
                  pl.BlockSpec((HEAD_DIM, D_MODEL), lambda s: (late(s), 0))]
    in_specs += cast_specs
    out_specs = [
        scan_tile,
        pl.BlockSpec((None, CONV_W - 1, 2 * MIX), per_b3),
        pl.BlockSpec((None, HEADS, HEAD_DIM, HEAD_DIM), per_b4),
        pl.BlockSpec((None, HEADS, HEAD_DIM), per_b3),
        pl.BlockSpec((None, 8, CHUNK), per_b3),
        pl.BlockSpec((None, HEADS, HEAD_DIM, HEAD_DIM), per_b4),
    ] + cast_specs
    out_shape = [
        jax.ShapeDtypeStruct((B, T, D_MODEL), F32),
        jax.ShapeDtypeStruct((B, CONV_W - 1, 2 * MIX), F32),
        jax.ShapeDtypeStruct((B, HEADS, HEAD_DIM, HEAD_DIM), F32),
        jax.ShapeDtypeStruct((B, HEADS, HEAD_DIM), F32),
        jax.ShapeDtypeStruct((B, 8, CHUNK), F32),
        jax.ShapeDtypeStruct((B, HEADS, HEAD_DIM, HEAD_DIM), F32),
        jax.ShapeDtypeStruct((D_MODEL, D_FF), BF16), jax.ShapeDtypeStruct((D_MODEL, D_FF), BF16),
        jax.ShapeDtypeStruct((D_FF, D_MODEL), BF16),
    ]
    n_chunks = tt // CHUNK
    scratch = [
        pltpu.VMEM((2, tt + UOFF, N_COLS), F32),
        pltpu.VMEM((2, n_chunks, T_ROWS, CHUNK), F32),
        pltpu.VMEM((2, n_chunks, GT_ROWS, CHUNK), F32),
    ] + [
        pltpu.VMEM((UOFF, 2 * MIX), F32),
        pltpu.VMEM((HEADS, HEAD_DIM, HEAD_DIM), F32),
        pltpu.VMEM((tt, D_MODEL), BF16),
    ]
    return pl.pallas_call(
        functools.partial(_mixer_prompt_kernel, tt=tt, tiles_per_seq=tps),
        grid=(n_tiles + 1,),
        in_specs=in_specs, out_specs=out_specs, out_shape=out_shape,
        scratch_shapes=scratch,
        compiler_params=_params("arbitrary"),
    )(x, x, wn, wt, wgt, w_out, g_mix, bgt, w_conv, b_conv, gmh_cols, grh_cols, cos_f, sin_s,
      decay_t, qdec_rows, kdec_cols, cdec_rows, w_gate, w_up, w_down)


def _attn_prompt_kernel(x1_ref, k_ref, v_ref, wcq_ref, gx_ref, *rest, n_sample_steps):
    sample_in, o_ref = rest[:SAMPLE_MIXER_INPUTS], rest[SAMPLE_MIXER_INPUTS]
    sample_out_and_scratch = rest[SAMPLE_MIXER_INPUTS + 1:]
    step = pl.program_id(0) * pl.num_programs(1) + pl.program_id(1)

    @pl.when(step < n_sample_steps)
    def _():
        _mixer_sample_kernel(*sample_in, *sample_out_and_scratch, bb=SAMPLE_BLOCK)

    tm = x1_ref.shape[0]
    rows = [slice(r, r + tm // ROW_GROUPS) for r in range(0, tm, tm // ROW_GROUPS)]
    xq = [_rms(x1_ref[r, :], gx_ref[...]).astype(BF16) for r in rows]
    q = [jnp.dot(a, wcq_ref[...], preferred_element_type=F32).astype(BF16) for a in xq]
    sl = [slice(h * X_HEAD_DIM, (h + 1) * X_HEAD_DIM) for h in range(X_HEADS)]
    items = [(g, h) for g in range(ROW_GROUPS) for h in range(X_HEADS)]
    s = [_dot_nt(q[g][:, sl[h]], k_ref[:, sl[h]]) * (X_HEAD_DIM ** -0.5) for g, h in items]
    e = [jnp.exp(a - jnp.max(a, axis=-1, keepdims=True)) for a in s]
    p = [a / jnp.sum(a, axis=-1, keepdims=True) for a in e]
    for (g, h), a in zip(items, p):
        o_ref[rows[g], sl[h]] = _dot(a, v_ref[:, sl[h]]).astype(BF16)


def _attn_prompt(x1, mk, mv, w_cq, g_x, sample_mixer_operands, tm=1024):
    B, T, _ = x1.shape
    tps = T // tm
    Bs = sample_mixer_operands[0].shape[0]
    n_sample_steps = Bs // SAMPLE_BLOCK
    assert n_sample_steps <= B * tps
    block_of = lambda b, t: jnp.minimum(b * tps + t, n_sample_steps - 1)
    s_in, s_out, s_shape, s_scratch = _sample_mixer_specs(Bs, block_of)
    tile = pl.BlockSpec((None, tm, D_MODEL), lambda b, t: (b, t, 0))
    kv = pl.BlockSpec((None, N_MEM, D_MODEL), lambda b, t: (b, 0, 0))
    outs = pl.pallas_call(
        functools.partial(_attn_prompt_kernel, n_sample_steps=n_sample_steps),
        grid=(B, tps),
        in_specs=[tile, kv, kv, _const_spec((D_MODEL, D_MODEL)), _const_spec((1, D_MODEL))] + s_in,
        out_specs=[tile] + s_out,
        out_shape=[jax.ShapeDtypeStruct((B, T, D_MODEL), BF16)] + s_shape,
        scratch_shapes=s_scratch,
        compiler_params=_params("arbitrary", "arbitrary"),
    )(x1, mk, mv, w_cq, g_x, *sample_mixer_operands)
    return outs[0], outs[1:]


V7X_MXU_DIM = 256
FF_CHUNKS = ((0, 6 * V7X_MXU_DIM), (6 * V7X_MXU_DIM, D_FF))


def _ffn_kernel(x1_ref, o_ref, wco_ref, wg_ref, wu_ref, wd_ref, gffn_ref, gfin_ref, *rest, n_groups):
    if len(rest) == 1:
        (y_ref,) = rest
        sample_stages = iter(())
    else:
        qs_ref, ck_ref, cv_ref, y_ref, os_ref = rest
        sample_stages = _sample_attention(qs_ref, ck_ref, cv_ref, os_ref)
    tm = x1_ref.shape[0]
    rows = [slice(r, r + tm // n_groups) for r in range(0, tm, tm // n_groups)]
    acc = [x1_ref[r, :] + _dot(o_ref[r, :], wco_ref[...]) for r in rows]
    hf = [_rms(a, gffn_ref[...]).astype(BF16) for a in acc]
    for c0, c1 in FF_CHUNKS:
        next(sample_stages, None)
        gate = [jnp.dot(h, wg_ref[:, c0:c1], preferred_element_type=F32) for h in hf]
        up = [jnp.dot(h, wu_ref[:, c0:c1], preferred_element_type=F32) for h in hf]
        act = [_silu(g) * u for g, u in zip(gate, up)]
        acc = [a + _dot(p, wd_ref[c0:c1, :]) for a, p in zip(acc, act)]
    next(sample_stages, None)
    for r, a in zip(rows, acc):
        y_ref[r, :] = _rms(a, gfin_ref[...])


def _ffn(x1, o, w_co, w_gate, w_up, w_down, g_ffn, g_final, tm, sample=None):
    rows = x1.shape[0]
    steps = rows // tm
    row_spec = pl.BlockSpec((tm, D_MODEL), lambda i: (i, 0))
    in_specs = [row_spec, row_spec, _const_spec((D_MODEL, D_MODEL)),
                _const_spec((D_MODEL, D_FF)), _const_spec((D_MODEL, D_FF)),
                _const_spec((D_FF, D_MODEL)), _const_spec((1, D_MODEL)), _const_spec((1, D_MODEL))]
    out_specs = [row_spec]
    out_shape = [jax.ShapeDtypeStruct((rows, D_MODEL), F32)]
    operands = [x1, o, w_co, w_gate, w_up, w_down, g_ffn, g_final]
    if sample is not None:
        q_s, ck_s, cv_s = sample
        Bs = q_s.shape[0]
        ba = Bs // steps
        assert ba * steps == Bs
        s_rows = pl.BlockSpec((ba, QROWS, V7X_LANES), lambda i: (i, 0, 0))
        s_kv = pl.BlockSpec((ba, KV_ROWS, V7X_LANES), lambda i: (i, 0, 0))
        in_specs += [s_rows, s_kv, s_kv]
        out_specs += [s_rows]
        out_shape += [jax.ShapeDtypeStruct((Bs, QROWS, V7X_LANES), F32)]
        operands += [q_s.reshape(Bs, QROWS, V7X_LANES), ck_s, cv_s]
    outs = pl.pallas_call(
        functools.partial(_ffn_kernel, n_groups=ROW_GROUPS if tm >= 512 else 1),
        grid=(steps,),
        in_specs=in_specs, out_specs=out_specs, out_shape=out_shape,
        compiler_params=_params("parallel"),
    )(*operands)
    if sample is None:
        return outs[0]
    return outs[0], outs[1].reshape(Bs, D_MODEL)


def _inproj_sample_kernel(x_ref, g_ref, wn_ref, wt_ref, wgt_ref, un_ref, ut_ref, gs_ref):
    xn = _rms(x_ref[...], g_ref[...]).astype(BF16)
    un_ref[...] = jnp.dot(xn, wn_ref[...], preferred_element_type=F32)
    ut_ref[...] = lax.dot_general(xn, wt_ref[...], NT_DIMS, preferred_element_type=F32)
    gs_ref[...] = lax.dot_general(xn, wgt_ref[...], NT_DIMS, preferred_element_type=F32)


def _inproj_sample(x, g_mix, wn, wt, wgt):
    rows = x.shape[0]
    full = lambda cols: pl.BlockSpec((rows, cols), lambda i: (0, 0))
    return pl.pallas_call(
        _inproj_sample_kernel,
        grid=(1,),
        in_specs=[_const_spec((rows, D_MODEL)), _const_spec((1, D_MODEL)),
                  _const_spec((D_MODEL, N_COLS)), _const_spec((T_ROWS, D_MODEL)),
                  _const_spec((GT_ROWS, D_MODEL))],
        out_specs=[full(N_COLS), full(T_ROWS), full(GT_ROWS)],
        out_shape=[jax.ShapeDtypeStruct((rows, N_COLS), F32),
                   jax.ShapeDtypeStruct((rows, T_ROWS), F32),
                   jax.ShapeDtypeStruct((rows, GT_ROWS), F32)],
        compiler_params=_params("arbitrary"),
    )(x, g_mix, wn, wt, wgt)


def _mixer_sample_kernel(un_ref, ut_ref, gs_ref, conv_ref, C_ref, n_ref, m_ref, S_ref, bg_ref,
                         wconv_ref, bconv_ref, gmh_ref, grh_ref, cos_ref, sin_ref, rtab_ref,
                         h_ref, convo_ref, Co_ref, no_ref, mo_ref, So_ref,
                         q_s, k_s, vws_s, qr_s, kr_s, qc_s, qs_s, carry_s, *, bb):
    k_scale = HEAD_DIM ** -0.5
    uqk = un_ref[:, N_QK:N_RQ]
    conv = (bconv_ref[...] + wconv_ref[0:1, :] * conv_ref[0] + wconv_ref[1:2, :] * conv_ref[1]
            + wconv_ref[2:3, :] * conv_ref[2] + wconv_ref[3:4, :] * uqk)
    convo_ref[0] = conv_ref[1]
    convo_ref[1] = conv_ref[2]
    convo_ref[2] = uqk
    qk_act = _silu(conv)
    q_s[...] = qk_act[:, 0:MIX]
    k_s[...] = qk_act[:, MIX:2 * MIX] * k_scale

    gates = gs_ref[...] + bg_ref[...]
    ig = gates[:, 0:HEADS]
    lf = _log_sigmoid(gates[:, 8:8 + HEADS])
    inter = lf + m_ref[...]
    m_t = jnp.maximum(inter, ig)
    ws = jnp.exp(ig - m_t)
    w_in = jnp.exp(inter - m_t)
    mo_ref[...] = m_t
    carry_s[:, 0:HEADS] = w_in

    cos_f = cos_ref[...]
    sin_s = sin_ref[...]
    for h in range(HEADS):
        lo = h * HEAD_DIM
        hs = slice(lo, lo + HEAD_DIM)
        vws_s[:, hs] = ut_ref[:, T_V + lo:T_V + lo + HEAD_DIM] * ws[:, h:h + 1]
        qr_s[:, hs] = _rope(un_ref[:, N_RQ + lo:N_RQ + lo + HEAD_DIM], cos_f, sin_s)
        kr_s[:, hs] = _rope(un_ref[:, N_RK + lo:N_RK + lo + HEAD_DIM], cos_f, sin_s) * k_scale

    heads = range(HEADS)
    seqs = range(bb)
    hsl = [slice(h * HEAD_DIM, (h + 1) * HEAD_DIM) for h in heads]
    seq_id = lax.broadcasted_iota(jnp.int32, (bb, HEAD_DIM), 0)
    for h in heads:
        q_h, k_h, vws_h = q_s[:, hsl[h]], k_s[:, hsl[h]], vws_s[:, hsl[h]]
        qr_h = qr_s[:, hsl[h]]
        krd_h = kr_s[:, hsl[h]] * rtab_ref[1:2, h:h + 1]
        rv_h = ut_ref[:, T_RV + h * HEAD_DIM:T_RV + (h + 1) * HEAD_DIM]
        c_old = [C_ref[b, h] for b in seqs]
        s_old = [S_ref[b, h] for b in seqs]
        q_c = [_dot_nt(q_h, c_old[b]) for b in seqs]
        q_st = [_dot(qr_h, s_old[b]) for b in seqs]
        d_c = [_dot_tn(jnp.where(seq_id == b, vws_h, 0.0), k_h) for b in seqs]
        d_s = [_dot_tn(jnp.where(seq_id == b, krd_h, 0.0), rv_h) for b in seqs]
        for b in seqs:
            row = slice(b, b + 1)
            qc_s[row, hsl[h]] = q_c[b][row]
            qs_s[row, hsl[h]] = q_st[b][row]
            Co_ref[b, h] = carry_s[row, h:h + 1] * c_old[b] + d_c[b]
            So_ref[b, h] = rtab_ref[2:3, h:h + 1] * s_old[b] + d_s[b]

    for h in range(HEADS):
        lo = h * HEAD_DIM
        hs = slice(lo, lo + HEAD_DIM)
        q_h, k_h, n_h = q_s[:, hs], k_s[:, hs], n_ref[:, hs]
        ws_h, w_in_h = ws[:, h:h + 1], w_in[:, h:h + 1]
        wts = ws_h * jnp.sum(q_h * k_h, axis=1, keepdims=True)
        num = wts * ut_ref[:, T_V + lo:T_V + lo + HEAD_DIM] + w_in_h * qc_s[:, hs]
        den = wts + w_in_h * jnp.sum(n_h * q_h, axis=1, keepdims=True)
        hm = num / jnp.maximum(jnp.abs(den), jnp.exp(-m_t[:, h:h + 1]))
        mo = ut_ref[:, T_O + lo:T_O + lo + HEAD_DIM]
        h_ref[:, hs] = _head_norm(hm) * gmh_ref[:, hs] * jax.nn.sigmoid(mo)
        no_ref[:, hs] = w_in_h * n_h + ws_h * k_h

        att = jnp.sum(qr_s[:, hs] * kr_s[:, hs], axis=1, keepdims=True) * rtab_ref[3:4, h:h + 1]
        o = (att * ut_ref[:, T_RV + lo:T_RV + lo + HEAD_DIM]
             + rtab_ref[0:1, h:h + 1] * qs_s[:, hs])
        rg = ut_ref[:, T_RG + lo:T_RG + lo + HEAD_DIM]
        h_ref[:, MIX + lo:MIX + lo + HEAD_DIM] = _head_norm(o) * grh_ref[:, hs] * _silu(rg)


SAMPLE_BLOCK = 8
SAMPLE_MIXER_INPUTS = 16


def _sample_mixer_specs(B, block_of):
    bb = SAMPLE_BLOCK
    rows = lambda cols: pl.BlockSpec((bb, cols), lambda *g: (block_of(*g), 0))
    mats = pl.BlockSpec((bb, HEADS, HEAD_DIM, HEAD_DIM), lambda *g: (block_of(*g), 0, 0, 0))
    convs = pl.BlockSpec((CONV_W - 1, bb, 2 * MIX), lambda *g: (0, block_of(*g), 0))
    in_specs = [rows(N_COLS), rows(T_ROWS), rows(GT_ROWS), convs, mats, rows(MIX), rows(HEADS), mats,
                _const_spec((1, GT_ROWS)), _const_spec((CONV_W, 2 * MIX)), _const_spec((1, 2 * MIX)),
                _const_spec((1, MIX)), _const_spec((1, MIX)),
                _const_spec((1, HEAD_DIM)), _const_spec((1, HEAD_DIM)), _const_spec((8, HEAD_DIM))]
    assert len(in_specs) == SAMPLE_MIXER_INPUTS
    out_specs = [rows(D_MODEL), convs, mats, rows(MIX), rows(HEADS), mats]
    out_shape = [jax.ShapeDtypeStruct((B, D_MODEL), F32),
                 jax.ShapeDtypeStruct((CONV_W - 1, B, 2 * MIX), F32),
                 jax.ShapeDtypeStruct((B, HEADS, HEAD_DIM, HEAD_DIM), F32),
                 jax.ShapeDtypeStruct((B, MIX), F32),
                 jax.ShapeDtypeStruct((B, HEADS), F32),
                 jax.ShapeDtypeStruct((B, HEADS, HEAD_DIM, HEAD_DIM), F32)]
    scratch = [pltpu.VMEM((bb, MIX), F32) for _ in range(7)] + [pltpu.VMEM((bb, HEAD_DIM), F32)]
    return in_specs, out_specs, out_shape, scratch


def _outq_sample_kernel(x_ref, h_ref, wout_ref, wcq_ref, gx_ref, x1_ref, q_ref):
    x1 = x_ref[...] + _dot(h_ref[...], wout_ref[...])
    x1_ref[...] = x1
    q_ref[...] = _dot(_rms(x1, gx_ref[...]), wcq_ref[...])


def _outq_sample(x, hcat, w_out, w_cq, g_x):
    rows = x.shape[0]
    full = _const_spec((rows, D_MODEL))
    return pl.pallas_call(
        _outq_sample_kernel,
        grid=(1,),
        in_specs=[full, full, _const_spec((D_MODEL, D_MODEL)), _const_spec((D_MODEL, D_MODEL)),
                  _const_spec((1, D_MODEL))],
        out_specs=[pl.BlockSpec((rows, D_MODEL), lambda i: (0, 0))] * 2,
        out_shape=[jax.ShapeDtypeStruct((rows, D_MODEL), F32)] * 2,
        compiler_params=_params("arbitrary"),
    )(x, hcat, w_out, w_cq, g_x)


def _sample_attention(q_ref, k_ref, v_ref, o_ref):
    assert X_HALVES == 2
    r_id = lax.broadcasted_iota(jnp.int32, (QROWS, KV_ROWS), 0)
    n_id = lax.broadcasted_iota(jnp.int32, (QROWS, KV_ROWS), 1)
    own = (n_id % QROWS) == (r_id // X_HALVES) + X_HEADS * (r_id % X_HALVES)
    low_half = (lax.broadcasted_iota(jnp.int32, (1, KV_ROWS), 1) % QROWS) < X_HEADS
    seqs = range(q_ref.shape[0])
    z = [_dot_nt(q_ref[j], k_ref[j]) for j in seqs]
    zc = [jnp.sum(jnp.where(own, z[j], 0.0), axis=0, keepdims=True) for j in seqs]
    yield
    other = [jnp.where(low_half, pltpu.roll(zc[j], KV_ROWS - X_HEADS, 1),
                       pltpu.roll(zc[j], X_HEADS, 1)) for j in seqs]
    s = [jnp.where(own, (zc[j] + other[j]) * (X_HEAD_DIM ** -0.5), -jnp.inf) for j in seqs]
    m = [jnp.max(s[j], axis=-1, keepdims=True) for j in seqs]
    e = [jnp.exp(s[j] - m[j]) for j in seqs]
    den = [jnp.sum(e[j], axis=-1, keepdims=True) for j in seqs]
    yield
    for j in seqs:
        o_ref[j] = _dot(e[j] / den[j], v_ref[j])
    yield


def _kv_rows(cache):
    B = cache.shape[0]
    c5 = cache.reshape(B, N_MEM, X_HEADS, X_HALVES, V7X_LANES)
    return jnp.transpose(c5, (0, 1, 3, 2, 4)).reshape(B, KV_ROWS, V7X_LANES)


def _kv_from_rows(rows):
    B = rows.shape[0]
    r5 = rows.reshape(B, N_MEM, X_HALVES, X_HEADS, V7X_LANES)
    return jnp.transpose(r5, (0, 1, 3, 2, 4)).reshape(1, B, N_MEM, X_HEADS, X_HEAD_DIM)


def _rope_tables(pos):
    half = HEAD_DIM // 2
    inv = ROPE_THETA ** (-np.arange(half, dtype=np.float64) / half)
    ang = np.asarray(pos, np.float64)[:, None] * inv[None, :]
    cos, sin = np.cos(ang), np.sin(ang)
    return (np.concatenate([cos, cos], axis=-1).astype(np.float32),
            np.concatenate([-sin, sin], axis=-1).astype(np.float32))


def _retention_tables(L):
    lg = np.log1p(-np.exp2(-5.0 - np.arange(HEADS, dtype=np.float64)))
    t = np.arange(L, dtype=np.float64)
    diff = t[:, None] - t[None, :]
    decay = np.where(diff >= 0, np.exp(lg[:, None, None] * np.maximum(diff, 0.0)), 0.0)
    q_dec = np.exp(lg[:, None] * (t + 1.0))
    k_dec = np.exp(lg[:, None] * (L - 1.0 - t))
    chunk_dec = np.exp(lg * L)
    return tuple(a.astype(np.float32) for a in (decay, q_dec, k_dec, chunk_dec))


def _lanes(a, n):
    xp = np if isinstance(a, np.ndarray) else jnp
    return xp.broadcast_to(a[..., None], a.shape + (n,))


def _pad_rows(a, rows):
    return np.pad(a, ((0, rows - a.shape[0]),) + ((0, 0),) * (a.ndim - 1))


def kernel(x_prompt, x_sample, cache_mem_k, cache_mem_v, state_mlstm_conv, state_mlstm_C, state_mlstm_n, state_mlstm_m, state_ret_S, mem_prompt, w_in, b_gate, w_conv, b_conv, g_mix, g_mhead, g_rhead, w_out, g_xattn, g_mem, w_ck, w_cv, w_cq, w_co, g_ffn, w_gate, w_up, w_down, g_final):
    Bp, Tp, _ = x_prompt.shape
    Bs = x_sample.shape[0]
    l = 0
    n_m = 4 * MIX
    wi = w_in[l]
    w_gates = wi[:, n_m:n_m + 2 * HEADS]
    w_ret = wi[:, n_m + 2 * HEADS:]
    row = lambda a: a.reshape(1, -1)
    bf = lambda a: a.astype(BF16)
    g_mix_r, g_mh_r, g_rh_r = row(g_mix[l]), row(g_mhead[l]), row(g_rhead[l])
    g_x_r, g_mem_r, g_ffn_r, g_fin_r = row(g_xattn[l]), row(g_mem[l]), row(g_ffn[l]), row(g_final)
    b_conv_r = row(b_conv[l])
    w_out_b, w_cq_b, w_co_b = bf(w_out[l]), bf(w_cq[l]), bf(w_co[l])

    mk_b, mv_b, mk_rows, mv_rows = _memkv(mem_prompt, g_mem_r, bf(w_ck[l]), bf(w_cv[l]))
    wn = bf(jnp.concatenate([wi[:, :2 * MIX], w_ret[:, :2 * MIX]], axis=1))
    wt = bf(jnp.concatenate([wi[:, 2 * MIX:n_m], w_ret[:, 2 * MIX:]], axis=1).T)
    gate_rows = lambda a: jnp.concatenate(
        [a[:HEADS], jnp.zeros((8 - HEADS,) + a.shape[1:], F32),
         a[HEADS:], jnp.zeros((8 - HEADS,) + a.shape[1:], F32)], axis=0)
    wgt = bf(gate_rows(w_gates.T))
    bg_rows = gate_rows(b_gate[l][:, None])
    bgt = jnp.broadcast_to(bg_rows, (GT_ROWS, CHUNK))
    cos_p, sin_p = _rope_tables(np.arange(Tp))
    decay, q_dec, k_dec, chunk_dec = _retention_tables(CHUNK)
    x1_p, conv_p, C_p, n_p, m_p, S_p, w_gate_b, w_up_b, w_down_b = _mixer_prompt(
        x_prompt, wn, wt, wgt, w_out_b, g_mix_r, bgt, w_conv[l], b_conv_r,
        _lanes(g_mhead[l], CHUNK), _lanes(g_rhead[l], CHUNK), cos_p, sin_p,
        np.swapaxes(decay, 1, 2), _pad_rows(q_dec, 8), _lanes(k_dec, HEAD_DIM),
        _pad_rows(_lanes(chunk_dec, HEAD_DIM), 8), w_gate[l], w_up[l], w_down[l])

    xs = x_sample.reshape(Bs, D_MODEL)
    un_s, ut_s, gs_s = _inproj_sample(xs, g_mix_r, wn, wt, wgt)
    cos_s, sin_s = _rope_tables(PAST_LEN + np.arange(1))
    decay1, q_dec1, k_dec1, chunk_dec1 = _retention_tables(1)
    rtab = np.zeros((8, HEAD_DIM), np.float32)
    rtab[:4, :HEADS] = np.stack([q_dec1[:, 0], k_dec1[:, 0], chunk_dec1, decay1[:, 0, 0]])
    sample_mixer_operands = (
        un_s, ut_s, gs_s, jnp.transpose(state_mlstm_conv[l], (1, 0, 2)), state_mlstm_C[l],
        state_mlstm_n[l].reshape(Bs, MIX), state_mlstm_m[l], state_ret_S[l],
        bg_rows.reshape(1, GT_ROWS), w_conv[l], b_conv_r, g_mh_r, g_rh_r, cos_s, sin_s, rtab)

    o_p, (hcat_s, conv_s, C_s, n_s, m_s, S_s) = _attn_prompt(
        x1_p, mk_b, mv_b, w_cq_b, g_x_r, sample_mixer_operands)
    x1_s, q_s = _outq_sample(xs, hcat_s, w_out_b, w_cq_b, g_x_r)
    y_p, o_s = _ffn(x1_p.reshape(Bp * Tp, D_MODEL), o_p.reshape(Bp * Tp, D_MODEL),
                    w_co_b, w_gate_b, w_up_b, w_down_b, g_ffn_r, g_fin_r, tm=512,
                    sample=(q_s, _kv_rows(cache_mem_k[l]), _kv_rows(cache_mem_v[l])))
    y_s = _ffn(x1_s, o_s, w_co_b, w_gate_b, w_up_b, w_down_b, g_ffn_r, g_fin_r, tm=Bs)

    return (y_p.reshape(Bp, Tp, D_MODEL), y_s.reshape(Bs, 1, D_MODEL),
            _kv_from_rows(mk_rows), _kv_from_rows(mv_rows),
            conv_p[None], C_p[None], n_p[None], m_p[None, :, :HEADS, 0], S_p[None],
            jnp.transpose(conv_s, (1, 0, 2))[None], C_s[None],
            n_s.reshape(1, Bs, HEADS, HEAD_DIM), m_s[None], S_s[None])
```
